```python
import math
import jax
import jax.numpy as jnp
from jax import lax
import numpy as np

D_MODEL = 1024
BATCH = 16
SEQ = 256
DEPTH = 2
DEC_BATCH = 4
DEC_SEQ = 1024
PAST_LEN = 256

GRID_W = 64
MIX_WIDTH = D_MODEL
GM_HEADS = 4
GM_HEAD_DIM = 64
GM_WIDTH = GM_HEADS * GM_HEAD_DIM
CHUNK = 128
HY_WIDTH = MIX_WIDTH // 4
HY_ORDER = 2
HY_SHORT = 3
HY_EMB_DIM = 33
HY_FILTER_HIDDEN = 64
HY_DECAY_TARGET = 1e-2
HY_DECAY_PCT_SHORT = 0.3
HY_DECAY_PCT_LONG = 1.5
MLA_NOPE = 64
MLA_ROPE = 32
MLA_V = 64
MLA_WIDTH = MIX_WIDTH - GM_WIDTH - HY_WIDTH
MLA_HEADS = MLA_WIDTH // MLA_V
MLA_Q_LORA = D_MODEL // 4
MLA_KV_LORA = D_MODEL // 8
ROPE_THETA = 10000.0
Q_BLOCK = 128
SPLIT_SIZES = (GM_WIDTH, GM_WIDTH, 3 * HY_WIDTH, MLA_Q_LORA, MLA_KV_LORA, MLA_ROPE)
SPLIT_POINTS = tuple(int(s) for s in np.cumsum(SPLIT_SIZES)[:-1])
IN_COLS = sum(SPLIT_SIZES)
MOE_GROUPS = 4
MOE_EXPERTS_PER_GROUP = 8
MOE_N_EXPERTS = MOE_GROUPS * MOE_EXPERTS_PER_GROUP
MOE_TOP_K = 2
MOE_D_EXPERT = 256
ALPHA = (2.0 * DEPTH) ** 0.25
BETA = (8.0 * DEPTH) ** -0.25
LN_EPS = 1e-5
RMS_EPS = 1e-6
F32 = jnp.float32

kernel_name = 'hybrid_gmlp_hyena_mla_hmoe_diffusion_step'


def layer_norm(x, g, b):
    xf = x.astype(F32)
    xc = xf - jnp.mean(xf, -1, keepdims=True)
    var = jnp.mean(xc * xc, -1, keepdims=True)
    return (xc * lax.rsqrt(var + LN_EPS) * g.astype(F32) + b.astype(F32)).astype(x.dtype)


def rms_norm(x, g):
    xf = x.astype(F32)
    return (xf * lax.rsqrt(jnp.mean(xf * xf, -1, keepdims=True) + RMS_EPS) * g.astype(F32)).astype(x.dtype)


def axial_rope(L):
    rows = L // GRID_W
    row = jnp.repeat(jnp.arange(rows, dtype=F32), GRID_W)
    col = jnp.tile(jnp.arange(GRID_W, dtype=F32), rows)
    n = MLA_ROPE // 4
    inv = ROPE_THETA ** (-jnp.arange(n, dtype=F32) / n)
    ang = jnp.concatenate([row[:, None] * inv, col[:, None] * inv], -1)
    return jnp.cos(ang), jnp.sin(ang)


def apply_rope(x, cos, sin):
    x1, x2 = x[..., 0::2], x[..., 1::2]
    return jnp.stack([x1 * cos - x2 * sin, x1 * sin + x2 * cos], -1).reshape(x.shape).astype(x.dtype)


def gmlp_chunk_mixer(u, v, p):
    B, L, _ = u.shape
    n = L // CHUNK
    u = jax.nn.gelu(u)
    vh = jax.nn.gelu(v).reshape(B, L, GM_HEADS, GM_HEAD_DIM)
    vh = layer_norm(vh, p['gm_ln_g'].reshape(GM_HEADS, GM_HEAD_DIM), p['gm_ln_b'].reshape(GM_HEADS, GM_HEAD_DIM))
    vh = vh.reshape(B, n, CHUNK, GM_HEADS, GM_HEAD_DIM)
    s = jnp.einsum('hqk,bnkhd->bnqhd', p['gm_ws'], vh) + p['gm_bs'].T[:, :, None]
    return u * s.reshape(B, L, GM_WIDTH)


def hyena_filter_spectrum(L, p):
    t = jnp.linspace(0.0, 1.0, L, dtype=F32)[:, None]
    bands = (HY_EMB_DIM - 1) // 2
    w = 2.0 * math.pi * jnp.arange(L, dtype=F32)[:, None] / L
    f = jnp.linspace(1e-4, bands - 1, bands, dtype=F32)[None]
    z = jnp.concatenate([t, jnp.cos(f * w), -jnp.sin(f * w)], -1)
    freq = p['hy_f_freq'].astype(F32)
    hid = jnp.sin(freq[0] * (z @ p['hy_f_w1'].astype(F32) + p['hy_f_b1'].astype(F32)))
    hid = jnp.sin(freq[1] * (hid @ p['hy_f_w2'].astype(F32) + p['hy_f_b2'].astype(F32)))
    filt = (hid @ p['hy_f_w3'].astype(F32)).reshape(L, HY_ORDER, 2, HY_WIDTH)
    min_decay = math.log(HY_DECAY_TARGET) / HY_DECAY_PCT_LONG
    max_decay = math.log(HY_DECAY_TARGET) / HY_DECAY_PCT_SHORT
    deltas = jnp.abs(jnp.linspace(min_decay, max_decay, HY_WIDTH, dtype=F32))
    filt = filt * jnp.exp(-t * deltas)[:, None, None, :]
    fwd, bwd = filt[:, :, 0], filt[:, :, 1]
    two_sided = jnp.concatenate([fwd, jnp.zeros_like(fwd[:1]), bwd[:0:-1]], axis=0)
    return jnp.fft.rfft(two_sided, axis=0)


def long_conv(u, k_spec, bias):
    L = u.shape[1]
    u_f = jnp.fft.rfft(u.astype(F32), n=2 * L, axis=1)
    y = jnp.fft.irfft(u_f * k_spec[None], n=2 * L, axis=1)[:, :L]
    return (y + u.astype(F32) * bias.astype(F32)).astype(u.dtype)


def hyena_mixer(z, p):
    B, L, _ = z.shape
    pad = HY_SHORT // 2
    zp = jnp.pad(z, ((0, 0), (pad, pad), (0, 0)))
    z = sum(zp[:, j:j + L] * p['hy_conv_w'][j] for j in range(HY_SHORT)) + p['hy_conv_b']
    x1, x2, v = jnp.split(z, 3, axis=-1)
    k_spec = hyena_filter_spectrum(L, p)
    y = v
    for o, gate in enumerate((x1, x2)):
        y = gate * long_conv(y, k_spec[:, o], p['hy_bias'][o])
    return y


def mla_attend(q_nope, q_pe, k_nope, k_pe, v):
    B, Lq, H, _ = q_nope.shape
    nb = Lq // Q_BLOCK
    qn = q_nope.reshape(B, nb, Q_BLOCK, H, MLA_NOPE).swapaxes(0, 1)
    qp = q_pe.reshape(B, nb, Q_BLOCK, H, MLA_ROPE).swapaxes(0, 1)
    scale = 1.0 / math.sqrt(MLA_NOPE + MLA_ROPE)

    def one_block(qs):
        qn_b, qp_b = qs
        s = jnp.einsum('bqhd,bkhd->bhqk', qn_b, k_nope) + jnp.einsum('bqhr,bkr->bhqk', qp_b, k_pe)
        prob = jax.nn.softmax(s.astype(F32) * scale, axis=-1).astype(v.dtype)
        return jnp.einsum('bhqk,bkhd->bqhd', prob, v)

    o = lax.map(one_block, (qn, qp))
    return o.swapaxes(0, 1).reshape(B, Lq, H * MLA_V)


def mla_mixer(c_q, c_kv, k_pe, rope, ctx_cache, p):
    B, L, _ = c_q.shape
    q = (rms_norm(c_q, p['mla_gq']) @ p['mla_wuq']).reshape(B, L, MLA_HEADS, MLA_NOPE + MLA_ROPE)
    q_nope, q_pe = q[..., :MLA_NOPE], q[..., MLA_NOPE:]
    ckv = rms_norm(c_kv, p['mla_gkv'])
    if rope is not None:
        cos, sin = rope
        q_pe = apply_rope(q_pe, cos[:, None], sin[:, None])
        k_pe = apply_rope(k_pe, cos, sin)
    keys_ckv, keys_kpe = ckv, k_pe
    if ctx_cache is not None:
        keys_ckv = jnp.concatenate([ctx_cache[0].astype(ckv.dtype), ckv], axis=1)
        keys_kpe = jnp.concatenate([ctx_cache[1].astype(k_pe.dtype), k_pe], axis=1)
    Lk = keys_ckv.shape[1]
    kv = (keys_ckv @ p['mla_wukv']).reshape(B, Lk, MLA_HEADS, MLA_NOPE + MLA_V)
    out = mla_attend(q_nope, q_pe, kv[..., :MLA_NOPE], keys_kpe, kv[..., MLA_NOPE:])
    return out, ckv, k_pe


def token_mixers(h, rope, ctx_cache, p):
    proj = h @ p['w_in']
    u, v, hy_in, c_q, c_kv, k_pe = jnp.split(proj, SPLIT_POINTS, axis=-1)
    a = gmlp_chunk_mixer(u, v, p)
    b = hyena_mixer(hy_in, p)
    m, ckv, kpe = mla_mixer(c_q, c_kv, k_pe, rope, ctx_cache, p)
    out = jnp.concatenate([a, b.astype(a.dtype), m.astype(a.dtype)], axis=-1) @ p['w_out']
    return out, ckv, kpe


def hier_moe(h, p):
    B, L, D = h.shape
    t = h.reshape(B * L, D)
    g_logits = (t @ p['moe_w_gr'] + p['moe_b_gr']).astype(F32)
    g_idx = jnp.argmax(g_logits, -1)
    g_w = jnp.max(jax.nn.softmax(g_logits, -1), -1, keepdims=True)
    e_logits = (jnp.einsum('td,dge->tge', t, p['moe_w_er']) + p['moe_b_er']).astype(F32)
    e_in = jnp.einsum('tge,tg->te', e_logits, jax.nn.one_hot(g_idx, MOE_GROUPS, dtype=F32))
    top_v, top_i = lax.top_k(e_in, MOE_TOP_K)
    top_w = jax.nn.softmax(top_v, -1) * g_w
    flat = g_idx[:, None] * MOE_EXPERTS_PER_GROUP + top_i
    dense_w = jnp.sum(jax.nn.one_hot(flat, MOE_N_EXPERTS, dtype=F32) * top_w[..., None], 1).astype(t.dtype)
    gate = jnp.einsum('td,edf->tef', t, p['moe_w_gate'])
    up = jnp.einsum('td,edf->tef', t, p['moe_w_up'])
    hid = jax.nn.silu(gate) * up * dense_w[..., None]
    return jnp.einsum('tef,efd->td', hid, p['moe_w_down']).reshape(B, L, D)


def trunk_layer(x, cond, rope, ctx_cache, p):
    D = x.shape[-1]
    mod = (jax.nn.silu(cond) @ p['w_mod'] + p['b_mod']).reshape(cond.shape[0], 1, 6, D)
    sh1, sc1, g1, sh2, sc2, g2 = [mod[:, :, i] for i in range(6)]
    h = x * (1 + sc1) + sh1
    mixed, ckv, kpe = token_mixers(h, rope, ctx_cache, p)
    x = layer_norm(ALPHA * x + g1 * mixed, p['ln1_g'], p['ln1_b'])
    h = x * (1 + sc2) + sh2
    x = layer_norm(ALPHA * x + g2 * hier_moe(h, p), p['ln2_g'], p['ln2_b'])
    return x, ckv, kpe


def setup_inputs(seed: int = 0) -> dict:
    key = jax.random.key(seed)
    keys = iter(jax.random.split(key, 48))

    def nrm(shape, scale):
        return jax.random.normal(next(keys), shape, F32) * scale

    D = D_MODEL
    FH = HY_FILTER_HIDDEN
    return {
        'x_prompt': nrm((BATCH, SEQ, D), 1.0),
        'x_sample': nrm((DEC_BATCH, DEC_SEQ, D), 1.0),
        'c': nrm((DEC_BATCH, D), 1.0),
        'cache_ckv': nrm((DEC_BATCH, DEPTH, PAST_LEN, MLA_KV_LORA), 1.0),
        'cache_kpe': nrm((DEC_BATCH, DEPTH, PAST_LEN, MLA_ROPE), 1.0),
        'c_ctx': nrm((D,), 1.0),
        'ln_in_g': 1.0 + nrm((D,), 0.02),
        'ln_in_b': nrm((D,), 0.02),
        'w_mod': nrm((DEPTH, D, 6 * D), 0.5 * D ** -0.5),
        'b_mod': nrm((DEPTH, 6 * D), 0.02),
        'w_in': nrm((DEPTH, D, IN_COLS), D ** -0.5),
        'gm_ln_g': 1.0 + nrm((DEPTH, GM_WIDTH), 0.02),
        'gm_ln_b': nrm((DEPTH, GM_WIDTH), 0.02),
        'gm_ws': nrm((DEPTH, GM_HEADS, CHUNK, CHUNK), CHUNK ** -0.5),
        'gm_bs': 1.0 + nrm((DEPTH, GM_HEADS, CHUNK), 0.02),
        'hy_conv_w': nrm((DEPTH, HY_SHORT, 3 * HY_WIDTH), HY_SHORT ** -0.5),
        'hy_conv_b': nrm((DEPTH, 3 * HY_WIDTH), 0.02),
        'hy_f_w1': nrm((DEPTH, HY_EMB_DIM, FH), HY_EMB_DIM ** -0.5),
        'hy_f_b1': nrm((DEPTH, FH), 0.02),
        'hy_f_w2': nrm((DEPTH, FH, FH), FH ** -0.5),
        'hy_f_b2': nrm((DEPTH, FH), 0.02),
        'hy_f_w3': nrm((DEPTH, FH, HY_ORDER * 2 * HY_WIDTH), 0.1 * FH ** -0.5),
        'hy_f_freq': 1.0 + nrm((DEPTH, 2, FH), 0.02),
        'hy_bias': nrm((DEPTH, HY_ORDER, HY_WIDTH), 0.1),
        'mla_gq': 1.0 + nrm((DEPTH, MLA_Q_LORA), 0.02),
        'mla_gkv': 1.0 + nrm((DEPTH, MLA_KV_LORA), 0.02),
        'mla_wuq': nrm((DEPTH, MLA_Q_LORA, MLA_HEADS * (MLA_NOPE + MLA_ROPE)), MLA_Q_LORA ** -0.5),
        'mla_wukv': nrm((DEPTH, MLA_KV_LORA, MLA_HEADS * (MLA_NOPE + MLA_V)), MLA_KV_LORA ** -0.5),
        'w_out': nrm((DEPTH, MIX_WIDTH, D), BETA * MIX_WIDTH ** -0.5),
        'ln1_g': 1.0 + nrm((DEPTH, D), 0.02),
        'ln1_b': nrm((DEPTH, D), 0.02),
        'ln2_g': 1.0 + nrm((DEPTH, D), 0.02),
        'ln2_b': nrm((DEPTH, D), 0.02),
        'moe_w_gr': nrm((DEPTH, D, MOE_GROUPS), D ** -0.5),
        'moe_b_gr': nrm((DEPTH, MOE_GROUPS), 0.01),
        'moe_w_er': nrm((DEPTH, D, MOE_GROUPS, MOE_EXPERTS_PER_GROUP), D ** -0.5),
        'moe_b_er': nrm((DEPTH, MOE_GROUPS, MOE_EXPERTS_PER_GROUP), 0.01),
        'moe_w_gate': nrm((DEPTH, MOE_N_EXPERTS, D, MOE_D_EXPERT), D ** -0.5),
        'moe_w_up': nrm((DEPTH, MOE_N_EXPERTS, D, MOE_D_EXPERT), D ** -0.5),
        'moe_w_down': nrm((DEPTH, MOE_N_EXPERTS, MOE_D_EXPERT, D), BETA * MOE_D_EXPERT ** -0.5),
    }


def reference(x_prompt, x_sample, c, cache_ckv, cache_kpe, c_ctx, ln_in_g, ln_in_b, w_mod, b_mod, w_in,
              gm_ln_g, gm_ln_b, gm_ws, gm_bs, hy_conv_w, hy_conv_b, hy_f_w1, hy_f_b1, hy_f_w2, hy_f_b2,
              hy_f_w3, hy_f_freq, hy_bias, mla_gq, mla_gkv, mla_wuq, mla_wukv, w_out, ln1_g, ln1_b,
              ln2_g, ln2_b, moe_w_gr, moe_b_gr, moe_w_er, moe_b_er, moe_w_gate, moe_w_up, moe_w_down):
    def layer_params(l):
        return dict(w_mod=w_mod[l], b_mod=b_mod[l], w_in=w_in[l], gm_ln_g=gm_ln_g[l], gm_ln_b=gm_ln_b[l],
                    gm_ws=gm_ws[l], gm_bs=gm_bs[l], hy_conv_w=hy_conv_w[l], hy_conv_b=hy_conv_b[l],
                    hy_f_w1=hy_f_w1[l], hy_f_b1=hy_f_b1[l], hy_f_w2=hy_f_w2[l], hy_f_b2=hy_f_b2[l],
                    hy_f_w3=hy_f_w3[l], hy_f_freq=hy_f_freq[l], hy_bias=hy_bias[l], mla_gq=mla_gq[l],
                    mla_gkv=mla_gkv[l], mla_wuq=mla_wuq[l], mla_wukv=mla_wukv[l], w_out=w_out[l],
                    ln1_g=ln1_g[l], ln1_b=ln1_b[l], ln2_g=ln2_g[l], ln2_b=ln2_b[l],
                    moe_w_gr=moe_w_gr[l], moe_b_gr=moe_b_gr[l], moe_w_er=moe_w_er[l], moe_b_er=moe_b_er[l],
                    moe_w_gate=moe_w_gate[l], moe_w_up=moe_w_up[l], moe_w_down=moe_w_down[l])

    xp = layer_norm(x_prompt, ln_in_g, ln_in_b)
    cond_ctx = c_ctx[None]
    ckv_list, kpe_list = [], []
    for l in range(DEPTH):
        xp, ckv_l, kpe_l = trunk_layer(xp, cond_ctx, None, None, layer_params(l))
        ckv_list.append(ckv_l)
        kpe_list.append(kpe_l)
    state_ckv = jnp.stack(ckv_list, axis=1)
    state_kpe = jnp.stack(kpe_list, axis=1)

    rope_lat = axial_rope(x_sample.shape[1])
    xs = layer_norm(x_sample, ln_in_g, ln_in_b)
    for l in range(DEPTH):
        xs, _, _ = trunk_layer(xs, c, rope_lat, (cache_ckv[:, l], cache_kpe[:, l]), layer_params(l))

    return (xp, xs, state_ckv, state_kpe)
```

```python
import functools
import math

import numpy as np
import jax
import jax.numpy as jnp
from jax import lax
from jax.experimental import pallas as pl
from jax.experimental.pallas import tpu as pltpu

D = 1024
N_CTX_B, CTX_L = 16, 256
N_LAT_B, LAT_L = 4, 1024
DEPTH = 2
T_CTX = N_CTX_B * CTX_L
T_LAT = N_LAT_B * LAT_L
T = T_CTX + T_LAT
PAST = 256
GRID_W = 64

GM_HEADS, GM_HD, GM_W, CHUNK = 4, 64, 256, 128
HY_W, HY_EMB, HY_FH = 256, 33, 64
NOPE, ROPE, VD, HEADS = 64, 32, 64, 8
Q_LORA, KV_LORA = 256, 128
SLOT = 128
N_GROUPS, EPG, N_EXP, D_EXP = 4, 8, 32, 256
ALPHA = (2.0 * DEPTH) ** 0.25
LN_EPS, RMS_EPS = 1e-5, 1e-6
ROPE_THETA = 10000.0

TM = 256
N_TILES = T // TM
CTX_TILES = T_CTX // TM
LAT_TILES_PER_B = LAT_L // TM
N_ASSIGN = 2 * T
MOE_TILES = N_ASSIGN // TM + N_EXP
MOE_ROWS = MOE_TILES * TM
IN_EXT = 1920

F32, BF16 = jnp.float32, jnp.bfloat16
VMEM_LIMIT = 52 * 1024 * 1024


def _cparams(*sem):
    return pltpu.CompilerParams(dimension_semantics=sem, vmem_limit_bytes=VMEM_LIMIT)


def _cond_of_tile(i):
    return jnp.where(i < CTX_TILES, 0, 1 + (i - CTX_TILES) // LAT_TILES_PER_B)


def _rope_block_of_tile(i):
    return jnp.where(i < CTX_TILES, 0, 1 + (i - CTX_TILES) % LAT_TILES_PER_B)


def _full(shape):
    n = len(shape)
    return pl.BlockSpec(shape, lambda *_: (0,) * n)


def _layer_norm(x, g, b):
    mu = jnp.mean(x, -1, keepdims=True)
    xc = x - mu
    var = jnp.mean(xc * xc, -1, keepdims=True)
    return xc * lax.rsqrt(var + LN_EPS) * g + b


def _rms_norm(x, g):
    return x * lax.rsqrt(jnp.mean(x * x, -1, keepdims=True) + RMS_EPS) * g


def _bdot(a, b):
    return jnp.dot(a.astype(BF16), b.astype(BF16), preferred_element_type=F32)


def _split3_dot(x, w_bf16):
    hi = x.astype(BF16)
    r1 = x - hi.astype(F32)
    mid = r1.astype(BF16)
    lo = (r1 - mid.astype(F32)).astype(BF16)
    dot = functools.partial(jnp.dot, preferred_element_type=F32)
    return dot(hi, w_bf16) + dot(mid, w_bf16) + dot(lo, w_bf16)


def _dft_tables(L):
    f = np.arange(L, dtype=np.int64)
    ft = np.outer(f, f) % (2 * L)
    ang = np.pi * ft / L
    c = np.cos(ang)
    s = np.sin(ang)
    alt = np.where(f % 2 == 0, 1.0, -1.0)
    sf = s.copy()
    sf[0, :] = alt
    return (jnp.asarray(c, BF16), jnp.asarray(sf, BF16), jnp.asarray(sf.T.copy(), BF16),
            jnp.asarray(alt[:, None], F32))


def _rope_tables():
    rows = LAT_L // GRID_W
    row = np.repeat(np.arange(rows, dtype=np.float64), GRID_W)
    col = np.tile(np.arange(GRID_W, dtype=np.float64), rows)
    n = ROPE // 4
    inv = ROPE_THETA ** (-np.arange(n, dtype=np.float64) / n)
    ang = np.concatenate([row[:, None] * inv, col[:, None] * inv], -1)
    cos = np.zeros((CTX_L + LAT_L, SLOT))
    sin = np.zeros((CTX_L + LAT_L, SLOT))
    cos[:, :NOPE + ROPE] = 1.0
    cos[CTX_L:, NOPE:NOPE + ROPE:2] = np.cos(ang)
    cos[CTX_L:, NOPE + 1:NOPE + ROPE:2] = np.cos(ang)
    sin[CTX_L:, NOPE:NOPE + ROPE:2] = -np.sin(ang)
    sin[CTX_L:, NOPE + 1:NOPE + ROPE:2] = np.sin(ang)
    scale = 1.0 / math.sqrt(NOPE + ROPE)
    return (jnp.asarray(cos * scale, F32), jnp.asarray(sin * scale, F32),
            jnp.asarray(cos, F32), jnp.asarray(sin, F32))


def _hyena_positions(L):
    t = jnp.linspace(0.0, 1.0, L, dtype=F32)[:, None]
    bands = (HY_EMB - 1) // 2
    w = 2.0 * math.pi * jnp.arange(L, dtype=F32)[:, None] / L
    f = jnp.linspace(1e-4, bands - 1, bands, dtype=F32)[None]
    z = jnp.concatenate([t, jnp.cos(f * w), -jnp.sin(f * w)], -1)
    z = jnp.pad(z, ((0, 0), (0, 128 - HY_EMB)))
    min_decay = math.log(1e-2) / 1.5
    max_decay = math.log(1e-2) / 0.3
    deltas = jnp.abs(jnp.linspace(min_decay, max_decay, HY_W, dtype=F32))
    return z, jnp.exp(-t * deltas)


def _mod_kernel(c_ref, w_ref, b_ref, o_ref):
    c = c_ref[...]
    s = c / (1.0 + jnp.exp(-c))
    o_ref[0] = _bdot(s, w_ref[0]) + b_ref[0]


def _modulation(cond8, w_mod, b_mod):
    tn = 1536
    return pl.pallas_call(
        _mod_kernel,
        out_shape=jax.ShapeDtypeStruct((DEPTH, 8, 6 * D), F32),
        grid=(DEPTH, 6 * D // tn),
        in_specs=[pl.BlockSpec((8, D), lambda l, j: (0, 0)),
                  pl.BlockSpec((1, D, tn), lambda l, j: (l, 0, j)),
                  pl.BlockSpec((1, 1, tn), lambda l, j: (l, 0, j))],
        out_specs=pl.BlockSpec((1, 8, tn), lambda l, j: (l, 0, j)),
        compiler_params=_cparams("parallel", "parallel"),
        name="modulation",
    )(cond8, w_mod, b_mod.reshape(DEPTH, 1, 6 * D))


def _ln_in_kernel(xc_ref, xl_ref, g_ref, b_ref, o_ref):
    i = pl.program_id(0)

    @pl.when(i < CTX_TILES)
    def _():
        o_ref[...] = _layer_norm(xc_ref[...], g_ref[...], b_ref[...])

    @pl.when(i >= CTX_TILES)
    def _():
        o_ref[...] = _layer_norm(xl_ref[...], g_ref[...], b_ref[...])


def _input_norm(xc, xl, g, b):
    return pl.pallas_call(
        _ln_in_kernel,
        out_shape=jax.ShapeDtypeStruct((T, D), F32),
        grid=(N_TILES,),
        in_specs=[pl.BlockSpec((TM, D), lambda i: (jnp.minimum(i, CTX_TILES - 1), 0)),
                  pl.BlockSpec((TM, D), lambda i: (jnp.maximum(i - CTX_TILES, 0), 0)),
                  _full((1, D)), _full((1, D))],
        out_specs=pl.BlockSpec((TM, D), lambda i: (i, 0)),
        compiler_params=_cparams("parallel"),
        name="input_norm",
    )(xc, xl, g.reshape(1, D), b.reshape(1, D))


def _filter_kernel(z_ref, w1_ref, b1_ref, w2_ref, b2_ref, w3_ref, fr_ref, dec_ref, c_ref, sf_ref,
                   alt_ref, kr_ref, ki_ref, krn_ref, *, L):
    hdot = functools.partial(jnp.dot, preferred_element_type=F32, precision=lax.Precision.HIGHEST)
    h1 = jnp.sin(fr_ref[0:1] * (hdot(z_ref[...], w1_ref[...]) + b1_ref[...]))
    h2 = jnp.sin(fr_ref[1:2] * (hdot(h1, w2_ref[...]) + b2_ref[...]))
    filt = hdot(h2, w3_ref[...])
    row = lax.broadcasted_iota(jnp.int32, (L, HY_W), 0)
    dec = dec_ref[...]
    for o in range(2):
        fwd = filt[:, o * 2 * HY_W:o * 2 * HY_W + HY_W] * dec
        bwd = jnp.where(row == 0, 0.0, filt[:, o * 2 * HY_W + HY_W:(o + 1) * 2 * HY_W] * dec)
        sm = fwd + bwd
        df = fwd - bwd
        kr = _bdot(c_ref[...], sm)
        ki = -_bdot(sf_ref[...], df)
        kr_nyq = jnp.sum(alt_ref[...] * sm, axis=0, keepdims=True)
        kr_ref[o] = jnp.where(row == 0, kr * (0.5 / L), kr * (1.0 / L))
        ki_ref[o] = jnp.where(row == 0, 0.0, ki * (1.0 / L))
        krn_ref[o] = jnp.where(row == 0, kr_nyq * (0.5 / L), kr * (1.0 / L))


def _filter_spectra(L, z, dec, dft, w1, b1, w2, b2, w3, freq):
    c, sf, _, alt = dft
    pad = 128 - HY_FH
    w1p = jnp.pad(w1, ((0, 128 - HY_EMB), (0, pad)))
    w2p = jnp.pad(w2, ((0, pad), (0, pad)))
    w3p = jnp.pad(w3, ((0, pad), (0, 0)))
    b1p = jnp.pad(b1, (0, pad)).reshape(1, 128)
    b2p = jnp.pad(b2, (0, pad)).reshape(1, 128)
    frp = jnp.pad(freq, ((0, 0), (0, pad)))
    shp = jax.ShapeDtypeStruct((2, L, HY_W), F32)
    args = (z, w1p, b1p, w2p, b2p, w3p, frp, dec, c, sf, alt)
    return pl.pallas_call(
        functools.partial(_filter_kernel, L=L),
        out_shape=(shp, shp, shp),
        grid=(1,),
        in_specs=[_full(a.shape) for a in args],
        out_specs=(_full((2, L, HY_W)),) * 3,
        compiler_params=_cparams("arbitrary"),
        name=f"hyena_filter_{L}",
    )(*args)


def _in_kernel(x_ref, mod_ref, w_ref, gg_ref, gb_ref, avg_ref, ws_ref, bs_ref, gq_ref, gkv_ref,
               wuq_ref, wuk_ref, wuv_ref, cq_ref, sq_ref, ck_ref, sk_ref,
               a_ref, hy_ref, q_ref, k_ref, v_ref, ckv_ref, kpe_ref):
    m = mod_ref[0]
    h = x_ref[...] * (1.0 + m[1:2]) + m[0:1]
    proj = _bdot(h, w_ref[...])

    gu = jax.nn.gelu(proj[:, 0:GM_W], approximate=True)
    gv = jax.nn.gelu(proj[:, GM_W:2 * GM_W], approximate=True)
    avg = avg_ref[...]
    mu = _split3_dot(gv, avg)
    vc = gv - mu
    var = _split3_dot(vc * vc, avg)
    vln = (vc * lax.rsqrt(var + LN_EPS) * gg_ref[...] + gb_ref[...]).astype(BF16)
    lane = lax.broadcasted_iota(jnp.int32, (CHUNK, GM_W), 1)
    for c in range(TM // CHUNK):
        vchunk = vln[c * CHUNK:(c + 1) * CHUNK]
        s = bs_ref[...]
        for hd in range(GM_HEADS):
            sh = jnp.dot(ws_ref[hd], vchunk, preferred_element_type=F32)
            s = s + jnp.where(lane // GM_HD == hd, sh, 0.0)
        a_ref[c * CHUNK:(c + 1) * CHUNK, :] = (gu[c * CHUNK:(c + 1) * CHUNK] * s).astype(BF16)

    hy_ref[...] = proj[:, 512:1280]

    cq = _rms_norm(proj[:, 1280:1536], gq_ref[...])
    qq = _bdot(cq, wuq_ref[...])
    cos_q = jnp.concatenate([cq_ref[...]] * HEADS, axis=1)
    sin_q = jnp.concatenate([sq_ref[...]] * HEADS, axis=1)
    q_ref[...] = (qq[:, :HEADS * SLOT] * cos_q + qq[:, HEADS * SLOT:] * sin_q).astype(BF16)

    ckv = _rms_norm(proj[:, 1536:1664], gkv_ref[...])
    ckv_ref[...] = ckv
    kpe = proj[:, 1664:1792]
    kpe_ref[...] = kpe
    krot = kpe * ck_ref[...] + proj[:, 1792:1920] * sk_ref[...]
    kn = _bdot(ckv, wuk_ref[...])
    k_ref[...] = (kn + jnp.concatenate([krot] * HEADS, axis=1)).astype(BF16)
    v_ref[...] = _bdot(ckv, wuv_ref[...]).astype(BF16)


def _in_proj(x, mod_l, w_in_ext, gm_g, gm_b, avg, ws, bs_full, gq, gkv, wuq_arr, wuk_arr, wuv_arr, rope):
    cos_q, sin_q, cos_k, sin_k = rope
    tile = lambda n: pl.BlockSpec((TM, n), lambda i: (i, 0))
    rope_spec = pl.BlockSpec((TM, SLOT), lambda i: (_rope_block_of_tile(i), 0))
    out_shapes = (jax.ShapeDtypeStruct((T, GM_W), BF16),
                  jax.ShapeDtypeStruct((T, 3 * HY_W), F32),
                  jax.ShapeDtypeStruct((T, HEADS * SLOT), BF16),
                  jax.ShapeDtypeStruct((T, HEADS * SLOT), BF16),
                  jax.ShapeDtypeStruct((T, HEADS * VD), BF16),
                  jax.ShapeDtypeStruct((T, KV_LORA), F32),
                  jax.ShapeDtypeStruct((T, SLOT), F32))
    return pl.pallas_call(
        _in_kernel,
        out_shape=out_shapes,
        grid=(N_TILES,),
        in_specs=[tile(D),
                  pl.BlockSpec((1, 6, D), lambda i: (_cond_of_tile(i), 0, 0)),
                  _full((D, IN_EXT)), _full((1, GM_W)), _full((1, GM_W)), _full((GM_W, GM_W)),
                  _full((GM_HEADS, CHUNK, CHUNK)), _full((CHUNK, GM_W)),
                  _full((1, Q_LORA)), _full((1, KV_LORA)),
                  _full((Q_LORA, 2 * HEADS * SLOT)), _full((KV_LORA, HEADS * SLOT)),
                  _full((KV_LORA, HEADS * VD)),
                  rope_spec, rope_spec, rope_spec, rope_spec],
        out_specs=(tile(GM_W), tile(3 * HY_W), tile(HEADS * SLOT), tile(HEADS * SLOT),
                   tile(HEADS * VD), tile(KV_LORA), tile(SLOT)),
        compiler_params=_cparams("parallel"),
        name="in_proj",
    )(x, mod_l, w_in_ext, gm_g, gm_b, avg, ws, bs_full, gq, gkv, wuq_arr, wuk_arr, wuv_arr,
      cos_q, sin_q, cos_k, sin_k)


def _cache_kernel(ckv_ref, kpe_ref, wuk_ref, wuv_ref, k_ref, v_ref):
    ckv = ckv_ref[0, 0]
    kn = _bdot(ckv, wuk_ref[0])
    k_ref[0, 0] = (kn + jnp.concatenate([kpe_ref[0, 0]] * HEADS, axis=1)).astype(BF16)
    v_ref[0, 0] = _bdot(ckv, wuv_ref[0]).astype(BF16)


def _cache_kv(cache_ckv, cache_kpe_slot, wuk_arr, wuv_arr):
    return pl.pallas_call(
        _cache_kernel,
        out_shape=(jax.ShapeDtypeStruct((DEPTH, N_LAT_B, PAST, HEADS * SLOT), BF16),
                   jax.ShapeDtypeStruct((DEPTH, N_LAT_B, PAST, HEADS * VD), BF16)),
        grid=(DEPTH, N_LAT_B),
        in_specs=[pl.BlockSpec((1, 1, PAST, KV_LORA), lambda l, b: (b, l, 0, 0)),
                  pl.BlockSpec((1, 1, PAST, SLOT), lambda l, b: (b, l, 0, 0)),
                  pl.BlockSpec((1, KV_LORA, HEADS * SLOT), lambda l, b: (l, 0, 0)),
                  pl.BlockSpec((1, KV_LORA, HEADS * VD), lambda l, b: (l, 0, 0))],
        out_specs=(pl.BlockSpec((1, 1, PAST, HEADS * SLOT), lambda l, b: (l, b, 0, 0)),
                   pl.BlockSpec((1, 1, PAST, HEADS * VD), lambda l, b: (l, b, 0, 0))),
        compiler_params=_cparams("parallel", "parallel"),
        name="cache_kv",
    )(cache_ckv, cache_kpe_slot, wuk_arr, wuv_arr)


def _attend(q_ref, segments, o_ref):
    lq = q_ref.shape[0]
    lane = lax.broadcasted_iota(jnp.int32, (lq, 2 * VD), 1)
    nt = (((1,), (1,)), ((), ()))
    for pair in range(HEADS // 2):
        outs = []
        for hd in (2 * pair, 2 * pair + 1):
            qh = q_ref[:, hd * SLOT:(hd + 1) * SLOT]
            scores = [lax.dot_general(qh, k_ref[:, hd * SLOT:(hd + 1) * SLOT], nt,
                                      preferred_element_type=F32) for k_ref, _ in segments]
            mx = functools.reduce(jnp.maximum, [jnp.max(s, -1, keepdims=True) for s in scores])
            ps = [jnp.exp(s - mx) for s in scores]
            den = functools.reduce(jnp.add, [jnp.sum(p, -1, keepdims=True) for p in ps])
            acc = functools.reduce(jnp.add, [
                jnp.dot(p.astype(BF16), v_ref[:, pair * 2 * VD:(pair + 1) * 2 * VD],
                        preferred_element_type=F32) for p, (_, v_ref) in zip(ps, segments)])
            outs.append(acc / den)
        o_ref[:, pair * 2 * VD:(pair + 1) * 2 * VD] = jnp.where(lane < VD, outs[0], outs[1]).astype(BF16)


def _attn_ctx_kernel(q_ref, k_ref, v_ref, o_ref):
    _attend(q_ref, [(k_ref, v_ref)], o_ref)


def _attn_lat_kernel(q_ref, kc_ref, vc_ref, k_ref, v_ref, o_ref):
    _attend(q_ref, [(kc_ref.at[0], vc_ref.at[0]), (k_ref, v_ref)], o_ref)


def _attention(q, k, v, kc, vc):
    kw, vw = HEADS * SLOT, HEADS * VD
    ctx = pl.pallas_call(
        _attn_ctx_kernel,
        out_shape=jax.ShapeDtypeStruct((T_CTX, vw), BF16),
        grid=(N_CTX_B,),
        in_specs=[pl.BlockSpec((CTX_L, kw), lambda b: (b, 0)),
                  pl.BlockSpec((CTX_L, kw), lambda b: (b, 0)),
                  pl.BlockSpec((CTX_L, vw), lambda b: (b, 0))],
        out_specs=pl.BlockSpec((CTX_L, vw), lambda b: (b, 0)),
        compiler_params=_cparams("parallel"),
        name="attn_ctx",
    )(q, k, v)
    nq = LAT_L // TM
    off = T_CTX // LAT_L
    lat = pl.pallas_call(
        _attn_lat_kernel,
        out_shape=jax.ShapeDtypeStruct((T_LAT, vw), BF16),
        grid=(N_LAT_B, nq),
        in_specs=[pl.BlockSpec((TM, kw), lambda b, j: (CTX_TILES + b * nq + j, 0)),
                  pl.BlockSpec((1, PAST, kw), lambda b, j: (b, 0, 0)),
                  pl.BlockSpec((1, PAST, vw), lambda b, j: (b, 0, 0)),
                  pl.BlockSpec((LAT_L, kw), lambda b, j: (off + b, 0)),
                  pl.BlockSpec((LAT_L, vw), lambda b, j: (off + b, 0))],
        out_specs=pl.BlockSpec((TM, vw), lambda b, j: (b * nq + j, 0)),
        compiler_params=_cparams("parallel", "parallel"),
        name="attn_lat",
    )(q, kc, vc, k, v)
    return jnp.concatenate([ctx, lat], axis=0)


def _hyena_kernel(hy_ref, cw_ref, cb_ref, c_ref, sf_ref, sb_ref, kr_ref, ki_ref, krn_ref, hb_ref,
                  o_ref, *, L):
    x = hy_ref[...]
    row = lax.broadcasted_iota(jnp.int32, x.shape, 0)
    prev = jnp.where(row == 0, 0.0, pltpu.roll(x, 1, 0))
    nxt = jnp.where(row == L - 1, 0.0, pltpu.roll(x, L - 1, 0))
    z = prev * cw_ref[0:1] + x * cw_ref[1:2] + nxt * cw_ref[2:3] + cb_ref[...]
    y = z[:, 2 * HY_W:]
    for o in range(2):
        gate = z[:, o * HY_W:(o + 1) * HY_W]
        yb = y.astype(BF16)
        a_re = jnp.dot(c_ref[...], yb, preferred_element_type=F32)
        a_im = jnp.dot(sf_ref[...], yb, preferred_element_type=F32)
        ki = ki_ref[o]
        z_re = a_re * kr_ref[o] + a_im * ki
        z_im = a_im * krn_ref[o] - a_re * ki
        conv = (jnp.dot(c_ref[...], z_re.astype(BF16), preferred_element_type=F32)
                + jnp.dot(sb_ref[...], z_im.astype(BF16), preferred_element_type=F32))
        y = gate * (conv + y * hb_ref[o:o + 1])
    o_ref[...] = y.astype(BF16)


def _hyena_group(hy, L, nb, blk0, dft, spectra, conv_w, conv_b, hy_bias):
    c, sf, sb, _ = dft
    kr, ki, krn = spectra
    return pl.pallas_call(
        functools.partial(_hyena_kernel, L=L),
        out_shape=jax.ShapeDtypeStruct((nb * L, HY_W), BF16),
        grid=(nb,),
        in_specs=[pl.BlockSpec((L, 3 * HY_W), lambda b: (blk0 + b, 0)),
                  _full((3, 3 * HY_W)), _full((1, 3 * HY_W)),
                  _full((L, L)), _full((L, L)), _full((L, L)),
                  _full((2, L, HY_W)), _full((2, L, HY_W)), _full((2, L, HY_W)),
                  _full((2, HY_W))],
        out_specs=pl.BlockSpec((L, HY_W), lambda b: (b, 0)),
        compiler_params=_cparams("parallel"),
        name=f"hyena_{L}",
    )(hy, conv_w, conv_b.reshape(1, 3 * HY_W), c, sf, sb, kr, ki, krn, hy_bias)


def _out_kernel(a_ref, b_ref, m_ref, w_ref, x_ref, mod_ref, g_ref, be_ref, wr_ref, br_ref, tri_ref,
                x1_ref, h2_ref, route_ref, cnt_ref, carry_ref):
    i = pl.program_id(0)

    @pl.when(i == 0)
    def _():
        carry_ref[...] = jnp.zeros_like(carry_ref)

    dot = functools.partial(jnp.dot, preferred_element_type=F32)
    mixed = (dot(a_ref[...], w_ref[0:GM_W]) + dot(b_ref[...], w_ref[GM_W:GM_W + HY_W])
             + dot(m_ref[...], w_ref[GM_W + HY_W:]))
    m = mod_ref[0]
    x1 = _layer_norm(ALPHA * x_ref[...] + m[2:3] * mixed, g_ref[...], be_ref[...])
    x1_ref[...] = x1
    h2 = x1 * (1.0 + m[4:5]) + m[3:4]
    h2_ref[...] = h2

    logits = jnp.dot(h2, wr_ref[...], preferred_element_type=F32,
                     precision=lax.Precision.HIGHEST) + br_ref[...]
    lane = lax.broadcasted_iota(jnp.int32, logits.shape, 1)
    lanef = lane.astype(F32)
    big = jnp.float32(1e9)
    ninf = jnp.float32(-jnp.inf)
    is_g = lane < N_GROUPS
    gl = jnp.where(is_g, logits, ninf)
    gmax = jnp.max(gl, -1, keepdims=True)
    gidx = jnp.min(jnp.where(gl == gmax, lanef, big), -1, keepdims=True)
    gw = 1.0 / jnp.sum(jnp.where(is_g, jnp.exp(logits - gmax), 0.0), -1, keepdims=True)
    ex = lane - N_GROUPS
    in_group = (ex >= 0) & (ex < N_EXP) & ((ex // EPG).astype(F32) == gidx)
    el = jnp.where(in_group, logits, ninf)
    v1 = jnp.max(el, -1, keepdims=True)
    i1 = jnp.min(jnp.where(el == v1, lanef, big), -1, keepdims=True)
    el2 = jnp.where(lanef == i1, ninf, el)
    v2 = jnp.max(el2, -1, keepdims=True)
    i2 = jnp.min(jnp.where(el2 == v2, lanef, big), -1, keepdims=True)
    e21 = jnp.exp(v2 - v1)
    w1 = gw / (1.0 + e21)
    w2 = gw * e21 / (1.0 + e21)
    e1 = i1 - N_GROUPS
    e2 = i2 - N_GROUPS

    oh1 = jnp.where(lanef == e1, 1.0, 0.0)
    oh2 = jnp.where(lanef == e2, 1.0, 0.0)
    ex1 = dot(tri_ref[...], oh1.astype(BF16))
    ex2 = dot(tri_ref[...], oh2.astype(BF16))
    col1 = jnp.sum(oh1, axis=0, keepdims=True)
    col2 = jnp.sum(oh2, axis=0, keepdims=True)
    carry = carry_ref[...]
    r1 = jnp.sum(oh1 * (carry + ex1), -1, keepdims=True)
    r2 = jnp.sum(oh2 * (carry + col1 + ex2), -1, keepdims=True)
    carry = carry + col1 + col2
    carry_ref[...] = carry
    cnt_ref[...] = jnp.broadcast_to(carry, cnt_ref.shape)

    route = jnp.zeros_like(logits)
    for j, val in enumerate((e1, e2, w1, w2, r1, r2)):
        route = jnp.where(lane == j, val, route)
    route_ref[...] = route


def _out_proj(a, hyb, att, w_out, x, mod_l, g, b, w_route, b_route, tri):
    tile = lambda n: pl.BlockSpec((TM, n), lambda i: (i, 0))
    return pl.pallas_call(
        _out_kernel,
        out_shape=(jax.ShapeDtypeStruct((T, D), F32), jax.ShapeDtypeStruct((T, D), F32),
                   jax.ShapeDtypeStruct((T, 128), F32), jax.ShapeDtypeStruct((8, 128), F32)),
        grid=(N_TILES,),
        in_specs=[tile(GM_W), tile(HY_W), tile(HEADS * VD), _full((D, D)), tile(D),
                  pl.BlockSpec((1, 6, D), lambda i: (_cond_of_tile(i), 0, 0)),
                  _full((1, D)), _full((1, D)), _full((D, 128)), _full((1, 128)), _full((TM, TM))],
        out_specs=(tile(D), tile(D), tile(128), _full((8, 128))),
        scratch_shapes=[pltpu.VMEM((1, 128), F32)],
        compiler_params=_cparams("arbitrary"),
        name="out_proj_route",
    )(a, hyb, att, w_out, x, mod_l, g, b, w_route, b_route, tri)


def _expert_kernel(te_ref, tv_ref, tok_ref, h_hbm, wg_ref, wu_ref, wd_ref, y_ref, xbuf, sem):
    j = pl.program_id(0)

    @pl.when(tv_ref[j] == 0)
    def _():
        y_ref[...] = jnp.zeros_like(y_ref)

    @pl.when(tv_ref[j] != 0)
    def _():
        def row_copy(r):
            tok = tok_ref[j * TM + r]
            return pltpu.make_async_copy(h_hbm.at[pl.ds(tok, 1), :], xbuf.at[pl.ds(r, 1), :], sem)

        def start(r, c):
            row_copy(r).start()
            return c

        def wait(r, c):
            row_copy(r).wait()
            return c

        lax.fori_loop(0, TM, start, 0)
        lax.fori_loop(0, TM, wait, 0)
        xb = xbuf[...].astype(BF16)
        gate = jnp.dot(xb, wg_ref[0].astype(BF16), preferred_element_type=F32)
        up = jnp.dot(xb, wu_ref[0].astype(BF16), preferred_element_type=F32)
        hid = gate / (1.0 + jnp.exp(-gate)) * up
        y_ref[...] = jnp.dot(hid.astype(BF16), wd_ref[0].astype(BF16), preferred_element_type=F32)


def _experts(tile_expert, tile_valid, row_token, h2, w_gate, w_up, w_down):
    grid_spec = pltpu.PrefetchScalarGridSpec(
        num_scalar_prefetch=3,
        grid=(MOE_TILES,),
        in_specs=[pl.BlockSpec(memory_space=pl.ANY),
                  pl.BlockSpec((1, D, D_EXP), lambda j, te, tv, tok: (te[j], 0, 0)),
                  pl.BlockSpec((1, D, D_EXP), lambda j, te, tv, tok: (te[j], 0, 0)),
                  pl.BlockSpec((1, D_EXP, D), lambda j, te, tv, tok: (te[j], 0, 0))],
        out_specs=pl.BlockSpec((TM, D), lambda j, te, tv, tok: (j, 0)),
        scratch_shapes=[pltpu.VMEM((TM, D), F32), pltpu.SemaphoreType.DMA(())],
    )
    return pl.pallas_call(
        _expert_kernel,
        out_shape=jax.ShapeDtypeStruct((MOE_ROWS, D), F32),
        grid_spec=grid_spec,
        compiler_params=_cparams("arbitrary"),
        name="experts",
    )(tile_expert, tile_valid, row_token, h2, w_gate, w_up, w_down)


def _combine_kernel(pos_ref, y_hbm, route_ref, x1_ref, mod_ref, g_ref, b_ref, o_ref, buf, sem):
    i = pl.program_id(0)

    def row_copy(k, r):
        p = pos_ref[k * T + i * TM + r]
        return pltpu.make_async_copy(y_hbm.at[pl.ds(p, 1), :], buf.at[k, pl.ds(r, 1), :], sem)

    def start(r, c):
        row_copy(0, r).start()
        row_copy(1, r).start()
        return c

    def wait(r, c):
        row_copy(0, r).wait()
        row_copy(1, r).wait()
        return c

    lax.fori_loop(0, TM, start, 0)
    lax.fori_loop(0, TM, wait, 0)
    route = route_ref[...]
    moe = route[:, 2:3] * buf[0] + route[:, 3:4] * buf[1]
    m = mod_ref[0]
    o_ref[...] = _layer_norm(ALPHA * x1_ref[...] + m[5:6] * moe, g_ref[...], b_ref[...])


def _combine(pos, y, route, x1, mod_l, g, b):
    grid_spec = pltpu.PrefetchScalarGridSpec(
        num_scalar_prefetch=1,
        grid=(N_TILES,),
        in_specs=[pl.BlockSpec(memory_space=pl.ANY),
                  pl.BlockSpec((TM, 128), lambda i, pos: (i, 0)),
                  pl.BlockSpec((TM, D), lambda i, pos: (i, 0)),
                  pl.BlockSpec((1, 6, D), lambda i, pos: (_cond_of_tile(i), 0, 0)),
                  pl.BlockSpec((1, D), lambda i, pos: (0, 0)),
                  pl.BlockSpec((1, D), lambda i, pos: (0, 0))],
        out_specs=pl.BlockSpec((TM, D), lambda i, pos: (i, 0)),
        scratch_shapes=[pltpu.VMEM((2, TM, D), F32), pltpu.SemaphoreType.DMA(())],
    )
    return pl.pallas_call(
        _combine_kernel,
        out_shape=jax.ShapeDtypeStruct((T, D), F32),
        grid_spec=grid_spec,
        compiler_params=_cparams("arbitrary"),
        name="moe_combine",
    )(pos, y, route, x1, mod_l, g, b)


def _dispatch_tables(route, counts):
    e = route[:, 0:2].astype(jnp.int32)
    rank = route[:, 4:6].astype(jnp.int32)
    cnt = counts[0, :N_EXP].astype(jnp.int32)
    ntile = (cnt + TM - 1) // TM
    tile_end = jnp.cumsum(ntile)
    tile_off = tile_end - ntile
    pos = tile_off[e] * TM + rank
    tok = jnp.broadcast_to(jnp.arange(T, dtype=jnp.int32)[:, None], (T, 2))
    row_token = jnp.zeros((MOE_ROWS,), jnp.int32).at[pos.reshape(-1)].set(
        tok.reshape(-1), unique_indices=True)
    total = tile_end[-1]
    jt = jnp.minimum(jnp.arange(MOE_TILES, dtype=jnp.int32), total - 1)
    tile_expert = jnp.minimum(jnp.searchsorted(tile_end, jt, side="right"), N_EXP - 1).astype(jnp.int32)
    tile_valid = (jnp.arange(MOE_TILES, dtype=jnp.int32) < total).astype(jnp.int32)
    return pos.T.reshape(-1), row_token, tile_expert, tile_valid


def _swap_pairs(w):
    return w.reshape(w.shape[:-1] + (ROPE // 2, 2))[..., ::-1].reshape(w.shape)


def _rope_slot(pe):
    return jnp.pad(pe, [(0, 0)] * (pe.ndim - 1) + [(NOPE, SLOT - NOPE - ROPE)])


def _w_in_layout(w):
    pe = w[:, 1664:]
    return jnp.concatenate([w[:, :1664], _rope_slot(pe), _rope_slot(_swap_pairs(pe))], axis=1).astype(BF16)


def _wuq_layout(w):
    w = w.reshape(Q_LORA, HEADS, NOPE + ROPE)
    nope, pe = w[..., :NOPE], w[..., NOPE:]
    plain = jnp.concatenate([nope, jnp.pad(pe, ((0, 0), (0, 0), (0, SLOT - NOPE - ROPE)))], axis=-1)
    swapped = _rope_slot(_swap_pairs(pe))
    return jnp.concatenate([plain.reshape(Q_LORA, HEADS * SLOT),
                            swapped.reshape(Q_LORA, HEADS * SLOT)], axis=1).astype(BF16)


def _wukv_layout(w):
    w = w.reshape(DEPTH, KV_LORA, HEADS, NOPE + VD)
    wk = jnp.pad(w[..., :NOPE], ((0, 0), (0, 0), (0, 0), (0, SLOT - NOPE)))
    return (wk.reshape(DEPTH, KV_LORA, HEADS * SLOT).astype(BF16),
            w[..., NOPE:].reshape(DEPTH, KV_LORA, HEADS * VD).astype(BF16))


def kernel(x_prompt, x_sample, c, cache_ckv, cache_kpe, c_ctx, ln_in_g, ln_in_b, w_mod, b_mod, w_in,
           gm_ln_g, gm_ln_b, gm_ws, gm_bs, hy_conv_w, hy_conv_b, hy_f_w1, hy_f_b1, hy_f_w2, hy_f_b2,
           hy_f_w3, hy_f_freq, hy_bias, mla_gq, mla_gkv, mla_wuq, mla_wukv, w_out, ln1_g, ln1_b,
           ln2_g, ln2_b, moe_w_gr, moe_b_gr, moe_w_er, moe_b_er, moe_w_gate, moe_w_up, moe_w_down):
    rope = _rope_tables()
    dft = {L: _dft_tables(L) for L in (CTX_L, LAT_L)}
    pos_tab = {L: _hyena_positions(L) for L in (CTX_L, LAT_L)}
    hd = np.arange(GM_W) // GM_HD
    avg = jnp.asarray((hd[:, None] == hd[None, :]) / GM_HD, BF16)
    tri = jnp.asarray(np.tril(np.ones((TM, TM)), -1), BF16)

    cond8 = jnp.concatenate([c_ctx[None], c, jnp.zeros((8 - 1 - N_LAT_B, D), F32)], axis=0)
    mod = _modulation(cond8, w_mod, b_mod).reshape(DEPTH, 8, 6, D)

    wuk_all, wuv_all = _wukv_layout(mla_wukv)
    kc_all, vc_all = _cache_kv(cache_ckv, _rope_slot(cache_kpe), wuk_all, wuv_all)

    x = _input_norm(x_prompt.reshape(T_CTX, D), x_sample.reshape(T_LAT, D), ln_in_g, ln_in_b)
    ckv_states, kpe_states = [], []
    for l in range(DEPTH):
        w_in_ext = _w_in_layout(w_in[l])
        wuq_arr = _wuq_layout(mla_wuq[l])
        bs_full = jnp.repeat(gm_bs[l].T, GM_HD, axis=1)
        a, hy, q, k, v, ckv, kpe = _in_proj(
            x, mod[l], w_in_ext, gm_ln_g[l].reshape(1, GM_W), gm_ln_b[l].reshape(1, GM_W), avg,
            gm_ws[l].astype(BF16), bs_full, mla_gq[l].reshape(1, Q_LORA), mla_gkv[l].reshape(1, KV_LORA),
            wuq_arr, wuk_all[l], wuv_all[l], rope)
        ckv_states.append(ckv[:T_CTX].reshape(N_CTX_B, CTX_L, KV_LORA))
        kpe_states.append(kpe[:T_CTX, NOPE:NOPE + ROPE].reshape(N_CTX_B, CTX_L, ROPE))

        hyb = []
        for L, nb, blk0 in ((CTX_L, N_CTX_B, 0), (LAT_L, N_LAT_B, T_CTX // LAT_L)):
            z, dec = pos_tab[L]
            spectra = _filter_spectra(L, z, dec, dft[L], hy_f_w1[l], hy_f_b1[l], hy_f_w2[l],
                                      hy_f_b2[l], hy_f_w3[l], hy_f_freq[l])
            hyb.append(_hyena_group(hy, L, nb, blk0, dft[L], spectra, hy_conv_w[l], hy_conv_b[l],
                                    hy_bias[l]))
        hyb = jnp.concatenate(hyb, axis=0)

        att = _attention(q, k, v, kc_all[l], vc_all[l])

        w_route = jnp.pad(jnp.concatenate([moe_w_gr[l], moe_w_er[l].reshape(D, N_EXP)], axis=1),
                          ((0, 0), (0, 128 - N_GROUPS - N_EXP)))
        b_route = jnp.pad(jnp.concatenate([moe_b_gr[l], moe_b_er[l].reshape(N_EXP)]),
                          (0, 128 - N_GROUPS - N_EXP)).reshape(1, 128)
        x1, h2, route, counts = _out_proj(a, hyb, att, w_out[l].astype(BF16), x, mod[l],
                                          ln1_g[l].reshape(1, D), ln1_b[l].reshape(1, D),
                                          w_route, b_route, tri)
        pos, row_token, tile_expert, tile_valid = _dispatch_tables(route, counts)
        y = _experts(tile_expert, tile_valid, row_token, h2, moe_w_gate[l], moe_w_up[l], moe_w_down[l])
        x = _combine(pos, y, route, x1, mod[l], ln2_g[l].reshape(1, D), ln2_b[l].reshape(1, D))

    y_prompt = x[:T_CTX].reshape(N_CTX_B, CTX_L, D)
    y_sample = x[T_CTX:].reshape(N_LAT_B, LAT_L, D)
    return (y_prompt, y_sample, jnp.stack(ckv_states, axis=1), jnp.stack(kpe_states, axis=1))
```

```python
import functools
import math

import numpy as np
import jax
import jax.numpy as jnp
from jax import lax
from jax.experimental import pallas as pl
from jax.experimental.pallas import tpu as pltpu

D = 1024
N_CTX_B, CTX_L = 16, 256
N_LAT_B, LAT_L = 4, 1024
DEPTH = 2
T_CTX = N_CTX_B * CTX_L
T_LAT = N_LAT_B * LAT_L
T = T_CTX + T_LAT
PAST = 256
GRID_W = 64

GM_HEADS, GM_HD, GM_W, CHUNK = 4, 64, 256, 128
HY_W, HY_EMB, HY_FH = 256, 33, 64
NOPE, ROPE, VD, HEADS = 64, 32, 64, 8
Q_LORA, KV_LORA = 256, 128
SLOT = 128
N_GROUPS, EPG, N_EXP, D_EXP = 4, 8, 32, 256
ALPHA = (2.0 * DEPTH) ** 0.25
LN_EPS, RMS_EPS = 1e-5, 1e-6
ROPE_THETA = 10000.0

TM = 256
N_TILES = T // TM
CTX_TILES = T_CTX // TM
LAT_TILES_PER_B = LAT_L // TM
N_ASSIGN = 2 * T
BLK = 8
BLK_PER_TILE = TM // BLK
LOC = 2 * TM + N_EXP * BLK
LOC_BLKS = LOC // BLK
ZERO_BLK = LOC_BLKS - 1
MOE_TILES = (N_ASSIGN + N_TILES * N_EXP * (BLK - 1)) // TM + N_EXP
MOE_ROWS = MOE_TILES * TM
IN_EXT = 1920

F32, BF16 = jnp.float32, jnp.bfloat16
VMEM_LIMIT = 52 * 1024 * 1024


def _cparams(*sem):
    return pltpu.CompilerParams(dimension_semantics=sem, vmem_limit_bytes=VMEM_LIMIT)


def _cond_of_tile(i):
    return jnp.where(i < CTX_TILES, 0, 1 + (i - CTX_TILES) // LAT_TILES_PER_B)


def _rope_block_of_tile(i):
    return jnp.where(i < CTX_TILES, 0, 1 + (i - CTX_TILES) % LAT_TILES_PER_B)


def _full(shape):
    n = len(shape)
    return pl.BlockSpec(shape, lambda *_: (0,) * n)


def _layer_norm(x, g, b):
    mu = jnp.mean(x, -1, keepdims=True)
    xc = x - mu
    var = jnp.mean(xc * xc, -1, keepdims=True)
    return xc * lax.rsqrt(var + LN_EPS) * g + b


def _rms_norm(x, g):
    return x * lax.rsqrt(jnp.mean(x * x, -1, keepdims=True) + RMS_EPS) * g


def _bdot(a, b):
    return jnp.dot(a.astype(BF16), b.astype(BF16), preferred_element_type=F32)


def _split3_dot(x, w_bf16):
    hi = x.astype(BF16)
    r1 = x - hi.astype(F32)
    mid = r1.astype(BF16)
    lo = (r1 - mid.astype(F32)).astype(BF16)
    dot = functools.partial(jnp.dot, preferred_element_type=F32)
    return dot(hi, w_bf16) + dot(mid, w_bf16) + dot(lo, w_bf16)


def _dft_tables(L):
    f = np.arange(L, dtype=np.int64)
    ft = np.outer(f, f) % (2 * L)
    ang = np.pi * ft / L
    c = np.cos(ang)
    s = np.sin(ang)
    alt = np.where(f % 2 == 0, 1.0, -1.0)
    sf = s.copy()
    sf[0, :] = alt
    return (jnp.asarray(c, F32).astype(BF16), jnp.asarray(sf, F32).astype(BF16),
            jnp.asarray(sf.T.copy(), F32).astype(BF16), jnp.asarray(alt[:, None], F32))


def _rope_tables():
    rows = LAT_L // GRID_W
    row = np.repeat(np.arange(rows, dtype=np.float64), GRID_W)
    col = np.tile(np.arange(GRID_W, dtype=np.float64), rows)
    n = ROPE // 4
    inv = ROPE_THETA ** (-np.arange(n, dtype=np.float64) / n)
    ang = np.concatenate([row[:, None] * inv, col[:, None] * inv], -1)
    cos = np.zeros((CTX_L + LAT_L, SLOT))
    sin = np.zeros((CTX_L + LAT_L, SLOT))
    cos[:, :NOPE + ROPE] = 1.0
    cos[CTX_L:, NOPE:NOPE + ROPE:2] = np.cos(ang)
    cos[CTX_L:, NOPE + 1:NOPE + ROPE:2] = np.cos(ang)
    sin[CTX_L:, NOPE:NOPE + ROPE:2] = -np.sin(ang)
    sin[CTX_L:, NOPE + 1:NOPE + ROPE:2] = np.sin(ang)
    scale = 1.0 / math.sqrt(NOPE + ROPE)
    return (jnp.asarray(cos * scale, F32), jnp.asarray(sin * scale, F32),
            jnp.asarray(cos, F32), jnp.asarray(sin, F32))


def _hyena_positions(L):
    t = jnp.linspace(0.0, 1.0, L, dtype=F32)[:, None]
    bands = (HY_EMB - 1) // 2
    w = 2.0 * math.pi * jnp.arange(L, dtype=F32)[:, None] / L
    f = jnp.linspace(1e-4, bands - 1, bands, dtype=F32)[None]
    z = jnp.concatenate([t, jnp.cos(f * w), -jnp.sin(f * w)], -1)
    z = jnp.pad(z, ((0, 0), (0, 128 - HY_EMB)))
    min_decay = math.log(1e-2) / 1.5
    max_decay = math.log(1e-2) / 0.3
    deltas = jnp.abs(jnp.linspace(min_decay, max_decay, HY_W, dtype=F32))
    return z, jnp.exp(-t * deltas)


def _mod_kernel(c_ref, w_ref, b_ref, o_ref):
    c = c_ref[...]
    s = c / (1.0 + jnp.exp(-c))
    o_ref[0] = _bdot(s, w_ref[0]) + b_ref[0]


def _modulation(cond8, w_mod, b_mod):
    tn = 1536
    return pl.pallas_call(
        _mod_kernel,
        out_shape=jax.ShapeDtypeStruct((DEPTH, 8, 6 * D), F32),
        grid=(DEPTH, 6 * D // tn),
        in_specs=[pl.BlockSpec((8, D), lambda l, j: (0, 0)),
                  pl.BlockSpec((1, D, tn), lambda l, j: (l, 0, j)),
                  pl.BlockSpec((1, 1, tn), lambda l, j: (l, 0, j))],
        out_specs=pl.BlockSpec((1, 8, tn), lambda l, j: (l, 0, j)),
        compiler_params=_cparams("parallel", "parallel"),
        name="modulation",
    )(cond8, w_mod, b_mod.reshape(DEPTH, 1, 6 * D))


def _ln_in_kernel(xc_ref, xl_ref, g_ref, b_ref, o_ref):
    i = pl.program_id(0)

    @pl.when(i < CTX_TILES)
    def _():
        o_ref[...] = _layer_norm(xc_ref[...], g_ref[...], b_ref[...])

    @pl.when(i >= CTX_TILES)
    def _():
        o_ref[...] = _layer_norm(xl_ref[...], g_ref[...], b_ref[...])


def _input_norm(xc, xl, g, b):
    return pl.pallas_call(
        _ln_in_kernel,
        out_shape=jax.ShapeDtypeStruct((T, D), F32),
        grid=(N_TILES,),
        in_specs=[pl.BlockSpec((TM, D), lambda i: (jnp.minimum(i, CTX_TILES - 1), 0)),
                  pl.BlockSpec((TM, D), lambda i: (jnp.maximum(i - CTX_TILES, 0), 0)),
                  _full((1, D)), _full((1, D))],
        out_specs=pl.BlockSpec((TM, D), lambda i: (i, 0)),
        compiler_params=_cparams("parallel"),
        name="input_norm",
    )(xc, xl, g.reshape(1, D), b.reshape(1, D))


def _filter_kernel(z_ref, w1_ref, b1_ref, w2_ref, b2_ref, w3_ref, fr_ref, dec_ref, c_ref, sf_ref,
                   alt_ref, kr_ref, ki_ref, krn_ref, *, L):
    hdot = functools.partial(jnp.dot, preferred_element_type=F32, precision=lax.Precision.HIGHEST)
    h1 = jnp.sin(fr_ref[0:1] * (hdot(z_ref[...], w1_ref[...]) + b1_ref[...]))
    h2 = jnp.sin(fr_ref[1:2] * (hdot(h1, w2_ref[...]) + b2_ref[...]))
    filt = hdot(h2, w3_ref[...])
    row = lax.broadcasted_iota(jnp.int32, (L, HY_W), 0)
    dec = dec_ref[...]
    for o in range(2):
        fwd = filt[:, o * 2 * HY_W:o * 2 * HY_W + HY_W] * dec
        bwd = jnp.where(row == 0, 0.0, filt[:, o * 2 * HY_W + HY_W:(o + 1) * 2 * HY_W] * dec)
        sm = fwd + bwd
        df = fwd - bwd
        kr = _bdot(c_ref[...], sm)
        ki = -_bdot(sf_ref[...], df)
        kr_nyq = jnp.sum(alt_ref[...] * sm, axis=0, keepdims=True)
        kr_ref[o] = jnp.where(row == 0, kr * (0.5 / L), kr * (1.0 / L))
        ki_ref[o] = jnp.where(row == 0, 0.0, ki * (1.0 / L))
        krn_ref[o] = jnp.where(row == 0, kr_nyq * (0.5 / L), kr * (1.0 / L))


def _filter_spectra(L, z, dec, dft, w1, b1, w2, b2, w3, freq):
    c, sf, _, alt = dft
    pad = 128 - HY_FH
    w1p = jnp.pad(w1, ((0, 128 - HY_EMB), (0, pad)))
    w2p = jnp.pad(w2, ((0, pad), (0, pad)))
    w3p = jnp.pad(w3, ((0, pad), (0, 0)))
    b1p = jnp.pad(b1, (0, pad)).reshape(1, 128)
    b2p = jnp.pad(b2, (0, pad)).reshape(1, 128)
    frp = jnp.pad(freq, ((0, 0), (0, pad)))
    shp = jax.ShapeDtypeStruct((2, L, HY_W), F32)
    args = (z, w1p, b1p, w2p, b2p, w3p, frp, dec, c, sf, alt)
    return pl.pallas_call(
        functools.partial(_filter_kernel, L=L),
        out_shape=(shp, shp, shp),
        grid=(1,),
        in_specs=[_full(a.shape) for a in args],
        out_specs=(_full((2, L, HY_W)),) * 3,
        compiler_params=_cparams("arbitrary"),
        name=f"hyena_filter_{L}",
    )(*args)


def _in_kernel(x_ref, mod_ref, w_ref, gg_ref, gb_ref, avg_ref, ws_ref, bs_ref, gq_ref, gkv_ref,
               wuq_ref, wuk_ref, wuv_ref, cq_ref, sq_ref, ck_ref, sk_ref,
               a_ref, hy_ref, q_ref, k_ref, v_ref, ckv_ref, kpe_ref):
    m = mod_ref[0]
    h = x_ref[...] * (1.0 + m[1:2]) + m[0:1]
    proj = _bdot(h, w_ref[...])

    gu = jax.nn.gelu(proj[:, 0:GM_W], approximate=True)
    gv = jax.nn.gelu(proj[:, GM_W:2 * GM_W], approximate=True)
    avg = avg_ref[...]
    mu = _split3_dot(gv, avg)
    vc = gv - mu
    var = _split3_dot(vc * vc, avg)
    vln = (vc * lax.rsqrt(var + LN_EPS) * gg_ref[...] + gb_ref[...]).astype(BF16)
    lane = lax.broadcasted_iota(jnp.int32, (CHUNK, GM_W), 1)
    for c in range(TM // CHUNK):
        vchunk = vln[c * CHUNK:(c + 1) * CHUNK]
        s = bs_ref[...]
        for hd in range(GM_HEADS):
            sh = jnp.dot(ws_ref[hd], vchunk, preferred_element_type=F32)
            s = s + jnp.where(lane // GM_HD == hd, sh, 0.0)
        a_ref[c * CHUNK:(c + 1) * CHUNK, :] = (gu[c * CHUNK:(c + 1) * CHUNK] * s).astype(BF16)

    hy_ref[...] = proj[:, 512:1280]

    cq = _rms_norm(proj[:, 1280:1536], gq_ref[...])
    qq = _bdot(cq, wuq_ref[...])
    cos_q = jnp.concatenate([cq_ref[...]] * HEADS, axis=1)
    sin_q = jnp.concatenate([sq_ref[...]] * HEADS, axis=1)
    q_ref[...] = (qq[:, :HEADS * SLOT] * cos_q + qq[:, HEADS * SLOT:] * sin_q).astype(BF16)

    ckv = _rms_norm(proj[:, 1536:1664], gkv_ref[...])
    ckv_ref[...] = ckv
    kpe = proj[:, 1664:1792]
    kpe_ref[...] = kpe
    krot = kpe * ck_ref[...] + proj[:, 1792:1920] * sk_ref[...]
    kn = _bdot(ckv, wuk_ref[...])
    k_ref[...] = (kn + jnp.concatenate([krot] * HEADS, axis=1)).astype(BF16)
    v_ref[...] = _bdot(ckv, wuv_ref[...]).astype(BF16)


def _in_proj(x, mod_l, w_in_ext, gm_g, gm_b, avg, ws, bs_full, gq, gkv, wuq_arr, wuk_arr, wuv_arr, rope):
    cos_q, sin_q, cos_k, sin_k = rope
    tile = lambda n: pl.BlockSpec((TM, n), lambda i: (i, 0))
    rope_spec = pl.BlockSpec((TM, SLOT), lambda i: (_rope_block_of_tile(i), 0))
    out_shapes = (jax.ShapeDtypeStruct((T, GM_W), BF16),
                  jax.ShapeDtypeStruct((T, 3 * HY_W), F32),
                  jax.ShapeDtypeStruct((T, HEADS * SLOT), BF16),
                  jax.ShapeDtypeStruct((T, HEADS * SLOT), BF16),
                  jax.ShapeDtypeStruct((T, HEADS * VD), BF16),
                  jax.ShapeDtypeStruct((T, KV_LORA), F32),
                  jax.ShapeDtypeStruct((T, SLOT), F32))
    return pl.pallas_call(
        _in_kernel,
        out_shape=out_shapes,
        grid=(N_TILES,),
        in_specs=[tile(D),
                  pl.BlockSpec((1, 6, D), lambda i: (_cond_of_tile(i), 0, 0)),
                  _full((D, IN_EXT)), _full((1, GM_W)), _full((1, GM_W)), _full((GM_W, GM_W)),
                  _full((GM_HEADS, CHUNK, CHUNK)), _full((CHUNK, GM_W)),
                  _full((1, Q_LORA)), _full((1, KV_LORA)),
                  _full((Q_LORA, 2 * HEADS * SLOT)), _full((KV_LORA, HEADS * SLOT)),
                  _full((KV_LORA, HEADS * VD)),
                  rope_spec, rope_spec, rope_spec, rope_spec],
        out_specs=(tile(GM_W), tile(3 * HY_W), tile(HEADS * SLOT), tile(HEADS * SLOT),
                   tile(HEADS * VD), tile(KV_LORA), tile(SLOT)),
        compiler_params=_cparams("parallel"),
        name="in_proj",
    )(x, mod_l, w_in_ext, gm_g, gm_b, avg, ws, bs_full, gq, gkv, wuq_arr, wuk_arr, wuv_arr,
      cos_q, sin_q, cos_k, sin_k)


def _cache_kernel(ckv_ref, kpe_ref, wuk_ref, wuv_ref, k_ref, v_ref):
    ckv = ckv_ref[0, 0]
    kn = _bdot(ckv, wuk_ref[0])
    k_ref[0, 0] = (kn + jnp.concatenate([kpe_ref[0, 0]] * HEADS, axis=1)).astype(BF16)
    v_ref[0, 0] = _bdot(ckv, wuv_ref[0]).astype(BF16)


def _cache_kv(cache_ckv, cache_kpe_slot, wuk_arr, wuv_arr):
    return pl.pallas_call(
        _cache_kernel,
        out_shape=(jax.ShapeDtypeStruct((DEPTH, N_LAT_B, PAST, HEADS * SLOT), BF16),
                   jax.ShapeDtypeStruct((DEPTH, N_LAT_B, PAST, HEADS * VD), BF16)),
        grid=(DEPTH, N_LAT_B),
        in_specs=[pl.BlockSpec((1, 1, PAST, KV_LORA), lambda l, b: (b, l, 0, 0)),
                  pl.BlockSpec((1, 1, PAST, SLOT), lambda l, b: (b, l, 0, 0)),
                  pl.BlockSpec((1, KV_LORA, HEADS * SLOT), lambda l, b: (l, 0, 0)),
                  pl.BlockSpec((1, KV_LORA, HEADS * VD), lambda l, b: (l, 0, 0))],
        out_specs=(pl.BlockSpec((1, 1, PAST, HEADS * SLOT), lambda l, b: (l, b, 0, 0)),
                   pl.BlockSpec((1, 1, PAST, HEADS * VD), lambda l, b: (l, b, 0, 0))),
        compiler_params=_cparams("parallel", "parallel"),
        name="cache_kv",
    )(cache_ckv, cache_kpe_slot, wuk_arr, wuv_arr)


def _attend(q_ref, segments, o_ref):
    lq = q_ref.shape[0]
    lane = lax.broadcasted_iota(jnp.int32, (lq, 2 * VD), 1)
    nt = (((1,), (1,)), ((), ()))
    for pair in range(HEADS // 2):
        outs = []
        for hd in (2 * pair, 2 * pair + 1):
            qh = q_ref[:, hd * SLOT:(hd + 1) * SLOT]
            scores = [lax.dot_general(qh, k_ref[:, hd * SLOT:(hd + 1) * SLOT], nt,
                                      preferred_element_type=F32) for k_ref, _ in segments]
            mx = functools.reduce(jnp.maximum, [jnp.max(s, -1, keepdims=True) for s in scores])
            ps = [jnp.exp(s - mx) for s in scores]
            den = functools.reduce(jnp.add, [jnp.sum(p, -1, keepdims=True) for p in ps])
            acc = functools.reduce(jnp.add, [
                jnp.dot(p.astype(BF16), v_ref[:, pair * 2 * VD:(pair + 1) * 2 * VD],
                        preferred_element_type=F32) for p, (_, v_ref) in zip(ps, segments)])
            outs.append(acc / den)
        o_ref[:, pair * 2 * VD:(pair + 1) * 2 * VD] = jnp.where(lane < VD, outs[0], outs[1]).astype(BF16)


def _attn_ctx_kernel(q_ref, k_ref, v_ref, o_ref):
    _attend(q_ref, [(k_ref, v_ref)], o_ref)


def _attn_lat_kernel(q_ref, kc_ref, vc_ref, k_ref, v_ref, o_ref):
    _attend(q_ref, [(kc_ref.at[0], vc_ref.at[0]), (k_ref, v_ref)], o_ref)


def _attention(q, k, v, kc, vc):
    kw, vw = HEADS * SLOT, HEADS * VD
    ctx = pl.pallas_call(
        _attn_ctx_kernel,
        out_shape=jax.ShapeDtypeStruct((T_CTX, vw), BF16),
        grid=(N_CTX_B,),
        in_specs=[pl.BlockSpec((CTX_L, kw), lambda b: (b, 0)),
                  pl.BlockSpec((CTX_L, kw), lambda b: (b, 0)),
                  pl.BlockSpec((CTX_L, vw), lambda b: (b, 0))],
        out_specs=pl.BlockSpec((CTX_L, vw), lambda b: (b, 0)),
        compiler_params=_cparams("parallel"),
        name="attn_ctx",
    )(q, k, v)
    nq = LAT_L // TM
    off = T_CTX // LAT_L
    lat = pl.pallas_call(
        _attn_lat_kernel,
        out_shape=jax.ShapeDtypeStruct((T_LAT, vw), BF16),
        grid=(N_LAT_B, nq),
        in_specs=[pl.BlockSpec((TM, kw), lambda b, j: (CTX_TILES + b * nq + j, 0)),
                  pl.BlockSpec((1, PAST, kw), lambda b, j: (b, 0, 0)),
                  pl.BlockSpec((1, PAST, vw), lambda b, j: (b, 0, 0)),
                  pl.BlockSpec((LAT_L, kw), lambda b, j: (off + b, 0)),
                  pl.BlockSpec((LAT_L, vw), lambda b, j: (off + b, 0))],
        out_specs=pl.BlockSpec((TM, vw), lambda b, j: (b * nq + j, 0)),
        compiler_params=_cparams("parallel", "parallel"),
        name="attn_lat",
    )(q, kc, vc, k, v)
    return jnp.concatenate([ctx, lat], axis=0)


def _hyena_kernel(hy_ref, cw_ref, cb_ref, c_ref, sf_ref, sb_ref, kr_ref, ki_ref, krn_ref, hb_ref,
                  o_ref, *, L):
    x = hy_ref[...]
    row = lax.broadcasted_iota(jnp.int32, x.shape, 0)
    prev = jnp.where(row == 0, 0.0, pltpu.roll(x, 1, 0))
    nxt = jnp.where(row == L - 1, 0.0, pltpu.roll(x, L - 1, 0))
    z = prev * cw_ref[0:1] + x * cw_ref[1:2] + nxt * cw_ref[2:3] + cb_ref[...]
    y = z[:, 2 * HY_W:]
    for o in range(2):
        gate = z[:, o * HY_W:(o + 1) * HY_W]
        yb = y.astype(BF16)
        a_re = jnp.dot(c_ref[...], yb, preferred_element_type=F32)
        a_im = jnp.dot(sf_ref[...], yb, preferred_element_type=F32)
        ki = ki_ref[o]
        z_re = a_re * kr_ref[o] + a_im * ki
        z_im = a_im * krn_ref[o] - a_re * ki
        conv = (jnp.dot(c_ref[...], z_re.astype(BF16), preferred_element_type=F32)
                + jnp.dot(sb_ref[...], z_im.astype(BF16), preferred_element_type=F32))
        y = gate * (conv + y * hb_ref[o:o + 1])
    o_ref[...] = y.astype(BF16)


def _hyena_group(hy, L, nb, blk0, dft, spectra, conv_w, conv_b, hy_bias):
    c, sf, sb, _ = dft
    kr, ki, krn = spectra
    return pl.pallas_call(
        functools.partial(_hyena_kernel, L=L),
        out_shape=jax.ShapeDtypeStruct((nb * L, HY_W), BF16),
        grid=(nb,),
        in_specs=[pl.BlockSpec((L, 3 * HY_W), lambda b: (blk0 + b, 0)),
                  _full((3, 3 * HY_W)), _full((1, 3 * HY_W)),
                  _full((L, L)), _full((L, L)), _full((L, L)),
                  _full((2, L, HY_W)), _full((2, L, HY_W)), _full((2, L, HY_W)),
                  _full((2, HY_W))],
        out_specs=pl.BlockSpec((L, HY_W), lambda b: (b, 0)),
        compiler_params=_cparams("parallel"),
        name=f"hyena_{L}",
    )(hy, conv_w, conv_b.reshape(1, 3 * HY_W), c, sf, sb, kr, ki, krn, hy_bias)


def _out_kernel(a_ref, b_ref, m_ref, w_ref, x_ref, mod_ref, g_ref, be_ref, wr_ref, br_ref, tri_ref,
                upper_ref, x1_ref, xloc_ref, route_ref, cnt_ref):
    dot = functools.partial(jnp.dot, preferred_element_type=F32)
    mixed = (dot(a_ref[...], w_ref[0:GM_W]) + dot(b_ref[...], w_ref[GM_W:GM_W + HY_W])
             + dot(m_ref[...], w_ref[GM_W + HY_W:]))
    m = mod_ref[0]
    x1 = _layer_norm(ALPHA * x_ref[...] + m[2:3] * mixed, g_ref[...], be_ref[...])
    x1_ref[...] = x1
    h2 = x1 * (1.0 + m[4:5]) + m[3:4]

    logits = jnp.dot(h2, wr_ref[...], preferred_element_type=F32,
                     precision=lax.Precision.HIGHEST) + br_ref[...]
    lane = lax.broadcasted_iota(jnp.int32, logits.shape, 1)
    lanef = lane.astype(F32)
    big = jnp.float32(1e9)
    ninf = jnp.float32(-jnp.inf)
    is_g = lane < N_GROUPS
    gl = jnp.where(is_g, logits, ninf)
    gmax = jnp.max(gl, -1, keepdims=True)
    gidx = jnp.min(jnp.where(gl == gmax, lanef, big), -1, keepdims=True)
    gw = 1.0 / jnp.sum(jnp.where(is_g, jnp.exp(logits - gmax), 0.0), -1, keepdims=True)
    ex = lane - N_GROUPS
    in_group = (ex >= 0) & (ex < N_EXP) & ((ex // EPG).astype(F32) == gidx)
    el = jnp.where(in_group, logits, ninf)
    v1 = jnp.max(el, -1, keepdims=True)
    i1 = jnp.min(jnp.where(el == v1, lanef, big), -1, keepdims=True)
    el2 = jnp.where(lanef == i1, ninf, el)
    v2 = jnp.max(el2, -1, keepdims=True)
    i2 = jnp.min(jnp.where(el2 == v2, lanef, big), -1, keepdims=True)
    e21 = jnp.exp(v2 - v1)
    w1 = gw / (1.0 + e21)
    w2 = gw * e21 / (1.0 + e21)
    e1 = i1 - N_GROUPS
    e2 = i2 - N_GROUPS

    oh1 = jnp.where(lanef == e1, 1.0, 0.0)
    oh2 = jnp.where(lanef == e2, 1.0, 0.0)
    ex1 = dot(tri_ref[...], oh1.astype(BF16))
    ex2 = dot(tri_ref[...], oh2.astype(BF16))
    col1 = jnp.sum(oh1, axis=0, keepdims=True)
    col2 = jnp.sum(oh2, axis=0, keepdims=True)
    n = col1 + col2
    run = jnp.floor((n + 7.0) * 0.125) * 8.0
    start = dot(jnp.broadcast_to(run, (8, 128)).astype(BF16), upper_ref[...])[0:1]
    loc1 = jnp.sum(oh1 * (start + ex1), -1, keepdims=True)
    loc2 = jnp.sum(oh2 * (start + col1 + ex2), -1, keepdims=True)
    cnt_ref[...] = jnp.broadcast_to(n, cnt_ref.shape)

    route = jnp.zeros_like(logits)
    for j, val in enumerate((e1, e2, w1, w2, loc1, loc2)):
        route = jnp.where(lane == j, val, route)
    route_ref[...] = route

    loc1_row = jnp.transpose(jnp.broadcast_to(loc1, (TM, 128)))[0:1].astype(jnp.int32)
    loc2_row = jnp.transpose(jnp.broadcast_to(loc2, (TM, 128)))[0:1].astype(jnp.int32)
    slot = lax.broadcasted_iota(jnp.int32, (LOC, TM), 0)
    perm = jnp.where((slot == loc1_row) | (slot == loc2_row), 1.0, 0.0).astype(BF16)
    xloc_ref[...] = dot(perm, h2.astype(BF16))


def _out_proj(a, hyb, att, w_out, x, mod_l, g, b, w_route, b_route, tri, upper):
    tile = lambda n: pl.BlockSpec((TM, n), lambda i: (i, 0))
    return pl.pallas_call(
        _out_kernel,
        out_shape=(jax.ShapeDtypeStruct((T, D), F32), jax.ShapeDtypeStruct((N_TILES * LOC, D), F32),
                   jax.ShapeDtypeStruct((T, 128), F32), jax.ShapeDtypeStruct((N_TILES * 8, 128), F32)),
        grid=(N_TILES,),
        in_specs=[tile(GM_W), tile(HY_W), tile(HEADS * VD), _full((D, D)), tile(D),
                  pl.BlockSpec((1, 6, D), lambda i: (_cond_of_tile(i), 0, 0)),
                  _full((1, D)), _full((1, D)), _full((D, 128)), _full((1, 128)), _full((TM, TM)),
                  _full((128, 128))],
        out_specs=(tile(D), pl.BlockSpec((LOC, D), lambda i: (i, 0)), tile(128),
                   pl.BlockSpec((8, 128), lambda i: (i, 0))),
        compiler_params=_cparams("parallel"),
        name="out_proj_route",
    )(a, hyb, att, w_out, x, mod_l, g, b, w_route, b_route, tri, upper)


def _expert_kernel(te_ref, tv_ref, src_ref, x_hbm, wg_ref, wu_ref, wd_ref, y_ref, xbuf, sem):
    j = pl.program_id(0)
    slot = j % 2

    def block_copies(t, s):
        return [pltpu.make_async_copy(
            x_hbm.at[pl.ds(pl.multiple_of(src_ref[t * BLK_PER_TILE + b] * BLK, BLK), BLK), :],
            xbuf.at[s, pl.ds(b * BLK, BLK), :], sem.at[s]) for b in range(BLK_PER_TILE)]

    @pl.when(j == 0)
    def _():
        for cp in block_copies(0, 0):
            cp.start()

    @pl.when(j + 1 < MOE_TILES)
    def _():
        for cp in block_copies(j + 1, 1 - slot):
            cp.start()

    for cp in block_copies(j, slot):
        cp.wait()

    @pl.when(tv_ref[j] == 0)
    def _():
        y_ref[...] = jnp.zeros_like(y_ref)

    @pl.when(tv_ref[j] != 0)
    def _():
        xb = xbuf[slot].astype(BF16)
        gate = jnp.dot(xb, wg_ref[0, 0].astype(BF16), preferred_element_type=F32)
        up = jnp.dot(xb, wu_ref[0, 0].astype(BF16), preferred_element_type=F32)
        hid = gate / (1.0 + jnp.exp(-gate)) * up
        y_ref[...] = jnp.dot(hid.astype(BF16), wd_ref[0, 0].astype(BF16), preferred_element_type=F32)


def _experts(l, tile_expert, tile_valid, src_blk, xloc, w_gate, w_up, w_down):
    grid_spec = pltpu.PrefetchScalarGridSpec(
        num_scalar_prefetch=3,
        grid=(MOE_TILES,),
        in_specs=[pl.BlockSpec(memory_space=pl.ANY),
                  pl.BlockSpec((1, 1, D, D_EXP), lambda j, te, tv, src: (l, te[j], 0, 0)),
                  pl.BlockSpec((1, 1, D, D_EXP), lambda j, te, tv, src: (l, te[j], 0, 0)),
                  pl.BlockSpec((1, 1, D_EXP, D), lambda j, te, tv, src: (l, te[j], 0, 0))],
        out_specs=pl.BlockSpec((TM, D), lambda j, te, tv, src: (j, 0)),
        scratch_shapes=[pltpu.VMEM((2, TM, D), F32), pltpu.SemaphoreType.DMA((2,))],
    )
    return pl.pallas_call(
        _expert_kernel,
        out_shape=jax.ShapeDtypeStruct((MOE_ROWS, D), F32),
        grid_spec=grid_spec,
        compiler_params=_cparams("arbitrary"),
        name="experts",
    )(tile_expert, tile_valid, src_blk, xloc, w_gate, w_up, w_down)


def _combine_kernel(nblk_ref, gsrc_ref, y_hbm, route_ref, x1_ref, mod_ref, g_ref, b_ref, *rest, split):
    outs, (ybuf, sem) = rest[:-2], rest[-2:]
    i = pl.program_id(0)

    @pl.when(i == 0)
    def _():
        ybuf[...] = jnp.zeros_like(ybuf)

    def block_copy(lb):
        src = pl.multiple_of(gsrc_ref[i * LOC_BLKS + lb] * BLK, BLK)
        return pltpu.make_async_copy(y_hbm.at[pl.ds(src, BLK), :],
                                     ybuf.at[pl.ds(pl.multiple_of(lb * BLK, BLK), BLK), :], sem)

    def start(lb, c):
        block_copy(lb).start()
        return c

    def wait(lb, c):
        block_copy(lb).wait()
        return c

    lax.fori_loop(0, nblk_ref[i], start, 0)
    lax.fori_loop(0, nblk_ref[i], wait, 0)
    route = route_ref[...]
    yb = ybuf[...].astype(BF16)
    slot = lax.broadcasted_iota(jnp.int32, (TM, LOC), 1)
    pick1 = jnp.where(slot == route[:, 4:5].astype(jnp.int32), 1.0, 0.0).astype(BF16)
    pick2 = jnp.where(slot == route[:, 5:6].astype(jnp.int32), 1.0, 0.0).astype(BF16)
    moe = (route[:, 2:3] * jnp.dot(pick1, yb, preferred_element_type=F32)
           + route[:, 3:4] * jnp.dot(pick2, yb, preferred_element_type=F32))
    m = mod_ref[0]
    res = _layer_norm(ALPHA * x1_ref[...] + m[5:6] * moe, g_ref[...], b_ref[...])
    if not split:
        outs[0][...] = res
    else:
        @pl.when(i < CTX_TILES)
        def _():
            outs[0][...] = res

        @pl.when(i >= CTX_TILES)
        def _():
            outs[1][...] = res


def _combine(nblk, gsrc, y, route, x1, mod_l, g, b, split):
    if split:
        out_shape = (jax.ShapeDtypeStruct((T_CTX, D), F32), jax.ShapeDtypeStruct((T_LAT, D), F32))
        out_specs = (pl.BlockSpec((TM, D), lambda i, nb, gs: (jnp.minimum(i, CTX_TILES - 1), 0)),
                     pl.BlockSpec((TM, D), lambda i, nb, gs: (jnp.maximum(i - CTX_TILES, 0), 0)))
    else:
        out_shape = jax.ShapeDtypeStruct((T, D), F32)
        out_specs = pl.BlockSpec((TM, D), lambda i, nb, gs: (i, 0))
    grid_spec = pltpu.PrefetchScalarGridSpec(
        num_scalar_prefetch=2,
        grid=(N_TILES,),
        in_specs=[pl.BlockSpec(memory_space=pl.ANY),
                  pl.BlockSpec((TM, 128), lambda i, nb, gs: (i, 0)),
                  pl.BlockSpec((TM, D), lambda i, nb, gs: (i, 0)),
                  pl.BlockSpec((1, 6, D), lambda i, nb, gs: (_cond_of_tile(i), 0, 0)),
                  pl.BlockSpec((1, D), lambda i, nb, gs: (0, 0)),
                  pl.BlockSpec((1, D), lambda i, nb, gs: (0, 0))],
        out_specs=out_specs,
        scratch_shapes=[pltpu.VMEM((LOC, D), F32), pltpu.SemaphoreType.DMA(())],
    )
    return pl.pallas_call(
        functools.partial(_combine_kernel, split=split),
        out_shape=out_shape,
        grid_spec=grid_spec,
        compiler_params=_cparams("arbitrary"),
        name="moe_combine",
    )(nblk, gsrc, y, route, x1, mod_l, g, b)


def _prefix_pick(starts, query, table):
    delta = table - jnp.concatenate([jnp.zeros_like(table[..., :1]), table[..., :-1]], axis=-1)
    return jnp.sum(jnp.where(starts <= query, delta, 0), axis=-1)


def _dispatch_tables(cnt):
    i32 = jnp.int32
    run = (cnt + BLK - 1) // BLK * BLK
    loc_start = jnp.cumsum(run, axis=1) - run
    nblk_loc = jnp.sum(run, axis=1) // BLK
    seg_rows = jnp.sum(run, axis=0)
    seg_tiles = (seg_rows + TM - 1) // TM
    tile_end = jnp.cumsum(seg_tiles)
    seg_start = (tile_end - seg_tiles) * TM
    glob_start = seg_start[None, :] + jnp.cumsum(run, axis=0) - run
    total = tile_end[-1]
    jt = jnp.arange(MOE_TILES, dtype=i32)
    tile_expert = jnp.sum(tile_end[None, :] <= jnp.minimum(jt, total - 1)[:, None], axis=1).astype(i32)
    tile_valid = (jt < total).astype(i32)

    g_blk = (glob_start.T.reshape(1, -1)) // BLK
    n_blk = (run.T.reshape(1, -1)) // BLK
    l_blk = ((jnp.arange(N_TILES, dtype=i32) * LOC)[None, :] + loc_start.T).reshape(1, -1) // BLK
    gb = jnp.arange(MOE_ROWS // BLK, dtype=i32)[:, None]
    off = gb[:, 0] - _prefix_pick(g_blk, gb, g_blk)
    src_blk = jnp.where(off < _prefix_pick(g_blk, gb, n_blk), _prefix_pick(g_blk, gb, l_blk) + off,
                        ZERO_BLK).astype(i32)

    pos = (jnp.arange(LOC_BLKS, dtype=i32) * BLK)[None, :, None]
    shift = _prefix_pick(loc_start[:, None, :], pos, (glob_start - loc_start)[:, None, :])
    gsrc = ((pos[:, :, 0] + shift) // BLK).astype(i32)
    return tile_expert, tile_valid, src_blk, nblk_loc.astype(i32), gsrc.reshape(-1)


def _swap_pairs(w):
    return w.reshape(w.shape[:-1] + (ROPE // 2, 2))[..., ::-1].reshape(w.shape)


def _rope_slot(pe):
    return jnp.pad(pe, [(0, 0)] * (pe.ndim - 1) + [(NOPE, SLOT - NOPE - ROPE)])


def _w_in_layout(w):
    pe = w[:, 1664:]
    return jnp.concatenate([w[:, :1664], _rope_slot(pe), _rope_slot(_swap_pairs(pe))], axis=1).astype(BF16)


def _wuq_layout(w):
    w = w.reshape(Q_LORA, HEADS, NOPE + ROPE)
    nope, pe = w[..., :NOPE], w[..., NOPE:]
    plain = jnp.concatenate([nope, jnp.pad(pe, ((0, 0), (0, 0), (0, SLOT - NOPE - ROPE)))], axis=-1)
    swapped = _rope_slot(_swap_pairs(pe))
    return jnp.concatenate([plain.reshape(Q_LORA, HEADS * SLOT),
                            swapped.reshape(Q_LORA, HEADS * SLOT)], axis=1).astype(BF16)


def _wukv_layout(w):
    w = w.reshape(DEPTH, KV_LORA, HEADS, NOPE + VD)
    wk = jnp.pad(w[..., :NOPE], ((0, 0), (0, 0), (0, 0), (0, SLOT - NOPE)))
    return (wk.reshape(DEPTH, KV_LORA, HEADS * SLOT).astype(BF16),
            w[..., NOPE:].reshape(DEPTH, KV_LORA, HEADS * VD).astype(BF16))


def kernel(x_prompt, x_sample, c, cache_ckv, cache_kpe, c_ctx, ln_in_g, ln_in_b, w_mod, b_mod, w_in,
           gm_ln_g, gm_ln_b, gm_ws, gm_bs, hy_conv_w, hy_conv_b, hy_f_w1, hy_f_b1, hy_f_w2, hy_f_b2,
           hy_f_w3, hy_f_freq, hy_bias, mla_gq, mla_gkv, mla_wuq, mla_wukv, w_out, ln1_g, ln1_b,
           ln2_g, ln2_b, moe_w_gr, moe_b_gr, moe_w_er, moe_b_er, moe_w_gate, moe_w_up, moe_w_down):
    rope = _rope_tables()
    dft = {L: _dft_tables(L) for L in (CTX_L, LAT_L)}
    pos_tab = {L: _hyena_positions(L) for L in (CTX_L, LAT_L)}
    hd = np.arange(GM_W) // GM_HD
    avg = jnp.asarray((hd[:, None] == hd[None, :]) / GM_HD, BF16)
    tri = jnp.asarray(np.tril(np.ones((TM, TM)), -1), BF16)
    upper = jnp.asarray(np.triu(np.ones((128, 128)), 1), BF16)

    cond8 = jnp.concatenate([c_ctx[None], c, jnp.zeros((8 - 1 - N_LAT_B, D), F32)], axis=0)
    mod = _modulation(cond8, w_mod, b_mod).reshape(DEPTH, 8, 6, D)

    wuk_all, wuv_all = _wukv_layout(mla_wukv)
    kc_all, vc_all = _cache_kv(cache_ckv, _rope_slot(cache_kpe), wuk_all, wuv_all)

    x = _input_norm(x_prompt.reshape(T_CTX, D), x_sample.reshape(T_LAT, D), ln_in_g, ln_in_b)
    ckv_states, kpe_states = [], []
    for l in range(DEPTH):
        w_in_ext = _w_in_layout(w_in[l])
        wuq_arr = _wuq_layout(mla_wuq[l])
        bs_full = jnp.repeat(gm_bs[l].T, GM_HD, axis=1)
        a, hy, q, k, v, ckv, kpe = _in_proj(
            x, mod[l], w_in_ext, gm_ln_g[l].reshape(1, GM_W), gm_ln_b[l].reshape(1, GM_W), avg,
            gm_ws[l].astype(BF16), bs_full, mla_gq[l].reshape(1, Q_LORA), mla_gkv[l].reshape(1, KV_LORA),
            wuq_arr, wuk_all[l], wuv_all[l], rope)
        ckv_states.append(ckv[:T_CTX].reshape(N_CTX_B, CTX_L, KV_LORA))
        kpe_states.append(kpe[:T_CTX, NOPE:NOPE + ROPE].reshape(N_CTX_B, CTX_L, ROPE))

        hyb = []
        for L, nb, blk0 in ((CTX_L, N_CTX_B, 0), (LAT_L, N_LAT_B, T_CTX // LAT_L)):
            z, dec = pos_tab[L]
            spectra = _filter_spectra(L, z, dec, dft[L], hy_f_w1[l], hy_f_b1[l], hy_f_w2[l],
                                      hy_f_b2[l], hy_f_w3[l], hy_f_freq[l])
            hyb.append(_hyena_group(hy, L, nb, blk0, dft[L], spectra, hy_conv_w[l], hy_conv_b[l],
                                    hy_bias[l]))
        hyb = jnp.concatenate(hyb, axis=0)

        att = _attention(q, k, v, kc_all[l], vc_all[l])

        w_route = jnp.pad(jnp.concatenate([moe_w_gr[l], moe_w_er[l].reshape(D, N_EXP)], axis=1),
                          ((0, 0), (0, 128 - N_GROUPS - N_EXP)))
        b_route = jnp.pad(jnp.concatenate([moe_b_gr[l], moe_b_er[l].reshape(N_EXP)]),
                          (0, 128 - N_GROUPS - N_EXP)).reshape(1, 128)
        x1, xloc, route, counts = _out_proj(a, hyb, att, w_out[l].astype(BF16), x, mod[l],
                                            ln1_g[l].reshape(1, D), ln1_b[l].reshape(1, D),
                                            w_route, b_route, tri, upper)
        cnt = counts.reshape(N_TILES, 8, 128)[:, 0, :N_EXP].astype(jnp.int32)
        tile_expert, tile_valid, src_blk, nblk_loc, gsrc = _dispatch_tables(cnt)
        y = _experts(l, tile_expert, tile_valid, src_blk, xloc, moe_w_gate, moe_w_up, moe_w_down)
        x = _combine(nblk_loc, gsrc, y, route, x1, mod[l], ln2_g[l].reshape(1, D),
                     ln2_b[l].reshape(1, D), split=(l == DEPTH - 1))

    y_prompt = x[0].reshape(N_CTX_B, CTX_L, D)
    y_sample = x[1].reshape(N_LAT_B, LAT_L, D)
    return (y_prompt, y_sample, jnp.stack(ckv_states, axis=1), jnp.stack(kpe_states, axis=1))
```

```python
import functools
import math

import numpy as np
import jax
import jax.numpy as jnp
from jax import lax
from jax.experimental import pallas as pl
from jax.experimental.pallas import tpu as pltpu

D = 1024
N_CTX_B, CTX_L = 16, 256
N_LAT_B, LAT_L = 4, 1024
DEPTH = 2
T_CTX = N_CTX_B * CTX_L
T_LAT = N_LAT_B * LAT_L
T = T_CTX + T_LAT
PAST = 256
GRID_W = 64

GM_HEADS, GM_HD, GM_W, CHUNK = 4, 64, 256, 128
HY_W, HY_EMB, HY_FH = 256, 33, 64
NOPE, ROPE, VD, HEADS = 64, 32, 64, 8
Q_LORA, KV_LORA = 256, 128
SLOT = 128
N_GROUPS, EPG, N_EXP, D_EXP = 4, 8, 32, 256
ALPHA = (2.0 * DEPTH) ** 0.25
LN_EPS, RMS_EPS = 1e-5, 1e-6
ROPE_THETA = 10000.0

TM = 256
N_TILES = T // TM
CTX_TILES = T_CTX // TM
LAT_TILES_PER_B = LAT_L // TM
N_ASSIGN = 2 * T
BLK = 8
BLK_PER_TILE = TM // BLK
LOC = 2 * TM + N_EXP * BLK
LOC_BLKS = LOC // BLK
ZERO_BLK = LOC_BLKS - 1
MOE_TILES = (N_ASSIGN + N_TILES * N_EXP * (BLK - 1)) // TM + N_EXP
MOE_ROWS = MOE_TILES * TM
IN_EXT = 1920

F32, BF16 = jnp.float32, jnp.bfloat16
VMEM_LIMIT = 52 * 1024 * 1024


def _cparams(*sem):
    return pltpu.CompilerParams(dimension_semantics=sem, vmem_limit_bytes=VMEM_LIMIT)


def _cond_of_tile(i):
    return jnp.where(i < CTX_TILES, 0, 1 + (i - CTX_TILES) // LAT_TILES_PER_B)


def _rope_block_of_tile(i):
    return jnp.where(i < CTX_TILES, 0, 1 + (i - CTX_TILES) % LAT_TILES_PER_B)


def _full(shape):
    n = len(shape)
    return pl.BlockSpec(shape, lambda *_: (0,) * n)


def _layer_norm(x, g, b):
    mu = jnp.mean(x, -1, keepdims=True)
    xc = x - mu
    var = jnp.mean(xc * xc, -1, keepdims=True)
    return xc * lax.rsqrt(var + LN_EPS) * g + b


def _rms_norm(x, g):
    return x * lax.rsqrt(jnp.mean(x * x, -1, keepdims=True) + RMS_EPS) * g


def _bdot(a, b):
    return jnp.dot(a.astype(BF16), b.astype(BF16), preferred_element_type=F32)


def _split3_dot(x, w_bf16):
    hi = x.astype(BF16)
    r1 = x - hi.astype(F32)
    mid = r1.astype(BF16)
    lo = (r1 - mid.astype(F32)).astype(BF16)
    dot = functools.partial(jnp.dot, preferred_element_type=F32)
    return dot(hi, w_bf16) + dot(mid, w_bf16) + dot(lo, w_bf16)


def _dft_tables(L):
    f = np.arange(L, dtype=np.int64)
    ft = np.outer(f, f) % (2 * L)
    ang = np.pi * ft / L
    c = np.cos(ang)
    s = np.sin(ang)
    alt = np.where(f % 2 == 0, 1.0, -1.0)
    sf = s.copy()
    sf[0, :] = alt
    return (jnp.asarray(c, F32).astype(BF16), jnp.asarray(sf, F32).astype(BF16),
            jnp.asarray(sf.T.copy(), F32).astype(BF16), jnp.asarray(alt[:, None], F32))


def _rope_tables():
    rows = LAT_L // GRID_W
    row = np.repeat(np.arange(rows, dtype=np.float64), GRID_W)
    col = np.tile(np.arange(GRID_W, dtype=np.float64), rows)
    n = ROPE // 4
    inv = ROPE_THETA ** (-np.arange(n, dtype=np.float64) / n)
    ang = np.concatenate([row[:, None] * inv, col[:, None] * inv], -1)
    cos = np.zeros((CTX_L + LAT_L, SLOT))
    sin = np.zeros((CTX_L + LAT_L, SLOT))
    cos[:, :NOPE + ROPE] = 1.0
    cos[CTX_L:, NOPE:NOPE + ROPE:2] = np.cos(ang)
    cos[CTX_L:, NOPE + 1:NOPE + ROPE:2] = np.cos(ang)
    sin[CTX_L:, NOPE:NOPE + ROPE:2] = -np.sin(ang)
    sin[CTX_L:, NOPE + 1:NOPE + ROPE:2] = np.sin(ang)
    scale = 1.0 / math.sqrt(NOPE + ROPE)
    return (jnp.asarray(cos * scale, F32), jnp.asarray(sin * scale, F32),
            jnp.asarray(cos, F32), jnp.asarray(sin, F32))


def _hyena_positions(L):
    t = jnp.linspace(0.0, 1.0, L, dtype=F32)[:, None]
    bands = (HY_EMB - 1) // 2
    w = 2.0 * math.pi * jnp.arange(L, dtype=F32)[:, None] / L
    f = jnp.linspace(1e-4, bands - 1, bands, dtype=F32)[None]
    z = jnp.concatenate([t, jnp.cos(f * w), -jnp.sin(f * w)], -1)
    z = jnp.pad(z, ((0, 0), (0, 128 - HY_EMB)))
    min_decay = math.log(1e-2) / 1.5
    max_decay = math.log(1e-2) / 0.3
    deltas = jnp.abs(jnp.linspace(min_decay, max_decay, HY_W, dtype=F32))
    return z, jnp.exp(-t * deltas)


def _mod_kernel(c_ref, w_ref, b_ref, o_ref):
    c = c_ref[...]
    s = c / (1.0 + jnp.exp(-c))
    o_ref[0] = _bdot(s, w_ref[0]) + b_ref[0]


def _modulation(cond8, w_mod, b_mod):
    tn = 1536
    return pl.pallas_call(
        _mod_kernel,
        out_shape=jax.ShapeDtypeStruct((DEPTH, 8, 6 * D), F32),
        grid=(DEPTH, 6 * D // tn),
        in_specs=[pl.BlockSpec((8, D), lambda l, j: (0, 0)),
                  pl.BlockSpec((1, D, tn), lambda l, j: (l, 0, j)),
                  pl.BlockSpec((1, 1, tn), lambda l, j: (l, 0, j))],
        out_specs=pl.BlockSpec((1, 8, tn), lambda l, j: (l, 0, j)),
        compiler_params=_cparams("parallel", "parallel"),
        name="modulation",
    )(cond8, w_mod, b_mod.reshape(DEPTH, 1, 6 * D))


def _ln_in_kernel(xc_ref, xl_ref, g_ref, b_ref, o_ref):
    i = pl.program_id(0)

    @pl.when(i < CTX_TILES)
    def _():
        o_ref[...] = _layer_norm(xc_ref[...], g_ref[...], b_ref[...])

    @pl.when(i >= CTX_TILES)
    def _():
        o_ref[...] = _layer_norm(xl_ref[...], g_ref[...], b_ref[...])


def _input_norm(xc, xl, g, b):
    return pl.pallas_call(
        _ln_in_kernel,
        out_shape=jax.ShapeDtypeStruct((T, D), F32),
        grid=(N_TILES,),
        in_specs=[pl.BlockSpec((TM, D), lambda i: (jnp.minimum(i, CTX_TILES - 1), 0)),
                  pl.BlockSpec((TM, D), lambda i: (jnp.maximum(i - CTX_TILES, 0), 0)),
                  _full((1, D)), _full((1, D))],
        out_specs=pl.BlockSpec((TM, D), lambda i: (i, 0)),
        compiler_params=_cparams("parallel"),
        name="input_norm",
    )(xc, xl, g.reshape(1, D), b.reshape(1, D))


def _filter_kernel(z_ref, w1_ref, b1_ref, w2_ref, b2_ref, w3_ref, fr_ref, dec_ref, c_ref, sf_ref,
                   alt_ref, kr_ref, ki_ref, krn_ref, *, L):
    hdot = functools.partial(jnp.dot, preferred_element_type=F32, precision=lax.Precision.HIGHEST)
    h1 = jnp.sin(fr_ref[0:1] * (hdot(z_ref[...], w1_ref[...]) + b1_ref[...]))
    h2 = jnp.sin(fr_ref[1:2] * (hdot(h1, w2_ref[...]) + b2_ref[...]))
    filt = hdot(h2, w3_ref[...])
    row = lax.broadcasted_iota(jnp.int32, (L, HY_W), 0)
    dec = dec_ref[...]
    for o in range(2):
        fwd = filt[:, o * 2 * HY_W:o * 2 * HY_W + HY_W] * dec
        bwd = jnp.where(row == 0, 0.0, filt[:, o * 2 * HY_W + HY_W:(o + 1) * 2 * HY_W] * dec)
        sm = fwd + bwd
        df = fwd - bwd
        kr = _bdot(c_ref[...], sm)
        ki = -_bdot(sf_ref[...], df)
        kr_nyq = jnp.sum(alt_ref[...] * sm, axis=0, keepdims=True)
        kr_ref[o] = jnp.where(row == 0, kr * (0.5 / L), kr * (1.0 / L))
        ki_ref[o] = jnp.where(row == 0, 0.0, ki * (1.0 / L))
        krn_ref[o] = jnp.where(row == 0, kr_nyq * (0.5 / L), kr * (1.0 / L))


def _filter_spectra(L, z, dec, dft, w1, b1, w2, b2, w3, freq):
    c, sf, _, alt = dft
    pad = 128 - HY_FH
    w1p = jnp.pad(w1, ((0, 128 - HY_EMB), (0, pad)))
    w2p = jnp.pad(w2, ((0, pad), (0, pad)))
    w3p = jnp.pad(w3, ((0, pad), (0, 0)))
    b1p = jnp.pad(b1, (0, pad)).reshape(1, 128)
    b2p = jnp.pad(b2, (0, pad)).reshape(1, 128)
    frp = jnp.pad(freq, ((0, 0), (0, pad)))
    shp = jax.ShapeDtypeStruct((2, L, HY_W), F32)
    args = (z, w1p, b1p, w2p, b2p, w3p, frp, dec, c, sf, alt)
    return pl.pallas_call(
        functools.partial(_filter_kernel, L=L),
        out_shape=(shp, shp, shp),
        grid=(1,),
        in_specs=[_full(a.shape) for a in args],
        out_specs=(_full((2, L, HY_W)),) * 3,
        compiler_params=_cparams("arbitrary"),
        name=f"hyena_filter_{L}",
    )(*args)


def _in_kernel(x_ref, mod_ref, w_ref, gg_ref, gb_ref, avg_ref, ws_ref, bs_ref, gq_ref, gkv_ref,
               wuq_ref, wuk_ref, wuv_ref, cq_ref, sq_ref, ck_ref, sk_ref,
               a_ref, hy_ref, q_ref, k_ref, v_ref, ckv_ref, kpe_ref):
    m = mod_ref[0]
    h = x_ref[...] * (1.0 + m[1:2]) + m[0:1]
    proj = _bdot(h, w_ref[...])

    gu = jax.nn.gelu(proj[:, 0:GM_W], approximate=True)
    gv = jax.nn.gelu(proj[:, GM_W:2 * GM_W], approximate=True)
    avg = avg_ref[...]
    mu = _split3_dot(gv, avg)
    vc = gv - mu
    var = _split3_dot(vc * vc, avg)
    vln = (vc * lax.rsqrt(var + LN_EPS) * gg_ref[...] + gb_ref[...]).astype(BF16)
    lane = lax.broadcasted_iota(jnp.int32, (CHUNK, GM_W), 1)
    for c in range(TM // CHUNK):
        vchunk = vln[c * CHUNK:(c + 1) * CHUNK]
        s = bs_ref[...]
        for hd in range(GM_HEADS):
            sh = jnp.dot(ws_ref[hd], vchunk, preferred_element_type=F32)
            s = s + jnp.where(lane // GM_HD == hd, sh, 0.0)
        a_ref[c * CHUNK:(c + 1) * CHUNK, :] = (gu[c * CHUNK:(c + 1) * CHUNK] * s).astype(BF16)

    hy_ref[...] = proj[:, 512:1280]

    cq = _rms_norm(proj[:, 1280:1536], gq_ref[...])
    qq = _bdot(cq, wuq_ref[...])
    cos_q = jnp.concatenate([cq_ref[...]] * HEADS, axis=1)
    sin_q = jnp.concatenate([sq_ref[...]] * HEADS, axis=1)
    q_ref[...] = (qq[:, :HEADS * SLOT] * cos_q + qq[:, HEADS * SLOT:] * sin_q).astype(BF16)

    ckv = _rms_norm(proj[:, 1536:1664], gkv_ref[...])
    ckv_ref[...] = ckv
    kpe = proj[:, 1664:1792]
    kpe_ref[...] = kpe
    krot = kpe * ck_ref[...] + proj[:, 1792:1920] * sk_ref[...]
    kn = _bdot(ckv, wuk_ref[...])
    k_ref[...] = (kn + jnp.concatenate([krot] * HEADS, axis=1)).astype(BF16)
    v_ref[...] = _bdot(ckv, wuv_ref[...]).astype(BF16)


def _in_proj(x, mod_l, w_in_ext, gm_g, gm_b, avg, ws, bs_full, gq, gkv, wuq_arr, wuk_arr, wuv_arr, rope):
    cos_q, sin_q, cos_k, sin_k = rope
    tile = lambda n: pl.BlockSpec((TM, n), lambda i: (i, 0))
    rope_spec = pl.BlockSpec((TM, SLOT), lambda i: (_rope_block_of_tile(i), 0))
    out_shapes = (jax.ShapeDtypeStruct((T, GM_W), BF16),
                  jax.ShapeDtypeStruct((T, 3 * HY_W), F32),
                  jax.ShapeDtypeStruct((T, HEADS * SLOT), BF16),
                  jax.ShapeDtypeStruct((T, HEADS * SLOT), BF16),
                  jax.ShapeDtypeStruct((T, HEADS * VD), BF16),
                  jax.ShapeDtypeStruct((T, KV_LORA), F32),
                  jax.ShapeDtypeStruct((T, SLOT), F32))
    return pl.pallas_call(
        _in_kernel,
        out_shape=out_shapes,
        grid=(N_TILES,),
        in_specs=[tile(D),
                  pl.BlockSpec((1, 6, D), lambda i: (_cond_of_tile(i), 0, 0)),
                  _full((D, IN_EXT)), _full((1, GM_W)), _full((1, GM_W)), _full((GM_W, GM_W)),
                  _full((GM_HEADS, CHUNK, CHUNK)), _full((CHUNK, GM_W)),
                  _full((1, Q_LORA)), _full((1, KV_LORA)),
                  _full((Q_LORA, 2 * HEADS * SLOT)), _full((KV_LORA, HEADS * SLOT)),
                  _full((KV_LORA, HEADS * VD)),
                  rope_spec, rope_spec, rope_spec, rope_spec],
        out_specs=(tile(GM_W), tile(3 * HY_W), tile(HEADS * SLOT), tile(HEADS * SLOT),
                   tile(HEADS * VD), tile(KV_LORA), tile(SLOT)),
        compiler_params=_cparams("parallel"),
        name="in_proj",
    )(x, mod_l, w_in_ext, gm_g, gm_b, avg, ws, bs_full, gq, gkv, wuq_arr, wuk_arr, wuv_arr,
      cos_q, sin_q, cos_k, sin_k)


def _cache_kernel(ckv_ref, kpe_ref, wuk_ref, wuv_ref, k_ref, v_ref):
    ckv = ckv_ref[0, 0]
    kn = _bdot(ckv, wuk_ref[0])
    k_ref[0, 0] = (kn + jnp.concatenate([kpe_ref[0, 0]] * HEADS, axis=1)).astype(BF16)
    v_ref[0, 0] = _bdot(ckv, wuv_ref[0]).astype(BF16)


def _cache_kv(cache_ckv, cache_kpe_slot, wuk_arr, wuv_arr):
    return pl.pallas_call(
        _cache_kernel,
        out_shape=(jax.ShapeDtypeStruct((DEPTH, N_LAT_B, PAST, HEADS * SLOT), BF16),
                   jax.ShapeDtypeStruct((DEPTH, N_LAT_B, PAST, HEADS * VD), BF16)),
        grid=(DEPTH, N_LAT_B),
        in_specs=[pl.BlockSpec((1, 1, PAST, KV_LORA), lambda l, b: (b, l, 0, 0)),
                  pl.BlockSpec((1, 1, PAST, SLOT), lambda l, b: (b, l, 0, 0)),
                  pl.BlockSpec((1, KV_LORA, HEADS * SLOT), lambda l, b: (l, 0, 0)),
                  pl.BlockSpec((1, KV_LORA, HEADS * VD), lambda l, b: (l, 0, 0))],
        out_specs=(pl.BlockSpec((1, 1, PAST, HEADS * SLOT), lambda l, b: (l, b, 0, 0)),
                   pl.BlockSpec((1, 1, PAST, HEADS * VD), lambda l, b: (l, b, 0, 0))),
        compiler_params=_cparams("parallel", "parallel"),
        name="cache_kv",
    )(cache_ckv, cache_kpe_slot, wuk_arr, wuv_arr)


def _attend(q_ref, segments, o_ref):
    lq = q_ref.shape[0]
    lane = lax.broadcasted_iota(jnp.int32, (lq, 2 * VD), 1)
    nt = (((1,), (1,)), ((), ()))
    for pair in range(HEADS // 2):
        outs = []
        for hd in (2 * pair, 2 * pair + 1):
            qh = q_ref[:, hd * SLOT:(hd + 1) * SLOT]
            scores = [lax.dot_general(qh, k_ref[:, hd * SLOT:(hd + 1) * SLOT], nt,
                                      preferred_element_type=F32) for k_ref, _ in segments]
            mx = functools.reduce(jnp.maximum, [jnp.max(s, -1, keepdims=True) for s in scores])
            ps = [jnp.exp(s - mx) for s in scores]
            den = functools.reduce(jnp.add, [jnp.sum(p, -1, keepdims=True) for p in ps])
            acc = functools.reduce(jnp.add, [
                jnp.dot(p.astype(BF16), v_ref[:, pair * 2 * VD:(pair + 1) * 2 * VD],
                        preferred_element_type=F32) for p, (_, v_ref) in zip(ps, segments)])
            outs.append(acc / den)
        o_ref[:, pair * 2 * VD:(pair + 1) * 2 * VD] = jnp.where(lane < VD, outs[0], outs[1]).astype(BF16)


def _attn_ctx_kernel(q_ref, k_ref, v_ref, o_ref):
    _attend(q_ref, [(k_ref, v_ref)], o_ref)


def _attn_lat_kernel(q_ref, kc_ref, vc_ref, k_ref, v_ref, o_ref):
    _attend(q_ref, [(kc_ref.at[0], vc_ref.at[0]), (k_ref, v_ref)], o_ref)


def _attention(q, k, v, kc, vc):
    kw, vw = HEADS * SLOT, HEADS * VD
    ctx = pl.pallas_call(
        _attn_ctx_kernel,
        out_shape=jax.ShapeDtypeStruct((T_CTX, vw), BF16),
        grid=(N_CTX_B,),
        in_specs=[pl.BlockSpec((CTX_L, kw), lambda b: (b, 0)),
                  pl.BlockSpec((CTX_L, kw), lambda b: (b, 0)),
                  pl.BlockSpec((CTX_L, vw), lambda b: (b, 0))],
        out_specs=pl.BlockSpec((CTX_L, vw), lambda b: (b, 0)),
        compiler_params=_cparams("parallel"),
        name="attn_ctx",
    )(q, k, v)
    nq = LAT_L // TM
    off = T_CTX // LAT_L
    lat = pl.pallas_call(
        _attn_lat_kernel,
        out_shape=jax.ShapeDtypeStruct((T_LAT, vw), BF16),
        grid=(N_LAT_B, nq),
        in_specs=[pl.BlockSpec((TM, kw), lambda b, j: (CTX_TILES + b * nq + j, 0)),
                  pl.BlockSpec((1, PAST, kw), lambda b, j: (b, 0, 0)),
                  pl.BlockSpec((1, PAST, vw), lambda b, j: (b, 0, 0)),
                  pl.BlockSpec((LAT_L, kw), lambda b, j: (off + b, 0)),
                  pl.BlockSpec((LAT_L, vw), lambda b, j: (off + b, 0))],
        out_specs=pl.BlockSpec((TM, vw), lambda b, j: (b * nq + j, 0)),
        compiler_params=_cparams("parallel", "parallel"),
        name="attn_lat",
    )(q, kc, vc, k, v)
    return jnp.concatenate([ctx, lat], axis=0)


def _hyena_kernel(hy_ref, cw_ref, cb_ref, c_ref, sf_ref, sb_ref, kr_ref, ki_ref, krn_ref, hb_ref,
                  o_ref, *, L):
    x = hy_ref[...]
    row = lax.broadcasted_iota(jnp.int32, x.shape, 0)
    prev = jnp.where(row == 0, 0.0, pltpu.roll(x, 1, 0))
    nxt = jnp.where(row == L - 1, 0.0, pltpu.roll(x, L - 1, 0))
    z = prev * cw_ref[0:1] + x * cw_ref[1:2] + nxt * cw_ref[2:3] + cb_ref[...]
    y = z[:, 2 * HY_W:]
    for o in range(2):
        gate = z[:, o * HY_W:(o + 1) * HY_W]
        yb = y.astype(BF16)
        a_re = jnp.dot(c_ref[...], yb, preferred_element_type=F32)
        a_im = jnp.dot(sf_ref[...], yb, preferred_element_type=F32)
        ki = ki_ref[o]
        z_re = a_re * kr_ref[o] + a_im * ki
        z_im = a_im * krn_ref[o] - a_re * ki
        conv = (jnp.dot(c_ref[...], z_re.astype(BF16), preferred_element_type=F32)
                + jnp.dot(sb_ref[...], z_im.astype(BF16), preferred_element_type=F32))
        y = gate * (conv + y * hb_ref[o:o + 1])
    o_ref[...] = y.astype(BF16)


def _hyena_group(hy, L, nb, blk0, dft, spectra, conv_w, conv_b, hy_bias):
    c, sf, sb, _ = dft
    kr, ki, krn = spectra
    return pl.pallas_call(
        functools.partial(_hyena_kernel, L=L),
        out_shape=jax.ShapeDtypeStruct((nb * L, HY_W), BF16),
        grid=(nb,),
        in_specs=[pl.BlockSpec((L, 3 * HY_W), lambda b: (blk0 + b, 0)),
                  _full((3, 3 * HY_W)), _full((1, 3 * HY_W)),
                  _full((L, L)), _full((L, L)), _full((L, L)),
                  _full((2, L, HY_W)), _full((2, L, HY_W)), _full((2, L, HY_W)),
                  _full((2, HY_W))],
        out_specs=pl.BlockSpec((L, HY_W), lambda b: (b, 0)),
        compiler_params=_cparams("parallel"),
        name=f"hyena_{L}",
    )(hy, conv_w, conv_b.reshape(1, 3 * HY_W), c, sf, sb, kr, ki, krn, hy_bias)


def _out_kernel(a_ref, b_ref, m_ref, w_ref, x_ref, mod_ref, g_ref, be_ref, wr_ref, br_ref, tri_ref,
                upper_ref, x1_ref, xloc_ref, route_ref, cnt_ref):
    dot = functools.partial(jnp.dot, preferred_element_type=F32)
    mixed = (dot(a_ref[...], w_ref[0:GM_W]) + dot(b_ref[...], w_ref[GM_W:GM_W + HY_W])
             + dot(m_ref[...], w_ref[GM_W + HY_W:]))
    m = mod_ref[0]
    x1 = _layer_norm(ALPHA * x_ref[...] + m[2:3] * mixed, g_ref[...], be_ref[...])
    x1_ref[...] = x1
    h2 = x1 * (1.0 + m[4:5]) + m[3:4]

    h_hi = h2.astype(BF16)
    h_lo = (h2 - h_hi.astype(F32)).astype(BF16)
    w_hi = wr_ref[...].astype(BF16)
    w_lo = (wr_ref[...] - w_hi.astype(F32)).astype(BF16)
    part = dot(h_hi, jnp.concatenate([w_hi, w_lo], axis=1))
    logits = part[:, :128] + part[:, 128:] + dot(h_lo, w_hi) + br_ref[...]
    lane = lax.broadcasted_iota(jnp.int32, logits.shape, 1)
    lanef = lane.astype(F32)
    big = jnp.float32(1e9)
    ninf = jnp.float32(-jnp.inf)
    is_g = lane < N_GROUPS
    gl = jnp.where(is_g, logits, ninf)
    gmax = jnp.max(gl, -1, keepdims=True)
    gidx = jnp.min(jnp.where(gl == gmax, lanef, big), -1, keepdims=True)
    gw = 1.0 / jnp.sum(jnp.where(is_g, jnp.exp(logits - gmax), 0.0), -1, keepdims=True)
    ex = lane - N_GROUPS
    in_group = (ex >= 0) & (ex < N_EXP) & ((ex // EPG).astype(F32) == gidx)
    el = jnp.where(in_group, logits, ninf)
    v1 = jnp.max(el, -1, keepdims=True)
    i1 = jnp.min(jnp.where(el == v1, lanef, big), -1, keepdims=True)
    el2 = jnp.where(lanef == i1, ninf, el)
    v2 = jnp.max(el2, -1, keepdims=True)
    i2 = jnp.min(jnp.where(el2 == v2, lanef, big), -1, keepdims=True)
    e21 = jnp.exp(v2 - v1)
    w1 = gw / (1.0 + e21)
    w2 = gw * e21 / (1.0 + e21)
    e1 = i1 - N_GROUPS
    e2 = i2 - N_GROUPS

    oh1 = jnp.where(lanef == e1, 1.0, 0.0)
    oh2 = jnp.where(lanef == e2, 1.0, 0.0)
    ex1 = dot(tri_ref[...], oh1.astype(BF16))
    ex2 = dot(tri_ref[...], oh2.astype(BF16))
    col1 = jnp.sum(oh1, axis=0, keepdims=True)
    col2 = jnp.sum(oh2, axis=0, keepdims=True)
    n = col1 + col2
    run = jnp.floor((n + 7.0) * 0.125) * 8.0
    start = dot(jnp.broadcast_to(run, (8, 128)).astype(BF16), upper_ref[...])[0:1]
    loc1 = jnp.sum(oh1 * (start + ex1), -1, keepdims=True)
    loc2 = jnp.sum(oh2 * (start + col1 + ex2), -1, keepdims=True)
    cnt_ref[...] = jnp.broadcast_to(n, cnt_ref.shape)

    route = jnp.zeros_like(logits)
    for j, val in enumerate((e1, e2, w1, w2, loc1, loc2)):
        route = jnp.where(lane == j, val, route)
    route_ref[...] = route

    loc1_row = jnp.transpose(jnp.broadcast_to(loc1, (TM, 128)))[0:1].astype(jnp.int32)
    loc2_row = jnp.transpose(jnp.broadcast_to(loc2, (TM, 128)))[0:1].astype(jnp.int32)
    slot = lax.broadcasted_iota(jnp.int32, (LOC, TM), 0)
    perm = jnp.where((slot == loc1_row) | (slot == loc2_row), 1.0, 0.0).astype(BF16)
    xloc_ref[...] = dot(perm, h2.astype(BF16))


def _out_proj(a, hyb, att, w_out, x, mod_l, g, b, w_route, b_route, tri, upper):
    tile = lambda n: pl.BlockSpec((TM, n), lambda i: (i, 0))
    return pl.pallas_call(
        _out_kernel,
        out_shape=(jax.ShapeDtypeStruct((T, D), F32), jax.ShapeDtypeStruct((N_TILES * LOC, D), F32),
                   jax.ShapeDtypeStruct((T, 128), F32), jax.ShapeDtypeStruct((N_TILES * 8, 128), F32)),
        grid=(N_TILES,),
        in_specs=[tile(GM_W), tile(HY_W), tile(HEADS * VD), _full((D, D)), tile(D),
                  pl.BlockSpec((1, 6, D), lambda i: (_cond_of_tile(i), 0, 0)),
                  _full((1, D)), _full((1, D)), _full((D, 128)), _full((1, 128)), _full((TM, TM)),
                  _full((128, 128))],
        out_specs=(tile(D), pl.BlockSpec((LOC, D), lambda i: (i, 0)), tile(128),
                   pl.BlockSpec((8, 128), lambda i: (i, 0))),
        compiler_params=_cparams("parallel"),
        name="out_proj_route",
    )(a, hyb, att, w_out, x, mod_l, g, b, w_route, b_route, tri, upper)


def _expert_kernel(te_ref, tv_ref, src_ref, x_hbm, wg_ref, wu_ref, wd_ref, y_ref, xbuf, sem):
    j = pl.program_id(0)
    slot = j % 2

    def block_copies(t, s):
        return [pltpu.make_async_copy(
            x_hbm.at[pl.ds(pl.multiple_of(src_ref[t * BLK_PER_TILE + b] * BLK, BLK), BLK), :],
            xbuf.at[s, pl.ds(b * BLK, BLK), :], sem.at[s]) for b in range(BLK_PER_TILE)]

    @pl.when(j == 0)
    def _():
        for cp in block_copies(0, 0):
            cp.start()

    @pl.when(jnp.logical_and(j + 1 < MOE_TILES, tv_ref[jnp.minimum(j + 1, MOE_TILES - 1)] != 0))
    def _():
        for cp in block_copies(j + 1, 1 - slot):
            cp.start()

    @pl.when(tv_ref[j] == 0)
    def _():
        y_ref[...] = jnp.zeros_like(y_ref)

    @pl.when(tv_ref[j] != 0)
    def _():
        for cp in block_copies(j, slot):
            cp.wait()
        xb = xbuf[slot].astype(BF16)
        gate = jnp.dot(xb, wg_ref[0, 0].astype(BF16), preferred_element_type=F32)
        up = jnp.dot(xb, wu_ref[0, 0].astype(BF16), preferred_element_type=F32)
        hid = gate / (1.0 + jnp.exp(-gate)) * up
        y_ref[...] = jnp.dot(hid.astype(BF16), wd_ref[0, 0].astype(BF16), preferred_element_type=F32)


def _experts(l, tile_expert, tile_valid, src_blk, xloc, w_gate, w_up, w_down):
    grid_spec = pltpu.PrefetchScalarGridSpec(
        num_scalar_prefetch=3,
        grid=(MOE_TILES,),
        in_specs=[pl.BlockSpec(memory_space=pl.ANY),
                  pl.BlockSpec((1, 1, D, D_EXP), lambda j, te, tv, src: (l, te[j], 0, 0)),
                  pl.BlockSpec((1, 1, D, D_EXP), lambda j, te, tv, src: (l, te[j], 0, 0)),
                  pl.BlockSpec((1, 1, D_EXP, D), lambda j, te, tv, src: (l, te[j], 0, 0))],
        out_specs=pl.BlockSpec((TM, D), lambda j, te, tv, src: (j, 0)),
        scratch_shapes=[pltpu.VMEM((2, TM, D), F32), pltpu.SemaphoreType.DMA((2,))],
    )
    return pl.pallas_call(
        _expert_kernel,
        out_shape=jax.ShapeDtypeStruct((MOE_ROWS, D), F32),
        grid_spec=grid_spec,
        compiler_params=_cparams("arbitrary"),
        name="experts",
    )(tile_expert, tile_valid, src_blk, xloc, w_gate, w_up, w_down)


def _combine_kernel(nblk_ref, gsrc_ref, y_hbm, route_ref, x1_ref, mod_ref, g_ref, b_ref, *rest, split):
    outs, (ybuf, sem) = rest[:-2], rest[-2:]
    i = pl.program_id(0)

    slot_i = i % 2

    def block_copy(t, s, lb):
        src = pl.multiple_of(gsrc_ref[t * LOC_BLKS + lb] * BLK, BLK)
        return pltpu.make_async_copy(y_hbm.at[pl.ds(src, BLK), :],
                                     ybuf.at[s, pl.ds(pl.multiple_of(lb * BLK, BLK), BLK), :], sem.at[s])

    def fetch(t, s):
        def start(lb, c):
            block_copy(t, s, lb).start()
            return c
        lax.fori_loop(0, nblk_ref[t], start, 0)

    @pl.when(i == 0)
    def _():
        ybuf[...] = jnp.zeros_like(ybuf)
        fetch(0, 0)

    @pl.when(i + 1 < N_TILES)
    def _():
        fetch(jnp.minimum(i + 1, N_TILES - 1), 1 - slot_i)

    def wait(lb, c):
        block_copy(i, slot_i, lb).wait()
        return c

    lax.fori_loop(0, nblk_ref[i], wait, 0)
    route = route_ref[...]
    yb = ybuf[slot_i].astype(BF16)
    slot = lax.broadcasted_iota(jnp.int32, (TM, LOC), 1)
    pick1 = jnp.where(slot == route[:, 4:5].astype(jnp.int32), 1.0, 0.0).astype(BF16)
    pick2 = jnp.where(slot == route[:, 5:6].astype(jnp.int32), 1.0, 0.0).astype(BF16)
    moe = (route[:, 2:3] * jnp.dot(pick1, yb, preferred_element_type=F32)
           + route[:, 3:4] * jnp.dot(pick2, yb, preferred_element_type=F32))
    m = mod_ref[0]
    res = _layer_norm(ALPHA * x1_ref[...] + m[5:6] * moe, g_ref[...], b_ref[...])
    if not split:
        outs[0][...] = res
    else:
        @pl.when(i < CTX_TILES)
        def _():
            outs[0][...] = res

        @pl.when(i >= CTX_TILES)
        def _():
            outs[1][...] = res


def _combine(nblk, gsrc, y, route, x1, mod_l, g, b, split):
    if split:
        out_shape = (jax.ShapeDtypeStruct((T_CTX, D), F32), jax.ShapeDtypeStruct((T_LAT, D), F32))
        out_specs = (pl.BlockSpec((TM, D), lambda i, nb, gs: (jnp.minimum(i, CTX_TILES - 1), 0)),
                     pl.BlockSpec((TM, D), lambda i, nb, gs: (jnp.maximum(i - CTX_TILES, 0), 0)))
    else:
        out_shape = jax.ShapeDtypeStruct((T, D), F32)
        out_specs = pl.BlockSpec((TM, D), lambda i, nb, gs: (i, 0))
    grid_spec = pltpu.PrefetchScalarGridSpec(
        num_scalar_prefetch=2,
        grid=(N_TILES,),
        in_specs=[pl.BlockSpec(memory_space=pl.ANY),
                  pl.BlockSpec((TM, 128), lambda i, nb, gs: (i, 0)),
                  pl.BlockSpec((TM, D), lambda i, nb, gs: (i, 0)),
                  pl.BlockSpec((1, 6, D), lambda i, nb, gs: (_cond_of_tile(i), 0, 0)),
                  pl.BlockSpec((1, D), lambda i, nb, gs: (0, 0)),
                  pl.BlockSpec((1, D), lambda i, nb, gs: (0, 0))],
        out_specs=out_specs,
        scratch_shapes=[pltpu.VMEM((2, LOC, D), F32), pltpu.SemaphoreType.DMA((2,))],
    )
    return pl.pallas_call(
        functools.partial(_combine_kernel, split=split),
        out_shape=out_shape,
        grid_spec=grid_spec,
        compiler_params=_cparams("arbitrary"),
        name="moe_combine",
    )(nblk, gsrc, y, route, x1, mod_l, g, b)


def _prefix_pick(starts, query, table):
    delta = table - jnp.concatenate([jnp.zeros_like(table[..., :1]), table[..., :-1]], axis=-1)
    return jnp.sum(jnp.where(starts <= query, delta, 0), axis=-1)


def _dispatch_tables(cnt):
    i32 = jnp.int32
    run = (cnt + BLK - 1) // BLK * BLK
    loc_start = jnp.cumsum(run, axis=1) - run
    nblk_loc = jnp.sum(run, axis=1) // BLK
    seg_rows = jnp.sum(run, axis=0)
    seg_tiles = (seg_rows + TM - 1) // TM
    tile_end = jnp.cumsum(seg_tiles)
    seg_start = (tile_end - seg_tiles) * TM
    glob_start = seg_start[None, :] + jnp.cumsum(run, axis=0) - run
    total = tile_end[-1]
    jt = jnp.arange(MOE_TILES, dtype=i32)
    tile_expert = jnp.sum(tile_end[None, :] <= jnp.minimum(jt, total - 1)[:, None], axis=1).astype(i32)
    tile_valid = (jt < total).astype(i32)

    g_blk = (glob_start.T.reshape(1, -1)) // BLK
    n_blk = (run.T.reshape(1, -1)) // BLK
    l_blk = ((jnp.arange(N_TILES, dtype=i32) * LOC)[None, :] + loc_start.T).reshape(1, -1) // BLK
    gb = jnp.arange(MOE_ROWS // BLK, dtype=i32)[:, None]
    off = gb[:, 0] - _prefix_pick(g_blk, gb, g_blk)
    src_blk = jnp.where(off < _prefix_pick(g_blk, gb, n_blk), _prefix_pick(g_blk, gb, l_blk) + off,
                        ZERO_BLK).astype(i32)

    pos = (jnp.arange(LOC_BLKS, dtype=i32) * BLK)[None, :, None]
    shift = _prefix_pick(loc_start[:, None, :], pos, (glob_start - loc_start)[:, None, :])
    gsrc = ((pos[:, :, 0] + shift) // BLK).astype(i32)
    return tile_expert, tile_valid, src_blk, nblk_loc.astype(i32), gsrc.reshape(-1)


def _swap_pairs(w):
    return w.reshape(w.shape[:-1] + (ROPE // 2, 2))[..., ::-1].reshape(w.shape)


def _rope_slot(pe):
    return jnp.pad(pe, [(0, 0)] * (pe.ndim - 1) + [(NOPE, SLOT - NOPE - ROPE)])


def _w_in_layout(w):
    pe = w[:, 1664:]
    return jnp.concatenate([w[:, :1664], _rope_slot(pe), _rope_slot(_swap_pairs(pe))], axis=1).astype(BF16)


def _wuq_layout(w):
    w = w.reshape(Q_LORA, HEADS, NOPE + ROPE)
    nope, pe = w[..., :NOPE], w[..., NOPE:]
    plain = jnp.concatenate([nope, jnp.pad(pe, ((0, 0), (0, 0), (0, SLOT - NOPE - ROPE)))], axis=-1)
    swapped = _rope_slot(_swap_pairs(pe))
    return jnp.concatenate([plain.reshape(Q_LORA, HEADS * SLOT),
                            swapped.reshape(Q_LORA, HEADS * SLOT)], axis=1).astype(BF16)


def _wukv_layout(w):
    w = w.reshape(DEPTH, KV_LORA, HEADS, NOPE + VD)
    wk = jnp.pad(w[..., :NOPE], ((0, 0), (0, 0), (0, 0), (0, SLOT - NOPE)))
    return (wk.reshape(DEPTH, KV_LORA, HEADS * SLOT).astype(BF16),
            w[..., NOPE:].reshape(DEPTH, KV_LORA, HEADS * VD).astype(BF16))


def kernel(x_prompt, x_sample, c, cache_ckv, cache_kpe, c_ctx, ln_in_g, ln_in_b, w_mod, b_mod, w_in,
           gm_ln_g, gm_ln_b, gm_ws, gm_bs, hy_conv_w, hy_conv_b, hy_f_w1, hy_f_b1, hy_f_w2, hy_f_b2,
           hy_f_w3, hy_f_freq, hy_bias, mla_gq, mla_gkv, mla_wuq, mla_wukv, w_out, ln1_g, ln1_b,
           ln2_g, ln2_b, moe_w_gr, moe_b_gr, moe_w_er, moe_b_er, moe_w_gate, moe_w_up, moe_w_down):
    rope = _rope_tables()
    dft = {L: _dft_tables(L) for L in (CTX_L, LAT_L)}
    pos_tab = {L: _hyena_positions(L) for L in (CTX_L, LAT_L)}
    hd = np.arange(GM_W) // GM_HD
    avg = jnp.asarray((hd[:, None] == hd[None, :]) / GM_HD, BF16)
    tri = jnp.asarray(np.tril(np.ones((TM, TM)), -1), BF16)
    upper = jnp.asarray(np.triu(np.ones((128, 128)), 1), BF16)

    cond8 = jnp.concatenate([c_ctx[None], c, jnp.zeros((8 - 1 - N_LAT_B, D), F32)], axis=0)
    mod = _modulation(cond8, w_mod, b_mod).reshape(DEPTH, 8, 6, D)

    wuk_all, wuv_all = _wukv_layout(mla_wukv)
    kc_all, vc_all = _cache_kv(cache_ckv, _rope_slot(cache_kpe), wuk_all, wuv_all)

    x = _input_norm(x_prompt.reshape(T_CTX, D), x_sample.reshape(T_LAT, D), ln_in_g, ln_in_b)
    ckv_states, kpe_states = [], []
    for l in range(DEPTH):
        w_in_ext = _w_in_layout(w_in[l])
        wuq_arr = _wuq_layout(mla_wuq[l])
        bs_full = jnp.repeat(gm_bs[l].T, GM_HD, axis=1)
        a, hy, q, k, v, ckv, kpe = _in_proj(
            x, mod[l], w_in_ext, gm_ln_g[l].reshape(1, GM_W), gm_ln_b[l].reshape(1, GM_W), avg,
            gm_ws[l].astype(BF16), bs_full, mla_gq[l].reshape(1, Q_LORA), mla_gkv[l].reshape(1, KV_LORA),
            wuq_arr, wuk_all[l], wuv_all[l], rope)
        ckv_states.append(ckv[:T_CTX].reshape(N_CTX_B, CTX_L, KV_LORA))
        kpe_states.append(kpe[:T_CTX, NOPE:NOPE + ROPE].reshape(N_CTX_B, CTX_L, ROPE))

        hyb = []
        for L, nb, blk0 in ((CTX_L, N_CTX_B, 0), (LAT_L, N_LAT_B, T_CTX // LAT_L)):
            z, dec = pos_tab[L]
            spectra = _filter_spectra(L, z, dec, dft[L], hy_f_w1[l], hy_f_b1[l], hy_f_w2[l],
                                      hy_f_b2[l], hy_f_w3[l], hy_f_freq[l])
            hyb.append(_hyena_group(hy, L, nb, blk0, dft[L], spectra, hy_conv_w[l], hy_conv_b[l],
                                    hy_bias[l]))
        hyb = jnp.concatenate(hyb, axis=0)

        att = _attention(q, k, v, kc_all[l], vc_all[l])

        w_route = jnp.pad(jnp.concatenate([moe_w_gr[l], moe_w_er[l].reshape(D, N_EXP)], axis=1),
                          ((0, 0), (0, 128 - N_GROUPS - N_EXP)))
        b_route =jnp.pad(jnp.concatenate([moe_b_gr[l], moe_b_er[l].reshape(N_EXP)]),
                          (0, 128 - N_GROUPS - N_EXP)).reshape(1, 128)
        x1, xloc, route, counts = _out_proj(a, hyb, att, w_out[l].astype(BF16), x, mod[l],
                                            ln1_g[l].reshape(1, D), ln1_b[l].reshape(1, D),
                                            w_route, b_route, tri, upper)
        cnt = counts.reshape(N_TILES, 8, 128)[:, 0, :N_EXP].astype(jnp.int32)
        tile_expert, tile_valid, src_blk, nblk_loc, gsrc = _dispatch_tables(cnt)
        y = _experts(l, tile_expert, tile_valid, src_blk, xloc, moe_w_gate, moe_w_up, moe_w_down)
        x = _combine(nblk_loc, gsrc, y, route, x1, mod[l], ln2_g[l].reshape(1, D),
                     ln2_b[l].reshape(1, D), split=(l == DEPTH - 1))

    y_prompt = x[0].reshape(N_CTX_B, CTX_L, D)
    y_sample = x[1].reshape(N_LAT_B, LAT_L, D)
    return (y_prompt, y_sample, jnp.stack(ckv_states, axis=1), jnp.stack(kpe_states, axis=1))
```

```python
import functools
import math

import numpy as np
import jax
import jax.numpy as jnp
from jax import lax
from jax.experimental import pallas as pl
from jax.experimental.pallas import tpu as pltpu

D = 1024
N_CTX_B, CTX_L = 16, 256
N_LAT_B, LAT_L = 4, 1024
DEPTH = 2
T_CTX = N_CTX_B * CTX_L
T_LAT = N_LAT_B * LAT_L
T = T_CTX + T_LAT
PAST = 256
GRID_W = 64

GM_HEADS, GM_HD, GM_W, CHUNK = 4, 64, 256, 128
HY_W, HY_EMB, HY_FH = 256, 33, 64
NOPE, ROPE, VD, HEADS = 64, 32, 64, 8
Q_LORA, KV_LORA = 256, 128
SLOT = 128
N_GROUPS, EPG, N_EXP, D_EXP = 4, 8, 32, 256
ALPHA = (2.0 * DEPTH) ** 0.25
LN_EPS, RMS_EPS = 1e-5, 1e-6
ROPE_THETA = 10000.0

TM = 256
N_TILES = T // TM
CTX_TILES = T_CTX // TM
LAT_TILES_PER_B = LAT_L // TM
N_ASSIGN = 2 * T
BLK = 16
BLK_PER_TILE = TM // BLK
LOC = 2 * TM + N_EXP * BLK
LOC_BLKS = LOC // BLK
ZERO_BLK = LOC_BLKS - 1
MOE_TILES = (N_ASSIGN + N_TILES * N_EXP * (BLK - 1)) // TM + N_EXP
MOE_ROWS = MOE_TILES * TM
IN_EXT = 1920

F32, BF16 = jnp.float32, jnp.bfloat16
VMEM_LIMIT = 52 * 1024 * 1024


def _cparams(*sem):
    return pltpu.CompilerParams(dimension_semantics=sem, vmem_limit_bytes=VMEM_LIMIT)


def _cond_of_tile(i):
    return jnp.where(i < CTX_TILES, 0, 1 + (i - CTX_TILES) // LAT_TILES_PER_B)


def _rope_block_of_tile(i):
    return jnp.where(i < CTX_TILES, 0, 1 + (i - CTX_TILES) % LAT_TILES_PER_B)


def _full(shape):
    n = len(shape)
    return pl.BlockSpec(shape, lambda *_: (0,) * n)


def _layer_norm(x, g, b):
    mu = jnp.mean(x, -1, keepdims=True)
    xc = x - mu
    var = jnp.mean(xc * xc, -1, keepdims=True)
    return xc * lax.rsqrt(var + LN_EPS) * g + b


def _rms_norm(x, g):
    return x * lax.rsqrt(jnp.mean(x * x, -1, keepdims=True) + RMS_EPS) * g


def _bdot(a, b):
    return jnp.dot(a.astype(BF16), b.astype(BF16), preferred_element_type=F32)


def _split3_dot(x, w_bf16):
    hi = x.astype(BF16)
    r1 = x - hi.astype(F32)
    mid = r1.astype(BF16)
    lo = (r1 - mid.astype(F32)).astype(BF16)
    dot = functools.partial(jnp.dot, preferred_element_type=F32)
    return dot(hi, w_bf16) + dot(mid, w_bf16) + dot(lo, w_bf16)


def _dft_tables(L):
    f = np.arange(L, dtype=np.int64)
    ft = np.outer(f, f) % (2 * L)
    ang = np.pi * ft / L
    c = np.cos(ang)
    s = np.sin(ang)
    alt = np.where(f % 2 == 0, 1.0, -1.0)
    sf = s.copy()
    sf[0, :] = alt
    return (jnp.asarray(c, F32).astype(BF16), jnp.asarray(sf, F32).astype(BF16),
            jnp.asarray(sf.T.copy(), F32).astype(BF16), jnp.asarray(alt[:, None], F32))


def _rope_tables():
    rows = LAT_L // GRID_W
    row = np.repeat(np.arange(rows, dtype=np.float64), GRID_W)
    col = np.tile(np.arange(GRID_W, dtype=np.float64), rows)
    n = ROPE // 4
    inv = ROPE_THETA ** (-np.arange(n, dtype=np.float64) / n)
    ang = np.concatenate([row[:, None] * inv, col[:, None] * inv], -1)
    cos = np.zeros((CTX_L + LAT_L, SLOT))
    sin = np.zeros((CTX_L + LAT_L, SLOT))
    cos[:, :NOPE + ROPE] = 1.0
    cos[CTX_L:, NOPE:NOPE + ROPE:2] = np.cos(ang)
    cos[CTX_L:, NOPE + 1:NOPE + ROPE:2] = np.cos(ang)
    sin[CTX_L:, NOPE:NOPE + ROPE:2] = -np.sin(ang)
    sin[CTX_L:, NOPE + 1:NOPE + ROPE:2] = np.sin(ang)
    scale = 1.0 / math.sqrt(NOPE + ROPE)
    return (jnp.asarray(cos * scale, F32), jnp.asarray(sin * scale, F32),
            jnp.asarray(cos, F32), jnp.asarray(sin, F32))


def _hyena_positions(L):
    t = jnp.linspace(0.0, 1.0, L, dtype=F32)[:, None]
    bands = (HY_EMB - 1) // 2
    w = 2.0 * math.pi * jnp.arange(L, dtype=F32)[:, None] / L
    f = jnp.linspace(1e-4, bands - 1, bands, dtype=F32)[None]
    z = jnp.concatenate([t, jnp.cos(f * w), -jnp.sin(f * w)], -1)
    z = jnp.pad(z, ((0, 0), (0, 128 - HY_EMB)))
    min_decay = math.log(1e-2) / 1.5
    max_decay = math.log(1e-2) / 0.3
    deltas = jnp.abs(jnp.linspace(min_decay, max_decay, HY_W, dtype=F32))
    return z, jnp.exp(-t * deltas)


def _mod_kernel(c_ref, w_ref, b_ref, o_ref):
    c = c_ref[...]
    s = c / (1.0 + jnp.exp(-c))
    o_ref[0] = _bdot(s, w_ref[0]) + b_ref[0]


def _modulation(cond8, w_mod, b_mod):
    tn = 1536
    return pl.pallas_call(
        _mod_kernel,
        out_shape=jax.ShapeDtypeStruct((DEPTH, 8, 6 * D), F32),
        grid=(DEPTH, 6 * D // tn),
        in_specs=[pl.BlockSpec((8, D), lambda l, j: (0, 0)),
                  pl.BlockSpec((1, D, tn), lambda l, j: (l, 0, j)),
                  pl.BlockSpec((1, 1, tn), lambda l, j: (l, 0, j))],
        out_specs=pl.BlockSpec((1, 8, tn), lambda l, j: (l, 0, j)),
        compiler_params=_cparams("parallel", "parallel"),
        name="modulation",
    )(cond8, w_mod, b_mod.reshape(DEPTH, 1, 6 * D))


def _ln_in_kernel(xc_ref, xl_ref, g_ref, b_ref, o_ref):
    i = pl.program_id(0)

    @pl.when(i < CTX_TILES)
    def _():
        o_ref[...] = _layer_norm(xc_ref[...], g_ref[...], b_ref[...])

    @pl.when(i >= CTX_TILES)
    def _():
        o_ref[...] = _layer_norm(xl_ref[...], g_ref[...], b_ref[...])


def _input_norm(xc, xl, g, b):
    return pl.pallas_call(
        _ln_in_kernel,
        out_shape=jax.ShapeDtypeStruct((T, D), F32),
        grid=(N_TILES,),
        in_specs=[pl.BlockSpec((TM, D), lambda i: (jnp.minimum(i, CTX_TILES - 1), 0)),
                  pl.BlockSpec((TM, D), lambda i: (jnp.maximum(i - CTX_TILES, 0), 0)),
                  _full((1, D)), _full((1, D))],
        out_specs=pl.BlockSpec((TM, D), lambda i: (i, 0)),
        compiler_params=_cparams("parallel"),
        name="input_norm",
    )(xc, xl, g.reshape(1, D), b.reshape(1, D))


def _filter_kernel(z_ref, w1_ref, b1_ref, w2_ref, b2_ref, w3_ref, fr_ref, dec_ref, c_ref, sf_ref,
                   alt_ref, kr_ref, ki_ref, krn_ref, *, L):
    hdot = functools.partial(jnp.dot, preferred_element_type=F32, precision=lax.Precision.HIGHEST)
    h1 = jnp.sin(fr_ref[0:1] * (hdot(z_ref[...], w1_ref[...]) + b1_ref[...]))
    h2 = jnp.sin(fr_ref[1:2] * (hdot(h1, w2_ref[...]) + b2_ref[...]))
    filt = hdot(h2, w3_ref[...])
    row = lax.broadcasted_iota(jnp.int32, (L, HY_W), 0)
    dec = dec_ref[...]
    for o in range(2):
        fwd = filt[:, o * 2 * HY_W:o * 2 * HY_W + HY_W] * dec
        bwd = jnp.where(row == 0, 0.0, filt[:, o * 2 * HY_W + HY_W:(o + 1) * 2 * HY_W] * dec)
        sm = fwd + bwd
        df = fwd - bwd
        kr = _bdot(c_ref[...], sm)
        ki = -_bdot(sf_ref[...], df)
        kr_nyq = jnp.sum(alt_ref[...] * sm, axis=0, keepdims=True)
        kr_ref[o] = jnp.where(row == 0, kr * (0.5 / L), kr * (1.0 / L))
        ki_ref[o] = jnp.where(row == 0, 0.0, ki * (1.0 / L))
        krn_ref[o] = jnp.where(row == 0, kr_nyq * (0.5 / L), kr * (1.0 / L))


def _filter_spectra(L, z, dec, dft, w1, b1, w2, b2, w3, freq):
    c, sf, _, alt = dft
    pad = 128 - HY_FH
    w1p = jnp.pad(w1, ((0, 128 - HY_EMB), (0, pad)))
    w2p = jnp.pad(w2, ((0, pad), (0, pad)))
    w3p = jnp.pad(w3, ((0, pad), (0, 0)))
    b1p = jnp.pad(b1, (0, pad)).reshape(1, 128)
    b2p = jnp.pad(b2, (0, pad)).reshape(1, 128)
    frp = jnp.pad(freq, ((0, 0), (0, pad)))
    shp = jax.ShapeDtypeStruct((2, L, HY_W), F32)
    args = (z, w1p, b1p, w2p, b2p, w3p, frp, dec, c, sf, alt)
    return pl.pallas_call(
        functools.partial(_filter_kernel, L=L),
        out_shape=(shp, shp, shp),
        grid=(1,),
        in_specs=[_full(a.shape) for a in args],
        out_specs=(_full((2, L, HY_W)),) * 3,
        compiler_params=_cparams("arbitrary"),
        name=f"hyena_filter_{L}",
    )(*args)


def _in_kernel(x_ref, mod_ref, w_ref, gg_ref, gb_ref, avg_ref, ws_ref, bs_ref, gq_ref, gkv_ref,
               wuq_ref, wuk_ref, wuv_ref, cq_ref, sq_ref, ck_ref, sk_ref,
               a_ref, hy_ref, q_ref, k_ref, v_ref, ckv_ref, kpe_ref):
    m = mod_ref[0]
    h = x_ref[...] * (1.0 + m[1:2]) + m[0:1]
    proj = _bdot(h, w_ref[...])

    gu = jax.nn.gelu(proj[:, 0:GM_W], approximate=True)
    gv = jax.nn.gelu(proj[:, GM_W:2 * GM_W], approximate=True)
    avg = avg_ref[...]
    mu = _split3_dot(gv, avg)
    vc = gv - mu
    var = _split3_dot(vc * vc, avg)
    vln = (vc * lax.rsqrt(var + LN_EPS) * gg_ref[...] + gb_ref[...]).astype(BF16)
    lane = lax.broadcasted_iota(jnp.int32, (CHUNK, GM_W), 1)
    for c in range(TM // CHUNK):
        vchunk = vln[c * CHUNK:(c + 1) * CHUNK]
        s = bs_ref[...]
        for hd in range(GM_HEADS):
            sh = jnp.dot(ws_ref[hd], vchunk, preferred_element_type=F32)
            s = s + jnp.where(lane // GM_HD == hd, sh, 0.0)
        a_ref[c * CHUNK:(c + 1) * CHUNK, :] = (gu[c * CHUNK:(c + 1) * CHUNK] * s).astype(BF16)

    hy_ref[...] = proj[:, 512:1280]

    cq = _rms_norm(proj[:, 1280:1536], gq_ref[...])
    qq = _bdot(cq, wuq_ref[...])
    cos_q = jnp.concatenate([cq_ref[...]] * HEADS, axis=1)
    sin_q = jnp.concatenate([sq_ref[...]] * HEADS, axis=1)
    q_ref[...] = (qq[:, :HEADS * SLOT] * cos_q + qq[:, HEADS * SLOT:] * sin_q).astype(BF16)

    ckv = _rms_norm(proj[:, 1536:1664], gkv_ref[...])
    ckv_ref[...] = ckv
    kpe = proj[:, 1664:1792]
    kpe_ref[...] = kpe
    krot = kpe * ck_ref[...] + proj[:, 1792:1920] * sk_ref[...]
    kn = _bdot(ckv, wuk_ref[...])
    k_ref[...] = (kn + jnp.concatenate([krot] * HEADS, axis=1)).astype(BF16)
    v_ref[...] = _bdot(ckv, wuv_ref[...]).astype(BF16)


def _in_proj(x, mod_l, w_in_ext, gm_g, gm_b, avg, ws, bs_full, gq, gkv, wuq_arr, wuk_arr, wuv_arr, rope):
    cos_q, sin_q, cos_k, sin_k = rope
    tile = lambda n: pl.BlockSpec((TM, n), lambda i: (i, 0))
    rope_spec = pl.BlockSpec((TM, SLOT), lambda i: (_rope_block_of_tile(i), 0))
    out_shapes = (jax.ShapeDtypeStruct((T, GM_W), BF16),
                  jax.ShapeDtypeStruct((T, 3 * HY_W), F32),
                  jax.ShapeDtypeStruct((T, HEADS * SLOT), BF16),
                  jax.ShapeDtypeStruct((T, HEADS * SLOT), BF16),
                  jax.ShapeDtypeStruct((T, HEADS * VD), BF16),
                  jax.ShapeDtypeStruct((T, KV_LORA), F32),
                  jax.ShapeDtypeStruct((T, SLOT), F32))
    return pl.pallas_call(
        _in_kernel,
        out_shape=out_shapes,
        grid=(N_TILES,),
        in_specs=[tile(D),
                  pl.BlockSpec((1, 6, D), lambda i: (_cond_of_tile(i), 0, 0)),
                  _full((D, IN_EXT)), _full((1, GM_W)), _full((1, GM_W)), _full((GM_W, GM_W)),
                  _full((GM_HEADS, CHUNK, CHUNK)), _full((CHUNK, GM_W)),
                  _full((1, Q_LORA)), _full((1, KV_LORA)),
                  _full((Q_LORA, 2 * HEADS * SLOT)), _full((KV_LORA, HEADS * SLOT)),
                  _full((KV_LORA, HEADS * VD)),
                  rope_spec, rope_spec, rope_spec, rope_spec],
        out_specs=(tile(GM_W), tile(3 * HY_W), tile(HEADS * SLOT), tile(HEADS * SLOT),
                   tile(HEADS * VD), tile(KV_LORA), tile(SLOT)),
        compiler_params=_cparams("parallel"),
        name="in_proj",
    )(x, mod_l, w_in_ext, gm_g, gm_b, avg, ws, bs_full, gq, gkv, wuq_arr, wuk_arr, wuv_arr,
      cos_q, sin_q, cos_k, sin_k)


def _cache_kernel(ckv_ref, kpe_ref, wuk_ref, wuv_ref, k_ref, v_ref):
    ckv = ckv_ref[0, 0]
    kn = _bdot(ckv, wuk_ref[0])
    k_ref[0, 0] = (kn + jnp.concatenate([kpe_ref[0, 0]] * HEADS, axis=1)).astype(BF16)
    v_ref[0, 0] = _bdot(ckv, wuv_ref[0]).astype(BF16)


def _cache_kv(cache_ckv, cache_kpe_slot, wuk_arr, wuv_arr):
    return pl.pallas_call(
        _cache_kernel,
        out_shape=(jax.ShapeDtypeStruct((DEPTH, N_LAT_B, PAST, HEADS * SLOT), BF16),
                   jax.ShapeDtypeStruct((DEPTH, N_LAT_B, PAST, HEADS * VD), BF16)),
        grid=(DEPTH, N_LAT_B),
        in_specs=[pl.BlockSpec((1, 1, PAST, KV_LORA), lambda l, b: (b, l, 0, 0)),
                  pl.BlockSpec((1, 1, PAST, SLOT), lambda l, b: (b, l, 0, 0)),
                  pl.BlockSpec((1, KV_LORA, HEADS * SLOT), lambda l, b: (l, 0, 0)),
                  pl.BlockSpec((1, KV_LORA, HEADS * VD), lambda l, b: (l, 0, 0))],
        out_specs=(pl.BlockSpec((1, 1, PAST, HEADS * SLOT), lambda l, b: (l, b, 0, 0)),
                   pl.BlockSpec((1, 1, PAST, HEADS * VD), lambda l, b: (l, b, 0, 0))),
        compiler_params=_cparams("parallel", "parallel"),
        name="cache_kv",
    )(cache_ckv, cache_kpe_slot, wuk_arr, wuv_arr)


def _attend(q_ref, segments, o_ref):
    lq = q_ref.shape[0]
    lane = lax.broadcasted_iota(jnp.int32, (lq, 2 * VD), 1)
    nt = (((1,), (1,)), ((), ()))
    for pair in range(HEADS // 2):
        outs = []
        for hd in (2 * pair, 2 * pair + 1):
            qh = q_ref[:, hd * SLOT:(hd + 1) * SLOT]
            scores = [lax.dot_general(qh, k_ref[:, hd * SLOT:(hd + 1) * SLOT], nt,
                                      preferred_element_type=F32) for k_ref, _ in segments]
            mx = functools.reduce(jnp.maximum, [jnp.max(s, -1, keepdims=True) for s in scores])
            ps = [jnp.exp(s - mx) for s in scores]
            den = functools.reduce(jnp.add, [jnp.sum(p, -1, keepdims=True) for p in ps])
            acc = functools.reduce(jnp.add, [
                jnp.dot(p.astype(BF16), v_ref[:, pair * 2 * VD:(pair + 1) * 2 * VD],
                        preferred_element_type=F32) for p, (_, v_ref) in zip(ps, segments)])
            outs.append(acc / den)
        o_ref[:, pair * 2 * VD:(pair + 1) * 2 * VD] = jnp.where(lane < VD, outs[0], outs[1]).astype(BF16)


def _attn_ctx_kernel(q_ref, k_ref, v_ref, o_ref):
    _attend(q_ref, [(k_ref, v_ref)], o_ref)


def _attn_lat_kernel(q_ref, kc_ref, vc_ref, k_ref, v_ref, o_ref):
    _attend(q_ref, [(kc_ref.at[0], vc_ref.at[0]), (k_ref, v_ref)], o_ref)


def _attention(q, k, v, kc, vc):
    kw, vw = HEADS * SLOT, HEADS * VD
    ctx = pl.pallas_call(
        _attn_ctx_kernel,
        out_shape=jax.ShapeDtypeStruct((T_CTX, vw), BF16),
        grid=(N_CTX_B,),
        in_specs=[pl.BlockSpec((CTX_L, kw), lambda b: (b, 0)),
                  pl.BlockSpec((CTX_L, kw), lambda b: (b, 0)),
                  pl.BlockSpec((CTX_L, vw), lambda b: (b, 0))],
        out_specs=pl.BlockSpec((CTX_L, vw), lambda b: (b, 0)),
        compiler_params=_cparams("parallel"),
        name="attn_ctx",
    )(q, k, v)
    nq = LAT_L // TM
    off = T_CTX // LAT_L
    lat = pl.pallas_call(
        _attn_lat_kernel,
        out_shape=jax.ShapeDtypeStruct((T_LAT, vw), BF16),
        grid=(N_LAT_B, nq),
        in_specs=[pl.BlockSpec((TM, kw), lambda b, j: (CTX_TILES + b * nq + j, 0)),
                  pl.BlockSpec((1, PAST, kw), lambda b, j: (b, 0, 0)),
                  pl.BlockSpec((1, PAST, vw), lambda b, j: (b, 0, 0)),
                  pl.BlockSpec((LAT_L, kw), lambda b, j: (off + b, 0)),
                  pl.BlockSpec((LAT_L, vw), lambda b, j: (off + b, 0))],
        out_specs=pl.BlockSpec((TM, vw), lambda b, j: (b * nq + j, 0)),
        compiler_params=_cparams("parallel", "parallel"),
        name="attn_lat",
    )(q, kc, vc, k, v)
    return jnp.concatenate([ctx, lat], axis=0)


def _hyena_kernel(hy_ref, cw_ref, cb_ref, c_ref, sf_ref, sb_ref, kr_ref, ki_ref, krn_ref, hb_ref,
                  o_ref, *, L):
    x = hy_ref[...]
    row = lax.broadcasted_iota(jnp.int32, x.shape, 0)
    prev = jnp.where(row == 0, 0.0, pltpu.roll(x, 1, 0))
    nxt = jnp.where(row == L - 1, 0.0, pltpu.roll(x, L - 1, 0))
    z = prev * cw_ref[0:1] + x * cw_ref[1:2] + nxt * cw_ref[2:3] + cb_ref[...]
    y = z[:, 2 * HY_W:]
    for o in range(2):
        gate = z[:, o * HY_W:(o + 1) * HY_W]
        yb = y.astype(BF16)
        a_re = jnp.dot(c_ref[...], yb, preferred_element_type=F32)
        a_im = jnp.dot(sf_ref[...], yb, preferred_element_type=F32)
        ki = ki_ref[o]
        z_re = a_re * kr_ref[o] + a_im * ki
        z_im = a_im * krn_ref[o] - a_re * ki
        conv = (jnp.dot(c_ref[...], z_re.astype(BF16), preferred_element_type=F32)
                + jnp.dot(sb_ref[...], z_im.astype(BF16), preferred_element_type=F32))
        y = gate * (conv + y * hb_ref[o:o + 1])
    o_ref[...] = y.astype(BF16)


def _hyena_group(hy, L, nb, blk0, dft, spectra, conv_w, conv_b, hy_bias):
    c, sf, sb, _ = dft
    kr, ki, krn = spectra
    return pl.pallas_call(
        functools.partial(_hyena_kernel, L=L),
        out_shape=jax.ShapeDtypeStruct((nb * L, HY_W), BF16),
        grid=(nb,),
        in_specs=[pl.BlockSpec((L, 3 * HY_W), lambda b: (blk0 + b, 0)),
                  _full((3, 3 * HY_W)), _full((1, 3 * HY_W)),
                  _full((L, L)), _full((L, L)), _full((L, L)),
                  _full((2, L, HY_W)), _full((2, L, HY_W)), _full((2, L, HY_W)),
                  _full((2, HY_W))],
        out_specs=pl.BlockSpec((L, HY_W), lambda b: (b, 0)),
        compiler_params=_cparams("parallel"),
        name=f"hyena_{L}",
    )(hy, conv_w, conv_b.reshape(1, 3 * HY_W), c, sf, sb, kr, ki, krn, hy_bias)


def _out_kernel(a_ref, b_ref, m_ref, w_ref, x_ref, mod_ref, g_ref, be_ref, wr_ref, br_ref, tri_ref,
                upper_ref, x1_ref, xloc_ref, route_ref, cnt_ref):
    dot = functools.partial(jnp.dot, preferred_element_type=F32)
    mixed = (dot(a_ref[...], w_ref[0:GM_W]) + dot(b_ref[...], w_ref[GM_W:GM_W + HY_W])
             + dot(m_ref[...], w_ref[GM_W + HY_W:]))
    m = mod_ref[0]
    x1 = _layer_norm(ALPHA * x_ref[...] + m[2:3] * mixed, g_ref[...], be_ref[...])
    x1_ref[...] = x1
    h2 = x1 * (1.0 + m[4:5]) + m[3:4]

    h_hi = h2.astype(BF16)
    h_lo = (h2 - h_hi.astype(F32)).astype(BF16)
    w_hi = wr_ref[...].astype(BF16)
    w_lo = (wr_ref[...] - w_hi.astype(F32)).astype(BF16)
    part = dot(h_hi, jnp.concatenate([w_hi, w_lo], axis=1))
    logits = part[:, :128] + part[:, 128:] + dot(h_lo, w_hi) + br_ref[...]
    lane = lax.broadcasted_iota(jnp.int32, logits.shape, 1)
    lanef = lane.astype(F32)
    big = jnp.float32(1e9)
    ninf = jnp.float32(-jnp.inf)
    is_g = lane < N_GROUPS
    gl = jnp.where(is_g, logits, ninf)
    gmax = jnp.max(gl, -1, keepdims=True)
    gidx = jnp.min(jnp.where(gl == gmax, lanef, big), -1, keepdims=True)
    gw = 1.0 / jnp.sum(jnp.where(is_g, jnp.exp(logits - gmax), 0.0), -1, keepdims=True)
    ex = lane - N_GROUPS
    in_group = (ex >= 0) & (ex < N_EXP) & ((ex // EPG).astype(F32) == gidx)
    el = jnp.where(in_group, logits, ninf)
    v1 = jnp.max(el, -1, keepdims=True)
    i1 = jnp.min(jnp.where(el == v1, lanef, big), -1, keepdims=True)
    el2 = jnp.where(lanef == i1, ninf, el)
    v2 = jnp.max(el2, -1, keepdims=True)
    i2 = jnp.min(jnp.where(el2 == v2, lanef, big), -1, keepdims=True)
    e21 = jnp.exp(v2 - v1)
    w1 = gw / (1.0 + e21)
    w2 = gw * e21 / (1.0 + e21)
    e1 = i1 - N_GROUPS
    e2 = i2 - N_GROUPS

    oh1 = jnp.where(lanef == e1, 1.0, 0.0)
    oh2 = jnp.where(lanef == e2, 1.0, 0.0)
    ex1 = dot(tri_ref[...], oh1.astype(BF16))
    ex2 = dot(tri_ref[...], oh2.astype(BF16))
    col1 = jnp.sum(oh1, axis=0, keepdims=True)
    col2 = jnp.sum(oh2, axis=0, keepdims=True)
    n = col1 + col2
    run = jnp.floor((n + (BLK - 1.0)) * (1.0 / BLK)) * BLK
    start = dot(jnp.broadcast_to(run, (8, 128)).astype(BF16), upper_ref[...])[0:1]
    loc1 = jnp.sum(oh1 * (start + ex1), -1, keepdims=True)
    loc2 = jnp.sum(oh2 * (start + col1 + ex2), -1, keepdims=True)
    cnt_ref[...] = jnp.broadcast_to(n, cnt_ref.shape)

    route = jnp.zeros_like(logits)
    for j, val in enumerate((e1, e2, w1, w2, loc1, loc2)):
        route = jnp.where(lane == j, val, route)
    route_ref[...] = route

    loc1_row = jnp.transpose(jnp.broadcast_to(loc1, (TM, 128)))[0:1].astype(jnp.int32)
    loc2_row = jnp.transpose(jnp.broadcast_to(loc2, (TM, 128)))[0:1].astype(jnp.int32)
    slot = lax.broadcasted_iota(jnp.int32, (LOC, TM), 0)
    perm = jnp.where((slot == loc1_row) | (slot == loc2_row), 1.0, 0.0).astype(BF16)
    xloc_ref[...] = dot(perm, h_hi).astype(BF16)


def _out_proj(a, hyb, att, w_out, x, mod_l, g, b, w_route, b_route, tri, upper):
    tile = lambda n: pl.BlockSpec((TM, n), lambda i: (i, 0))
    return pl.pallas_call(
        _out_kernel,
        out_shape=(jax.ShapeDtypeStruct((T, D), F32), jax.ShapeDtypeStruct((N_TILES * LOC, D), BF16),
                   jax.ShapeDtypeStruct((T, 128), F32), jax.ShapeDtypeStruct((N_TILES * 8, 128), F32)),
        grid=(N_TILES,),
        in_specs=[tile(GM_W), tile(HY_W), tile(HEADS * VD), _full((D, D)), tile(D),
                  pl.BlockSpec((1, 6, D), lambda i: (_cond_of_tile(i), 0, 0)),
                  _full((1, D)), _full((1, D)), _full((D, 128)), _full((1, 128)), _full((TM, TM)),
                  _full((128, 128))],
        out_specs=(tile(D), pl.BlockSpec((LOC, D), lambda i: (i, 0)), tile(128),
                   pl.BlockSpec((8, 128), lambda i: (i, 0))),
        compiler_params=_cparams("parallel"),
        name="out_proj_route",
    )(a, hyb, att, w_out, x, mod_l, g, b, w_route, b_route, tri, upper)


def _expert_kernel(te_ref, tv_ref, first_ref, ws_ref, nxt_ref, src_ref, x_hbm, wg_hbm, wu_hbm, wd_hbm,
                   y_ref, xbuf, wg_f, wu_f, wd_f, wgu_b, wd_b, sem, wsem, *, layer):
    j = pl.program_id(0)
    slot = j % 2

    def block_copies(t, s):
        return [pltpu.make_async_copy(
            x_hbm.at[pl.ds(pl.multiple_of(src_ref[t * BLK_PER_TILE + b] * BLK, BLK), BLK), :],
            xbuf.at[s, pl.ds(b * BLK, BLK), :], sem.at[s]) for b in range(BLK_PER_TILE)]

    def weight_copies(e, s):
        return [pltpu.make_async_copy(w_hbm.at[layer, e], w_f.at[s], wsem.at[s])
                for w_hbm, w_f in ((wg_hbm, wg_f), (wu_hbm, wu_f), (wd_hbm, wd_f))]

    @pl.when(j == 0)
    def _():
        for cp in weight_copies(te_ref[0], 0):
            cp.start()
        for cp in block_copies(0, 0):
            cp.start()

    @pl.when(jnp.logical_and(j + 1 < MOE_TILES, tv_ref[jnp.minimum(j + 1, MOE_TILES - 1)] != 0))
    def _():
        for cp in block_copies(j + 1, 1 - slot):
            cp.start()

    @pl.when(first_ref[j] != 0)
    def _():
        ws = ws_ref[j]
        for cp in weight_copies(te_ref[j], ws):
            cp.wait()

        @pl.when(nxt_ref[j] >= 0)
        def _():
            for cp in weight_copies(nxt_ref[j], 1 - ws):
                cp.start()

        wgu_b[:, 0:D_EXP] = wg_f[ws].astype(BF16)
        wgu_b[:, D_EXP:2 * D_EXP] = wu_f[ws].astype(BF16)
        wd_b[...] = wd_f[ws].astype(BF16)

    @pl.when(tv_ref[j] == 0)
    def _():
        y_ref[...] = jnp.zeros_like(y_ref)

    @pl.when(tv_ref[j] != 0)
    def _():
        for cp in block_copies(j, slot):
            cp.wait()
        gu = jnp.dot(xbuf[slot], wgu_b[...], preferred_element_type=F32)
        gate, up = gu[:, :D_EXP], gu[:, D_EXP:]
        hid = gate / (1.0 + jnp.exp(-gate)) * up
        y_ref[...] = jnp.dot(hid.astype(BF16), wd_b[...], preferred_element_type=F32).astype(BF16)


def _experts(l, tables, xloc, w_gate, w_up, w_down):
    grid_spec = pltpu.PrefetchScalarGridSpec(
        num_scalar_prefetch=len(tables),
        grid=(MOE_TILES,),
        in_specs=[pl.BlockSpec(memory_space=pl.ANY)] * 4,
        out_specs=pl.BlockSpec((TM, D), lambda j, *_: (j, 0)),
        scratch_shapes=[pltpu.VMEM((2, TM, D), BF16),
                        pltpu.VMEM((2, D, D_EXP), F32), pltpu.VMEM((2, D, D_EXP), F32),
                        pltpu.VMEM((2, D_EXP, D), F32),
                        pltpu.VMEM((D, 2 * D_EXP), BF16), pltpu.VMEM((D_EXP, D), BF16),
                        pltpu.SemaphoreType.DMA((2,)), pltpu.SemaphoreType.DMA((2,))],
    )
    return pl.pallas_call(
        functools.partial(_expert_kernel, layer=l),
        out_shape=jax.ShapeDtypeStruct((MOE_ROWS, D), BF16),
        grid_spec=grid_spec,
        compiler_params=_cparams("arbitrary"),
        name="experts",
    )(*tables, xloc, w_gate, w_up, w_down)


def _combine_kernel(nblk_ref, gsrc_ref, y_hbm, route_ref, x1_ref, mod_ref, g_ref, b_ref, *rest, split):
    outs, (ybuf, sem) = rest[:-2], rest[-2:]
    i = pl.program_id(0)

    slot_i = i % 2

    def block_copy(t, s, lb):
        src = pl.multiple_of(gsrc_ref[t * LOC_BLKS + lb] * BLK, BLK)
        return pltpu.make_async_copy(y_hbm.at[pl.ds(src, BLK), :],
                                     ybuf.at[s, pl.ds(pl.multiple_of(lb * BLK, BLK), BLK), :], sem.at[s])

    def fetch(t, s):
        def start(lb, c):
            block_copy(t, s, lb).start()
            return c
        lax.fori_loop(0, nblk_ref[t], start, 0)

    @pl.when(i == 0)
    def _():
        ybuf[...] = jnp.zeros_like(ybuf)
        fetch(0, 0)

    @pl.when(i + 1 < N_TILES)
    def _():
        fetch(jnp.minimum(i + 1, N_TILES - 1), 1 - slot_i)

    def wait(lb, c):
        block_copy(i, slot_i, lb).wait()
        return c

    lax.fori_loop(0, nblk_ref[i], wait, 0)
    route = route_ref[...]
    yb = ybuf[slot_i]
    slot = lax.broadcasted_iota(jnp.int32, (TM, LOC), 1)
    pick1 = jnp.where(slot == route[:, 4:5].astype(jnp.int32), 1.0, 0.0).astype(BF16)
    pick2 = jnp.where(slot == route[:, 5:6].astype(jnp.int32), 1.0, 0.0).astype(BF16)
    moe = (route[:, 2:3] * jnp.dot(pick1, yb, preferred_element_type=F32)
           + route[:, 3:4] * jnp.dot(pick2, yb, preferred_element_type=F32))
    m = mod_ref[0]
    res = _layer_norm(ALPHA * x1_ref[...] + m[5:6] * moe, g_ref[...], b_ref[...])
    if not split:
        outs[0][...] = res
    else:
        @pl.when(i < CTX_TILES)
        def _():
            outs[0][...] = res

        @pl.when(i >= CTX_TILES)
        def _():
            outs[1][...] = res


def _combine(nblk, gsrc, y, route, x1, mod_l, g, b, split):
    if split:
        out_shape = (jax.ShapeDtypeStruct((T_CTX, D), F32), jax.ShapeDtypeStruct((T_LAT, D), F32))
        out_specs = (pl.BlockSpec((TM, D), lambda i, nb, gs: (jnp.minimum(i, CTX_TILES - 1), 0)),
                     pl.BlockSpec((TM, D), lambda i, nb, gs: (jnp.maximum(i - CTX_TILES, 0), 0)))
    else:
        out_shape = jax.ShapeDtypeStruct((T, D), F32)
        out_specs = pl.BlockSpec((TM, D), lambda i, nb, gs: (i, 0))
    grid_spec = pltpu.PrefetchScalarGridSpec(
        num_scalar_prefetch=2,
        grid=(N_TILES,),
        in_specs=[pl.BlockSpec(memory_space=pl.ANY),
                  pl.BlockSpec((TM, 128), lambda i, nb, gs: (i, 0)),
                  pl.BlockSpec((TM, D), lambda i, nb, gs: (i, 0)),
                  pl.BlockSpec((1, 6, D), lambda i, nb, gs: (_cond_of_tile(i), 0, 0)),
                  pl.BlockSpec((1, D), lambda i, nb, gs: (0, 0)),
                  pl.BlockSpec((1, D), lambda i, nb, gs: (0, 0))],
        out_specs=out_specs,
        scratch_shapes=[pltpu.VMEM((2, LOC, D), BF16), pltpu.SemaphoreType.DMA((2,))],
    )
    return pl.pallas_call(
        functools.partial(_combine_kernel, split=split),
        out_shape=out_shape,
        grid_spec=grid_spec,
        compiler_params=_cparams("arbitrary"),
        name="moe_combine",
    )(nblk, gsrc, y, route, x1, mod_l, g, b)


def _prefix_pick(starts, query, table):
    delta = table - jnp.concatenate([jnp.zeros_like(table[..., :1]), table[..., :-1]], axis=-1)
    return jnp.sum(jnp.where(starts <= query, delta, 0), axis=-1)


def _dispatch_tables(cnt):
    i32 = jnp.int32
    run = (cnt + BLK - 1) // BLK * BLK
    loc_start = jnp.cumsum(run, axis=1) - run
    nblk_loc = jnp.sum(run, axis=1) // BLK
    seg_rows = jnp.sum(run, axis=0)
    seg_tiles = (seg_rows + TM - 1) // TM
    tile_end = jnp.cumsum(seg_tiles)
    seg_start = (tile_end - seg_tiles) * TM
    glob_start = seg_start[None, :] + jnp.cumsum(run, axis=0) - run
    total = tile_end[-1]
    jt = jnp.arange(MOE_TILES, dtype=i32)
    tile_expert = jnp.sum(tile_end[None, :] <= jnp.minimum(jt, total - 1)[:, None], axis=1).astype(i32)
    tile_valid = (jt < total).astype(i32)
    prev_expert = jnp.concatenate([jnp.full((1,), -1, i32), tile_expert[:-1]])
    first = tile_valid * (tile_expert != prev_expert).astype(i32)
    ring_slot = ((jnp.cumsum(first) - 1) % 2).astype(i32)
    onehot_e = tile_expert[:, None] == jnp.arange(N_EXP, dtype=i32)[None, :]
    nxt_tile = jnp.sum(jnp.where(onehot_e, tile_end[None, :], 0), axis=1)
    nxt_expert = jnp.sum(jnp.where(nxt_tile[:, None] == jt[None, :], tile_expert[None, :], 0), axis=1)
    nxt = jnp.where(nxt_tile < total, nxt_expert, -1).astype(i32)

    g_blk = (glob_start.T.reshape(1, -1)) // BLK
    n_blk = (run.T.reshape(1, -1)) // BLK
    l_blk = ((jnp.arange(N_TILES, dtype=i32) * LOC)[None, :] + loc_start.T).reshape(1, -1) // BLK
    gb = jnp.arange(MOE_ROWS // BLK, dtype=i32)[:, None]
    off = gb[:, 0] - _prefix_pick(g_blk, gb, g_blk)
    src_blk = jnp.where(off < _prefix_pick(g_blk, gb, n_blk), _prefix_pick(g_blk, gb, l_blk) + off,
                        ZERO_BLK).astype(i32)

    pos = (jnp.arange(LOC_BLKS, dtype=i32) * BLK)[None, :, None]
    shift = _prefix_pick(loc_start[:, None, :], pos, (glob_start - loc_start)[:, None, :])
    gsrc = ((pos[:, :, 0] + shift) // BLK).astype(i32)
    expert_tables = (tile_expert, tile_valid, first, ring_slot, nxt, src_blk)
    return expert_tables, nblk_loc.astype(i32), gsrc.reshape(-1)


def _swap_pairs(w):
    return w.reshape(w.shape[:-1] + (ROPE // 2, 2))[..., ::-1].reshape(w.shape)


def _rope_slot(pe):
    return jnp.pad(pe, [(0, 0)] * (pe.ndim - 1) + [(NOPE, SLOT - NOPE - ROPE)])


def _w_in_layout(w):
    pe = w[:, 1664:]
    return jnp.concatenate([w[:, :1664], _rope_slot(pe), _rope_slot(_swap_pairs(pe))], axis=1).astype(BF16)


def _wuq_layout(w):
    w = w.reshape(Q_LORA, HEADS, NOPE + ROPE)
    nope, pe = w[..., :NOPE], w[..., NOPE:]
    plain = jnp.concatenate([nope, jnp.pad(pe, ((0, 0), (0, 0), (0, SLOT - NOPE - ROPE)))], axis=-1)
    swapped = _rope_slot(_swap_pairs(pe))
    return jnp.concatenate([plain.reshape(Q_LORA, HEADS * SLOT),
                            swapped.reshape(Q_LORA, HEADS * SLOT)], axis=1).astype(BF16)


def _wukv_layout(w):
    w = w.reshape(DEPTH, KV_LORA, HEADS, NOPE + VD)
    wk = jnp.pad(w[..., :NOPE], ((0, 0), (0, 0), (0, 0), (0, SLOT - NOPE)))
    return (wk.reshape(DEPTH, KV_LORA, HEADS * SLOT).astype(BF16),
            w[..., NOPE:].reshape(DEPTH, KV_LORA, HEADS * VD).astype(BF16))


def kernel(x_prompt, x_sample, c, cache_ckv, cache_kpe, c_ctx, ln_in_g, ln_in_b, w_mod, b_mod, w_in,
           gm_ln_g, gm_ln_b, gm_ws, gm_bs, hy_conv_w, hy_conv_b, hy_f_w1, hy_f_b1, hy_f_w2, hy_f_b2,
           hy_f_w3, hy_f_freq, hy_bias, mla_gq, mla_gkv, mla_wuq, mla_wukv, w_out, ln1_g, ln1_b,
           ln2_g, ln2_b, moe_w_gr, moe_b_gr, moe_w_er, moe_b_er, moe_w_gate, moe_w_up, moe_w_down):
    rope = _rope_tables()
    dft = {L: _dft_tables(L) for L in (CTX_L, LAT_L)}
    pos_tab = {L: _hyena_positions(L) for L in (CTX_L, LAT_L)}
    hd = np.arange(GM_W) // GM_HD
    avg = jnp.asarray((hd[:, None] == hd[None, :]) / GM_HD, BF16)
    tri = jnp.asarray(np.tril(np.ones((TM, TM)), -1), BF16)
    upper = jnp.asarray(np.triu(np.ones((128, 128)), 1), BF16)

    cond8 = jnp.concatenate([c_ctx[None], c, jnp.zeros((8 - 1 - N_LAT_B, D), F32)], axis=0)
    mod = _modulation(cond8, w_mod, b_mod).reshape(DEPTH, 8, 6, D)

    wuk_all, wuv_all = _wukv_layout(mla_wukv)
    kc_all, vc_all = _cache_kv(cache_ckv, _rope_slot(cache_kpe), wuk_all, wuv_all)

    x = _input_norm(x_prompt.reshape(T_CTX, D), x_sample.reshape(T_LAT, D), ln_in_g, ln_in_b)
    ckv_states, kpe_states = [], []
    for l in range(DEPTH):
        w_in_ext = _w_in_layout(w_in[l])
        wuq_arr = _wuq_layout(mla_wuq[l])
        bs_full = jnp.repeat(gm_bs[l].T, GM_HD, axis=1)
        a, hy, q, k, v, ckv, kpe = _in_proj(
            x, mod[l], w_in_ext, gm_ln_g[l].reshape(1, GM_W), gm_ln_b[l].reshape(1, GM_W), avg,
            gm_ws[l].astype(BF16), bs_full, mla_gq[l].reshape(1, Q_LORA), mla_gkv[l].reshape(1, KV_LORA),
            wuq_arr, wuk_all[l], wuv_all[l], rope)
        ckv_states.append(ckv[:T_CTX].reshape(N_CTX_B, CTX_L, KV_LORA))
        kpe_states.append(kpe[:T_CTX, NOPE:NOPE + ROPE].reshape(N_CTX_B, CTX_L, ROPE))

        hyb = []
        for L, nb, blk0 in ((CTX_L, N_CTX_B, 0), (LAT_L, N_LAT_B, T_CTX // LAT_L)):
            z, dec = pos_tab[L]
            spectra = _filter_spectra(L, z, dec, dft[L], hy_f_w1[l], hy_f_b1[l], hy_f_w2[l],
                                      hy_f_b2[l], hy_f_w3[l], hy_f_freq[l])
            hyb.append(_hyena_group(hy, L, nb, blk0, dft[L], spectra, hy_conv_w[l], hy_conv_b[l],
                                    hy_bias[l]))
        hyb = jnp.concatenate(hyb, axis=0)

        att = _attention(q, k, v, kc_all[l], vc_all[l])

        w_route = jnp.pad(jnp.concatenate([moe_w_gr[l], moe_w_er[l].reshape(D, N_EXP)], axis=1),
                          ((0, 0), (0, 128 - N_GROUPS - N_EXP)))
        b_route =jnp.pad(jnp.concatenate([moe_b_gr[l], moe_b_er[l].reshape(N_EXP)]),
                          (0, 128 - N_GROUPS - N_EXP)).reshape(1, 128)
        x1, xloc, route, counts = _out_proj(a, hyb, att, w_out[l].astype(BF16), x, mod[l],
                                            ln1_g[l].reshape(1, D), ln1_b[l].reshape(1, D),
                                            w_route, b_route, tri, upper)
        cnt = counts.reshape(N_TILES, 8, 128)[:, 0, :N_EXP].astype(jnp.int32)
        expert_tables, nblk_loc, gsrc = _dispatch_tables(cnt)
        y = _experts(l, expert_tables, xloc, moe_w_gate, moe_w_up, moe_w_down)
        x = _combine(nblk_loc, gsrc, y, route, x1, mod[l], ln2_g[l].reshape(1, D),
                     ln2_b[l].reshape(1, D), split=(l == DEPTH - 1))

    y_prompt = x[0].reshape(N_CTX_B, CTX_L, D)
    y_sample = x[1].reshape(N_LAT_B, LAT_L, D)
    return (y_prompt, y_sample, jnp.stack(ckv_states, axis=1), jnp.stack(kpe_states, axis=1))
```

```python
import functools
import math

import numpy as np
import jax
import jax.numpy as jnp
from jax import lax
from jax.experimental import pallas as pl
from jax.experimental.pallas import tpu as pltpu

D = 1024
N_CTX_B, CTX_L = 16, 256
N_LAT_B, LAT_L = 4, 1024
DEPTH = 2
T_CTX = N_CTX_B * CTX_L
T_LAT = N_LAT_B * LAT_L
T = T_CTX + T_LAT
PAST = 256
GRID_W = 64

GM_HEADS, GM_HD, GM_W, CHUNK = 4, 64, 256, 128
HY_W, HY_EMB, HY_FH = 256, 33, 64
NOPE, ROPE, VD, HEADS = 64, 32, 64, 8
Q_LORA, KV_LORA = 256, 128
SLOT = 128
N_GROUPS, EPG, N_EXP, D_EXP = 4, 8, 32, 256
ALPHA = (2.0 * DEPTH) ** 0.25
LN_EPS, RMS_EPS = 1e-5, 1e-6
ROPE_THETA = 10000.0

TM = 256
N_TILES = T // TM
CTX_TILES = T_CTX // TM
LAT_TILES_PER_B = LAT_L // TM
N_ASSIGN = 2 * T
BLK = 16
BLK_PER_TILE = TM // BLK
LOC = 2 * TM + N_EXP * BLK
LOC_BLKS = LOC // BLK
ZERO_BLK = LOC_BLKS - 1
MOE_TILES = (N_ASSIGN + N_TILES * N_EXP * (BLK - 1)) // TM + N_EXP
MOE_ROWS = MOE_TILES * TM
IN_EXT = 1920

F32, BF16 = jnp.float32, jnp.bfloat16
VMEM_LIMIT = 52 * 1024 * 1024


def _cparams(*sem):
    return pltpu.CompilerParams(dimension_semantics=sem, vmem_limit_bytes=VMEM_LIMIT)


def _cond_of_tile(i):
    return jnp.where(i < CTX_TILES, 0, 1 + (i - CTX_TILES) // LAT_TILES_PER_B)


def _rope_block_of_tile(i):
    return jnp.where(i < CTX_TILES, 0, 1 + (i - CTX_TILES) % LAT_TILES_PER_B)


def _full(shape):
    n = len(shape)
    return pl.BlockSpec(shape, lambda *_: (0,) * n)


def _pair_specs(n):
    return [pl.BlockSpec((TM, n), lambda i, *_: (jnp.minimum(i, CTX_TILES - 1), 0)),
            pl.BlockSpec((TM, n), lambda i, *_: (jnp.maximum(i - CTX_TILES, 0), 0))]


def _pair_read(c_ref, l_ref):
    return jnp.where(pl.program_id(0) < CTX_TILES, c_ref[...], l_ref[...])


def _layer_norm(x, g, b):
    mu = jnp.mean(x, -1, keepdims=True)
    xc = x - mu
    var = jnp.mean(xc * xc, -1, keepdims=True)
    return xc * lax.rsqrt(var + LN_EPS) * g + b


def _rms_norm(x, g):
    return x * lax.rsqrt(jnp.mean(x * x, -1, keepdims=True) + RMS_EPS) * g


def _bdot(a, b):
    return jnp.dot(a.astype(BF16), b.astype(BF16), preferred_element_type=F32)


def _split3_dot(x, w_bf16):
    hi = x.astype(BF16)
    r1 = x - hi.astype(F32)
    mid = r1.astype(BF16)
    lo = (r1 - mid.astype(F32)).astype(BF16)
    dot = functools.partial(jnp.dot, preferred_element_type=F32)
    return dot(hi, w_bf16) + dot(mid, w_bf16) + dot(lo, w_bf16)


def _dft_tables(L):
    f = np.arange(L, dtype=np.int64)
    ft = np.outer(f, f) % (2 * L)
    ang = np.pi * ft / L
    c = np.cos(ang)
    s = np.sin(ang)
    alt = np.where(f % 2 == 0, 1.0, -1.0)
    sf = s.copy()
    sf[0, :] = alt
    return (jnp.asarray(c, F32).astype(BF16), jnp.asarray(sf, F32).astype(BF16),
            jnp.asarray(sf.T.copy(), F32).astype(BF16), jnp.asarray(alt[:, None], F32))


def _rope_tables():
    rows = LAT_L // GRID_W
    row = np.repeat(np.arange(rows, dtype=np.float64), GRID_W)
    col = np.tile(np.arange(GRID_W, dtype=np.float64), rows)
    n = ROPE // 4
    inv = ROPE_THETA ** (-np.arange(n, dtype=np.float64) / n)
    ang = np.concatenate([row[:, None] * inv, col[:, None] * inv], -1)
    cos = np.zeros((CTX_L + LAT_L, SLOT))
    sin = np.zeros((CTX_L + LAT_L, SLOT))
    cos[:, :NOPE + ROPE] = 1.0
    cos[CTX_L:, NOPE:NOPE + ROPE:2] = np.cos(ang)
    cos[CTX_L:, NOPE + 1:NOPE + ROPE:2] = np.cos(ang)
    sin[CTX_L:, NOPE:NOPE + ROPE:2] = -np.sin(ang)
    sin[CTX_L:, NOPE + 1:NOPE + ROPE:2] = np.sin(ang)
    scale = 1.0 / math.sqrt(NOPE + ROPE)
    return (jnp.asarray(cos * scale, F32), jnp.asarray(sin * scale, F32),
            jnp.asarray(cos, F32), jnp.asarray(sin, F32))


def _hyena_positions(L):
    t = jnp.linspace(0.0, 1.0, L, dtype=F32)[:, None]
    bands = (HY_EMB - 1) // 2
    w = 2.0 * math.pi * jnp.arange(L, dtype=F32)[:, None] / L
    f = jnp.linspace(1e-4, bands - 1, bands, dtype=F32)[None]
    z = jnp.concatenate([t, jnp.cos(f * w), -jnp.sin(f * w)], -1)
    z = jnp.pad(z, ((0, 0), (0, 128 - HY_EMB)))
    min_decay = math.log(1e-2) / 1.5
    max_decay = math.log(1e-2) / 0.3
    deltas = jnp.abs(jnp.linspace(min_decay, max_decay, HY_W, dtype=F32))
    return z, jnp.exp(-t * deltas)


def _mod_kernel(c_ref, w_ref, b_ref, o_ref):
    c = c_ref[...]
    s = c / (1.0 + jnp.exp(-c))
    o_ref[0] = _bdot(s, w_ref[0]) + b_ref[0]


def _modulation(cond8, w_mod, b_mod):
    tn = 1536
    return pl.pallas_call(
        _mod_kernel,
        out_shape=jax.ShapeDtypeStruct((DEPTH, 8, 6 * D), F32),
        grid=(DEPTH, 6 * D // tn),
        in_specs=[pl.BlockSpec((8, D), lambda l, j: (0, 0)),
                  pl.BlockSpec((1, D, tn), lambda l, j: (l, 0, j)),
                  pl.BlockSpec((1, 1, tn), lambda l, j: (l, 0, j))],
        out_specs=pl.BlockSpec((1, 8, tn), lambda l, j: (l, 0, j)),
        compiler_params=_cparams("parallel", "parallel"),
        name="modulation",
    )(cond8, w_mod, b_mod.reshape(DEPTH, 1, 6 * D))


def _filter_kernel(z_ref, w1_ref, b1_ref, w2_ref, b2_ref, w3_ref, fr_ref, dec_ref, c_ref, sf_ref,
                   alt_ref, kr_ref, ki_ref, krn_ref, *, L):
    hdot = functools.partial(jnp.dot, preferred_element_type=F32, precision=lax.Precision.HIGHEST)
    h1 = jnp.sin(fr_ref[0:1] * (hdot(z_ref[...], w1_ref[...]) + b1_ref[...]))
    h2 = jnp.sin(fr_ref[1:2] * (hdot(h1, w2_ref[...]) + b2_ref[...]))
    filt = hdot(h2, w3_ref[...])
    row = lax.broadcasted_iota(jnp.int32, (L, HY_W), 0)
    dec = dec_ref[...]
    for o in range(2):
        fwd = filt[:, o * 2 * HY_W:o * 2 * HY_W + HY_W] * dec
        bwd = jnp.where(row == 0, 0.0, filt[:, o * 2 * HY_W + HY_W:(o + 1) * 2 * HY_W] * dec)
        sm = fwd + bwd
        df = fwd - bwd
        kr = _bdot(c_ref[...], sm)
        ki = -_bdot(sf_ref[...], df)
        kr_nyq = jnp.sum(alt_ref[...] * sm, axis=0, keepdims=True)
        kr_ref[o] = jnp.where(row == 0, kr * (0.5 / L), kr * (1.0 / L))
        ki_ref[o] = jnp.where(row == 0, 0.0, ki * (1.0 / L))
        krn_ref[o] = jnp.where(row == 0, kr_nyq * (0.5 / L), kr * (1.0 / L))


def _filter_spectra(L, z, dec, dft, w1, b1, w2, b2, w3, freq):
    c, sf, _, alt = dft
    pad = 128 - HY_FH
    w1p = jnp.pad(w1, ((0, 128 - HY_EMB), (0, pad)))
    w2p = jnp.pad(w2, ((0, pad), (0, pad)))
    w3p = jnp.pad(w3, ((0, pad), (0, 0)))
    b1p = jnp.pad(b1, (0, pad)).reshape(1, 128)
    b2p = jnp.pad(b2, (0, pad)).reshape(1, 128)
    frp = jnp.pad(freq, ((0, 0), (0, pad)))
    shp = jax.ShapeDtypeStruct((2, L, HY_W), F32)
    args = (z, w1p, b1p, w2p, b2p, w3p, frp, dec, c, sf, alt)
    return pl.pallas_call(
        functools.partial(_filter_kernel, L=L),
        out_shape=(shp, shp, shp),
        grid=(1,),
        in_specs=[_full(a.shape) for a in args],
        out_specs=(_full((2, L, HY_W)),) * 3,
        compiler_params=_cparams("arbitrary"),
        name=f"hyena_filter_{L}",
    )(*args)


def _in_kernel(xc_ref, xl_ref, ng_ref, nb_ref, mod_ref, w_ref, gg_ref, gb_ref, avg_ref, ws_ref, bs_ref,
               gq_ref, gkv_ref, wuq_ref, wuk_ref, wuv_ref, cq_ref, sq_ref, ck_ref, sk_ref,
               a_ref, hy_ref, q_ref, k_ref, v_ref, ckv_ref, kpe_ref, *, pre_norm):
    x = _pair_read(xc_ref, xl_ref)
    if pre_norm:
        x = _layer_norm(x, ng_ref[...], nb_ref[...])
    m = mod_ref[0]
    h = x * (1.0 + m[1:2]) + m[0:1]
    proj = _bdot(h, w_ref[...])

    gu = jax.nn.gelu(proj[:, 0:GM_W], approximate=True)
    gv = jax.nn.gelu(proj[:, GM_W:2 * GM_W], approximate=True)
    avg = avg_ref[...]
    mu = _split3_dot(gv, avg)
    vc = gv - mu
    var = _split3_dot(vc * vc, avg)
    vln = (vc * lax.rsqrt(var + LN_EPS) * gg_ref[...] + gb_ref[...]).astype(BF16)
    lane = lax.broadcasted_iota(jnp.int32, (CHUNK, GM_W), 1)
    for c in range(TM // CHUNK):
        vchunk = vln[c * CHUNK:(c + 1) * CHUNK]
        s = bs_ref[...]
        for hd in range(GM_HEADS):
            sh = jnp.dot(ws_ref[hd], vchunk, preferred_element_type=F32)
            s = s + jnp.where(lane // GM_HD == hd, sh, 0.0)
        a_ref[c * CHUNK:(c + 1) * CHUNK, :] = (gu[c * CHUNK:(c + 1) * CHUNK] * s).astype(BF16)

    hy_ref[...] = proj[:, 512:1280]

    cq = _rms_norm(proj[:, 1280:1536], gq_ref[...])
    qq = _bdot(cq, wuq_ref[...])
    cos_q = jnp.concatenate([cq_ref[...]] * HEADS, axis=1)
    sin_q = jnp.concatenate([sq_ref[...]] * HEADS, axis=1)
    q_ref[...] = (qq[:, :HEADS * SLOT] * cos_q + qq[:, HEADS * SLOT:] * sin_q).astype(BF16)

    ckv = _rms_norm(proj[:, 1536:1664], gkv_ref[...])
    ckv_ref[...] = ckv
    kpe = proj[:, 1664:1792]
    kpe_ref[...] = kpe
    krot = kpe * ck_ref[...] + proj[:, 1792:1920] * sk_ref[...]
    kn = _bdot(ckv, wuk_ref[...])
    k_ref[...] = (kn + jnp.concatenate([krot] * HEADS, axis=1)).astype(BF16)
    v_ref[...] = _bdot(ckv, wuv_ref[...]).astype(BF16)


def _in_proj(x_pair, norm, pre_norm, mod_l, w_in_ext, gm_g, gm_b, avg, ws, bs_full, gq, gkv, wuq_arr,
             wuk_arr, wuv_arr, rope):
    cos_q, sin_q, cos_k, sin_k = rope
    tile = lambda n: pl.BlockSpec((TM, n), lambda i: (i, 0))
    rope_spec = pl.BlockSpec((TM, SLOT), lambda i: (_rope_block_of_tile(i), 0))
    out_shapes = (jax.ShapeDtypeStruct((T, GM_W), BF16),
                  jax.ShapeDtypeStruct((T, 3 * HY_W), F32),
                  jax.ShapeDtypeStruct((T, HEADS * SLOT), BF16),
                  jax.ShapeDtypeStruct((T, HEADS * SLOT), BF16),
                  jax.ShapeDtypeStruct((T, HEADS * VD), BF16),
                  jax.ShapeDtypeStruct((T, KV_LORA), F32),
                  jax.ShapeDtypeStruct((T, SLOT), F32))
    return pl.pallas_call(
        functools.partial(_in_kernel, pre_norm=pre_norm),
        out_shape=out_shapes,
        grid=(N_TILES,),
        in_specs=_pair_specs(D) + [
                  _full((1, D)), _full((1, D)),
                  pl.BlockSpec((1, 6, D), lambda i: (_cond_of_tile(i), 0, 0)),
                  _full((D, IN_EXT)), _full((1, GM_W)), _full((1, GM_W)), _full((GM_W, GM_W)),
                  _full((GM_HEADS, CHUNK, CHUNK)), _full((CHUNK, GM_W)),
                  _full((1, Q_LORA)), _full((1, KV_LORA)),
                  _full((Q_LORA, 2 * HEADS * SLOT)), _full((KV_LORA, HEADS * SLOT)),
                  _full((KV_LORA, HEADS * VD)),
                  rope_spec, rope_spec, rope_spec, rope_spec],
        out_specs=(tile(GM_W), tile(3 * HY_W), tile(HEADS * SLOT), tile(HEADS * SLOT),
                   tile(HEADS * VD), tile(KV_LORA), tile(SLOT)),
        compiler_params=_cparams("parallel"),
        name="in_proj",
    )(*x_pair, *norm, mod_l, w_in_ext, gm_g, gm_b, avg, ws, bs_full, gq, gkv, wuq_arr, wuk_arr, wuv_arr,
      cos_q, sin_q, cos_k, sin_k)


def _cache_kernel(ckv_ref, kpe_ref, wuk_ref, wuv_ref, k_ref, v_ref):
    ckv = ckv_ref[0, 0]
    kn = _bdot(ckv, wuk_ref[0])
    k_ref[0, 0] = (kn + jnp.concatenate([kpe_ref[0, 0]] * HEADS, axis=1)).astype(BF16)
    v_ref[0, 0] = _bdot(ckv, wuv_ref[0]).astype(BF16)


def _cache_kv(cache_ckv, cache_kpe_slot, wuk_arr, wuv_arr):
    return pl.pallas_call(
        _cache_kernel,
        out_shape=(jax.ShapeDtypeStruct((DEPTH, N_LAT_B, PAST, HEADS * SLOT), BF16),
                   jax.ShapeDtypeStruct((DEPTH, N_LAT_B, PAST, HEADS * VD), BF16)),
        grid=(DEPTH, N_LAT_B),
        in_specs=[pl.BlockSpec((1, 1, PAST, KV_LORA), lambda l, b: (b, l, 0, 0)),
                  pl.BlockSpec((1, 1, PAST, SLOT), lambda l, b: (b, l, 0, 0)),
                  pl.BlockSpec((1, KV_LORA, HEADS * SLOT), lambda l, b: (l, 0, 0)),
                  pl.BlockSpec((1, KV_LORA, HEADS * VD), lambda l, b: (l, 0, 0))],
        out_specs=(pl.BlockSpec((1, 1, PAST, HEADS * SLOT), lambda l, b: (l, b, 0, 0)),
                   pl.BlockSpec((1, 1, PAST, HEADS * VD), lambda l, b: (l, b, 0, 0))),
        compiler_params=_cparams("parallel", "parallel"),
        name="cache_kv",
    )(cache_ckv, cache_kpe_slot, wuk_arr, wuv_arr)


def _attend(q_ref, segments, o_ref):
    lq = q_ref.shape[0]
    lane = lax.broadcasted_iota(jnp.int32, (lq, 2 * VD), 1)
    nt = (((1,), (1,)), ((), ()))
    for pair in range(HEADS // 2):
        outs = []
        for hd in (2 * pair, 2 * pair + 1):
            qh = q_ref[:, hd * SLOT:(hd + 1) * SLOT]
            scores = [lax.dot_general(qh, k_ref[:, hd * SLOT:(hd + 1) * SLOT], nt,
                                      preferred_element_type=F32) for k_ref, _ in segments]
            mx = functools.reduce(jnp.maximum, [jnp.max(s, -1, keepdims=True) for s in scores])
            ps = [jnp.exp(s - mx) for s in scores]
            den = functools.reduce(jnp.add, [jnp.sum(p, -1, keepdims=True) for p in ps])
            acc = functools.reduce(jnp.add, [
                jnp.dot(p.astype(BF16), v_ref[:, pair * 2 * VD:(pair + 1) * 2 * VD],
                        preferred_element_type=F32) for p, (_, v_ref) in zip(ps, segments)])
            outs.append(acc / den)
        o_ref[:, pair * 2 * VD:(pair + 1) * 2 * VD] = jnp.where(lane < VD, outs[0], outs[1]).astype(BF16)


def _attn_ctx_kernel(q_ref, k_ref, v_ref, o_ref):
    _attend(q_ref, [(k_ref, v_ref)], o_ref)


def _attn_lat_kernel(q_ref, kc_ref, vc_ref, k_ref, v_ref, o_ref):
    _attend(q_ref, [(kc_ref.at[0], vc_ref.at[0]), (k_ref, v_ref)], o_ref)


def _attention(q, k, v, kc, vc):
    kw, vw = HEADS * SLOT, HEADS * VD
    ctx = pl.pallas_call(
        _attn_ctx_kernel,
        out_shape=jax.ShapeDtypeStruct((T_CTX, vw), BF16),
        grid=(N_CTX_B,),
        in_specs=[pl.BlockSpec((CTX_L, kw), lambda b: (b, 0)),
                  pl.BlockSpec((CTX_L, kw), lambda b: (b, 0)),
                  pl.BlockSpec((CTX_L, vw), lambda b: (b, 0))],
        out_specs=pl.BlockSpec((CTX_L, vw), lambda b: (b, 0)),
        compiler_params=_cparams("parallel"),
        name="attn_ctx",
    )(q, k, v)
    nq = LAT_L // TM
    off = T_CTX // LAT_L
    lat = pl.pallas_call(
        _attn_lat_kernel,
        out_shape=jax.ShapeDtypeStruct((T_LAT, vw), BF16),
        grid=(N_LAT_B, nq),
        in_specs=[pl.BlockSpec((TM, kw), lambda b, j: (CTX_TILES + b * nq + j, 0)),
                  pl.BlockSpec((1, PAST, kw), lambda b, j: (b, 0, 0)),
                  pl.BlockSpec((1, PAST, vw), lambda b, j: (b, 0, 0)),
                  pl.BlockSpec((LAT_L, kw), lambda b, j: (off + b, 0)),
                  pl.BlockSpec((LAT_L, vw), lambda b, j: (off + b, 0))],
        out_specs=pl.BlockSpec((TM, vw), lambda b, j: (b * nq + j, 0)),
        compiler_params=_cparams("parallel", "parallel"),
        name="attn_lat",
    )(q, kc, vc, k, v)
    return ctx, lat


def _hyena_kernel(hy_ref, cw_ref, cb_ref, c_ref, sf_ref, sb_ref, kr_ref, ki_ref, krn_ref, hb_ref,
                  o_ref, *, L):
    x = hy_ref[...]
    row = lax.broadcasted_iota(jnp.int32, x.shape, 0)
    prev = jnp.where(row == 0, 0.0, pltpu.roll(x, 1, 0))
    nxt = jnp.where(row == L - 1, 0.0, pltpu.roll(x, L - 1, 0))
    z = prev * cw_ref[0:1] + x * cw_ref[1:2] + nxt * cw_ref[2:3] + cb_ref[...]
    y = z[:, 2 * HY_W:]
    for o in range(2):
        gate = z[:, o * HY_W:(o + 1) * HY_W]
        yb = y.astype(BF16)
        a_re = jnp.dot(c_ref[...], yb, preferred_element_type=F32)
        a_im = jnp.dot(sf_ref[...], yb, preferred_element_type=F32)
        ki = ki_ref[o]
        z_re = a_re * kr_ref[o] + a_im * ki
        z_im = a_im * krn_ref[o] - a_re * ki
        conv = (jnp.dot(c_ref[...], z_re.astype(BF16), preferred_element_type=F32)
                + jnp.dot(sb_ref[...], z_im.astype(BF16), preferred_element_type=F32))
        y = gate * (conv + y * hb_ref[o:o + 1])
    o_ref[...] = y.astype(BF16)


def _hyena_group(hy, L, nb, blk0, dft, spectra, conv_w, conv_b, hy_bias):
    c, sf, sb, _ = dft
    kr, ki, krn = spectra
    return pl.pallas_call(
        functools.partial(_hyena_kernel, L=L),
        out_shape=jax.ShapeDtypeStruct((nb * L, HY_W), BF16),
        grid=(nb,),
        in_specs=[pl.BlockSpec((L, 3 * HY_W), lambda b: (blk0 + b, 0)),
                  _full((3, 3 * HY_W)), _full((1, 3 * HY_W)),
                  _full((L, L)), _full((L, L)), _full((L, L)),
                  _full((2, L, HY_W)), _full((2, L, HY_W)), _full((2, L, HY_W)),
                  _full((2, HY_W))],
        out_specs=pl.BlockSpec((L, HY_W), lambda b: (b, 0)),
        compiler_params=_cparams("parallel"),
        name=f"hyena_{L}",
    )(hy, conv_w, conv_b.reshape(1, 3 * HY_W), c, sf, sb, kr, ki, krn, hy_bias)


def _out_kernel(a_ref, bc_ref, bl_ref, mc_ref, ml_ref, w_ref, xc_ref, xl_ref, ng_ref, nb_ref, mod_ref,
                g_ref, be_ref, wr_ref, br_ref, tri_ref, upper_ref, x1_ref, xloc_ref, route_ref, cnt_ref,
                *, pre_norm):
    dot = functools.partial(jnp.dot, preferred_element_type=F32)
    mixed = (dot(a_ref[...], w_ref[0:GM_W]) + dot(_pair_read(bc_ref, bl_ref), w_ref[GM_W:GM_W + HY_W])
             + dot(_pair_read(mc_ref, ml_ref), w_ref[GM_W + HY_W:]))
    x = _pair_read(xc_ref, xl_ref)
    if pre_norm:
        x = _layer_norm(x, ng_ref[...], nb_ref[...])
    m = mod_ref[0]
    x1 = _layer_norm(ALPHA * x + m[2:3] * mixed, g_ref[...], be_ref[...])
    x1_ref[...] = x1
    h2 = x1 * (1.0 + m[4:5]) + m[3:4]

    h_hi = h2.astype(BF16)
    h_lo = (h2 - h_hi.astype(F32)).astype(BF16)
    w_hi = wr_ref[...].astype(BF16)
    w_lo = (wr_ref[...] - w_hi.astype(F32)).astype(BF16)
    part = dot(h_hi, jnp.concatenate([w_hi, w_lo], axis=1))
    logits = part[:, :128] + part[:, 128:] + dot(h_lo, w_hi) + br_ref[...]
    lane = lax.broadcasted_iota(jnp.int32, logits.shape, 1)
    lanef = lane.astype(F32)
    big = jnp.float32(1e9)
    ninf = jnp.float32(-jnp.inf)
    is_g = lane < N_GROUPS
    gl = jnp.where(is_g, logits, ninf)
    gmax = jnp.max(gl, -1, keepdims=True)
    gidx = jnp.min(jnp.where(gl == gmax, lanef, big), -1, keepdims=True)
    gw = 1.0 / jnp.sum(jnp.where(is_g, jnp.exp(logits - gmax), 0.0), -1, keepdims=True)
    ex = lane - N_GROUPS
    in_group = (ex >= 0) & (ex < N_EXP) & ((ex // EPG).astype(F32) == gidx)
    el = jnp.where(in_group, logits, ninf)
    v1 = jnp.max(el, -1, keepdims=True)
    i1 = jnp.min(jnp.where(el == v1, lanef, big), -1, keepdims=True)
    el2 = jnp.where(lanef == i1, ninf, el)
    v2 = jnp.max(el2, -1, keepdims=True)
    i2 = jnp.min(jnp.where(el2 == v2, lanef, big), -1, keepdims=True)
    e21 = jnp.exp(v2 - v1)
    w1 = gw / (1.0 + e21)
    w2 = gw * e21 / (1.0 + e21)
    e1 = i1 - N_GROUPS
    e2 = i2 - N_GROUPS

    oh1 = jnp.where(lanef == e1, 1.0, 0.0)
    oh2 = jnp.where(lanef == e2, 1.0, 0.0)
    ex1 = dot(tri_ref[...], oh1.astype(BF16))
    ex2 = dot(tri_ref[...], oh2.astype(BF16))
    col1 = jnp.sum(oh1, axis=0, keepdims=True)
    col2 = jnp.sum(oh2, axis=0, keepdims=True)
    n = col1 + col2
    run = jnp.floor((n + (BLK - 1.0)) * (1.0 / BLK)) * BLK
    start = dot(jnp.broadcast_to(run, (8, 128)).astype(BF16), upper_ref[...])[0:1]
    loc1 = jnp.sum(oh1 * (start + ex1), -1, keepdims=True)
    loc2 = jnp.sum(oh2 * (start + col1 + ex2), -1, keepdims=True)
    cnt_ref[...] = jnp.broadcast_to(n, cnt_ref.shape)

    route = jnp.zeros_like(logits)
    for j, val in enumerate((e1, e2, w1, w2, loc1, loc2)):
        route = jnp.where(lane == j, val, route)
    route_ref[...] = route

    loc1_row = jnp.transpose(jnp.broadcast_to(loc1, (TM, 128)))[0:1].astype(jnp.int32)
    loc2_row = jnp.transpose(jnp.broadcast_to(loc2, (TM, 128)))[0:1].astype(jnp.int32)
    slot = lax.broadcasted_iota(jnp.int32, (LOC, TM), 0)
    perm = jnp.where((slot == loc1_row) | (slot == loc2_row), 1.0, 0.0).astype(BF16)
    xloc_ref[...] = dot(perm, h_hi).astype(BF16)


def _out_proj(a, hyb_pair, att_pair, w_out, x_pair, norm, pre_norm, mod_l, g, b, w_route, b_route, tri,
              upper):
    tile = lambda n: pl.BlockSpec((TM, n), lambda i: (i, 0))
    return pl.pallas_call(
        functools.partial(_out_kernel, pre_norm=pre_norm),
        out_shape=(jax.ShapeDtypeStruct((T, D), F32), jax.ShapeDtypeStruct((N_TILES * LOC, D), BF16),
                   jax.ShapeDtypeStruct((T, 128), F32), jax.ShapeDtypeStruct((N_TILES * 8, 128), F32)),
        grid=(N_TILES,),
        in_specs=[tile(GM_W)] + _pair_specs(HY_W) + _pair_specs(HEADS * VD) + [_full((D, D))]
        + _pair_specs(D) + [
                  _full((1, D)), _full((1, D)),
                  pl.BlockSpec((1, 6, D), lambda i: (_cond_of_tile(i), 0, 0)),
                  _full((1, D)), _full((1, D)), _full((D, 128)), _full((1, 128)), _full((TM, TM)),
                  _full((128, 128))],
        out_specs=(tile(D), pl.BlockSpec((LOC, D), lambda i: (i, 0)), tile(128),
                   pl.BlockSpec((8, 128), lambda i: (i, 0))),
        compiler_params=_cparams("parallel"),
        name="out_proj_route",
    )(a, *hyb_pair, *att_pair, w_out, *x_pair, *norm, mod_l, g, b, w_route, b_route, tri, upper)


def _expert_kernel(te_ref, tv_ref, first_ref, ws_ref, nxt_ref, src_ref, x_hbm, wg_hbm, wu_hbm, wd_hbm,
                   y_ref, xbuf, wg_f, wu_f, wd_f, wgu_b, wd_b, sem, wsem, *, layer):
    j = pl.program_id(0)
    slot = j % 2

    def block_copies(t, s):
        return [pltpu.make_async_copy(
            x_hbm.at[pl.ds(pl.multiple_of(src_ref[t * BLK_PER_TILE + b] * BLK, BLK), BLK), :],
            xbuf.at[s, pl.ds(b * BLK, BLK), :], sem.at[s]) for b in range(BLK_PER_TILE)]

    def weight_copies(e, s):
        return [pltpu.make_async_copy(w_hbm.at[layer, e], w_f.at[s], wsem.at[s])
                for w_hbm, w_f in ((wg_hbm, wg_f), (wu_hbm, wu_f), (wd_hbm, wd_f))]

    @pl.when(j == 0)
    def _():
        for cp in weight_copies(te_ref[0], 0):
            cp.start()
        for cp in block_copies(0, 0):
            cp.start()

    @pl.when(jnp.logical_and(j + 1 < MOE_TILES, tv_ref[jnp.minimum(j + 1, MOE_TILES - 1)] != 0))
    def _():
        for cp in block_copies(j + 1, 1 - slot):
            cp.start()

    @pl.when(first_ref[j] != 0)
    def _():
        ws = ws_ref[j]
        for cp in weight_copies(te_ref[j], ws):
            cp.wait()

        @pl.when(nxt_ref[j] >= 0)
        def _():
            for cp in weight_copies(nxt_ref[j], 1 - ws):
                cp.start(priority=1)

        wgu_b[:, 0:D_EXP] = wg_f[ws].astype(BF16)
        wgu_b[:, D_EXP:2 * D_EXP] = wu_f[ws].astype(BF16)
        wd_b[...] = wd_f[ws].astype(BF16)

    @pl.when(tv_ref[j] == 0)
    def _():
        y_ref[...] = jnp.zeros_like(y_ref)

    @pl.when(tv_ref[j] != 0)
    def _():
        for cp in block_copies(j, slot):
            cp.wait()
        gu = jnp.dot(xbuf[slot], wgu_b[...], preferred_element_type=F32)
        gate, up = gu[:, :D_EXP], gu[:, D_EXP:]
        hid = gate / (1.0 + jnp.exp(-gate)) * up
        y_ref[...] = jnp.dot(hid.astype(BF16), wd_b[...], preferred_element_type=F32).astype(BF16)


def _experts(l, tables, xloc, w_gate, w_up, w_down):
    grid_spec = pltpu.PrefetchScalarGridSpec(
        num_scalar_prefetch=len(tables),
        grid=(MOE_TILES,),
        in_specs=[pl.BlockSpec(memory_space=pl.ANY)] * 4,
        out_specs=pl.BlockSpec((TM, D), lambda j, *_: (j, 0)),
        scratch_shapes=[pltpu.VMEM((2, TM, D), BF16),
                        pltpu.VMEM((2, D, D_EXP), F32), pltpu.VMEM((2, D, D_EXP), F32),
                        pltpu.VMEM((2, D_EXP, D), F32),
                        pltpu.VMEM((D, 2 * D_EXP), BF16), pltpu.VMEM((D_EXP, D), BF16),
                        pltpu.SemaphoreType.DMA((2,)), pltpu.SemaphoreType.DMA((2,))],
    )
    return pl.pallas_call(
        functools.partial(_expert_kernel, layer=l),
        out_shape=jax.ShapeDtypeStruct((MOE_ROWS, D), BF16),
        grid_spec=grid_spec,
        compiler_params=_cparams("arbitrary"),
        name="experts",
    )(*tables, xloc, w_gate, w_up, w_down)


def _combine_kernel(nblk_ref, gsrc_ref, y_hbm, route_ref, x1_ref, mod_ref, g_ref, b_ref, oc_ref, ol_ref,
                    ybuf, sem):
    i = pl.program_id(0)

    slot_i = i % 2

    def block_copy(t, s, lb):
        src = pl.multiple_of(gsrc_ref[t * LOC_BLKS + lb] * BLK, BLK)
        return pltpu.make_async_copy(y_hbm.at[pl.ds(src, BLK), :],
                                     ybuf.at[s, pl.ds(pl.multiple_of(lb * BLK, BLK), BLK), :], sem.at[s])

    def fetch(t, s):
        def start(lb, c):
            block_copy(t, s, lb).start()
            return c
        lax.fori_loop(0, nblk_ref[t], start, 0)

    @pl.when(i == 0)
    def _():
        ybuf[...] = jnp.zeros_like(ybuf)
        fetch(0, 0)

    @pl.when(i + 1 < N_TILES)
    def _():
        fetch(jnp.minimum(i + 1, N_TILES - 1), 1 - slot_i)

    def wait(lb, c):
        block_copy(i, slot_i, lb).wait()
        return c

    lax.fori_loop(0, nblk_ref[i], wait, 0)
    route = route_ref[...]
    yb = ybuf[slot_i]
    slot = lax.broadcasted_iota(jnp.int32, (TM, LOC), 1)
    pick1 = jnp.where(slot == route[:, 4:5].astype(jnp.int32), 1.0, 0.0).astype(BF16)
    pick2 = jnp.where(slot == route[:, 5:6].astype(jnp.int32), 1.0, 0.0).astype(BF16)
    moe = (route[:, 2:3] * jnp.dot(pick1, yb, preferred_element_type=F32)
           + route[:, 3:4] * jnp.dot(pick2, yb, preferred_element_type=F32))
    m = mod_ref[0]
    res = _layer_norm(ALPHA * x1_ref[...] + m[5:6] * moe, g_ref[...], b_ref[...])

    @pl.when(i < CTX_TILES)
    def _():
        oc_ref[...] = res

    @pl.when(i >= CTX_TILES)
    def _():
        ol_ref[...] = res


def _combine(nblk, gsrc, y, route, x1, mod_l, g, b):
    out_shape = (jax.ShapeDtypeStruct((T_CTX, D), F32), jax.ShapeDtypeStruct((T_LAT, D), F32))
    out_specs = tuple(_pair_specs(D))
    grid_spec = pltpu.PrefetchScalarGridSpec(
        num_scalar_prefetch=2,
        grid=(N_TILES,),
        in_specs=[pl.BlockSpec(memory_space=pl.ANY),
                  pl.BlockSpec((TM, 128), lambda i, nb, gs: (i, 0)),
                  pl.BlockSpec((TM, D), lambda i, nb, gs: (i, 0)),
                  pl.BlockSpec((1, 6, D), lambda i, nb, gs: (_cond_of_tile(i), 0, 0)),
                  pl.BlockSpec((1, D), lambda i, nb, gs: (0, 0)),
                  pl.BlockSpec((1, D), lambda i, nb, gs: (0, 0))],
        out_specs=out_specs,
        scratch_shapes=[pltpu.VMEM((2, LOC, D), BF16), pltpu.SemaphoreType.DMA((2,))],
    )
    return pl.pallas_call(
        _combine_kernel,
        out_shape=out_shape,
        grid_spec=grid_spec,
        compiler_params=_cparams("arbitrary"),
        name="moe_combine",
    )(nblk, gsrc, y, route, x1, mod_l, g, b)


def _prefix_pick(starts, query, table):
    delta = table - jnp.concatenate([jnp.zeros_like(table[..., :1]), table[..., :-1]], axis=-1)
    return jnp.sum(jnp.where(starts <= query, delta, 0), axis=-1)


def _dispatch_tables(cnt):
    i32 = jnp.int32
    run = (cnt + BLK - 1) // BLK * BLK
    loc_start = jnp.cumsum(run, axis=1) - run
    nblk_loc = jnp.sum(run, axis=1) // BLK
    seg_rows = jnp.sum(run, axis=0)
    seg_tiles = (seg_rows + TM - 1) // TM
    tile_end = jnp.cumsum(seg_tiles)
    seg_start = (tile_end - seg_tiles) * TM
    glob_start = seg_start[None, :] + jnp.cumsum(run, axis=0) - run
    total = tile_end[-1]
    jt = jnp.arange(MOE_TILES, dtype=i32)
    tile_expert = jnp.sum(tile_end[None, :] <= jnp.minimum(jt, total - 1)[:, None], axis=1).astype(i32)
    tile_valid = (jt < total).astype(i32)
    prev_expert = jnp.concatenate([jnp.full((1,), -1, i32), tile_expert[:-1]])
    first = tile_valid * (tile_expert != prev_expert).astype(i32)
    ring_slot = ((jnp.cumsum(first) - 1) % 2).astype(i32)
    onehot_e = tile_expert[:, None] == jnp.arange(N_EXP, dtype=i32)[None, :]
    nxt_tile = jnp.sum(jnp.where(onehot_e, tile_end[None, :], 0), axis=1)
    nxt_expert = jnp.sum(jnp.where(nxt_tile[:, None] == jt[None, :], tile_expert[None, :], 0), axis=1)
    nxt = jnp.where(nxt_tile < total, nxt_expert, -1).astype(i32)

    g_blk = (glob_start.T.reshape(1, -1)) // BLK
    n_blk = (run.T.reshape(1, -1)) // BLK
    l_blk = ((jnp.arange(N_TILES, dtype=i32) * LOC)[None, :] + loc_start.T).reshape(1, -1) // BLK
    gb = jnp.arange(MOE_ROWS // BLK, dtype=i32)[:, None]
    off = gb[:, 0] - _prefix_pick(g_blk, gb, g_blk)
    src_blk = jnp.where(off < _prefix_pick(g_blk, gb, n_blk), _prefix_pick(g_blk, gb, l_blk) + off,
                        ZERO_BLK).astype(i32)

    pos = (jnp.arange(LOC_BLKS, dtype=i32) * BLK)[None, :, None]
    shift = _prefix_pick(loc_start[:, None, :], pos, (glob_start - loc_start)[:, None, :])
    gsrc = ((pos[:, :, 0] + shift) // BLK).astype(i32)
    expert_tables = (tile_expert, tile_valid, first, ring_slot, nxt, src_blk)
    return expert_tables, nblk_loc.astype(i32), gsrc.reshape(-1)


def _swap_pairs(w):
    return w.reshape(w.shape[:-1] + (ROPE // 2, 2))[..., ::-1].reshape(w.shape)


def _rope_slot(pe):
    return jnp.pad(pe, [(0, 0)] * (pe.ndim - 1) + [(NOPE, SLOT - NOPE - ROPE)])


def _w_in_layout(w):
    pe = w[:, 1664:]
    return jnp.concatenate([w[:, :1664], _rope_slot(pe), _rope_slot(_swap_pairs(pe))], axis=1).astype(BF16)


def _wuq_layout(w):
    w = w.reshape(Q_LORA, HEADS, NOPE + ROPE)
    nope, pe = w[..., :NOPE], w[..., NOPE:]
    plain = jnp.concatenate([nope, jnp.pad(pe, ((0, 0), (0, 0), (0, SLOT - NOPE - ROPE)))], axis=-1)
    swapped = _rope_slot(_swap_pairs(pe))
    return jnp.concatenate([plain.reshape(Q_LORA, HEADS * SLOT),
                            swapped.reshape(Q_LORA, HEADS * SLOT)], axis=1).astype(BF16)


def _wukv_layout(w):
    w = w.reshape(DEPTH, KV_LORA, HEADS, NOPE + VD)
    wk = jnp.pad(w[..., :NOPE], ((0, 0), (0, 0), (0, 0), (0, SLOT - NOPE)))
    return (wk.reshape(DEPTH, KV_LORA, HEADS * SLOT).astype(BF16),
            w[..., NOPE:].reshape(DEPTH, KV_LORA, HEADS * VD).astype(BF16))


def kernel(x_prompt, x_sample, c, cache_ckv, cache_kpe, c_ctx, ln_in_g, ln_in_b, w_mod, b_mod, w_in,
           gm_ln_g, gm_ln_b, gm_ws, gm_bs, hy_conv_w, hy_conv_b, hy_f_w1, hy_f_b1, hy_f_w2, hy_f_b2,
           hy_f_w3, hy_f_freq, hy_bias, mla_gq, mla_gkv, mla_wuq, mla_wukv, w_out, ln1_g, ln1_b,
           ln2_g, ln2_b, moe_w_gr, moe_b_gr, moe_w_er, moe_b_er, moe_w_gate, moe_w_up, moe_w_down):
    rope = _rope_tables()
    dft = {L: _dft_tables(L) for L in (CTX_L, LAT_L)}
    pos_tab = {L: _hyena_positions(L) for L in (CTX_L, LAT_L)}
    hd = np.arange(GM_W) // GM_HD
    avg = jnp.asarray((hd[:, None] == hd[None, :]) / GM_HD, BF16)
    tri = jnp.asarray(np.tril(np.ones((TM, TM)), -1), BF16)
    upper = jnp.asarray(np.triu(np.ones((128, 128)), 1), BF16)

    cond8 = jnp.concatenate([c_ctx[None], c, jnp.zeros((8 - 1 - N_LAT_B, D), F32)], axis=0)
    mod = _modulation(cond8, w_mod, b_mod).reshape(DEPTH, 8, 6, D)

    wuk_all, wuv_all = _wukv_layout(mla_wukv)
    kc_all, vc_all = _cache_kv(cache_ckv, _rope_slot(cache_kpe), wuk_all, wuv_all)

    x_pair = (x_prompt.reshape(T_CTX, D), x_sample.reshape(T_LAT, D))
    norm = (ln_in_g.reshape(1, D), ln_in_b.reshape(1, D))
    ckv_states, kpe_states = [], []
    for l in range(DEPTH):
        w_in_ext = _w_in_layout(w_in[l])
        wuq_arr = _wuq_layout(mla_wuq[l])
        bs_full = jnp.repeat(gm_bs[l].T, GM_HD, axis=1)
        a, hy, q, k, v, ckv, kpe = _in_proj(
            x_pair, norm, l == 0, mod[l], w_in_ext, gm_ln_g[l].reshape(1, GM_W), gm_ln_b[l].reshape(1, GM_W), avg,
            gm_ws[l].astype(BF16), bs_full, mla_gq[l].reshape(1, Q_LORA), mla_gkv[l].reshape(1, KV_LORA),
            wuq_arr, wuk_all[l], wuv_all[l], rope)
        ckv_states.append(ckv[:T_CTX].reshape(N_CTX_B, CTX_L, KV_LORA))
        kpe_states.append(kpe[:T_CTX, NOPE:NOPE + ROPE].reshape(N_CTX_B, CTX_L, ROPE))

        hyb = []
        for L, nb, blk0 in ((CTX_L, N_CTX_B, 0), (LAT_L, N_LAT_B, T_CTX // LAT_L)):
            z, dec = pos_tab[L]
            spectra = _filter_spectra(L, z, dec, dft[L], hy_f_w1[l], hy_f_b1[l], hy_f_w2[l],
                                      hy_f_b2[l], hy_f_w3[l], hy_f_freq[l])
            hyb.append(_hyena_group(hy, L, nb, blk0, dft[L], spectra, hy_conv_w[l], hy_conv_b[l],
                                    hy_bias[l]))

        att = _attention(q, k, v, kc_all[l], vc_all[l])

        w_route = jnp.pad(jnp.concatenate([moe_w_gr[l], moe_w_er[l].reshape(D, N_EXP)], axis=1),
                          ((0, 0), (0, 128 - N_GROUPS - N_EXP)))
        b_route =jnp.pad(jnp.concatenate([moe_b_gr[l], moe_b_er[l].reshape(N_EXP)]),
                          (0, 128 - N_GROUPS - N_EXP)).reshape(1, 128)
        x1, xloc, route, counts = _out_proj(a, hyb, att, w_out[l].astype(BF16), x_pair, norm, l == 0, mod[l],
                                            ln1_g[l].reshape(1, D), ln1_b[l].reshape(1, D),
                                            w_route, b_route, tri, upper)
        cnt = counts.reshape(N_TILES, 8, 128)[:, 0, :N_EXP].astype(jnp.int32)
        expert_tables, nblk_loc, gsrc = _dispatch_tables(cnt)
        y = _experts(l, expert_tables, xloc, moe_w_gate, moe_w_up, moe_w_down)
        x_pair = _combine(nblk_loc, gsrc, y, route, x1, mod[l], ln2_g[l].reshape(1, D), ln2_b[l].reshape(1, D))

    y_prompt = x_pair[0].reshape(N_CTX_B, CTX_L, D)
    y_sample = x_pair[1].reshape(N_LAT_B, LAT_L, D)
    return (y_prompt, y_sample, jnp.stack(ckv_states, axis=1), jnp.stack(kpe_states, axis=1))
```

```python
import functools
import math

import numpy as np
import jax
import jax.numpy as jnp
from jax import lax
from jax.experimental import pallas as pl
from jax.experimental.pallas import tpu as pltpu

D = 1024
N_CTX_B, CTX_L = 16, 256
N_LAT_B, LAT_L = 4, 1024
DEPTH = 2
T_CTX = N_CTX_B * CTX_L
T_LAT = N_LAT_B * LAT_L
T = T_CTX + T_LAT
PAST = 256
GRID_W = 64

GM_HEADS, GM_HD, GM_W, CHUNK = 4, 64, 256, 128
HY_W, HY_EMB, HY_FH = 256, 33, 64
NOPE, ROPE, VD, HEADS = 64, 32, 64, 8
Q_LORA, KV_LORA = 256, 128
SLOT = 128
N_GROUPS, EPG, N_EXP, D_EXP = 4, 8, 32, 256
ALPHA = (2.0 * DEPTH) ** 0.25
LN_EPS, RMS_EPS = 1e-5, 1e-6
ROPE_THETA = 10000.0

TM = 256
N_TILES = T // TM
CTX_TILES = T_CTX // TM
LAT_TILES_PER_B = LAT_L // TM
N_ASSIGN = 2 * T
BLK = 16
BLK_PER_TILE = TM // BLK
LOC = 2 * TM + N_EXP * BLK
LOC_BLKS = LOC // BLK
ZERO_BLK = LOC_BLKS - 1
MOE_TILES = (N_ASSIGN + N_TILES * N_EXP * (BLK - 1)) // TM + N_EXP
MOE_ROWS = MOE_TILES * TM
IN_EXT = 1920

F32, BF16 = jnp.float32, jnp.bfloat16
VMEM_LIMIT = 52 * 1024 * 1024


def _cparams(*sem):
    return pltpu.CompilerParams(dimension_semantics=sem, vmem_limit_bytes=VMEM_LIMIT)


def _cond_of_tile(i):
    return jnp.where(i < CTX_TILES, 0, 1 + (i - CTX_TILES) // LAT_TILES_PER_B)


def _rope_block_of_tile(i):
    return jnp.where(i < CTX_TILES, 0, 1 + (i - CTX_TILES) % LAT_TILES_PER_B)


def _full(shape):
    n = len(shape)
    return pl.BlockSpec(shape, lambda *_: (0,) * n)


def _pair_specs(n):
    return [pl.BlockSpec((TM, n), lambda i, *_: (jnp.minimum(i, CTX_TILES - 1), 0)),
            pl.BlockSpec((TM, n), lambda i, *_: (jnp.maximum(i - CTX_TILES, 0), 0))]


def _pair_read(c_ref, l_ref):
    return jnp.where(pl.program_id(0) < CTX_TILES, c_ref[...], l_ref[...])


def _layer_norm(x, g, b):
    mu = jnp.mean(x, -1, keepdims=True)
    xc = x - mu
    var = jnp.mean(xc * xc, -1, keepdims=True)
    return xc * lax.rsqrt(var + LN_EPS) * g + b


def _rms_norm(x, g):
    return x * lax.rsqrt(jnp.mean(x * x, -1, keepdims=True) + RMS_EPS) * g


def _bdot(a, b):
    return jnp.dot(a.astype(BF16), b.astype(BF16), preferred_element_type=F32)


def _split3_dot(x, w_bf16):
    hi = x.astype(BF16)
    r1 = x - hi.astype(F32)
    mid = r1.astype(BF16)
    lo = (r1 - mid.astype(F32)).astype(BF16)
    dot = functools.partial(jnp.dot, preferred_element_type=F32)
    return dot(hi, w_bf16) + dot(mid, w_bf16) + dot(lo, w_bf16)


def _dft_tables(L):
    f = np.arange(L, dtype=np.int64)
    ft = np.outer(f, f) % (2 * L)
    ang = np.pi * ft / L
    c = np.cos(ang)
    s = np.sin(ang)
    alt = np.where(f % 2 == 0, 1.0, -1.0)
    sf = s.copy()
    sf[0, :] = alt
    return (jnp.asarray(c, F32).astype(BF16), jnp.asarray(sf, F32).astype(BF16),
            jnp.asarray(sf.T.copy(), F32).astype(BF16), jnp.asarray(alt[:, None], F32))


def _rope_tables():
    rows = LAT_L // GRID_W
    row = np.repeat(np.arange(rows, dtype=np.float64), GRID_W)
    col = np.tile(np.arange(GRID_W, dtype=np.float64), rows)
    n = ROPE // 4
    inv = ROPE_THETA ** (-np.arange(n, dtype=np.float64) / n)
    ang = np.concatenate([row[:, None] * inv, col[:, None] * inv], -1)
    cos = np.zeros((CTX_L + LAT_L, SLOT))
    sin = np.zeros((CTX_L + LAT_L, SLOT))
    cos[:, :NOPE + ROPE] = 1.0
    cos[CTX_L:, NOPE:NOPE + ROPE:2] = np.cos(ang)
    cos[CTX_L:, NOPE + 1:NOPE + ROPE:2] = np.cos(ang)
    sin[CTX_L:, NOPE:NOPE + ROPE:2] = -np.sin(ang)
    sin[CTX_L:, NOPE + 1:NOPE + ROPE:2] = np.sin(ang)
    scale = 1.0 / math.sqrt(NOPE + ROPE)
    return (jnp.asarray(cos * scale, F32), jnp.asarray(sin * scale, F32),
            jnp.asarray(cos, F32), jnp.asarray(sin, F32))


def _hyena_positions(L):
    t = jnp.linspace(0.0, 1.0, L, dtype=F32)[:, None]
    bands = (HY_EMB - 1) // 2
    w = 2.0 * math.pi * jnp.arange(L, dtype=F32)[:, None] / L
    f = jnp.linspace(1e-4, bands - 1, bands, dtype=F32)[None]
    z = jnp.concatenate([t, jnp.cos(f * w), -jnp.sin(f * w)], -1)
    z = jnp.pad(z, ((0, 0), (0, 128 - HY_EMB)))
    min_decay = math.log(1e-2) / 1.5
    max_decay = math.log(1e-2) / 0.3
    deltas = jnp.abs(jnp.linspace(min_decay, max_decay, HY_W, dtype=F32))
    return z, jnp.exp(-t * deltas)


def _mod_kernel(c_ref, w_ref, b_ref, o_ref):
    c = c_ref[...]
    s = c / (1.0 + jnp.exp(-c))
    o_ref[0] = _bdot(s, w_ref[0]) + b_ref[0]


def _modulation(cond8, w_mod, b_mod):
    tn = 1536
    return pl.pallas_call(
        _mod_kernel,
        out_shape=jax.ShapeDtypeStruct((DEPTH, 8, 6 * D), F32),
        grid=(DEPTH, 6 * D // tn),
        in_specs=[pl.BlockSpec((8, D), lambda l, j: (0, 0)),
                  pl.BlockSpec((1, D, tn), lambda l, j: (l, 0, j)),
                  pl.BlockSpec((1, 1, tn), lambda l, j: (l, 0, j))],
        out_specs=pl.BlockSpec((1, 8, tn), lambda l, j: (l, 0, j)),
        compiler_params=_cparams("parallel", "parallel"),
        name="modulation",
    )(cond8, w_mod, b_mod.reshape(DEPTH, 1, 6 * D))


def _filter_kernel(z_ref, w1_ref, b1_ref, w2_ref, b2_ref, w3_ref, fr_ref, dec_ref, c_ref, sf_ref,
                   alt_ref, kr_ref, ki_ref, krn_ref, *, L):
    hdot = functools.partial(jnp.dot, preferred_element_type=F32, precision=lax.Precision.HIGHEST)
    h1 = jnp.sin(fr_ref[0:1] * (hdot(z_ref[...], w1_ref[...]) + b1_ref[...]))
    h2 = jnp.sin(fr_ref[1:2] * (hdot(h1, w2_ref[...]) + b2_ref[...]))
    filt = hdot(h2, w3_ref[...])
    row = lax.broadcasted_iota(jnp.int32, (L, HY_W), 0)
    dec = dec_ref[...]
    for o in range(2):
        fwd = filt[:, o * 2 * HY_W:o * 2 * HY_W + HY_W] * dec
        bwd = jnp.where(row == 0, 0.0, filt[:, o * 2 * HY_W + HY_W:(o + 1) * 2 * HY_W] * dec)
        sm = fwd + bwd
        df = fwd - bwd
        kr = _bdot(c_ref[...], sm)
        ki = -_bdot(sf_ref[...], df)
        kr_nyq = jnp.sum(alt_ref[...] * sm, axis=0, keepdims=True)
        kr_ref[o] = jnp.where(row == 0, kr * (0.5 / L), kr * (1.0 / L))
        ki_ref[o] = jnp.where(row == 0, 0.0, ki * (1.0 / L))
        krn_ref[o] = jnp.where(row == 0, kr_nyq * (0.5 / L), kr * (1.0 / L))


def _filter_spectra(L, z, dec, dft, w1, b1, w2, b2, w3, freq):
    c, sf, _, alt = dft
    pad = 128 - HY_FH
    w1p = jnp.pad(w1, ((0, 128 - HY_EMB), (0, pad)))
    w2p = jnp.pad(w2, ((0, pad), (0, pad)))
    w3p = jnp.pad(w3, ((0, pad), (0, 0)))
    b1p = jnp.pad(b1, (0, pad)).reshape(1, 128)
    b2p = jnp.pad(b2, (0, pad)).reshape(1, 128)
    frp = jnp.pad(freq, ((0, 0), (0, pad)))
    shp = jax.ShapeDtypeStruct((2, L, HY_W), F32)
    args = (z, w1p, b1p, w2p, b2p, w3p, frp, dec, c, sf, alt)
    return pl.pallas_call(
        functools.partial(_filter_kernel, L=L),
        out_shape=(shp, shp, shp),
        grid=(1,),
        in_specs=[_full(a.shape) for a in args],
        out_specs=(_full((2, L, HY_W)),) * 3,
        compiler_params=_cparams("arbitrary"),
        name=f"hyena_filter_{L}",
    )(*args)


def _in_kernel(xc_ref, xl_ref, ng_ref, nb_ref, mod_ref, w_ref, gg_ref, gb_ref, avg_ref, ws_ref, bs_ref,
               gq_ref, gkv_ref, wuq_ref, wuk_ref, wuv_ref, cq_ref, sq_ref, ck_ref, sk_ref,
               a_ref, hy_ref, q_ref, k_ref, v_ref, ckv_ref, kpe_ref, *, pre_norm):
    x = _pair_read(xc_ref, xl_ref)
    if pre_norm:
        x = _layer_norm(x, ng_ref[...], nb_ref[...])
    m = mod_ref[0]
    h = x * (1.0 + m[1:2]) + m[0:1]
    proj = _bdot(h, w_ref[...])

    gu = jax.nn.gelu(proj[:, 0:GM_W], approximate=True)
    gv = jax.nn.gelu(proj[:, GM_W:2 * GM_W], approximate=True)
    avg = avg_ref[...]
    mu = _split3_dot(gv, avg)
    vc = gv - mu
    var = _split3_dot(vc * vc, avg)
    vln = (vc * lax.rsqrt(var + LN_EPS) * gg_ref[...] + gb_ref[...]).astype(BF16)
    lane = lax.broadcasted_iota(jnp.int32, (CHUNK, GM_W), 1)
    for c in range(TM // CHUNK):
        vchunk = vln[c * CHUNK:(c + 1) * CHUNK]
        s = bs_ref[...]
        for hd in range(GM_HEADS):
            sh = jnp.dot(ws_ref[hd], vchunk, preferred_element_type=F32)
            s = s + jnp.where(lane // GM_HD == hd, sh, 0.0)
        a_ref[c * CHUNK:(c + 1) * CHUNK, :] = (gu[c * CHUNK:(c + 1) * CHUNK] * s).astype(BF16)

    hy_ref[...] = proj[:, 512:1280]

    cq = _rms_norm(proj[:, 1280:1536], gq_ref[...])
    qq = _bdot(cq, wuq_ref[...])
    cos_q = jnp.concatenate([cq_ref[...]] * HEADS, axis=1)
    sin_q = jnp.concatenate([sq_ref[...]] * HEADS, axis=1)
    q_ref[...] = (qq[:, :HEADS * SLOT] * cos_q + qq[:, HEADS * SLOT:] * sin_q).astype(BF16)

    ckv = _rms_norm(proj[:, 1536:1664], gkv_ref[...])
    ckv_ref[...] = ckv
    kpe = proj[:, 1664:1792]
    kpe_ref[...] = kpe
    krot = kpe * ck_ref[...] + proj[:, 1792:1920] * sk_ref[...]
    kn = _bdot(ckv, wuk_ref[...])
    k_ref[...] = (kn + jnp.concatenate([krot] * HEADS, axis=1)).astype(BF16)
    v_ref[...] = _bdot(ckv, wuv_ref[...]).astype(BF16)


def _in_proj(x_pair, norm, pre_norm, mod_l, w_in_ext, gm_g, gm_b, avg, ws, bs_full, gq, gkv, wuq_arr,
             wuk_arr, wuv_arr, rope):
    cos_q, sin_q, cos_k, sin_k = rope
    tile = lambda n: pl.BlockSpec((TM, n), lambda i: (i, 0))
    rope_spec = pl.BlockSpec((TM, SLOT), lambda i: (_rope_block_of_tile(i), 0))
    out_shapes = (jax.ShapeDtypeStruct((T, GM_W), BF16),
                  jax.ShapeDtypeStruct((T, 3 * HY_W), F32),
                  jax.ShapeDtypeStruct((T, HEADS * SLOT), BF16),
                  jax.ShapeDtypeStruct((T, HEADS * SLOT), BF16),
                  jax.ShapeDtypeStruct((T, HEADS * VD), BF16),
                  jax.ShapeDtypeStruct((T, KV_LORA), F32),
                  jax.ShapeDtypeStruct((T, SLOT), F32))
    return pl.pallas_call(
        functools.partial(_in_kernel, pre_norm=pre_norm),
        out_shape=out_shapes,
        grid=(N_TILES,),
        in_specs=_pair_specs(D) + [
                  _full((1, D)), _full((1, D)),
                  pl.BlockSpec((1, 6, D), lambda i: (_cond_of_tile(i), 0, 0)),
                  _full((D, IN_EXT)), _full((1, GM_W)), _full((1, GM_W)), _full((GM_W, GM_W)),
                  _full((GM_HEADS, CHUNK, CHUNK)), _full((CHUNK, GM_W)),
                  _full((1, Q_LORA)), _full((1, KV_LORA)),
                  _full((Q_LORA, 2 * HEADS * SLOT)), _full((KV_LORA, HEADS * SLOT)),
                  _full((KV_LORA, HEADS * VD)),
                  rope_spec, rope_spec, rope_spec, rope_spec],
        out_specs=(tile(GM_W), tile(3 * HY_W), tile(HEADS * SLOT), tile(HEADS * SLOT),
                   tile(HEADS * VD), tile(KV_LORA), tile(SLOT)),
        compiler_params=_cparams("parallel"),
        name="in_proj",
    )(*x_pair, *norm, mod_l, w_in_ext, gm_g, gm_b, avg, ws, bs_full, gq, gkv, wuq_arr, wuk_arr, wuv_arr,
      cos_q, sin_q, cos_k, sin_k)


def _cache_kernel(ckv_ref, kpe_ref, wuk_ref, wuv_ref, k_ref, v_ref):
    ckv = ckv_ref[0, 0]
    kn = _bdot(ckv, wuk_ref[0])
    k_ref[0, 0] = (kn + jnp.concatenate([kpe_ref[0, 0]] * HEADS, axis=1)).astype(BF16)
    v_ref[0, 0] = _bdot(ckv, wuv_ref[0]).astype(BF16)


def _cache_kv(cache_ckv, cache_kpe_slot, wuk_arr, wuv_arr):
    return pl.pallas_call(
        _cache_kernel,
        out_shape=(jax.ShapeDtypeStruct((DEPTH, N_LAT_B, PAST, HEADS * SLOT), BF16),
                   jax.ShapeDtypeStruct((DEPTH, N_LAT_B, PAST, HEADS * VD), BF16)),
        grid=(DEPTH, N_LAT_B),
        in_specs=[pl.BlockSpec((1, 1, PAST, KV_LORA), lambda l, b: (b, l, 0, 0)),
                  pl.BlockSpec((1, 1, PAST, SLOT), lambda l, b: (b, l, 0, 0)),
                  pl.BlockSpec((1, KV_LORA, HEADS * SLOT), lambda l, b: (l, 0, 0)),
                  pl.BlockSpec((1, KV_LORA, HEADS * VD), lambda l, b: (l, 0, 0))],
        out_specs=(pl.BlockSpec((1, 1, PAST, HEADS * SLOT), lambda l, b: (l, b, 0, 0)),
                   pl.BlockSpec((1, 1, PAST, HEADS * VD), lambda l, b: (l, b, 0, 0))),
        compiler_params=_cparams("parallel", "parallel"),
        name="cache_kv",
    )(cache_ckv, cache_kpe_slot, wuk_arr, wuv_arr)


def _attend(q_ref, segments, o_ref):
    lq = q_ref.shape[0]
    lane = lax.broadcasted_iota(jnp.int32, (lq, 2 * VD), 1)
    nt = (((1,), (1,)), ((), ()))
    for pair in range(HEADS // 2):
        outs = []
        for hd in (2 * pair, 2 * pair + 1):
            qh = q_ref[:, hd * SLOT:(hd + 1) * SLOT]
            scores = [lax.dot_general(qh, k_ref[:, hd * SLOT:(hd + 1) * SLOT], nt,
                                      preferred_element_type=F32) for k_ref, _ in segments]
            mx = functools.reduce(jnp.maximum, [jnp.max(s, -1, keepdims=True) for s in scores])
            ps = [jnp.exp(s - mx) for s in scores]
            den = functools.reduce(jnp.add, [jnp.sum(p, -1, keepdims=True) for p in ps])
            acc = functools.reduce(jnp.add, [
                jnp.dot(p.astype(BF16), v_ref[:, pair * 2 * VD:(pair + 1) * 2 * VD],
                        preferred_element_type=F32) for p, (_, v_ref) in zip(ps, segments)])
            outs.append(acc / den)
        o_ref[:, pair * 2 * VD:(pair + 1) * 2 * VD] = jnp.where(lane < VD, outs[0], outs[1]).astype(BF16)


def _attn_ctx_kernel(q_ref, k_ref, v_ref, o_ref):
    _attend(q_ref, [(k_ref, v_ref)], o_ref)


def _attn_lat_kernel(q_ref, kc_ref, vc_ref, k_ref, v_ref, o_ref):
    _attend(q_ref, [(kc_ref.at[0], vc_ref.at[0]), (k_ref, v_ref)], o_ref)


def _attention(q, k, v, kc, vc):
    kw, vw = HEADS * SLOT, HEADS * VD
    ctx = pl.pallas_call(
        _attn_ctx_kernel,
        out_shape=jax.ShapeDtypeStruct((T_CTX, vw), BF16),
        grid=(N_CTX_B,),
        in_specs=[pl.BlockSpec((CTX_L, kw), lambda b: (b, 0)),
                  pl.BlockSpec((CTX_L, kw), lambda b: (b, 0)),
                  pl.BlockSpec((CTX_L, vw), lambda b: (b, 0))],
        out_specs=pl.BlockSpec((CTX_L, vw), lambda b: (b, 0)),
        compiler_params=_cparams("parallel"),
        name="attn_ctx",
    )(q, k, v)
    nq = LAT_L // TM
    off = T_CTX // LAT_L
    lat = pl.pallas_call(
        _attn_lat_kernel,
        out_shape=jax.ShapeDtypeStruct((T_LAT, vw), BF16),
        grid=(N_LAT_B, nq),
        in_specs=[pl.BlockSpec((TM, kw), lambda b, j: (CTX_TILES + b * nq + j, 0)),
                  pl.BlockSpec((1, PAST, kw), lambda b, j: (b, 0, 0)),
                  pl.BlockSpec((1, PAST, vw), lambda b, j: (b, 0, 0)),
                  pl.BlockSpec((LAT_L, kw), lambda b, j: (off + b, 0)),
                  pl.BlockSpec((LAT_L, vw), lambda b, j: (off + b, 0))],
        out_specs=pl.BlockSpec((TM, vw), lambda b, j: (b * nq + j, 0)),
        compiler_params=_cparams("parallel", "parallel"),
        name="attn_lat",
    )(q, kc, vc, k, v)
    return ctx, lat


def _hyena_kernel(hy_ref, cw_ref, cb_ref, c_ref, sf_ref, sb_ref, kr_ref, ki_ref, krn_ref, hb_ref,
                  o_ref, *, L):
    x = hy_ref[...]
    row = lax.broadcasted_iota(jnp.int32, x.shape, 0)
    prev = jnp.where(row == 0, 0.0, pltpu.roll(x, 1, 0))
    nxt = jnp.where(row == L - 1, 0.0, pltpu.roll(x, L - 1, 0))
    z = prev * cw_ref[0:1] + x * cw_ref[1:2] + nxt * cw_ref[2:3] + cb_ref[...]
    y = z[:, 2 * HY_W:]
    for o in range(2):
        gate = z[:, o * HY_W:(o + 1) * HY_W]
        yb = y.astype(BF16)
        a_re = jnp.dot(c_ref[...], yb, preferred_element_type=F32)
        a_im = jnp.dot(sf_ref[...], yb, preferred_element_type=F32)
        ki = ki_ref[o]
        z_re = a_re * kr_ref[o] + a_im * ki
        z_im = a_im * krn_ref[o] - a_re * ki
        conv = (jnp.dot(c_ref[...], z_re.astype(BF16), preferred_element_type=F32)
                + jnp.dot(sb_ref[...], z_im.astype(BF16), preferred_element_type=F32))
        y = gate * (conv + y * hb_ref[o:o + 1])
    o_ref[...] = y.astype(BF16)


def _hyena_group(hy, L, nb, blk0, dft, spectra, conv_w, conv_b, hy_bias):
    c, sf, sb, _ = dft
    kr, ki, krn = spectra
    return pl.pallas_call(
        functools.partial(_hyena_kernel, L=L),
        out_shape=jax.ShapeDtypeStruct((nb * L, HY_W), BF16),
        grid=(nb,),
        in_specs=[pl.BlockSpec((L, 3 * HY_W), lambda b: (blk0 + b, 0)),
                  _full((3, 3 * HY_W)), _full((1, 3 * HY_W)),
                  _full((L, L)), _full((L, L)), _full((L, L)),
                  _full((2, L, HY_W)), _full((2, L, HY_W)), _full((2, L, HY_W)),
                  _full((2, HY_W))],
        out_specs=pl.BlockSpec((L, HY_W), lambda b: (b, 0)),
        compiler_params=_cparams("parallel"),
        name=f"hyena_{L}",
    )(hy, conv_w, conv_b.reshape(1, 3 * HY_W), c, sf, sb, kr, ki, krn, hy_bias)


def _out_kernel(a_ref, bc_ref, bl_ref, mc_ref, ml_ref, w_ref, xc_ref, xl_ref, ng_ref, nb_ref, mod_ref,
                g_ref, be_ref, wr_ref, br_ref, tri_ref, upper_ref, x1_ref, xloc_ref, route_ref, cnt_ref,
                *, pre_norm):
    dot = functools.partial(jnp.dot, preferred_element_type=F32)
    mixed = (dot(a_ref[...], w_ref[0:GM_W]) + dot(_pair_read(bc_ref, bl_ref), w_ref[GM_W:GM_W + HY_W])
             + dot(_pair_read(mc_ref, ml_ref), w_ref[GM_W + HY_W:]))
    x = _pair_read(xc_ref, xl_ref)
    if pre_norm:
        x = _layer_norm(x, ng_ref[...], nb_ref[...])
    m = mod_ref[0]
    x1 = _layer_norm(ALPHA * x + m[2:3] * mixed, g_ref[...], be_ref[...])
    x1_ref[...] = x1
    h2 = x1 * (1.0 + m[4:5]) + m[3:4]

    h_hi = h2.astype(BF16)
    h_lo = (h2 - h_hi.astype(F32)).astype(BF16)
    w_hi = wr_ref[...].astype(BF16)
    w_lo = (wr_ref[...] - w_hi.astype(F32)).astype(BF16)
    part = dot(h_hi, jnp.concatenate([w_hi, w_lo], axis=1))
    logits = part[:, :128] + part[:, 128:] + dot(h_lo, w_hi) + br_ref[...]
    lane = lax.broadcasted_iota(jnp.int32, logits.shape, 1)
    lanef = lane.astype(F32)
    big = jnp.float32(1e9)
    ninf = jnp.float32(-jnp.inf)
    is_g = lane < N_GROUPS
    gl = jnp.where(is_g, logits, ninf)
    gmax = jnp.max(gl, -1, keepdims=True)
    gidx = jnp.min(jnp.where(gl == gmax, lanef, big), -1, keepdims=True)
    gw = 1.0 / jnp.sum(jnp.where(is_g, jnp.exp(logits - gmax), 0.0), -1, keepdims=True)
    ex = lane - N_GROUPS
    in_group = (ex >= 0) & (ex < N_EXP) & ((ex // EPG).astype(F32) == gidx)
    el = jnp.where(in_group, logits, ninf)
    v1 = jnp.max(el, -1, keepdims=True)
    i1 = jnp.min(jnp.where(el == v1, lanef, big), -1, keepdims=True)
    el2 = jnp.where(lanef == i1, ninf, el)
    v2 = jnp.max(el2, -1, keepdims=True)
    i2 = jnp.min(jnp.where(el2 == v2, lanef, big), -1, keepdims=True)
    e21 = jnp.exp(v2 - v1)
    w1 = gw / (1.0 + e21)
    w2 = gw * e21 / (1.0 + e21)
    e1 = i1 - N_GROUPS
    e2 = i2 - N_GROUPS

    oh1 = jnp.where(lanef == e1, 1.0, 0.0)
    oh2 = jnp.where(lanef == e2, 1.0, 0.0)
    ex1 = dot(tri_ref[...], oh1.astype(BF16))
    ex2 = dot(tri_ref[...], oh2.astype(BF16))
    col1 = jnp.sum(oh1, axis=0, keepdims=True)
    col2 = jnp.sum(oh2, axis=0, keepdims=True)
    n = col1 + col2
    run = jnp.floor((n + (BLK - 1.0)) * (1.0 / BLK)) * BLK
    start = dot(jnp.broadcast_to(run, (8, 128)).astype(BF16), upper_ref[...])[0:1]
    loc1 = jnp.sum(oh1 * (start + ex1), -1, keepdims=True)
    loc2 = jnp.sum(oh2 * (start + col1 + ex2), -1, keepdims=True)
    cnt_ref[...] = jnp.broadcast_to(n, cnt_ref.shape)

    route = jnp.zeros_like(logits)
    for j, val in enumerate((e1, e2, w1, w2, loc1, loc2)):
        route = jnp.where(lane == j, val, route)
    route_ref[...] = route

    loc1_row = jnp.transpose(jnp.broadcast_to(loc1, (TM, 128)))[0:1].astype(jnp.int32)
    loc2_row = jnp.transpose(jnp.broadcast_to(loc2, (TM, 128)))[0:1].astype(jnp.int32)
    slot = lax.broadcasted_iota(jnp.int32, (LOC, TM), 0)
    perm = jnp.where((slot == loc1_row) | (slot == loc2_row), 1.0, 0.0).astype(BF16)
    xloc_ref[...] = dot(perm, h_hi).astype(BF16)


def _out_proj(a, hyb_pair, att_pair, w_out, x_pair, norm, pre_norm, mod_l, g, b, w_route, b_route, tri,
              upper):
    tile = lambda n: pl.BlockSpec((TM, n), lambda i: (i, 0))
    return pl.pallas_call(
        functools.partial(_out_kernel, pre_norm=pre_norm),
        out_shape=(jax.ShapeDtypeStruct((T, D), F32), jax.ShapeDtypeStruct((N_TILES * LOC, D), BF16),
                   jax.ShapeDtypeStruct((T, 128), F32), jax.ShapeDtypeStruct((N_TILES * 8, 128), F32)),
        grid=(N_TILES,),
        in_specs=[tile(GM_W)] + _pair_specs(HY_W) + _pair_specs(HEADS * VD) + [_full((D, D))]
        + _pair_specs(D) + [
                  _full((1, D)), _full((1, D)),
                  pl.BlockSpec((1, 6, D), lambda i: (_cond_of_tile(i), 0, 0)),
                  _full((1, D)), _full((1, D)), _full((D, 128)), _full((1, 128)), _full((TM, TM)),
                  _full((128, 128))],
        out_specs=(tile(D), pl.BlockSpec((LOC, D), lambda i: (i, 0)), tile(128),
                   pl.BlockSpec((8, 128), lambda i: (i, 0))),
        compiler_params=_cparams("parallel"),
        name="out_proj_route",
    )(a, *hyb_pair, *att_pair, w_out, *x_pair, *norm, mod_l, g, b, w_route, b_route, tri, upper)


def _expert_kernel(off_ref, cnt_ref, src_ref, x_hbm, wg_ref, wu_ref, wd_ref, y_hbm,
                   xbuf, ybuf, wgu_b, wd_b, sem, osem):
    e = pl.program_id(0)
    first_tile = off_ref[e]
    n_tiles = cnt_ref[e]
    total = off_ref[N_EXP - 1] + cnt_ref[N_EXP - 1]

    def block_copies(t, s):
        return [pltpu.make_async_copy(
            x_hbm.at[pl.ds(pl.multiple_of(src_ref[t * BLK_PER_TILE + b] * BLK, BLK), BLK), :],
            xbuf.at[s, pl.ds(b * BLK, BLK), :], sem.at[s]) for b in range(BLK_PER_TILE)]

    def out_copy(t, s):
        return pltpu.make_async_copy(ybuf.at[s], y_hbm.at[pl.ds(pl.multiple_of(t * TM, TM), TM), :],
                                     osem.at[s])

    @pl.when(e == 0)
    def _():
        for cp in block_copies(0, 0):
            cp.start()

    @pl.when(n_tiles > 0)
    def _():
        wgu_b[:, 0:D_EXP] = wg_ref[0, 0].astype(BF16)
        wgu_b[:, D_EXP:2 * D_EXP] = wu_ref[0, 0].astype(BF16)
        wd_b[...] = wd_ref[0, 0].astype(BF16)

    def tile_body(t, carry):
        s = t % 2

        @pl.when(t + 1 < total)
        def _():
            for cp in block_copies(t + 1, 1 - s):
                cp.start()

        for cp in block_copies(t, s):
            cp.wait()

        @pl.when(t >= 2)
        def _():
            out_copy(t - 2, s).wait()

        gu = jnp.dot(xbuf[s], wgu_b[...], preferred_element_type=F32)
        gate, up = gu[:, :D_EXP], gu[:, D_EXP:]
        hid = gate / (1.0 + jnp.exp(-gate)) * up
        ybuf[s] = jnp.dot(hid.astype(BF16), wd_b[...], preferred_element_type=F32).astype(BF16)
        out_copy(t, s).start()
        return carry

    lax.fori_loop(first_tile, first_tile + n_tiles, tile_body, 0)

    @pl.when(e == N_EXP - 1)
    def _():
        @pl.when(total >= 2)
        def _():
            out_copy(total - 2, total % 2).wait()

        out_copy(total - 1, (total - 1) % 2).wait()
        ybuf[0] = jnp.zeros((TM, D), BF16)

        def zero_start(t, carry):
            out_copy(t, 0).start()
            return carry

        def zero_wait(t, carry):
            out_copy(t, 0).wait()
            return carry

        lax.fori_loop(total, MOE_TILES, zero_start, 0)
        lax.fori_loop(total, MOE_TILES, zero_wait, 0)


def _experts(l, tables, xloc, w_gate, w_up, w_down):
    grid_spec = pltpu.PrefetchScalarGridSpec(
        num_scalar_prefetch=len(tables),
        grid=(N_EXP,),
        in_specs=[pl.BlockSpec(memory_space=pl.ANY),
                  pl.BlockSpec((1, 1, D, D_EXP), lambda e, *_: (l, e, 0, 0)),
                  pl.BlockSpec((1, 1, D, D_EXP), lambda e, *_: (l, e, 0, 0)),
                  pl.BlockSpec((1, 1, D_EXP, D), lambda e, *_: (l, e, 0, 0))],
        out_specs=pl.BlockSpec(memory_space=pl.ANY),
        scratch_shapes=[pltpu.VMEM((2, TM, D), BF16), pltpu.VMEM((2, TM, D), BF16),
                        pltpu.VMEM((D, 2 * D_EXP), BF16), pltpu.VMEM((D_EXP, D), BF16),
                        pltpu.SemaphoreType.DMA((2,)), pltpu.SemaphoreType.DMA((2,))],
    )
    return pl.pallas_call(
        _expert_kernel,
        out_shape=jax.ShapeDtypeStruct((MOE_ROWS, D), BF16),
        grid_spec=grid_spec,
        compiler_params=_cparams("arbitrary"),
        name="experts",
    )(*tables, xloc, w_gate, w_up, w_down)


def _combine_kernel(nblk_ref, gsrc_ref, y_hbm, route_ref, x1_ref, mod_ref, g_ref, b_ref, oc_ref, ol_ref,
                    ybuf, sem):
    i = pl.program_id(0)

    slot_i = i % 2

    def block_copy(t, s, lb):
        src = pl.multiple_of(gsrc_ref[t * LOC_BLKS + lb] * BLK, BLK)
        return pltpu.make_async_copy(y_hbm.at[pl.ds(src, BLK), :],
                                     ybuf.at[s, pl.ds(pl.multiple_of(lb * BLK, BLK), BLK), :], sem.at[s])

    def fetch(t, s):
        def start(lb, c):
            block_copy(t, s, lb).start()
            return c
        lax.fori_loop(0, nblk_ref[t], start, 0)

    @pl.when(i == 0)
    def _():
        ybuf[...] = jnp.zeros_like(ybuf)
        fetch(0, 0)

    @pl.when(i + 1 < N_TILES)
    def _():
        fetch(jnp.minimum(i + 1, N_TILES - 1), 1 - slot_i)

    def wait(lb, c):
        block_copy(i, slot_i, lb).wait()
        return c

    lax.fori_loop(0, nblk_ref[i], wait, 0)
    route = route_ref[...]
    yb = ybuf[slot_i]
    slot = lax.broadcasted_iota(jnp.int32, (TM, LOC), 1)
    pick = (jnp.where(slot == route[:, 4:5].astype(jnp.int32), route[:, 2:3], 0.0)
            + jnp.where(slot == route[:, 5:6].astype(jnp.int32), route[:, 3:4], 0.0)).astype(BF16)
    moe = jnp.dot(pick, yb, preferred_element_type=F32)
    m = mod_ref[0]
    res = _layer_norm(ALPHA * x1_ref[...] + m[5:6] * moe, g_ref[...], b_ref[...])

    @pl.when(i < CTX_TILES)
    def _():
        oc_ref[...] = res

    @pl.when(i >= CTX_TILES)
    def _():
        ol_ref[...] = res


def _combine(nblk, gsrc, y, route, x1, mod_l, g, b):
    out_shape = (jax.ShapeDtypeStruct((T_CTX, D), F32), jax.ShapeDtypeStruct((T_LAT, D), F32))
    out_specs = tuple(_pair_specs(D))
    grid_spec = pltpu.PrefetchScalarGridSpec(
        num_scalar_prefetch=2,
        grid=(N_TILES,),
        in_specs=[pl.BlockSpec(memory_space=pl.ANY),
                  pl.BlockSpec((TM, 128), lambda i, nb, gs: (i, 0)),
                  pl.BlockSpec((TM, D), lambda i, nb, gs: (i, 0)),
                  pl.BlockSpec((1, 6, D), lambda i, nb, gs: (_cond_of_tile(i), 0, 0)),
                  pl.BlockSpec((1, D), lambda i, nb, gs: (0, 0)),
                  pl.BlockSpec((1, D), lambda i, nb, gs: (0, 0))],
        out_specs=out_specs,
        scratch_shapes=[pltpu.VMEM((2, LOC, D), BF16), pltpu.SemaphoreType.DMA((2,))],
    )
    return pl.pallas_call(
        _combine_kernel,
        out_shape=out_shape,
        grid_spec=grid_spec,
        compiler_params=_cparams("arbitrary"),
        name="moe_combine",
    )(nblk, gsrc, y, route, x1, mod_l, g, b)


def _prefix_pick(starts, query, table):
    delta = table - jnp.concatenate([jnp.zeros_like(table[..., :1]), table[..., :-1]], axis=-1)
    return jnp.sum(jnp.where(starts <= query, delta, 0), axis=-1)


def _dispatch_tables(cnt):
    i32 = jnp.int32
    run = (cnt + BLK - 1) // BLK * BLK
    loc_start = jnp.cumsum(run, axis=1) - run
    nblk_loc = jnp.sum(run, axis=1) // BLK
    seg_rows = jnp.sum(run, axis=0)
    seg_tiles = (seg_rows + TM - 1) // TM
    tile_end = jnp.cumsum(seg_tiles)
    seg_start = (tile_end - seg_tiles) * TM
    glob_start = seg_start[None, :] + jnp.cumsum(run, axis=0) - run

    g_blk = (glob_start.T.reshape(1, -1)) // BLK
    n_blk = (run.T.reshape(1, -1)) // BLK
    l_blk = ((jnp.arange(N_TILES, dtype=i32) * LOC)[None, :] + loc_start.T).reshape(1, -1) // BLK
    gb = jnp.arange(MOE_ROWS // BLK, dtype=i32)[:, None]
    off = gb[:, 0] - _prefix_pick(g_blk, gb, g_blk)
    src_blk = jnp.where(off < _prefix_pick(g_blk, gb, n_blk), _prefix_pick(g_blk, gb, l_blk) + off,
                        ZERO_BLK).astype(i32)

    pos = (jnp.arange(LOC_BLKS, dtype=i32) * BLK)[None, :, None]
    shift = _prefix_pick(loc_start[:, None, :], pos, (glob_start - loc_start)[:, None, :])
    gsrc = ((pos[:, :, 0] + shift) // BLK).astype(i32)
    expert_tables = ((tile_end - seg_tiles).astype(i32), seg_tiles.astype(i32), src_blk)
    return expert_tables, nblk_loc.astype(i32), gsrc.reshape(-1)


def _swap_pairs(w):
    return w.reshape(w.shape[:-1] + (ROPE // 2, 2))[..., ::-1].reshape(w.shape)


def _rope_slot(pe):
    return jnp.pad(pe, [(0, 0)] * (pe.ndim - 1) + [(NOPE, SLOT - NOPE - ROPE)])


def _w_in_layout(w):
    pe = w[:, 1664:]
    return jnp.concatenate([w[:, :1664], _rope_slot(pe), _rope_slot(_swap_pairs(pe))], axis=1).astype(BF16)


def _wuq_layout(w):
    w = w.reshape(Q_LORA, HEADS, NOPE + ROPE)
    nope, pe = w[..., :NOPE], w[..., NOPE:]
    plain = jnp.concatenate([nope, jnp.pad(pe, ((0, 0), (0, 0), (0, SLOT - NOPE - ROPE)))], axis=-1)
    swapped = _rope_slot(_swap_pairs(pe))
    return jnp.concatenate([plain.reshape(Q_LORA, HEADS * SLOT),
                            swapped.reshape(Q_LORA, HEADS * SLOT)], axis=1).astype(BF16)


def _wukv_layout(w):
    w = w.reshape(DEPTH, KV_LORA, HEADS, NOPE + VD)
    wk = jnp.pad(w[..., :NOPE], ((0, 0), (0, 0), (0, 0), (0, SLOT - NOPE)))
    return (wk.reshape(DEPTH, KV_LORA, HEADS * SLOT).astype(BF16),
            w[..., NOPE:].reshape(DEPTH, KV_LORA, HEADS * VD).astype(BF16))


def kernel(x_prompt, x_sample, c, cache_ckv, cache_kpe, c_ctx, ln_in_g, ln_in_b, w_mod, b_mod, w_in,
           gm_ln_g, gm_ln_b, gm_ws, gm_bs, hy_conv_w, hy_conv_b, hy_f_w1, hy_f_b1, hy_f_w2, hy_f_b2,
           hy_f_w3, hy_f_freq, hy_bias, mla_gq, mla_gkv, mla_wuq, mla_wukv, w_out, ln1_g, ln1_b,
           ln2_g, ln2_b, moe_w_gr, moe_b_gr, moe_w_er, moe_b_er, moe_w_gate, moe_w_up, moe_w_down):
    rope = _rope_tables()
    dft = {L: _dft_tables(L) for L in (CTX_L, LAT_L)}
    pos_tab = {L: _hyena_positions(L) for L in (CTX_L, LAT_L)}
    hd = np.arange(GM_W) // GM_HD
    avg = jnp.asarray((hd[:, None] == hd[None, :]) / GM_HD, BF16)
    tri = jnp.asarray(np.tril(np.ones((TM, TM)), -1), BF16)
    upper = jnp.asarray(np.triu(np.ones((128, 128)), 1), BF16)

    cond8 = jnp.concatenate([c_ctx[None], c, jnp.zeros((8 - 1 - N_LAT_B, D), F32)], axis=0)
    mod = _modulation(cond8, w_mod, b_mod).reshape(DEPTH, 8, 6, D)

    wuk_all, wuv_all = _wukv_layout(mla_wukv)
    kc_all, vc_all = _cache_kv(cache_ckv, _rope_slot(cache_kpe), wuk_all, wuv_all)

    x_pair = (x_prompt.reshape(T_CTX, D), x_sample.reshape(T_LAT, D))
    norm = (ln_in_g.reshape(1, D), ln_in_b.reshape(1, D))
    ckv_states, kpe_states = [], []
    for l in range(DEPTH):
        w_in_ext = _w_in_layout(w_in[l])
        wuq_arr = _wuq_layout(mla_wuq[l])
        bs_full = jnp.repeat(gm_bs[l].T, GM_HD, axis=1)
        a, hy, q, k, v, ckv, kpe = _in_proj(
            x_pair, norm, l == 0, mod[l], w_in_ext, gm_ln_g[l].reshape(1, GM_W), gm_ln_b[l].reshape(1, GM_W), avg,
            gm_ws[l].astype(BF16), bs_full, mla_gq[l].reshape(1, Q_LORA), mla_gkv[l].reshape(1, KV_LORA),
            wuq_arr, wuk_all[l], wuv_all[l], rope)
        ckv_states.append(ckv[:T_CTX].reshape(N_CTX_B, CTX_L, KV_LORA))
        kpe_states.append(kpe[:T_CTX, NOPE:NOPE + ROPE].reshape(N_CTX_B, CTX_L, ROPE))

        hyb = []
        for L, nb, blk0 in ((CTX_L, N_CTX_B, 0), (LAT_L, N_LAT_B, T_CTX // LAT_L)):
            z, dec = pos_tab[L]
            spectra = _filter_spectra(L, z, dec, dft[L], hy_f_w1[l], hy_f_b1[l], hy_f_w2[l],
                                      hy_f_b2[l], hy_f_w3[l], hy_f_freq[l])
            hyb.append(_hyena_group(hy, L, nb, blk0, dft[L], spectra, hy_conv_w[l], hy_conv_b[l],
                                    hy_bias[l]))

        att = _attention(q, k, v, kc_all[l], vc_all[l])

        w_route = jnp.pad(jnp.concatenate([moe_w_gr[l], moe_w_er[l].reshape(D, N_EXP)], axis=1),
                          ((0, 0), (0, 128 - N_GROUPS - N_EXP)))
        b_route =jnp.pad(jnp.concatenate([moe_b_gr[l], moe_b_er[l].reshape(N_EXP)]),
                          (0, 128 - N_GROUPS - N_EXP)).reshape(1, 128)
        x1, xloc, route, counts = _out_proj(a, hyb, att, w_out[l].astype(BF16), x_pair, norm, l == 0, mod[l],
                                            ln1_g[l].reshape(1, D), ln1_b[l].reshape(1, D),
                                            w_route, b_route, tri, upper)
        cnt = counts.reshape(N_TILES, 8, 128)[:, 0, :N_EXP].astype(jnp.int32)
        expert_tables, nblk_loc, gsrc = _dispatch_tables(cnt)
        y = _experts(l, expert_tables, xloc, moe_w_gate, moe_w_up, moe_w_down)
        x_pair = _combine(nblk_loc, gsrc, y, route, x1, mod[l], ln2_g[l].reshape(1, D), ln2_b[l].reshape(1, D))

    y_prompt = x_pair[0].reshape(N_CTX_B, CTX_L, D)
    y_sample = x_pair[1].reshape(N_LAT_B, LAT_L, D)
    return (y_prompt, y_sample, jnp.stack(ckv_states, axis=1), jnp.stack(kpe_states, axis=1))
```

```python
import functools
import math

import numpy as np
import jax
import jax.numpy as jnp
from jax import lax
from jax.experimental import pallas as pl
from jax.experimental.pallas import tpu as pltpu

D = 1024
N_CTX_B, CTX_L = 16, 256
N_LAT_B, LAT_L = 4, 1024
DEPTH = 2
T_CTX = N_CTX_B * CTX_L
T_LAT = N_LAT_B * LAT_L
T = T_CTX + T_LAT
PAST = 256
GRID_W = 64

GM_HEADS, GM_HD, GM_W, CHUNK = 4, 64, 256, 128
HY_W, HY_EMB, HY_FH = 256, 33, 64
NOPE, ROPE, VD, HEADS = 64, 32, 64, 8
Q_LORA, KV_LORA = 256, 128
SLOT = 128
N_GROUPS, EPG, N_EXP, D_EXP = 4, 8, 32, 256
ALPHA = (2.0 * DEPTH) ** 0.25
LN_EPS, RMS_EPS = 1e-5, 1e-6
ROPE_THETA = 10000.0

TM = 256
N_TILES = T // TM
CTX_TILES = T_CTX // TM
LAT_TILES_PER_B = LAT_L // TM
N_ASSIGN = 2 * T
BLK = 16
BLK_PER_TILE = TM // BLK
LOC = 2 * TM + N_EXP * BLK
LOC_BLKS = LOC // BLK
ZERO_BLK = LOC_BLKS - 1
MOE_TILES = (N_ASSIGN + N_TILES * N_EXP * (BLK - 1)) // TM + N_EXP
MOE_ROWS = MOE_TILES * TM
IN_EXT = 1920

F32, BF16 = jnp.float32, jnp.bfloat16
VMEM_LIMIT = 52 * 1024 * 1024


def _cparams(*sem):
    return pltpu.CompilerParams(dimension_semantics=sem, vmem_limit_bytes=VMEM_LIMIT)


def _cond_of_tile(i):
    return jnp.where(i < CTX_TILES, 0, 1 + (i - CTX_TILES) // LAT_TILES_PER_B)


def _rope_block_of_tile(i):
    return jnp.where(i < CTX_TILES, 0, 1 + (i - CTX_TILES) % LAT_TILES_PER_B)


def _full(shape):
    n = len(shape)
    return pl.BlockSpec(shape, lambda *_: (0,) * n)


def _pair_specs(n):
    return [pl.BlockSpec((TM, n), lambda i, *_: (jnp.minimum(i, CTX_TILES - 1), 0)),
            pl.BlockSpec((TM, n), lambda i, *_: (jnp.maximum(i - CTX_TILES, 0), 0))]


def _pair_read(c_ref, l_ref):
    return jnp.where(pl.program_id(0) < CTX_TILES, c_ref[...], l_ref[...])


def _layer_norm(x, g, b):
    mu = jnp.mean(x, -1, keepdims=True)
    xc = x - mu
    var = jnp.mean(xc * xc, -1, keepdims=True)
    return xc * lax.rsqrt(var + LN_EPS) * g + b


def _rms_norm(x, g):
    return x * lax.rsqrt(jnp.mean(x * x, -1, keepdims=True) + RMS_EPS) * g


def _bdot(a, b):
    return jnp.dot(a.astype(BF16), b.astype(BF16), preferred_element_type=F32)


def _split2_dot(x, w_bf16):
    hi = x.astype(BF16)
    lo = (x - hi.astype(F32)).astype(BF16)
    dot = functools.partial(jnp.dot, preferred_element_type=F32)
    return dot(hi, w_bf16) + dot(lo, w_bf16)


def _dft_tables(L):
    f = np.arange(L, dtype=np.int64)
    ft = np.outer(f, f) % (2 * L)
    ang = np.pi * ft / L
    c = np.cos(ang)
    s = np.sin(ang)
    alt = np.where(f % 2 == 0, 1.0, -1.0)
    sf = s.copy()
    sf[0, :] = alt
    return (jnp.asarray(c, F32).astype(BF16), jnp.asarray(sf, F32).astype(BF16),
            jnp.asarray(sf.T.copy(), F32).astype(BF16), jnp.asarray(alt[:, None], F32))


def _rope_tables():
    rows = LAT_L // GRID_W
    row = np.repeat(np.arange(rows, dtype=np.float64), GRID_W)
    col = np.tile(np.arange(GRID_W, dtype=np.float64), rows)
    n = ROPE // 4
    inv = ROPE_THETA ** (-np.arange(n, dtype=np.float64) / n)
    ang = np.concatenate([row[:, None] * inv, col[:, None] * inv], -1)
    cos = np.zeros((CTX_L + LAT_L, SLOT))
    sin = np.zeros((CTX_L + LAT_L, SLOT))
    cos[:, :NOPE + ROPE] = 1.0
    cos[CTX_L:, NOPE:NOPE + ROPE:2] = np.cos(ang)
    cos[CTX_L:, NOPE + 1:NOPE + ROPE:2] = np.cos(ang)
    sin[CTX_L:, NOPE:NOPE + ROPE:2] = -np.sin(ang)
    sin[CTX_L:, NOPE + 1:NOPE + ROPE:2] = np.sin(ang)
    scale = 1.0 / math.sqrt(NOPE + ROPE)
    return (jnp.asarray(cos * scale, F32), jnp.asarray(sin * scale, F32),
            jnp.asarray(cos, F32), jnp.asarray(sin, F32))


def _hyena_positions(L):
    t = jnp.linspace(0.0, 1.0, L, dtype=F32)[:, None]
    bands = (HY_EMB - 1) // 2
    w = 2.0 * math.pi * jnp.arange(L, dtype=F32)[:, None] / L
    f = jnp.linspace(1e-4, bands - 1, bands, dtype=F32)[None]
    z = jnp.concatenate([t, jnp.cos(f * w), -jnp.sin(f * w)], -1)
    z = jnp.pad(z, ((0, 0), (0, 128 - HY_EMB)))
    min_decay = math.log(1e-2) / 1.5
    max_decay = math.log(1e-2) / 0.3
    deltas = jnp.abs(jnp.linspace(min_decay, max_decay, HY_W, dtype=F32))
    return z, jnp.exp(-t * deltas)


def _mod_kernel(c_ref, w_ref, b_ref, o_ref):
    c = c_ref[...]
    s = c / (1.0 + jnp.exp(-c))
    o_ref[0] = _bdot(s, w_ref[0]) + b_ref[0]


def _modulation(cond8, w_mod, b_mod):
    tn = 1536
    return pl.pallas_call(
        _mod_kernel,
        out_shape=jax.ShapeDtypeStruct((DEPTH, 8, 6 * D), F32),
        grid=(DEPTH, 6 * D // tn),
        in_specs=[pl.BlockSpec((8, D), lambda l, j: (0, 0)),
                  pl.BlockSpec((1, D, tn), lambda l, j: (l, 0, j)),
                  pl.BlockSpec((1, 1, tn), lambda l, j: (l, 0, j))],
        out_specs=pl.BlockSpec((1, 8, tn), lambda l, j: (l, 0, j)),
        compiler_params=_cparams("parallel", "parallel"),
        name="modulation",
    )(cond8, w_mod, b_mod.reshape(DEPTH, 1, 6 * D))


def _filter_kernel(z_ref, w1_ref, b1_ref, w2_ref, b2_ref, w3_ref, fr_ref, dec_ref, c_ref, sf_ref,
                   alt_ref, kr_ref, ki_ref, krn_ref, *, L):
    hdot = functools.partial(jnp.dot, preferred_element_type=F32, precision=lax.Precision.HIGHEST)
    h1 = jnp.sin(fr_ref[0:1] * (hdot(z_ref[...], w1_ref[...]) + b1_ref[...]))
    h2 = jnp.sin(fr_ref[1:2] * (hdot(h1, w2_ref[...]) + b2_ref[...]))
    filt = hdot(h2, w3_ref[...])
    row = lax.broadcasted_iota(jnp.int32, (L, HY_W), 0)
    dec = dec_ref[...]
    for o in range(2):
        fwd = filt[:, o * 2 * HY_W:o * 2 * HY_W + HY_W] * dec
        bwd = jnp.where(row == 0, 0.0, filt[:, o * 2 * HY_W + HY_W:(o + 1) * 2 * HY_W] * dec)
        sm = fwd + bwd
        df = fwd - bwd
        kr = _bdot(c_ref[...], sm)
        ki = -_bdot(sf_ref[...], df)
        kr_nyq = jnp.sum(alt_ref[...] * sm, axis=0, keepdims=True)
        kr_ref[o] = jnp.where(row == 0, kr * (0.5 / L), kr * (1.0 / L))
        ki_ref[o] = jnp.where(row == 0, 0.0, ki * (1.0 / L))
        krn_ref[o] = jnp.where(row == 0, kr_nyq * (0.5 / L), kr * (1.0 / L))


def _filter_spectra(L, z, dec, dft, w1, b1, w2, b2, w3, freq):
    c, sf, _, alt = dft
    pad = 128 - HY_FH
    w1p = jnp.pad(w1, ((0, 128 - HY_EMB), (0, pad)))
    w2p = jnp.pad(w2, ((0, pad), (0, pad)))
    w3p = jnp.pad(w3, ((0, pad), (0, 0)))
    b1p = jnp.pad(b1, (0, pad)).reshape(1, 128)
    b2p = jnp.pad(b2, (0, pad)).reshape(1, 128)
    frp = jnp.pad(freq, ((0, 0), (0, pad)))
    shp = jax.ShapeDtypeStruct((2, L, HY_W), F32)
    args = (z, w1p, b1p, w2p, b2p, w3p, frp, dec, c, sf, alt)
    return pl.pallas_call(
        functools.partial(_filter_kernel, L=L),
        out_shape=(shp, shp, shp),
        grid=(1,),
        in_specs=[_full(a.shape) for a in args],
        out_specs=(_full((2, L, HY_W)),) * 3,
        compiler_params=_cparams("arbitrary"),
        name=f"hyena_filter_{L}",
    )(*args)


def _in_kernel(xc_ref, xl_ref, ng_ref, nb_ref, mod_ref, w_ref, gg_ref, gb_ref, avg_ref, ws_ref, bs_ref,
               gq_ref, gkv_ref, wuq_ref, wuk_ref, wuv_ref, cq_ref, sq_ref, ck_ref, sk_ref,
               a_ref, hy_ref, q_ref, k_ref, v_ref, ckv_ref, kpe_ref, *, pre_norm):
    x = _pair_read(xc_ref, xl_ref)
    if pre_norm:
        x = _layer_norm(x, ng_ref[...], nb_ref[...])
    m = mod_ref[0]
    h = x * (1.0 + m[1:2]) + m[0:1]
    proj = _bdot(h, w_ref[...])

    gu = jax.nn.gelu(proj[:, 0:GM_W], approximate=True)
    gv = jax.nn.gelu(proj[:, GM_W:2 * GM_W], approximate=True)
    avg = avg_ref[...]
    mu = _split2_dot(gv, avg)
    vc = gv - mu
    var = _split2_dot(vc * vc, avg)
    vln = (vc * lax.rsqrt(var + LN_EPS) * gg_ref[...] + gb_ref[...]).astype(BF16)
    lane = lax.broadcasted_iota(jnp.int32, (CHUNK, GM_W), 1)
    for c in range(TM // CHUNK):
        vchunk = vln[c * CHUNK:(c + 1) * CHUNK]
        s = bs_ref[...]
        for hd in range(GM_HEADS):
            sh = jnp.dot(ws_ref[hd], vchunk, preferred_element_type=F32)
            s = s + jnp.where(lane // GM_HD == hd, sh, 0.0)
        a_ref[c * CHUNK:(c + 1) * CHUNK, :] = (gu[c * CHUNK:(c + 1) * CHUNK] * s).astype(BF16)

    hy_ref[...] = proj[:, 512:1280]

    cq = _rms_norm(proj[:, 1280:1536], gq_ref[...])
    qq = _bdot(cq, wuq_ref[...])
    cos_q = jnp.concatenate([cq_ref[...]] * HEADS, axis=1)
    sin_q = jnp.concatenate([sq_ref[...]] * HEADS, axis=1)
    q_ref[...] = (qq[:, :HEADS * SLOT] * cos_q + qq[:, HEADS * SLOT:] * sin_q).astype(BF16)

    ckv = _rms_norm(proj[:, 1536:1664], gkv_ref[...])
    ckv_ref[...] = ckv
    kpe = proj[:, 1664:1792]
    kpe_ref[...] = kpe
    krot = kpe * ck_ref[...] + proj[:, 1792:1920] * sk_ref[...]
    kn = _bdot(ckv, wuk_ref[...])
    k_ref[...] = (kn + jnp.concatenate([krot] * HEADS, axis=1)).astype(BF16)
    v_ref[...] = _bdot(ckv, wuv_ref[...]).astype(BF16)


def _in_proj(x_pair, norm, pre_norm, mod_l, w_in_ext, gm_g, gm_b, avg, ws, bs_full, gq, gkv, wuq_arr,
             wuk_arr, wuv_arr, rope):
    cos_q, sin_q, cos_k, sin_k = rope
    tile = lambda n: pl.BlockSpec((TM, n), lambda i: (i, 0))
    rope_spec = pl.BlockSpec((TM, SLOT), lambda i: (_rope_block_of_tile(i), 0))
    out_shapes = (jax.ShapeDtypeStruct((T, GM_W), BF16),
                  jax.ShapeDtypeStruct((T, 3 * HY_W), F32),
                  jax.ShapeDtypeStruct((T, HEADS * SLOT), BF16),
                  jax.ShapeDtypeStruct((T, HEADS * SLOT), BF16),
                  jax.ShapeDtypeStruct((T, HEADS * VD), BF16),
                  jax.ShapeDtypeStruct((T, KV_LORA), F32),
                  jax.ShapeDtypeStruct((T, SLOT), F32))
    return pl.pallas_call(
        functools.partial(_in_kernel, pre_norm=pre_norm),
        out_shape=out_shapes,
        grid=(N_TILES,),
        in_specs=_pair_specs(D) + [
                  _full((1, D)), _full((1, D)),
                  pl.BlockSpec((1, 6, D), lambda i: (_cond_of_tile(i), 0, 0)),
                  _full((D, IN_EXT)), _full((1, GM_W)), _full((1, GM_W)), _full((GM_W, GM_W)),
                  _full((GM_HEADS, CHUNK, CHUNK)), _full((CHUNK, GM_W)),
                  _full((1, Q_LORA)), _full((1, KV_LORA)),
                  _full((Q_LORA, 2 * HEADS * SLOT)), _full((KV_LORA, HEADS * SLOT)),
                  _full((KV_LORA, HEADS * VD)),
                  rope_spec, rope_spec, rope_spec, rope_spec],
        out_specs=(tile(GM_W), tile(3 * HY_W), tile(HEADS * SLOT), tile(HEADS * SLOT),
                   tile(HEADS * VD), tile(KV_LORA), tile(SLOT)),
        compiler_params=_cparams("parallel"),
        name="in_proj",
    )(*x_pair, *norm, mod_l, w_in_ext, gm_g, gm_b, avg, ws, bs_full, gq, gkv, wuq_arr, wuk_arr, wuv_arr,
      cos_q, sin_q, cos_k, sin_k)


def _cache_kernel(ckv_ref, kpe_ref, wuk_ref, wuv_ref, k_ref, v_ref):
    ckv = ckv_ref[0, 0]
    kn = _bdot(ckv, wuk_ref[0])
    k_ref[0, 0] = (kn + jnp.concatenate([kpe_ref[0, 0]] * HEADS, axis=1)).astype(BF16)
    v_ref[0, 0] = _bdot(ckv, wuv_ref[0]).astype(BF16)


def _cache_kv(cache_ckv, cache_kpe_slot, wuk_arr, wuv_arr):
    return pl.pallas_call(
        _cache_kernel,
        out_shape=(jax.ShapeDtypeStruct((DEPTH, N_LAT_B, PAST, HEADS * SLOT), BF16),
                   jax.ShapeDtypeStruct((DEPTH, N_LAT_B, PAST, HEADS * VD), BF16)),
        grid=(DEPTH, N_LAT_B),
        in_specs=[pl.BlockSpec((1, 1, PAST, KV_LORA), lambda l, b: (b, l, 0, 0)),
                  pl.BlockSpec((1, 1, PAST, SLOT), lambda l, b: (b, l, 0, 0)),
                  pl.BlockSpec((1, KV_LORA, HEADS * SLOT), lambda l, b: (l, 0, 0)),
                  pl.BlockSpec((1, KV_LORA, HEADS * VD), lambda l, b: (l, 0, 0))],
        out_specs=(pl.BlockSpec((1, 1, PAST, HEADS * SLOT), lambda l, b: (l, b, 0, 0)),
                   pl.BlockSpec((1, 1, PAST, HEADS * VD), lambda l, b: (l, b, 0, 0))),
        compiler_params=_cparams("parallel", "parallel"),
        name="cache_kv",
    )(cache_ckv, cache_kpe_slot, wuk_arr, wuv_arr)


def _attend(q_ref, segments, o_ref):
    lq = q_ref.shape[0]
    lane = lax.broadcasted_iota(jnp.int32, (lq, 2 * VD), 1)
    nt = (((1,), (1,)), ((), ()))
    for pair in range(HEADS // 2):
        outs = []
        for hd in (2 * pair, 2 * pair + 1):
            qh = q_ref[:, hd * SLOT:(hd + 1) * SLOT]
            scores = [lax.dot_general(qh, k_ref[:, hd * SLOT:(hd + 1) * SLOT], nt,
                                      preferred_element_type=F32) for k_ref, _ in segments]
            mx = functools.reduce(jnp.maximum, [jnp.max(s, -1, keepdims=True) for s in scores])
            ps = [jnp.exp(s - mx) for s in scores]
            den = functools.reduce(jnp.add, [jnp.sum(p, -1, keepdims=True) for p in ps])
            acc = functools.reduce(jnp.add, [
                jnp.dot(p.astype(BF16), v_ref[:, pair * 2 * VD:(pair + 1) * 2 * VD],
                        preferred_element_type=F32) for p, (_, v_ref) in zip(ps, segments)])
            outs.append(acc / den)
        o_ref[:, pair * 2 * VD:(pair + 1) * 2 * VD] = jnp.where(lane < VD, outs[0], outs[1]).astype(BF16)


def _attn_ctx_kernel(q_ref, k_ref, v_ref, o_ref):
    _attend(q_ref, [(k_ref, v_ref)], o_ref)


def _attn_lat_kernel(q_ref, kc_ref, vc_ref, k_ref, v_ref, o_ref):
    _attend(q_ref, [(kc_ref.at[0], vc_ref.at[0]), (k_ref, v_ref)], o_ref)


def _attention(q, k, v, kc, vc):
    kw, vw = HEADS * SLOT, HEADS * VD
    ctx = pl.pallas_call(
        _attn_ctx_kernel,
        out_shape=jax.ShapeDtypeStruct((T_CTX, vw), BF16),
        grid=(N_CTX_B,),
        in_specs=[pl.BlockSpec((CTX_L, kw), lambda b: (b, 0)),
                  pl.BlockSpec((CTX_L, kw), lambda b: (b, 0)),
                  pl.BlockSpec((CTX_L, vw), lambda b: (b, 0))],
        out_specs=pl.BlockSpec((CTX_L, vw), lambda b: (b, 0)),
        compiler_params=_cparams("parallel"),
        name="attn_ctx",
    )(q, k, v)
    nq = LAT_L // TM
    off = T_CTX // LAT_L
    lat = pl.pallas_call(
        _attn_lat_kernel,
        out_shape=jax.ShapeDtypeStruct((T_LAT, vw), BF16),
        grid=(N_LAT_B, nq),
        in_specs=[pl.BlockSpec((TM, kw), lambda b, j: (CTX_TILES + b * nq + j, 0)),
                  pl.BlockSpec((1, PAST, kw), lambda b, j: (b, 0, 0)),
                  pl.BlockSpec((1, PAST, vw), lambda b, j: (b, 0, 0)),
                  pl.BlockSpec((LAT_L, kw), lambda b, j: (off + b, 0)),
                  pl.BlockSpec((LAT_L, vw), lambda b, j: (off + b, 0))],
        out_specs=pl.BlockSpec((TM, vw), lambda b, j: (b * nq + j, 0)),
        compiler_params=_cparams("parallel", "parallel"),
        name="attn_lat",
    )(q, kc, vc, k, v)
    return ctx, lat


def _hyena_kernel(hy_ref, cw_ref, cb_ref, c_ref, sf_ref, sb_ref, kr_ref, ki_ref, krn_ref, hb_ref,
                  o_ref, *, L):
    x = hy_ref[...]
    row = lax.broadcasted_iota(jnp.int32, x.shape, 0)
    prev = jnp.where(row == 0, 0.0, pltpu.roll(x, 1, 0))
    nxt = jnp.where(row == L - 1, 0.0, pltpu.roll(x, L - 1, 0))
    z = prev * cw_ref[0:1] + x * cw_ref[1:2] + nxt * cw_ref[2:3] + cb_ref[...]
    y = z[:, 2 * HY_W:]
    for o in range(2):
        gate = z[:, o * HY_W:(o + 1) * HY_W]
        yb = y.astype(BF16)
        a_re = jnp.dot(c_ref[...], yb, preferred_element_type=F32)
        a_im = jnp.dot(sf_ref[...], yb, preferred_element_type=F32)
        ki = ki_ref[o]
        z_re = a_re * kr_ref[o] + a_im * ki
        z_im = a_im * krn_ref[o] - a_re * ki
        conv = (jnp.dot(c_ref[...], z_re.astype(BF16), preferred_element_type=F32)
                + jnp.dot(sb_ref[...], z_im.astype(BF16), preferred_element_type=F32))
        y = gate * (conv + y * hb_ref[o:o + 1])
    o_ref[...] = y.astype(BF16)


def _hyena_group(hy, L, nb, blk0, dft, spectra, conv_w, conv_b, hy_bias):
    c, sf, sb, _ = dft
    kr, ki, krn = spectra
    return pl.pallas_call(
        functools.partial(_hyena_kernel, L=L),
        out_shape=jax.ShapeDtypeStruct((nb * L, HY_W), BF16),
        grid=(nb,),
        in_specs=[pl.BlockSpec((L, 3 * HY_W), lambda b: (blk0 + b, 0)),
                  _full((3, 3 * HY_W)), _full((1, 3 * HY_W)),
                  _full((L, L)), _full((L, L)), _full((L, L)),
                  _full((2, L, HY_W)), _full((2, L, HY_W)), _full((2, L, HY_W)),
                  _full((2, HY_W))],
        out_specs=pl.BlockSpec((L, HY_W), lambda b: (b, 0)),
        compiler_params=_cparams("parallel"),
        name=f"hyena_{L}",
    )(hy, conv_w, conv_b.reshape(1, 3 * HY_W), c, sf, sb, kr, ki, krn, hy_bias)


def _out_kernel(a_ref, bc_ref, bl_ref, mc_ref, ml_ref, w_ref, xc_ref, xl_ref, ng_ref, nb_ref, mod_ref,
                g_ref, be_ref, wr_ref, br_ref, tri_ref, upper_ref, x1_ref, xloc_ref, route_ref, cnt_ref,
                *, pre_norm):
    dot = functools.partial(jnp.dot, preferred_element_type=F32)
    mixed = (dot(a_ref[...], w_ref[0:GM_W]) + dot(_pair_read(bc_ref, bl_ref), w_ref[GM_W:GM_W + HY_W])
             + dot(_pair_read(mc_ref, ml_ref), w_ref[GM_W + HY_W:]))
    x = _pair_read(xc_ref, xl_ref)
    if pre_norm:
        x = _layer_norm(x, ng_ref[...], nb_ref[...])
    m = mod_ref[0]
    x1 = _layer_norm(ALPHA * x + m[2:3] * mixed, g_ref[...], be_ref[...])
    x1_ref[...] = x1
    h2 = x1 * (1.0 + m[4:5]) + m[3:4]

    h_hi = h2.astype(BF16)
    h_lo = (h2 - h_hi.astype(F32)).astype(BF16)
    w_hi = wr_ref[...].astype(BF16)
    w_lo = (wr_ref[...] - w_hi.astype(F32)).astype(BF16)
    part = dot(h_hi, jnp.concatenate([w_hi, w_lo], axis=1))
    logits = part[:, :128] + part[:, 128:] + dot(h_lo, w_hi) + br_ref[...]
    lane = lax.broadcasted_iota(jnp.int32, logits.shape, 1)
    lanef = lane.astype(F32)
    big = jnp.float32(1e9)
    ninf = jnp.float32(-jnp.inf)
    is_g = lane < N_GROUPS
    gl = jnp.where(is_g, logits, ninf)
    gmax = jnp.max(gl, -1, keepdims=True)
    gidx = jnp.min(jnp.where(gl == gmax, lanef, big), -1, keepdims=True)
    gw = 1.0 / jnp.sum(jnp.where(is_g, jnp.exp(logits - gmax), 0.0), -1, keepdims=True)
    ex = lane - N_GROUPS
    in_group = (ex >= 0) & (ex < N_EXP) & ((ex // EPG).astype(F32) == gidx)
    el = jnp.where(in_group, logits, ninf)
    v1 = jnp.max(el, -1, keepdims=True)
    i1 = jnp.min(jnp.where(el == v1, lanef, big), -1, keepdims=True)
    el2 = jnp.where(lanef == i1, ninf, el)
    v2 = jnp.max(el2, -1, keepdims=True)
    i2 = jnp.min(jnp.where(el2 == v2, lanef, big), -1, keepdims=True)
    e21 = jnp.exp(v2 - v1)
    w1 = gw / (1.0 + e21)
    w2 = gw * e21 / (1.0 + e21)
    e1 = i1 - N_GROUPS
    e2 = i2 - N_GROUPS

    oh1 = jnp.where(lanef == e1, 1.0, 0.0)
    oh2 = jnp.where(lanef == e2, 1.0, 0.0)
    ex1 = dot(tri_ref[...], oh1.astype(BF16))
    ex2 = dot(tri_ref[...], oh2.astype(BF16))
    col1 = jnp.sum(oh1, axis=0, keepdims=True)
    col2 = jnp.sum(oh2, axis=0, keepdims=True)
    n = col1 + col2
    run = jnp.floor((n + (BLK - 1.0)) * (1.0 / BLK)) * BLK
    start = dot(jnp.broadcast_to(run, (8, 128)).astype(BF16), upper_ref[...])[0:1]
    loc1 = jnp.sum(oh1 * (start + ex1), -1, keepdims=True)
    loc2 = jnp.sum(oh2 * (start + col1 + ex2), -1, keepdims=True)
    cnt_ref[...] = jnp.broadcast_to(n, cnt_ref.shape)

    route = jnp.zeros_like(logits)
    for j, val in enumerate((e1, e2, w1, w2, loc1, loc2)):
        route = jnp.where(lane == j, val, route)
    route_ref[...] = route

    loc1_row = jnp.transpose(jnp.broadcast_to(loc1, (TM, 128)))[0:1].astype(jnp.int32)
    loc2_row = jnp.transpose(jnp.broadcast_to(loc2, (TM, 128)))[0:1].astype(jnp.int32)
    slot = lax.broadcasted_iota(jnp.int32, (LOC, TM), 0)
    perm = jnp.where((slot == loc1_row) | (slot == loc2_row), 1.0, 0.0).astype(BF16)
    xloc_ref[...] = dot(perm, h_hi).astype(BF16)


def _out_proj(a, hyb_pair, att_pair, w_out, x_pair, norm, pre_norm, mod_l, g, b, w_route, b_route, tri,
              upper):
    tile = lambda n: pl.BlockSpec((TM, n), lambda i: (i, 0))
    return pl.pallas_call(
        functools.partial(_out_kernel, pre_norm=pre_norm),
        out_shape=(jax.ShapeDtypeStruct((T, D), F32), jax.ShapeDtypeStruct((N_TILES * LOC, D), BF16),
                   jax.ShapeDtypeStruct((T, 128), F32), jax.ShapeDtypeStruct((N_TILES * 8, 128), F32)),
        grid=(N_TILES,),
        in_specs=[tile(GM_W)] + _pair_specs(HY_W) + _pair_specs(HEADS * VD) + [_full((D, D))]
        + _pair_specs(D) + [
                  _full((1, D)), _full((1, D)),
                  pl.BlockSpec((1, 6, D), lambda i: (_cond_of_tile(i), 0, 0)),
                  _full((1, D)), _full((1, D)), _full((D, 128)), _full((1, 128)), _full((TM, TM)),
                  _full((128, 128))],
        out_specs=(tile(D), pl.BlockSpec((LOC, D), lambda i: (i, 0)), tile(128),
                   pl.BlockSpec((8, 128), lambda i: (i, 0))),
        compiler_params=_cparams("parallel"),
        name="out_proj_route",
    )(a, *hyb_pair, *att_pair, w_out, *x_pair, *norm, mod_l, g, b, w_route, b_route, tri, upper)


X_RING = 4


def _expert_kernel(off_ref, cnt_ref, src_ref, x_hbm, wg_ref, wu_ref, wd_ref, y_hbm,
                   xbuf, ybuf, wgu_b, wd_b, sem, osem):
    e = pl.program_id(0)
    first_tile = off_ref[e]
    n_tiles = cnt_ref[e]
    total = off_ref[N_EXP - 1] + cnt_ref[N_EXP - 1]

    def block_copies(t, s):
        return [pltpu.make_async_copy(
            x_hbm.at[pl.ds(pl.multiple_of(src_ref[t * BLK_PER_TILE + b] * BLK, BLK), BLK), :],
            xbuf.at[s, pl.ds(b * BLK, BLK), :], sem.at[s]) for b in range(BLK_PER_TILE)]

    def out_copy(t, s):
        return pltpu.make_async_copy(ybuf.at[s], y_hbm.at[pl.ds(pl.multiple_of(t * TM, TM), TM), :],
                                     osem.at[s])

    @pl.when(e == 0)
    def _():
        for t0 in range(X_RING - 1):
            @pl.when(t0 < total)
            def _():
                for cp in block_copies(t0, t0):
                    cp.start()

    @pl.when(n_tiles > 0)
    def _():
        wgu_b[:, 0:D_EXP] = wg_ref[0, 0].astype(BF16)
        wgu_b[:, D_EXP:2 * D_EXP] = wu_ref[0, 0].astype(BF16)
        wd_b[...] = wd_ref[0, 0].astype(BF16)

    def tile_body(t, carry):
        s = t % 2
        xs = t % X_RING

        @pl.when(t + X_RING - 1 < total)
        def _():
            for cp in block_copies(t + X_RING - 1, (t + X_RING - 1) % X_RING):
                cp.start()

        for cp in block_copies(t, xs):
            cp.wait()

        @pl.when(t >= 2)
        def _():
            out_copy(t - 2, s).wait()

        gu = jnp.dot(xbuf[xs], wgu_b[...], preferred_element_type=F32)
        gate, up = gu[:, :D_EXP], gu[:, D_EXP:]
        hid = gate / (1.0 + jnp.exp(-gate)) * up
        ybuf[s] = jnp.dot(hid.astype(BF16), wd_b[...], preferred_element_type=F32).astype(BF16)
        out_copy(t, s).start()
        return carry

    lax.fori_loop(first_tile, first_tile + n_tiles, tile_body, 0)

    @pl.when(e == N_EXP - 1)
    def _():
        @pl.when(total >= 2)
        def _():
            out_copy(total - 2, total % 2).wait()

        out_copy(total - 1, (total - 1) % 2).wait()
        ybuf[0] = jnp.zeros((TM, D), BF16)

        def zero_start(t, carry):
            out_copy(t, 0).start()
            return carry

        def zero_wait(t, carry):
            out_copy(t, 0).wait()
            return carry

        lax.fori_loop(total, MOE_TILES, zero_start, 0)
        lax.fori_loop(total, MOE_TILES, zero_wait, 0)


def _experts(l, tables, xloc, w_gate, w_up, w_down):
    grid_spec = pltpu.PrefetchScalarGridSpec(
        num_scalar_prefetch=len(tables),
        grid=(N_EXP,),
        in_specs=[pl.BlockSpec(memory_space=pl.ANY),
                  pl.BlockSpec((1, 1, D, D_EXP), lambda e, *_: (l, e, 0, 0)),
                  pl.BlockSpec((1, 1, D, D_EXP), lambda e, *_: (l, e, 0, 0)),
                  pl.BlockSpec((1, 1, D_EXP, D), lambda e, *_: (l, e, 0, 0))],
        out_specs=pl.BlockSpec(memory_space=pl.ANY),
        scratch_shapes=[pltpu.VMEM((X_RING, TM, D), BF16), pltpu.VMEM((2, TM, D), BF16),
                        pltpu.VMEM((D, 2 * D_EXP), BF16), pltpu.VMEM((D_EXP, D), BF16),
                        pltpu.SemaphoreType.DMA((X_RING,)), pltpu.SemaphoreType.DMA((2,))],
    )
    return pl.pallas_call(
        _expert_kernel,
        out_shape=jax.ShapeDtypeStruct((MOE_ROWS, D), BF16),
        grid_spec=grid_spec,
        compiler_params=_cparams("arbitrary"),
        name="experts",
    )(*tables, xloc, w_gate, w_up, w_down)


def _combine_kernel(nblk_ref, gsrc_ref, y_hbm, route_ref, x1_ref, mod_ref, g_ref, b_ref, oc_ref, ol_ref,
                    ybuf, sem):
    i = pl.program_id(0)

    slot_i = i % 2

    def block_copy(t, s, lb):
        src = pl.multiple_of(gsrc_ref[t * LOC_BLKS + lb] * BLK, BLK)
        return pltpu.make_async_copy(y_hbm.at[pl.ds(src, BLK), :],
                                     ybuf.at[s, pl.ds(pl.multiple_of(lb * BLK, BLK), BLK), :], sem.at[s])

    def fetch(t, s):
        def start(lb, c):
            block_copy(t, s, lb).start()
            return c
        lax.fori_loop(0, nblk_ref[t], start, 0)

    @pl.when(i == 0)
    def _():
        ybuf[...] = jnp.zeros_like(ybuf)
        fetch(0, 0)

    @pl.when(i + 1 < N_TILES)
    def _():
        fetch(jnp.minimum(i + 1, N_TILES - 1), 1 - slot_i)

    def wait(lb, c):
        block_copy(i, slot_i, lb).wait()
        return c

    lax.fori_loop(0, nblk_ref[i], wait, 0)
    route = route_ref[...]
    yb = ybuf[slot_i]
    slot = lax.broadcasted_iota(jnp.int32, (TM, LOC), 1)
    pick = (jnp.where(slot == route[:, 4:5].astype(jnp.int32), route[:, 2:3], 0.0)
            + jnp.where(slot == route[:, 5:6].astype(jnp.int32), route[:, 3:4], 0.0)).astype(BF16)
    moe = jnp.dot(pick, yb, preferred_element_type=F32)
    m = mod_ref[0]
    res = _layer_norm(ALPHA * x1_ref[...] + m[5:6] * moe, g_ref[...], b_ref[...])

    @pl.when(i < CTX_TILES)
    def _():
        oc_ref[...] = res

    @pl.when(i >= CTX_TILES)
    def _():
        ol_ref[...] = res


def _combine(nblk, gsrc, y, route, x1, mod_l, g, b):
    out_shape = (jax.ShapeDtypeStruct((T_CTX, D), F32), jax.ShapeDtypeStruct((T_LAT, D), F32))
    out_specs = tuple(_pair_specs(D))
    grid_spec = pltpu.PrefetchScalarGridSpec(
        num_scalar_prefetch=2,
        grid=(N_TILES,),
        in_specs=[pl.BlockSpec(memory_space=pl.ANY),
                  pl.BlockSpec((TM, 128), lambda i, nb, gs: (i, 0)),
                  pl.BlockSpec((TM, D), lambda i, nb, gs: (i, 0)),
                  pl.BlockSpec((1, 6, D), lambda i, nb, gs: (_cond_of_tile(i), 0, 0)),
                  pl.BlockSpec((1, D), lambda i, nb, gs: (0, 0)),
                  pl.BlockSpec((1, D), lambda i, nb, gs: (0, 0))],
        out_specs=out_specs,
        scratch_shapes=[pltpu.VMEM((2, LOC, D), BF16), pltpu.SemaphoreType.DMA((2,))],
    )
    return pl.pallas_call(
        _combine_kernel,
        out_shape=out_shape,
        grid_spec=grid_spec,
        compiler_params=_cparams("arbitrary"),
        name="moe_combine",
    )(nblk, gsrc, y, route, x1, mod_l, g, b)


def _prefix_pick(starts, query, table):
    delta = table - jnp.concatenate([jnp.zeros_like(table[..., :1]), table[..., :-1]], axis=-1)
    return jnp.sum(jnp.where(starts <= query, delta, 0), axis=-1)


def _dispatch_tables(cnt):
    i32 = jnp.int32
    run = (cnt + BLK - 1) // BLK * BLK
    loc_start = jnp.cumsum(run, axis=1) - run
    nblk_loc = jnp.sum(run, axis=1) // BLK
    seg_rows = jnp.sum(run, axis=0)
    seg_tiles = (seg_rows + TM - 1) // TM
    tile_end = jnp.cumsum(seg_tiles)
    seg_start = (tile_end - seg_tiles) * TM
    glob_start = seg_start[None, :] + jnp.cumsum(run, axis=0) - run

    g_blk = (glob_start.T.reshape(1, -1)) // BLK
    n_blk = (run.T.reshape(1, -1)) // BLK
    l_blk = ((jnp.arange(N_TILES, dtype=i32) * LOC)[None, :] + loc_start.T).reshape(1, -1) // BLK
    gb = jnp.arange(MOE_ROWS // BLK, dtype=i32)[:, None]
    off = gb[:, 0] - _prefix_pick(g_blk, gb, g_blk)
    src_blk = jnp.where(off < _prefix_pick(g_blk, gb, n_blk), _prefix_pick(g_blk, gb, l_blk) + off,
                        ZERO_BLK).astype(i32)

    pos = (jnp.arange(LOC_BLKS, dtype=i32) * BLK)[None, :, None]
    shift = _prefix_pick(loc_start[:, None, :], pos, (glob_start - loc_start)[:, None, :])
    gsrc = ((pos[:, :, 0] + shift) // BLK).astype(i32)
    expert_tables = ((tile_end - seg_tiles).astype(i32), seg_tiles.astype(i32), src_blk)
    return expert_tables, nblk_loc.astype(i32), gsrc.reshape(-1)


def _swap_pairs(w):
    return w.reshape(w.shape[:-1] + (ROPE // 2, 2))[..., ::-1].reshape(w.shape)


def _rope_slot(pe):
    return jnp.pad(pe, [(0, 0)] * (pe.ndim - 1) + [(NOPE, SLOT - NOPE - ROPE)])


def _w_in_layout(w):
    pe = w[:, 1664:]
    return jnp.concatenate([w[:, :1664], _rope_slot(pe), _rope_slot(_swap_pairs(pe))], axis=1).astype(BF16)


def _wuq_layout(w):
    w = w.reshape(Q_LORA, HEADS, NOPE + ROPE)
    nope, pe = w[..., :NOPE], w[..., NOPE:]
    plain = jnp.concatenate([nope, jnp.pad(pe, ((0, 0), (0, 0), (0, SLOT - NOPE - ROPE)))], axis=-1)
    swapped = _rope_slot(_swap_pairs(pe))
    return jnp.concatenate([plain.reshape(Q_LORA, HEADS * SLOT),
                            swapped.reshape(Q_LORA, HEADS * SLOT)], axis=1).astype(BF16)


def _wukv_layout(w):
    w = w.reshape(DEPTH, KV_LORA, HEADS, NOPE + VD)
    wk = jnp.pad(w[..., :NOPE], ((0, 0), (0, 0), (0, 0), (0, SLOT - NOPE)))
    return (wk.reshape(DEPTH, KV_LORA, HEADS * SLOT).astype(BF16),
            w[..., NOPE:].reshape(DEPTH, KV_LORA, HEADS * VD).astype(BF16))


def kernel(x_prompt, x_sample, c, cache_ckv, cache_kpe, c_ctx, ln_in_g, ln_in_b, w_mod, b_mod, w_in,
           gm_ln_g, gm_ln_b, gm_ws, gm_bs, hy_conv_w, hy_conv_b, hy_f_w1, hy_f_b1, hy_f_w2, hy_f_b2,
           hy_f_w3, hy_f_freq, hy_bias, mla_gq, mla_gkv, mla_wuq, mla_wukv, w_out, ln1_g, ln1_b,
           ln2_g, ln2_b, moe_w_gr, moe_b_gr, moe_w_er, moe_b_er, moe_w_gate, moe_w_up, moe_w_down):
    rope = _rope_tables()
    dft = {L: _dft_tables(L) for L in (CTX_L, LAT_L)}
    pos_tab = {L: _hyena_positions(L) for L in (CTX_L, LAT_L)}
    hd = np.arange(GM_W) // GM_HD
    avg = jnp.asarray((hd[:, None] == hd[None, :]) / GM_HD, BF16)
    tri = jnp.asarray(np.tril(np.ones((TM, TM)), -1), BF16)
    upper = jnp.asarray(np.triu(np.ones((128, 128)), 1), BF16)

    cond8 = jnp.concatenate([c_ctx[None], c, jnp.zeros((8 - 1 - N_LAT_B, D), F32)], axis=0)
    mod = _modulation(cond8, w_mod, b_mod).reshape(DEPTH, 8, 6, D)

    wuk_all, wuv_all = _wukv_layout(mla_wukv)
    kc_all, vc_all = _cache_kv(cache_ckv, _rope_slot(cache_kpe), wuk_all, wuv_all)

    x_pair = (x_prompt.reshape(T_CTX, D), x_sample.reshape(T_LAT, D))
    norm = (ln_in_g.reshape(1, D), ln_in_b.reshape(1, D))
    ckv_states, kpe_states = [], []
    for l in range(DEPTH):
        w_in_ext = _w_in_layout(w_in[l])
        wuq_arr = _wuq_layout(mla_wuq[l])
        bs_full = jnp.repeat(gm_bs[l].T, GM_HD, axis=1)
        a, hy, q, k, v, ckv, kpe = _in_proj(
            x_pair, norm, l == 0, mod[l], w_in_ext, gm_ln_g[l].reshape(1, GM_W), gm_ln_b[l].reshape(1, GM_W), avg,
            gm_ws[l].astype(BF16), bs_full, mla_gq[l].reshape(1, Q_LORA), mla_gkv[l].reshape(1, KV_LORA),
            wuq_arr, wuk_all[l], wuv_all[l], rope)
        ckv_states.append(ckv[:T_CTX].reshape(N_CTX_B, CTX_L, KV_LORA))
        kpe_states.append(kpe[:T_CTX, NOPE:NOPE + ROPE].reshape(N_CTX_B, CTX_L, ROPE))

        hyb = []
        for L, nb, blk0 in ((CTX_L, N_CTX_B, 0), (LAT_L, N_LAT_B, T_CTX // LAT_L)):
            z, dec = pos_tab[L]
            spectra = _filter_spectra(L, z, dec, dft[L], hy_f_w1[l], hy_f_b1[l], hy_f_w2[l],
                                      hy_f_b2[l], hy_f_w3[l], hy_f_freq[l])
            hyb.append(_hyena_group(hy, L, nb, blk0, dft[L], spectra, hy_conv_w[l], hy_conv_b[l],
                                    hy_bias[l]))

        att = _attention(q, k, v, kc_all[l], vc_all[l])

        w_route = jnp.pad(jnp.concatenate([moe_w_gr[l], moe_w_er[l].reshape(D, N_EXP)], axis=1),
                          ((0, 0), (0, 128 - N_GROUPS - N_EXP)))
        b_route =jnp.pad(jnp.concatenate([moe_b_gr[l], moe_b_er[l].reshape(N_EXP)]),
                          (0, 128 - N_GROUPS - N_EXP)).reshape(1, 128)
        x1, xloc, route, counts = _out_proj(a, hyb, att, w_out[l].astype(BF16), x_pair, norm, l == 0, mod[l],
                                            ln1_g[l].reshape(1, D), ln1_b[l].reshape(1, D),
                                            w_route, b_route, tri, upper)
        cnt = counts.reshape(N_TILES, 8, 128)[:, 0, :N_EXP].astype(jnp.int32)
        expert_tables, nblk_loc, gsrc = _dispatch_tables(cnt)
        y = _experts(l, expert_tables, xloc, moe_w_gate, moe_w_up, moe_w_down)
        x_pair = _combine(nblk_loc, gsrc, y, route, x1, mod[l], ln2_g[l].reshape(1, D), ln2_b[l].reshape(1, D))

    y_prompt = x_pair[0].reshape(N_CTX_B, CTX_L, D)
    y_sample = x_pair[1].reshape(N_LAT_B, LAT_L, D)
    return (y_prompt, y_sample, jnp.stack(ckv_states, axis=1), jnp.stack(kpe_states, axis=1))
```

```python
import functools
import math

import numpy as np
import jax
import jax.numpy as jnp
from jax import lax
from jax.experimental import pallas as pl
from jax.experimental.pallas import tpu as pltpu

D = 1024
N_CTX_B, CTX_L = 16, 256
N_LAT_B, LAT_L = 4, 1024
DEPTH = 2
T_CTX = N_CTX_B * CTX_L
T_LAT = N_LAT_B * LAT_L
T = T_CTX + T_LAT
PAST = 256
GRID_W = 64

GM_HEADS, GM_HD, GM_W, CHUNK = 4, 64, 256, 128
HY_W, HY_EMB, HY_FH = 256, 33, 64
NOPE, ROPE, VD, HEADS = 64, 32, 64, 8
Q_LORA, KV_LORA = 256, 128
SLOT = 128
N_GROUPS, EPG, N_EXP, D_EXP = 4, 8, 32, 256
ALPHA = (2.0 * DEPTH) ** 0.25
LN_EPS, RMS_EPS = 1e-5, 1e-6
ROPE_THETA = 10000.0

TM = 256
N_TILES = T // TM
CTX_TILES = T_CTX // TM
LAT_TILES_PER_B = LAT_L // TM
N_ASSIGN = 2 * T
BLK = 16
BLK_PER_TILE = TM // BLK
LOC = 2 * TM + N_EXP * BLK
LOC_BLKS = LOC // BLK
ZERO_BLK = LOC_BLKS - 1
MOE_TILES = (N_ASSIGN + N_TILES * N_EXP * (BLK - 1)) // TM + N_EXP
MOE_ROWS = MOE_TILES * TM
IN_EXT = 1792

F32, BF16 = jnp.float32, jnp.bfloat16
VMEM_LIMIT = 52 * 1024 * 1024


def _cparams(*sem):
    return pltpu.CompilerParams(dimension_semantics=sem, vmem_limit_bytes=VMEM_LIMIT)


def _cond_of_tile(i):
    return jnp.where(i < CTX_TILES, 0, 1 + (i - CTX_TILES) // LAT_TILES_PER_B)


def _rope_block_of_tile(i):
    return jnp.where(i < CTX_TILES, 0, 1 + (i - CTX_TILES) % LAT_TILES_PER_B)


def _full(shape):
    n = len(shape)
    return pl.BlockSpec(shape, lambda *_: (0,) * n)


def _pair_specs(n):
    return [pl.BlockSpec((TM, n), lambda i, *_: (jnp.minimum(i, CTX_TILES - 1), 0)),
            pl.BlockSpec((TM, n), lambda i, *_: (jnp.maximum(i - CTX_TILES, 0), 0))]


def _pair_read(c_ref, l_ref):
    return jnp.where(pl.program_id(0) < CTX_TILES, c_ref[...], l_ref[...])


def _layer_norm(x, g, b):
    mu = jnp.mean(x, -1, keepdims=True)
    xc = x - mu
    var = jnp.mean(xc * xc, -1, keepdims=True)
    return xc * lax.rsqrt(var + LN_EPS) * g + b


def _rms_norm(x, g):
    return x * lax.rsqrt(jnp.mean(x * x, -1, keepdims=True) + RMS_EPS) * g


def _bdot(a, b):
    return jnp.dot(a.astype(BF16), b.astype(BF16), preferred_element_type=F32)


def _dot3(x, w):
    x_hi = x.astype(BF16)
    x_lo = (x - x_hi.astype(F32)).astype(BF16)
    w_hi = w.astype(BF16)
    w_lo = (w - w_hi.astype(F32)).astype(BF16)
    dot = functools.partial(jnp.dot, preferred_element_type=F32)
    return dot(x_hi, w_hi) + dot(x_hi, w_lo) + dot(x_lo, w_hi)


def _swap_pair_lanes(x):
    n = x.shape[1]
    lane = lax.broadcasted_iota(jnp.int32, x.shape, 1)
    return jnp.where(lane % 2 == 0, pltpu.roll(x, n - 1, 1), pltpu.roll(x, 1, 1))


def _split2_dot(x, w_bf16):
    hi = x.astype(BF16)
    lo = (x - hi.astype(F32)).astype(BF16)
    dot = functools.partial(jnp.dot, preferred_element_type=F32)
    return dot(hi, w_bf16) + dot(lo, w_bf16)


def _dft_tables(L):
    f = np.arange(L, dtype=np.int64)
    ft = np.outer(f, f) % (2 * L)
    ang = np.pi * ft / L
    c = np.cos(ang)
    s = np.sin(ang)
    alt = np.where(f % 2 == 0, 1.0, -1.0)
    sf = s.copy()
    sf[0, :] = alt
    return (jnp.asarray(c, F32).astype(BF16), jnp.asarray(sf, F32).astype(BF16),
            jnp.asarray(sf.T.copy(), F32).astype(BF16), jnp.asarray(alt[:, None], F32))


def _rope_tables():
    rows = LAT_L // GRID_W
    row = np.repeat(np.arange(rows, dtype=np.float64), GRID_W)
    col = np.tile(np.arange(GRID_W, dtype=np.float64), rows)
    n = ROPE // 4
    inv = ROPE_THETA ** (-np.arange(n, dtype=np.float64) / n)
    ang = np.concatenate([row[:, None] * inv, col[:, None] * inv], -1)
    cos = np.zeros((CTX_L + LAT_L, SLOT))
    sin = np.zeros((CTX_L + LAT_L, SLOT))
    cos[:, :NOPE + ROPE] = 1.0
    cos[CTX_L:, NOPE:NOPE + ROPE:2] = np.cos(ang)
    cos[CTX_L:, NOPE + 1:NOPE + ROPE:2] = np.cos(ang)
    sin[CTX_L:, NOPE:NOPE + ROPE:2] = -np.sin(ang)
    sin[CTX_L:, NOPE + 1:NOPE + ROPE:2] = np.sin(ang)
    scale = 1.0 / math.sqrt(NOPE + ROPE)
    return (jnp.asarray(cos * scale, F32), jnp.asarray(sin * scale, F32),
            jnp.asarray(cos, F32), jnp.asarray(sin, F32))


def _hyena_positions(L):
    t = jnp.linspace(0.0, 1.0, L, dtype=F32)[:, None]
    bands = (HY_EMB - 1) // 2
    w = 2.0 * math.pi * jnp.arange(L, dtype=F32)[:, None] / L
    f = jnp.linspace(1e-4, bands - 1, bands, dtype=F32)[None]
    z = jnp.concatenate([t, jnp.cos(f * w), -jnp.sin(f * w)], -1)
    z = jnp.pad(z, ((0, 0), (0, 128 - HY_EMB)))
    min_decay = math.log(1e-2) / 1.5
    max_decay = math.log(1e-2) / 0.3
    deltas = jnp.abs(jnp.linspace(min_decay, max_decay, HY_W, dtype=F32))
    return z, jnp.exp(-t * deltas)


def _mod_kernel(c_ref, w_ref, b_ref, o_ref):
    c = c_ref[...]
    s = c / (1.0 + jnp.exp(-c))
    o_ref[0] = _bdot(s, w_ref[0]) + b_ref[0]


def _modulation(cond8, w_mod, b_mod):
    tn = 1536
    return pl.pallas_call(
        _mod_kernel,
        out_shape=jax.ShapeDtypeStruct((DEPTH, 8, 6 * D), F32),
        grid=(DEPTH, 6 * D // tn),
        in_specs=[pl.BlockSpec((8, D), lambda l, j: (0, 0)),
                  pl.BlockSpec((1, D, tn), lambda l, j: (l, 0, j)),
                  pl.BlockSpec((1, 1, tn), lambda l, j: (l, 0, j))],
        out_specs=pl.BlockSpec((1, 8, tn), lambda l, j: (l, 0, j)),
        compiler_params=_cparams("parallel", "parallel"),
        name="modulation",
    )(cond8, w_mod, b_mod.reshape(DEPTH, 1, 6 * D))


def _filter_kernel(z_ref, w1_ref, b1_ref, w2_ref, b2_ref, w3_ref, fr_ref, dec_ref, c_ref, sf_ref,
                   alt_ref, kr_ref, ki_ref, krn_ref, *, L):
    h1 = jnp.sin(fr_ref[0:1] * (_dot3(z_ref[...], w1_ref[...]) + b1_ref[...]))
    h2 = jnp.sin(fr_ref[1:2] * (_dot3(h1, w2_ref[...]) + b2_ref[...]))
    filt = _dot3(h2, w3_ref[...])
    row = lax.broadcasted_iota(jnp.int32, (L, HY_W), 0)
    dec = dec_ref[...]
    for o in range(2):
        fwd = filt[:, o * 2 * HY_W:o * 2 * HY_W + HY_W] * dec
        bwd = jnp.where(row == 0, 0.0, filt[:, o * 2 * HY_W + HY_W:(o + 1) * 2 * HY_W] * dec)
        sm = fwd + bwd
        df = fwd - bwd
        kr = _bdot(c_ref[...], sm)
        ki = -_bdot(sf_ref[...], df)
        kr_nyq = jnp.sum(alt_ref[...] * sm, axis=0, keepdims=True)
        kr_ref[o] = jnp.where(row == 0, kr * (0.5 / L), kr * (1.0 / L))
        ki_ref[o] = jnp.where(row == 0, 0.0, ki * (1.0 / L))
        krn_ref[o] = jnp.where(row == 0, kr_nyq * (0.5 / L), kr * (1.0 / L))


def _filter_spectra(L, z, dec, dft, w1, b1, w2, b2, w3, freq):
    c, sf, _, alt = dft
    pad = 128 - HY_FH
    w1p = jnp.pad(w1, ((0, 128 - HY_EMB), (0, pad)))
    w2p = jnp.pad(w2, ((0, pad), (0, pad)))
    w3p = jnp.pad(w3, ((0, pad), (0, 0)))
    b1p = jnp.pad(b1, (0, pad)).reshape(1, 128)
    b2p = jnp.pad(b2, (0, pad)).reshape(1, 128)
    frp = jnp.pad(freq, ((0, 0), (0, pad)))
    shp = jax.ShapeDtypeStruct((2, L, HY_W), F32)
    args = (z, w1p, b1p, w2p, b2p, w3p, frp, dec, c, sf, alt)
    return pl.pallas_call(
        functools.partial(_filter_kernel, L=L),
        out_shape=(shp, shp, shp),
        grid=(1,),
        in_specs=[_full(a.shape) for a in args],
        out_specs=(_full((2, L, HY_W)),) * 3,
        compiler_params=_cparams("arbitrary"),
        name=f"hyena_filter_{L}",
    )(*args)


def _in_kernel(xc_ref, xl_ref, ng_ref, nb_ref, mod_ref, w_ref, gg_ref, gb_ref, avg_ref, ws_ref, bs_ref,
               gq_ref, gkv_ref, wuq_ref, wuk_ref, wuv_ref, cq_ref, sq_ref, ck_ref, sk_ref,
               a_ref, hy_ref, q_ref, k_ref, v_ref, ckv_ref, kpe_ref, *, pre_norm):
    x = _pair_read(xc_ref, xl_ref)
    if pre_norm:
        x = _layer_norm(x, ng_ref[...], nb_ref[...])
    m = mod_ref[0]
    h = x * (1.0 + m[1:2]) + m[0:1]
    proj = _bdot(h, w_ref[0])

    gu = jax.nn.gelu(proj[:, 0:GM_W], approximate=True)
    gv = jax.nn.gelu(proj[:, GM_W:2 * GM_W], approximate=True)
    avg = avg_ref[...]
    mu = _split2_dot(gv, avg)
    vc = gv - mu
    var = _split2_dot(vc * vc, avg)
    vln = (vc * lax.rsqrt(var + LN_EPS) * gg_ref[...] + gb_ref[...]).astype(BF16)
    lane = lax.broadcasted_iota(jnp.int32, (CHUNK, GM_W), 1)
    for c in range(TM // CHUNK):
        vchunk = vln[c * CHUNK:(c + 1) * CHUNK]
        s = bs_ref[...]
        for hd in range(GM_HEADS):
            sh = jnp.dot(ws_ref[0, hd], vchunk, preferred_element_type=F32)
            s = s + jnp.where(lane // GM_HD == hd, sh, 0.0)
        a_ref[c * CHUNK:(c + 1) * CHUNK, :] = (gu[c * CHUNK:(c + 1) * CHUNK] * s).astype(BF16)

    hy_ref[...] = proj[:, 512:1280]

    cq = _rms_norm(proj[:, 1280:1536], gq_ref[...])
    qq = _bdot(cq, wuq_ref[0])
    cos_q = jnp.concatenate([cq_ref[...]] * HEADS, axis=1)
    sin_q = jnp.concatenate([sq_ref[...]] * HEADS, axis=1)
    q_ref[...] = (qq * cos_q + _swap_pair_lanes(qq) * sin_q).astype(BF16)

    ckv = _rms_norm(proj[:, 1536:1664], gkv_ref[...])
    ckv_ref[...] = ckv
    kpe = proj[:, 1664:1792]
    kpe_ref[...] = kpe
    krot = kpe * ck_ref[...] + _swap_pair_lanes(kpe) * sk_ref[...]
    kn = _bdot(ckv, wuk_ref[0])
    k_ref[...] = (kn + jnp.concatenate([krot] * HEADS, axis=1)).astype(BF16)
    v_ref[...] = _bdot(ckv, wuv_ref[0]).astype(BF16)


def _in_proj(l, x_pair, norm, mod_l, w_in_ext, gm_g, gm_b, avg, ws, bs_full, gq, gkv, wuq_arr,
             wuk_arr, wuv_arr, rope):
    layer = lambda *shape: pl.BlockSpec((1,) + shape, lambda i: (l,) + (0,) * len(shape))
    cos_q, sin_q, cos_k, sin_k = rope
    tile = lambda n: pl.BlockSpec((TM, n), lambda i: (i, 0))
    rope_spec = pl.BlockSpec((TM, SLOT), lambda i: (_rope_block_of_tile(i), 0))
    out_shapes = (jax.ShapeDtypeStruct((T, GM_W), BF16),
                  jax.ShapeDtypeStruct((T, 3 * HY_W), F32),
                  jax.ShapeDtypeStruct((T, HEADS * SLOT), BF16),
                  jax.ShapeDtypeStruct((T, HEADS * SLOT), BF16),
                  jax.ShapeDtypeStruct((T, HEADS * VD), BF16),
                  jax.ShapeDtypeStruct((T, KV_LORA), F32),
                  jax.ShapeDtypeStruct((T, SLOT), F32))
    return pl.pallas_call(
        functools.partial(_in_kernel, pre_norm=(l == 0)),
        out_shape=out_shapes,
        grid=(N_TILES,),
        in_specs=_pair_specs(D) + [
                  _full((1, D)), _full((1, D)),
                  pl.BlockSpec((1, 6, D), lambda i: (_cond_of_tile(i), 0, 0)),
                  layer(D, IN_EXT), _full((1, GM_W)), _full((1, GM_W)), _full((GM_W, GM_W)),
                  layer(GM_HEADS, CHUNK, CHUNK), _full((CHUNK, GM_W)),
                  _full((1, Q_LORA)), _full((1, KV_LORA)),
                  layer(Q_LORA, HEADS * SLOT), layer(KV_LORA, HEADS * SLOT),
                  layer(KV_LORA, HEADS * VD),
                  rope_spec, rope_spec, rope_spec, rope_spec],
        out_specs=(tile(GM_W), tile(3 * HY_W), tile(HEADS * SLOT), tile(HEADS * SLOT),
                   tile(HEADS * VD), tile(KV_LORA), tile(SLOT)),
        compiler_params=_cparams("parallel"),
        name="in_proj",
    )(*x_pair, *norm, mod_l, w_in_ext, gm_g, gm_b, avg, ws, bs_full, gq, gkv, wuq_arr, wuk_arr, wuv_arr,
      cos_q, sin_q, cos_k, sin_k)


def _cache_kernel(ckv_ref, kpe_ref, wuk_ref, wuv_ref, k_ref, v_ref):
    ckv = ckv_ref[0, 0]
    kn = _bdot(ckv, wuk_ref[0])
    k_ref[0, 0] = (kn + jnp.concatenate([kpe_ref[0, 0]] * HEADS, axis=1)).astype(BF16)
    v_ref[0, 0] = _bdot(ckv, wuv_ref[0]).astype(BF16)


def _cache_kv(cache_ckv, cache_kpe_slot, wuk_arr, wuv_arr):
    return pl.pallas_call(
        _cache_kernel,
        out_shape=(jax.ShapeDtypeStruct((DEPTH, N_LAT_B, PAST, HEADS * SLOT), BF16),
                   jax.ShapeDtypeStruct((DEPTH, N_LAT_B, PAST, HEADS * VD), BF16)),
        grid=(DEPTH, N_LAT_B),
        in_specs=[pl.BlockSpec((1, 1, PAST, KV_LORA), lambda l, b: (b, l, 0, 0)),
                  pl.BlockSpec((1, 1, PAST, SLOT), lambda l, b: (b, l, 0, 0)),
                  pl.BlockSpec((1, KV_LORA, HEADS * SLOT), lambda l, b: (l, 0, 0)),
                  pl.BlockSpec((1, KV_LORA, HEADS * VD), lambda l, b: (l, 0, 0))],
        out_specs=(pl.BlockSpec((1, 1, PAST, HEADS * SLOT), lambda l, b: (l, b, 0, 0)),
                   pl.BlockSpec((1, 1, PAST, HEADS * VD), lambda l, b: (l, b, 0, 0))),
        compiler_params=_cparams("parallel", "parallel"),
        name="cache_kv",
    )(cache_ckv, cache_kpe_slot, wuk_arr, wuv_arr)


def _attend(q_ref, segments, o_ref):
    lq = q_ref.shape[0]
    lane = lax.broadcasted_iota(jnp.int32, (lq, 2 * VD), 1)
    nt = (((1,), (1,)), ((), ()))
    for pair in range(HEADS // 2):
        outs = []
        for hd in (2 * pair, 2 * pair + 1):
            qh = q_ref[:, hd * SLOT:(hd + 1) * SLOT]
            scores = [lax.dot_general(qh, k_ref[:, hd * SLOT:(hd + 1) * SLOT], nt,
                                      preferred_element_type=F32) for k_ref, _ in segments]
            mx = functools.reduce(jnp.maximum, [jnp.max(s, -1, keepdims=True) for s in scores])
            ps = [jnp.exp(s - mx) for s in scores]
            den = functools.reduce(jnp.add, [jnp.sum(p, -1, keepdims=True) for p in ps])
            acc = functools.reduce(jnp.add, [
                jnp.dot(p.astype(BF16), v_ref[:, pair * 2 * VD:(pair + 1) * 2 * VD],
                        preferred_element_type=F32) for p, (_, v_ref) in zip(ps, segments)])
            outs.append(acc / den)
        o_ref[:, pair * 2 * VD:(pair + 1) * 2 * VD] = jnp.where(lane < VD, outs[0], outs[1]).astype(BF16)


def _attn_ctx_kernel(q_ref, k_ref, v_ref, o_ref):
    _attend(q_ref, [(k_ref, v_ref)], o_ref)


def _attn_lat_kernel(q_ref, kc_ref, vc_ref, k_ref, v_ref, o_ref):
    _attend(q_ref, [(kc_ref.at[0], vc_ref.at[0]), (k_ref, v_ref)], o_ref)


def _attention(q, k, v, kc, vc):
    kw, vw = HEADS * SLOT, HEADS * VD
    ctx = pl.pallas_call(
        _attn_ctx_kernel,
        out_shape=jax.ShapeDtypeStruct((T_CTX, vw), BF16),
        grid=(N_CTX_B,),
        in_specs=[pl.BlockSpec((CTX_L, kw), lambda b: (b, 0)),
                  pl.BlockSpec((CTX_L, kw), lambda b: (b, 0)),
                  pl.BlockSpec((CTX_L, vw), lambda b: (b, 0))],
        out_specs=pl.BlockSpec((CTX_L, vw), lambda b: (b, 0)),
        compiler_params=_cparams("parallel"),
        name="attn_ctx",
    )(q, k, v)
    nq = LAT_L // TM
    off = T_CTX // LAT_L
    lat = pl.pallas_call(
        _attn_lat_kernel,
        out_shape=jax.ShapeDtypeStruct((T_LAT, vw), BF16),
        grid=(N_LAT_B, nq),
        in_specs=[pl.BlockSpec((TM, kw), lambda b, j: (CTX_TILES + b * nq + j, 0)),
                  pl.BlockSpec((1, PAST, kw), lambda b, j: (b, 0, 0)),
                  pl.BlockSpec((1, PAST, vw), lambda b, j: (b, 0, 0)),
                  pl.BlockSpec((LAT_L, kw), lambda b, j: (off + b, 0)),
                  pl.BlockSpec((LAT_L, vw), lambda b, j: (off + b, 0))],
        out_specs=pl.BlockSpec((TM, vw), lambda b, j: (b * nq + j, 0)),
        compiler_params=_cparams("parallel", "parallel"),
        name="attn_lat",
    )(q, kc, vc, k, v)
    return ctx, lat


def _hyena_kernel(hy_ref, cw_ref, cb_ref, c_ref, sf_ref, sb_ref, kr_ref, ki_ref, krn_ref, hb_ref,
                  o_ref, *, L):
    x = hy_ref[...]
    row = lax.broadcasted_iota(jnp.int32, x.shape, 0)
    prev = jnp.where(row == 0, 0.0, pltpu.roll(x, 1, 0))
    nxt = jnp.where(row == L - 1, 0.0, pltpu.roll(x, L - 1, 0))
    z = prev * cw_ref[0:1] + x * cw_ref[1:2] + nxt * cw_ref[2:3] + cb_ref[...]
    y = z[:, 2 * HY_W:]
    for o in range(2):
        gate = z[:, o * HY_W:(o + 1) * HY_W]
        yb = y.astype(BF16)
        a_re = jnp.dot(c_ref[...], yb, preferred_element_type=F32)
        a_im = jnp.dot(sf_ref[...], yb, preferred_element_type=F32)
        ki = ki_ref[o]
        z_re = a_re * kr_ref[o] + a_im * ki
        z_im = a_im * krn_ref[o] - a_re * ki
        conv = (jnp.dot(c_ref[...], z_re.astype(BF16), preferred_element_type=F32)
                + jnp.dot(sb_ref[...], z_im.astype(BF16), preferred_element_type=F32))
        y = gate * (conv + y * hb_ref[o:o + 1])
    o_ref[...] = y.astype(BF16)


def _hyena_group(hy, L, nb, blk0, dft, spectra, conv_w, conv_b, hy_bias):
    c, sf, sb, _ = dft
    kr, ki, krn = spectra
    return pl.pallas_call(
        functools.partial(_hyena_kernel, L=L),
        out_shape=jax.ShapeDtypeStruct((nb * L, HY_W), BF16),
        grid=(nb,),
        in_specs=[pl.BlockSpec((L, 3 * HY_W), lambda b: (blk0 + b, 0)),
                  _full((3, 3 * HY_W)), _full((1, 3 * HY_W)),
                  _full((L, L)), _full((L, L)), _full((L, L)),
                  _full((2, L, HY_W)), _full((2, L, HY_W)), _full((2, L, HY_W)),
                  _full((2, HY_W))],
        out_specs=pl.BlockSpec((L, HY_W), lambda b: (b, 0)),
        compiler_params=_cparams("parallel"),
        name=f"hyena_{L}",
    )(hy, conv_w, conv_b.reshape(1, 3 * HY_W), c, sf, sb, kr, ki, krn, hy_bias)


def _out_kernel(a_ref, bc_ref, bl_ref, mc_ref, ml_ref, w_ref, xc_ref, xl_ref, ng_ref, nb_ref, mod_ref,
                g_ref, be_ref, wr_ref, br_ref, tri_ref, upper_ref, x1_ref, xloc_ref, route_ref, cnt_ref,
                *, pre_norm):
    dot = functools.partial(jnp.dot, preferred_element_type=F32)
    mixed = (dot(a_ref[...], w_ref[0, 0:GM_W]) + dot(_pair_read(bc_ref, bl_ref), w_ref[0, GM_W:GM_W + HY_W])
             + dot(_pair_read(mc_ref, ml_ref), w_ref[0, GM_W + HY_W:]))
    x = _pair_read(xc_ref, xl_ref)
    if pre_norm:
        x = _layer_norm(x, ng_ref[...], nb_ref[...])
    m = mod_ref[0]
    x1 = _layer_norm(ALPHA * x + m[2:3] * mixed, g_ref[...], be_ref[...])
    x1_ref[...] = x1
    h2 = x1 * (1.0 + m[4:5]) + m[3:4]

    h_hi = h2.astype(BF16)
    h_lo = (h2 - h_hi.astype(F32)).astype(BF16)
    w_hi = wr_ref[...].astype(BF16)
    w_lo = (wr_ref[...] - w_hi.astype(F32)).astype(BF16)
    part = dot(h_hi, jnp.concatenate([w_hi, w_lo], axis=1))
    logits = part[:, :128] + part[:, 128:] + dot(h_lo, w_hi) + br_ref[...]
    lane = lax.broadcasted_iota(jnp.int32, logits.shape, 1)
    lanef = lane.astype(F32)
    big = jnp.float32(1e9)
    ninf = jnp.float32(-jnp.inf)
    is_g = lane < N_GROUPS
    gl = jnp.where(is_g, logits, ninf)
    gmax = jnp.max(gl, -1, keepdims=True)
    gidx = jnp.min(jnp.where(gl == gmax, lanef, big), -1, keepdims=True)
    gw = 1.0 / jnp.sum(jnp.where(is_g, jnp.exp(logits - gmax), 0.0), -1, keepdims=True)
    ex = lane - N_GROUPS
    in_group = (ex >= 0) & (ex < N_EXP) & ((ex // EPG).astype(F32) == gidx)
    el = jnp.where(in_group, logits, ninf)
    v1 = jnp.max(el, -1, keepdims=True)
    i1 = jnp.min(jnp.where(el == v1, lanef, big), -1, keepdims=True)
    el2 = jnp.where(lanef == i1, ninf, el)
    v2 = jnp.max(el2, -1, keepdims=True)
    i2 = jnp.min(jnp.where(el2 == v2, lanef, big), -1, keepdims=True)
    e21 = jnp.exp(v2 - v1)
    w1 = gw / (1.0 + e21)
    w2 = gw * e21 / (1.0 + e21)
    e1 = i1 - N_GROUPS
    e2 = i2 - N_GROUPS

    oh1 = jnp.where(lanef == e1, 1.0, 0.0)
    oh2 = jnp.where(lanef == e2, 1.0, 0.0)
    ex1 = dot(tri_ref[...], oh1.astype(BF16))
    ex2 = dot(tri_ref[...], oh2.astype(BF16))
    col1 = jnp.sum(oh1, axis=0, keepdims=True)
    col2 = jnp.sum(oh2, axis=0, keepdims=True)
    n = col1 + col2
    run = jnp.floor((n + (BLK - 1.0)) * (1.0 / BLK)) * BLK
    start = dot(jnp.broadcast_to(run, (8, 128)).astype(BF16), upper_ref[...])[0:1]
    loc1 = jnp.sum(oh1 * (start + ex1), -1, keepdims=True)
    loc2 = jnp.sum(oh2 * (start + col1 + ex2), -1, keepdims=True)
    cnt_ref[...] = jnp.broadcast_to(n, cnt_ref.shape)

    route = jnp.zeros_like(logits)
    for j, val in enumerate((e1, e2, w1, w2, loc1, loc2)):
        route = jnp.where(lane == j, val, route)
    route_ref[...] = route

    loc1_row = jnp.transpose(jnp.broadcast_to(loc1, (TM, 128)))[0:1].astype(jnp.int32)
    loc2_row = jnp.transpose(jnp.broadcast_to(loc2, (TM, 128)))[0:1].astype(jnp.int32)
    slot = lax.broadcasted_iota(jnp.int32, (LOC, TM), 0)
    perm = jnp.where((slot == loc1_row) | (slot == loc2_row), 1.0, 0.0).astype(BF16)
    xloc_ref[...] = dot(perm, h_hi).astype(BF16)


def _out_proj(l, a, hyb_pair, att_pair, w_out, x_pair, norm, mod_l, g, b, w_route, b_route, tri, upper):
    tile = lambda n: pl.BlockSpec((TM, n), lambda i: (i, 0))
    return pl.pallas_call(
        functools.partial(_out_kernel, pre_norm=(l == 0)),
        out_shape=(jax.ShapeDtypeStruct((T, D), F32), jax.ShapeDtypeStruct((N_TILES * LOC, D), BF16),
                   jax.ShapeDtypeStruct((T, 128), F32), jax.ShapeDtypeStruct((N_TILES * 8, 128), F32)),
        grid=(N_TILES,),
        in_specs=[tile(GM_W)] + _pair_specs(HY_W) + _pair_specs(HEADS * VD)
        + [pl.BlockSpec((1, D, D), lambda i: (l, 0, 0))]
        + _pair_specs(D) + [
                  _full((1, D)), _full((1, D)),
                  pl.BlockSpec((1, 6, D), lambda i: (_cond_of_tile(i), 0, 0)),
                  _full((1, D)), _full((1, D)), _full((D, 128)), _full((1, 128)), _full((TM, TM)),
                  _full((128, 128))],
        out_specs=(tile(D), pl.BlockSpec((LOC, D), lambda i: (i, 0)), tile(128),
                   pl.BlockSpec((8, 128), lambda i: (i, 0))),
        compiler_params=_cparams("parallel"),
        name="out_proj_route",
    )(a, *hyb_pair, *att_pair, w_out, *x_pair, *norm, mod_l, g, b, w_route, b_route, tri, upper)


X_RING = 4


def _expert_kernel(off_ref, cnt_ref, src_ref, x_hbm, wg_ref, wu_ref, wd_ref, y_hbm,
                   xbuf, ybuf, wgu_b, wd_b, sem, osem):
    e = pl.program_id(0)
    first_tile = off_ref[e]
    n_tiles = cnt_ref[e]
    total = off_ref[N_EXP - 1] + cnt_ref[N_EXP - 1]

    def block_copies(t, s):
        return [pltpu.make_async_copy(
            x_hbm.at[pl.ds(pl.multiple_of(src_ref[t * BLK_PER_TILE + b] * BLK, BLK), BLK), :],
            xbuf.at[s, pl.ds(b * BLK, BLK), :], sem.at[s]) for b in range(BLK_PER_TILE)]

    def out_copy(t, s):
        return pltpu.make_async_copy(ybuf.at[s], y_hbm.at[pl.ds(pl.multiple_of(t * TM, TM), TM), :],
                                     osem.at[s])

    @pl.when(e == 0)
    def _():
        for t0 in range(X_RING - 1):
            @pl.when(t0 < total)
            def _():
                for cp in block_copies(t0, t0):
                    cp.start()

    @pl.when(n_tiles > 0)
    def _():
        wgu_b[:, 0:D_EXP] = wg_ref[0, 0].astype(BF16)
        wgu_b[:, D_EXP:2 * D_EXP] = wu_ref[0, 0].astype(BF16)
        wd_b[...] = wd_ref[0, 0].astype(BF16)

    def tile_body(t, carry):
        s = t % 2
        xs = t % X_RING

        @pl.when(t + X_RING - 1 < total)
        def _():
            for cp in block_copies(t + X_RING - 1, (t + X_RING - 1) % X_RING):
                cp.start()

        for cp in block_copies(t, xs):
            cp.wait()

        @pl.when(t >= 2)
        def _():
            out_copy(t - 2, s).wait()

        gu = jnp.dot(xbuf[xs], wgu_b[...], preferred_element_type=F32)
        gate, up = gu[:, :D_EXP], gu[:, D_EXP:]
        hid = gate / (1.0 + jnp.exp(-gate)) * up
        ybuf[s] = jnp.dot(hid.astype(BF16), wd_b[...], preferred_element_type=F32).astype(BF16)
        out_copy(t, s).start()
        return carry

    lax.fori_loop(first_tile, first_tile + n_tiles, tile_body, 0)

    @pl.when(e == N_EXP - 1)
    def _():
        @pl.when(total >= 2)
        def _():
            out_copy(total - 2, total % 2).wait()

        out_copy(total - 1, (total - 1) % 2).wait()
        ybuf[0] = jnp.zeros((TM, D), BF16)

        def zero_start(t, carry):
            out_copy(t, 0).start()
            return carry

        def zero_wait(t, carry):
            out_copy(t, 0).wait()
            return carry

        lax.fori_loop(total, MOE_TILES, zero_start, 0)
        lax.fori_loop(total, MOE_TILES, zero_wait, 0)


def _experts(l, tables, xloc, w_gate, w_up, w_down):
    grid_spec = pltpu.PrefetchScalarGridSpec(
        num_scalar_prefetch=len(tables),
        grid=(N_EXP,),
        in_specs=[pl.BlockSpec(memory_space=pl.ANY),
                  pl.BlockSpec((1, 1, D, D_EXP), lambda e, *_: (l, e, 0, 0)),
                  pl.BlockSpec((1, 1, D, D_EXP), lambda e, *_: (l, e, 0, 0)),
                  pl.BlockSpec((1, 1, D_EXP, D), lambda e, *_: (l, e, 0, 0))],
        out_specs=pl.BlockSpec(memory_space=pl.ANY),
        scratch_shapes=[pltpu.VMEM((X_RING, TM, D), BF16), pltpu.VMEM((2, TM, D), BF16),
                        pltpu.VMEM((D, 2 * D_EXP), BF16), pltpu.VMEM((D_EXP, D), BF16),
                        pltpu.SemaphoreType.DMA((X_RING,)), pltpu.SemaphoreType.DMA((2,))],
    )
    return pl.pallas_call(
        _expert_kernel,
        out_shape=jax.ShapeDtypeStruct((MOE_ROWS, D), BF16),
        grid_spec=grid_spec,
        compiler_params=_cparams("arbitrary"),
        name="experts",
    )(*tables, xloc, w_gate, w_up, w_down)


def _combine_kernel(nblk_ref, gsrc_ref, y_hbm, route_ref, x1_ref, mod_ref, g_ref, b_ref, oc_ref, ol_ref,
                    ybuf, sem):
    i = pl.program_id(0)

    slot_i = i % 2

    def block_copy(t, s, lb):
        src = pl.multiple_of(gsrc_ref[t * LOC_BLKS + lb] * BLK, BLK)
        return pltpu.make_async_copy(y_hbm.at[pl.ds(src, BLK), :],
                                     ybuf.at[s, pl.ds(pl.multiple_of(lb * BLK, BLK), BLK), :], sem.at[s])

    def fetch(t, s):
        def start(lb, c):
            block_copy(t, s, lb).start()
            return c
        lax.fori_loop(0, nblk_ref[t], start, 0)

    @pl.when(i == 0)
    def _():
        ybuf[...] = jnp.zeros_like(ybuf)
        fetch(0, 0)

    @pl.when(i + 1 < N_TILES)
    def _():
        fetch(jnp.minimum(i + 1, N_TILES - 1), 1 - slot_i)

    def wait(lb, c):
        block_copy(i, slot_i, lb).wait()
        return c

    lax.fori_loop(0, nblk_ref[i], wait, 0)
    route = route_ref[...]
    yb = ybuf[slot_i]
    slot = lax.broadcasted_iota(jnp.int32, (TM, LOC), 1)
    pick = (jnp.where(slot == route[:, 4:5].astype(jnp.int32), route[:, 2:3], 0.0)
            + jnp.where(slot == route[:, 5:6].astype(jnp.int32), route[:, 3:4], 0.0)).astype(BF16)
    moe = jnp.dot(pick, yb, preferred_element_type=F32)
    m = mod_ref[0]
    res = _layer_norm(ALPHA * x1_ref[...] + m[5:6] * moe, g_ref[...], b_ref[...])

    @pl.when(i < CTX_TILES)
    def _():
        oc_ref[...] = res

    @pl.when(i >= CTX_TILES)
    def _():
        ol_ref[...] = res


def _combine(nblk, gsrc, y, route, x1, mod_l, g, b):
    out_shape = (jax.ShapeDtypeStruct((T_CTX, D), F32), jax.ShapeDtypeStruct((T_LAT, D), F32))
    out_specs = tuple(_pair_specs(D))
    grid_spec = pltpu.PrefetchScalarGridSpec(
        num_scalar_prefetch=2,
        grid=(N_TILES,),
        in_specs=[pl.BlockSpec(memory_space=pl.ANY),
                  pl.BlockSpec((TM, 128), lambda i, nb, gs: (i, 0)),
                  pl.BlockSpec((TM, D), lambda i, nb, gs: (i, 0)),
                  pl.BlockSpec((1, 6, D), lambda i, nb, gs: (_cond_of_tile(i), 0, 0)),
                  pl.BlockSpec((1, D), lambda i, nb, gs: (0, 0)),
                  pl.BlockSpec((1, D), lambda i, nb, gs: (0, 0))],
        out_specs=out_specs,
        scratch_shapes=[pltpu.VMEM((2, LOC, D), BF16), pltpu.SemaphoreType.DMA((2,))],
    )
    return pl.pallas_call(
        _combine_kernel,
        out_shape=out_shape,
        grid_spec=grid_spec,
        compiler_params=_cparams("arbitrary"),
        name="moe_combine",
    )(nblk, gsrc, y, route, x1, mod_l, g, b)


def _prefix_pick(starts, query, table):
    delta = table - jnp.concatenate([jnp.zeros_like(table[..., :1]), table[..., :-1]], axis=-1)
    return jnp.sum(jnp.where(starts <= query, delta, 0), axis=-1)


def _dispatch_tables(cnt):
    i32 = jnp.int32
    run = (cnt + BLK - 1) // BLK * BLK
    loc_start = jnp.cumsum(run, axis=1) - run
    nblk_loc = jnp.sum(run, axis=1) // BLK
    seg_rows = jnp.sum(run, axis=0)
    seg_tiles = (seg_rows + TM - 1) // TM
    tile_end = jnp.cumsum(seg_tiles)
    seg_start = (tile_end - seg_tiles) * TM
    glob_start = seg_start[None, :] + jnp.cumsum(run, axis=0) - run

    g_blk = (glob_start.T.reshape(1, -1)) // BLK
    n_blk = (run.T.reshape(1, -1)) // BLK
    l_blk = ((jnp.arange(N_TILES, dtype=i32) * LOC)[None, :] + loc_start.T).reshape(1, -1) // BLK
    gb = jnp.arange(MOE_ROWS // BLK, dtype=i32)[:, None]
    off = gb[:, 0] - _prefix_pick(g_blk, gb, g_blk)
    src_blk = jnp.where(off < _prefix_pick(g_blk, gb, n_blk), _prefix_pick(g_blk, gb, l_blk) + off,
                        ZERO_BLK).astype(i32)

    pos = (jnp.arange(LOC_BLKS, dtype=i32) * BLK)[None, :, None]
    shift = _prefix_pick(loc_start[:, None, :], pos, (glob_start - loc_start)[:, None, :])
    gsrc = ((pos[:, :, 0] + shift) // BLK).astype(i32)
    expert_tables = ((tile_end - seg_tiles).astype(i32), seg_tiles.astype(i32), src_blk)
    return expert_tables, nblk_loc.astype(i32), gsrc.reshape(-1)


def _rope_slot(pe):
    return jnp.pad(pe, [(0, 0)] * (pe.ndim - 1) + [(NOPE, SLOT - NOPE - ROPE)])


def _w_in_layout(w):
    return jnp.concatenate([w[..., :1664], _rope_slot(w[..., 1664:])], axis=-1).astype(BF16)


def _wuq_layout(w):
    w = w.reshape(DEPTH, Q_LORA, HEADS, NOPE + ROPE)
    w = jnp.pad(w, ((0, 0), (0, 0), (0, 0), (0, SLOT - NOPE - ROPE)))
    return w.reshape(DEPTH, Q_LORA, HEADS * SLOT).astype(BF16)


def _wukv_layout(w):
    w = w.reshape(DEPTH, KV_LORA, HEADS, NOPE + VD)
    wk = jnp.pad(w[..., :NOPE], ((0, 0), (0, 0), (0, 0), (0, SLOT - NOPE)))
    return (wk.reshape(DEPTH, KV_LORA, HEADS * SLOT).astype(BF16),
            w[..., NOPE:].reshape(DEPTH, KV_LORA, HEADS * VD).astype(BF16))


def kernel(x_prompt, x_sample, c, cache_ckv, cache_kpe, c_ctx, ln_in_g, ln_in_b, w_mod, b_mod, w_in,
           gm_ln_g, gm_ln_b, gm_ws, gm_bs, hy_conv_w, hy_conv_b, hy_f_w1, hy_f_b1, hy_f_w2, hy_f_b2,
           hy_f_w3, hy_f_freq, hy_bias, mla_gq, mla_gkv, mla_wuq, mla_wukv, w_out, ln1_g, ln1_b,
           ln2_g, ln2_b, moe_w_gr, moe_b_gr, moe_w_er, moe_b_er, moe_w_gate, moe_w_up, moe_w_down):
    rope = _rope_tables()
    dft = {L: _dft_tables(L) for L in (CTX_L, LAT_L)}
    pos_tab = {L: _hyena_positions(L) for L in (CTX_L, LAT_L)}
    hd = np.arange(GM_W) // GM_HD
    avg = jnp.asarray((hd[:, None] == hd[None, :]) / GM_HD, BF16)
    tri = jnp.asarray(np.tril(np.ones((TM, TM)), -1), BF16)
    upper = jnp.asarray(np.triu(np.ones((128, 128)), 1), BF16)

    cond8 = jnp.concatenate([c_ctx[None], c, jnp.zeros((8 - 1 - N_LAT_B, D), F32)], axis=0)
    mod = _modulation(cond8, w_mod, b_mod).reshape(DEPTH, 8, 6, D)

    wuk_all, wuv_all = _wukv_layout(mla_wukv)
    w_in_ext = _w_in_layout(w_in)
    wuq_arr = _wuq_layout(mla_wuq)
    w_out_b = w_out.astype(BF16)
    ws_b = gm_ws.astype(BF16)
    kc_all, vc_all = _cache_kv(cache_ckv, _rope_slot(cache_kpe), wuk_all, wuv_all)

    x_pair = (x_prompt.reshape(T_CTX, D), x_sample.reshape(T_LAT, D))
    norm = (ln_in_g.reshape(1, D), ln_in_b.reshape(1, D))
    ckv_states, kpe_states = [], []
    for l in range(DEPTH):
        bs_full = jnp.repeat(gm_bs[l].T, GM_HD, axis=1)
        a, hy, q, k, v, ckv, kpe = _in_proj(
            l, x_pair, norm, mod[l], w_in_ext, gm_ln_g[l].reshape(1, GM_W), gm_ln_b[l].reshape(1, GM_W), avg,
            ws_b, bs_full, mla_gq[l].reshape(1, Q_LORA), mla_gkv[l].reshape(1, KV_LORA),
            wuq_arr, wuk_all, wuv_all, rope)
        ckv_states.append(ckv[:T_CTX].reshape(N_CTX_B, CTX_L, KV_LORA))
        kpe_states.append(kpe[:T_CTX, NOPE:NOPE + ROPE].reshape(N_CTX_B, CTX_L, ROPE))

        hyb = []
        for L, nb, blk0 in ((CTX_L, N_CTX_B, 0), (LAT_L, N_LAT_B, T_CTX // LAT_L)):
            z, dec = pos_tab[L]
            spectra = _filter_spectra(L, z, dec, dft[L], hy_f_w1[l], hy_f_b1[l], hy_f_w2[l],
                                      hy_f_b2[l], hy_f_w3[l], hy_f_freq[l])
            hyb.append(_hyena_group(hy, L, nb, blk0, dft[L], spectra, hy_conv_w[l], hy_conv_b[l],
                                    hy_bias[l]))

        att = _attention(q, k, v, kc_all[l], vc_all[l])

        w_route = jnp.pad(jnp.concatenate([moe_w_gr[l], moe_w_er[l].reshape(D, N_EXP)], axis=1),
                          ((0, 0), (0, 128 - N_GROUPS - N_EXP)))
        b_route =jnp.pad(jnp.concatenate([moe_b_gr[l], moe_b_er[l].reshape(N_EXP)]),
                          (0, 128 - N_GROUPS - N_EXP)).reshape(1, 128)
        x1, xloc, route, counts = _out_proj(l, a, hyb, att, w_out_b, x_pair, norm, mod[l],
                                            ln1_g[l].reshape(1, D), ln1_b[l].reshape(1, D),
                                            w_route, b_route, tri, upper)
        cnt = counts.reshape(N_TILES, 8, 128)[:, 0, :N_EXP].astype(jnp.int32)
        expert_tables, nblk_loc, gsrc = _dispatch_tables(cnt)
        y = _experts(l, expert_tables, xloc, moe_w_gate, moe_w_up, moe_w_down)
        x_pair = _combine(nblk_loc, gsrc, y, route, x1, mod[l], ln2_g[l].reshape(1, D), ln2_b[l].reshape(1, D))

    y_prompt = x_pair[0].reshape(N_CTX_B, CTX_L, D)
    y_sample = x_pair[1].reshape(N_LAT_B, LAT_L, D)
    return (y_prompt, y_sample, jnp.stack(ckv_states, axis=1), jnp.stack(kpe_states, axis=1))
```

```python
import functools
import math

import numpy as np
import jax
import jax.numpy as jnp
from jax import lax
from jax.experimental import pallas as pl
from jax.experimental.pallas import tpu as pltpu

D = 1024
N_CTX_B, CTX_L = 16, 256
N_LAT_B, LAT_L = 4, 1024
DEPTH = 2
T_CTX = N_CTX_B * CTX_L
T_LAT = N_LAT_B * LAT_L
T = T_CTX + T_LAT
PAST = 256
GRID_W = 64

GM_HEADS, GM_HD, GM_W, CHUNK = 4, 64, 256, 128
HY_W, HY_EMB, HY_FH = 256, 33, 64
NOPE, ROPE, VD, HEADS = 64, 32, 64, 8
Q_LORA, KV_LORA = 256, 128
SLOT = 128
N_GROUPS, EPG, N_EXP, D_EXP = 4, 8, 32, 256
ALPHA = (2.0 * DEPTH) ** 0.25
LN_EPS, RMS_EPS = 1e-5, 1e-6
ROPE_THETA = 10000.0

TM = 256
N_TILES = T // TM
CTX_TILES = T_CTX // TM
LAT_TILES_PER_B = LAT_L // TM
N_ASSIGN = 2 * T
BLK = 16
BLK_PER_TILE = TM // BLK
LOC = 2 * TM + N_EXP * BLK
LOC_BLKS = LOC // BLK
ZERO_BLK = LOC_BLKS - 1
MOE_TILES = (N_ASSIGN + N_TILES * N_EXP * (BLK - 1)) // TM + N_EXP
MOE_ROWS = MOE_TILES * TM
IN_EXT = 1792

F32, BF16 = jnp.float32, jnp.bfloat16
VMEM_LIMIT = 52 * 1024 * 1024


def _cparams(*sem):
    return pltpu.CompilerParams(dimension_semantics=sem, vmem_limit_bytes=VMEM_LIMIT)


def _cond_of_tile(i):
    return jnp.where(i < CTX_TILES, 0, 1 + (i - CTX_TILES) // LAT_TILES_PER_B)


def _rope_block_of_tile(i):
    return jnp.where(i < CTX_TILES, 0, 1 + (i - CTX_TILES) % LAT_TILES_PER_B)


def _full(shape):
    n = len(shape)
    return pl.BlockSpec(shape, lambda *_: (0,) * n)


def _pair_specs(n):
    return [pl.BlockSpec((TM, n), lambda i, *_: (jnp.minimum(i, CTX_TILES - 1), 0)),
            pl.BlockSpec((TM, n), lambda i, *_: (jnp.maximum(i - CTX_TILES, 0), 0))]


def _pair_read(c_ref, l_ref):
    return jnp.where(pl.program_id(0) < CTX_TILES, c_ref[...], l_ref[...])


def _layer_norm(x, g, b):
    mu = jnp.mean(x, -1, keepdims=True)
    xc = x - mu
    var = jnp.mean(xc * xc, -1, keepdims=True)
    return xc * lax.rsqrt(var + LN_EPS) * g + b


def _rms_norm(x, g):
    return x * lax.rsqrt(jnp.mean(x * x, -1, keepdims=True) + RMS_EPS) * g


def _bdot(a, b):
    return jnp.dot(a.astype(BF16), b.astype(BF16), preferred_element_type=F32)


def _dot3(x, w):
    x_hi = x.astype(BF16)
    x_lo = (x - x_hi.astype(F32)).astype(BF16)
    w_hi = w.astype(BF16)
    w_lo = (w - w_hi.astype(F32)).astype(BF16)
    dot = functools.partial(jnp.dot, preferred_element_type=F32)
    return dot(x_hi, w_hi) + dot(x_hi, w_lo) + dot(x_lo, w_hi)


def _swap_pair_lanes(x):
    n = x.shape[1]
    lane = lax.broadcasted_iota(jnp.int32, x.shape, 1)
    return jnp.where(lane % 2 == 0, pltpu.roll(x, n - 1, 1), pltpu.roll(x, 1, 1))


def _split2_dot(x, w_bf16):
    hi = x.astype(BF16)
    lo = (x - hi.astype(F32)).astype(BF16)
    dot = functools.partial(jnp.dot, preferred_element_type=F32)
    return dot(hi, w_bf16) + dot(lo, w_bf16)


def _dft_tables(L):
    f = np.arange(L, dtype=np.int64)
    ft = np.outer(f, f) % (2 * L)
    ang = np.pi * ft / L
    c = np.cos(ang)
    s = np.sin(ang)
    alt = np.where(f % 2 == 0, 1.0, -1.0)
    sf = s.copy()
    sf[0, :] = alt
    return (jnp.asarray(c, F32).astype(BF16), jnp.asarray(sf, F32).astype(BF16),
            jnp.asarray(sf.T.copy(), F32).astype(BF16), jnp.asarray(alt[:, None], F32))


def _rope_tables():
    rows = LAT_L // GRID_W
    row = np.repeat(np.arange(rows, dtype=np.float64), GRID_W)
    col = np.tile(np.arange(GRID_W, dtype=np.float64), rows)
    n = ROPE // 4
    inv = ROPE_THETA ** (-np.arange(n, dtype=np.float64) / n)
    ang = np.concatenate([row[:, None] * inv, col[:, None] * inv], -1)
    cos = np.zeros((CTX_L + LAT_L, SLOT))
    sin = np.zeros((CTX_L + LAT_L, SLOT))
    cos[:, :NOPE + ROPE] = 1.0
    cos[CTX_L:, NOPE:NOPE + ROPE:2] = np.cos(ang)
    cos[CTX_L:, NOPE + 1:NOPE + ROPE:2] = np.cos(ang)
    sin[CTX_L:, NOPE:NOPE + ROPE:2] = -np.sin(ang)
    sin[CTX_L:, NOPE + 1:NOPE + ROPE:2] = np.sin(ang)
    scale = 1.0 / math.sqrt(NOPE + ROPE)
    return (jnp.asarray(cos * scale, F32), jnp.asarray(sin * scale, F32),
            jnp.asarray(cos, F32), jnp.asarray(sin, F32))


def _hyena_positions(L):
    t = jnp.linspace(0.0, 1.0, L, dtype=F32)[:, None]
    bands = (HY_EMB - 1) // 2
    w = 2.0 * math.pi * jnp.arange(L, dtype=F32)[:, None] / L
    f = jnp.linspace(1e-4, bands - 1, bands, dtype=F32)[None]
    z = jnp.concatenate([t, jnp.cos(f * w), -jnp.sin(f * w)], -1)
    z = jnp.pad(z, ((0, 0), (0, 128 - HY_EMB)))
    min_decay = math.log(1e-2) / 1.5
    max_decay = math.log(1e-2) / 0.3
    deltas = jnp.abs(jnp.linspace(min_decay, max_decay, HY_W, dtype=F32))
    return z, jnp.exp(-t * deltas)


def _mod_kernel(c_ref, w_ref, b_ref, o_ref):
    c = c_ref[...]
    s = c / (1.0 + jnp.exp(-c))
    o_ref[0] = _bdot(s, w_ref[0]) + b_ref[0]


def _modulation(cond8, w_mod, b_mod):
    tn = 1536
    return pl.pallas_call(
        _mod_kernel,
        out_shape=jax.ShapeDtypeStruct((DEPTH, 8, 6 * D), F32),
        grid=(DEPTH, 6 * D // tn),
        in_specs=[pl.BlockSpec((8, D), lambda l, j: (0, 0)),
                  pl.BlockSpec((1, D, tn), lambda l, j: (l, 0, j)),
                  pl.BlockSpec((1, 1, tn), lambda l, j: (l, 0, j))],
        out_specs=pl.BlockSpec((1, 8, tn), lambda l, j: (l, 0, j)),
        compiler_params=_cparams("parallel", "parallel"),
        name="modulation",
    )(cond8, w_mod, b_mod.reshape(DEPTH, 1, 6 * D))


def _filter_kernel(z_ref, w1_ref, b1_ref, w2_ref, b2_ref, w3_ref, fr_ref, dec_ref, c_ref, sf_ref,
                   alt_ref, kr_ref, ki_ref, krn_ref, *, L):
    h1 = jnp.sin(fr_ref[0:1] * (_dot3(z_ref[...], w1_ref[...]) + b1_ref[...]))
    h2 = jnp.sin(fr_ref[1:2] * (_dot3(h1, w2_ref[...]) + b2_ref[...]))
    filt = _dot3(h2, w3_ref[...])
    row = lax.broadcasted_iota(jnp.int32, (L, HY_W), 0)
    dec = dec_ref[...]
    for o in range(2):
        fwd = filt[:, o * 2 * HY_W:o * 2 * HY_W + HY_W] * dec
        bwd = jnp.where(row == 0, 0.0, filt[:, o * 2 * HY_W + HY_W:(o + 1) * 2 * HY_W] * dec)
        sm = fwd + bwd
        df = fwd - bwd
        kr = _bdot(c_ref[...], sm)
        ki = -_bdot(sf_ref[...], df)
        kr_nyq = jnp.sum(alt_ref[...] * sm, axis=0, keepdims=True)
        kr_ref[o] = jnp.where(row == 0, kr * (0.5 / L), kr * (1.0 / L))
        ki_ref[o] = jnp.where(row == 0, 0.0, ki * (1.0 / L))
        krn_ref[o] = jnp.where(row == 0, kr_nyq * (0.5 / L), kr * (1.0 / L))


def _filter_spectra(L, z, dec, dft, w1, b1, w2, b2, w3, freq):
    c, sf, _, alt = dft
    pad = 128 - HY_FH
    w1p = jnp.pad(w1, ((0, 128 - HY_EMB), (0, pad)))
    w2p = jnp.pad(w2, ((0, pad), (0, pad)))
    w3p = jnp.pad(w3, ((0, pad), (0, 0)))
    b1p = jnp.pad(b1, (0, pad)).reshape(1, 128)
    b2p = jnp.pad(b2, (0, pad)).reshape(1, 128)
    frp = jnp.pad(freq, ((0, 0), (0, pad)))
    shp = jax.ShapeDtypeStruct((2, L, HY_W), F32)
    args = (z, w1p, b1p, w2p, b2p, w3p, frp, dec, c, sf, alt)
    return pl.pallas_call(
        functools.partial(_filter_kernel, L=L),
        out_shape=(shp, shp, shp),
        grid=(1,),
        in_specs=[_full(a.shape) for a in args],
        out_specs=(_full((2, L, HY_W)),) * 3,
        compiler_params=_cparams("arbitrary"),
        name=f"hyena_filter_{L}",
    )(*args)


def _in_kernel(xc_ref, xl_ref, ng_ref, nb_ref, mod_ref, w_ref, gg_ref, gb_ref, avg_ref, ws_ref, bs_ref,
               gq_ref, gkv_ref, wuq_ref, wuk_ref, wuv_ref, cq_ref, sq_ref, ck_ref, sk_ref,
               a_ref, hy_ref, q_ref, k_ref, v_ref, ckv_ref, kpe_ref, *, pre_norm):
    x = _pair_read(xc_ref, xl_ref)
    if pre_norm:
        x = _layer_norm(x, ng_ref[...], nb_ref[...])
    m = mod_ref[0]
    h = x * (1.0 + m[1:2]) + m[0:1]
    proj = _bdot(h, w_ref[0])

    gu = jax.nn.gelu(proj[:, 0:GM_W], approximate=True)
    gv = jax.nn.gelu(proj[:, GM_W:2 * GM_W], approximate=True)
    avg = avg_ref[...]
    mu = _split2_dot(gv, avg)
    vc = gv - mu
    var = _split2_dot(vc * vc, avg)
    vln = (vc * lax.rsqrt(var + LN_EPS) * gg_ref[...] + gb_ref[...]).astype(BF16)
    lane = lax.broadcasted_iota(jnp.int32, (CHUNK, GM_W), 1)
    for c in range(TM // CHUNK):
        vchunk = vln[c * CHUNK:(c + 1) * CHUNK]
        s = bs_ref[...]
        for hd in range(GM_HEADS):
            sh = jnp.dot(ws_ref[0, hd], vchunk, preferred_element_type=F32)
            s = s + jnp.where(lane // GM_HD == hd, sh, 0.0)
        a_ref[c * CHUNK:(c + 1) * CHUNK, :] = (gu[c * CHUNK:(c + 1) * CHUNK] * s).astype(BF16)

    hy_ref[...] = proj[:, 512:1280]

    cq = _rms_norm(proj[:, 1280:1536], gq_ref[...])
    qq = _bdot(cq, wuq_ref[0])
    cos_q = jnp.concatenate([cq_ref[...]] * HEADS, axis=1)
    sin_q = jnp.concatenate([sq_ref[...]] * HEADS, axis=1)
    q_ref[...] = (qq * cos_q + _swap_pair_lanes(qq) * sin_q).astype(BF16)

    ckv = _rms_norm(proj[:, 1536:1664], gkv_ref[...])
    ckv_ref[...] = ckv
    kpe = proj[:, 1664:1792]
    kpe_ref[...] = kpe
    krot = kpe * ck_ref[...] + _swap_pair_lanes(kpe) * sk_ref[...]
    kn = _bdot(ckv, wuk_ref[0])
    k_ref[...] = (kn + jnp.concatenate([krot] * HEADS, axis=1)).astype(BF16)
    v_ref[...] = _bdot(ckv, wuv_ref[0]).astype(BF16)


def _in_proj(l, x_pair, norm, mod_l, w_in_ext, gm_g, gm_b, avg, ws, bs_full, gq, gkv, wuq_arr,
             wuk_arr, wuv_arr, rope):
    layer = lambda *shape: pl.BlockSpec((1,) + shape, lambda i: (l,) + (0,) * len(shape))
    cos_q, sin_q, cos_k, sin_k = rope
    tile = lambda n: pl.BlockSpec((TM, n), lambda i: (i, 0))
    rope_spec = pl.BlockSpec((TM, SLOT), lambda i: (_rope_block_of_tile(i), 0))
    out_shapes = (jax.ShapeDtypeStruct((T, GM_W), BF16),
                  jax.ShapeDtypeStruct((T, 3 * HY_W), F32),
                  jax.ShapeDtypeStruct((T, HEADS * SLOT), BF16),
                  jax.ShapeDtypeStruct((T, HEADS * SLOT), BF16),
                  jax.ShapeDtypeStruct((T, HEADS * VD), BF16),
                  jax.ShapeDtypeStruct((T, KV_LORA), F32),
                  jax.ShapeDtypeStruct((T, SLOT), F32))
    return pl.pallas_call(
        functools.partial(_in_kernel, pre_norm=(l == 0)),
        out_shape=out_shapes,
        grid=(N_TILES,),
        in_specs=_pair_specs(D) + [
                  _full((1, D)), _full((1, D)),
                  pl.BlockSpec((1, 6, D), lambda i: (_cond_of_tile(i), 0, 0)),
                  layer(D, IN_EXT), _full((1, GM_W)), _full((1, GM_W)), _full((GM_W, GM_W)),
                  layer(GM_HEADS, CHUNK, CHUNK), _full((CHUNK, GM_W)),
                  _full((1, Q_LORA)), _full((1, KV_LORA)),
                  layer(Q_LORA, HEADS * SLOT), layer(KV_LORA, HEADS * SLOT),
                  layer(KV_LORA, HEADS * VD),
                  rope_spec, rope_spec, rope_spec, rope_spec],
        out_specs=(tile(GM_W), tile(3 * HY_W), tile(HEADS * SLOT), tile(HEADS * SLOT),
                   tile(HEADS * VD), tile(KV_LORA), tile(SLOT)),
        compiler_params=_cparams("parallel"),
        name="in_proj",
    )(*x_pair, *norm, mod_l, w_in_ext, gm_g, gm_b, avg, ws, bs_full, gq, gkv, wuq_arr, wuk_arr, wuv_arr,
      cos_q, sin_q, cos_k, sin_k)


def _cache_kernel(ckv_ref, kpe_ref, wuk_ref, wuv_ref, k_ref, v_ref):
    ckv = ckv_ref[0, 0]
    kn = _bdot(ckv, wuk_ref[0])
    k_ref[0, 0] = (kn + jnp.concatenate([kpe_ref[0, 0]] * HEADS, axis=1)).astype(BF16)
    v_ref[0, 0] = _bdot(ckv, wuv_ref[0]).astype(BF16)


def _cache_kv(cache_ckv, cache_kpe_slot, wuk_arr, wuv_arr):
    return pl.pallas_call(
        _cache_kernel,
        out_shape=(jax.ShapeDtypeStruct((DEPTH, N_LAT_B, PAST, HEADS * SLOT), BF16),
                   jax.ShapeDtypeStruct((DEPTH, N_LAT_B, PAST, HEADS * VD), BF16)),
        grid=(DEPTH, N_LAT_B),
        in_specs=[pl.BlockSpec((1, 1, PAST, KV_LORA), lambda l, b: (b, l, 0, 0)),
                  pl.BlockSpec((1, 1, PAST, SLOT), lambda l, b: (b, l, 0, 0)),
                  pl.BlockSpec((1, KV_LORA, HEADS * SLOT), lambda l, b: (l, 0, 0)),
                  pl.BlockSpec((1, KV_LORA, HEADS * VD), lambda l, b: (l, 0, 0))],
        out_specs=(pl.BlockSpec((1, 1, PAST, HEADS * SLOT), lambda l, b: (l, b, 0, 0)),
                   pl.BlockSpec((1, 1, PAST, HEADS * VD), lambda l, b: (l, b, 0, 0))),
        compiler_params=_cparams("parallel", "parallel"),
        name="cache_kv",
    )(cache_ckv, cache_kpe_slot, wuk_arr, wuv_arr)


def _attend(q_ref, segments, o_ref):
    lq = q_ref.shape[0]
    lane = lax.broadcasted_iota(jnp.int32, (lq, 2 * VD), 1)
    nt = (((1,), (1,)), ((), ()))
    for pair in range(HEADS // 2):
        outs = []
        for hd in (2 * pair, 2 * pair + 1):
            qh = q_ref[:, hd * SLOT:(hd + 1) * SLOT]
            scores = [lax.dot_general(qh, k_ref[:, hd * SLOT:(hd + 1) * SLOT], nt,
                                      preferred_element_type=F32) for k_ref, _ in segments]
            mx = functools.reduce(jnp.maximum, [jnp.max(s, -1, keepdims=True) for s in scores])
            ps = [jnp.exp(s - mx) for s in scores]
            den = functools.reduce(jnp.add, [jnp.sum(p, -1, keepdims=True) for p in ps])
            acc = functools.reduce(jnp.add, [
                jnp.dot(p.astype(BF16), v_ref[:, pair * 2 * VD:(pair + 1) * 2 * VD],
                        preferred_element_type=F32) for p, (_, v_ref) in zip(ps, segments)])
            outs.append(acc / den)
        o_ref[:, pair * 2 * VD:(pair + 1) * 2 * VD] = jnp.where(lane < VD, outs[0], outs[1]).astype(BF16)


def _attn_ctx_kernel(q_ref, k_ref, v_ref, o_ref):
    _attend(q_ref, [(k_ref, v_ref)], o_ref)


def _attn_lat_kernel(q_ref, kc_ref, vc_ref, k_ref, v_ref, o_ref):
    _attend(q_ref, [(kc_ref.at[0], vc_ref.at[0]), (k_ref, v_ref)], o_ref)


def _attention(q, k, v, kc, vc):
    kw, vw = HEADS * SLOT, HEADS * VD
    ctx = pl.pallas_call(
        _attn_ctx_kernel,
        out_shape=jax.ShapeDtypeStruct((T_CTX, vw), BF16),
        grid=(N_CTX_B,),
        in_specs=[pl.BlockSpec((CTX_L, kw), lambda b: (b, 0)),
                  pl.BlockSpec((CTX_L, kw), lambda b: (b, 0)),
                  pl.BlockSpec((CTX_L, vw), lambda b: (b, 0))],
        out_specs=pl.BlockSpec((CTX_L, vw), lambda b: (b, 0)),
        compiler_params=_cparams("parallel"),
        name="attn_ctx",
    )(q, k, v)
    nq = LAT_L // TM
    off = T_CTX // LAT_L
    lat = pl.pallas_call(
        _attn_lat_kernel,
        out_shape=jax.ShapeDtypeStruct((T_LAT, vw), BF16),
        grid=(N_LAT_B, nq),
        in_specs=[pl.BlockSpec((TM, kw), lambda b, j: (CTX_TILES + b * nq + j, 0)),
                  pl.BlockSpec((1, PAST, kw), lambda b, j: (b, 0, 0)),
                  pl.BlockSpec((1, PAST, vw), lambda b, j: (b, 0, 0)),
                  pl.BlockSpec((LAT_L, kw), lambda b, j: (off + b, 0)),
                  pl.BlockSpec((LAT_L, vw), lambda b, j: (off + b, 0))],
        out_specs=pl.BlockSpec((TM, vw), lambda b, j: (b * nq + j, 0)),
        compiler_params=_cparams("parallel", "parallel"),
        name="attn_lat",
    )(q, kc, vc, k, v)
    return ctx, lat


def _hyena_kernel(hy_ref, cw_ref, cb_ref, c_ref, sf_ref, sb_ref, kr_ref, ki_ref, krn_ref, hb_ref,
                  o_ref, *, L, group):
    row = lax.broadcasted_iota(jnp.int32, (L, 3 * HY_W), 0)
    zs = []
    for j in range(group):
        x = hy_ref[j * L:(j + 1) * L, :]
        prev = jnp.where(row == 0, 0.0, pltpu.roll(x, 1, 0))
        nxt = jnp.where(row == L - 1, 0.0, pltpu.roll(x, L - 1, 0))
        zs.append(prev * cw_ref[0:1] + x * cw_ref[1:2] + nxt * cw_ref[2:3] + cb_ref[...])
    side = lambda parts: jnp.concatenate(parts, axis=1)
    y = side([z[:, 2 * HY_W:] for z in zs])
    for o in range(2):
        gate = side([z[:, o * HY_W:(o + 1) * HY_W] for z in zs])
        kr, ki, krn = (side([r[o]] * group) for r in (kr_ref, ki_ref, krn_ref))
        yb = y.astype(BF16)
        a_re = jnp.dot(c_ref[...], yb, preferred_element_type=F32)
        a_im = jnp.dot(sf_ref[...], yb, preferred_element_type=F32)
        z_re = a_re * kr + a_im * ki
        z_im = a_im * krn - a_re * ki
        conv = (jnp.dot(c_ref[...], z_re.astype(BF16), preferred_element_type=F32)
                + jnp.dot(sb_ref[...], z_im.astype(BF16), preferred_element_type=F32))
        y = gate * (conv + y * side([hb_ref[o:o + 1]] * group))
    for j in range(group):
        o_ref[j * L:(j + 1) * L, :] = y[:, j * HY_W:(j + 1) * HY_W].astype(BF16)


def _hyena_group(hy, L, nb, group, blk0, dft, spectra, conv_w, conv_b, hy_bias):
    c, sf, sb, _ = dft
    kr, ki, krn = spectra
    const = lambda shape: pl.BlockSpec(shape, lambda b: (0,) * len(shape), pipeline_mode=pl.Buffered(1))
    return pl.pallas_call(
        functools.partial(_hyena_kernel, L=L, group=group),
        out_shape=jax.ShapeDtypeStruct((nb * L, HY_W), BF16),
        grid=(nb // group,),
        in_specs=[pl.BlockSpec((group * L, 3 * HY_W), lambda b: (blk0 // group + b, 0)),
                  _full((3, 3 * HY_W)), _full((1, 3 * HY_W)),
                  const((L, L)), const((L, L)), const((L, L)),
                  const((2, L, HY_W)), const((2, L, HY_W)), const((2, L, HY_W)),
                  _full((2, HY_W))],
        out_specs=pl.BlockSpec((group * L, HY_W), lambda b: (b, 0)),
        compiler_params=_cparams("parallel"),
        name=f"hyena_{L}",
    )(hy, conv_w, conv_b.reshape(1, 3 * HY_W), c, sf, sb, kr, ki, krn, hy_bias)


def _out_kernel(a_ref, bc_ref, bl_ref, mc_ref, ml_ref, w_ref, xc_ref, xl_ref, ng_ref, nb_ref, mod_ref,
                g_ref, be_ref, wr_ref, br_ref, tri_ref, upper_ref, x1_ref, xloc_ref, route_ref, cnt_ref,
                *, pre_norm):
    dot = functools.partial(jnp.dot, preferred_element_type=F32)
    mixed = (dot(a_ref[...], w_ref[0, 0:GM_W]) + dot(_pair_read(bc_ref, bl_ref), w_ref[0, GM_W:GM_W + HY_W])
             + dot(_pair_read(mc_ref, ml_ref), w_ref[0, GM_W + HY_W:]))
    x = _pair_read(xc_ref, xl_ref)
    if pre_norm:
        x = _layer_norm(x, ng_ref[...], nb_ref[...])
    m = mod_ref[0]
    x1 = _layer_norm(ALPHA * x + m[2:3] * mixed, g_ref[...], be_ref[...])
    x1_ref[...] = x1
    h2 = x1 * (1.0 + m[4:5]) + m[3:4]

    h_hi = h2.astype(BF16)
    h_lo = (h2 - h_hi.astype(F32)).astype(BF16)
    w_hi = wr_ref[...].astype(BF16)
    w_lo = (wr_ref[...] - w_hi.astype(F32)).astype(BF16)
    part = dot(h_hi, jnp.concatenate([w_hi, w_lo], axis=1))
    logits = part[:, :128] + part[:, 128:] + dot(h_lo, w_hi) + br_ref[...]
    lane = lax.broadcasted_iota(jnp.int32, logits.shape, 1)
    lanef = lane.astype(F32)
    big = jnp.float32(1e9)
    ninf = jnp.float32(-jnp.inf)
    is_g = lane < N_GROUPS
    gl = jnp.where(is_g, logits, ninf)
    gmax = jnp.max(gl, -1, keepdims=True)
    gidx = jnp.min(jnp.where(gl == gmax, lanef, big), -1, keepdims=True)
    gw = 1.0 / jnp.sum(jnp.where(is_g, jnp.exp(logits - gmax), 0.0), -1, keepdims=True)
    ex = lane - N_GROUPS
    in_group = (ex >= 0) & (ex < N_EXP) & ((ex // EPG).astype(F32) == gidx)
    el = jnp.where(in_group, logits, ninf)
    v1 = jnp.max(el, -1, keepdims=True)
    i1 = jnp.min(jnp.where(el == v1, lanef, big), -1, keepdims=True)
    el2 = jnp.where(lanef == i1, ninf, el)
    v2 = jnp.max(el2, -1, keepdims=True)
    i2 = jnp.min(jnp.where(el2 == v2, lanef, big), -1, keepdims=True)
    e21 = jnp.exp(v2 - v1)
    w1 = gw / (1.0 + e21)
    w2 = gw * e21 / (1.0 + e21)
    e1 = i1 - N_GROUPS
    e2 = i2 - N_GROUPS

    oh1 = jnp.where(lanef == e1, 1.0, 0.0)
    oh2 = jnp.where(lanef == e2, 1.0, 0.0)
    ex1 = dot(tri_ref[...], oh1.astype(BF16))
    ex2 = dot(tri_ref[...], oh2.astype(BF16))
    col1 = jnp.sum(oh1, axis=0, keepdims=True)
    col2 = jnp.sum(oh2, axis=0, keepdims=True)
    n = col1 + col2
    run = jnp.floor((n + (BLK - 1.0)) * (1.0 / BLK)) * BLK
    start = dot(jnp.broadcast_to(run, (8, 128)).astype(BF16), upper_ref[...])[0:1]
    loc1 = jnp.sum(oh1 * (start + ex1), -1, keepdims=True)
    loc2 = jnp.sum(oh2 * (start + col1 + ex2), -1, keepdims=True)
    cnt_ref[...] = jnp.broadcast_to(n, cnt_ref.shape)

    route = jnp.zeros_like(logits)
    for j, val in enumerate((e1, e2, w1, w2, loc1, loc2)):
        route = jnp.where(lane == j, val, route)
    route_ref[...] = route

    loc1_row = jnp.transpose(jnp.broadcast_to(loc1, (TM, 128)))[0:1].astype(jnp.int32)
    loc2_row = jnp.transpose(jnp.broadcast_to(loc2, (TM, 128)))[0:1].astype(jnp.int32)
    slot = lax.broadcasted_iota(jnp.int32, (LOC, TM), 0)
    perm = jnp.where((slot == loc1_row) | (slot == loc2_row), 1.0, 0.0).astype(BF16)
    xloc_ref[...] = dot(perm, h_hi).astype(BF16)


def _out_proj(l, a, hyb_pair, att_pair, w_out, x_pair, norm, mod_l, g, b, w_route, b_route, tri, upper):
    tile = lambda n: pl.BlockSpec((TM, n), lambda i: (i, 0))
    return pl.pallas_call(
        functools.partial(_out_kernel, pre_norm=(l == 0)),
        out_shape=(jax.ShapeDtypeStruct((T, D), F32), jax.ShapeDtypeStruct((N_TILES * LOC, D), BF16),
                   jax.ShapeDtypeStruct((T, 128), F32), jax.ShapeDtypeStruct((N_TILES * 8, 128), F32)),
        grid=(N_TILES,),
        in_specs=[tile(GM_W)] + _pair_specs(HY_W) + _pair_specs(HEADS * VD)
        + [pl.BlockSpec((1, D, D), lambda i: (l, 0, 0))]
        + _pair_specs(D) + [
                  _full((1, D)), _full((1, D)),
                  pl.BlockSpec((1, 6, D), lambda i: (_cond_of_tile(i), 0, 0)),
                  _full((1, D)), _full((1, D)), _full((D, 128)), _full((1, 128)), _full((TM, TM)),
                  _full((128, 128))],
        out_specs=(tile(D), pl.BlockSpec((LOC, D), lambda i: (i, 0)), tile(128),
                   pl.BlockSpec((8, 128), lambda i: (i, 0))),
        compiler_params=_cparams("parallel"),
        name="out_proj_route",
    )(a, *hyb_pair, *att_pair, w_out, *x_pair, *norm, mod_l, g, b, w_route, b_route, tri, upper)


X_RING = 4


def _expert_kernel(off_ref, cnt_ref, src_ref, x_hbm, wg_ref, wu_ref, wd_ref, y_hbm,
                   xbuf, ybuf, wgu_b, wd_b, sem, osem):
    e = pl.program_id(0)
    first_tile = off_ref[e]
    n_tiles = cnt_ref[e]
    total = off_ref[N_EXP - 1] + cnt_ref[N_EXP - 1]

    def block_copies(t, s):
        return [pltpu.make_async_copy(
            x_hbm.at[pl.ds(pl.multiple_of(src_ref[t * BLK_PER_TILE + b] * BLK, BLK), BLK), :],
            xbuf.at[s, pl.ds(b * BLK, BLK), :], sem.at[s]) for b in range(BLK_PER_TILE)]

    def out_copy(t, s):
        return pltpu.make_async_copy(ybuf.at[s], y_hbm.at[pl.ds(pl.multiple_of(t * TM, TM), TM), :],
                                     osem.at[s])

    @pl.when(e == 0)
    def _():
        for t0 in range(X_RING - 1):
            @pl.when(t0 < total)
            def _():
                for cp in block_copies(t0, t0):
                    cp.start()

    @pl.when(n_tiles > 0)
    def _():
        wgu_b[:, 0:D_EXP] = wg_ref[0, 0].astype(BF16)
        wgu_b[:, D_EXP:2 * D_EXP] = wu_ref[0, 0].astype(BF16)
        wd_b[...] = wd_ref[0, 0].astype(BF16)

    def tile_body(t, carry):
        s = t % 2
        xs = t % X_RING

        @pl.when(t + X_RING - 1 < total)
        def _():
            for cp in block_copies(t + X_RING - 1, (t + X_RING - 1) % X_RING):
                cp.start()

        for cp in block_copies(t, xs):
            cp.wait()

        @pl.when(t >= 2)
        def _():
            out_copy(t - 2, s).wait()

        gu = jnp.dot(xbuf[xs], wgu_b[...], preferred_element_type=F32)
        gate, up = gu[:, :D_EXP], gu[:, D_EXP:]
        hid = gate / (1.0 + jnp.exp(-gate)) * up
        ybuf[s] = jnp.dot(hid.astype(BF16), wd_b[...], preferred_element_type=F32).astype(BF16)
        out_copy(t, s).start()
        return carry

    lax.fori_loop(first_tile, first_tile + n_tiles, tile_body, 0)

    @pl.when(e == N_EXP - 1)
    def _():
        @pl.when(total >= 2)
        def _():
            out_copy(total - 2, total % 2).wait()

        out_copy(total - 1, (total - 1) % 2).wait()
        ybuf[0] = jnp.zeros((TM, D), BF16)

        def zero_start(t, carry):
            out_copy(t, 0).start()
            return carry

        def zero_wait(t, carry):
            out_copy(t, 0).wait()
            return carry

        lax.fori_loop(total, MOE_TILES, zero_start, 0)
        lax.fori_loop(total, MOE_TILES, zero_wait, 0)


def _experts(l, tables, xloc, w_gate, w_up, w_down):
    grid_spec = pltpu.PrefetchScalarGridSpec(
        num_scalar_prefetch=len(tables),
        grid=(N_EXP,),
        in_specs=[pl.BlockSpec(memory_space=pl.ANY),
                  pl.BlockSpec((1, 1, D, D_EXP), lambda e, *_: (l, e, 0, 0)),
                  pl.BlockSpec((1, 1, D, D_EXP), lambda e, *_: (l, e, 0, 0)),
                  pl.BlockSpec((1, 1, D_EXP, D), lambda e, *_: (l, e, 0, 0))],
        out_specs=pl.BlockSpec(memory_space=pl.ANY),
        scratch_shapes=[pltpu.VMEM((X_RING, TM, D), BF16), pltpu.VMEM((2, TM, D), BF16),
                        pltpu.VMEM((D, 2 * D_EXP), BF16), pltpu.VMEM((D_EXP, D), BF16),
                        pltpu.SemaphoreType.DMA((X_RING,)), pltpu.SemaphoreType.DMA((2,))],
    )
    return pl.pallas_call(
        _expert_kernel,
        out_shape=jax.ShapeDtypeStruct((MOE_ROWS, D), BF16),
        grid_spec=grid_spec,
        compiler_params=_cparams("arbitrary"),
        name="experts",
    )(*tables, xloc, w_gate, w_up, w_down)


def _combine_kernel(nblk_ref, gsrc_ref, y_hbm, route_ref, x1_ref, mod_ref, g_ref, b_ref, oc_ref, ol_ref,
                    ybuf, sem):
    i = pl.program_id(0)

    slot_i = i % 2

    def block_copy(t, s, lb):
        src = pl.multiple_of(gsrc_ref[t * LOC_BLKS + lb] * BLK, BLK)
        return pltpu.make_async_copy(y_hbm.at[pl.ds(src, BLK), :],
                                     ybuf.at[s, pl.ds(pl.multiple_of(lb * BLK, BLK), BLK), :], sem.at[s])

    def fetch(t, s):
        def start(lb, c):
            block_copy(t, s, lb).start()
            return c
        lax.fori_loop(0, nblk_ref[t], start, 0)

    @pl.when(i == 0)
    def _():
        ybuf[...] = jnp.zeros_like(ybuf)
        fetch(0, 0)

    @pl.when(i + 1 < N_TILES)
    def _():
        fetch(jnp.minimum(i + 1, N_TILES - 1), 1 - slot_i)

    def wait(lb, c):
        block_copy(i, slot_i, lb).wait()
        return c

    lax.fori_loop(0, nblk_ref[i], wait, 0)
    route = route_ref[...]
    yb = ybuf[slot_i]
    slot = lax.broadcasted_iota(jnp.int32, (TM, LOC), 1)
    pick = (jnp.where(slot == route[:, 4:5].astype(jnp.int32), route[:, 2:3], 0.0)
            + jnp.where(slot == route[:, 5:6].astype(jnp.int32), route[:, 3:4], 0.0)).astype(BF16)
    moe = jnp.dot(pick, yb, preferred_element_type=F32)
    m = mod_ref[0]
    res = _layer_norm(ALPHA * x1_ref[...] + m[5:6] * moe, g_ref[...], b_ref[...])

    @pl.when(i < CTX_TILES)
    def _():
        oc_ref[...] = res

    @pl.when(i >= CTX_TILES)
    def _():
        ol_ref[...] = res


def _combine(nblk, gsrc, y, route, x1, mod_l, g, b):
    out_shape = (jax.ShapeDtypeStruct((T_CTX, D), F32), jax.ShapeDtypeStruct((T_LAT, D), F32))
    out_specs = tuple(_pair_specs(D))
    grid_spec = pltpu.PrefetchScalarGridSpec(
        num_scalar_prefetch=2,
        grid=(N_TILES,),
        in_specs=[pl.BlockSpec(memory_space=pl.ANY),
                  pl.BlockSpec((TM, 128), lambda i, nb, gs: (i, 0)),
                  pl.BlockSpec((TM, D), lambda i, nb, gs: (i, 0)),
                  pl.BlockSpec((1, 6, D), lambda i, nb, gs: (_cond_of_tile(i), 0, 0)),
                  pl.BlockSpec((1, D), lambda i, nb, gs: (0, 0)),
                  pl.BlockSpec((1, D), lambda i, nb, gs: (0, 0))],
        out_specs=out_specs,
        scratch_shapes=[pltpu.VMEM((2, LOC, D), BF16), pltpu.SemaphoreType.DMA((2,))],
    )
    return pl.pallas_call(
        _combine_kernel,
        out_shape=out_shape,
        grid_spec=grid_spec,
        compiler_params=_cparams("arbitrary"),
        name="moe_combine",
    )(nblk, gsrc, y, route, x1, mod_l, g, b)


def _prefix_pick(starts, query, table):
    delta = table - jnp.concatenate([jnp.zeros_like(table[..., :1]), table[..., :-1]], axis=-1)
    return jnp.sum(jnp.where(starts <= query, delta, 0), axis=-1)


def _dispatch_tables(cnt):
    i32 = jnp.int32
    run = (cnt + BLK - 1) // BLK * BLK
    loc_start = jnp.cumsum(run, axis=1) - run
    nblk_loc = jnp.sum(run, axis=1) // BLK
    seg_rows = jnp.sum(run, axis=0)
    seg_tiles = (seg_rows + TM - 1) // TM
    tile_end = jnp.cumsum(seg_tiles)
    seg_start = (tile_end - seg_tiles) * TM
    glob_start = seg_start[None, :] + jnp.cumsum(run, axis=0) - run

    g_blk = (glob_start.T.reshape(1, -1)) // BLK
    n_blk = (run.T.reshape(1, -1)) // BLK
    l_blk = ((jnp.arange(N_TILES, dtype=i32) * LOC)[None, :] + loc_start.T).reshape(1, -1) // BLK
    gb = jnp.arange(MOE_ROWS // BLK, dtype=i32)[:, None]
    off = gb[:, 0] - _prefix_pick(g_blk, gb, g_blk)
    src_blk = jnp.where(off < _prefix_pick(g_blk, gb, n_blk), _prefix_pick(g_blk, gb, l_blk) + off,
                        ZERO_BLK).astype(i32)

    pos = (jnp.arange(LOC_BLKS, dtype=i32) * BLK)[None, :, None]
    shift = _prefix_pick(loc_start[:, None, :], pos, (glob_start - loc_start)[:, None, :])
    gsrc = ((pos[:, :, 0] + shift) // BLK).astype(i32)
    expert_tables = ((tile_end - seg_tiles).astype(i32), seg_tiles.astype(i32), src_blk)
    return expert_tables, nblk_loc.astype(i32), gsrc.reshape(-1)


def _rope_slot(pe):
    return jnp.pad(pe, [(0, 0)] * (pe.ndim - 1) + [(NOPE, SLOT - NOPE - ROPE)])


def _w_in_layout(w):
    return jnp.concatenate([w[..., :1664], _rope_slot(w[..., 1664:])], axis=-1).astype(BF16)


def _wuq_layout(w):
    w = w.reshape(DEPTH, Q_LORA, HEADS, NOPE + ROPE)
    w = jnp.pad(w, ((0, 0), (0, 0), (0, 0), (0, SLOT - NOPE - ROPE)))
    return w.reshape(DEPTH, Q_LORA, HEADS * SLOT).astype(BF16)


def _wukv_layout(w):
    w = w.reshape(DEPTH, KV_LORA, HEADS, NOPE + VD)
    wk = jnp.pad(w[..., :NOPE], ((0, 0), (0, 0), (0, 0), (0, SLOT - NOPE)))
    return (wk.reshape(DEPTH, KV_LORA, HEADS * SLOT).astype(BF16),
            w[..., NOPE:].reshape(DEPTH, KV_LORA, HEADS * VD).astype(BF16))


def kernel(x_prompt, x_sample, c, cache_ckv, cache_kpe, c_ctx, ln_in_g, ln_in_b, w_mod, b_mod, w_in,
           gm_ln_g, gm_ln_b, gm_ws, gm_bs, hy_conv_w, hy_conv_b, hy_f_w1, hy_f_b1, hy_f_w2, hy_f_b2,
           hy_f_w3, hy_f_freq, hy_bias, mla_gq, mla_gkv, mla_wuq, mla_wukv, w_out, ln1_g, ln1_b,
           ln2_g, ln2_b, moe_w_gr, moe_b_gr, moe_w_er, moe_b_er, moe_w_gate, moe_w_up, moe_w_down):
    rope = _rope_tables()
    dft = {L: _dft_tables(L) for L in (CTX_L, LAT_L)}
    pos_tab = {L: _hyena_positions(L) for L in (CTX_L, LAT_L)}
    hd = np.arange(GM_W) // GM_HD
    avg = jnp.asarray((hd[:, None] == hd[None, :]) / GM_HD, BF16)
    tri = jnp.asarray(np.tril(np.ones((TM, TM)), -1), BF16)
    upper = jnp.asarray(np.triu(np.ones((128, 128)), 1), BF16)

    cond8 = jnp.concatenate([c_ctx[None], c, jnp.zeros((8 - 1 - N_LAT_B, D), F32)], axis=0)
    mod = _modulation(cond8, w_mod, b_mod).reshape(DEPTH, 8, 6, D)

    wuk_all, wuv_all = _wukv_layout(mla_wukv)
    w_in_ext = _w_in_layout(w_in)
    wuq_arr = _wuq_layout(mla_wuq)
    w_out_b = w_out.astype(BF16)
    ws_b = gm_ws.astype(BF16)
    kc_all, vc_all = _cache_kv(cache_ckv, _rope_slot(cache_kpe), wuk_all, wuv_all)

    x_pair = (x_prompt.reshape(T_CTX, D), x_sample.reshape(T_LAT, D))
    norm = (ln_in_g.reshape(1, D), ln_in_b.reshape(1, D))
    ckv_states, kpe_states = [], []
    for l in range(DEPTH):
        bs_full = jnp.repeat(gm_bs[l].T, GM_HD, axis=1)
        a, hy, q, k, v, ckv, kpe = _in_proj(
            l, x_pair, norm, mod[l], w_in_ext, gm_ln_g[l].reshape(1, GM_W), gm_ln_b[l].reshape(1, GM_W), avg,
            ws_b, bs_full, mla_gq[l].reshape(1, Q_LORA), mla_gkv[l].reshape(1, KV_LORA),
            wuq_arr, wuk_all, wuv_all, rope)
        ckv_states.append(ckv[:T_CTX].reshape(N_CTX_B, CTX_L, KV_LORA))
        kpe_states.append(kpe[:T_CTX, NOPE:NOPE + ROPE].reshape(N_CTX_B, CTX_L, ROPE))

        hyb = []
        for L, nb, group, blk0 in ((CTX_L, N_CTX_B, 4, 0), (LAT_L, N_LAT_B, 2, T_CTX // LAT_L)):
            z, dec = pos_tab[L]
            spectra = _filter_spectra(L, z, dec, dft[L], hy_f_w1[l], hy_f_b1[l], hy_f_w2[l],
                                      hy_f_b2[l], hy_f_w3[l], hy_f_freq[l])
            hyb.append(_hyena_group(hy, L, nb, group, blk0, dft[L], spectra, hy_conv_w[l], hy_conv_b[l],
                                    hy_bias[l]))

        att = _attention(q, k, v, kc_all[l], vc_all[l])

        w_route = jnp.pad(jnp.concatenate([moe_w_gr[l], moe_w_er[l].reshape(D, N_EXP)], axis=1),
                          ((0, 0), (0, 128 - N_GROUPS - N_EXP)))
        b_route =jnp.pad(jnp.concatenate([moe_b_gr[l], moe_b_er[l].reshape(N_EXP)]),
                          (0, 128 - N_GROUPS - N_EXP)).reshape(1, 128)
        x1, xloc, route, counts = _out_proj(l, a, hyb, att, w_out_b, x_pair, norm, mod[l],
                                            ln1_g[l].reshape(1, D), ln1_b[l].reshape(1, D),
                                            w_route, b_route, tri, upper)
        cnt = counts.reshape(N_TILES, 8, 128)[:, 0, :N_EXP].astype(jnp.int32)
        expert_tables, nblk_loc, gsrc = _dispatch_tables(cnt)
        y = _experts(l, expert_tables, xloc, moe_w_gate, moe_w_up, moe_w_down)
        x_pair = _combine(nblk_loc, gsrc, y, route, x1, mod[l], ln2_g[l].reshape(1, D), ln2_b[l].reshape(1, D))

    y_prompt = x_pair[0].reshape(N_CTX_B, CTX_L, D)
    y_sample = x_pair[1].reshape(N_LAT_B, LAT_L, D)
    return (y_prompt, y_sample, jnp.stack(ckv_states, axis=1), jnp.stack(kpe_states, axis=1))
```

```python
import functools
import math

import numpy as np
import jax
import jax.numpy as jnp
from jax import lax
from jax.experimental import pallas as pl
from jax.experimental.pallas import tpu as pltpu

D = 1024
N_CTX_B, CTX_L = 16, 256
N_LAT_B, LAT_L = 4, 1024
DEPTH = 2
T_CTX = N_CTX_B * CTX_L
T_LAT = N_LAT_B * LAT_L
T = T_CTX + T_LAT
PAST = 256
GRID_W = 64

GM_HEADS, GM_HD, GM_W, CHUNK = 4, 64, 256, 128
HY_W, HY_EMB, HY_FH = 256, 33, 64
NOPE, ROPE, VD, HEADS = 64, 32, 64, 8
Q_LORA, KV_LORA = 256, 128
SLOT = 128
N_GROUPS, EPG, N_EXP, D_EXP = 4, 8, 32, 256
ALPHA = (2.0 * DEPTH) ** 0.25
LN_EPS, RMS_EPS = 1e-5, 1e-6
ROPE_THETA = 10000.0

TM = 256
N_TILES = T // TM
CTX_TILES = T_CTX // TM
LAT_TILES_PER_B = LAT_L // TM
N_ASSIGN = 2 * T
BLK = 16
BLK_PER_TILE = TM // BLK
LOC = 2 * TM + N_EXP * BLK
LOC_BLKS = LOC // BLK
ZERO_BLK = LOC_BLKS - 1
MOE_TILES = (N_ASSIGN + N_TILES * N_EXP * (BLK - 1)) // TM + N_EXP
MOE_ROWS = MOE_TILES * TM
IN_EXT = 1792

F32, BF16 = jnp.float32, jnp.bfloat16
_NT = (((1,), (1,)), ((), ()))
VMEM_LIMIT = 52 * 1024 * 1024


def _cparams(*sem):
    return pltpu.CompilerParams(dimension_semantics=sem, vmem_limit_bytes=VMEM_LIMIT)


def _cond_of_tile(i):
    return jnp.where(i < CTX_TILES, 0, 1 + (i - CTX_TILES) // LAT_TILES_PER_B)


def _rope_block_of_tile(i):
    return jnp.where(i < CTX_TILES, 0, 1 + (i - CTX_TILES) % LAT_TILES_PER_B)


def _full(shape):
    n = len(shape)
    return pl.BlockSpec(shape, lambda *_: (0,) * n)


def _pair_specs(n):
    return [pl.BlockSpec((TM, n), lambda i, *_: (jnp.minimum(i, CTX_TILES - 1), 0)),
            pl.BlockSpec((TM, n), lambda i, *_: (jnp.maximum(i - CTX_TILES, 0), 0))]


def _pair_read(c_ref, l_ref):
    return jnp.where(pl.program_id(0) < CTX_TILES, c_ref[...], l_ref[...])


def _layer_norm(x, g, b):
    mu = jnp.mean(x, -1, keepdims=True)
    xc = x - mu
    var = jnp.mean(xc * xc, -1, keepdims=True)
    return xc * lax.rsqrt(var + LN_EPS) * g + b


def _rms_norm(x, g):
    return x * lax.rsqrt(jnp.mean(x * x, -1, keepdims=True) + RMS_EPS) * g


def _bdot(a, b):
    return jnp.dot(a.astype(BF16), b.astype(BF16), preferred_element_type=F32)


def _dot3(x, w):
    x_hi = x.astype(BF16)
    x_lo = (x - x_hi.astype(F32)).astype(BF16)
    w_hi = w.astype(BF16)
    w_lo = (w - w_hi.astype(F32)).astype(BF16)
    dot = functools.partial(jnp.dot, preferred_element_type=F32)
    return dot(x_hi, w_hi) + dot(x_hi, w_lo) + dot(x_lo, w_hi)


def _swap_pair_lanes(x):
    n = x.shape[1]
    lane = lax.broadcasted_iota(jnp.int32, x.shape, 1)
    return jnp.where(lane % 2 == 0, pltpu.roll(x, n - 1, 1), pltpu.roll(x, 1, 1))


def _split2_dot(x, w_bf16):
    hi = x.astype(BF16)
    lo = (x - hi.astype(F32)).astype(BF16)
    dot = functools.partial(jnp.dot, preferred_element_type=F32)
    return dot(hi, w_bf16) + dot(lo, w_bf16)


def _dft_tables(L):
    f = np.arange(L, dtype=np.int64)
    ft = np.outer(f, f) % (2 * L)
    ang = np.pi * ft / L
    c = np.cos(ang)
    s = np.sin(ang)
    alt = np.where(f % 2 == 0, 1.0, -1.0)
    sf = s.copy()
    sf[0, :] = alt
    return (jnp.asarray(c, F32).astype(BF16), jnp.asarray(sf, F32).astype(BF16),
            jnp.asarray(sf.T.copy(), F32).astype(BF16), jnp.asarray(alt[:, None], F32))


def _rope_tables():
    rows = LAT_L // GRID_W
    row = np.repeat(np.arange(rows, dtype=np.float64), GRID_W)
    col = np.tile(np.arange(GRID_W, dtype=np.float64), rows)
    n = ROPE // 4
    inv = ROPE_THETA ** (-np.arange(n, dtype=np.float64) / n)
    ang = np.concatenate([row[:, None] * inv, col[:, None] * inv], -1)
    cos = np.zeros((CTX_L + LAT_L, SLOT))
    sin = np.zeros((CTX_L + LAT_L, SLOT))
    cos[:, :NOPE + ROPE] = 1.0
    cos[CTX_L:, NOPE:NOPE + ROPE:2] = np.cos(ang)
    cos[CTX_L:, NOPE + 1:NOPE + ROPE:2] = np.cos(ang)
    sin[CTX_L:, NOPE:NOPE + ROPE:2] = -np.sin(ang)
    sin[CTX_L:, NOPE + 1:NOPE + ROPE:2] = np.sin(ang)
    scale = math.log2(math.e) / math.sqrt(NOPE + ROPE)
    return (jnp.asarray(cos * scale, F32), jnp.asarray(sin * scale, F32),
            jnp.asarray(cos, F32), jnp.asarray(sin, F32))


def _hyena_positions(L):
    t = jnp.linspace(0.0, 1.0, L, dtype=F32)[:, None]
    bands = (HY_EMB - 1) // 2
    w = 2.0 * math.pi * jnp.arange(L, dtype=F32)[:, None] / L
    f = jnp.linspace(1e-4, bands - 1, bands, dtype=F32)[None]
    z = jnp.concatenate([t, jnp.cos(f * w), -jnp.sin(f * w)], -1)
    z = jnp.pad(z, ((0, 0), (0, 128 - HY_EMB)))
    min_decay = math.log(1e-2) / 1.5
    max_decay = math.log(1e-2) / 0.3
    deltas = jnp.abs(jnp.linspace(min_decay, max_decay, HY_W, dtype=F32))
    return z, jnp.exp(-t * deltas)


def _mod_kernel(c_ref, w_ref, b_ref, o_ref):
    c = c_ref[...]
    s = c / (1.0 + jnp.exp(-c))
    o_ref[0] = _bdot(s, w_ref[0]) + b_ref[0]


def _modulation(cond8, w_mod, b_mod):
    tn = 1536
    return pl.pallas_call(
        _mod_kernel,
        out_shape=jax.ShapeDtypeStruct((DEPTH, 8, 6 * D), F32),
        grid=(DEPTH, 6 * D // tn),
        in_specs=[pl.BlockSpec((8, D), lambda l, j: (0, 0)),
                  pl.BlockSpec((1, D, tn), lambda l, j: (l, 0, j)),
                  pl.BlockSpec((1, 1, tn), lambda l, j: (l, 0, j))],
        out_specs=pl.BlockSpec((1, 8, tn), lambda l, j: (l, 0, j)),
        compiler_params=_cparams("parallel", "parallel"),
        name="modulation",
    )(cond8, w_mod, b_mod.reshape(DEPTH, 1, 6 * D))


def _filter_kernel(z_ref, w1_ref, b1_ref, w2_ref, b2_ref, w3_ref, fr_ref, dec_ref, c_ref, sf_ref,
                   alt_ref, kr_ref, ki_ref, krn_ref, *, L):
    h1 = jnp.sin(fr_ref[0:1] * (_dot3(z_ref[...], w1_ref[...]) + b1_ref[...]))
    h2 = jnp.sin(fr_ref[1:2] * (_dot3(h1, w2_ref[...]) + b2_ref[...]))
    filt = _dot3(h2, w3_ref[...])
    row = lax.broadcasted_iota(jnp.int32, (L, HY_W), 0)
    dec = dec_ref[...]
    for o in range(2):
        fwd = filt[:, o * 2 * HY_W:o * 2 * HY_W + HY_W] * dec
        bwd = jnp.where(row == 0, 0.0, filt[:, o * 2 * HY_W + HY_W:(o + 1) * 2 * HY_W] * dec)
        sm = fwd + bwd
        df = fwd - bwd
        kr = _bdot(c_ref[...], sm)
        ki = -_bdot(sf_ref[...], df)
        kr_nyq = jnp.sum(alt_ref[...] * sm, axis=0, keepdims=True)
        kr_ref[o] = jnp.where(row == 0, kr * (0.5 / L), kr * (1.0 / L))
        ki_ref[o] = jnp.where(row == 0, 0.0, ki * (1.0 / L))
        krn_ref[o] = jnp.where(row == 0, kr_nyq * (0.5 / L), kr * (1.0 / L))


def _filter_spectra(L, z, dec, dft, w1, b1, w2, b2, w3, freq):
    c, sf, _, alt = dft
    pad = 128 - HY_FH
    w1p = jnp.pad(w1, ((0, 128 - HY_EMB), (0, pad)))
    w2p = jnp.pad(w2, ((0, pad), (0, pad)))
    w3p = jnp.pad(w3, ((0, pad), (0, 0)))
    b1p = jnp.pad(b1, (0, pad)).reshape(1, 128)
    b2p = jnp.pad(b2, (0, pad)).reshape(1, 128)
    frp = jnp.pad(freq, ((0, 0), (0, pad)))
    shp = jax.ShapeDtypeStruct((2, L, HY_W), F32)
    args = (z, w1p, b1p, w2p, b2p, w3p, frp, dec, c, sf, alt)
    return pl.pallas_call(
        functools.partial(_filter_kernel, L=L),
        out_shape=(shp, shp, shp),
        grid=(1,),
        in_specs=[_full(a.shape) for a in args],
        out_specs=(_full((2, L, HY_W)),) * 3,
        compiler_params=_cparams("arbitrary"),
        name=f"hyena_filter_{L}",
    )(*args)


def _in_kernel(xc_ref, xl_ref, ng_ref, nb_ref, mod_ref, w_ref, gg_ref, gb_ref, avg_ref, ws_ref, bs_ref,
               gq_ref, gkv_ref, wuq_ref, wuk_ref, wuv_ref, wuvt_ref, cq_ref, sq_ref, ck_ref, sk_ref,
               a_ref, hy_ref, q_ref, k_ref, v_ref, vt_ref, ckv_ref, kpe_ref, *, pre_norm):
    x = _pair_read(xc_ref, xl_ref)
    if pre_norm:
        x = _layer_norm(x, ng_ref[...], nb_ref[...])
    m = mod_ref[0]
    h = x * (1.0 + m[1:2]) + m[0:1]
    proj = _bdot(h, w_ref[0])

    gu = jax.nn.gelu(proj[:, 0:GM_W], approximate=True)
    gv = jax.nn.gelu(proj[:, GM_W:2 * GM_W], approximate=True)
    avg = avg_ref[...]
    mu = _split2_dot(gv, avg)
    vc = gv - mu
    var = _split2_dot(vc * vc, avg)
    vln = (vc * lax.rsqrt(var + LN_EPS) * gg_ref[...] + gb_ref[...]).astype(BF16)
    lane = lax.broadcasted_iota(jnp.int32, (CHUNK, GM_W), 1)
    for c in range(TM // CHUNK):
        vchunk = vln[c * CHUNK:(c + 1) * CHUNK]
        s = bs_ref[...]
        for hd in range(GM_HEADS):
            sh = jnp.dot(ws_ref[0, hd], vchunk, preferred_element_type=F32)
            s = s + jnp.where(lane // GM_HD == hd, sh, 0.0)
        a_ref[c * CHUNK:(c + 1) * CHUNK, :] = (gu[c * CHUNK:(c + 1) * CHUNK] * s).astype(BF16)

    hy_ref[...] = proj[:, 512:1280]

    cq = _rms_norm(proj[:, 1280:1536], gq_ref[...])
    qq = _bdot(cq, wuq_ref[0])
    cos_q = jnp.concatenate([cq_ref[...]] * HEADS, axis=1)
    sin_q = jnp.concatenate([sq_ref[...]] * HEADS, axis=1)
    q_ref[...] = (qq * cos_q + _swap_pair_lanes(qq) * sin_q).astype(BF16)

    ckv = _rms_norm(proj[:, 1536:1664], gkv_ref[...])
    ckv_ref[...] = ckv
    kpe = proj[:, 1664:1792]
    kpe_ref[...] = kpe
    krot = kpe * ck_ref[...] + _swap_pair_lanes(kpe) * sk_ref[...]
    kn = _bdot(ckv, wuk_ref[0])
    k_ref[...] = (kn + jnp.concatenate([krot] * HEADS, axis=1)).astype(BF16)
    v_ref[...] = _bdot(ckv, wuv_ref[0]).astype(BF16)
    vt_ref[...] = lax.dot_general(wuvt_ref[0], ckv.astype(BF16), _NT,
                                  preferred_element_type=F32).astype(BF16)


def _in_proj(l, x_pair, norm, mod_l, w_in_ext, gm_g, gm_b, avg, ws, bs_full, gq, gkv, wuq_arr,
             wuk_arr, wuv_arr, wuvt_arr, rope):
    layer = lambda *shape: pl.BlockSpec((1,) + shape, lambda i: (l,) + (0,) * len(shape))
    cos_q, sin_q, cos_k, sin_k = rope
    tile = lambda n: pl.BlockSpec((TM, n), lambda i: (i, 0))
    rope_spec = pl.BlockSpec((TM, SLOT), lambda i: (_rope_block_of_tile(i), 0))
    out_shapes = (jax.ShapeDtypeStruct((T, GM_W), BF16),
                  jax.ShapeDtypeStruct((T, 3 * HY_W), F32),
                  jax.ShapeDtypeStruct((T, HEADS * SLOT), BF16),
                  jax.ShapeDtypeStruct((T, HEADS * SLOT), BF16),
                  jax.ShapeDtypeStruct((T, HEADS * VD), BF16),
                  jax.ShapeDtypeStruct((HEADS * VD, T), BF16),
                  jax.ShapeDtypeStruct((T, KV_LORA), F32),
                  jax.ShapeDtypeStruct((T, SLOT), F32))
    return pl.pallas_call(
        functools.partial(_in_kernel, pre_norm=(l == 0)),
        out_shape=out_shapes,
        grid=(N_TILES,),
        in_specs=_pair_specs(D) + [
                  _full((1, D)), _full((1, D)),
                  pl.BlockSpec((1, 6, D), lambda i: (_cond_of_tile(i), 0, 0)),
                  layer(D, IN_EXT), _full((1, GM_W)), _full((1, GM_W)), _full((GM_W, GM_W)),
                  layer(GM_HEADS, CHUNK, CHUNK), _full((CHUNK, GM_W)),
                  _full((1, Q_LORA)), _full((1, KV_LORA)),
                  layer(Q_LORA, HEADS * SLOT), layer(KV_LORA, HEADS * SLOT),
                  layer(KV_LORA, HEADS * VD), layer(HEADS * VD, KV_LORA),
                  rope_spec, rope_spec, rope_spec, rope_spec],
        out_specs=(tile(GM_W), tile(3 * HY_W), tile(HEADS * SLOT), tile(HEADS * SLOT), tile(HEADS * VD),
                   pl.BlockSpec((HEADS * VD, TM), lambda i: (0, i)), tile(KV_LORA), tile(SLOT)),
        compiler_params=_cparams("parallel"),
        name="in_proj",
    )(*x_pair, *norm, mod_l, w_in_ext, gm_g, gm_b, avg, ws, bs_full, gq, gkv, wuq_arr, wuk_arr, wuv_arr,
      wuvt_arr, cos_q, sin_q, cos_k, sin_k)


def _cache_kernel(ckv_ref, kpe_ref, wuk_ref, wuv_ref, k_ref, v_ref):
    ckv = ckv_ref[0, 0]
    kn = _bdot(ckv, wuk_ref[0])
    k_ref[0, 0] = (kn + jnp.concatenate([kpe_ref[0, 0]] * HEADS, axis=1)).astype(BF16)
    v_ref[0, 0] = lax.dot_general(wuv_ref[0], ckv.astype(BF16), _NT,
                                  preferred_element_type=F32).astype(BF16)


def _cache_kv(cache_ckv, cache_kpe_slot, wuk_arr, wuv_arr):
    return pl.pallas_call(
        _cache_kernel,
        out_shape=(jax.ShapeDtypeStruct((DEPTH, N_LAT_B, PAST, HEADS * SLOT), BF16),
                   jax.ShapeDtypeStruct((DEPTH, N_LAT_B, HEADS * VD, PAST), BF16)),
        grid=(DEPTH, N_LAT_B),
        in_specs=[pl.BlockSpec((1, 1, PAST, KV_LORA), lambda l, b: (b, l, 0, 0)),
                  pl.BlockSpec((1, 1, PAST, SLOT), lambda l, b: (b, l, 0, 0)),
                  pl.BlockSpec((1, KV_LORA, HEADS * SLOT), lambda l, b: (l, 0, 0)),
                  pl.BlockSpec((1, HEADS * VD, KV_LORA), lambda l, b: (l, 0, 0))],
        out_specs=(pl.BlockSpec((1, 1, PAST, HEADS * SLOT), lambda l, b: (l, b, 0, 0)),
                   pl.BlockSpec((1, 1, HEADS * VD, PAST), lambda l, b: (l, b, 0, 0))),
        compiler_params=_cparams("parallel", "parallel"),
        name="cache_kv",
    )(cache_ckv, cache_kpe_slot, wuk_arr, wuv_arr)


def _attend_rows(q_ref, k_ref, v_ref, o_ref):
    lq = q_ref.shape[0]
    lane = lax.broadcasted_iota(jnp.int32, (lq, 2 * VD), 1)
    for pair in range(HEADS // 2):
        outs = []
        for hd in (2 * pair, 2 * pair + 1):
            s = lax.dot_general(q_ref[:, hd * SLOT:(hd + 1) * SLOT], k_ref[:, hd * SLOT:(hd + 1) * SLOT], _NT,
                                preferred_element_type=F32)
            p = jnp.exp2(s - jnp.max(s, -1, keepdims=True))
            acc = jnp.dot(p.astype(BF16), v_ref[:, pair * 2 * VD:(pair + 1) * 2 * VD],
                          preferred_element_type=F32)
            outs.append(acc / jnp.sum(p, -1, keepdims=True))
        o_ref[:, pair * 2 * VD:(pair + 1) * 2 * VD] = jnp.where(lane < VD, outs[0], outs[1]).astype(BF16)


def _attend_cols(q_ref, segments, o_ref):
    lq = q_ref.shape[0]
    lane = lax.broadcasted_iota(jnp.int32, (lq, 2 * SLOT), 1)
    row = lax.broadcasted_iota(jnp.int32, (2 * VD, lq), 0)
    outs = []
    for pair in range(HEADS // 2):
        qp = q_ref[:, pair * 2 * SLOT:(pair + 1) * 2 * SLOT]
        zero = jnp.zeros_like(qp)
        q_bd = jnp.concatenate([jnp.where(lane < SLOT, qp, zero), jnp.where(lane < SLOT, zero, qp)], axis=0)
        scores = [lax.dot_general(k_ref[:, pair * 2 * SLOT:(pair + 1) * 2 * SLOT], q_bd, _NT,
                                  preferred_element_type=F32) for k_ref, _ in segments]
        mx = functools.reduce(jnp.maximum, [jnp.max(s, 0, keepdims=True) for s in scores])
        ps = [jnp.exp2(s - mx) for s in scores]
        den = functools.reduce(jnp.add, [jnp.sum(p, 0, keepdims=True) for p in ps])
        acc = functools.reduce(jnp.add, [
            jnp.dot(vt_ref[pair * 2 * VD:(pair + 1) * 2 * VD, :], p.astype(BF16),
                    preferred_element_type=F32) for p, (_, vt_ref) in zip(ps, segments)])
        acc = acc / den
        outs.append(jnp.where(row < VD, acc[:, :lq], acc[:, lq:]))
    o_ref[...] = jnp.transpose(jnp.concatenate(outs, axis=0)).astype(BF16)


def _attn_ctx_kernel(q_ref, k_ref, v_ref, o_ref):
    _attend_rows(q_ref, k_ref, v_ref, o_ref)


def _attn_lat_kernel(q_ref, kc_ref, vc_ref, k_ref, vt_ref, o_ref):
    _attend_cols(q_ref, [(kc_ref.at[0], vc_ref.at[0]), (k_ref, vt_ref)], o_ref)


def _attention(q, k, v, vt, kc, vc):
    kw, vw = HEADS * SLOT, HEADS * VD
    ctx = pl.pallas_call(
        _attn_ctx_kernel,
        out_shape=jax.ShapeDtypeStruct((T_CTX, vw), BF16),
        grid=(N_CTX_B,),
        in_specs=[pl.BlockSpec((CTX_L, kw), lambda b: (b, 0)),
                  pl.BlockSpec((CTX_L, kw), lambda b: (b, 0)),
                  pl.BlockSpec((CTX_L, vw), lambda b: (b, 0))],
        out_specs=pl.BlockSpec((CTX_L, vw), lambda b: (b, 0)),
        compiler_params=_cparams("parallel"),
        name="attn_ctx",
    )(q, k, v)
    nq = LAT_L // TM
    off = T_CTX // LAT_L
    lat = pl.pallas_call(
        _attn_lat_kernel,
        out_shape=jax.ShapeDtypeStruct((T_LAT, vw), BF16),
        grid=(N_LAT_B, nq),
        in_specs=[pl.BlockSpec((TM, kw), lambda b, j: (CTX_TILES + b * nq + j, 0)),
                  pl.BlockSpec((1, PAST, kw), lambda b, j: (b, 0, 0)),
                  pl.BlockSpec((1, vw, PAST), lambda b, j: (b, 0, 0)),
                  pl.BlockSpec((LAT_L, kw), lambda b, j: (off + b, 0)),
                  pl.BlockSpec((vw, LAT_L), lambda b, j: (0, off + b))],
        out_specs=pl.BlockSpec((TM, vw), lambda b, j: (b * nq + j, 0)),
        compiler_params=_cparams("parallel", "parallel"),
        name="attn_lat",
    )(q, kc, vc, k, vt)
    return ctx, lat


def _hyena_kernel(hy_ref, cw_ref, cb_ref, c_ref, sf_ref, sb_ref, kr_ref, ki_ref, krn_ref, hb_ref,
                  o_ref, *, L, group):
    row = lax.broadcasted_iota(jnp.int32, (L, 3 * HY_W), 0)
    zs = []
    for j in range(group):
        x = hy_ref[j * L:(j + 1) * L, :]
        prev = jnp.where(row == 0, 0.0, pltpu.roll(x, 1, 0))
        nxt = jnp.where(row == L - 1, 0.0, pltpu.roll(x, L - 1, 0))
        zs.append(prev * cw_ref[0:1] + x * cw_ref[1:2] + nxt * cw_ref[2:3] + cb_ref[...])
    side = lambda parts: jnp.concatenate(parts, axis=1)
    y = side([z[:, 2 * HY_W:] for z in zs])
    for o in range(2):
        gate = side([z[:, o * HY_W:(o + 1) * HY_W] for z in zs])
        kr, ki, krn = (side([r[o]] * group) for r in (kr_ref, ki_ref, krn_ref))
        yb = y.astype(BF16)
        a_re = jnp.dot(c_ref[...], yb, preferred_element_type=F32)
        a_im = jnp.dot(sf_ref[...], yb, preferred_element_type=F32)
        z_re = a_re * kr + a_im * ki
        z_im = a_im * krn - a_re * ki
        conv = (jnp.dot(c_ref[...], z_re.astype(BF16), preferred_element_type=F32)
                + jnp.dot(sb_ref[...], z_im.astype(BF16), preferred_element_type=F32))
        y = gate * (conv + y * side([hb_ref[o:o + 1]] * group))
    for j in range(group):
        o_ref[j * L:(j + 1) * L, :] = y[:, j * HY_W:(j + 1) * HY_W].astype(BF16)


def _hyena_group(hy, L, nb, group, blk0, dft, spectra, conv_w, conv_b, hy_bias):
    c, sf, sb, _ = dft
    kr, ki, krn = spectra
    const = lambda shape: pl.BlockSpec(shape, lambda b: (0,) * len(shape), pipeline_mode=pl.Buffered(1))
    return pl.pallas_call(
        functools.partial(_hyena_kernel, L=L, group=group),
        out_shape=jax.ShapeDtypeStruct((nb * L, HY_W), BF16),
        grid=(nb // group,),
        in_specs=[pl.BlockSpec((group * L, 3 * HY_W), lambda b: (blk0 // group + b, 0)),
                  _full((3, 3 * HY_W)), _full((1, 3 * HY_W)),
                  const((L, L)), const((L, L)), const((L, L)),
                  const((2, L, HY_W)), const((2, L, HY_W)), const((2, L, HY_W)),
                  _full((2, HY_W))],
        out_specs=pl.BlockSpec((group * L, HY_W), lambda b: (b, 0)),
        compiler_params=_cparams("parallel"),
        name=f"hyena_{L}",
    )(hy, conv_w, conv_b.reshape(1, 3 * HY_W), c, sf, sb, kr, ki, krn, hy_bias)


def _out_kernel(a_ref, bc_ref, bl_ref, mc_ref, ml_ref, w_ref, xc_ref, xl_ref, ng_ref, nb_ref, mod_ref,
                g_ref, be_ref, wr_ref, br_ref, tri_ref, upper_ref, x1_ref, xloc_ref, route_ref, cnt_ref,
                *, pre_norm):
    dot = functools.partial(jnp.dot, preferred_element_type=F32)
    mixed = (dot(a_ref[...], w_ref[0, 0:GM_W]) + dot(_pair_read(bc_ref, bl_ref), w_ref[0, GM_W:GM_W + HY_W])
             + dot(_pair_read(mc_ref, ml_ref), w_ref[0, GM_W + HY_W:]))
    x = _pair_read(xc_ref, xl_ref)
    if pre_norm:
        x = _layer_norm(x, ng_ref[...], nb_ref[...])
    m = mod_ref[0]
    x1 = _layer_norm(ALPHA * x + m[2:3] * mixed, g_ref[...], be_ref[...])
    x1_ref[...] = x1
    h2 = x1 * (1.0 + m[4:5]) + m[3:4]

    h_hi = h2.astype(BF16)
    h_lo = (h2 - h_hi.astype(F32)).astype(BF16)
    w_hi = wr_ref[...].astype(BF16)
    w_lo = (wr_ref[...] - w_hi.astype(F32)).astype(BF16)
    part = dot(h_hi, jnp.concatenate([w_hi, w_lo], axis=1))
    logits = part[:, :128] + part[:, 128:] + dot(h_lo, w_hi) + br_ref[...]
    lane = lax.broadcasted_iota(jnp.int32, logits.shape, 1)
    lanef = lane.astype(F32)
    big = jnp.float32(1e9)
    ninf = jnp.float32(-jnp.inf)
    is_g = lane < N_GROUPS
    gl = jnp.where(is_g, logits, ninf)
    gmax = jnp.max(gl, -1, keepdims=True)
    gidx = jnp.min(jnp.where(gl == gmax, lanef, big), -1, keepdims=True)
    gw = 1.0 / jnp.sum(jnp.where(is_g, jnp.exp(logits - gmax), 0.0), -1, keepdims=True)
    ex = lane - N_GROUPS
    in_group = (ex >= 0) & (ex < N_EXP) & ((ex // EPG).astype(F32) == gidx)
    el = jnp.where(in_group, logits, ninf)
    v1 = jnp.max(el, -1, keepdims=True)
    i1 = jnp.min(jnp.where(el == v1, lanef, big), -1, keepdims=True)
    el2 = jnp.where(lanef == i1, ninf, el)
    v2 = jnp.max(el2, -1, keepdims=True)
    i2 = jnp.min(jnp.where(el2 == v2, lanef, big), -1, keepdims=True)
    e21 = jnp.exp(v2 - v1)
    w1 = gw / (1.0 + e21)
    w2 = gw * e21 / (1.0 + e21)
    e1 = i1 - N_GROUPS
    e2 = i2 - N_GROUPS

    oh1 = jnp.where(lanef == e1, 1.0, 0.0)
    oh2 = jnp.where(lanef == e2, 1.0, 0.0)
    ex1 = dot(tri_ref[...], oh1.astype(BF16))
    ex2 = dot(tri_ref[...], oh2.astype(BF16))
    col1 = jnp.sum(oh1, axis=0, keepdims=True)
    col2 = jnp.sum(oh2, axis=0, keepdims=True)
    n = col1 + col2
    run = jnp.floor((n + (BLK - 1.0)) * (1.0 / BLK)) * BLK
    start = dot(jnp.broadcast_to(run, (8, 128)).astype(BF16), upper_ref[...])[0:1]
    loc1 = jnp.sum(oh1 * (start + ex1), -1, keepdims=True)
    loc2 = jnp.sum(oh2 * (start + col1 + ex2), -1, keepdims=True)
    cnt_ref[...] = jnp.broadcast_to(n, cnt_ref.shape)

    route = jnp.zeros_like(logits)
    for j, val in enumerate((e1, e2, w1, w2, loc1, loc2)):
        route = jnp.where(lane == j, val, route)
    route_ref[...] = route

    loc1_row = jnp.transpose(jnp.broadcast_to(loc1, (TM, 128)))[0:1].astype(jnp.int32)
    loc2_row = jnp.transpose(jnp.broadcast_to(loc2, (TM, 128)))[0:1].astype(jnp.int32)
    slot = lax.broadcasted_iota(jnp.int32, (LOC, TM), 0)
    perm = jnp.where((slot == loc1_row) | (slot == loc2_row), 1.0, 0.0).astype(BF16)
    xloc_ref[...] = dot(perm, h_hi).astype(BF16)


def _out_proj(l, a, hyb_pair, att_pair, w_out, x_pair, norm, mod_l, g, b, w_route, b_route, tri, upper):
    tile = lambda n: pl.BlockSpec((TM, n), lambda i: (i, 0))
    return pl.pallas_call(
        functools.partial(_out_kernel, pre_norm=(l == 0)),
        out_shape=(jax.ShapeDtypeStruct((T, D), F32), jax.ShapeDtypeStruct((N_TILES * LOC, D), BF16),
                   jax.ShapeDtypeStruct((T, 128), F32), jax.ShapeDtypeStruct((N_TILES * 8, 128), F32)),
        grid=(N_TILES,),
        in_specs=[tile(GM_W)] + _pair_specs(HY_W) + _pair_specs(HEADS * VD)
        + [pl.BlockSpec((1, D, D), lambda i: (l, 0, 0))]
        + _pair_specs(D) + [
                  _full((1, D)), _full((1, D)),
                  pl.BlockSpec((1, 6, D), lambda i: (_cond_of_tile(i), 0, 0)),
                  _full((1, D)), _full((1, D)), _full((D, 128)), _full((1, 128)), _full((TM, TM)),
                  _full((128, 128))],
        out_specs=(tile(D), pl.BlockSpec((LOC, D), lambda i: (i, 0)), tile(128),
                   pl.BlockSpec((8, 128), lambda i: (i, 0))),
        compiler_params=_cparams("parallel"),
        name="out_proj_route",
    )(a, *hyb_pair, *att_pair, w_out, *x_pair, *norm, mod_l, g, b, w_route, b_route, tri, upper)


X_RING = 4


def _expert_kernel(off_ref, cnt_ref, src_ref, x_hbm, wg_ref, wu_ref, wd_ref, y_hbm,
                   xbuf, ybuf, wgu_b, wd_b, sem, osem):
    e = pl.program_id(0)
    first_tile = off_ref[e]
    n_tiles = cnt_ref[e]
    total = off_ref[N_EXP - 1] + cnt_ref[N_EXP - 1]

    def block_copies(t, s):
        return [pltpu.make_async_copy(
            x_hbm.at[pl.ds(pl.multiple_of(src_ref[t * BLK_PER_TILE + b] * BLK, BLK), BLK), :],
            xbuf.at[s, pl.ds(b * BLK, BLK), :], sem.at[s]) for b in range(BLK_PER_TILE)]

    def out_copy(t, s):
        return pltpu.make_async_copy(ybuf.at[s], y_hbm.at[pl.ds(pl.multiple_of(t * TM, TM), TM), :],
                                     osem.at[s])

    @pl.when(e == 0)
    def _():
        for t0 in range(X_RING - 1):
            @pl.when(t0 < total)
            def _():
                for cp in block_copies(t0, t0):
                    cp.start()

    @pl.when(n_tiles > 0)
    def _():
        wgu_b[:, 0:D_EXP] = wg_ref[0, 0].astype(BF16)
        wgu_b[:, D_EXP:2 * D_EXP] = wu_ref[0, 0].astype(BF16)
        wd_b[...] = wd_ref[0, 0].astype(BF16)

    def tile_body(t, carry):
        s = t % 2
        xs = t % X_RING

        @pl.when(t + X_RING - 1 < total)
        def _():
            for cp in block_copies(t + X_RING - 1, (t + X_RING - 1) % X_RING):
                cp.start()

        for cp in block_copies(t, xs):
            cp.wait()

        @pl.when(t >= 2)
        def _():
            out_copy(t - 2, s).wait()

        gu = jnp.dot(xbuf[xs], wgu_b[...], preferred_element_type=F32)
        gate, up = gu[:, :D_EXP], gu[:, D_EXP:]
        hid = gate / (1.0 + jnp.exp(-gate)) * up
        ybuf[s] = jnp.dot(hid.astype(BF16), wd_b[...], preferred_element_type=F32).astype(BF16)
        out_copy(t, s).start()
        return carry

    lax.fori_loop(first_tile, first_tile + n_tiles, tile_body, 0)

    @pl.when(e == N_EXP - 1)
    def _():
        @pl.when(total >= 2)
        def _():
            out_copy(total - 2, total % 2).wait()

        out_copy(total - 1, (total - 1) % 2).wait()
        ybuf[0] = jnp.zeros((TM, D), BF16)

        def zero_start(t, carry):
            out_copy(t, 0).start()
            return carry

        def zero_wait(t, carry):
            out_copy(t, 0).wait()
            return carry

        lax.fori_loop(total, MOE_TILES, zero_start, 0)
        lax.fori_loop(total, MOE_TILES, zero_wait, 0)


def _experts(l, tables, xloc, w_gate, w_up, w_down):
    grid_spec = pltpu.PrefetchScalarGridSpec(
        num_scalar_prefetch=len(tables),
        grid=(N_EXP,),
        in_specs=[pl.BlockSpec(memory_space=pl.ANY),
                  pl.BlockSpec((1, 1, D, D_EXP), lambda e, *_: (l, e, 0, 0)),
                  pl.BlockSpec((1, 1, D, D_EXP), lambda e, *_: (l, e, 0, 0)),
                  pl.BlockSpec((1, 1, D_EXP, D), lambda e, *_: (l, e, 0, 0))],
        out_specs=pl.BlockSpec(memory_space=pl.ANY),
        scratch_shapes=[pltpu.VMEM((X_RING, TM, D), BF16), pltpu.VMEM((2, TM, D), BF16),
                        pltpu.VMEM((D, 2 * D_EXP), BF16), pltpu.VMEM((D_EXP, D), BF16),
                        pltpu.SemaphoreType.DMA((X_RING,)), pltpu.SemaphoreType.DMA((2,))],
    )
    return pl.pallas_call(
        _expert_kernel,
        out_shape=jax.ShapeDtypeStruct((MOE_ROWS, D), BF16),
        grid_spec=grid_spec,
        compiler_params=_cparams("arbitrary"),
        name="experts",
    )(*tables, xloc, w_gate, w_up, w_down)


def _combine_kernel(nblk_ref, gsrc_ref, y_hbm, route_ref, x1_ref, mod_ref, g_ref, b_ref, oc_ref, ol_ref,
                    ybuf, sem):
    i = pl.program_id(0)

    slot_i = i % 2

    def block_copy(t, s, lb):
        src = pl.multiple_of(gsrc_ref[t * LOC_BLKS + lb] * BLK, BLK)
        return pltpu.make_async_copy(y_hbm.at[pl.ds(src, BLK), :],
                                     ybuf.at[s, pl.ds(pl.multiple_of(lb * BLK, BLK), BLK), :], sem.at[s])

    def fetch(t, s):
        def start(lb, c):
            block_copy(t, s, lb).start()
            return c
        lax.fori_loop(0, nblk_ref[t], start, 0)

    @pl.when(i == 0)
    def _():
        ybuf[...] = jnp.zeros_like(ybuf)
        fetch(0, 0)

    @pl.when(i + 1 < N_TILES)
    def _():
        fetch(jnp.minimum(i + 1, N_TILES - 1), 1 - slot_i)

    def wait(lb, c):
        block_copy(i, slot_i, lb).wait()
        return c

    lax.fori_loop(0, nblk_ref[i], wait, 0)
    route = route_ref[...]
    yb = ybuf[slot_i]
    slot = lax.broadcasted_iota(jnp.int32, (TM, LOC), 1)
    pick = (jnp.where(slot == route[:, 4:5].astype(jnp.int32), route[:, 2:3], 0.0)
            + jnp.where(slot == route[:, 5:6].astype(jnp.int32), route[:, 3:4], 0.0)).astype(BF16)
    moe = jnp.dot(pick, yb, preferred_element_type=F32)
    m = mod_ref[0]
    res = _layer_norm(ALPHA * x1_ref[...] + m[5:6] * moe, g_ref[...], b_ref[...])

    @pl.when(i < CTX_TILES)
    def _():
        oc_ref[...] = res

    @pl.when(i >= CTX_TILES)
    def _():
        ol_ref[...] = res


def _combine(nblk, gsrc, y, route, x1, mod_l, g, b):
    out_shape = (jax.ShapeDtypeStruct((T_CTX, D), F32), jax.ShapeDtypeStruct((T_LAT, D), F32))
    out_specs = tuple(_pair_specs(D))
    grid_spec = pltpu.PrefetchScalarGridSpec(
        num_scalar_prefetch=2,
        grid=(N_TILES,),
        in_specs=[pl.BlockSpec(memory_space=pl.ANY),
                  pl.BlockSpec((TM, 128), lambda i, nb, gs: (i, 0)),
                  pl.BlockSpec((TM, D), lambda i, nb, gs: (i, 0)),
                  pl.BlockSpec((1, 6, D), lambda i, nb, gs: (_cond_of_tile(i), 0, 0)),
                  pl.BlockSpec((1, D), lambda i, nb, gs: (0, 0)),
                  pl.BlockSpec((1, D), lambda i, nb, gs: (0, 0))],
        out_specs=out_specs,
        scratch_shapes=[pltpu.VMEM((2, LOC, D), BF16), pltpu.SemaphoreType.DMA((2,))],
    )
    return pl.pallas_call(
        _combine_kernel,
        out_shape=out_shape,
        grid_spec=grid_spec,
        compiler_params=_cparams("arbitrary"),
        name="moe_combine",
    )(nblk, gsrc, y, route, x1, mod_l, g, b)


def _prefix_pick(starts, query, table):
    delta = table - jnp.concatenate([jnp.zeros_like(table[..., :1]), table[..., :-1]], axis=-1)
    return jnp.sum(jnp.where(starts <= query, delta, 0), axis=-1)


def _dispatch_tables(cnt):
    i32 = jnp.int32
    run = (cnt + BLK - 1) // BLK * BLK
    loc_start = jnp.cumsum(run, axis=1) - run
    nblk_loc = jnp.sum(run, axis=1) // BLK
    seg_rows = jnp.sum(run, axis=0)
    seg_tiles = (seg_rows + TM - 1) // TM
    tile_end = jnp.cumsum(seg_tiles)
    seg_start = (tile_end - seg_tiles) * TM
    glob_start = seg_start[None, :] + jnp.cumsum(run, axis=0) - run

    g_blk = (glob_start.T.reshape(1, -1)) // BLK
    n_blk = (run.T.reshape(1, -1)) // BLK
    l_blk = ((jnp.arange(N_TILES, dtype=i32) * LOC)[None, :] + loc_start.T).reshape(1, -1) // BLK
    gb = jnp.arange(MOE_ROWS // BLK, dtype=i32)[:, None]
    off = gb[:, 0] - _prefix_pick(g_blk, gb, g_blk)
    src_blk = jnp.where(off < _prefix_pick(g_blk, gb, n_blk), _prefix_pick(g_blk, gb, l_blk) + off,
                        ZERO_BLK).astype(i32)

    pos = (jnp.arange(LOC_BLKS, dtype=i32) * BLK)[None, :, None]
    shift = _prefix_pick(loc_start[:, None, :], pos, (glob_start - loc_start)[:, None, :])
    gsrc = ((pos[:, :, 0] + shift) // BLK).astype(i32)
    expert_tables = ((tile_end - seg_tiles).astype(i32), seg_tiles.astype(i32), src_blk)
    return expert_tables, nblk_loc.astype(i32), gsrc.reshape(-1)


def _rope_slot(pe):
    return jnp.pad(pe, [(0, 0)] * (pe.ndim - 1) + [(NOPE, SLOT - NOPE - ROPE)])


def _w_in_layout(w):
    return jnp.concatenate([w[..., :1664], _rope_slot(w[..., 1664:])], axis=-1).astype(BF16)


def _wuq_layout(w):
    w = w.reshape(DEPTH, Q_LORA, HEADS, NOPE + ROPE)
    w = jnp.pad(w, ((0, 0), (0, 0), (0, 0), (0, SLOT - NOPE - ROPE)))
    return w.reshape(DEPTH, Q_LORA, HEADS * SLOT).astype(BF16)


def _wukv_layout(w):
    w = w.reshape(DEPTH, KV_LORA, HEADS, NOPE + VD)
    wk = jnp.pad(w[..., :NOPE], ((0, 0), (0, 0), (0, 0), (0, SLOT - NOPE)))
    wv = w[..., NOPE:].reshape(DEPTH, KV_LORA, HEADS * VD).astype(BF16)
    return wk.reshape(DEPTH, KV_LORA, HEADS * SLOT).astype(BF16), wv, jnp.swapaxes(wv, 1, 2)


def kernel(x_prompt, x_sample, c, cache_ckv, cache_kpe, c_ctx, ln_in_g, ln_in_b, w_mod, b_mod, w_in,
           gm_ln_g, gm_ln_b, gm_ws, gm_bs, hy_conv_w, hy_conv_b, hy_f_w1, hy_f_b1, hy_f_w2, hy_f_b2,
           hy_f_w3, hy_f_freq, hy_bias, mla_gq, mla_gkv, mla_wuq, mla_wukv, w_out, ln1_g, ln1_b,
           ln2_g, ln2_b, moe_w_gr, moe_b_gr, moe_w_er, moe_b_er, moe_w_gate, moe_w_up, moe_w_down):
    rope = _rope_tables()
    dft = {L: _dft_tables(L) for L in (CTX_L, LAT_L)}
    pos_tab = {L: _hyena_positions(L) for L in (CTX_L, LAT_L)}
    hd = np.arange(GM_W) // GM_HD
    avg = jnp.asarray((hd[:, None] == hd[None, :]) / GM_HD, BF16)
    tri = jnp.asarray(np.tril(np.ones((TM, TM)), -1), BF16)
    upper = jnp.asarray(np.triu(np.ones((128, 128)), 1), BF16)

    cond8 = jnp.concatenate([c_ctx[None], c, jnp.zeros((8 - 1 - N_LAT_B, D), F32)], axis=0)
    mod = _modulation(cond8, w_mod, b_mod).reshape(DEPTH, 8, 6, D)

    wuk_all, wuv_all, wuvt_all = _wukv_layout(mla_wukv)
    w_in_ext = _w_in_layout(w_in)
    wuq_arr = _wuq_layout(mla_wuq)
    w_out_b = w_out.astype(BF16)
    ws_b = gm_ws.astype(BF16)
    kc_all, vc_all = _cache_kv(cache_ckv, _rope_slot(cache_kpe), wuk_all, wuvt_all)

    x_pair = (x_prompt.reshape(T_CTX, D), x_sample.reshape(T_LAT, D))
    norm = (ln_in_g.reshape(1, D), ln_in_b.reshape(1, D))
    ckv_states, kpe_states = [], []
    for l in range(DEPTH):
        bs_full = jnp.repeat(gm_bs[l].T, GM_HD, axis=1)
        a, hy, q, k, v, vt, ckv, kpe = _in_proj(
            l, x_pair, norm, mod[l], w_in_ext, gm_ln_g[l].reshape(1, GM_W), gm_ln_b[l].reshape(1, GM_W), avg,
            ws_b, bs_full, mla_gq[l].reshape(1, Q_LORA), mla_gkv[l].reshape(1, KV_LORA),
            wuq_arr, wuk_all, wuv_all, wuvt_all, rope)
        ckv_states.append(ckv[:T_CTX].reshape(N_CTX_B, CTX_L, KV_LORA))
        kpe_states.append(kpe[:T_CTX, NOPE:NOPE + ROPE].reshape(N_CTX_B, CTX_L, ROPE))

        hyb = []
        for L, nb, group, blk0 in ((CTX_L, N_CTX_B, 4, 0), (LAT_L, N_LAT_B, 2, T_CTX // LAT_L)):
            z, dec = pos_tab[L]
            spectra = _filter_spectra(L, z, dec, dft[L], hy_f_w1[l], hy_f_b1[l], hy_f_w2[l],
                                      hy_f_b2[l], hy_f_w3[l], hy_f_freq[l])
            hyb.append(_hyena_group(hy, L, nb, group, blk0, dft[L], spectra, hy_conv_w[l], hy_conv_b[l],
                                    hy_bias[l]))

        att = _attention(q, k, v, vt, kc_all[l], vc_all[l])

        w_route = jnp.pad(jnp.concatenate([moe_w_gr[l], moe_w_er[l].reshape(D, N_EXP)], axis=1),
                          ((0, 0), (0, 128 - N_GROUPS - N_EXP)))
        b_route =jnp.pad(jnp.concatenate([moe_b_gr[l], moe_b_er[l].reshape(N_EXP)]),
                          (0, 128 - N_GROUPS - N_EXP)).reshape(1, 128)
        x1, xloc, route, counts = _out_proj(l, a, hyb, att, w_out_b, x_pair, norm, mod[l],
                                            ln1_g[l].reshape(1, D), ln1_b[l].reshape(1, D),
                                            w_route, b_route, tri, upper)
        cnt = counts.reshape(N_TILES, 8, 128)[:, 0, :N_EXP].astype(jnp.int32)
        expert_tables, nblk_loc, gsrc = _dispatch_tables(cnt)
        y = _experts(l, expert_tables, xloc, moe_w_gate, moe_w_up, moe_w_down)
        x_pair = _combine(nblk_loc, gsrc, y, route, x1, mod[l], ln2_g[l].reshape(1, D), ln2_b[l].reshape(1, D))

    y_prompt = x_pair[0].reshape(N_CTX_B, CTX_L, D)
    y_sample = x_pair[1].reshape(N_LAT_B, LAT_L, D)
    return (y_prompt, y_sample, jnp.stack(ckv_states, axis=1), jnp.stack(kpe_states, axis=1))
```

```python
import functools
import math

import numpy as np
import jax
import jax.numpy as jnp
from jax import lax
from jax.experimental import pallas as pl
from jax.experimental.pallas import tpu as pltpu

D = 1024
N_CTX_B, CTX_L = 16, 256
N_LAT_B, LAT_L = 4, 1024
DEPTH = 2
T_CTX = N_CTX_B * CTX_L
T_LAT = N_LAT_B * LAT_L
T = T_CTX + T_LAT
PAST = 256
GRID_W = 64

GM_HEADS, GM_HD, GM_W, CHUNK = 4, 64, 256, 128
HY_W, HY_EMB, HY_FH = 256, 33, 64
NOPE, ROPE, VD, HEADS = 64, 32, 64, 8
Q_LORA, KV_LORA = 256, 128
SLOT = 128
N_GROUPS, EPG, N_EXP, D_EXP = 4, 8, 32, 256
ALPHA = (2.0 * DEPTH) ** 0.25
LN_EPS, RMS_EPS = 1e-5, 1e-6
ROPE_THETA = 10000.0

TM = 256
N_TILES = T // TM
CTX_TILES = T_CTX // TM
LAT_TILES_PER_B = LAT_L // TM
N_ASSIGN = 2 * T
BLK = 16
BLK_PER_TILE = TM // BLK
LOC = 2 * TM + N_EXP * BLK
LOC_BLKS = LOC // BLK
ZERO_BLK = LOC_BLKS - 1
MOE_TILES = (N_ASSIGN + N_TILES * N_EXP * (BLK - 1)) // TM + N_EXP
MOE_ROWS = MOE_TILES * TM
IN_EXT = 1792

F32, BF16 = jnp.float32, jnp.bfloat16
_NT = (((1,), (1,)), ((), ()))
VMEM_LIMIT = 52 * 1024 * 1024


def _cparams(*sem):
    return pltpu.CompilerParams(dimension_semantics=sem, vmem_limit_bytes=VMEM_LIMIT)


def _cond_of_tile(i):
    return jnp.where(i < CTX_TILES, 0, 1 + (i - CTX_TILES) // LAT_TILES_PER_B)


def _rope_block_of_tile(i):
    return jnp.where(i < CTX_TILES, 0, 1 + (i - CTX_TILES) % LAT_TILES_PER_B)


def _full(shape):
    n = len(shape)
    return pl.BlockSpec(shape, lambda *_: (0,) * n)


def _layer_spec(l, *shape):
    return pl.BlockSpec((None,) + shape, lambda *_: (l,) + (0,) * len(shape))


def _mod_spec(l):
    return pl.BlockSpec((None, 1, 6, D), lambda i, *_: (l, _cond_of_tile(i), 0, 0))


def _pair_specs(n):
    return [pl.BlockSpec((TM, n), lambda i, *_: (jnp.minimum(i, CTX_TILES - 1), 0)),
            pl.BlockSpec((TM, n), lambda i, *_: (jnp.maximum(i - CTX_TILES, 0), 0))]


def _pair_read(c_ref, l_ref):
    return jnp.where(pl.program_id(0) < CTX_TILES, c_ref[...], l_ref[...])


def _layer_norm(x, g, b):
    mu = jnp.mean(x, -1, keepdims=True)
    xc = x - mu
    var = jnp.mean(xc * xc, -1, keepdims=True)
    return xc * lax.rsqrt(var + LN_EPS) * g + b


def _rms_norm(x, g):
    return x * lax.rsqrt(jnp.mean(x * x, -1, keepdims=True) + RMS_EPS) * g


def _bdot(a, b):
    return jnp.dot(a.astype(BF16), b.astype(BF16), preferred_element_type=F32)


def _dot3(x, w):
    x_hi = x.astype(BF16)
    x_lo = (x - x_hi.astype(F32)).astype(BF16)
    w_hi = w.astype(BF16)
    w_lo = (w - w_hi.astype(F32)).astype(BF16)
    dot = functools.partial(jnp.dot, preferred_element_type=F32)
    return dot(x_hi, w_hi) + dot(x_hi, w_lo) + dot(x_lo, w_hi)


def _swap_pair_lanes(x):
    n = x.shape[1]
    lane = lax.broadcasted_iota(jnp.int32, x.shape, 1)
    return jnp.where(lane % 2 == 0, pltpu.roll(x, n - 1, 1), pltpu.roll(x, 1, 1))


def _split2_dot(x, w_bf16):
    hi = x.astype(BF16)
    lo = (x - hi.astype(F32)).astype(BF16)
    dot = functools.partial(jnp.dot, preferred_element_type=F32)
    return dot(hi, w_bf16) + dot(lo, w_bf16)


def _dft_tables(L):
    f = np.arange(L, dtype=np.int64)
    ft = np.outer(f, f) % (2 * L)
    ang = np.pi * ft / L
    c = np.cos(ang)
    s = np.sin(ang)
    alt = np.where(f % 2 == 0, 1.0, -1.0)
    sf = s.copy()
    sf[0, :] = alt
    return (jnp.asarray(c, F32).astype(BF16), jnp.asarray(sf, F32).astype(BF16),
            jnp.asarray(sf.T.copy(), F32).astype(BF16), jnp.asarray(alt[:, None], F32))


def _rope_tables():
    rows = LAT_L // GRID_W
    row = np.repeat(np.arange(rows, dtype=np.float64), GRID_W)
    col = np.tile(np.arange(GRID_W, dtype=np.float64), rows)
    n = ROPE // 4
    inv = ROPE_THETA ** (-np.arange(n, dtype=np.float64) / n)
    ang = np.concatenate([row[:, None] * inv, col[:, None] * inv], -1)
    cos = np.zeros((CTX_L + LAT_L, SLOT))
    sin = np.zeros((CTX_L + LAT_L, SLOT))
    cos[:, :NOPE + ROPE] = 1.0
    cos[CTX_L:, NOPE:NOPE + ROPE:2] = np.cos(ang)
    cos[CTX_L:, NOPE + 1:NOPE + ROPE:2] = np.cos(ang)
    sin[CTX_L:, NOPE:NOPE + ROPE:2] = -np.sin(ang)
    sin[CTX_L:, NOPE + 1:NOPE + ROPE:2] = np.sin(ang)
    scale = math.log2(math.e) / math.sqrt(NOPE + ROPE)
    return (jnp.asarray(cos * scale, F32), jnp.asarray(sin * scale, F32),
            jnp.asarray(cos, F32), jnp.asarray(sin, F32))


def _hyena_positions(L):
    t = jnp.linspace(0.0, 1.0, L, dtype=F32)[:, None]
    bands = (HY_EMB - 1) // 2
    w = 2.0 * math.pi * jnp.arange(L, dtype=F32)[:, None] / L
    f = jnp.linspace(1e-4, bands - 1, bands, dtype=F32)[None]
    z = jnp.concatenate([t, jnp.cos(f * w), -jnp.sin(f * w)], -1)
    z = jnp.pad(z, ((0, 0), (0, 128 - HY_EMB)))
    min_decay = math.log(1e-2) / 1.5
    max_decay = math.log(1e-2) / 0.3
    deltas = jnp.abs(jnp.linspace(min_decay, max_decay, HY_W, dtype=F32))
    return z, jnp.exp(-t * deltas)


def _mod_kernel(c_ref, w_ref, b_ref, o_ref):
    c = c_ref[...]
    s = c / (1.0 + jnp.exp(-c))
    o_ref[0] = _bdot(s, w_ref[0]) + b_ref[0]


def _modulation(cond8, w_mod, b_mod):
    tn = 1536
    return pl.pallas_call(
        _mod_kernel,
        out_shape=jax.ShapeDtypeStruct((DEPTH, 8, 6 * D), F32),
        grid=(DEPTH, 6 * D // tn),
        in_specs=[pl.BlockSpec((8, D), lambda l, j: (0, 0)),
                  pl.BlockSpec((1, D, tn), lambda l, j: (l, 0, j)),
                  pl.BlockSpec((1, 1, tn), lambda l, j: (l, 0, j))],
        out_specs=pl.BlockSpec((1, 8, tn), lambda l, j: (l, 0, j)),
        compiler_params=_cparams("parallel", "parallel"),
        name="modulation",
    )(cond8, w_mod, b_mod.reshape(DEPTH, 1, 6 * D))


def _filter_kernel(z_ref, w1_ref, b1_ref, w2_ref, b2_ref, w3_ref, fr_ref, dec_ref, c_ref, sf_ref,
                   alt_ref, kr_ref, ki_ref, krn_ref, *, L):
    h1 = jnp.sin(fr_ref[0:1] * (_dot3(z_ref[...], w1_ref[...]) + b1_ref[...]))
    h2 = jnp.sin(fr_ref[1:2] * (_dot3(h1, w2_ref[...]) + b2_ref[...]))
    filt = _dot3(h2, w3_ref[...])
    row = lax.broadcasted_iota(jnp.int32, (L, HY_W), 0)
    dec = dec_ref[...]
    fwd = [filt[:, o * 2 * HY_W:o * 2 * HY_W + HY_W] * dec for o in range(2)]
    bwd = [jnp.where(row == 0, 0.0, filt[:, o * 2 * HY_W + HY_W:(o + 1) * 2 * HY_W] * dec) for o in range(2)]
    sm = [f + b for f, b in zip(fwd, bwd)]
    kr2 = _bdot(c_ref[...], jnp.concatenate(sm, axis=1))
    ki2 = -_bdot(sf_ref[...], jnp.concatenate([f - b for f, b in zip(fwd, bwd)], axis=1))
    for o in range(2):
        kr = kr2[:, o * HY_W:(o + 1) * HY_W]
        kr_nyq = jnp.sum(alt_ref[...] * sm[o], axis=0, keepdims=True)
        kr_ref[o] = jnp.where(row == 0, kr * (0.5 / L), kr * (1.0 / L))
        ki_ref[o] = jnp.where(row == 0, 0.0, ki2[:, o * HY_W:(o + 1) * HY_W] * (1.0 / L))
        krn_ref[o] = jnp.where(row == 0, kr_nyq * (0.5 / L), kr * (1.0 / L))


def _filter_spectra(L, z, dec, dft, w1, b1, w2, b2, w3, freq):
    c, sf, _, alt = dft
    pad = 128 - HY_FH
    w1p = jnp.pad(w1, ((0, 0), (0, 128 - HY_EMB), (0, pad)))
    w2p = jnp.pad(w2, ((0, 0), (0, pad), (0, pad)))
    w3p = jnp.pad(w3, ((0, 0), (0, pad), (0, 0)))
    b1p = jnp.pad(b1, ((0, 0), (0, pad))).reshape(DEPTH, 1, 128)
    b2p = jnp.pad(b2, ((0, 0), (0, pad))).reshape(DEPTH, 1, 128)
    frp = jnp.pad(freq, ((0, 0), (0, 0), (0, pad)))
    shp = jax.ShapeDtypeStruct((DEPTH, 2, L, HY_W), F32)
    per_layer = lambda a: pl.BlockSpec((None,) + a.shape[1:], lambda l: (l,) + (0,) * (a.ndim - 1))
    out_spec = pl.BlockSpec((None, 2, L, HY_W), lambda l: (l, 0, 0, 0))
    return pl.pallas_call(
        functools.partial(_filter_kernel, L=L),
        out_shape=(shp, shp, shp),
        grid=(DEPTH,),
        in_specs=[_full(z.shape)] + [per_layer(a) for a in (w1p, b1p, w2p, b2p, w3p, frp)]
        + [_full(a.shape) for a in (dec, c, sf, alt)],
        out_specs=(out_spec,) * 3,
        compiler_params=_cparams("arbitrary"),
        name=f"hyena_filter_{L}",
    )(z, w1p, b1p, w2p, b2p, w3p, frp, dec, c, sf, alt)


def _in_kernel(xc_ref, xl_ref, ng_ref, nb_ref, mod_ref, w_ref, gg_ref, gb_ref, avg_ref, ws_ref, bs_ref,
               gq_ref, gkv_ref, wuq_ref, wuk_ref, wuv_ref, wuvt_ref, cq_ref, sq_ref, ck_ref, sk_ref,
               a_ref, hy_ref, q_ref, k_ref, v_ref, vt_ref, ckv_ref, kpe_ref, *, pre_norm):
    x = _pair_read(xc_ref, xl_ref)
    if pre_norm:
        x = _layer_norm(x, ng_ref[...], nb_ref[...])
    m = mod_ref[0]
    h = x * (1.0 + m[1:2]) + m[0:1]
    proj = _bdot(h, w_ref[0])

    gu = jax.nn.gelu(proj[:, 0:GM_W], approximate=True)
    gv = jax.nn.gelu(proj[:, GM_W:2 * GM_W], approximate=True)
    avg = avg_ref[...]
    mu = _split2_dot(gv, avg)
    vc = gv - mu
    var = _split2_dot(vc * vc, avg)
    vln = (vc * lax.rsqrt(var + LN_EPS) * gg_ref[...] + gb_ref[...]).astype(BF16)
    lane = lax.broadcasted_iota(jnp.int32, (CHUNK, GM_W), 1)
    for c in range(TM // CHUNK):
        vchunk = vln[c * CHUNK:(c + 1) * CHUNK]
        s = bs_ref[...]
        for hd in range(GM_HEADS):
            sh = jnp.dot(ws_ref[0, hd], vchunk, preferred_element_type=F32)
            s = s + jnp.where(lane // GM_HD == hd, sh, 0.0)
        a_ref[c * CHUNK:(c + 1) * CHUNK, :] = (gu[c * CHUNK:(c + 1) * CHUNK] * s).astype(BF16)

    hy_ref[...] = proj[:, 512:1280]

    cq = _rms_norm(proj[:, 1280:1536], gq_ref[...])
    qq = _bdot(cq, wuq_ref[0])
    cos_q = jnp.concatenate([cq_ref[...]] * HEADS, axis=1)
    sin_q = jnp.concatenate([sq_ref[...]] * HEADS, axis=1)
    q_ref[...] = (qq * cos_q + _swap_pair_lanes(qq) * sin_q).astype(BF16)

    ckv = _rms_norm(proj[:, 1536:1664], gkv_ref[...])
    ckv_ref[...] = ckv
    kpe = proj[:, 1664:1792]
    kpe_ref[...] = kpe
    krot = kpe * ck_ref[...] + _swap_pair_lanes(kpe) * sk_ref[...]
    kn = _bdot(ckv, wuk_ref[0])
    k_ref[...] = (kn + jnp.concatenate([krot] * HEADS, axis=1)).astype(BF16)
    v_ref[...] = _bdot(ckv, wuv_ref[0]).astype(BF16)
    vt_ref[...] = lax.dot_general(wuvt_ref[0], ckv.astype(BF16), _NT,
                                  preferred_element_type=F32).astype(BF16)


def _in_proj(l, x_pair, norm, mod_l, w_in_ext, gm_g, gm_b, avg, ws, bs_full, gq, gkv, wuq_arr,
             wuk_arr, wuv_arr, wuvt_arr, rope):
    layer = lambda *shape: pl.BlockSpec((1,) + shape, lambda i: (l,) + (0,) * len(shape))
    cos_q, sin_q, cos_k, sin_k = rope
    tile = lambda n: pl.BlockSpec((TM, n), lambda i: (i, 0))
    rope_spec = pl.BlockSpec((TM, SLOT), lambda i: (_rope_block_of_tile(i), 0))
    out_shapes = (jax.ShapeDtypeStruct((T, GM_W), BF16),
                  jax.ShapeDtypeStruct((T, 3 * HY_W), F32),
                  jax.ShapeDtypeStruct((T, HEADS * SLOT), BF16),
                  jax.ShapeDtypeStruct((T, HEADS * SLOT), BF16),
                  jax.ShapeDtypeStruct((T, HEADS * VD), BF16),
                  jax.ShapeDtypeStruct((HEADS * VD, T), BF16),
                  jax.ShapeDtypeStruct((T, KV_LORA), F32),
                  jax.ShapeDtypeStruct((T, SLOT), F32))
    return pl.pallas_call(
        functools.partial(_in_kernel, pre_norm=(l == 0)),
        out_shape=out_shapes,
        grid=(N_TILES,),
        in_specs=_pair_specs(D) + [
                  _full((1, D)), _full((1, D)),
                  _mod_spec(l),
                  layer(D, IN_EXT), _layer_spec(l, 1, GM_W), _layer_spec(l, 1, GM_W), _full((GM_W, GM_W)),
                  layer(GM_HEADS, CHUNK, CHUNK), _layer_spec(l, CHUNK, GM_W),
                  _layer_spec(l, 1, Q_LORA), _layer_spec(l, 1, KV_LORA),
                  layer(Q_LORA, HEADS * SLOT), layer(KV_LORA, HEADS * SLOT),
                  layer(KV_LORA, HEADS * VD), layer(HEADS * VD, KV_LORA),
                  rope_spec, rope_spec, rope_spec, rope_spec],
        out_specs=(tile(GM_W), tile(3 * HY_W), tile(HEADS * SLOT), tile(HEADS * SLOT), tile(HEADS * VD),
                   pl.BlockSpec((HEADS * VD, TM), lambda i: (0, i)), tile(KV_LORA), tile(SLOT)),
        compiler_params=_cparams("parallel"),
        name="in_proj",
    )(*x_pair, *norm, mod_l, w_in_ext, gm_g, gm_b, avg, ws, bs_full, gq, gkv, wuq_arr, wuk_arr, wuv_arr,
      wuvt_arr, cos_q, sin_q, cos_k, sin_k)


def _cache_kernel(ckv_ref, kpe_ref, wuk_ref, wuv_ref, k_ref, v_ref):
    ckv = ckv_ref[0, 0]
    kn = _bdot(ckv, wuk_ref[0])
    k_ref[0, 0] = (kn + jnp.concatenate([kpe_ref[0, 0]] * HEADS, axis=1)).astype(BF16)
    v_ref[0, 0] = lax.dot_general(wuv_ref[0], ckv.astype(BF16), _NT,
                                  preferred_element_type=F32).astype(BF16)


def _cache_kv(cache_ckv, cache_kpe_slot, wuk_arr, wuv_arr):
    return pl.pallas_call(
        _cache_kernel,
        out_shape=(jax.ShapeDtypeStruct((DEPTH, N_LAT_B, PAST, HEADS * SLOT), BF16),
                   jax.ShapeDtypeStruct((DEPTH, N_LAT_B, HEADS * VD, PAST), BF16)),
        grid=(DEPTH, N_LAT_B),
        in_specs=[pl.BlockSpec((1, 1, PAST, KV_LORA), lambda l, b: (b, l, 0, 0)),
                  pl.BlockSpec((1, 1, PAST, SLOT), lambda l, b: (b, l, 0, 0)),
                  pl.BlockSpec((1, KV_LORA, HEADS * SLOT), lambda l, b: (l, 0, 0)),
                  pl.BlockSpec((1, HEADS * VD, KV_LORA), lambda l, b: (l, 0, 0))],
        out_specs=(pl.BlockSpec((1, 1, PAST, HEADS * SLOT), lambda l, b: (l, b, 0, 0)),
                   pl.BlockSpec((1, 1, HEADS * VD, PAST), lambda l, b: (l, b, 0, 0))),
        compiler_params=_cparams("parallel", "parallel"),
        name="cache_kv",
    )(cache_ckv, cache_kpe_slot, wuk_arr, wuv_arr)


def _attend_rows(q_ref, k_ref, v_ref, o_ref):
    lq = q_ref.shape[0]
    lane = lax.broadcasted_iota(jnp.int32, (lq, 2 * VD), 1)
    for pair in range(HEADS // 2):
        outs = []
        for hd in (2 * pair, 2 * pair + 1):
            s = lax.dot_general(q_ref[:, hd * SLOT:(hd + 1) * SLOT], k_ref[:, hd * SLOT:(hd + 1) * SLOT], _NT,
                                preferred_element_type=F32)
            p = jnp.exp2(s - jnp.max(s, -1, keepdims=True))
            acc = jnp.dot(p.astype(BF16), v_ref[:, pair * 2 * VD:(pair + 1) * 2 * VD],
                          preferred_element_type=F32)
            outs.append(acc / jnp.sum(p, -1, keepdims=True))
        o_ref[:, pair * 2 * VD:(pair + 1) * 2 * VD] = jnp.where(lane < VD, outs[0], outs[1]).astype(BF16)


def _attend_cols(q_ref, segments, o_ref):
    lq = q_ref.shape[0]
    lane = lax.broadcasted_iota(jnp.int32, (lq, 2 * SLOT), 1)
    row = lax.broadcasted_iota(jnp.int32, (2 * VD, lq), 0)
    outs = []
    for pair in range(HEADS // 2):
        qp = q_ref[:, pair * 2 * SLOT:(pair + 1) * 2 * SLOT]
        zero = jnp.zeros_like(qp)
        q_bd = jnp.concatenate([jnp.where(lane < SLOT, qp, zero), jnp.where(lane < SLOT, zero, qp)], axis=0)
        scores = [lax.dot_general(k_ref[:, pair * 2 * SLOT:(pair + 1) * 2 * SLOT], q_bd, _NT,
                                  preferred_element_type=F32) for k_ref, _ in segments]
        mx = functools.reduce(jnp.maximum, [jnp.max(s, 0, keepdims=True) for s in scores])
        ps = [jnp.exp2(s - mx) for s in scores]
        den = functools.reduce(jnp.add, [jnp.sum(p, 0, keepdims=True) for p in ps])
        acc = functools.reduce(jnp.add, [
            jnp.dot(vt_ref[pair * 2 * VD:(pair + 1) * 2 * VD, :], p.astype(BF16),
                    preferred_element_type=F32) for p, (_, vt_ref) in zip(ps, segments)])
        acc = acc / den
        outs.append(jnp.where(row < VD, acc[:, :lq], acc[:, lq:]))
    o_ref[...] = jnp.transpose(jnp.concatenate(outs, axis=0)).astype(BF16)


def _attn_ctx_kernel(q_ref, k_ref, v_ref, o_ref):
    _attend_rows(q_ref, k_ref, v_ref, o_ref)


def _attn_lat_kernel(q_ref, kc_ref, vc_ref, k_ref, vt_ref, o_ref):
    _attend_cols(q_ref, [(kc_ref.at[0], vc_ref.at[0]), (k_ref, vt_ref)], o_ref)


def _attention(q, k, v, vt, kc, vc):
    kw, vw = HEADS * SLOT, HEADS * VD
    ctx = pl.pallas_call(
        _attn_ctx_kernel,
        out_shape=jax.ShapeDtypeStruct((T_CTX, vw), BF16),
        grid=(N_CTX_B,),
        in_specs=[pl.BlockSpec((CTX_L, kw), lambda b: (b, 0)),
                  pl.BlockSpec((CTX_L, kw), lambda b: (b, 0)),
                  pl.BlockSpec((CTX_L, vw), lambda b: (b, 0))],
        out_specs=pl.BlockSpec((CTX_L, vw), lambda b: (b, 0)),
        compiler_params=_cparams("parallel"),
        name="attn_ctx",
    )(q, k, v)
    nq = LAT_L // TM
    off = T_CTX // LAT_L
    lat = pl.pallas_call(
        _attn_lat_kernel,
        out_shape=jax.ShapeDtypeStruct((T_LAT, vw), BF16),
        grid=(N_LAT_B, nq),
        in_specs=[pl.BlockSpec((TM, kw), lambda b, j: (CTX_TILES + b * nq + j, 0)),
                  pl.BlockSpec((1, PAST, kw), lambda b, j: (b, 0, 0)),
                  pl.BlockSpec((1, vw, PAST), lambda b, j: (b, 0, 0)),
                  pl.BlockSpec((LAT_L, kw), lambda b, j: (off + b, 0)),
                  pl.BlockSpec((vw, LAT_L), lambda b, j: (0, off + b))],
        out_specs=pl.BlockSpec((TM, vw), lambda b, j: (b * nq + j, 0)),
        compiler_params=_cparams("parallel", "parallel"),
        name="attn_lat",
    )(q, kc, vc, k, vt)
    return ctx, lat


def _hyena_kernel(hy_ref, cw_ref, cb_ref, c_ref, sf_ref, sb_ref, kr_ref, ki_ref, krn_ref, hb_ref,
                  o_ref, *, L, group):
    row = lax.broadcasted_iota(jnp.int32, (L, 3 * HY_W), 0)
    zs = []
    for j in range(group):
        x = hy_ref[j * L:(j + 1) * L, :]
        prev = jnp.where(row == 0, 0.0, pltpu.roll(x, 1, 0))
        nxt = jnp.where(row == L - 1, 0.0, pltpu.roll(x, L - 1, 0))
        zs.append(prev * cw_ref[0:1] + x * cw_ref[1:2] + nxt * cw_ref[2:3] + cb_ref[...])
    side = lambda parts: jnp.concatenate(parts, axis=1)
    y = side([z[:, 2 * HY_W:] for z in zs])
    for o in range(2):
        gate = side([z[:, o * HY_W:(o + 1) * HY_W] for z in zs])
        kr, ki, krn = (side([r[o]] * group) for r in (kr_ref, ki_ref, krn_ref))
        yb = y.astype(BF16)
        a_re = jnp.dot(c_ref[...], yb, preferred_element_type=F32)
        a_im = jnp.dot(sf_ref[...], yb, preferred_element_type=F32)
        z_re = a_re * kr + a_im * ki
        z_im = a_im * krn - a_re * ki
        conv = (jnp.dot(c_ref[...], z_re.astype(BF16), preferred_element_type=F32)
                + jnp.dot(sb_ref[...], z_im.astype(BF16), preferred_element_type=F32))
        y = gate * (conv + y * side([hb_ref[o:o + 1]] * group))
    for j in range(group):
        o_ref[j * L:(j + 1) * L, :] = y[:, j * HY_W:(j + 1) * HY_W].astype(BF16)


def _hyena_group(l, hy, L, nb, group, blk0, dft, spectra, conv_w, conv_b, hy_bias):
    c, sf, sb, _ = dft
    kr, ki, krn = spectra
    const = lambda shape: pl.BlockSpec(shape, lambda b: (0,) * len(shape), pipeline_mode=pl.Buffered(1))
    spec = pl.BlockSpec((None, 2, L, HY_W), lambda b: (l, 0, 0, 0), pipeline_mode=pl.Buffered(1))
    return pl.pallas_call(
        functools.partial(_hyena_kernel, L=L, group=group),
        out_shape=jax.ShapeDtypeStruct((nb * L, HY_W), BF16),
        grid=(nb // group,),
        in_specs=[pl.BlockSpec((group * L, 3 * HY_W), lambda b: (blk0 // group + b, 0)),
                  _layer_spec(l, 3, 3 * HY_W), _layer_spec(l, 1, 3 * HY_W),
                  const((L, L)), const((L, L)), const((L, L)), spec, spec, spec,
                  _layer_spec(l, 2, HY_W)],
        out_specs=pl.BlockSpec((group * L, HY_W), lambda b: (b, 0)),
        compiler_params=_cparams("parallel"),
        name=f"hyena_{L}",
    )(hy, conv_w, conv_b.reshape(DEPTH, 1, 3 * HY_W), c, sf, sb, kr, ki, krn, hy_bias)


def _out_kernel(a_ref, bc_ref, bl_ref, mc_ref, ml_ref, w_ref, xc_ref, xl_ref, ng_ref, nb_ref, mod_ref,
                g_ref, be_ref, wr_ref, br_ref, tri_ref, upper_ref, x1_ref, xloc_ref, route_ref, cnt_ref,
                *, pre_norm):
    dot = functools.partial(jnp.dot, preferred_element_type=F32)
    mixed = (dot(a_ref[...], w_ref[0, 0:GM_W]) + dot(_pair_read(bc_ref, bl_ref), w_ref[0, GM_W:GM_W + HY_W])
             + dot(_pair_read(mc_ref, ml_ref), w_ref[0, GM_W + HY_W:]))
    x = _pair_read(xc_ref, xl_ref)
    if pre_norm:
        x = _layer_norm(x, ng_ref[...], nb_ref[...])
    m = mod_ref[0]
    x1 = _layer_norm(ALPHA * x + m[2:3] * mixed, g_ref[...], be_ref[...])
    x1_ref[...] = x1
    h2 = x1 * (1.0 + m[4:5]) + m[3:4]

    h_hi = h2.astype(BF16)
    h_lo = (h2 - h_hi.astype(F32)).astype(BF16)
    w_hi = wr_ref[...].astype(BF16)
    w_lo = (wr_ref[...] - w_hi.astype(F32)).astype(BF16)
    part = dot(h_hi, jnp.concatenate([w_hi, w_lo], axis=1))
    logits = part[:, :128] + part[:, 128:] + dot(h_lo, w_hi) + br_ref[...]
    lane = lax.broadcasted_iota(jnp.int32, logits.shape, 1)
    lanef = lane.astype(F32)
    big = jnp.float32(1e9)
    ninf = jnp.float32(-jnp.inf)
    is_g = lane < N_GROUPS
    gl = jnp.where(is_g, logits, ninf)
    gmax = jnp.max(gl, -1, keepdims=True)
    gidx = jnp.min(jnp.where(gl == gmax, lanef, big), -1, keepdims=True)
    gw = 1.0 / jnp.sum(jnp.where(is_g, jnp.exp(logits - gmax), 0.0), -1, keepdims=True)
    ex = lane - N_GROUPS
    in_group = (ex >= 0) & (ex < N_EXP) & ((ex // EPG).astype(F32) == gidx)
    el = jnp.where(in_group, logits, ninf)
    v1 = jnp.max(el, -1, keepdims=True)
    i1 = jnp.min(jnp.where(el == v1, lanef, big), -1, keepdims=True)
    el2 = jnp.where(lanef == i1, ninf, el)
    v2 = jnp.max(el2, -1, keepdims=True)
    i2 = jnp.min(jnp.where(el2 == v2, lanef, big), -1, keepdims=True)
    e21 = jnp.exp(v2 - v1)
    w1 = gw / (1.0 + e21)
    w2 = gw * e21 / (1.0 + e21)
    e1 = i1 - N_GROUPS
    e2 = i2 - N_GROUPS

    oh1 = jnp.where(lanef == e1, 1.0, 0.0)
    oh2 = jnp.where(lanef == e2, 1.0, 0.0)
    ex1 = dot(tri_ref[...], oh1.astype(BF16))
    ex2 = dot(tri_ref[...], oh2.astype(BF16))
    col1 = jnp.sum(oh1, axis=0, keepdims=True)
    col2 = jnp.sum(oh2, axis=0, keepdims=True)
    n = col1 + col2
    run = jnp.floor((n + (BLK - 1.0)) * (1.0 / BLK)) * BLK
    start = dot(jnp.broadcast_to(run, (8, 128)).astype(BF16), upper_ref[...])[0:1]
    loc1 = jnp.sum(oh1 * (start + ex1), -1, keepdims=True)
    loc2 = jnp.sum(oh2 * (start + col1 + ex2), -1, keepdims=True)
    cnt_ref[...] = jnp.broadcast_to(n, cnt_ref.shape)

    route = jnp.zeros_like(logits)
    for j, val in enumerate((e1, e2, w1, w2, loc1, loc2)):
        route = jnp.where(lane == j, val, route)
    route_ref[...] = route

    loc1_row = jnp.transpose(jnp.broadcast_to(loc1, (TM, 128)))[0:1].astype(jnp.int32)
    loc2_row = jnp.transpose(jnp.broadcast_to(loc2, (TM, 128)))[0:1].astype(jnp.int32)
    slot = lax.broadcasted_iota(jnp.int32, (LOC, TM), 0)
    perm = jnp.where((slot == loc1_row) | (slot == loc2_row), 1.0, 0.0).astype(BF16)
    xloc_ref[...] = dot(perm, h_hi).astype(BF16)


def _out_proj(l, a, hyb_pair, att_pair, w_out, x_pair, norm, mod_l, g, b, w_route, b_route, tri, upper):
    tile = lambda n: pl.BlockSpec((TM, n), lambda i: (i, 0))
    return pl.pallas_call(
        functools.partial(_out_kernel, pre_norm=(l == 0)),
        out_shape=(jax.ShapeDtypeStruct((T, D), F32), jax.ShapeDtypeStruct((N_TILES * LOC, D), BF16),
                   jax.ShapeDtypeStruct((T, 128), F32), jax.ShapeDtypeStruct((N_TILES * 8, 128), F32)),
        grid=(N_TILES,),
        in_specs=[tile(GM_W)] + _pair_specs(HY_W) + _pair_specs(HEADS * VD)
        + [pl.BlockSpec((1, D, D), lambda i: (l, 0, 0))]
        + _pair_specs(D) + [
                  _full((1, D)), _full((1, D)),
                  _mod_spec(l),
                  _layer_spec(l, 1, D), _layer_spec(l, 1, D), _layer_spec(l, D, 128), _layer_spec(l, 1, 128),
                  _full((TM, TM)), _full((128, 128))],
        out_specs=(tile(D), pl.BlockSpec((LOC, D), lambda i: (i, 0)), tile(128),
                   pl.BlockSpec((8, 128), lambda i: (i, 0))),
        compiler_params=_cparams("parallel"),
        name="out_proj_route",
    )(a, *hyb_pair, *att_pair, w_out, *x_pair, *norm, mod_l, g, b, w_route, b_route, tri, upper)


X_RING = 8


def _expert_kernel(off_ref, cnt_ref, src_ref, x_hbm, wg_ref, wu_ref, wd_ref, y_hbm,
                   xbuf, ybuf, wgu_b, wd_b, sem, osem):
    e = pl.program_id(0)
    first_tile = off_ref[e]
    n_tiles = cnt_ref[e]
    total = off_ref[N_EXP - 1] + cnt_ref[N_EXP - 1]

    def block_copies(t, s):
        return [pltpu.make_async_copy(
            x_hbm.at[pl.ds(pl.multiple_of(src_ref[t * BLK_PER_TILE + b] * BLK, BLK), BLK), :],
            xbuf.at[s, pl.ds(b * BLK, BLK), :], sem.at[s]) for b in range(BLK_PER_TILE)]

    def out_copy(t, s):
        return pltpu.make_async_copy(ybuf.at[s], y_hbm.at[pl.ds(pl.multiple_of(t * TM, TM), TM), :],
                                     osem.at[s])

    @pl.when(e == 0)
    def _():
        for t0 in range(X_RING - 1):
            @pl.when(t0 < total)
            def _():
                for cp in block_copies(t0, t0):
                    cp.start()

    @pl.when(n_tiles > 0)
    def _():
        wgu_b[:, 0:D_EXP] = wg_ref[0, 0].astype(BF16)
        wgu_b[:, D_EXP:2 * D_EXP] = wu_ref[0, 0].astype(BF16)
        wd_b[...] = wd_ref[0, 0].astype(BF16)

    def tile_body(t, carry):
        s = t % 2
        xs = t % X_RING

        @pl.when(t + X_RING - 1 < total)
        def _():
            for cp in block_copies(t + X_RING - 1, (t + X_RING - 1) % X_RING):
                cp.start()

        for cp in block_copies(t, xs):
            cp.wait()

        @pl.when(t >= 2)
        def _():
            out_copy(t - 2, s).wait()

        gu = jnp.dot(xbuf[xs], wgu_b[...], preferred_element_type=F32)
        gate, up = gu[:, :D_EXP], gu[:, D_EXP:]
        hid = gate / (1.0 + jnp.exp(-gate)) * up
        ybuf[s] = jnp.dot(hid.astype(BF16), wd_b[...], preferred_element_type=F32).astype(BF16)
        out_copy(t, s).start()
        return carry

    lax.fori_loop(first_tile, first_tile + n_tiles, tile_body, 0)

    @pl.when(e == N_EXP - 1)
    def _():
        @pl.when(total >= 2)
        def _():
            out_copy(total - 2, total % 2).wait()

        out_copy(total - 1, (total - 1) % 2).wait()
        ybuf[0] = jnp.zeros((TM, D), BF16)

        def zero_start(t, carry):
            out_copy(t, 0).start()
            return carry

        def zero_wait(t, carry):
            out_copy(t, 0).wait()
            return carry

        lax.fori_loop(total, MOE_TILES, zero_start, 0)
        lax.fori_loop(total, MOE_TILES, zero_wait, 0)


def _experts(l, tables, xloc, w_gate, w_up, w_down):
    grid_spec = pltpu.PrefetchScalarGridSpec(
        num_scalar_prefetch=len(tables),
        grid=(N_EXP,),
        in_specs=[pl.BlockSpec(memory_space=pl.ANY),
                  pl.BlockSpec((1, 1, D, D_EXP), lambda e, *_: (l, e, 0, 0)),
                  pl.BlockSpec((1, 1, D, D_EXP), lambda e, *_: (l, e, 0, 0)),
                  pl.BlockSpec((1, 1, D_EXP, D), lambda e, *_: (l, e, 0, 0))],
        out_specs=pl.BlockSpec(memory_space=pl.ANY),
        scratch_shapes=[pltpu.VMEM((X_RING, TM, D), BF16), pltpu.VMEM((2, TM, D), BF16),
                        pltpu.VMEM((D, 2 * D_EXP), BF16), pltpu.VMEM((D_EXP, D), BF16),
                        pltpu.SemaphoreType.DMA((X_RING,)), pltpu.SemaphoreType.DMA((2,))],
    )
    return pl.pallas_call(
        _expert_kernel,
        out_shape=jax.ShapeDtypeStruct((MOE_ROWS, D), BF16),
        grid_spec=grid_spec,
        compiler_params=_cparams("arbitrary"),
        name="experts",
    )(*tables, xloc, w_gate, w_up, w_down)


def _combine_kernel(nblk_ref, gsrc_ref, y_hbm, route_ref, x1_ref, mod_ref, g_ref, b_ref, oc_ref, ol_ref,
                    ybuf, sem):
    i = pl.program_id(0)

    slot_i = i % 2

    def block_copy(t, s, lb):
        src = pl.multiple_of(gsrc_ref[t * LOC_BLKS + lb] * BLK, BLK)
        return pltpu.make_async_copy(y_hbm.at[pl.ds(src, BLK), :],
                                     ybuf.at[s, pl.ds(pl.multiple_of(lb * BLK, BLK), BLK), :], sem.at[s])

    def fetch(t, s):
        def start(lb, c):
            block_copy(t, s, lb).start()
            return c
        lax.fori_loop(0, nblk_ref[t], start, 0)

    @pl.when(i == 0)
    def _():
        ybuf[...] = jnp.zeros_like(ybuf)
        fetch(0, 0)

    @pl.when(i + 1 < N_TILES)
    def _():
        fetch(jnp.minimum(i + 1, N_TILES - 1), 1 - slot_i)

    def wait(lb, c):
        block_copy(i, slot_i, lb).wait()
        return c

    lax.fori_loop(0, nblk_ref[i], wait, 0)
    route = route_ref[...]
    yb = ybuf[slot_i]
    slot = lax.broadcasted_iota(jnp.int32, (TM, LOC), 1)
    pick = (jnp.where(slot == route[:, 4:5].astype(jnp.int32), route[:, 2:3], 0.0)
            + jnp.where(slot == route[:, 5:6].astype(jnp.int32), route[:, 3:4], 0.0)).astype(BF16)
    moe = jnp.dot(pick, yb, preferred_element_type=F32)
    m = mod_ref[0]
    res = _layer_norm(ALPHA * x1_ref[...] + m[5:6] * moe, g_ref[...], b_ref[...])

    @pl.when(i < CTX_TILES)
    def _():
        oc_ref[...] = res

    @pl.when(i >= CTX_TILES)
    def _():
        ol_ref[...] = res


def _combine(l, nblk, gsrc, y, route, x1, mod_l, g, b):
    out_shape = (jax.ShapeDtypeStruct((T_CTX, D), F32), jax.ShapeDtypeStruct((T_LAT, D), F32))
    out_specs = tuple(_pair_specs(D))
    grid_spec = pltpu.PrefetchScalarGridSpec(
        num_scalar_prefetch=2,
        grid=(N_TILES,),
        in_specs=[pl.BlockSpec(memory_space=pl.ANY),
                  pl.BlockSpec((TM, 128), lambda i, nb, gs: (i, 0)),
                  pl.BlockSpec((TM, D), lambda i, nb, gs: (i, 0)),
                  _mod_spec(l), _layer_spec(l, 1, D), _layer_spec(l, 1, D)],
        out_specs=out_specs,
        scratch_shapes=[pltpu.VMEM((2, LOC, D), BF16), pltpu.SemaphoreType.DMA((2,))],
    )
    return pl.pallas_call(
        _combine_kernel,
        out_shape=out_shape,
        grid_spec=grid_spec,
        compiler_params=_cparams("arbitrary"),
        name="moe_combine",
    )(nblk, gsrc, y, route, x1, mod_l, g, b)


def _prefix_pick(starts, query, table):
    delta = table - jnp.concatenate([jnp.zeros_like(table[..., :1]), table[..., :-1]], axis=-1)
    return jnp.sum(jnp.where(starts <= query, delta, 0), axis=-1)


def _dispatch_tables(cnt):
    i32 = jnp.int32
    run = (cnt + BLK - 1) // BLK * BLK
    loc_start = jnp.cumsum(run, axis=1) - run
    nblk_loc = jnp.sum(run, axis=1) // BLK
    seg_rows = jnp.sum(run, axis=0)
    seg_tiles = (seg_rows + TM - 1) // TM
    tile_end = jnp.cumsum(seg_tiles)
    seg_start = (tile_end - seg_tiles) * TM
    glob_start = seg_start[None, :] + jnp.cumsum(run, axis=0) - run

    g_blk = (glob_start.T.reshape(1, -1)) // BLK
    n_blk = (run.T.reshape(1, -1)) // BLK
    l_blk = ((jnp.arange(N_TILES, dtype=i32) * LOC)[None, :] + loc_start.T).reshape(1, -1) // BLK
    gb = jnp.arange(MOE_ROWS // BLK, dtype=i32)[:, None]
    off = gb[:, 0] - _prefix_pick(g_blk, gb, g_blk)
    src_blk = jnp.where(off < _prefix_pick(g_blk, gb, n_blk), _prefix_pick(g_blk, gb, l_blk) + off,
                        ZERO_BLK).astype(i32)

    pos = (jnp.arange(LOC_BLKS, dtype=i32) * BLK)[None, :, None]
    shift = _prefix_pick(loc_start[:, None, :], pos, (glob_start - loc_start)[:, None, :])
    gsrc = ((pos[:, :, 0] + shift) // BLK).astype(i32)
    expert_tables = ((tile_end - seg_tiles).astype(i32), seg_tiles.astype(i32), src_blk)
    return expert_tables, nblk_loc.astype(i32), gsrc.reshape(-1)


def _rope_slot(pe):
    return jnp.pad(pe, [(0, 0)] * (pe.ndim - 1) + [(NOPE, SLOT - NOPE - ROPE)])


def _w_in_layout(w):
    return jnp.concatenate([w[..., :1664], _rope_slot(w[..., 1664:])], axis=-1).astype(BF16)


def _wuq_layout(w):
    w = w.reshape(DEPTH, Q_LORA, HEADS, NOPE + ROPE)
    w = jnp.pad(w, ((0, 0), (0, 0), (0, 0), (0, SLOT - NOPE - ROPE)))
    return w.reshape(DEPTH, Q_LORA, HEADS * SLOT).astype(BF16)


def _wukv_layout(w):
    w = w.reshape(DEPTH, KV_LORA, HEADS, NOPE + VD)
    wk = jnp.pad(w[..., :NOPE], ((0, 0), (0, 0), (0, 0), (0, SLOT - NOPE)))
    wv = w[..., NOPE:].reshape(DEPTH, KV_LORA, HEADS * VD).astype(BF16)
    return wk.reshape(DEPTH, KV_LORA, HEADS * SLOT).astype(BF16), wv, jnp.swapaxes(wv, 1, 2)


def kernel(x_prompt, x_sample, c, cache_ckv, cache_kpe, c_ctx, ln_in_g, ln_in_b, w_mod, b_mod, w_in,
           gm_ln_g, gm_ln_b, gm_ws, gm_bs, hy_conv_w, hy_conv_b, hy_f_w1, hy_f_b1, hy_f_w2, hy_f_b2,
           hy_f_w3, hy_f_freq, hy_bias, mla_gq, mla_gkv, mla_wuq, mla_wukv, w_out, ln1_g, ln1_b,
           ln2_g, ln2_b, moe_w_gr, moe_b_gr, moe_w_er, moe_b_er, moe_w_gate, moe_w_up, moe_w_down):
    rope = _rope_tables()
    dft = {L: _dft_tables(L) for L in (CTX_L, LAT_L)}
    pos_tab = {L: _hyena_positions(L) for L in (CTX_L, LAT_L)}
    hd = np.arange(GM_W) // GM_HD
    avg = jnp.asarray((hd[:, None] == hd[None, :]) / GM_HD, BF16)
    tri = jnp.asarray(np.tril(np.ones((TM, TM)), -1), BF16)
    upper = jnp.asarray(np.triu(np.ones((128, 128)), 1), BF16)

    cond8 = jnp.concatenate([c_ctx[None], c, jnp.zeros((8 - 1 - N_LAT_B, D), F32)], axis=0)
    mod = _modulation(cond8, w_mod, b_mod).reshape(DEPTH, 8, 6, D)

    wuk_all, wuv_all, wuvt_all = _wukv_layout(mla_wukv)
    w_in_ext = _w_in_layout(w_in)
    wuq_arr = _wuq_layout(mla_wuq)
    w_out_b = w_out.astype(BF16)
    ws_b = gm_ws.astype(BF16)
    kc_all, vc_all = _cache_kv(cache_ckv, _rope_slot(cache_kpe), wuk_all, wuvt_all)

    x_pair = (x_prompt.reshape(T_CTX, D), x_sample.reshape(T_LAT, D))
    norm = (ln_in_g.reshape(1, D), ln_in_b.reshape(1, D))
    row_vec = lambda p: p.reshape(DEPTH, 1, p.shape[-1])
    bs_full = jnp.repeat(jnp.swapaxes(gm_bs, 1, 2), GM_HD, axis=2)
    w_route = jnp.pad(jnp.concatenate([moe_w_gr, moe_w_er.reshape(DEPTH, D, N_EXP)], axis=2),
                      ((0, 0), (0, 0), (0, 128 - N_GROUPS - N_EXP)))
    b_route = jnp.pad(jnp.concatenate([moe_b_gr, moe_b_er.reshape(DEPTH, N_EXP)], axis=1),
                      ((0, 0), (0, 128 - N_GROUPS - N_EXP))).reshape(DEPTH, 1, 128)
    hyena_groups = ((CTX_L, N_CTX_B, 4, 0), (LAT_L, N_LAT_B, 2, T_CTX // LAT_L))
    spectra = {L: _filter_spectra(L, *pos_tab[L], dft[L], hy_f_w1, hy_f_b1, hy_f_w2, hy_f_b2, hy_f_w3,
                                  hy_f_freq) for L, _, _, _ in hyena_groups}
    ckv_states, kpe_states = [], []
    for l in range(DEPTH):
        a, hy, q, k, v, vt, ckv, kpe = _in_proj(
            l, x_pair, norm, mod, w_in_ext, row_vec(gm_ln_g), row_vec(gm_ln_b), avg,
            ws_b, bs_full, row_vec(mla_gq), row_vec(mla_gkv), wuq_arr, wuk_all, wuv_all, wuvt_all, rope)
        ckv_states.append(ckv[:T_CTX].reshape(N_CTX_B, CTX_L, KV_LORA))
        kpe_states.append(kpe[:T_CTX, NOPE:NOPE + ROPE].reshape(N_CTX_B, CTX_L, ROPE))

        hyb = [_hyena_group(l, hy, L, nb, group, blk0, dft[L], spectra[L], hy_conv_w, hy_conv_b, hy_bias)
               for L, nb, group, blk0 in hyena_groups]

        att = _attention(q, k, v, vt, kc_all[l], vc_all[l])

        x1, xloc, route, counts = _out_proj(l, a, hyb, att, w_out_b, x_pair, norm, mod,
                                            row_vec(ln1_g), row_vec(ln1_b), w_route, b_route, tri, upper)
        cnt = counts.reshape(N_TILES, 8, 128)[:, 0, :N_EXP].astype(jnp.int32)
        expert_tables, nblk_loc, gsrc = _dispatch_tables(cnt)
        y = _experts(l, expert_tables, xloc, moe_w_gate, moe_w_up, moe_w_down)
        x_pair = _combine(l, nblk_loc, gsrc, y, route, x1, mod, row_vec(ln2_g), row_vec(ln2_b))

    y_prompt = x_pair[0].reshape(N_CTX_B, CTX_L, D)
    y_sample = x_pair[1].reshape(N_LAT_B, LAT_L, D)
    return (y_prompt, y_sample, jnp.stack(ckv_states, axis=1), jnp.stack(kpe_states, axis=1))
```

```python
import functools
import math

import numpy as np
import jax
import jax.numpy as jnp
from jax import lax
from jax.experimental import pallas as pl
from jax.experimental.pallas import tpu as pltpu

D = 1024
N_CTX_B, CTX_L = 16, 256
N_LAT_B, LAT_L = 4, 1024
DEPTH = 2
T_CTX = N_CTX_B * CTX_L
T_LAT = N_LAT_B * LAT_L
T = T_CTX + T_LAT
PAST = 256
GRID_W = 64

GM_HEADS, GM_HD, GM_W, CHUNK = 4, 64, 256, 128
HY_W, HY_EMB, HY_FH = 256, 33, 64
NOPE, ROPE, VD, HEADS = 64, 32, 64, 8
Q_LORA, KV_LORA = 256, 128
SLOT = 128
N_GROUPS, EPG, N_EXP, D_EXP = 4, 8, 32, 256
ALPHA = (2.0 * DEPTH) ** 0.25
LN_EPS, RMS_EPS = 1e-5, 1e-6
ROPE_THETA = 10000.0

TM = 256
N_TILES = T // TM
CTX_TILES = T_CTX // TM
LAT_TILES_PER_B = LAT_L // TM
N_ASSIGN = 2 * T
BLK = 16
BLK_PER_TILE = TM // BLK
LOC = 2 * TM + N_EXP * BLK
LOC_BLKS = LOC // BLK
ZERO_BLK = LOC_BLKS - 1
MOE_TILES = (N_ASSIGN + N_TILES * N_EXP * (BLK - 1)) // TM + N_EXP
MOE_ROWS = MOE_TILES * TM
IN_COLS = 1696

F32, BF16 = jnp.float32, jnp.bfloat16
_NT = (((1,), (1,)), ((), ()))
VMEM_LIMIT = 52 * 1024 * 1024


def _cparams(*sem):
    return pltpu.CompilerParams(dimension_semantics=sem, vmem_limit_bytes=VMEM_LIMIT)


def _cond_of_tile(i):
    return jnp.where(i < CTX_TILES, 0, 1 + (i - CTX_TILES) // LAT_TILES_PER_B)


def _rope_block_of_tile(i):
    return jnp.where(i < CTX_TILES, 0, 1 + (i - CTX_TILES) % LAT_TILES_PER_B)


def _full(shape):
    n = len(shape)
    return pl.BlockSpec(shape, lambda *_: (0,) * n)


def _layer_spec(l, *shape):
    return pl.BlockSpec((None,) + shape, lambda *_: (l,) + (0,) * len(shape))


def _mod_spec(l):
    return pl.BlockSpec((None, 1, 6, D), lambda i, *_: (l, _cond_of_tile(i), 0, 0))


def _pair_specs(n):
    return [pl.BlockSpec((TM, n), lambda i, *_: (jnp.minimum(i, CTX_TILES - 1), 0)),
            pl.BlockSpec((TM, n), lambda i, *_: (jnp.maximum(i - CTX_TILES, 0), 0))]


def _pair_read(c_ref, l_ref):
    return jnp.where(pl.program_id(0) < CTX_TILES, c_ref[...], l_ref[...])


def _layer_norm(x, g, b):
    mu = jnp.mean(x, -1, keepdims=True)
    xc = x - mu
    var = jnp.mean(xc * xc, -1, keepdims=True)
    return xc * lax.rsqrt(var + LN_EPS) * g + b


def _rms_norm(x, g):
    return x * lax.rsqrt(jnp.mean(x * x, -1, keepdims=True) + RMS_EPS) * g


def _bdot(a, b):
    return jnp.dot(a.astype(BF16), b.astype(BF16), preferred_element_type=F32)


def _dot3(x, w):
    x_hi = x.astype(BF16)
    x_lo = (x - x_hi.astype(F32)).astype(BF16)
    w_hi = w.astype(BF16)
    w_lo = (w - w_hi.astype(F32)).astype(BF16)
    dot = functools.partial(jnp.dot, preferred_element_type=F32)
    return dot(x_hi, w_hi) + dot(x_hi, w_lo) + dot(x_lo, w_hi)


def _swap_pair_lanes(x):
    n = x.shape[1]
    lane = lax.broadcasted_iota(jnp.int32, x.shape, 1)
    return jnp.where(lane % 2 == 0, pltpu.roll(x, n - 1, 1), pltpu.roll(x, 1, 1))


def _split2_dot(x, w_bf16):
    hi = x.astype(BF16)
    lo = (x - hi.astype(F32)).astype(BF16)
    dot = functools.partial(jnp.dot, preferred_element_type=F32)
    return dot(hi, w_bf16) + dot(lo, w_bf16)


def _dft_tables(L):
    f = np.arange(L, dtype=np.int64)
    ft = np.outer(f, f) % (2 * L)
    ang = np.pi * ft / L
    c = np.cos(ang)
    s = np.sin(ang)
    alt = np.where(f % 2 == 0, 1.0, -1.0)
    sf = s.copy()
    sf[0, :] = alt
    return (jnp.asarray(c, F32).astype(BF16), jnp.asarray(sf, F32).astype(BF16),
            jnp.asarray(sf.T.copy(), F32).astype(BF16), jnp.asarray(alt[:, None], F32))


def _rope_tables():
    rows = LAT_L // GRID_W
    row = np.repeat(np.arange(rows, dtype=np.float64), GRID_W)
    col = np.tile(np.arange(GRID_W, dtype=np.float64), rows)
    n = ROPE // 4
    inv = ROPE_THETA ** (-np.arange(n, dtype=np.float64) / n)
    ang = np.concatenate([row[:, None] * inv, col[:, None] * inv], -1)
    cos = np.zeros((CTX_L + LAT_L, SLOT))
    sin = np.zeros((CTX_L + LAT_L, SLOT))
    cos[:, :NOPE + ROPE] = 1.0
    cos[CTX_L:, NOPE:NOPE + ROPE:2] = np.cos(ang)
    cos[CTX_L:, NOPE + 1:NOPE + ROPE:2] = np.cos(ang)
    sin[CTX_L:, NOPE:NOPE + ROPE:2] = -np.sin(ang)
    sin[CTX_L:, NOPE + 1:NOPE + ROPE:2] = np.sin(ang)
    scale = math.log2(math.e) / math.sqrt(NOPE + ROPE)
    return (jnp.asarray(cos * scale, F32), jnp.asarray(sin * scale, F32),
            jnp.asarray(cos, F32), jnp.asarray(sin, F32))


def _hyena_positions(L):
    t = jnp.linspace(0.0, 1.0, L, dtype=F32)[:, None]
    bands = (HY_EMB - 1) // 2
    w = 2.0 * math.pi * jnp.arange(L, dtype=F32)[:, None] / L
    f = jnp.linspace(1e-4, bands - 1, bands, dtype=F32)[None]
    z = jnp.concatenate([t, jnp.cos(f * w), -jnp.sin(f * w)], -1)
    z = jnp.pad(z, ((0, 0), (0, 128 - HY_EMB)))
    min_decay = math.log(1e-2) / 1.5
    max_decay = math.log(1e-2) / 0.3
    deltas = jnp.abs(jnp.linspace(min_decay, max_decay, HY_W, dtype=F32))
    return z, jnp.exp(-t * deltas)


def _mod_kernel(c_ref, w_ref, b_ref, o_ref):
    c = c_ref[...]
    s = c / (1.0 + jnp.exp(-c))
    o_ref[0] = _bdot(s, w_ref[0]) + b_ref[0]


def _modulation(cond8, w_mod, b_mod):
    tn = 1536
    return pl.pallas_call(
        _mod_kernel,
        out_shape=jax.ShapeDtypeStruct((DEPTH, 8, 6 * D), F32),
        grid=(DEPTH, 6 * D // tn),
        in_specs=[pl.BlockSpec((8, D), lambda l, j: (0, 0)),
                  pl.BlockSpec((1, D, tn), lambda l, j: (l, 0, j)),
                  pl.BlockSpec((1, 1, tn), lambda l, j: (l, 0, j))],
        out_specs=pl.BlockSpec((1, 8, tn), lambda l, j: (l, 0, j)),
        compiler_params=_cparams("parallel", "parallel"),
        name="modulation",
    )(cond8, w_mod, b_mod.reshape(DEPTH, 1, 6 * D))


def _filter_kernel(z_ref, w1_ref, b1_ref, w2_ref, b2_ref, w3_ref, fr_ref, dec_ref, c_ref, sf_ref,
                   alt_ref, kr_ref, ki_ref, krn_ref, *, L):
    h1 = jnp.sin(fr_ref[0:1] * (_dot3(z_ref[...], w1_ref[...]) + b1_ref[...]))
    h2 = jnp.sin(fr_ref[1:2] * (_dot3(h1, w2_ref[...]) + b2_ref[...]))
    filt = _dot3(h2, w3_ref[...])
    row = lax.broadcasted_iota(jnp.int32, (L, HY_W), 0)
    dec = dec_ref[...]
    n = DEPTH * 2
    fwd = [filt[:, j * 2 * HY_W:j * 2 * HY_W + HY_W] * dec for j in range(n)]
    bwd = [jnp.where(row == 0, 0.0, filt[:, j * 2 * HY_W + HY_W:(j + 1) * 2 * HY_W] * dec) for j in range(n)]
    sm = [f + b for f, b in zip(fwd, bwd)]
    kr_all = _bdot(c_ref[...], jnp.concatenate(sm, axis=1))
    ki_all = -_bdot(sf_ref[...], jnp.concatenate([f - b for f, b in zip(fwd, bwd)], axis=1))
    for j in range(n):
        l, o = divmod(j, 2)
        kr = kr_all[:, j * HY_W:(j + 1) * HY_W]
        kr_nyq = jnp.sum(alt_ref[...] * sm[j], axis=0, keepdims=True)
        kr_ref[l, o] = jnp.where(row == 0, kr * (0.5 / L), kr * (1.0 / L))
        ki_ref[l, o] = jnp.where(row == 0, 0.0, ki_all[:, j * HY_W:(j + 1) * HY_W] * (1.0 / L))
        krn_ref[l, o] = jnp.where(row == 0, kr_nyq * (0.5 / L), kr * (1.0 / L))


def _block_diag(blocks):
    rows = []
    for i, blk in enumerate(blocks):
        rows.append(jnp.concatenate([blk if j == i else jnp.zeros((blk.shape[0], other.shape[1]), blk.dtype)
                                     for j, other in enumerate(blocks)], axis=1))
    return jnp.concatenate(rows, axis=0)


def _filter_spectra(L, z, dec, dft, w1, b1, w2, b2, w3, freq):
    assert DEPTH * HY_FH == 128
    c, sf, _, alt = dft
    layers = range(DEPTH)
    w1c = jnp.pad(jnp.concatenate([w1[l] for l in layers], axis=1), ((0, 128 - HY_EMB), (0, 0)))
    w2c = _block_diag([w2[l] for l in layers])
    w3c = _block_diag([w3[l] for l in layers])
    b1c, b2c = b1.reshape(1, 128), b2.reshape(1, 128)
    frc = jnp.swapaxes(freq, 0, 1).reshape(2, 128)
    shp = jax.ShapeDtypeStruct((DEPTH, 2, L, HY_W), F32)
    args = (z, w1c, b1c, w2c, b2c, w3c, frc, dec, c, sf, alt)
    return pl.pallas_call(
        functools.partial(_filter_kernel, L=L),
        out_shape=(shp, shp, shp),
        grid=(1,),
        in_specs=[_full(a.shape) for a in args],
        out_specs=(_full((DEPTH, 2, L, HY_W)),) * 3,
        compiler_params=_cparams("arbitrary"),
        name=f"hyena_filter_{L}",
    )(*args)


def _in_kernel(xc_ref, xl_ref, ng_ref, nb_ref, mod_ref, wt_ref, gg_ref, gb_ref, avg_ref, ws_ref, bs_ref,
               gq_ref, gkv_ref, wuq_ref, wuk_ref, wuv_ref, wuvt_ref, cq_ref, sq_ref, ck_ref, sk_ref,
               a_ref, hy_ref, q_ref, k_ref, v_ref, vt_ref, ckv_ref, kpe_ref, wt_b, *, pre_norm):
    @pl.when(pl.program_id(0) == 0)
    def _():
        wt_b[...] = wt_ref[...].astype(BF16)

    x = _pair_read(xc_ref, xl_ref)
    if pre_norm:
        x = _layer_norm(x, ng_ref[...], nb_ref[...])
    m = mod_ref[0]
    h = x * (1.0 + m[1:2]) + m[0:1]
    hb = h.astype(BF16)
    proj = lax.dot_general(hb, wt_b[0:1664, :], _NT, preferred_element_type=F32)

    gu = jax.nn.gelu(proj[:, 0:GM_W], approximate=True)
    gv = jax.nn.gelu(proj[:, GM_W:2 * GM_W], approximate=True)
    avg = avg_ref[...]
    mu = _split2_dot(gv, avg)
    vc = gv - mu
    var = _split2_dot(vc * vc, avg)
    vln = (vc * lax.rsqrt(var + LN_EPS) * gg_ref[...] + gb_ref[...]).astype(BF16)
    lane = lax.broadcasted_iota(jnp.int32, (CHUNK, GM_W), 1)
    for c in range(TM // CHUNK):
        vchunk = vln[c * CHUNK:(c + 1) * CHUNK]
        s = bs_ref[...]
        for hd in range(GM_HEADS):
            sh = jnp.dot(ws_ref[0, hd], vchunk, preferred_element_type=F32)
            s = s + jnp.where(lane // GM_HD == hd, sh, 0.0)
        a_ref[c * CHUNK:(c + 1) * CHUNK, :] = (gu[c * CHUNK:(c + 1) * CHUNK] * s).astype(BF16)

    hy_ref[...] = proj[:, 512:1280]

    cq = _rms_norm(proj[:, 1280:1536], gq_ref[...])
    qq = _bdot(cq, wuq_ref[0])
    cos_q = jnp.concatenate([cq_ref[...]] * HEADS, axis=1)
    sin_q = jnp.concatenate([sq_ref[...]] * HEADS, axis=1)
    q_ref[...] = (qq * cos_q + _swap_pair_lanes(qq) * sin_q).astype(BF16)

    ckv = _rms_norm(proj[:, 1536:1664], gkv_ref[...])
    ckv_ref[...] = ckv
    kpe = jnp.concatenate([jnp.zeros((TM, NOPE), F32),
                           lax.dot_general(hb, wt_b[1664:IN_COLS, :], _NT, preferred_element_type=F32),
                           jnp.zeros((TM, SLOT - NOPE - ROPE), F32)], axis=1)
    kpe_ref[...] = kpe
    krot = kpe * ck_ref[...] + _swap_pair_lanes(kpe) * sk_ref[...]
    kn = _bdot(ckv, wuk_ref[0])
    k_ref[...] = (kn + jnp.concatenate([krot] * HEADS, axis=1)).astype(BF16)
    v_ref[...] = _bdot(ckv, wuv_ref[0]).astype(BF16)
    vt_ref[...] = lax.dot_general(wuvt_ref[0], ckv.astype(BF16), _NT,
                                  preferred_element_type=F32).astype(BF16)


def _in_proj(l, x_pair, norm, mod_l, w_in_t, gm_g, gm_b, avg, ws, bs_full, gq, gkv, wuq_arr,
             wuk_arr, wuv_arr, wuvt_arr, rope):
    layer = lambda *shape: pl.BlockSpec((1,) + shape, lambda i: (l,) + (0,) * len(shape))
    cos_q, sin_q, cos_k, sin_k = rope
    tile = lambda n: pl.BlockSpec((TM, n), lambda i: (i, 0))
    rope_spec = pl.BlockSpec((TM, SLOT), lambda i: (_rope_block_of_tile(i), 0))
    out_shapes = (jax.ShapeDtypeStruct((T, GM_W), BF16),
                  jax.ShapeDtypeStruct((T, 3 * HY_W), F32),
                  jax.ShapeDtypeStruct((T, HEADS * SLOT), BF16),
                  jax.ShapeDtypeStruct((T, HEADS * SLOT), BF16),
                  jax.ShapeDtypeStruct((T, HEADS * VD), BF16),
                  jax.ShapeDtypeStruct((HEADS * VD, T), BF16),
                  jax.ShapeDtypeStruct((T, KV_LORA), F32),
                  jax.ShapeDtypeStruct((T, SLOT), F32))
    return pl.pallas_call(
        functools.partial(_in_kernel, pre_norm=(l == 0)),
        out_shape=out_shapes,
        grid=(N_TILES,),
        in_specs=_pair_specs(D) + [
                  _full((1, D)), _full((1, D)),
                  _mod_spec(l),
                  pl.BlockSpec((None, IN_COLS, D), lambda i: (l, 0, 0), pipeline_mode=pl.Buffered(1)),
                  _layer_spec(l, 1, GM_W), _layer_spec(l, 1, GM_W), _full((GM_W, GM_W)),
                  layer(GM_HEADS, CHUNK, CHUNK), _layer_spec(l, CHUNK, GM_W),
                  _layer_spec(l, 1, Q_LORA), _layer_spec(l, 1, KV_LORA),
                  layer(Q_LORA, HEADS * SLOT), layer(KV_LORA, HEADS * SLOT),
                  layer(KV_LORA, HEADS * VD), layer(HEADS * VD, KV_LORA),
                  rope_spec, rope_spec, rope_spec, rope_spec],
        out_specs=(tile(GM_W), tile(3 * HY_W), tile(HEADS * SLOT), tile(HEADS * SLOT), tile(HEADS * VD),
                   pl.BlockSpec((HEADS * VD, TM), lambda i: (0, i)), tile(KV_LORA), tile(SLOT)),
        scratch_shapes=[pltpu.VMEM((IN_COLS, D), BF16)],
        compiler_params=_cparams("arbitrary"),
        name="in_proj",
    )(*x_pair, *norm, mod_l, w_in_t, gm_g, gm_b, avg, ws, bs_full, gq, gkv, wuq_arr, wuk_arr, wuv_arr,
      wuvt_arr, cos_q, sin_q, cos_k, sin_k)


def _cache_kernel(ckv_ref, kpe_ref, wuk_ref, wuv_ref, k_ref, v_ref):
    ckv = ckv_ref[0, 0]
    kn = _bdot(ckv, wuk_ref[0])
    k_ref[0, 0] = (kn + jnp.concatenate([kpe_ref[0, 0]] * HEADS, axis=1)).astype(BF16)
    v_ref[0, 0] = lax.dot_general(wuv_ref[0], ckv.astype(BF16), _NT,
                                  preferred_element_type=F32).astype(BF16)


def _cache_kv(cache_ckv, cache_kpe_slot, wuk_arr, wuv_arr):
    return pl.pallas_call(
        _cache_kernel,
        out_shape=(jax.ShapeDtypeStruct((DEPTH, N_LAT_B, PAST, HEADS * SLOT), BF16),
                   jax.ShapeDtypeStruct((DEPTH, N_LAT_B, HEADS * VD, PAST), BF16)),
        grid=(DEPTH, N_LAT_B),
        in_specs=[pl.BlockSpec((1, 1, PAST, KV_LORA), lambda l, b: (b, l, 0, 0)),
                  pl.BlockSpec((1, 1, PAST, SLOT), lambda l, b: (b, l, 0, 0)),
                  pl.BlockSpec((1, KV_LORA, HEADS * SLOT), lambda l, b: (l, 0, 0)),
                  pl.BlockSpec((1, HEADS * VD, KV_LORA), lambda l, b: (l, 0, 0))],
        out_specs=(pl.BlockSpec((1, 1, PAST, HEADS * SLOT), lambda l, b: (l, b, 0, 0)),
                   pl.BlockSpec((1, 1, HEADS * VD, PAST), lambda l, b: (l, b, 0, 0))),
        compiler_params=_cparams("parallel", "parallel"),
        name="cache_kv",
    )(cache_ckv, cache_kpe_slot, wuk_arr, wuv_arr)


def _attend_rows(q_ref, k_ref, v_ref, o_ref):
    lq = q_ref.shape[0]
    lane = lax.broadcasted_iota(jnp.int32, (lq, 2 * VD), 1)
    for pair in range(HEADS // 2):
        outs = []
        for hd in (2 * pair, 2 * pair + 1):
            s = lax.dot_general(q_ref[:, hd * SLOT:(hd + 1) * SLOT], k_ref[:, hd * SLOT:(hd + 1) * SLOT], _NT,
                                preferred_element_type=F32)
            p = jnp.exp2(s - jnp.max(s, -1, keepdims=True))
            acc = jnp.dot(p.astype(BF16), v_ref[:, pair * 2 * VD:(pair + 1) * 2 * VD],
                          preferred_element_type=F32)
            outs.append(acc / jnp.sum(p, -1, keepdims=True))
        o_ref[:, pair * 2 * VD:(pair + 1) * 2 * VD] = jnp.where(lane < VD, outs[0], outs[1]).astype(BF16)


def _attend_cols(q_ref, segments, o_ref):
    lq = q_ref.shape[0]
    lane = lax.broadcasted_iota(jnp.int32, (lq, 2 * SLOT), 1)
    row = lax.broadcasted_iota(jnp.int32, (2 * VD, lq), 0)
    outs = []
    for pair in range(HEADS // 2):
        qp = q_ref[:, pair * 2 * SLOT:(pair + 1) * 2 * SLOT]
        zero = jnp.zeros_like(qp)
        q_bd = jnp.concatenate([jnp.where(lane < SLOT, qp, zero), jnp.where(lane < SLOT, zero, qp)], axis=0)
        scores = [lax.dot_general(k_ref[:, pair * 2 * SLOT:(pair + 1) * 2 * SLOT], q_bd, _NT,
                                  preferred_element_type=F32) for k_ref, _ in segments]
        mx = functools.reduce(jnp.maximum, [jnp.max(s, 0, keepdims=True) for s in scores])
        ps = [jnp.exp2(s - mx) for s in scores]
        den = functools.reduce(jnp.add, [jnp.sum(p, 0, keepdims=True) for p in ps])
        acc = functools.reduce(jnp.add, [
            jnp.dot(vt_ref[pair * 2 * VD:(pair + 1) * 2 * VD, :], p.astype(BF16),
                    preferred_element_type=F32) for p, (_, vt_ref) in zip(ps, segments)])
        acc = acc / den
        outs.append(jnp.where(row < VD, acc[:, :lq], acc[:, lq:]))
    o_ref[...] = jnp.transpose(jnp.concatenate(outs, axis=0)).astype(BF16)


def _attn_ctx_kernel(q_ref, k_ref, v_ref, o_ref):
    _attend_rows(q_ref, k_ref, v_ref, o_ref)


def _attn_lat_kernel(q_ref, kc_ref, vc_ref, k_ref, vt_ref, o_ref):
    _attend_cols(q_ref, [(kc_ref.at[0], vc_ref.at[0]), (k_ref, vt_ref)], o_ref)


def _attention(q, k, v, vt, kc, vc):
    kw, vw = HEADS * SLOT, HEADS * VD
    ctx = pl.pallas_call(
        _attn_ctx_kernel,
        out_shape=jax.ShapeDtypeStruct((T_CTX, vw), BF16),
        grid=(N_CTX_B,),
        in_specs=[pl.BlockSpec((CTX_L, kw), lambda b: (b, 0)),
                  pl.BlockSpec((CTX_L, kw), lambda b: (b, 0)),
                  pl.BlockSpec((CTX_L, vw), lambda b: (b, 0))],
        out_specs=pl.BlockSpec((CTX_L, vw), lambda b: (b, 0)),
        compiler_params=_cparams("parallel"),
        name="attn_ctx",
    )(q, k, v)
    nq = LAT_L // TM
    off = T_CTX // LAT_L
    lat = pl.pallas_call(
        _attn_lat_kernel,
        out_shape=jax.ShapeDtypeStruct((T_LAT, vw), BF16),
        grid=(N_LAT_B, nq),
        in_specs=[pl.BlockSpec((TM, kw), lambda b, j: (CTX_TILES + b * nq + j, 0)),
                  pl.BlockSpec((1, PAST, kw), lambda b, j: (b, 0, 0)),
                  pl.BlockSpec((1, vw, PAST), lambda b, j: (b, 0, 0)),
                  pl.BlockSpec((LAT_L, kw), lambda b, j: (off + b, 0)),
                  pl.BlockSpec((vw, LAT_L), lambda b, j: (0, off + b))],
        out_specs=pl.BlockSpec((TM, vw), lambda b, j: (b * nq + j, 0)),
        compiler_params=_cparams("parallel", "parallel"),
        name="attn_lat",
    )(q, kc, vc, k, vt)
    return ctx, lat


def _hyena_kernel(hy_ref, cw_ref, cb_ref, c_ref, sf_ref, sb_ref, kr_ref, ki_ref, krn_ref, hb_ref,
                  o_ref, *, L, group):
    row = lax.broadcasted_iota(jnp.int32, (L, 3 * HY_W), 0)
    zs = []
    for j in range(group):
        x = hy_ref[j * L:(j + 1) * L, :]
        prev = jnp.where(row == 0, 0.0, pltpu.roll(x, 1, 0))
        nxt = jnp.where(row == L - 1, 0.0, pltpu.roll(x, L - 1, 0))
        zs.append(prev * cw_ref[0:1] + x * cw_ref[1:2] + nxt * cw_ref[2:3] + cb_ref[...])
    side = lambda parts: jnp.concatenate(parts, axis=1)
    y = side([z[:, 2 * HY_W:] for z in zs])
    for o in range(2):
        gate = side([z[:, o * HY_W:(o + 1) * HY_W] for z in zs])
        kr, ki, krn = (side([r[o]] * group) for r in (kr_ref, ki_ref, krn_ref))
        yb = y.astype(BF16)
        a_re = jnp.dot(c_ref[...], yb, preferred_element_type=F32)
        a_im = jnp.dot(sf_ref[...], yb, preferred_element_type=F32)
        z_re = a_re * kr + a_im * ki
        z_im = a_im * krn - a_re * ki
        conv = (jnp.dot(c_ref[...], z_re.astype(BF16), preferred_element_type=F32)
                + jnp.dot(sb_ref[...], z_im.astype(BF16), preferred_element_type=F32))
        y = gate * (conv + y * side([hb_ref[o:o + 1]] * group))
    for j in range(group):
        o_ref[j * L:(j + 1) * L, :] = y[:, j * HY_W:(j + 1) * HY_W].astype(BF16)


def _hyena_group(l, hy, L, nb, group, blk0, dft, spectra, conv_w, conv_b, hy_bias):
    c, sf, sb, _ = dft
    kr, ki, krn = spectra
    const = lambda shape: pl.BlockSpec(shape, lambda b: (0,) * len(shape), pipeline_mode=pl.Buffered(1))
    spec = pl.BlockSpec((None, 2, L, HY_W), lambda b: (l, 0, 0, 0), pipeline_mode=pl.Buffered(1))
    return pl.pallas_call(
        functools.partial(_hyena_kernel, L=L, group=group),
        out_shape=jax.ShapeDtypeStruct((nb * L, HY_W), BF16),
        grid=(nb // group,),
        in_specs=[pl.BlockSpec((group * L, 3 * HY_W), lambda b: (blk0 // group + b, 0)),
                  _layer_spec(l, 3, 3 * HY_W), _layer_spec(l, 1, 3 * HY_W),
                  const((L, L)), const((L, L)), const((L, L)), spec, spec, spec,
                  _layer_spec(l, 2, HY_W)],
        out_specs=pl.BlockSpec((group * L, HY_W), lambda b: (b, 0)),
        compiler_params=_cparams("parallel"),
        name=f"hyena_{L}",
    )(hy, conv_w, conv_b.reshape(DEPTH, 1, 3 * HY_W), c, sf, sb, kr, ki, krn, hy_bias)


def _out_kernel(a_ref, bc_ref, bl_ref, mc_ref, ml_ref, w_ref, xc_ref, xl_ref, ng_ref, nb_ref, mod_ref,
                g_ref, be_ref, wr_ref, br_ref, tri_ref, upper_ref, x1_ref, xloc_ref, route_ref, cnt_ref,
                w_b, *, pre_norm):
    @pl.when(pl.program_id(0) == 0)
    def _():
        w_b[...] = w_ref[...].astype(BF16)

    dot = functools.partial(jnp.dot, preferred_element_type=F32)
    mixed = (dot(a_ref[...], w_b[0:GM_W]) + dot(_pair_read(bc_ref, bl_ref), w_b[GM_W:GM_W + HY_W])
             + dot(_pair_read(mc_ref, ml_ref), w_b[GM_W + HY_W:]))
    x = _pair_read(xc_ref, xl_ref)
    if pre_norm:
        x = _layer_norm(x, ng_ref[...], nb_ref[...])
    m = mod_ref[0]
    x1 = _layer_norm(ALPHA * x + m[2:3] * mixed, g_ref[...], be_ref[...])
    x1_ref[...] = x1
    h2 = x1 * (1.0 + m[4:5]) + m[3:4]

    h_hi = h2.astype(BF16)
    h_lo = (h2 - h_hi.astype(F32)).astype(BF16)
    w_hi = wr_ref[...].astype(BF16)
    w_lo = (wr_ref[...] - w_hi.astype(F32)).astype(BF16)
    part = dot(h_hi, jnp.concatenate([w_hi, w_lo], axis=1))
    logits = part[:, :128] + part[:, 128:] + dot(h_lo, w_hi) + br_ref[...]
    lane = lax.broadcasted_iota(jnp.int32, logits.shape, 1)
    lanef = lane.astype(F32)
    big = jnp.float32(1e9)
    ninf = jnp.float32(-jnp.inf)
    is_g = lane < N_GROUPS
    gl = jnp.where(is_g, logits, ninf)
    gmax = jnp.max(gl, -1, keepdims=True)
    gidx = jnp.min(jnp.where(gl == gmax, lanef, big), -1, keepdims=True)
    gw = 1.0 / jnp.sum(jnp.where(is_g, jnp.exp(logits - gmax), 0.0), -1, keepdims=True)
    ex = lane - N_GROUPS
    in_group = (ex >= 0) & (ex < N_EXP) & ((ex // EPG).astype(F32) == gidx)
    el = jnp.where(in_group, logits, ninf)
    v1 = jnp.max(el, -1, keepdims=True)
    i1 = jnp.min(jnp.where(el == v1, lanef, big), -1, keepdims=True)
    el2 = jnp.where(lanef == i1, ninf, el)
    v2 = jnp.max(el2, -1, keepdims=True)
    i2 = jnp.min(jnp.where(el2 == v2, lanef, big), -1, keepdims=True)
    e21 = jnp.exp(v2 - v1)
    w1 = gw / (1.0 + e21)
    w2 = gw * e21 / (1.0 + e21)
    e1 = i1 - N_GROUPS
    e2 = i2 - N_GROUPS

    oh1 = jnp.where(lanef == e1, 1.0, 0.0)
    oh2 = jnp.where(lanef == e2, 1.0, 0.0)
    ex1 = dot(tri_ref[...], oh1.astype(BF16))
    ex2 = dot(tri_ref[...], oh2.astype(BF16))
    col1 = jnp.sum(oh1, axis=0, keepdims=True)
    col2 = jnp.sum(oh2, axis=0, keepdims=True)
    n = col1 + col2
    run = jnp.floor((n + (BLK - 1.0)) * (1.0 / BLK)) * BLK
    start = dot(jnp.broadcast_to(run, (8, 128)).astype(BF16), upper_ref[...])[0:1]
    loc1 = jnp.sum(oh1 * (start + ex1), -1, keepdims=True)
    loc2 = jnp.sum(oh2 * (start + col1 + ex2), -1, keepdims=True)
    cnt_ref[...] = jnp.broadcast_to(n, cnt_ref.shape)

    route = jnp.zeros_like(logits)
    for j, val in enumerate((e1, e2, w1, w2, loc1, loc2)):
        route = jnp.where(lane == j, val, route)
    route_ref[...] = route

    loc1_row = jnp.transpose(jnp.broadcast_to(loc1, (TM, 128)))[0:1].astype(jnp.int32)
    loc2_row = jnp.transpose(jnp.broadcast_to(loc2, (TM, 128)))[0:1].astype(jnp.int32)
    slot = lax.broadcasted_iota(jnp.int32, (LOC, TM), 0)
    perm = jnp.where((slot == loc1_row) | (slot == loc2_row), 1.0, 0.0).astype(BF16)
    xloc_ref[...] = dot(perm, h_hi).astype(BF16)


def _out_proj(l, a, hyb_pair, att_pair, w_out, x_pair, norm, mod_l, g, b, w_route, b_route, tri, upper):
    tile = lambda n: pl.BlockSpec((TM, n), lambda i: (i, 0))
    return pl.pallas_call(
        functools.partial(_out_kernel, pre_norm=(l == 0)),
        out_shape=(jax.ShapeDtypeStruct((T, D), F32), jax.ShapeDtypeStruct((N_TILES * LOC, D), BF16),
                   jax.ShapeDtypeStruct((T, 128), F32), jax.ShapeDtypeStruct((N_TILES * 8, 128), F32)),
        grid=(N_TILES,),
        in_specs=[tile(GM_W)] + _pair_specs(HY_W) + _pair_specs(HEADS * VD)
        + [pl.BlockSpec((None, D, D), lambda i: (l, 0, 0), pipeline_mode=pl.Buffered(1))]
        + _pair_specs(D) + [
                  _full((1, D)), _full((1, D)),
                  _mod_spec(l),
                  _layer_spec(l, 1, D), _layer_spec(l, 1, D), _layer_spec(l, D, 128), _layer_spec(l, 1, 128),
                  _full((TM, TM)), _full((128, 128))],
        out_specs=(tile(D), pl.BlockSpec((LOC, D), lambda i: (i, 0)), tile(128),
                   pl.BlockSpec((8, 128), lambda i: (i, 0))),
        scratch_shapes=[pltpu.VMEM((D, D), BF16)],
        compiler_params=_cparams("arbitrary"),
        name="out_proj_route",
    )(a, *hyb_pair, *att_pair, w_out, *x_pair, *norm, mod_l, g, b, w_route, b_route, tri, upper)


X_RING = 8


def _expert_kernel(off_ref, cnt_ref, src_ref, x_hbm, wg_ref, wu_ref, wd_ref, y_hbm,
                   xbuf, ybuf, wgu_b, wd_b, sem, osem):
    e = pl.program_id(0)
    first_tile = off_ref[e]
    n_tiles = cnt_ref[e]
    total = off_ref[N_EXP - 1] + cnt_ref[N_EXP - 1]

    def block_copies(t, s):
        return [pltpu.make_async_copy(
            x_hbm.at[pl.ds(pl.multiple_of(src_ref[t * BLK_PER_TILE + b] * BLK, BLK), BLK), :],
            xbuf.at[s, pl.ds(b * BLK, BLK), :], sem.at[s]) for b in range(BLK_PER_TILE)]

    def out_copy(t, s):
        return pltpu.make_async_copy(ybuf.at[s], y_hbm.at[pl.ds(pl.multiple_of(t * TM, TM), TM), :],
                                     osem.at[s])

    @pl.when(e == 0)
    def _():
        for t0 in range(X_RING - 1):
            @pl.when(t0 < total)
            def _():
                for cp in block_copies(t0, t0):
                    cp.start()

    @pl.when(n_tiles > 0)
    def _():
        wgu_b[:, 0:D_EXP] = wg_ref[0, 0].astype(BF16)
        wgu_b[:, D_EXP:2 * D_EXP] = wu_ref[0, 0].astype(BF16)
        wd_b[...] = wd_ref[0, 0].astype(BF16)

    def tile_body(t, carry):
        s = t % 2
        xs = t % X_RING

        @pl.when(t + X_RING - 1 < total)
        def _():
            for cp in block_copies(t + X_RING - 1, (t + X_RING - 1) % X_RING):
                cp.start()

        for cp in block_copies(t, xs):
            cp.wait()

        @pl.when(t >= 2)
        def _():
            out_copy(t - 2, s).wait()

        gu = jnp.dot(xbuf[xs], wgu_b[...], preferred_element_type=F32)
        gate, up = gu[:, :D_EXP], gu[:, D_EXP:]
        hid = gate / (1.0 + jnp.exp(-gate)) * up
        ybuf[s] = jnp.dot(hid.astype(BF16), wd_b[...], preferred_element_type=F32).astype(BF16)
        out_copy(t, s).start()
        return carry

    lax.fori_loop(first_tile, first_tile + n_tiles, tile_body, 0)

    @pl.when(e == N_EXP - 1)
    def _():
        @pl.when(total >= 2)
        def _():
            out_copy(total - 2, total % 2).wait()

        out_copy(total - 1, (total - 1) % 2).wait()
        ybuf[0] = jnp.zeros((TM, D), BF16)

        def zero_start(t, carry):
            out_copy(t, 0).start()
            return carry

        def zero_wait(t, carry):
            out_copy(t, 0).wait()
            return carry

        lax.fori_loop(total, MOE_TILES, zero_start, 0)
        lax.fori_loop(total, MOE_TILES, zero_wait, 0)


def _experts(l, tables, xloc, w_gate, w_up, w_down):
    grid_spec = pltpu.PrefetchScalarGridSpec(
        num_scalar_prefetch=len(tables),
        grid=(N_EXP,),
        in_specs=[pl.BlockSpec(memory_space=pl.ANY),
                  pl.BlockSpec((1, 1, D, D_EXP), lambda e, *_: (l, e, 0, 0)),
                  pl.BlockSpec((1, 1, D, D_EXP), lambda e, *_: (l, e, 0, 0)),
                  pl.BlockSpec((1, 1, D_EXP, D), lambda e, *_: (l, e, 0, 0))],
        out_specs=pl.BlockSpec(memory_space=pl.ANY),
        scratch_shapes=[pltpu.VMEM((X_RING, TM, D), BF16), pltpu.VMEM((2, TM, D), BF16),
                        pltpu.VMEM((D, 2 * D_EXP), BF16), pltpu.VMEM((D_EXP, D), BF16),
                        pltpu.SemaphoreType.DMA((X_RING,)), pltpu.SemaphoreType.DMA((2,))],
    )
    return pl.pallas_call(
        _expert_kernel,
        out_shape=jax.ShapeDtypeStruct((MOE_ROWS, D), BF16),
        grid_spec=grid_spec,
        compiler_params=_cparams("arbitrary"),
        name="experts",
    )(*tables, xloc, w_gate, w_up, w_down)


def _combine_kernel(nblk_ref, gsrc_ref, y_hbm, route_ref, x1_ref, mod_ref, g_ref, b_ref, oc_ref, ol_ref,
                    ybuf, sem):
    i = pl.program_id(0)

    slot_i = i % 2

    def block_copy(t, s, lb):
        src = pl.multiple_of(gsrc_ref[t * LOC_BLKS + lb] * BLK, BLK)
        return pltpu.make_async_copy(y_hbm.at[pl.ds(src, BLK), :],
                                     ybuf.at[s, pl.ds(pl.multiple_of(lb * BLK, BLK), BLK), :], sem.at[s])

    def fetch(t, s):
        def start(lb, c):
            block_copy(t, s, lb).start()
            return c
        lax.fori_loop(0, nblk_ref[t], start, 0)

    @pl.when(i == 0)
    def _():
        ybuf[...] = jnp.zeros_like(ybuf)
        fetch(0, 0)

    @pl.when(i + 1 < N_TILES)
    def _():
        fetch(jnp.minimum(i + 1, N_TILES - 1), 1 - slot_i)

    def wait(lb, c):
        block_copy(i, slot_i, lb).wait()
        return c

    lax.fori_loop(0, nblk_ref[i], wait, 0)
    route = route_ref[...]
    yb = ybuf[slot_i]
    slot = lax.broadcasted_iota(jnp.int32, (TM, LOC), 1)
    pick = (jnp.where(slot == route[:, 4:5].astype(jnp.int32), route[:, 2:3], 0.0)
            + jnp.where(slot == route[:, 5:6].astype(jnp.int32), route[:, 3:4], 0.0)).astype(BF16)
    moe = jnp.dot(pick, yb, preferred_element_type=F32)
    m = mod_ref[0]
    res = _layer_norm(ALPHA * x1_ref[...] + m[5:6] * moe, g_ref[...], b_ref[...])

    @pl.when(i < CTX_TILES)
    def _():
        oc_ref[...] = res

    @pl.when(i >= CTX_TILES)
    def _():
        ol_ref[...] = res


def _combine(l, nblk, gsrc, y, route, x1, mod_l, g, b):
    out_shape = (jax.ShapeDtypeStruct((T_CTX, D), F32), jax.ShapeDtypeStruct((T_LAT, D), F32))
    out_specs = tuple(_pair_specs(D))
    grid_spec = pltpu.PrefetchScalarGridSpec(
        num_scalar_prefetch=2,
        grid=(N_TILES,),
        in_specs=[pl.BlockSpec(memory_space=pl.ANY),
                  pl.BlockSpec((TM, 128), lambda i, nb, gs: (i, 0)),
                  pl.BlockSpec((TM, D), lambda i, nb, gs: (i, 0)),
                  _mod_spec(l), _layer_spec(l, 1, D), _layer_spec(l, 1, D)],
        out_specs=out_specs,
        scratch_shapes=[pltpu.VMEM((2, LOC, D), BF16), pltpu.SemaphoreType.DMA((2,))],
    )
    return pl.pallas_call(
        _combine_kernel,
        out_shape=out_shape,
        grid_spec=grid_spec,
        compiler_params=_cparams("arbitrary"),
        name="moe_combine",
    )(nblk, gsrc, y, route, x1, mod_l, g, b)


def _prefix_pick(starts, query, table):
    delta = table - jnp.concatenate([jnp.zeros_like(table[..., :1]), table[..., :-1]], axis=-1)
    return jnp.sum(jnp.where(starts <= query, delta, 0), axis=-1)


def _dispatch_tables(cnt):
    i32 = jnp.int32
    run = (cnt + BLK - 1) // BLK * BLK
    loc_start = jnp.cumsum(run, axis=1) - run
    nblk_loc = jnp.sum(run, axis=1) // BLK
    seg_rows = jnp.sum(run, axis=0)
    seg_tiles = (seg_rows + TM - 1) // TM
    tile_end = jnp.cumsum(seg_tiles)
    seg_start = (tile_end - seg_tiles) * TM
    glob_start = seg_start[None, :] + jnp.cumsum(run, axis=0) - run

    g_blk = (glob_start.T.reshape(1, -1)) // BLK
    n_blk = (run.T.reshape(1, -1)) // BLK
    l_blk = ((jnp.arange(N_TILES, dtype=i32) * LOC)[None, :] + loc_start.T).reshape(1, -1) // BLK
    gb = jnp.arange(MOE_ROWS // BLK, dtype=i32)[:, None]
    off = gb[:, 0] - _prefix_pick(g_blk, gb, g_blk)
    src_blk = jnp.where(off < _prefix_pick(g_blk, gb, n_blk), _prefix_pick(g_blk, gb, l_blk) + off,
                        ZERO_BLK).astype(i32)

    pos = (jnp.arange(LOC_BLKS, dtype=i32) * BLK)[None, :, None]
    shift = _prefix_pick(loc_start[:, None, :], pos, (glob_start - loc_start)[:, None, :])
    gsrc = ((pos[:, :, 0] + shift) // BLK).astype(i32)
    expert_tables = ((tile_end - seg_tiles).astype(i32), seg_tiles.astype(i32), src_blk)
    return expert_tables, nblk_loc.astype(i32), gsrc.reshape(-1)


def _rope_slot(pe):
    return jnp.pad(pe, [(0, 0)] * (pe.ndim - 1) + [(NOPE, SLOT - NOPE - ROPE)])


def _wuq_layout(w):
    w = w.reshape(DEPTH, Q_LORA, HEADS, NOPE + ROPE)
    w = jnp.pad(w, ((0, 0), (0, 0), (0, 0), (0, SLOT - NOPE - ROPE)))
    return w.reshape(DEPTH, Q_LORA, HEADS * SLOT).astype(BF16)


def _wukv_layout(w):
    w = w.reshape(DEPTH, KV_LORA, HEADS, NOPE + VD)
    wk = jnp.pad(w[..., :NOPE], ((0, 0), (0, 0), (0, 0), (0, SLOT - NOPE)))
    wv = w[..., NOPE:].reshape(DEPTH, KV_LORA, HEADS * VD).astype(BF16)
    return wk.reshape(DEPTH, KV_LORA, HEADS * SLOT).astype(BF16), wv, jnp.swapaxes(wv, 1, 2)


def kernel(x_prompt, x_sample, c, cache_ckv, cache_kpe, c_ctx, ln_in_g, ln_in_b, w_mod, b_mod, w_in,
           gm_ln_g, gm_ln_b, gm_ws, gm_bs, hy_conv_w, hy_conv_b, hy_f_w1, hy_f_b1, hy_f_w2, hy_f_b2,
           hy_f_w3, hy_f_freq, hy_bias, mla_gq, mla_gkv, mla_wuq, mla_wukv, w_out, ln1_g, ln1_b,
           ln2_g, ln2_b, moe_w_gr, moe_b_gr, moe_w_er, moe_b_er, moe_w_gate, moe_w_up, moe_w_down):
    rope = _rope_tables()
    dft = {L: _dft_tables(L) for L in (CTX_L, LAT_L)}
    pos_tab = {L: _hyena_positions(L) for L in (CTX_L, LAT_L)}
    hd = np.arange(GM_W) // GM_HD
    avg = jnp.asarray((hd[:, None] == hd[None, :]) / GM_HD, BF16)
    tri = jnp.asarray(np.tril(np.ones((TM, TM)), -1), BF16)
    upper = jnp.asarray(np.triu(np.ones((128, 128)), 1), BF16)

    cond8 = jnp.concatenate([c_ctx[None], c, jnp.zeros((8 - 1 - N_LAT_B, D), F32)], axis=0)
    mod = _modulation(cond8, w_mod, b_mod).reshape(DEPTH, 8, 6, D)

    wuk_all, wuv_all, wuvt_all = _wukv_layout(mla_wukv)
    w_in_t = jnp.swapaxes(w_in, 1, 2)
    wuq_arr = _wuq_layout(mla_wuq)
    ws_b = gm_ws.astype(BF16)
    kc_all, vc_all = _cache_kv(cache_ckv, _rope_slot(cache_kpe), wuk_all, wuvt_all)

    x_pair = (x_prompt.reshape(T_CTX, D), x_sample.reshape(T_LAT, D))
    norm = (ln_in_g.reshape(1, D), ln_in_b.reshape(1, D))
    row_vec = lambda p: p.reshape(DEPTH, 1, p.shape[-1])
    bs_full = jnp.repeat(jnp.swapaxes(gm_bs, 1, 2), GM_HD, axis=2)
    w_route = jnp.pad(jnp.concatenate([moe_w_gr, moe_w_er.reshape(DEPTH, D, N_EXP)], axis=2),
                      ((0, 0), (0, 0), (0, 128 - N_GROUPS - N_EXP)))
    b_route = jnp.pad(jnp.concatenate([moe_b_gr, moe_b_er.reshape(DEPTH, N_EXP)], axis=1),
                      ((0, 0), (0, 128 - N_GROUPS - N_EXP))).reshape(DEPTH, 1, 128)
    hyena_groups = ((CTX_L, N_CTX_B, 4, 0), (LAT_L, N_LAT_B, 2, T_CTX // LAT_L))
    spectra = {L: _filter_spectra(L, *pos_tab[L], dft[L], hy_f_w1, hy_f_b1, hy_f_w2, hy_f_b2, hy_f_w3,
                                  hy_f_freq) for L, _, _, _ in hyena_groups}
    ckv_states, kpe_states = [], []
    for l in range(DEPTH):
        a, hy, q, k, v, vt, ckv, kpe = _in_proj(
            l, x_pair, norm, mod, w_in_t, row_vec(gm_ln_g), row_vec(gm_ln_b), avg,
            ws_b, bs_full, row_vec(mla_gq), row_vec(mla_gkv), wuq_arr, wuk_all, wuv_all, wuvt_all, rope)
        ckv_states.append(ckv[:T_CTX].reshape(N_CTX_B, CTX_L, KV_LORA))
        kpe_states.append(kpe[:T_CTX, NOPE:NOPE + ROPE].reshape(N_CTX_B, CTX_L, ROPE))

        hyb = [_hyena_group(l, hy, L, nb, group, blk0, dft[L], spectra[L], hy_conv_w, hy_conv_b, hy_bias)
               for L, nb, group, blk0 in hyena_groups]

        att = _attention(q, k, v, vt, kc_all[l], vc_all[l])

        x1, xloc, route, counts = _out_proj(l, a, hyb, att, w_out, x_pair, norm, mod,
                                            row_vec(ln1_g), row_vec(ln1_b), w_route, b_route, tri, upper)
        cnt = counts.reshape(N_TILES, 8, 128)[:, 0, :N_EXP].astype(jnp.int32)
        expert_tables, nblk_loc, gsrc = _dispatch_tables(cnt)
        y = _experts(l, expert_tables, xloc, moe_w_gate, moe_w_up, moe_w_down)
        x_pair = _combine(l, nblk_loc, gsrc, y, route, x1, mod, row_vec(ln2_g), row_vec(ln2_b))

    y_prompt = x_pair[0].reshape(N_CTX_B, CTX_L, D)
    y_sample = x_pair[1].reshape(N_LAT_B, LAT_L, D)
    return (y_prompt, y_sample, jnp.stack(ckv_states, axis=1), jnp.stack(kpe_states, axis=1))
```

```python
import functools
import math

import numpy as np
import jax
import jax.numpy as jnp
from jax import lax
from jax.experimental import pallas as pl
from jax.experimental.pallas import tpu as pltpu

D = 1024
N_CTX_B, CTX_L = 16, 256
N_LAT_B, LAT_L = 4, 1024
DEPTH = 2
T_CTX = N_CTX_B * CTX_L
T_LAT = N_LAT_B * LAT_L
T = T_CTX + T_LAT
PAST = 256
GRID_W = 64

GM_HEADS, GM_HD, GM_W, CHUNK = 4, 64, 256, 128
HY_W, HY_EMB, HY_FH = 256, 33, 64
NOPE, ROPE, VD, HEADS = 64, 32, 64, 8
Q_LORA, KV_LORA = 256, 128
SLOT = 128
N_GROUPS, EPG, N_EXP, D_EXP = 4, 8, 32, 256
ALPHA = (2.0 * DEPTH) ** 0.25
LN_EPS, RMS_EPS = 1e-5, 1e-6
ROPE_THETA = 10000.0

TM = 256
N_TILES = T // TM
CTX_TILES = T_CTX // TM
LAT_TILES_PER_B = LAT_L // TM
N_ASSIGN = 2 * T
BLK = 16
BLK_PER_TILE = TM // BLK
LOC = 2 * TM + N_EXP * BLK
LOC_BLKS = LOC // BLK
ZERO_BLK = LOC_BLKS - 1
MOE_TILES = (N_ASSIGN + N_TILES * N_EXP * (BLK - 1)) // TM + N_EXP
MOE_ROWS = MOE_TILES * TM
IN_COLS = 1696
IN_EXT = 1792

F32, BF16 = jnp.float32, jnp.bfloat16
_NT = (((1,), (1,)), ((), ()))
VMEM_LIMIT = 52 * 1024 * 1024


def _cparams(*sem):
    return pltpu.CompilerParams(dimension_semantics=sem, vmem_limit_bytes=VMEM_LIMIT)


def _cond_of_tile(i):
    return jnp.where(i < CTX_TILES, 0, 1 + (i - CTX_TILES) // LAT_TILES_PER_B)


def _rope_block_of_tile(i):
    return jnp.where(i < CTX_TILES, 0, 1 + (i - CTX_TILES) % LAT_TILES_PER_B)


def _full(shape):
    n = len(shape)
    return pl.BlockSpec(shape, lambda *_: (0,) * n)


def _layer_spec(l, *shape):
    return pl.BlockSpec((None,) + shape, lambda *_: (l,) + (0,) * len(shape))


def _mod_spec(l):
    return pl.BlockSpec((None, 1, 6, D), lambda i, *_: (l, _cond_of_tile(i), 0, 0))


def _pair_specs(n):
    return [pl.BlockSpec((TM, n), lambda i, *_: (jnp.minimum(i, CTX_TILES - 1), 0)),
            pl.BlockSpec((TM, n), lambda i, *_: (jnp.maximum(i - CTX_TILES, 0), 0))]


def _pair_read(c_ref, l_ref):
    return jnp.where(pl.program_id(0) < CTX_TILES, c_ref[...], l_ref[...])


def _layer_norm(x, g, b):
    mu = jnp.mean(x, -1, keepdims=True)
    xc = x - mu
    var = jnp.mean(xc * xc, -1, keepdims=True)
    return xc * lax.rsqrt(var + LN_EPS) * g + b


def _rms_norm(x, g):
    return x * lax.rsqrt(jnp.mean(x * x, -1, keepdims=True) + RMS_EPS) * g


def _bdot(a, b):
    return jnp.dot(a.astype(BF16), b.astype(BF16), preferred_element_type=F32)


def _dot3(x, w):
    x_hi = x.astype(BF16)
    x_lo = (x - x_hi.astype(F32)).astype(BF16)
    w_hi = w.astype(BF16)
    w_lo = (w - w_hi.astype(F32)).astype(BF16)
    dot = functools.partial(jnp.dot, preferred_element_type=F32)
    return dot(x_hi, w_hi) + dot(x_hi, w_lo) + dot(x_lo, w_hi)


def _swap_pair_lanes(x):
    n = x.shape[1]
    lane = lax.broadcasted_iota(jnp.int32, x.shape, 1)
    return jnp.where(lane % 2 == 0, pltpu.roll(x, n - 1, 1), pltpu.roll(x, 1, 1))


def _split2_dot(x, w_bf16):
    hi = x.astype(BF16)
    lo = (x - hi.astype(F32)).astype(BF16)
    dot = functools.partial(jnp.dot, preferred_element_type=F32)
    return dot(hi, w_bf16) + dot(lo, w_bf16)


def _dft_tables(L):
    f = np.arange(L, dtype=np.int64)
    ft = np.outer(f, f) % (2 * L)
    ang = np.pi * ft / L
    c = np.cos(ang)
    s = np.sin(ang)
    alt = np.where(f % 2 == 0, 1.0, -1.0)
    sf = s.copy()
    sf[0, :] = alt
    return (jnp.asarray(c, F32).astype(BF16), jnp.asarray(sf, F32).astype(BF16),
            jnp.asarray(sf.T.copy(), F32).astype(BF16), jnp.asarray(alt[:, None], F32))


def _rope_tables():
    rows = LAT_L // GRID_W
    row = np.repeat(np.arange(rows, dtype=np.float64), GRID_W)
    col = np.tile(np.arange(GRID_W, dtype=np.float64), rows)
    n = ROPE // 4
    inv = ROPE_THETA ** (-np.arange(n, dtype=np.float64) / n)
    ang = np.concatenate([row[:, None] * inv, col[:, None] * inv], -1)
    cos = np.zeros((CTX_L + LAT_L, SLOT))
    sin = np.zeros((CTX_L + LAT_L, SLOT))
    cos[:, :NOPE + ROPE] = 1.0
    cos[CTX_L:, NOPE:NOPE + ROPE:2] = np.cos(ang)
    cos[CTX_L:, NOPE + 1:NOPE + ROPE:2] = np.cos(ang)
    sin[CTX_L:, NOPE:NOPE + ROPE:2] = -np.sin(ang)
    sin[CTX_L:, NOPE + 1:NOPE + ROPE:2] = np.sin(ang)
    scale = math.log2(math.e) / math.sqrt(NOPE + ROPE)
    return (jnp.asarray(cos * scale, F32), jnp.asarray(sin * scale, F32),
            jnp.asarray(cos, F32), jnp.asarray(sin, F32))


def _hyena_positions(L):
    t = jnp.linspace(0.0, 1.0, L, dtype=F32)[:, None]
    bands = (HY_EMB - 1) // 2
    w = 2.0 * math.pi * jnp.arange(L, dtype=F32)[:, None] / L
    f = jnp.linspace(1e-4, bands - 1, bands, dtype=F32)[None]
    z = jnp.concatenate([t, jnp.cos(f * w), -jnp.sin(f * w)], -1)
    z = jnp.pad(z, ((0, 0), (0, 128 - HY_EMB)))
    min_decay = math.log(1e-2) / 1.5
    max_decay = math.log(1e-2) / 0.3
    deltas = jnp.abs(jnp.linspace(min_decay, max_decay, HY_W, dtype=F32))
    return z, jnp.exp(-t * deltas)


def _mod_kernel(c_ref, w_ref, b_ref, o_ref):
    c = c_ref[...]
    s = c / (1.0 + jnp.exp(-c))
    o_ref[0] = _bdot(s, w_ref[0]) + b_ref[0]


def _modulation(cond8, w_mod, b_mod):
    tn = 1536
    return pl.pallas_call(
        _mod_kernel,
        out_shape=jax.ShapeDtypeStruct((DEPTH, 8, 6 * D), F32),
        grid=(DEPTH, 6 * D // tn),
        in_specs=[pl.BlockSpec((8, D), lambda l, j: (0, 0)),
                  pl.BlockSpec((1, D, tn), lambda l, j: (l, 0, j)),
                  pl.BlockSpec((1, 1, tn), lambda l, j: (l, 0, j))],
        out_specs=pl.BlockSpec((1, 8, tn), lambda l, j: (l, 0, j)),
        compiler_params=_cparams("parallel", "parallel"),
        name="modulation",
    )(cond8, w_mod, b_mod.reshape(DEPTH, 1, 6 * D))


def _filter_kernel(z_ref, w1_ref, b1_ref, w2_ref, b2_ref, w3_ref, fr_ref, dec_ref, c_ref, sf_ref,
                   alt_ref, kr_ref, ki_ref, krn_ref, *, L):
    h1 = jnp.sin(fr_ref[0:1] * (_dot3(z_ref[...], w1_ref[...]) + b1_ref[...]))
    h2 = jnp.sin(fr_ref[1:2] * (_dot3(h1, w2_ref[...]) + b2_ref[...]))
    filt = _dot3(h2, w3_ref[...])
    row = lax.broadcasted_iota(jnp.int32, (L, HY_W), 0)
    dec = dec_ref[...]
    n = DEPTH * 2
    fwd = [filt[:, j * 2 * HY_W:j * 2 * HY_W + HY_W] * dec for j in range(n)]
    bwd = [jnp.where(row == 0, 0.0, filt[:, j * 2 * HY_W + HY_W:(j + 1) * 2 * HY_W] * dec) for j in range(n)]
    sm = [f + b for f, b in zip(fwd, bwd)]
    kr_all = _bdot(c_ref[...], jnp.concatenate(sm, axis=1))
    ki_all = -_bdot(sf_ref[...], jnp.concatenate([f - b for f, b in zip(fwd, bwd)], axis=1))
    for j in range(n):
        l, o = divmod(j, 2)
        kr = kr_all[:, j * HY_W:(j + 1) * HY_W]
        kr_nyq = jnp.sum(alt_ref[...] * sm[j], axis=0, keepdims=True)
        kr_ref[l, o] = jnp.where(row == 0, kr * (0.5 / L), kr * (1.0 / L))
        ki_ref[l, o] = jnp.where(row == 0, 0.0, ki_all[:, j * HY_W:(j + 1) * HY_W] * (1.0 / L))
        krn_ref[l, o] = jnp.where(row == 0, kr_nyq * (0.5 / L), kr * (1.0 / L))


def _block_diag(blocks):
    rows = []
    for i, blk in enumerate(blocks):
        rows.append(jnp.concatenate([blk if j == i else jnp.zeros((blk.shape[0], other.shape[1]), blk.dtype)
                                     for j, other in enumerate(blocks)], axis=1))
    return jnp.concatenate(rows, axis=0)


def _filter_spectra(L, z, dec, dft, w1, b1, w2, b2, w3, freq):
    assert DEPTH * HY_FH == 128
    c, sf, _, alt = dft
    layers = range(DEPTH)
    w1c = jnp.pad(jnp.concatenate([w1[l] for l in layers], axis=1), ((0, 128 - HY_EMB), (0, 0)))
    w2c = _block_diag([w2[l] for l in layers])
    w3c = _block_diag([w3[l] for l in layers])
    b1c, b2c = b1.reshape(1, 128), b2.reshape(1, 128)
    frc = jnp.swapaxes(freq, 0, 1).reshape(2, 128)
    shp = jax.ShapeDtypeStruct((DEPTH, 2, L, HY_W), F32)
    args = (z, w1c, b1c, w2c, b2c, w3c, frc, dec, c, sf, alt)
    return pl.pallas_call(
        functools.partial(_filter_kernel, L=L),
        out_shape=(shp, shp, shp),
        grid=(1,),
        in_specs=[_full(a.shape) for a in args],
        out_specs=(_full((DEPTH, 2, L, HY_W)),) * 3,
        compiler_params=_cparams("arbitrary"),
        name=f"hyena_filter_{L}",
    )(*args)


def _in_kernel(xc_ref, xl_ref, ng_ref, nb_ref, mod_ref, wt_ref, gg_ref, gb_ref, avg_ref, ws_ref, bs_ref,
               gq_ref, gkv_ref, wuq_ref, wuk_ref, wuv_ref, wuvt_ref, cq_ref, sq_ref, ck_ref, sk_ref,
               a_ref, hy_ref, q_ref, k_ref, v_ref, vt_ref, ckv_ref, kpe_ref, w_b, *, layer):
    l = layer

    @pl.when(pl.program_id(0) == 0)
    def _():
        for c0 in range(0, 1664, 128):
            w_b[:, c0:c0 + 128] = jnp.transpose(wt_ref[c0:c0 + 128, :]).astype(BF16)
        slot_rows = jnp.concatenate([jnp.zeros((NOPE, D), F32), wt_ref[1664:IN_COLS, :],
                                     jnp.zeros((SLOT - NOPE - ROPE, D), F32)], axis=0)
        w_b[:, 1664:IN_EXT] = jnp.transpose(slot_rows).astype(BF16)

    x = _pair_read(xc_ref, xl_ref)
    if l == 0:
        x = _layer_norm(x, ng_ref[...], nb_ref[...])
    m = mod_ref[0]
    h = x * (1.0 + m[1:2]) + m[0:1]
    proj = _bdot(h, w_b[...])

    gu = jax.nn.gelu(proj[:, 0:GM_W], approximate=True)
    gv = jax.nn.gelu(proj[:, GM_W:2 * GM_W], approximate=True)
    avg = avg_ref[...]
    mu = _split2_dot(gv, avg)
    vc = gv - mu
    var = _split2_dot(vc * vc, avg)
    vln = (vc * lax.rsqrt(var + LN_EPS) * gg_ref[l:l + 1, :] + gb_ref[l:l + 1, :]).astype(BF16)
    lane = lax.broadcasted_iota(jnp.int32, (CHUNK, GM_W), 1)
    for c in range(TM // CHUNK):
        vchunk = vln[c * CHUNK:(c + 1) * CHUNK]
        s = bs_ref[...]
        for hd in range(GM_HEADS):
            sh = jnp.dot(ws_ref[0, hd], vchunk, preferred_element_type=F32)
            s = s + jnp.where(lane // GM_HD == hd, sh, 0.0)
        a_ref[c * CHUNK:(c + 1) * CHUNK, :] = (gu[c * CHUNK:(c + 1) * CHUNK] * s).astype(BF16)

    hy_ref[...] = proj[:, 512:1280]

    cq = _rms_norm(proj[:, 1280:1536], gq_ref[l:l + 1, :])
    qq = _bdot(cq, wuq_ref[0])
    cos_q = jnp.concatenate([cq_ref[...]] * HEADS, axis=1)
    sin_q = jnp.concatenate([sq_ref[...]] * HEADS, axis=1)
    q_ref[...] = (qq * cos_q + _swap_pair_lanes(qq) * sin_q).astype(BF16)

    ckv = _rms_norm(proj[:, 1536:1664], gkv_ref[l:l + 1, :])
    ckv_ref[...] = ckv
    kpe = proj[:, 1664:IN_EXT]
    kpe_ref[...] = kpe
    krot = kpe * ck_ref[...] + _swap_pair_lanes(kpe) * sk_ref[...]
    kn = _bdot(ckv, wuk_ref[0])
    k_ref[...] = (kn + jnp.concatenate([krot] * HEADS, axis=1)).astype(BF16)
    v_ref[...] = _bdot(ckv, wuv_ref[0]).astype(BF16)
    vt_ref[...] = lax.dot_general(wuvt_ref[0], ckv.astype(BF16), _NT,
                                  preferred_element_type=F32).astype(BF16)


def _in_proj(l, x_pair, norm, mod_l, w_in_t, gm_g, gm_b, avg, ws, bs_full, gq, gkv, wuq_arr,
             wuk_arr, wuv_arr, wuvt_arr, rope):
    layer = lambda *shape: pl.BlockSpec((1,) + shape, lambda i: (l,) + (0,) * len(shape))
    cos_q, sin_q, cos_k, sin_k = rope
    tile = lambda n: pl.BlockSpec((TM, n), lambda i: (i, 0))
    rope_spec = pl.BlockSpec((TM, SLOT), lambda i: (_rope_block_of_tile(i), 0))
    out_shapes = (jax.ShapeDtypeStruct((T, GM_W), BF16),
                  jax.ShapeDtypeStruct((T, 3 * HY_W), F32),
                  jax.ShapeDtypeStruct((T, HEADS * SLOT), BF16),
                  jax.ShapeDtypeStruct((T, HEADS * SLOT), BF16),
                  jax.ShapeDtypeStruct((T, HEADS * VD), BF16),
                  jax.ShapeDtypeStruct((HEADS * VD, T), BF16),
                  jax.ShapeDtypeStruct((T, KV_LORA), F32),
                  jax.ShapeDtypeStruct((T, SLOT), F32))
    return pl.pallas_call(
        functools.partial(_in_kernel, layer=l),
        out_shape=out_shapes,
        grid=(N_TILES,),
        in_specs=_pair_specs(D) + [
                  _full((1, D)), _full((1, D)),
                  _mod_spec(l),
                  pl.BlockSpec((None, IN_COLS, D), lambda i: (l, 0, 0), pipeline_mode=pl.Buffered(1)),
                  _full((DEPTH, GM_W)), _full((DEPTH, GM_W)), _full((GM_W, GM_W)),
                  layer(GM_HEADS, CHUNK, CHUNK), _layer_spec(l, CHUNK, GM_W),
                  _full((DEPTH, Q_LORA)), _full((DEPTH, KV_LORA)),
                  layer(Q_LORA, HEADS * SLOT), layer(KV_LORA, HEADS * SLOT),
                  layer(KV_LORA, HEADS * VD), layer(HEADS * VD, KV_LORA),
                  rope_spec, rope_spec, rope_spec, rope_spec],
        out_specs=(tile(GM_W), tile(3 * HY_W), tile(HEADS * SLOT), tile(HEADS * SLOT), tile(HEADS * VD),
                   pl.BlockSpec((HEADS * VD, TM), lambda i: (0, i)), tile(KV_LORA), tile(SLOT)),
        scratch_shapes=[pltpu.VMEM((D, IN_EXT), BF16)],
        compiler_params=_cparams("arbitrary"),
        name="in_proj",
    )(*x_pair, *norm, mod_l, w_in_t, gm_g, gm_b, avg, ws, bs_full, gq, gkv, wuq_arr, wuk_arr, wuv_arr,
      wuvt_arr, cos_q, sin_q, cos_k, sin_k)


def _cache_kernel(ckv_ref, kpe_ref, wuk_ref, wuv_ref, k_ref, v_ref):
    ckv = ckv_ref[0, 0]
    kn = _bdot(ckv, wuk_ref[0])
    k_ref[0, 0] = (kn + jnp.concatenate([kpe_ref[0, 0]] * HEADS, axis=1)).astype(BF16)
    v_ref[0, 0] = lax.dot_general(wuv_ref[0], ckv.astype(BF16), _NT,
                                  preferred_element_type=F32).astype(BF16)


def _cache_kv(cache_ckv, cache_kpe_slot, wuk_arr, wuv_arr):
    return pl.pallas_call(
        _cache_kernel,
        out_shape=(jax.ShapeDtypeStruct((DEPTH, N_LAT_B, PAST, HEADS * SLOT), BF16),
                   jax.ShapeDtypeStruct((DEPTH, N_LAT_B, HEADS * VD, PAST), BF16)),
        grid=(DEPTH, N_LAT_B),
        in_specs=[pl.BlockSpec((1, 1, PAST, KV_LORA), lambda l, b: (b, l, 0, 0)),
                  pl.BlockSpec((1, 1, PAST, SLOT), lambda l, b: (b, l, 0, 0)),
                  pl.BlockSpec((1, KV_LORA, HEADS * SLOT), lambda l, b: (l, 0, 0)),
                  pl.BlockSpec((1, HEADS * VD, KV_LORA), lambda l, b: (l, 0, 0))],
        out_specs=(pl.BlockSpec((1, 1, PAST, HEADS * SLOT), lambda l, b: (l, b, 0, 0)),
                   pl.BlockSpec((1, 1, HEADS * VD, PAST), lambda l, b: (l, b, 0, 0))),
        compiler_params=_cparams("parallel", "parallel"),
        name="cache_kv",
    )(cache_ckv, cache_kpe_slot, wuk_arr, wuv_arr)


def _attend_rows(q_ref, k_ref, v_ref, o_ref):
    lq = q_ref.shape[0]
    lane = lax.broadcasted_iota(jnp.int32, (lq, 2 * VD), 1)
    for pair in range(HEADS // 2):
        outs = []
        for hd in (2 * pair, 2 * pair + 1):
            s = lax.dot_general(q_ref[:, hd * SLOT:(hd + 1) * SLOT], k_ref[:, hd * SLOT:(hd + 1) * SLOT], _NT,
                                preferred_element_type=F32)
            p = jnp.exp2(s - jnp.max(s, -1, keepdims=True))
            acc = jnp.dot(p.astype(BF16), v_ref[:, pair * 2 * VD:(pair + 1) * 2 * VD],
                          preferred_element_type=F32)
            outs.append(acc / jnp.sum(p, -1, keepdims=True))
        o_ref[:, pair * 2 * VD:(pair + 1) * 2 * VD] = jnp.where(lane < VD, outs[0], outs[1]).astype(BF16)


def _attend_cols(q_ref, segments, o_ref):
    lq = q_ref.shape[0]
    lane = lax.broadcasted_iota(jnp.int32, (lq, 2 * SLOT), 1)
    row = lax.broadcasted_iota(jnp.int32, (2 * VD, lq), 0)
    outs = []
    for pair in range(HEADS // 2):
        qp = q_ref[:, pair * 2 * SLOT:(pair + 1) * 2 * SLOT]
        zero = jnp.zeros_like(qp)
        q_bd = jnp.concatenate([jnp.where(lane < SLOT, qp, zero), jnp.where(lane < SLOT, zero, qp)], axis=0)
        scores = [lax.dot_general(k_ref[:, pair * 2 * SLOT:(pair + 1) * 2 * SLOT], q_bd, _NT,
                                  preferred_element_type=F32) for k_ref, _ in segments]
        mx = functools.reduce(jnp.maximum, [jnp.max(s, 0, keepdims=True) for s in scores])
        ps = [jnp.exp2(s - mx) for s in scores]
        den = functools.reduce(jnp.add, [jnp.sum(p, 0, keepdims=True) for p in ps])
        acc = functools.reduce(jnp.add, [
            jnp.dot(vt_ref[pair * 2 * VD:(pair + 1) * 2 * VD, :], p.astype(BF16),
                    preferred_element_type=F32) for p, (_, vt_ref) in zip(ps, segments)])
        acc = acc / den
        outs.append(jnp.where(row < VD, acc[:, :lq], acc[:, lq:]))
    o_ref[...] = jnp.transpose(jnp.concatenate(outs, axis=0)).astype(BF16)


def _attn_ctx_kernel(q_ref, k_ref, v_ref, o_ref):
    _attend_rows(q_ref, k_ref, v_ref, o_ref)


def _attn_lat_kernel(q_ref, kc_ref, vc_ref, k_ref, vt_ref, o_ref):
    _attend_cols(q_ref, [(kc_ref.at[0], vc_ref.at[0]), (k_ref, vt_ref)], o_ref)


def _attention(l, q, k, v, vt, kc, vc):
    kw, vw = HEADS * SLOT, HEADS * VD
    ctx = pl.pallas_call(
        _attn_ctx_kernel,
        out_shape=jax.ShapeDtypeStruct((T_CTX, vw), BF16),
        grid=(N_CTX_B,),
        in_specs=[pl.BlockSpec((CTX_L, kw), lambda b: (b, 0)),
                  pl.BlockSpec((CTX_L, kw), lambda b: (b, 0)),
                  pl.BlockSpec((CTX_L, vw), lambda b: (b, 0))],
        out_specs=pl.BlockSpec((CTX_L, vw), lambda b: (b, 0)),
        compiler_params=_cparams("parallel"),
        name="attn_ctx",
    )(q, k, v)
    nq = LAT_L // TM
    off = T_CTX // LAT_L
    lat = pl.pallas_call(
        _attn_lat_kernel,
        out_shape=jax.ShapeDtypeStruct((T_LAT, vw), BF16),
        grid=(N_LAT_B, nq),
        in_specs=[pl.BlockSpec((TM, kw), lambda b, j: (CTX_TILES + b * nq + j, 0)),
                  pl.BlockSpec((None, 1, PAST, kw), lambda b, j: (l, b, 0, 0)),
                  pl.BlockSpec((None, 1, vw, PAST), lambda b, j: (l, b, 0, 0)),
                  pl.BlockSpec((LAT_L, kw), lambda b, j: (off + b, 0)),
                  pl.BlockSpec((vw, LAT_L), lambda b, j: (0, off + b))],
        out_specs=pl.BlockSpec((TM, vw), lambda b, j: (b * nq + j, 0)),
        compiler_params=_cparams("parallel", "parallel"),
        name="attn_lat",
    )(q, kc, vc, k, vt)
    return ctx, lat


def _hyena_kernel(hy_ref, cw_ref, cb_ref, c_ref, sf_ref, sb_ref, kr_ref, ki_ref, krn_ref, hb_ref,
                  o_ref, *, L, group, layer):
    row = lax.broadcasted_iota(jnp.int32, (L, 3 * HY_W), 0)
    zs = []
    for j in range(group):
        x = hy_ref[j * L:(j + 1) * L, :]
        prev = jnp.where(row == 0, 0.0, pltpu.roll(x, 1, 0))
        nxt = jnp.where(row == L - 1, 0.0, pltpu.roll(x, L - 1, 0))
        zs.append(prev * cw_ref[0:1] + x * cw_ref[1:2] + nxt * cw_ref[2:3] + cb_ref[layer:layer + 1, :])
    side = lambda parts: jnp.concatenate(parts, axis=1)
    y = side([z[:, 2 * HY_W:] for z in zs])
    for o in range(2):
        gate = side([z[:, o * HY_W:(o + 1) * HY_W] for z in zs])
        kr, ki, krn = (side([r[o]] * group) for r in (kr_ref, ki_ref, krn_ref))
        yb = y.astype(BF16)
        a_re = jnp.dot(c_ref[...], yb, preferred_element_type=F32)
        a_im = jnp.dot(sf_ref[...], yb, preferred_element_type=F32)
        z_re = a_re * kr + a_im * ki
        z_im = a_im * krn - a_re * ki
        conv = (jnp.dot(c_ref[...], z_re.astype(BF16), preferred_element_type=F32)
                + jnp.dot(sb_ref[...], z_im.astype(BF16), preferred_element_type=F32))
        y = gate * (conv + y * side([hb_ref[o:o + 1]] * group))
    for j in range(group):
        o_ref[j * L:(j + 1) * L, :] = y[:, j * HY_W:(j + 1) * HY_W].astype(BF16)


def _hyena_group(l, hy, L, nb, group, blk0, dft, spectra, conv_w, conv_b, hy_bias):
    c, sf, sb, _ = dft
    kr, ki, krn = spectra
    const = lambda shape: pl.BlockSpec(shape, lambda b: (0,) * len(shape), pipeline_mode=pl.Buffered(1))
    spec = pl.BlockSpec((None, 2, L, HY_W), lambda b: (l, 0, 0, 0), pipeline_mode=pl.Buffered(1))
    return pl.pallas_call(
        functools.partial(_hyena_kernel, L=L, group=group, layer=l),
        out_shape=jax.ShapeDtypeStruct((nb * L, HY_W), BF16),
        grid=(nb // group,),
        in_specs=[pl.BlockSpec((group * L, 3 * HY_W), lambda b: (blk0 // group + b, 0)),
                  _layer_spec(l, 3, 3 * HY_W), _full((DEPTH, 3 * HY_W)),
                  const((L, L)), const((L, L)), const((L, L)), spec, spec, spec,
                  _layer_spec(l, 2, HY_W)],
        out_specs=pl.BlockSpec((group * L, HY_W), lambda b: (b, 0)),
        compiler_params=_cparams("parallel"),
        name=f"hyena_{L}",
    )(hy, conv_w, conv_b, c, sf, sb, kr, ki, krn, hy_bias)


def _out_kernel(a_ref, bc_ref, bl_ref, mc_ref, ml_ref, w_ref, xc_ref, xl_ref, ng_ref, nb_ref, mod_ref,
                g_ref, be_ref, wr_ref, br_ref, tri_ref, upper_ref, x1_ref, xloc_ref, route_ref, cnt_ref,
                w_b, *, layer):
    l = layer

    @pl.when(pl.program_id(0) == 0)
    def _():
        w_b[...] = w_ref[...].astype(BF16)

    dot = functools.partial(jnp.dot, preferred_element_type=F32)
    mixed = (dot(a_ref[...], w_b[0:GM_W]) + dot(_pair_read(bc_ref, bl_ref), w_b[GM_W:GM_W + HY_W])
             + dot(_pair_read(mc_ref, ml_ref), w_b[GM_W + HY_W:]))
    x = _pair_read(xc_ref, xl_ref)
    if l == 0:
        x = _layer_norm(x, ng_ref[...], nb_ref[...])
    m = mod_ref[0]
    x1 = _layer_norm(ALPHA * x + m[2:3] * mixed, g_ref[l:l + 1, :], be_ref[l:l + 1, :])
    x1_ref[...] = x1
    h2 = x1 * (1.0 + m[4:5]) + m[3:4]

    h_hi = h2.astype(BF16)
    h_lo = (h2 - h_hi.astype(F32)).astype(BF16)
    w_hi = wr_ref[...].astype(BF16)
    w_lo = (wr_ref[...] - w_hi.astype(F32)).astype(BF16)
    part = dot(h_hi, jnp.concatenate([w_hi, w_lo], axis=1))
    logits = part[:, :128] + part[:, 128:] + dot(h_lo, w_hi) + br_ref[l:l + 1, :]
    lane = lax.broadcasted_iota(jnp.int32, logits.shape, 1)
    lanef = lane.astype(F32)
    big = jnp.float32(1e9)
    ninf = jnp.float32(-jnp.inf)
    is_g = lane < N_GROUPS
    gl = jnp.where(is_g, logits, ninf)
    gmax = jnp.max(gl, -1, keepdims=True)
    gidx = jnp.min(jnp.where(gl == gmax, lanef, big), -1, keepdims=True)
    gw = 1.0 / jnp.sum(jnp.where(is_g, jnp.exp(logits - gmax), 0.0), -1, keepdims=True)
    ex = lane - N_GROUPS
    in_group = (ex >= 0) & (ex < N_EXP) & ((ex // EPG).astype(F32) == gidx)
    el = jnp.where(in_group, logits, ninf)
    v1 = jnp.max(el, -1, keepdims=True)
    i1 = jnp.min(jnp.where(el == v1, lanef, big), -1, keepdims=True)
    el2 = jnp.where(lanef == i1, ninf, el)
    v2 = jnp.max(el2, -1, keepdims=True)
    i2 = jnp.min(jnp.where(el2 == v2, lanef, big), -1, keepdims=True)
    e21 = jnp.exp(v2 - v1)
    w1 = gw / (1.0 + e21)
    w2 = gw * e21 / (1.0 + e21)
    e1 = i1 - N_GROUPS
    e2 = i2 - N_GROUPS

    oh1 = jnp.where(lanef == e1, 1.0, 0.0)
    oh2 = jnp.where(lanef == e2, 1.0, 0.0)
    ex1 = dot(tri_ref[...], oh1.astype(BF16))
    ex2 = dot(tri_ref[...], oh2.astype(BF16))
    col1 = jnp.sum(oh1, axis=0, keepdims=True)
    col2 = jnp.sum(oh2, axis=0, keepdims=True)
    n = col1 + col2
    run = jnp.floor((n + (BLK - 1.0)) * (1.0 / BLK)) * BLK
    start = dot(jnp.broadcast_to(run, (8, 128)).astype(BF16), upper_ref[...])[0:1]
    loc1 = jnp.sum(oh1 * (start + ex1), -1, keepdims=True)
    loc2 = jnp.sum(oh2 * (start + col1 + ex2), -1, keepdims=True)
    cnt_ref[...] = jnp.broadcast_to(n, cnt_ref.shape)

    route = jnp.zeros_like(logits)
    for j, val in enumerate((e1, e2, w1, w2, loc1, loc2)):
        route = jnp.where(lane == j, val, route)
    route_ref[...] = route

    loc1_row = jnp.transpose(jnp.broadcast_to(loc1, (TM, 128)))[0:1].astype(jnp.int32)
    loc2_row = jnp.transpose(jnp.broadcast_to(loc2, (TM, 128)))[0:1].astype(jnp.int32)
    slot = lax.broadcasted_iota(jnp.int32, (LOC, TM), 0)
    perm = jnp.where((slot == loc1_row) | (slot == loc2_row), 1.0, 0.0).astype(BF16)
    xloc_ref[...] = dot(perm, h_hi).astype(BF16)


def _out_proj(l, a, hyb_pair, att_pair, w_out, x_pair, norm, mod_l, g, b, w_route, b_route, tri, upper):
    tile = lambda n: pl.BlockSpec((TM, n), lambda i: (i, 0))
    return pl.pallas_call(
        functools.partial(_out_kernel, layer=l),
        out_shape=(jax.ShapeDtypeStruct((T, D), F32), jax.ShapeDtypeStruct((N_TILES * LOC, D), BF16),
                   jax.ShapeDtypeStruct((T, 128), F32), jax.ShapeDtypeStruct((N_TILES * 8, 128), F32)),
        grid=(N_TILES,),
        in_specs=[tile(GM_W)] + _pair_specs(HY_W) + _pair_specs(HEADS * VD)
        + [pl.BlockSpec((None, D, D), lambda i: (l, 0, 0), pipeline_mode=pl.Buffered(1))]
        + _pair_specs(D) + [
                  _full((1, D)), _full((1, D)),
                  _mod_spec(l),
                  _full((DEPTH, D)), _full((DEPTH, D)), _layer_spec(l, D, 128), _full((DEPTH, 128)),
                  _full((TM, TM)), _full((128, 128))],
        out_specs=(tile(D), pl.BlockSpec((LOC, D), lambda i: (i, 0)), tile(128),
                   pl.BlockSpec((8, 128), lambda i: (i, 0))),
        scratch_shapes=[pltpu.VMEM((D, D), BF16)],
        compiler_params=_cparams("arbitrary"),
        name="out_proj_route",
    )(a, *hyb_pair, *att_pair, w_out, *x_pair, *norm, mod_l, g, b, w_route, b_route, tri, upper)


X_RING = 8


def _expert_kernel(off_ref, cnt_ref, src_ref, x_hbm, wg_ref, wu_ref, wd_ref, y_hbm,
                   xbuf, ybuf, wgu_b, wd_b, sem, osem):
    e = pl.program_id(0)
    first_tile = off_ref[e]
    n_tiles = cnt_ref[e]
    total = off_ref[N_EXP - 1] + cnt_ref[N_EXP - 1]

    def block_copies(t, s):
        return [pltpu.make_async_copy(
            x_hbm.at[pl.ds(pl.multiple_of(src_ref[t * BLK_PER_TILE + b] * BLK, BLK), BLK), :],
            xbuf.at[s, pl.ds(b * BLK, BLK), :], sem.at[s]) for b in range(BLK_PER_TILE)]

    def out_copy(t, s):
        return pltpu.make_async_copy(ybuf.at[s], y_hbm.at[pl.ds(pl.multiple_of(t * TM, TM), TM), :],
                                     osem.at[s])

    @pl.when(e == 0)
    def _():
        for t0 in range(X_RING - 1):
            @pl.when(t0 < total)
            def _():
                for cp in block_copies(t0, t0):
                    cp.start()

    @pl.when(n_tiles > 0)
    def _():
        wgu_b[:, 0:D_EXP] = wg_ref[0, 0].astype(BF16)
        wgu_b[:, D_EXP:2 * D_EXP] = wu_ref[0, 0].astype(BF16)
        wd_b[...] = wd_ref[0, 0].astype(BF16)

    def tile_body(t, carry):
        s = t % 2
        xs = t % X_RING

        @pl.when(t + X_RING - 1 < total)
        def _():
            for cp in block_copies(t + X_RING - 1, (t + X_RING - 1) % X_RING):
                cp.start()

        for cp in block_copies(t, xs):
            cp.wait()

        @pl.when(t >= 2)
        def _():
            out_copy(t - 2, s).wait()

        gu = jnp.dot(xbuf[xs], wgu_b[...], preferred_element_type=F32)
        gate, up = gu[:, :D_EXP], gu[:, D_EXP:]
        hid = gate / (1.0 + jnp.exp(-gate)) * up
        ybuf[s] = jnp.dot(hid.astype(BF16), wd_b[...], preferred_element_type=F32).astype(BF16)
        out_copy(t, s).start()
        return carry

    lax.fori_loop(first_tile, first_tile + n_tiles, tile_body, 0)

    @pl.when(e == N_EXP - 1)
    def _():
        @pl.when(total >= 2)
        def _():
            out_copy(total - 2, total % 2).wait()

        out_copy(total - 1, (total - 1) % 2).wait()
        ybuf[0] = jnp.zeros((TM, D), BF16)

        def zero_start(t, carry):
            out_copy(t, 0).start()
            return carry

        def zero_wait(t, carry):
            out_copy(t, 0).wait()
            return carry

        lax.fori_loop(total, MOE_TILES, zero_start, 0)
        lax.fori_loop(total, MOE_TILES, zero_wait, 0)


def _experts(l, tables, xloc, w_gate, w_up, w_down):
    grid_spec = pltpu.PrefetchScalarGridSpec(
        num_scalar_prefetch=len(tables),
        grid=(N_EXP,),
        in_specs=[pl.BlockSpec(memory_space=pl.ANY),
                  pl.BlockSpec((1, 1, D, D_EXP), lambda e, *_: (l, e, 0, 0)),
                  pl.BlockSpec((1, 1, D, D_EXP), lambda e, *_: (l, e, 0, 0)),
                  pl.BlockSpec((1, 1, D_EXP, D), lambda e, *_: (l, e, 0, 0))],
        out_specs=pl.BlockSpec(memory_space=pl.ANY),
        scratch_shapes=[pltpu.VMEM((X_RING, TM, D), BF16), pltpu.VMEM((2, TM, D), BF16),
                        pltpu.VMEM((D, 2 * D_EXP), BF16), pltpu.VMEM((D_EXP, D), BF16),
                        pltpu.SemaphoreType.DMA((X_RING,)), pltpu.SemaphoreType.DMA((2,))],
    )
    return pl.pallas_call(
        _expert_kernel,
        out_shape=jax.ShapeDtypeStruct((MOE_ROWS, D), BF16),
        grid_spec=grid_spec,
        compiler_params=_cparams("arbitrary"),
        name="experts",
    )(*tables, xloc, w_gate, w_up, w_down)


def _combine_kernel(nblk_ref, gsrc_ref, y_hbm, route_ref, x1_ref, mod_ref, g_ref, b_ref, oc_ref, ol_ref,
                    ybuf, sem, *, layer):
    i = pl.program_id(0)

    slot_i = i % 2

    def block_copy(t, s, lb):
        src = pl.multiple_of(gsrc_ref[t * LOC_BLKS + lb] * BLK, BLK)
        return pltpu.make_async_copy(y_hbm.at[pl.ds(src, BLK), :],
                                     ybuf.at[s, pl.ds(pl.multiple_of(lb * BLK, BLK), BLK), :], sem.at[s])

    def fetch(t, s):
        def start(lb, c):
            block_copy(t, s, lb).start()
            return c
        lax.fori_loop(0, nblk_ref[t], start, 0)

    @pl.when(i == 0)
    def _():
        ybuf[...] = jnp.zeros_like(ybuf)
        fetch(0, 0)

    @pl.when(i + 1 < N_TILES)
    def _():
        fetch(jnp.minimum(i + 1, N_TILES - 1), 1 - slot_i)

    def wait(lb, c):
        block_copy(i, slot_i, lb).wait()
        return c

    lax.fori_loop(0, nblk_ref[i], wait, 0)
    route = route_ref[...]
    yb = ybuf[slot_i]
    slot = lax.broadcasted_iota(jnp.int32, (TM, LOC), 1)
    pick = (jnp.where(slot == route[:, 4:5].astype(jnp.int32), route[:, 2:3], 0.0)
            + jnp.where(slot == route[:, 5:6].astype(jnp.int32), route[:, 3:4], 0.0)).astype(BF16)
    moe = jnp.dot(pick, yb, preferred_element_type=F32)
    m = mod_ref[0]
    res = _layer_norm(ALPHA * x1_ref[...] + m[5:6] * moe, g_ref[layer:layer + 1, :], b_ref[layer:layer + 1, :])

    @pl.when(i < CTX_TILES)
    def _():
        oc_ref[...] = res

    @pl.when(i >= CTX_TILES)
    def _():
        ol_ref[...] = res


def _combine(l, nblk, gsrc, y, route, x1, mod_l, g, b):
    out_shape = (jax.ShapeDtypeStruct((T_CTX, D), F32), jax.ShapeDtypeStruct((T_LAT, D), F32))
    out_specs = tuple(_pair_specs(D))
    grid_spec = pltpu.PrefetchScalarGridSpec(
        num_scalar_prefetch=2,
        grid=(N_TILES,),
        in_specs=[pl.BlockSpec(memory_space=pl.ANY),
                  pl.BlockSpec((TM, 128), lambda i, nb, gs: (i, 0)),
                  pl.BlockSpec((TM, D), lambda i, nb, gs: (i, 0)),
                  _mod_spec(l), _full((DEPTH, D)), _full((DEPTH, D))],
        out_specs=out_specs,
        scratch_shapes=[pltpu.VMEM((2, LOC, D), BF16), pltpu.SemaphoreType.DMA((2,))],
    )
    return pl.pallas_call(
        functools.partial(_combine_kernel, layer=l),
        out_shape=out_shape,
        grid_spec=grid_spec,
        compiler_params=_cparams("arbitrary"),
        name="moe_combine",
    )(nblk, gsrc, y, route, x1, mod_l, g, b)


def _prefix_pick(starts, query, table):
    delta = table - jnp.concatenate([jnp.zeros_like(table[..., :1]), table[..., :-1]], axis=-1)
    return jnp.sum(jnp.where(starts <= query, delta, 0), axis=-1)


def _dispatch_tables(cnt):
    i32 = jnp.int32
    run = (cnt + BLK - 1) // BLK * BLK
    loc_start = jnp.cumsum(run, axis=1) - run
    nblk_loc = jnp.sum(run, axis=1) // BLK
    seg_rows = jnp.sum(run, axis=0)
    seg_tiles = (seg_rows + TM - 1) // TM
    tile_end = jnp.cumsum(seg_tiles)
    seg_start = (tile_end - seg_tiles) * TM
    glob_start = seg_start[None, :] + jnp.cumsum(run, axis=0) - run

    g_blk = (glob_start.T.reshape(1, -1)) // BLK
    n_blk = (run.T.reshape(1, -1)) // BLK
    l_blk = ((jnp.arange(N_TILES, dtype=i32) * LOC)[None, :] + loc_start.T).reshape(1, -1) // BLK
    gb = jnp.arange(MOE_ROWS // BLK, dtype=i32)[:, None]
    off = gb[:, 0] - _prefix_pick(g_blk, gb, g_blk)
    src_blk = jnp.where(off < _prefix_pick(g_blk, gb, n_blk), _prefix_pick(g_blk, gb, l_blk) + off,
                        ZERO_BLK).astype(i32)

    pos = (jnp.arange(LOC_BLKS, dtype=i32) * BLK)[None, :, None]
    shift = _prefix_pick(loc_start[:, None, :], pos, (glob_start - loc_start)[:, None, :])
    gsrc = ((pos[:, :, 0] + shift) // BLK).astype(i32)
    expert_tables = ((tile_end - seg_tiles).astype(i32), seg_tiles.astype(i32), src_blk)
    return expert_tables, nblk_loc.astype(i32), gsrc.reshape(-1)


def _rope_slot(pe):
    return jnp.pad(pe, [(0, 0)] * (pe.ndim - 1) + [(NOPE, SLOT - NOPE - ROPE)])


def _wuq_layout(w):
    w = w.reshape(DEPTH, Q_LORA, HEADS, NOPE + ROPE)
    w = jnp.pad(w, ((0, 0), (0, 0), (0, 0), (0, SLOT - NOPE - ROPE)))
    return w.reshape(DEPTH, Q_LORA, HEADS * SLOT).astype(BF16)


def _wukv_layout(w):
    w = w.reshape(DEPTH, KV_LORA, HEADS, NOPE + VD)
    wk = jnp.pad(w[..., :NOPE], ((0, 0), (0, 0), (0, 0), (0, SLOT - NOPE)))
    wv = w[..., NOPE:].reshape(DEPTH, KV_LORA, HEADS * VD).astype(BF16)
    return wk.reshape(DEPTH, KV_LORA, HEADS * SLOT).astype(BF16), wv, jnp.swapaxes(wv, 1, 2)


def kernel(x_prompt, x_sample, c, cache_ckv, cache_kpe, c_ctx, ln_in_g, ln_in_b, w_mod, b_mod, w_in,
           gm_ln_g, gm_ln_b, gm_ws, gm_bs, hy_conv_w, hy_conv_b, hy_f_w1, hy_f_b1, hy_f_w2, hy_f_b2,
           hy_f_w3, hy_f_freq, hy_bias, mla_gq, mla_gkv, mla_wuq, mla_wukv, w_out, ln1_g, ln1_b,
           ln2_g, ln2_b, moe_w_gr, moe_b_gr, moe_w_er, moe_b_er, moe_w_gate, moe_w_up, moe_w_down):
    rope = _rope_tables()
    dft = {L: _dft_tables(L) for L in (CTX_L, LAT_L)}
    pos_tab = {L: _hyena_positions(L) for L in (CTX_L, LAT_L)}
    hd = np.arange(GM_W) // GM_HD
    avg = jnp.asarray((hd[:, None] == hd[None, :]) / GM_HD, BF16)
    tri = jnp.asarray(np.tril(np.ones((TM, TM)), -1), BF16)
    upper = jnp.asarray(np.triu(np.ones((128, 128)), 1), BF16)

    cond8 = jnp.concatenate([c_ctx[None], c, jnp.zeros((8 - 1 - N_LAT_B, D), F32)], axis=0)
    mod = _modulation(cond8, w_mod, b_mod).reshape(DEPTH, 8, 6, D)

    wuk_all, wuv_all, wuvt_all = _wukv_layout(mla_wukv)
    w_in_t = jnp.swapaxes(w_in, 1, 2)
    wuq_arr = _wuq_layout(mla_wuq)
    ws_b = gm_ws.astype(BF16)
    kc_all, vc_all = _cache_kv(cache_ckv, _rope_slot(cache_kpe), wuk_all, wuvt_all)

    x_pair = (x_prompt.reshape(T_CTX, D), x_sample.reshape(T_LAT, D))
    norm = (ln_in_g.reshape(1, D), ln_in_b.reshape(1, D))
    bs_full = jnp.repeat(jnp.swapaxes(gm_bs, 1, 2), GM_HD, axis=2)
    w_route = jnp.pad(jnp.concatenate([moe_w_gr, moe_w_er.reshape(DEPTH, D, N_EXP)], axis=2),
                      ((0, 0), (0, 0), (0, 128 - N_GROUPS - N_EXP)))
    b_route = jnp.pad(jnp.concatenate([moe_b_gr, moe_b_er.reshape(DEPTH, N_EXP)], axis=1),
                      ((0, 0), (0, 128 - N_GROUPS - N_EXP)))
    hyena_groups = ((CTX_L, N_CTX_B, 4, 0), (LAT_L, N_LAT_B, 2, T_CTX // LAT_L))
    spectra = {L: _filter_spectra(L, *pos_tab[L], dft[L], hy_f_w1, hy_f_b1, hy_f_w2, hy_f_b2, hy_f_w3,
                                  hy_f_freq) for L, _, _, _ in hyena_groups}
    ckv_states, kpe_states = [], []
    for l in range(DEPTH):
        a, hy, q, k, v, vt, ckv, kpe = _in_proj(
            l, x_pair, norm, mod, w_in_t, gm_ln_g, gm_ln_b, avg,
            ws_b, bs_full, mla_gq, mla_gkv, wuq_arr, wuk_all, wuv_all, wuvt_all, rope)
        ckv_states.append(ckv[:T_CTX].reshape(N_CTX_B, CTX_L, KV_LORA))
        kpe_states.append(kpe[:T_CTX, NOPE:NOPE + ROPE].reshape(N_CTX_B, CTX_L, ROPE))

        hyb = [_hyena_group(l, hy, L, nb, group, blk0, dft[L], spectra[L], hy_conv_w, hy_conv_b, hy_bias)
               for L, nb, group, blk0 in hyena_groups]

        att = _attention(l, q, k, v, vt, kc_all, vc_all)

        x1, xloc, route, counts = _out_proj(l, a, hyb, att, w_out, x_pair, norm, mod,
                                            ln1_g, ln1_b, w_route, b_route, tri, upper)
        cnt = counts.reshape(N_TILES, 8, 128)[:, 0, :N_EXP].astype(jnp.int32)
        expert_tables, nblk_loc, gsrc = _dispatch_tables(cnt)
        y = _experts(l, expert_tables, xloc, moe_w_gate, moe_w_up, moe_w_down)
        x_pair = _combine(l, nblk_loc, gsrc, y, route, x1, mod, ln2_g, ln2_b)

    y_prompt = x_pair[0].reshape(N_CTX_B, CTX_L, D)
    y_sample = x_pair[1].reshape(N_LAT_B, LAT_L, D)
    return (y_prompt, y_sample, jnp.stack(ckv_states, axis=1), jnp.stack(kpe_states, axis=1))
```

```python
import functools
import math

import numpy as np
import jax
import jax.numpy as jnp
from jax import lax
from jax.experimental import pallas as pl
from jax.experimental.pallas import tpu as pltpu

D = 1024
N_CTX_B, CTX_L = 16, 256
N_LAT_B, LAT_L = 4, 1024
DEPTH = 2
T_CTX = N_CTX_B * CTX_L
T_LAT = N_LAT_B * LAT_L
T = T_CTX + T_LAT
PAST = 256
GRID_W = 64

GM_HEADS, GM_HD, GM_W, CHUNK = 4, 64, 256, 128
HY_W, HY_EMB, HY_FH = 256, 33, 64
NOPE, ROPE, VD, HEADS = 64, 32, 64, 8
Q_LORA, KV_LORA = 256, 128
SLOT = 128
N_GROUPS, EPG, N_EXP, D_EXP = 4, 8, 32, 256
ALPHA = (2.0 * DEPTH) ** 0.25
LN_EPS, RMS_EPS = 1e-5, 1e-6
ROPE_THETA = 10000.0

TM = 256
N_TILES = T // TM
CTX_TILES = T_CTX // TM
LAT_TILES_PER_B = LAT_L // TM
N_ASSIGN = 2 * T
BLK = 16
BLK_PER_TILE = TM // BLK
LOC = 2 * TM + N_EXP * BLK
LOC_BLKS = LOC // BLK
ZERO_BLK = LOC_BLKS - 1
MOE_TILES = (N_ASSIGN + N_TILES * N_EXP * (BLK - 1)) // TM + N_EXP
MOE_ROWS = MOE_TILES * TM
IN_COLS = 1696
IN_EXT = 1792

F32, BF16 = jnp.float32, jnp.bfloat16
_NT = (((1,), (1,)), ((), ()))
VMEM_LIMIT = 52 * 1024 * 1024


def _cparams(*sem):
    return pltpu.CompilerParams(dimension_semantics=sem, vmem_limit_bytes=VMEM_LIMIT)


def _cond_of_tile(i):
    return jnp.where(i < CTX_TILES, 0, 1 + (i - CTX_TILES) // LAT_TILES_PER_B)


def _rope_block_of_tile(i):
    return jnp.where(i < CTX_TILES, 0, 1 + (i - CTX_TILES) % LAT_TILES_PER_B)


def _full(shape):
    n = len(shape)
    return pl.BlockSpec(shape, lambda *_: (0,) * n)


def _layer_spec(l, *shape):
    return pl.BlockSpec((None,) + shape, lambda *_: (l,) + (0,) * len(shape))


def _mod_spec(l):
    return pl.BlockSpec((None, 1, 6, D), lambda i, *_: (l, _cond_of_tile(i), 0, 0))


def _pair_specs(n):
    return [pl.BlockSpec((TM, n), lambda i, *_: (jnp.minimum(i, CTX_TILES - 1), 0)),
            pl.BlockSpec((TM, n), lambda i, *_: (jnp.maximum(i - CTX_TILES, 0), 0))]


def _pair_read(c_ref, l_ref):
    return jnp.where(pl.program_id(0) < CTX_TILES, c_ref[...], l_ref[...])


def _layer_norm(x, g, b):
    mu = jnp.mean(x, -1, keepdims=True)
    xc = x - mu
    var = jnp.mean(xc * xc, -1, keepdims=True)
    return xc * lax.rsqrt(var + LN_EPS) * g + b


def _rms_norm(x, g):
    return x * lax.rsqrt(jnp.mean(x * x, -1, keepdims=True) + RMS_EPS) * g


def _bdot(a, b):
    return jnp.dot(a.astype(BF16), b.astype(BF16), preferred_element_type=F32)


def _dot3(x, w):
    x_hi = x.astype(BF16)
    x_lo = (x - x_hi.astype(F32)).astype(BF16)
    w_hi = w.astype(BF16)
    w_lo = (w - w_hi.astype(F32)).astype(BF16)
    dot = functools.partial(jnp.dot, preferred_element_type=F32)
    return dot(x_hi, w_hi) + dot(x_hi, w_lo) + dot(x_lo, w_hi)


def _swap_pair_lanes(x):
    n = x.shape[1]
    lane = lax.broadcasted_iota(jnp.int32, x.shape, 1)
    return jnp.where(lane % 2 == 0, pltpu.roll(x, n - 1, 1), pltpu.roll(x, 1, 1))


def _split2_dot(x, w_bf16):
    hi = x.astype(BF16)
    lo = (x - hi.astype(F32)).astype(BF16)
    dot = functools.partial(jnp.dot, preferred_element_type=F32)
    return dot(hi, w_bf16) + dot(lo, w_bf16)


def _dft_tables(L):
    f = np.arange(L, dtype=np.int64)
    ft = np.outer(f, f) % (2 * L)
    ang = np.pi * ft / L
    c = np.cos(ang)
    s = np.sin(ang)
    alt = np.where(f % 2 == 0, 1.0, -1.0)
    sf = s.copy()
    sf[0, :] = alt
    return (jnp.asarray(c, F32).astype(BF16), jnp.asarray(sf, F32).astype(BF16),
            jnp.asarray(sf.T.copy(), F32).astype(BF16), jnp.asarray(alt[:, None], F32))


def _rope_tables():
    rows = LAT_L // GRID_W
    row = np.repeat(np.arange(rows, dtype=np.float64), GRID_W)
    col = np.tile(np.arange(GRID_W, dtype=np.float64), rows)
    n = ROPE // 4
    inv = ROPE_THETA ** (-np.arange(n, dtype=np.float64) / n)
    ang = np.concatenate([row[:, None] * inv, col[:, None] * inv], -1)
    cos = np.zeros((CTX_L + LAT_L, SLOT))
    sin = np.zeros((CTX_L + LAT_L, SLOT))
    cos[:, :NOPE + ROPE] = 1.0
    cos[CTX_L:, NOPE:NOPE + ROPE:2] = np.cos(ang)
    cos[CTX_L:, NOPE + 1:NOPE + ROPE:2] = np.cos(ang)
    sin[CTX_L:, NOPE:NOPE + ROPE:2] = -np.sin(ang)
    sin[CTX_L:, NOPE + 1:NOPE + ROPE:2] = np.sin(ang)
    scale = math.log2(math.e) / math.sqrt(NOPE + ROPE)
    return (jnp.asarray(cos * scale, F32), jnp.asarray(sin * scale, F32),
            jnp.asarray(cos, F32), jnp.asarray(sin, F32))


def _hyena_positions(L):
    t = jnp.linspace(0.0, 1.0, L, dtype=F32)[:, None]
    bands = (HY_EMB - 1) // 2
    w = 2.0 * math.pi * jnp.arange(L, dtype=F32)[:, None] / L
    f = jnp.linspace(1e-4, bands - 1, bands, dtype=F32)[None]
    z = jnp.concatenate([t, jnp.cos(f * w), -jnp.sin(f * w)], -1)
    z = jnp.pad(z, ((0, 0), (0, 128 - HY_EMB)))
    min_decay = math.log(1e-2) / 1.5
    max_decay = math.log(1e-2) / 0.3
    deltas = jnp.abs(jnp.linspace(min_decay, max_decay, HY_W, dtype=F32))
    return z, jnp.exp(-t * deltas)


def _mod_kernel(c_ref, w_ref, b_ref, o_ref):
    c = c_ref[...]
    s = c / (1.0 + jnp.exp(-c))
    o_ref[0] = _bdot(s, w_ref[0]) + b_ref[0]


def _modulation(cond8, w_mod, b_mod):
    tn = 1536
    return pl.pallas_call(
        _mod_kernel,
        out_shape=jax.ShapeDtypeStruct((DEPTH, 8, 6 * D), F32),
        grid=(DEPTH, 6 * D // tn),
        in_specs=[pl.BlockSpec((8, D), lambda l, j: (0, 0)),
                  pl.BlockSpec((1, D, tn), lambda l, j: (l, 0, j)),
                  pl.BlockSpec((1, 1, tn), lambda l, j: (l, 0, j))],
        out_specs=pl.BlockSpec((1, 8, tn), lambda l, j: (l, 0, j)),
        compiler_params=_cparams("parallel", "parallel"),
        name="modulation",
    )(cond8, w_mod, b_mod.reshape(DEPTH, 1, 6 * D))


def _filter_kernel(z_ref, w1_ref, b1_ref, w2_ref, b2_ref, w3_ref, fr_ref, dec_ref, c_ref, sf_ref,
                   alt_ref, kr_ref, ki_ref, krn_ref, *, L):
    h1 = jnp.sin(fr_ref[0:1] * (_dot3(z_ref[...], w1_ref[...]) + b1_ref[...]))
    h2 = jnp.sin(fr_ref[1:2] * (_dot3(h1, w2_ref[...]) + b2_ref[...]))
    filt = _dot3(h2, w3_ref[...])
    row = lax.broadcasted_iota(jnp.int32, (L, HY_W), 0)
    dec = dec_ref[...]
    n = DEPTH * 2
    fwd = [filt[:, j * 2 * HY_W:j * 2 * HY_W + HY_W] * dec for j in range(n)]
    bwd = [jnp.where(row == 0, 0.0, filt[:, j * 2 * HY_W + HY_W:(j + 1) * 2 * HY_W] * dec) for j in range(n)]
    sm = [f + b for f, b in zip(fwd, bwd)]
    kr_all = _bdot(c_ref[...], jnp.concatenate(sm, axis=1))
    ki_all = -_bdot(sf_ref[...], jnp.concatenate([f - b for f, b in zip(fwd, bwd)], axis=1))
    for j in range(n):
        l, o = divmod(j, 2)
        kr = kr_all[:, j * HY_W:(j + 1) * HY_W]
        kr_nyq = jnp.sum(alt_ref[...] * sm[j], axis=0, keepdims=True)
        kr_ref[l, o] = jnp.where(row == 0, kr * (0.5 / L), kr * (1.0 / L))
        ki_ref[l, o] = jnp.where(row == 0, 0.0, ki_all[:, j * HY_W:(j + 1) * HY_W] * (1.0 / L))
        krn_ref[l, o] = jnp.where(row == 0, kr_nyq * (0.5 / L), kr * (1.0 / L))


def _block_diag(blocks):
    rows = []
    for i, blk in enumerate(blocks):
        rows.append(jnp.concatenate([blk if j == i else jnp.zeros((blk.shape[0], other.shape[1]), blk.dtype)
                                     for j, other in enumerate(blocks)], axis=1))
    return jnp.concatenate(rows, axis=0)


def _filter_spectra(L, z, dec, dft, w1, b1, w2, b2, w3, freq):
    assert DEPTH * HY_FH == 128
    c, sf, _, alt = dft
    layers = range(DEPTH)
    w1c = jnp.pad(jnp.concatenate([w1[l] for l in layers], axis=1), ((0, 128 - HY_EMB), (0, 0)))
    w2c = _block_diag([w2[l] for l in layers])
    w3c = _block_diag([w3[l] for l in layers])
    b1c, b2c = b1.reshape(1, 128), b2.reshape(1, 128)
    frc = jnp.swapaxes(freq, 0, 1).reshape(2, 128)
    shp = jax.ShapeDtypeStruct((DEPTH, 2, L, HY_W), F32)
    args = (z, w1c, b1c, w2c, b2c, w3c, frc, dec, c, sf, alt)
    return pl.pallas_call(
        functools.partial(_filter_kernel, L=L),
        out_shape=(shp, shp, shp),
        grid=(1,),
        in_specs=[_full(a.shape) for a in args],
        out_specs=(_full((DEPTH, 2, L, HY_W)),) * 3,
        compiler_params=_cparams("arbitrary"),
        name=f"hyena_filter_{L}",
    )(*args)


def _in_kernel(xc_ref, xl_ref, ng_ref, nb_ref, mod_ref, wt_ref, gg_ref, gb_ref, avg_ref, ws_ref, bs_ref,
               gq_ref, gkv_ref, wuq_ref, wuk_ref, wuv_ref, wuvt_ref, cq_ref, sq_ref, ck_ref, sk_ref,
               a_ref, hy_ref, q_ref, k_ref, v_ref, vt_ref, ckv_ref, kpe_ref, w_b, *, layer):
    l = layer

    @pl.when(pl.program_id(0) == 0)
    def _():
        for c0 in range(0, 1664, 128):
            w_b[:, c0:c0 + 128] = jnp.transpose(wt_ref[c0:c0 + 128, :]).astype(BF16)
        slot_rows = jnp.concatenate([jnp.zeros((NOPE, D), F32), wt_ref[1664:IN_COLS, :],
                                     jnp.zeros((SLOT - NOPE - ROPE, D), F32)], axis=0)
        w_b[:, 1664:IN_EXT] = jnp.transpose(slot_rows).astype(BF16)

    x = _pair_read(xc_ref, xl_ref)
    if l == 0:
        x = _layer_norm(x, ng_ref[...], nb_ref[...])
    m = mod_ref[0]
    h = x * (1.0 + m[1:2]) + m[0:1]
    proj = _bdot(h, w_b[...])

    gu = jax.nn.gelu(proj[:, 0:GM_W], approximate=True)
    gv = jax.nn.gelu(proj[:, GM_W:2 * GM_W], approximate=True)
    avg = avg_ref[...]
    mu = _split2_dot(gv, avg)
    vc = gv - mu
    var = _split2_dot(vc * vc, avg)
    vln = (vc * lax.rsqrt(var + LN_EPS) * gg_ref[l:l + 1, :] + gb_ref[l:l + 1, :]).astype(BF16)
    lane = lax.broadcasted_iota(jnp.int32, (CHUNK, GM_W), 1)
    for c in range(TM // CHUNK):
        vchunk = vln[c * CHUNK:(c + 1) * CHUNK]
        s = bs_ref[...]
        for hd in range(GM_HEADS):
            sh = jnp.dot(ws_ref[0, hd], vchunk, preferred_element_type=F32)
            s = s + jnp.where(lane // GM_HD == hd, sh, 0.0)
        a_ref[c * CHUNK:(c + 1) * CHUNK, :] = (gu[c * CHUNK:(c + 1) * CHUNK] * s).astype(BF16)

    hy_ref[...] = proj[:, 512:1280]

    cq = _rms_norm(proj[:, 1280:1536], gq_ref[l:l + 1, :])
    qq = _bdot(cq, wuq_ref[0])
    cos_q = jnp.concatenate([cq_ref[...]] * HEADS, axis=1)
    sin_q = jnp.concatenate([sq_ref[...]] * HEADS, axis=1)
    q_ref[...] = (qq * cos_q + _swap_pair_lanes(qq) * sin_q).astype(BF16)

    ckv = _rms_norm(proj[:, 1536:1664], gkv_ref[l:l + 1, :])
    ckv_ref[...] = ckv
    kpe = proj[:, 1664:IN_EXT]
    kpe_ref[...] = kpe
    krot = kpe * ck_ref[...] + _swap_pair_lanes(kpe) * sk_ref[...]
    kn = _bdot(ckv, wuk_ref[0])
    k_ref[...] = (kn + jnp.concatenate([krot] * HEADS, axis=1)).astype(BF16)
    v_ref[...] = _bdot(ckv, wuv_ref[0]).astype(BF16)
    vt_ref[...] = lax.dot_general(wuvt_ref[0], ckv.astype(BF16), _NT,
                                  preferred_element_type=F32).astype(BF16)


def _in_proj(l, x_pair, norm, mod_l, w_in_t, gm_g, gm_b, avg, ws, bs_full, gq, gkv, wuq_arr,
             wuk_arr, wuv_arr, wuvt_arr, rope):
    layer = lambda *shape: pl.BlockSpec((1,) + shape, lambda i: (l,) + (0,) * len(shape))
    cos_q, sin_q, cos_k, sin_k = rope
    tile = lambda n: pl.BlockSpec((TM, n), lambda i: (i, 0))
    rope_spec = pl.BlockSpec((TM, SLOT), lambda i: (_rope_block_of_tile(i), 0))
    out_shapes = (jax.ShapeDtypeStruct((T, GM_W), BF16),
                  jax.ShapeDtypeStruct((T, 3 * HY_W), F32),
                  jax.ShapeDtypeStruct((T, HEADS * SLOT), BF16),
                  jax.ShapeDtypeStruct((T, HEADS * SLOT), BF16),
                  jax.ShapeDtypeStruct((T, HEADS * VD), BF16),
                  jax.ShapeDtypeStruct((HEADS * VD, T), BF16),
                  jax.ShapeDtypeStruct((T, KV_LORA), F32),
                  jax.ShapeDtypeStruct((T, SLOT), F32))
    return pl.pallas_call(
        functools.partial(_in_kernel, layer=l),
        out_shape=out_shapes,
        grid=(N_TILES,),
        in_specs=_pair_specs(D) + [
                  _full((1, D)), _full((1, D)),
                  _mod_spec(l),
                  pl.BlockSpec((None, IN_COLS, D), lambda i: (l, 0, 0), pipeline_mode=pl.Buffered(1)),
                  _full((DEPTH, GM_W)), _full((DEPTH, GM_W)), _full((GM_W, GM_W)),
                  layer(GM_HEADS, CHUNK, CHUNK), _layer_spec(l, CHUNK, GM_W),
                  _full((DEPTH, Q_LORA)), _full((DEPTH, KV_LORA)),
                  layer(Q_LORA, HEADS * SLOT), layer(KV_LORA, HEADS * SLOT),
                  layer(KV_LORA, HEADS * VD), layer(HEADS * VD, KV_LORA),
                  rope_spec, rope_spec, rope_spec, rope_spec],
        out_specs=(tile(GM_W), tile(3 * HY_W), tile(HEADS * SLOT), tile(HEADS * SLOT), tile(HEADS * VD),
                   pl.BlockSpec((HEADS * VD, TM), lambda i: (0, i)), tile(KV_LORA), tile(SLOT)),
        scratch_shapes=[pltpu.VMEM((D, IN_EXT), BF16)],
        compiler_params=_cparams("arbitrary"),
        name="in_proj",
    )(*x_pair, *norm, mod_l, w_in_t, gm_g, gm_b, avg, ws, bs_full, gq, gkv, wuq_arr, wuk_arr, wuv_arr,
      wuvt_arr, cos_q, sin_q, cos_k, sin_k)


def _cache_kernel(ckv_ref, kpe_ref, wuk_ref, wuv_ref, k_ref, v_ref):
    ckv = ckv_ref[0, 0]
    kn = _bdot(ckv, wuk_ref[0])
    k_ref[0, 0] = (kn + jnp.concatenate([kpe_ref[0, 0]] * HEADS, axis=1)).astype(BF16)
    v_ref[0, 0] = lax.dot_general(wuv_ref[0], ckv.astype(BF16), _NT,
                                  preferred_element_type=F32).astype(BF16)


def _cache_kv(cache_ckv, cache_kpe_slot, wuk_arr, wuv_arr):
    return pl.pallas_call(
        _cache_kernel,
        out_shape=(jax.ShapeDtypeStruct((DEPTH, N_LAT_B, PAST, HEADS * SLOT), BF16),
                   jax.ShapeDtypeStruct((DEPTH, N_LAT_B, HEADS * VD, PAST), BF16)),
        grid=(DEPTH, N_LAT_B),
        in_specs=[pl.BlockSpec((1, 1, PAST, KV_LORA), lambda l, b: (b, l, 0, 0)),
                  pl.BlockSpec((1, 1, PAST, SLOT), lambda l, b: (b, l, 0, 0)),
                  pl.BlockSpec((1, KV_LORA, HEADS * SLOT), lambda l, b: (l, 0, 0)),
                  pl.BlockSpec((1, HEADS * VD, KV_LORA), lambda l, b: (l, 0, 0))],
        out_specs=(pl.BlockSpec((1, 1, PAST, HEADS * SLOT), lambda l, b: (l, b, 0, 0)),
                   pl.BlockSpec((1, 1, HEADS * VD, PAST), lambda l, b: (l, b, 0, 0))),
        compiler_params=_cparams("parallel", "parallel"),
        name="cache_kv",
    )(cache_ckv, cache_kpe_slot, wuk_arr, wuv_arr)


def _attend_rows(q_ref, k_ref, v_ref, o_ref):
    lq = q_ref.shape[0]
    lane = lax.broadcasted_iota(jnp.int32, (lq, 2 * VD), 1)
    for pair in range(HEADS // 2):
        outs = []
        for hd in (2 * pair, 2 * pair + 1):
            s = lax.dot_general(q_ref[:, hd * SLOT:(hd + 1) * SLOT], k_ref[:, hd * SLOT:(hd + 1) * SLOT], _NT,
                                preferred_element_type=F32)
            p = jnp.exp2(s - jnp.max(s, -1, keepdims=True))
            acc = jnp.dot(p.astype(BF16), v_ref[:, pair * 2 * VD:(pair + 1) * 2 * VD],
                          preferred_element_type=F32)
            outs.append(acc / jnp.sum(p, -1, keepdims=True))
        o_ref[:, pair * 2 * VD:(pair + 1) * 2 * VD] = jnp.where(lane < VD, outs[0], outs[1]).astype(BF16)


def _attend_cols(q_ref, segments, o_ref):
    lq = q_ref.shape[0]
    lane = lax.broadcasted_iota(jnp.int32, (lq, 2 * SLOT), 1)
    row = lax.broadcasted_iota(jnp.int32, (2 * VD, lq), 0)
    outs = []
    for pair in range(HEADS // 2):
        qp = q_ref[:, pair * 2 * SLOT:(pair + 1) * 2 * SLOT]
        zero = jnp.zeros_like(qp)
        q_bd = jnp.concatenate([jnp.where(lane < SLOT, qp, zero), jnp.where(lane < SLOT, zero, qp)], axis=0)
        scores = [lax.dot_general(k_ref[:, pair * 2 * SLOT:(pair + 1) * 2 * SLOT], q_bd, _NT,
                                  preferred_element_type=F32) for k_ref, _ in segments]
        mx = functools.reduce(jnp.maximum, [jnp.max(s, 0, keepdims=True) for s in scores])
        ps = [jnp.exp2(s - mx) for s in scores]
        den = functools.reduce(jnp.add, [jnp.sum(p, 0, keepdims=True) for p in ps])
        acc = functools.reduce(jnp.add, [
            jnp.dot(vt_ref[pair * 2 * VD:(pair + 1) * 2 * VD, :], p.astype(BF16),
                    preferred_element_type=F32) for p, (_, vt_ref) in zip(ps, segments)])
        acc = acc / den
        outs.append(jnp.where(row < VD, acc[:, :lq], acc[:, lq:]))
    o_ref[...] = jnp.transpose(jnp.concatenate(outs, axis=0)).astype(BF16)


def _attn_ctx_kernel(q_ref, k_ref, v_ref, o_ref):
    _attend_rows(q_ref, k_ref, v_ref, o_ref)


def _attn_lat_kernel(q_ref, kc_ref, vc_ref, k_ref, vt_ref, o_ref):
    _attend_cols(q_ref, [(kc_ref.at[0], vc_ref.at[0]), (k_ref, vt_ref)], o_ref)


def _attention(l, q, k, v, vt, kc, vc):
    kw, vw = HEADS * SLOT, HEADS * VD
    ctx = pl.pallas_call(
        _attn_ctx_kernel,
        out_shape=jax.ShapeDtypeStruct((T_CTX, vw), BF16),
        grid=(N_CTX_B,),
        in_specs=[pl.BlockSpec((CTX_L, kw), lambda b: (b, 0)),
                  pl.BlockSpec((CTX_L, kw), lambda b: (b, 0)),
                  pl.BlockSpec((CTX_L, vw), lambda b: (b, 0))],
        out_specs=pl.BlockSpec((CTX_L, vw), lambda b: (b, 0)),
        compiler_params=_cparams("parallel"),
        name="attn_ctx",
    )(q, k, v)
    nq = LAT_L // TM
    off = T_CTX // LAT_L
    lat = pl.pallas_call(
        _attn_lat_kernel,
        out_shape=jax.ShapeDtypeStruct((T_LAT, vw), BF16),
        grid=(N_LAT_B, nq),
        in_specs=[pl.BlockSpec((TM, kw), lambda b, j: (CTX_TILES + b * nq + j, 0)),
                  pl.BlockSpec((None, 1, PAST, kw), lambda b, j: (l, b, 0, 0)),
                  pl.BlockSpec((None, 1, vw, PAST), lambda b, j: (l, b, 0, 0)),
                  pl.BlockSpec((LAT_L, kw), lambda b, j: (off + b, 0)),
                  pl.BlockSpec((vw, LAT_L), lambda b, j: (0, off + b))],
        out_specs=pl.BlockSpec((TM, vw), lambda b, j: (b * nq + j, 0)),
        compiler_params=_cparams("parallel", "parallel"),
        name="attn_lat",
    )(q, kc, vc, k, vt)
    return ctx, lat


def _hyena_kernel(hy_ref, cw_ref, cb_ref, c_ref, sf_ref, sb_ref, kr_ref, ki_ref, krn_ref, hb_ref,
                  o_ref, *, L, group, layer):
    row = lax.broadcasted_iota(jnp.int32, (L, 3 * HY_W), 0)
    zs = []
    for j in range(group):
        x = hy_ref[j * L:(j + 1) * L, :]
        prev = jnp.where(row == 0, 0.0, pltpu.roll(x, 1, 0))
        nxt = jnp.where(row == L - 1, 0.0, pltpu.roll(x, L - 1, 0))
        zs.append(prev * cw_ref[0:1] + x * cw_ref[1:2] + nxt * cw_ref[2:3] + cb_ref[layer:layer + 1, :])
    side = lambda parts: jnp.concatenate(parts, axis=1)
    y = side([z[:, 2 * HY_W:] for z in zs])
    for o in range(2):
        gate = side([z[:, o * HY_W:(o + 1) * HY_W] for z in zs])
        kr, ki, krn = (side([r[o]] * group) for r in (kr_ref, ki_ref, krn_ref))
        yb = y.astype(BF16)
        a_re = jnp.dot(c_ref[...], yb, preferred_element_type=F32)
        a_im = jnp.dot(sf_ref[...], yb, preferred_element_type=F32)
        z_re = a_re * kr + a_im * ki
        z_im = a_im * krn - a_re * ki
        conv = (jnp.dot(c_ref[...], z_re.astype(BF16), preferred_element_type=F32)
                + jnp.dot(sb_ref[...], z_im.astype(BF16), preferred_element_type=F32))
        y = gate * (conv + y * side([hb_ref[o:o + 1]] * group))
    for j in range(group):
        o_ref[j * L:(j + 1) * L, :] = y[:, j * HY_W:(j + 1) * HY_W].astype(BF16)


def _hyena_group(l, hy, L, nb, group, blk0, dft, spectra, conv_w, conv_b, hy_bias):
    c, sf, sb, _ = dft
    kr, ki, krn = spectra
    const = lambda shape: pl.BlockSpec(shape, lambda b: (0,) * len(shape), pipeline_mode=pl.Buffered(1))
    spec = pl.BlockSpec((None, 2, L, HY_W), lambda b: (l, 0, 0, 0), pipeline_mode=pl.Buffered(1))
    return pl.pallas_call(
        functools.partial(_hyena_kernel, L=L, group=group, layer=l),
        out_shape=jax.ShapeDtypeStruct((nb * L, HY_W), BF16),
        grid=(nb // group,),
        in_specs=[pl.BlockSpec((group * L, 3 * HY_W), lambda b: (blk0 // group + b, 0)),
                  _layer_spec(l, 3, 3 * HY_W), _full((DEPTH, 3 * HY_W)),
                  const((L, L)), const((L, L)), const((L, L)), spec, spec, spec,
                  _layer_spec(l, 2, HY_W)],
        out_specs=pl.BlockSpec((group * L, HY_W), lambda b: (b, 0)),
        compiler_params=_cparams("parallel"),
        name=f"hyena_{L}",
    )(hy, conv_w, conv_b, c, sf, sb, kr, ki, krn, hy_bias)


def _out_kernel(a_ref, bc_ref, bl_ref, mc_ref, ml_ref, w_ref, xc_ref, xl_ref, ng_ref, nb_ref, mod_ref,
                g_ref, be_ref, wr_ref, br_ref, tri_ref, upper_ref, x1_ref, xloc_ref, route_ref, cnt_ref,
                w_b, *, layer):
    l = layer

    @pl.when(pl.program_id(0) == 0)
    def _():
        w_b[...] = w_ref[...].astype(BF16)

    dot = functools.partial(jnp.dot, preferred_element_type=F32)
    mixed = (dot(a_ref[...], w_b[0:GM_W]) + dot(_pair_read(bc_ref, bl_ref), w_b[GM_W:GM_W + HY_W])
             + dot(_pair_read(mc_ref, ml_ref), w_b[GM_W + HY_W:]))
    x = _pair_read(xc_ref, xl_ref)
    if l == 0:
        x = _layer_norm(x, ng_ref[...], nb_ref[...])
    m = mod_ref[0]
    x1 = _layer_norm(ALPHA * x + m[2:3] * mixed, g_ref[l:l + 1, :], be_ref[l:l + 1, :])
    x1_ref[...] = x1
    h2 = x1 * (1.0 + m[4:5]) + m[3:4]

    h_hi = h2.astype(BF16)
    h_lo = (h2 - h_hi.astype(F32)).astype(BF16)
    w_hi = wr_ref[...].astype(BF16)
    w_lo = (wr_ref[...] - w_hi.astype(F32)).astype(BF16)
    part = dot(h_hi, jnp.concatenate([w_hi, w_lo], axis=1))
    logits = part[:, :128] + part[:, 128:] + dot(h_lo, w_hi) + br_ref[l:l + 1, :]
    lane = lax.broadcasted_iota(jnp.int32, logits.shape, 1)
    lanef = lane.astype(F32)
    big = jnp.float32(1e9)
    ninf = jnp.float32(-jnp.inf)
    is_g = lane < N_GROUPS
    gl = jnp.where(is_g, logits, ninf)
    gmax = jnp.max(gl, -1, keepdims=True)
    gidx = jnp.min(jnp.where(gl == gmax, lanef, big), -1, keepdims=True)
    gw = 1.0 / jnp.sum(jnp.where(is_g, jnp.exp(logits - gmax), 0.0), -1, keepdims=True)
    ex = lane - N_GROUPS
    in_group = (ex >= 0) & (ex < N_EXP) & ((ex // EPG).astype(F32) == gidx)
    el = jnp.where(in_group, logits, ninf)
    v1 = jnp.max(el, -1, keepdims=True)
    i1 = jnp.min(jnp.where(el == v1, lanef, big), -1, keepdims=True)
    el2 = jnp.where(lanef == i1, ninf, el)
    v2 = jnp.max(el2, -1, keepdims=True)
    i2 = jnp.min(jnp.where(el2 == v2, lanef, big), -1, keepdims=True)
    e21 = jnp.exp(v2 - v1)
    w1 = gw / (1.0 + e21)
    w2 = gw * e21 / (1.0 + e21)
    e1 = i1 - N_GROUPS
    e2 = i2 - N_GROUPS

    oh1 = jnp.where(lanef == e1, 1.0, 0.0)
    oh2 = jnp.where(lanef == e2, 1.0, 0.0)
    ex1 = dot(tri_ref[...], oh1.astype(BF16))
    ex2 = dot(tri_ref[...], oh2.astype(BF16))
    col1 = jnp.sum(oh1, axis=0, keepdims=True)
    col2 = jnp.sum(oh2, axis=0, keepdims=True)
    n = col1 + col2
    run = jnp.floor((n + (BLK - 1.0)) * (1.0 / BLK)) * BLK
    start = dot(jnp.broadcast_to(run, (8, 128)).astype(BF16), upper_ref[...])[0:1]
    loc1 = jnp.sum(oh1 * (start + ex1), -1, keepdims=True)
    loc2 = jnp.sum(oh2 * (start + col1 + ex2), -1, keepdims=True)
    cnt_ref[...] = jnp.broadcast_to(n, cnt_ref.shape)

    route = jnp.zeros_like(logits)
    for j, val in enumerate((e1, e2, w1, w2, loc1, loc2)):
        route = jnp.where(lane == j, val, route)
    route_ref[...] = route

    loc1_row = jnp.transpose(jnp.broadcast_to(loc1, (TM, 128)))[0:1].astype(jnp.int32)
    loc2_row = jnp.transpose(jnp.broadcast_to(loc2, (TM, 128)))[0:1].astype(jnp.int32)
    slot = lax.broadcasted_iota(jnp.int32, (LOC, TM), 0)
    perm = jnp.where((slot == loc1_row) | (slot == loc2_row), 1.0, 0.0).astype(BF16)
    xloc_ref[...] = dot(perm, h_hi).astype(BF16)


def _out_proj(l, a, hyb_pair, att_pair, w_out, x_pair, norm, mod_l, g, b, w_route, b_route, tri, upper):
    tile = lambda n: pl.BlockSpec((TM, n), lambda i: (i, 0))
    return pl.pallas_call(
        functools.partial(_out_kernel, layer=l),
        out_shape=(jax.ShapeDtypeStruct((T, D), F32), jax.ShapeDtypeStruct((N_TILES * LOC, D), BF16),
                   jax.ShapeDtypeStruct((T, 128), F32), jax.ShapeDtypeStruct((N_TILES * 8, 128), F32)),
        grid=(N_TILES,),
        in_specs=[tile(GM_W)] + _pair_specs(HY_W) + _pair_specs(HEADS * VD)
        + [pl.BlockSpec((None, D, D), lambda i: (l, 0, 0), pipeline_mode=pl.Buffered(1))]
        + _pair_specs(D) + [
                  _full((1, D)), _full((1, D)),
                  _mod_spec(l),
                  _full((DEPTH, D)), _full((DEPTH, D)), _layer_spec(l, D, 128), _full((DEPTH, 128)),
                  _full((TM, TM)), _full((128, 128))],
        out_specs=(tile(D), pl.BlockSpec((LOC, D), lambda i: (i, 0)), tile(128),
                   pl.BlockSpec((8, 128), lambda i: (i, 0))),
        scratch_shapes=[pltpu.VMEM((D, D), BF16)],
        compiler_params=_cparams("arbitrary"),
        name="out_proj_route",
    )(a, *hyb_pair, *att_pair, w_out, *x_pair, *norm, mod_l, g, b, w_route, b_route, tri, upper)


X_RING = 8


def _expert_kernel(off_ref, cnt_ref, src_ref, x_hbm, wg_ref, wu_ref, wd_ref, y_hbm,
                   xbuf, ybuf, zbuf, wgu_b, wd_b, sem, osem, zsem):
    e = pl.program_id(0)
    first_tile = off_ref[e]
    n_tiles = cnt_ref[e]
    total = off_ref[N_EXP - 1] + cnt_ref[N_EXP - 1]

    def block_copies(t, s):
        return [pltpu.make_async_copy(
            x_hbm.at[pl.ds(pl.multiple_of(src_ref[t * BLK_PER_TILE + b] * BLK, BLK), BLK), :],
            xbuf.at[s, pl.ds(b * BLK, BLK), :], sem.at[s]) for b in range(BLK_PER_TILE)]

    def out_copy(t, s):
        return pltpu.make_async_copy(ybuf.at[s], y_hbm.at[pl.ds(pl.multiple_of(t * TM, TM), TM), :],
                                     osem.at[s])

    def zero_copy(t):
        return pltpu.make_async_copy(zbuf, y_hbm.at[pl.ds(pl.multiple_of(t * TM, TM), TM), :], zsem)

    @pl.when(e == 0)
    def _():
        zbuf[...] = jnp.zeros((TM, D), BF16)

        def zero_start(t, carry):
            zero_copy(t).start()
            return carry

        lax.fori_loop(total, MOE_TILES, zero_start, 0)
        for t0 in range(X_RING - 1):
            @pl.when(t0 < total)
            def _():
                for cp in block_copies(t0, t0):
                    cp.start()

    @pl.when(n_tiles > 0)
    def _():
        wgu_b[:, 0:D_EXP] = wg_ref[0, 0].astype(BF16)
        wgu_b[:, D_EXP:2 * D_EXP] = wu_ref[0, 0].astype(BF16)
        wd_b[...] = wd_ref[0, 0].astype(BF16)

    def tile_body(t, carry):
        s = t % 2
        xs = t % X_RING

        @pl.when(t + X_RING - 1 < total)
        def _():
            for cp in block_copies(t + X_RING - 1, (t + X_RING - 1) % X_RING):
                cp.start()

        for cp in block_copies(t, xs):
            cp.wait()

        @pl.when(t >= 2)
        def _():
            out_copy(t - 2, s).wait()

        gu = jnp.dot(xbuf[xs], wgu_b[...], preferred_element_type=F32)
        gate, up = gu[:, :D_EXP], gu[:, D_EXP:]
        hid = gate / (1.0 + jnp.exp(-gate)) * up
        ybuf[s] = jnp.dot(hid.astype(BF16), wd_b[...], preferred_element_type=F32).astype(BF16)
        out_copy(t, s).start()
        return carry

    lax.fori_loop(first_tile, first_tile + n_tiles, tile_body, 0)

    @pl.when(e == N_EXP - 1)
    def _():
        @pl.when(total >= 2)
        def _():
            out_copy(total - 2, total % 2).wait()

        out_copy(total - 1, (total - 1) % 2).wait()

        def zero_wait(t, carry):
            zero_copy(t).wait()
            return carry

        lax.fori_loop(total, MOE_TILES, zero_wait, 0)


def _experts(l, tables, xloc, w_gate, w_up, w_down):
    grid_spec = pltpu.PrefetchScalarGridSpec(
        num_scalar_prefetch=len(tables),
        grid=(N_EXP,),
        in_specs=[pl.BlockSpec(memory_space=pl.ANY),
                  pl.BlockSpec((1, 1, D, D_EXP), lambda e, *_: (l, e, 0, 0)),
                  pl.BlockSpec((1, 1, D, D_EXP), lambda e, *_: (l, e, 0, 0)),
                  pl.BlockSpec((1, 1, D_EXP, D), lambda e, *_: (l, e, 0, 0))],
        out_specs=pl.BlockSpec(memory_space=pl.ANY),
        scratch_shapes=[pltpu.VMEM((X_RING, TM, D), BF16), pltpu.VMEM((2, TM, D), BF16),
                        pltpu.VMEM((TM, D), BF16),
                        pltpu.VMEM((D, 2 * D_EXP), BF16), pltpu.VMEM((D_EXP, D), BF16),
                        pltpu.SemaphoreType.DMA((X_RING,)), pltpu.SemaphoreType.DMA((2,)),
                        pltpu.SemaphoreType.DMA(())],
    )
    return pl.pallas_call(
        _expert_kernel,
        out_shape=jax.ShapeDtypeStruct((MOE_ROWS, D), BF16),
        grid_spec=grid_spec,
        compiler_params=_cparams("arbitrary"),
        name="experts",
    )(*tables, xloc, w_gate, w_up, w_down)


def _combine_kernel(nblk_ref, gsrc_ref, y_hbm, route_ref, x1_ref, mod_ref, g_ref, b_ref, oc_ref, ol_ref,
                    ybuf, sem, *, layer):
    i = pl.program_id(0)

    slot_i = i % 2

    def block_copy(t, s, lb):
        src = pl.multiple_of(gsrc_ref[t * LOC_BLKS + lb] * BLK, BLK)
        return pltpu.make_async_copy(y_hbm.at[pl.ds(src, BLK), :],
                                     ybuf.at[s, pl.ds(pl.multiple_of(lb * BLK, BLK), BLK), :], sem.at[s])

    def fetch(t, s):
        def start(lb, c):
            block_copy(t, s, lb).start()
            return c
        lax.fori_loop(0, nblk_ref[t], start, 0)

    @pl.when(i == 0)
    def _():
        ybuf[...] = jnp.zeros_like(ybuf)
        fetch(0, 0)

    @pl.when(i + 1 < N_TILES)
    def _():
        fetch(jnp.minimum(i + 1, N_TILES - 1), 1 - slot_i)

    def wait(lb, c):
        block_copy(i, slot_i, lb).wait()
        return c

    lax.fori_loop(0, nblk_ref[i], wait, 0)
    route = route_ref[...]
    yb = ybuf[slot_i]
    slot = lax.broadcasted_iota(jnp.int32, (TM, LOC), 1)
    pick = (jnp.where(slot == route[:, 4:5].astype(jnp.int32), route[:, 2:3], 0.0)
            + jnp.where(slot == route[:, 5:6].astype(jnp.int32), route[:, 3:4], 0.0)).astype(BF16)
    moe = jnp.dot(pick, yb, preferred_element_type=F32)
    m = mod_ref[0]
    res = _layer_norm(ALPHA * x1_ref[...] + m[5:6] * moe, g_ref[layer:layer + 1, :], b_ref[layer:layer + 1, :])

    @pl.when(i < CTX_TILES)
    def _():
        oc_ref[...] = res

    @pl.when(i >= CTX_TILES)
    def _():
        ol_ref[...] = res


def _combine(l, nblk, gsrc, y, route, x1, mod_l, g, b):
    out_shape = (jax.ShapeDtypeStruct((T_CTX, D), F32), jax.ShapeDtypeStruct((T_LAT, D), F32))
    out_specs = tuple(_pair_specs(D))
    grid_spec = pltpu.PrefetchScalarGridSpec(
        num_scalar_prefetch=2,
        grid=(N_TILES,),
        in_specs=[pl.BlockSpec(memory_space=pl.ANY),
                  pl.BlockSpec((TM, 128), lambda i, nb, gs: (i, 0)),
                  pl.BlockSpec((TM, D), lambda i, nb, gs: (i, 0)),
                  _mod_spec(l), _full((DEPTH, D)), _full((DEPTH, D))],
        out_specs=out_specs,
        scratch_shapes=[pltpu.VMEM((2, LOC, D), BF16), pltpu.SemaphoreType.DMA((2,))],
    )
    return pl.pallas_call(
        functools.partial(_combine_kernel, layer=l),
        out_shape=out_shape,
        grid_spec=grid_spec,
        compiler_params=_cparams("arbitrary"),
        name="moe_combine",
    )(nblk, gsrc, y, route, x1, mod_l, g, b)


def _prefix_pick(starts, query, table):
    delta = table - jnp.concatenate([jnp.zeros_like(table[..., :1]), table[..., :-1]], axis=-1)
    return jnp.sum(jnp.where(starts <= query, delta, 0), axis=-1)


def _dispatch_tables(cnt):
    i32 = jnp.int32
    run = (cnt + BLK - 1) // BLK * BLK
    loc_start = jnp.cumsum(run, axis=1) - run
    nblk_loc = jnp.sum(run, axis=1) // BLK
    seg_rows = jnp.sum(run, axis=0)
    seg_tiles = (seg_rows + TM - 1) // TM
    tile_end = jnp.cumsum(seg_tiles)
    seg_start = (tile_end - seg_tiles) * TM
    glob_start = seg_start[None, :] + jnp.cumsum(run, axis=0) - run

    g_blk = (glob_start.T.reshape(1, -1)) // BLK
    n_blk = (run.T.reshape(1, -1)) // BLK
    l_blk = ((jnp.arange(N_TILES, dtype=i32) * LOC)[None, :] + loc_start.T).reshape(1, -1) // BLK
    gb = jnp.arange(MOE_ROWS // BLK, dtype=i32)[:, None]
    off = gb[:, 0] - _prefix_pick(g_blk, gb, g_blk)
    src_blk = jnp.where(off < _prefix_pick(g_blk, gb, n_blk), _prefix_pick(g_blk, gb, l_blk) + off,
                        ZERO_BLK).astype(i32)

    pos = (jnp.arange(LOC_BLKS, dtype=i32) * BLK)[None, :, None]
    shift = _prefix_pick(loc_start[:, None, :], pos, (glob_start - loc_start)[:, None, :])
    gsrc = ((pos[:, :, 0] + shift) // BLK).astype(i32)
    expert_tables = ((tile_end - seg_tiles).astype(i32), seg_tiles.astype(i32), src_blk)
    return expert_tables, nblk_loc.astype(i32), gsrc.reshape(-1)


def _rope_slot(pe):
    return jnp.pad(pe, [(0, 0)] * (pe.ndim - 1) + [(NOPE, SLOT - NOPE - ROPE)])


def _wuq_layout(w):
    w = w.reshape(DEPTH, Q_LORA, HEADS, NOPE + ROPE)
    w = jnp.pad(w, ((0, 0), (0, 0), (0, 0), (0, SLOT - NOPE - ROPE)))
    return w.reshape(DEPTH, Q_LORA, HEADS * SLOT).astype(BF16)


def _wukv_layout(w):
    w = w.reshape(DEPTH, KV_LORA, HEADS, NOPE + VD)
    wk = jnp.pad(w[..., :NOPE], ((0, 0), (0, 0), (0, 0), (0, SLOT - NOPE)))
    wv = w[..., NOPE:].reshape(DEPTH, KV_LORA, HEADS * VD).astype(BF16)
    return wk.reshape(DEPTH, KV_LORA, HEADS * SLOT).astype(BF16), wv, jnp.swapaxes(wv, 1, 2)


def kernel(x_prompt, x_sample, c, cache_ckv, cache_kpe, c_ctx, ln_in_g, ln_in_b, w_mod, b_mod, w_in,
           gm_ln_g, gm_ln_b, gm_ws, gm_bs, hy_conv_w, hy_conv_b, hy_f_w1, hy_f_b1, hy_f_w2, hy_f_b2,
           hy_f_w3, hy_f_freq, hy_bias, mla_gq, mla_gkv, mla_wuq, mla_wukv, w_out, ln1_g, ln1_b,
           ln2_g, ln2_b, moe_w_gr, moe_b_gr, moe_w_er, moe_b_er, moe_w_gate, moe_w_up, moe_w_down):
    rope = _rope_tables()
    dft = {L: _dft_tables(L) for L in (CTX_L, LAT_L)}
    pos_tab = {L: _hyena_positions(L) for L in (CTX_L, LAT_L)}
    hd = np.arange(GM_W) // GM_HD
    avg = jnp.asarray((hd[:, None] == hd[None, :]) / GM_HD, BF16)
    tri = jnp.asarray(np.tril(np.ones((TM, TM)), -1), BF16)
    upper = jnp.asarray(np.triu(np.ones((128, 128)), 1), BF16)

    cond8 = jnp.concatenate([c_ctx[None], c, jnp.zeros((8 - 1 - N_LAT_B, D), F32)], axis=0)
    mod = _modulation(cond8, w_mod, b_mod).reshape(DEPTH, 8, 6, D)

    wuk_all, wuv_all, wuvt_all = _wukv_layout(mla_wukv)
    w_in_t = jnp.swapaxes(w_in, 1, 2)
    wuq_arr = _wuq_layout(mla_wuq)
    ws_b = gm_ws.astype(BF16)
    kc_all, vc_all = _cache_kv(cache_ckv, _rope_slot(cache_kpe), wuk_all, wuvt_all)

    x_pair = (x_prompt.reshape(T_CTX, D), x_sample.reshape(T_LAT, D))
    norm = (ln_in_g.reshape(1, D), ln_in_b.reshape(1, D))
    bs_full = jnp.repeat(jnp.swapaxes(gm_bs, 1, 2), GM_HD, axis=2)
    w_route = jnp.pad(jnp.concatenate([moe_w_gr, moe_w_er.reshape(DEPTH, D, N_EXP)], axis=2),
                      ((0, 0), (0, 0), (0, 128 - N_GROUPS - N_EXP)))
    b_route = jnp.pad(jnp.concatenate([moe_b_gr, moe_b_er.reshape(DEPTH, N_EXP)], axis=1),
                      ((0, 0), (0, 128 - N_GROUPS - N_EXP)))
    hyena_groups = ((CTX_L, N_CTX_B, 4, 0), (LAT_L, N_LAT_B, 2, T_CTX // LAT_L))
    spectra = {L: _filter_spectra(L, *pos_tab[L], dft[L], hy_f_w1, hy_f_b1, hy_f_w2, hy_f_b2, hy_f_w3,
                                  hy_f_freq) for L, _, _, _ in hyena_groups}
    ckv_states, kpe_states = [], []
    for l in range(DEPTH):
        a, hy, q, k, v, vt, ckv, kpe = _in_proj(
            l, x_pair, norm, mod, w_in_t, gm_ln_g, gm_ln_b, avg,
            ws_b, bs_full, mla_gq, mla_gkv, wuq_arr, wuk_all, wuv_all, wuvt_all, rope)
        ckv_states.append(ckv[:T_CTX].reshape(N_CTX_B, CTX_L, KV_LORA))
        kpe_states.append(kpe[:T_CTX, NOPE:NOPE + ROPE].reshape(N_CTX_B, CTX_L, ROPE))

        hyb = [_hyena_group(l, hy, L, nb, group, blk0, dft[L], spectra[L], hy_conv_w, hy_conv_b, hy_bias)
               for L, nb, group, blk0 in hyena_groups]

        att = _attention(l, q, k, v, vt, kc_all, vc_all)

        x1, xloc, route, counts = _out_proj(l, a, hyb, att, w_out, x_pair, norm, mod,
                                            ln1_g, ln1_b, w_route, b_route, tri, upper)
        cnt = counts.reshape(N_TILES, 8, 128)[:, 0, :N_EXP].astype(jnp.int32)
        expert_tables, nblk_loc, gsrc = _dispatch_tables(cnt)
        y = _experts(l, expert_tables, xloc, moe_w_gate, moe_w_up, moe_w_down)
        x_pair = _combine(l, nblk_loc, gsrc, y, route, x1, mod, ln2_g, ln2_b)

    y_prompt = x_pair[0].reshape(N_CTX_B, CTX_L, D)
    y_sample = x_pair[1].reshape(N_LAT_B, LAT_L, D)
    return (y_prompt, y_sample, jnp.stack(ckv_states, axis=1), jnp.stack(kpe_states, axis=1))
```

```python
import functools
import math

import numpy as np
import jax
import jax.numpy as jnp
from jax import lax
from jax.experimental import pallas as pl
from jax.experimental.pallas import tpu as pltpu

D = 1024
N_CTX_B, CTX_L = 16, 256
N_LAT_B, LAT_L = 4, 1024
DEPTH = 2
T_CTX = N_CTX_B * CTX_L
T_LAT = N_LAT_B * LAT_L
T = T_CTX + T_LAT
PAST = 256
GRID_W = 64

GM_HEADS, GM_HD, GM_W, CHUNK = 4, 64, 256, 128
HY_W, HY_EMB, HY_FH = 256, 33, 64
NOPE, ROPE, VD, HEADS = 64, 32, 64, 8
Q_LORA, KV_LORA = 256, 128
SLOT = 128
N_GROUPS, EPG, N_EXP, D_EXP = 4, 8, 32, 256
ALPHA = (2.0 * DEPTH) ** 0.25
LN_EPS, RMS_EPS = 1e-5, 1e-6
ROPE_THETA = 10000.0

TM = 256
N_TILES = T // TM
CTX_TILES = T_CTX // TM
LAT_TILES_PER_B = LAT_L // TM
N_ASSIGN = 2 * T
BLK = 16
BLK_PER_TILE = TM // BLK
LOC = 2 * TM + N_EXP * BLK
LOC_BLKS = LOC // BLK
ZERO_BLK = LOC_BLKS - 1
MOE_TILES = (N_ASSIGN + N_TILES * N_EXP * (BLK - 1)) // TM + N_EXP
MOE_ROWS = MOE_TILES * TM
IN_COLS = 1696
IN_EXT = 1792

F32, BF16 = jnp.float32, jnp.bfloat16
_NT = (((1,), (1,)), ((), ()))
VMEM_LIMIT = 52 * 1024 * 1024


def _cparams(*sem):
    return pltpu.CompilerParams(dimension_semantics=sem, vmem_limit_bytes=VMEM_LIMIT)


def _cond_of_tile(i):
    return jnp.where(i < CTX_TILES, 0, 1 + (i - CTX_TILES) // LAT_TILES_PER_B)


def _rope_block_of_tile(i):
    return jnp.where(i < CTX_TILES, 0, 1 + (i - CTX_TILES) % LAT_TILES_PER_B)


def _full(shape):
    n = len(shape)
    return pl.BlockSpec(shape, lambda *_: (0,) * n)


def _layer_spec(l, *shape):
    return pl.BlockSpec((None,) + shape, lambda *_: (l,) + (0,) * len(shape))


def _mod_spec(l):
    return pl.BlockSpec((None, 1, 6, D), lambda i, *_: (l, _cond_of_tile(i), 0, 0))


def _pair_specs(n):
    return [pl.BlockSpec((TM, n), lambda i, *_: (jnp.minimum(i, CTX_TILES - 1), 0)),
            pl.BlockSpec((TM, n), lambda i, *_: (jnp.maximum(i - CTX_TILES, 0), 0))]


def _pair_read(c_ref, l_ref):
    return jnp.where(pl.program_id(0) < CTX_TILES, c_ref[...], l_ref[...])


def _layer_norm(x, g, b):
    mu = jnp.mean(x, -1, keepdims=True)
    xc = x - mu
    var = jnp.mean(xc * xc, -1, keepdims=True)
    return xc * lax.rsqrt(var + LN_EPS) * g + b


def _rms_norm(x, g):
    return x * lax.rsqrt(jnp.mean(x * x, -1, keepdims=True) + RMS_EPS) * g


def _bdot(a, b):
    return jnp.dot(a.astype(BF16), b.astype(BF16), preferred_element_type=F32)


def _dot3(x, w):
    x_hi = x.astype(BF16)
    x_lo = (x - x_hi.astype(F32)).astype(BF16)
    w_hi = w.astype(BF16)
    w_lo = (w - w_hi.astype(F32)).astype(BF16)
    dot = functools.partial(jnp.dot, preferred_element_type=F32)
    return dot(x_hi, w_hi) + dot(x_hi, w_lo) + dot(x_lo, w_hi)


def _swap_pair_lanes(x):
    n = x.shape[1]
    lane = lax.broadcasted_iota(jnp.int32, x.shape, 1)
    return jnp.where(lane % 2 == 0, pltpu.roll(x, n - 1, 1), pltpu.roll(x, 1, 1))


def _split2_dot(x, w_bf16):
    hi = x.astype(BF16)
    lo = (x - hi.astype(F32)).astype(BF16)
    dot = functools.partial(jnp.dot, preferred_element_type=F32)
    return dot(hi, w_bf16) + dot(lo, w_bf16)


def _dft_tables(L):
    f = np.arange(L, dtype=np.int64)
    ft = np.outer(f, f) % (2 * L)
    ang = np.pi * ft / L
    c = np.cos(ang)
    s = np.sin(ang)
    alt = np.where(f % 2 == 0, 1.0, -1.0)
    sf = s.copy()
    sf[0, :] = alt
    return (jnp.asarray(c, F32).astype(BF16), jnp.asarray(sf, F32).astype(BF16),
            jnp.asarray(sf.T.copy(), F32).astype(BF16), jnp.asarray(alt[:, None], F32))


def _rope_tables():
    rows = LAT_L // GRID_W
    row = np.repeat(np.arange(rows, dtype=np.float64), GRID_W)
    col = np.tile(np.arange(GRID_W, dtype=np.float64), rows)
    n = ROPE // 4
    inv = ROPE_THETA ** (-np.arange(n, dtype=np.float64) / n)
    ang = np.concatenate([row[:, None] * inv, col[:, None] * inv], -1)
    cos = np.zeros((CTX_L + LAT_L, SLOT))
    sin = np.zeros((CTX_L + LAT_L, SLOT))
    cos[:, :NOPE + ROPE] = 1.0
    cos[CTX_L:, NOPE:NOPE + ROPE:2] = np.cos(ang)
    cos[CTX_L:, NOPE + 1:NOPE + ROPE:2] = np.cos(ang)
    sin[CTX_L:, NOPE:NOPE + ROPE:2] = -np.sin(ang)
    sin[CTX_L:, NOPE + 1:NOPE + ROPE:2] = np.sin(ang)
    scale = math.log2(math.e) / math.sqrt(NOPE + ROPE)
    return (jnp.asarray(cos * scale, F32), jnp.asarray(sin * scale, F32),
            jnp.asarray(cos, F32), jnp.asarray(sin, F32))


def _hyena_positions(L):
    t = jnp.linspace(0.0, 1.0, L, dtype=F32)[:, None]
    bands = (HY_EMB - 1) // 2
    w = 2.0 * math.pi * jnp.arange(L, dtype=F32)[:, None] / L
    f = jnp.linspace(1e-4, bands - 1, bands, dtype=F32)[None]
    z = jnp.concatenate([t, jnp.cos(f * w), -jnp.sin(f * w)], -1)
    z = jnp.pad(z, ((0, 0), (0, 128 - HY_EMB)))
    min_decay = math.log(1e-2) / 1.5
    max_decay = math.log(1e-2) / 0.3
    deltas = jnp.abs(jnp.linspace(min_decay, max_decay, HY_W, dtype=F32))
    return z, jnp.exp(-t * deltas)


def _mod_kernel(c_ref, w_ref, b_ref, o_ref):
    c = c_ref[...]
    s = c / (1.0 + jnp.exp(-c))
    o_ref[0] = _bdot(s, w_ref[0]) + b_ref[0]


def _modulation(cond8, w_mod, b_mod):
    tn = 1536
    return pl.pallas_call(
        _mod_kernel,
        out_shape=jax.ShapeDtypeStruct((DEPTH, 8, 6 * D), F32),
        grid=(DEPTH, 6 * D // tn),
        in_specs=[pl.BlockSpec((8, D), lambda l, j: (0, 0)),
                  pl.BlockSpec((1, D, tn), lambda l, j: (l, 0, j)),
                  pl.BlockSpec((1, 1, tn), lambda l, j: (l, 0, j))],
        out_specs=pl.BlockSpec((1, 8, tn), lambda l, j: (l, 0, j)),
        compiler_params=_cparams("parallel", "parallel"),
        name="modulation",
    )(cond8, w_mod, b_mod.reshape(DEPTH, 1, 6 * D))


def _filter_kernel(z_ref, w1_ref, b1_ref, w2_ref, b2_ref, w3_ref, fr_ref, dec_ref, c_ref, sf_ref,
                   alt_ref, kr_ref, ki_ref, krn_ref, *, L):
    h1 = jnp.sin(fr_ref[0:1] * (_dot3(z_ref[...], w1_ref[...]) + b1_ref[...]))
    h2 = jnp.sin(fr_ref[1:2] * (_dot3(h1, w2_ref[...]) + b2_ref[...]))
    filt = _dot3(h2, w3_ref[...])
    row = lax.broadcasted_iota(jnp.int32, (L, HY_W), 0)
    dec = dec_ref[...]
    n = DEPTH * 2
    fwd = [filt[:, j * 2 * HY_W:j * 2 * HY_W + HY_W] * dec for j in range(n)]
    bwd = [jnp.where(row == 0, 0.0, filt[:, j * 2 * HY_W + HY_W:(j + 1) * 2 * HY_W] * dec) for j in range(n)]
    sm = [f + b for f, b in zip(fwd, bwd)]
    kr_all = _bdot(c_ref[...], jnp.concatenate(sm, axis=1))
    ki_all = -_bdot(sf_ref[...], jnp.concatenate([f - b for f, b in zip(fwd, bwd)], axis=1))
    for j in range(n):
        l, o = divmod(j, 2)
        kr = kr_all[:, j * HY_W:(j + 1) * HY_W]
        kr_nyq = jnp.sum(alt_ref[...] * sm[j], axis=0, keepdims=True)
        kr_ref[l, o] = jnp.where(row == 0, kr * (0.5 / L), kr * (1.0 / L))
        ki_ref[l, o] = jnp.where(row == 0, 0.0, ki_all[:, j * HY_W:(j + 1) * HY_W] * (1.0 / L))
        krn_ref[l, o] = jnp.where(row == 0, kr_nyq * (0.5 / L), kr * (1.0 / L))


def _block_diag(blocks):
    rows = []
    for i, blk in enumerate(blocks):
        rows.append(jnp.concatenate([blk if j == i else jnp.zeros((blk.shape[0], other.shape[1]), blk.dtype)
                                     for j, other in enumerate(blocks)], axis=1))
    return jnp.concatenate(rows, axis=0)


def _filter_spectra(L, z, dec, dft, w1, b1, w2, b2, w3, freq):
    assert DEPTH * HY_FH == 128
    c, sf, _, alt = dft
    layers = range(DEPTH)
    w1c = jnp.pad(jnp.concatenate([w1[l] for l in layers], axis=1), ((0, 128 - HY_EMB), (0, 0)))
    w2c = _block_diag([w2[l] for l in layers])
    w3c = _block_diag([w3[l] for l in layers])
    b1c, b2c = b1.reshape(1, 128), b2.reshape(1, 128)
    frc = jnp.swapaxes(freq, 0, 1).reshape(2, 128)
    shp = jax.ShapeDtypeStruct((DEPTH, 2, L, HY_W), F32)
    args = (z, w1c, b1c, w2c, b2c, w3c, frc, dec, c, sf, alt)
    return pl.pallas_call(
        functools.partial(_filter_kernel, L=L),
        out_shape=(shp, shp, shp),
        grid=(1,),
        in_specs=[_full(a.shape) for a in args],
        out_specs=(_full((DEPTH, 2, L, HY_W)),) * 3,
        compiler_params=_cparams("arbitrary"),
        name=f"hyena_filter_{L}",
    )(*args)


def _in_kernel(xc_ref, xl_ref, ng_ref, nb_ref, mod_ref, wt_ref, gg_ref, gb_ref, avg_ref, ws_ref, bs_ref,
               gq_ref, gkv_ref, wuq_ref, wuk_ref, wuv_ref, wuvt_ref, cq_ref, sq_ref, ck_ref, sk_ref,
               a_ref, hy_ref, q_ref, k_ref, v_ref, vt_ref, ckv_ref, kpe_ref, w_b, *, layer):
    l = layer

    @pl.when(pl.program_id(0) == 0)
    def _():
        for c0 in range(0, 1664, 128):
            w_b[:, c0:c0 + 128] = jnp.transpose(wt_ref[c0:c0 + 128, :]).astype(BF16)
        slot_rows = jnp.concatenate([jnp.zeros((NOPE, D), F32), wt_ref[1664:IN_COLS, :],
                                     jnp.zeros((SLOT - NOPE - ROPE, D), F32)], axis=0)
        w_b[:, 1664:IN_EXT] = jnp.transpose(slot_rows).astype(BF16)

    x = _pair_read(xc_ref, xl_ref)
    if l == 0:
        x = _layer_norm(x, ng_ref[...], nb_ref[...])
    m = mod_ref[0]
    h = x * (1.0 + m[1:2]) + m[0:1]
    proj = _bdot(h, w_b[...])

    gu = jax.nn.gelu(proj[:, 0:GM_W], approximate=True)
    gv = jax.nn.gelu(proj[:, GM_W:2 * GM_W], approximate=True)
    avg = avg_ref[...]
    mu = _split2_dot(gv, avg)
    vc = gv - mu
    var = _split2_dot(vc * vc, avg)
    vln = (vc * lax.rsqrt(var + LN_EPS) * gg_ref[l:l + 1, :] + gb_ref[l:l + 1, :]).astype(BF16)
    lane = lax.broadcasted_iota(jnp.int32, (CHUNK, GM_W), 1)
    for c in range(TM // CHUNK):
        vchunk = vln[c * CHUNK:(c + 1) * CHUNK]
        s = bs_ref[...]
        for hd in range(GM_HEADS):
            sh = jnp.dot(ws_ref[0, hd], vchunk, preferred_element_type=F32)
            s = s + jnp.where(lane // GM_HD == hd, sh, 0.0)
        a_ref[c * CHUNK:(c + 1) * CHUNK, :] = (gu[c * CHUNK:(c + 1) * CHUNK] * s).astype(BF16)

    hy_ref[...] = proj[:, 512:1280]

    cq = _rms_norm(proj[:, 1280:1536], gq_ref[l:l + 1, :])
    qq = _bdot(cq, wuq_ref[0])
    cos_q = jnp.concatenate([cq_ref[...]] * HEADS, axis=1)
    sin_q = jnp.concatenate([sq_ref[...]] * HEADS, axis=1)
    q_ref[...] = (qq * cos_q + _swap_pair_lanes(qq) * sin_q).astype(BF16)

    ckv = _rms_norm(proj[:, 1536:1664], gkv_ref[l:l + 1, :])
    ckv_ref[...] = ckv
    kpe = proj[:, 1664:IN_EXT]
    kpe_ref[...] = kpe
    krot = kpe * ck_ref[...] + _swap_pair_lanes(kpe) * sk_ref[...]
    kn = _bdot(ckv, wuk_ref[0])
    k_ref[...] = (kn + jnp.concatenate([krot] * HEADS, axis=1)).astype(BF16)
    v_ref[...] = _bdot(ckv, wuv_ref[0]).astype(BF16)
    vt_ref[...] = lax.dot_general(wuvt_ref[0], ckv.astype(BF16), _NT,
                                  preferred_element_type=F32).astype(BF16)


def _in_proj(l, x_pair, norm, mod_l, w_in_t, gm_g, gm_b, avg, ws, bs_full, gq, gkv, wuq_arr,
             wuk_arr, wuv_arr, wuvt_arr, rope):
    layer = lambda *shape: pl.BlockSpec((1,) + shape, lambda i: (l,) + (0,) * len(shape))
    cos_q, sin_q, cos_k, sin_k = rope
    tile = lambda n: pl.BlockSpec((TM, n), lambda i: (i, 0))
    rope_spec = pl.BlockSpec((TM, SLOT), lambda i: (_rope_block_of_tile(i), 0))
    out_shapes = (jax.ShapeDtypeStruct((T, GM_W), BF16),
                  jax.ShapeDtypeStruct((T, 3 * HY_W), F32),
                  jax.ShapeDtypeStruct((T, HEADS * SLOT), BF16),
                  jax.ShapeDtypeStruct((T, HEADS * SLOT), BF16),
                  jax.ShapeDtypeStruct((T, HEADS * VD), BF16),
                  jax.ShapeDtypeStruct((HEADS * VD, T), BF16),
                  jax.ShapeDtypeStruct((T, KV_LORA), F32),
                  jax.ShapeDtypeStruct((T, SLOT), F32))
    return pl.pallas_call(
        functools.partial(_in_kernel, layer=l),
        out_shape=out_shapes,
        grid=(N_TILES,),
        in_specs=_pair_specs(D) + [
                  _full((1, D)), _full((1, D)),
                  _mod_spec(l),
                  pl.BlockSpec((None, IN_COLS, D), lambda i: (l, 0, 0), pipeline_mode=pl.Buffered(1)),
                  _full((DEPTH, GM_W)), _full((DEPTH, GM_W)), _full((GM_W, GM_W)),
                  layer(GM_HEADS, CHUNK, CHUNK), _layer_spec(l, CHUNK, GM_W),
                  _full((DEPTH, Q_LORA)), _full((DEPTH, KV_LORA)),
                  layer(Q_LORA, HEADS * SLOT), layer(KV_LORA, HEADS * SLOT),
                  layer(KV_LORA, HEADS * VD), layer(HEADS * VD, KV_LORA),
                  rope_spec, rope_spec, rope_spec, rope_spec],
        out_specs=(tile(GM_W), tile(3 * HY_W), tile(HEADS * SLOT), tile(HEADS * SLOT), tile(HEADS * VD),
                   pl.BlockSpec((HEADS * VD, TM), lambda i: (0, i)), tile(KV_LORA), tile(SLOT)),
        scratch_shapes=[pltpu.VMEM((D, IN_EXT), BF16)],
        compiler_params=_cparams("arbitrary"),
        name="in_proj",
    )(*x_pair, *norm, mod_l, w_in_t, gm_g, gm_b, avg, ws, bs_full, gq, gkv, wuq_arr, wuk_arr, wuv_arr,
      wuvt_arr, cos_q, sin_q, cos_k, sin_k)


def _cache_kernel(ckv_ref, kpe_ref, wuk_ref, wuv_ref, k_ref, v_ref):
    ckv = ckv_ref[0, 0]
    kn = _bdot(ckv, wuk_ref[0])
    k_ref[0, 0] = (kn + jnp.concatenate([kpe_ref[0, 0]] * HEADS, axis=1)).astype(BF16)
    v_ref[0, 0] = lax.dot_general(wuv_ref[0], ckv.astype(BF16), _NT,
                                  preferred_element_type=F32).astype(BF16)


def _cache_kv(cache_ckv, cache_kpe_slot, wuk_arr, wuv_arr):
    return pl.pallas_call(
        _cache_kernel,
        out_shape=(jax.ShapeDtypeStruct((DEPTH, N_LAT_B, PAST, HEADS * SLOT), BF16),
                   jax.ShapeDtypeStruct((DEPTH, N_LAT_B, HEADS * VD, PAST), BF16)),
        grid=(DEPTH, N_LAT_B),
        in_specs=[pl.BlockSpec((1, 1, PAST, KV_LORA), lambda l, b: (b, l, 0, 0)),
                  pl.BlockSpec((1, 1, PAST, SLOT), lambda l, b: (b, l, 0, 0)),
                  pl.BlockSpec((1, KV_LORA, HEADS * SLOT), lambda l, b: (l, 0, 0)),
                  pl.BlockSpec((1, HEADS * VD, KV_LORA), lambda l, b: (l, 0, 0))],
        out_specs=(pl.BlockSpec((1, 1, PAST, HEADS * SLOT), lambda l, b: (l, b, 0, 0)),
                   pl.BlockSpec((1, 1, HEADS * VD, PAST), lambda l, b: (l, b, 0, 0))),
        compiler_params=_cparams("parallel", "parallel"),
        name="cache_kv",
    )(cache_ckv, cache_kpe_slot, wuk_arr, wuv_arr)


def _attend_rows(q_ref, k_ref, v_ref, o_ref):
    lq = q_ref.shape[0]
    lane = lax.broadcasted_iota(jnp.int32, (lq, 2 * VD), 1)

    def scores_of(hd):
        return lax.dot_general(q_ref[:, hd * SLOT:(hd + 1) * SLOT], k_ref[:, hd * SLOT:(hd + 1) * SLOT], _NT,
                               preferred_element_type=F32)

    outs = []
    nxt = scores_of(0)
    for hd in range(HEADS):
        s = nxt
        if hd + 1 < HEADS:
            nxt = scores_of(hd + 1)
        p = jnp.exp2(s - jnp.max(s, -1, keepdims=True))
        acc = jnp.dot(p.astype(BF16), v_ref[:, (hd // 2) * 2 * VD:(hd // 2 + 1) * 2 * VD],
                      preferred_element_type=F32)
        outs.append(acc / jnp.sum(p, -1, keepdims=True))
        if hd % 2 == 1:
            pair = hd // 2
            o_ref[:, pair * 2 * VD:(pair + 1) * 2 * VD] = jnp.where(lane < VD, outs[-2], outs[-1]).astype(BF16)


def _attend_cols(q_ref, segments, o_ref):
    lq = q_ref.shape[0]
    lane = lax.broadcasted_iota(jnp.int32, (lq, 2 * SLOT), 1)
    row = lax.broadcasted_iota(jnp.int32, (2 * VD, lq), 0)
    n_pairs = HEADS // 2

    def scores_of(pair):
        qp = q_ref[:, pair * 2 * SLOT:(pair + 1) * 2 * SLOT]
        zero = jnp.zeros_like(qp)
        q_bd = jnp.concatenate([jnp.where(lane < SLOT, qp, zero), jnp.where(lane < SLOT, zero, qp)], axis=0)
        return [lax.dot_general(k_ref[:, pair * 2 * SLOT:(pair + 1) * 2 * SLOT], q_bd, _NT,
                                preferred_element_type=F32) for k_ref, _ in segments]

    def head_pair_out(pair, scores):
        mx = functools.reduce(jnp.maximum, [jnp.max(s, 0, keepdims=True) for s in scores])
        ps = [jnp.exp2(s - mx) for s in scores]
        den = functools.reduce(jnp.add, [jnp.sum(p, 0, keepdims=True) for p in ps])
        acc = functools.reduce(jnp.add, [
            jnp.dot(vt_ref[pair * 2 * VD:(pair + 1) * 2 * VD, :], p.astype(BF16),
                    preferred_element_type=F32) for p, (_, vt_ref) in zip(ps, segments)])
        acc = acc / den
        return jnp.where(row < VD, acc[:, :lq], acc[:, lq:])

    outs = []
    nxt = scores_of(0)
    for pair in range(n_pairs):
        cur = nxt
        if pair + 1 < n_pairs:
            nxt = scores_of(pair + 1)
        outs.append(head_pair_out(pair, cur))
    o_ref[...] = jnp.transpose(jnp.concatenate(outs, axis=0)).astype(BF16)


def _attn_ctx_kernel(q_ref, k_ref, v_ref, o_ref):
    _attend_rows(q_ref, k_ref, v_ref, o_ref)


def _attn_lat_kernel(q_ref, kc_ref, vc_ref, k_ref, vt_ref, o_ref):
    _attend_cols(q_ref, [(kc_ref.at[0], vc_ref.at[0]), (k_ref, vt_ref)], o_ref)


def _attention(l, q, k, v, vt, kc, vc):
    kw, vw = HEADS * SLOT, HEADS * VD
    ctx = pl.pallas_call(
        _attn_ctx_kernel,
        out_shape=jax.ShapeDtypeStruct((T_CTX, vw), BF16),
        grid=(N_CTX_B,),
        in_specs=[pl.BlockSpec((CTX_L, kw), lambda b: (b, 0)),
                  pl.BlockSpec((CTX_L, kw), lambda b: (b, 0)),
                  pl.BlockSpec((CTX_L, vw), lambda b: (b, 0))],
        out_specs=pl.BlockSpec((CTX_L, vw), lambda b: (b, 0)),
        compiler_params=_cparams("parallel"),
        name="attn_ctx",
    )(q, k, v)
    nq = LAT_L // TM
    off = T_CTX // LAT_L
    lat = pl.pallas_call(
        _attn_lat_kernel,
        out_shape=jax.ShapeDtypeStruct((T_LAT, vw), BF16),
        grid=(N_LAT_B, nq),
        in_specs=[pl.BlockSpec((TM, kw), lambda b, j: (CTX_TILES + b * nq + j, 0)),
                  pl.BlockSpec((None, 1, PAST, kw), lambda b, j: (l, b, 0, 0)),
                  pl.BlockSpec((None, 1, vw, PAST), lambda b, j: (l, b, 0, 0)),
                  pl.BlockSpec((LAT_L, kw), lambda b, j: (off + b, 0)),
                  pl.BlockSpec((vw, LAT_L), lambda b, j: (0, off + b))],
        out_specs=pl.BlockSpec((TM, vw), lambda b, j: (b * nq + j, 0)),
        compiler_params=_cparams("parallel", "parallel"),
        name="attn_lat",
    )(q, kc, vc, k, vt)
    return ctx, lat


def _hyena_kernel(hy_ref, cw_ref, cb_ref, c_ref, sf_ref, sb_ref, kr_ref, ki_ref, krn_ref, hb_ref,
                  o_ref, *, L, group, layer):
    row = lax.broadcasted_iota(jnp.int32, (L, 3 * HY_W), 0)
    zs = []
    for j in range(group):
        x = hy_ref[j * L:(j + 1) * L, :]
        prev = jnp.where(row == 0, 0.0, pltpu.roll(x, 1, 0))
        nxt = jnp.where(row == L - 1, 0.0, pltpu.roll(x, L - 1, 0))
        zs.append(prev * cw_ref[0:1] + x * cw_ref[1:2] + nxt * cw_ref[2:3] + cb_ref[layer:layer + 1, :])
    side = lambda parts: jnp.concatenate(parts, axis=1)
    y = side([z[:, 2 * HY_W:] for z in zs])
    for o in range(2):
        gate = side([z[:, o * HY_W:(o + 1) * HY_W] for z in zs])
        kr, ki, krn = (side([r[o]] * group) for r in (kr_ref, ki_ref, krn_ref))
        yb = y.astype(BF16)
        a_re = jnp.dot(c_ref[...], yb, preferred_element_type=F32)
        a_im = jnp.dot(sf_ref[...], yb, preferred_element_type=F32)
        z_re = a_re * kr + a_im * ki
        z_im = a_im * krn - a_re * ki
        conv = (jnp.dot(c_ref[...], z_re.astype(BF16), preferred_element_type=F32)
                + jnp.dot(sb_ref[...], z_im.astype(BF16), preferred_element_type=F32))
        y = gate * (conv + y * side([hb_ref[o:o + 1]] * group))
    for j in range(group):
        o_ref[j * L:(j + 1) * L, :] = y[:, j * HY_W:(j + 1) * HY_W].astype(BF16)


def _hyena_group(l, hy, L, nb, group, blk0, dft, spectra, conv_w, conv_b, hy_bias):
    c, sf, sb, _ = dft
    kr, ki, krn = spectra
    const = lambda shape: pl.BlockSpec(shape, lambda b: (0,) * len(shape), pipeline_mode=pl.Buffered(1))
    spec = pl.BlockSpec((None, 2, L, HY_W), lambda b: (l, 0, 0, 0), pipeline_mode=pl.Buffered(1))
    return pl.pallas_call(
        functools.partial(_hyena_kernel, L=L, group=group, layer=l),
        out_shape=jax.ShapeDtypeStruct((nb * L, HY_W), BF16),
        grid=(nb // group,),
        in_specs=[pl.BlockSpec((group * L, 3 * HY_W), lambda b: (blk0 // group + b, 0)),
                  _layer_spec(l, 3, 3 * HY_W), _full((DEPTH, 3 * HY_W)),
                  const((L, L)), const((L, L)), const((L, L)), spec, spec, spec,
                  _layer_spec(l, 2, HY_W)],
        out_specs=pl.BlockSpec((group * L, HY_W), lambda b: (b, 0)),
        compiler_params=_cparams("parallel"),
        name=f"hyena_{L}",
    )(hy, conv_w, conv_b, c, sf, sb, kr, ki, krn, hy_bias)


def _out_kernel(a_ref, bc_ref, bl_ref, mc_ref, ml_ref, w_ref, xc_ref, xl_ref, ng_ref, nb_ref, mod_ref,
                g_ref, be_ref, wr_ref, br_ref, tri_ref, upper_ref, x1_ref, xloc_ref, route_ref, cnt_ref,
                w_b, *, layer):
    l = layer

    @pl.when(pl.program_id(0) == 0)
    def _():
        w_b[...] = w_ref[...].astype(BF16)

    dot = functools.partial(jnp.dot, preferred_element_type=F32)
    mixed = (dot(a_ref[...], w_b[0:GM_W]) + dot(_pair_read(bc_ref, bl_ref), w_b[GM_W:GM_W + HY_W])
             + dot(_pair_read(mc_ref, ml_ref), w_b[GM_W + HY_W:]))
    x = _pair_read(xc_ref, xl_ref)
    if l == 0:
        x = _layer_norm(x, ng_ref[...], nb_ref[...])
    m = mod_ref[0]
    x1 = _layer_norm(ALPHA * x + m[2:3] * mixed, g_ref[l:l + 1, :], be_ref[l:l + 1, :])
    x1_ref[...] = x1
    h2 = x1 * (1.0 + m[4:5]) + m[3:4]

    h_hi = h2.astype(BF16)
    h_lo = (h2 - h_hi.astype(F32)).astype(BF16)
    w_hi = wr_ref[...].astype(BF16)
    w_lo = (wr_ref[...] - w_hi.astype(F32)).astype(BF16)
    part = dot(h_hi, jnp.concatenate([w_hi, w_lo], axis=1))
    logits = part[:, :128] + part[:, 128:] + dot(h_lo, w_hi) + br_ref[l:l + 1, :]
    lane = lax.broadcasted_iota(jnp.int32, logits.shape, 1)
    lanef = lane.astype(F32)
    big = jnp.float32(1e9)
    ninf = jnp.float32(-jnp.inf)
    is_g = lane < N_GROUPS
    gl = jnp.where(is_g, logits, ninf)
    gmax = jnp.max(gl, -1, keepdims=True)
    gidx = jnp.min(jnp.where(gl == gmax, lanef, big), -1, keepdims=True)
    gw = 1.0 / jnp.sum(jnp.where(is_g, jnp.exp(logits - gmax), 0.0), -1, keepdims=True)
    ex = lane - N_GROUPS
    in_group = (ex >= 0) & (ex < N_EXP) & ((ex // EPG).astype(F32) == gidx)
    el = jnp.where(in_group, logits, ninf)
    v1 = jnp.max(el, -1, keepdims=True)
    i1 = jnp.min(jnp.where(el == v1, lanef, big), -1, keepdims=True)
    el2 = jnp.where(lanef == i1, ninf, el)
    v2 = jnp.max(el2, -1, keepdims=True)
    i2 = jnp.min(jnp.where(el2 == v2, lanef, big), -1, keepdims=True)
    e21 = jnp.exp(v2 - v1)
    w1 = gw / (1.0 + e21)
    w2 = gw * e21 / (1.0 + e21)
    e1 = i1 - N_GROUPS
    e2 = i2 - N_GROUPS

    oh1 = jnp.where(lanef == e1, 1.0, 0.0)
    oh2 = jnp.where(lanef == e2, 1.0, 0.0)
    ex1 = dot(tri_ref[...], oh1.astype(BF16))
    ex2 = dot(tri_ref[...], oh2.astype(BF16))
    col1 = jnp.sum(oh1, axis=0, keepdims=True)
    col2 = jnp.sum(oh2, axis=0, keepdims=True)
    n = col1 + col2
    run = jnp.floor((n + (BLK - 1.0)) * (1.0 / BLK)) * BLK
    start = dot(jnp.broadcast_to(run, (8, 128)).astype(BF16), upper_ref[...])[0:1]
    loc1 = jnp.sum(oh1 * (start + ex1), -1, keepdims=True)
    loc2 = jnp.sum(oh2 * (start + col1 + ex2), -1, keepdims=True)
    cnt_ref[...] = jnp.broadcast_to(n, cnt_ref.shape)

    route = jnp.zeros_like(logits)
    for j, val in enumerate((e1, e2, w1, w2, loc1, loc2)):
        route = jnp.where(lane == j, val, route)
    route_ref[...] = route

    loc1_row = jnp.transpose(jnp.broadcast_to(loc1, (TM, 128)))[0:1].astype(jnp.int32)
    loc2_row = jnp.transpose(jnp.broadcast_to(loc2, (TM, 128)))[0:1].astype(jnp.int32)
    slot = lax.broadcasted_iota(jnp.int32, (LOC, TM), 0)
    perm = jnp.where((slot == loc1_row) | (slot == loc2_row), 1.0, 0.0).astype(BF16)
    xloc_ref[...] = dot(perm, h_hi).astype(BF16)


def _out_proj(l, a, hyb_pair, att_pair, w_out, x_pair, norm, mod_l, g, b, w_route, b_route, tri, upper):
    tile = lambda n: pl.BlockSpec((TM, n), lambda i: (i, 0))
    return pl.pallas_call(
        functools.partial(_out_kernel, layer=l),
        out_shape=(jax.ShapeDtypeStruct((T, D), F32), jax.ShapeDtypeStruct((N_TILES * LOC, D), BF16),
                   jax.ShapeDtypeStruct((T, 128), F32), jax.ShapeDtypeStruct((N_TILES * 8, 128), F32)),
        grid=(N_TILES,),
        in_specs=[tile(GM_W)] + _pair_specs(HY_W) + _pair_specs(HEADS * VD)
        + [pl.BlockSpec((None, D, D), lambda i: (l, 0, 0), pipeline_mode=pl.Buffered(1))]
        + _pair_specs(D) + [
                  _full((1, D)), _full((1, D)),
                  _mod_spec(l),
                  _full((DEPTH, D)), _full((DEPTH, D)), _layer_spec(l, D, 128), _full((DEPTH, 128)),
                  _full((TM, TM)), _full((128, 128))],
        out_specs=(tile(D), pl.BlockSpec((LOC, D), lambda i: (i, 0)), tile(128),
                   pl.BlockSpec((8, 128), lambda i: (i, 0))),
        scratch_shapes=[pltpu.VMEM((D, D), BF16)],
        compiler_params=_cparams("arbitrary"),
        name="out_proj_route",
    )(a, *hyb_pair, *att_pair, w_out, *x_pair, *norm, mod_l, g, b, w_route, b_route, tri, upper)


X_RING = 8


def _expert_kernel(off_ref, cnt_ref, src_ref, x_hbm, wg_ref, wu_ref, wd_ref, y_hbm,
                   xbuf, ybuf, zbuf, wgu_b, wd_b, sem, osem, zsem):
    e = pl.program_id(0)
    first_tile = off_ref[e]
    n_tiles = cnt_ref[e]
    total = off_ref[N_EXP - 1] + cnt_ref[N_EXP - 1]

    def block_copies(t, s):
        return [pltpu.make_async_copy(
            x_hbm.at[pl.ds(pl.multiple_of(src_ref[t * BLK_PER_TILE + b] * BLK, BLK), BLK), :],
            xbuf.at[s, pl.ds(b * BLK, BLK), :], sem.at[s]) for b in range(BLK_PER_TILE)]

    def out_copy(t, s):
        return pltpu.make_async_copy(ybuf.at[s], y_hbm.at[pl.ds(pl.multiple_of(t * TM, TM), TM), :],
                                     osem.at[s])

    def zero_copy(t):
        return pltpu.make_async_copy(zbuf, y_hbm.at[pl.ds(pl.multiple_of(t * TM, TM), TM), :], zsem)

    @pl.when(e == 0)
    def _():
        zbuf[...] = jnp.zeros((TM, D), BF16)

        def zero_start(t, carry):
            zero_copy(t).start()
            return carry

        lax.fori_loop(total, MOE_TILES, zero_start, 0)
        for t0 in range(X_RING - 1):
            @pl.when(t0 < total)
            def _():
                for cp in block_copies(t0, t0):
                    cp.start()

    @pl.when(n_tiles > 0)
    def _():
        wgu_b[:, 0:D_EXP] = wg_ref[0, 0].astype(BF16)
        wgu_b[:, D_EXP:2 * D_EXP] = wu_ref[0, 0].astype(BF16)
        wd_b[...] = wd_ref[0, 0].astype(BF16)

    def tile_body(t, carry):
        s = t % 2
        xs = t % X_RING

        @pl.when(t + X_RING - 1 < total)
        def _():
            for cp in block_copies(t + X_RING - 1, (t + X_RING - 1) % X_RING):
                cp.start()

        for cp in block_copies(t, xs):
            cp.wait()

        @pl.when(t >= 2)
        def _():
            out_copy(t - 2, s).wait()

        gu = jnp.dot(xbuf[xs], wgu_b[...], preferred_element_type=F32)
        gate, up = gu[:, :D_EXP], gu[:, D_EXP:]
        hid = gate / (1.0 + jnp.exp(-gate)) * up
        ybuf[s] = jnp.dot(hid.astype(BF16), wd_b[...], preferred_element_type=F32).astype(BF16)
        out_copy(t, s).start()
        return carry

    lax.fori_loop(first_tile, first_tile + n_tiles, tile_body, 0)

    @pl.when(e == N_EXP - 1)
    def _():
        @pl.when(total >= 2)
        def _():
            out_copy(total - 2, total % 2).wait()

        out_copy(total - 1, (total - 1) % 2).wait()

        def zero_wait(t, carry):
            zero_copy(t).wait()
            return carry

        lax.fori_loop(total, MOE_TILES, zero_wait, 0)


def _experts(l, tables, xloc, w_gate, w_up, w_down):
    grid_spec = pltpu.PrefetchScalarGridSpec(
        num_scalar_prefetch=len(tables),
        grid=(N_EXP,),
        in_specs=[pl.BlockSpec(memory_space=pl.ANY),
                  pl.BlockSpec((1, 1, D, D_EXP), lambda e, *_: (l, e, 0, 0)),
                  pl.BlockSpec((1, 1, D, D_EXP), lambda e, *_: (l, e, 0, 0)),
                  pl.BlockSpec((1, 1, D_EXP, D), lambda e, *_: (l, e, 0, 0))],
        out_specs=pl.BlockSpec(memory_space=pl.ANY),
        scratch_shapes=[pltpu.VMEM((X_RING, TM, D), BF16), pltpu.VMEM((2, TM, D), BF16),
                        pltpu.VMEM((TM, D), BF16),
                        pltpu.VMEM((D, 2 * D_EXP), BF16), pltpu.VMEM((D_EXP, D), BF16),
                        pltpu.SemaphoreType.DMA((X_RING,)), pltpu.SemaphoreType.DMA((2,)),
                        pltpu.SemaphoreType.DMA(())],
    )
    return pl.pallas_call(
        _expert_kernel,
        out_shape=jax.ShapeDtypeStruct((MOE_ROWS, D), BF16),
        grid_spec=grid_spec,
        compiler_params=_cparams("arbitrary"),
        name="experts",
    )(*tables, xloc, w_gate, w_up, w_down)


def _combine_kernel(nblk_ref, gsrc_ref, y_hbm, route_ref, x1_ref, mod_ref, g_ref, b_ref, oc_ref, ol_ref,
                    ybuf, sem, *, layer):
    i = pl.program_id(0)

    slot_i = i % 2

    def block_copy(t, s, lb):
        src = pl.multiple_of(gsrc_ref[t * LOC_BLKS + lb] * BLK, BLK)
        return pltpu.make_async_copy(y_hbm.at[pl.ds(src, BLK), :],
                                     ybuf.at[s, pl.ds(pl.multiple_of(lb * BLK, BLK), BLK), :], sem.at[s])

    def fetch(t, s):
        def start(lb, c):
            block_copy(t, s, lb).start()
            return c
        lax.fori_loop(0, nblk_ref[t], start, 0)

    @pl.when(i == 0)
    def _():
        ybuf[...] = jnp.zeros_like(ybuf)
        fetch(0, 0)

    @pl.when(i + 1 < N_TILES)
    def _():
        fetch(jnp.minimum(i + 1, N_TILES - 1), 1 - slot_i)

    def wait(lb, c):
        block_copy(i, slot_i, lb).wait()
        return c

    lax.fori_loop(0, nblk_ref[i], wait, 0)
    route = route_ref[...]
    yb = ybuf[slot_i]
    slot = lax.broadcasted_iota(jnp.int32, (TM, LOC), 1)
    pick = (jnp.where(slot == route[:, 4:5].astype(jnp.int32), route[:, 2:3], 0.0)
            + jnp.where(slot == route[:, 5:6].astype(jnp.int32), route[:, 3:4], 0.0)).astype(BF16)
    moe = jnp.dot(pick, yb, preferred_element_type=F32)
    m = mod_ref[0]
    res = _layer_norm(ALPHA * x1_ref[...] + m[5:6] * moe, g_ref[layer:layer + 1, :], b_ref[layer:layer + 1, :])

    @pl.when(i < CTX_TILES)
    def _():
        oc_ref[...] = res

    @pl.when(i >= CTX_TILES)
    def _():
        ol_ref[...] = res


def _combine(l, nblk, gsrc, y, route, x1, mod_l, g, b):
    out_shape = (jax.ShapeDtypeStruct((T_CTX, D), F32), jax.ShapeDtypeStruct((T_LAT, D), F32))
    out_specs = tuple(_pair_specs(D))
    grid_spec = pltpu.PrefetchScalarGridSpec(
        num_scalar_prefetch=2,
        grid=(N_TILES,),
        in_specs=[pl.BlockSpec(memory_space=pl.ANY),
                  pl.BlockSpec((TM, 128), lambda i, nb, gs: (i, 0)),
                  pl.BlockSpec((TM, D), lambda i, nb, gs: (i, 0)),
                  _mod_spec(l), _full((DEPTH, D)), _full((DEPTH, D))],
        out_specs=out_specs,
        scratch_shapes=[pltpu.VMEM((2, LOC, D), BF16), pltpu.SemaphoreType.DMA((2,))],
    )
    return pl.pallas_call(
        functools.partial(_combine_kernel, layer=l),
        out_shape=out_shape,
        grid_spec=grid_spec,
        compiler_params=_cparams("arbitrary"),
        name="moe_combine",
    )(nblk, gsrc, y, route, x1, mod_l, g, b)


def _prefix_pick(starts, query, table):
    delta = table - jnp.concatenate([jnp.zeros_like(table[..., :1]), table[..., :-1]], axis=-1)
    return jnp.sum(jnp.where(starts <= query, delta, 0), axis=-1)


def _dispatch_tables(cnt):
    i32 = jnp.int32
    run = (cnt + BLK - 1) // BLK * BLK
    loc_start = jnp.cumsum(run, axis=1) - run
    nblk_loc = jnp.sum(run, axis=1) // BLK
    seg_rows = jnp.sum(run, axis=0)
    seg_tiles = (seg_rows + TM - 1) // TM
    tile_end = jnp.cumsum(seg_tiles)
    seg_start = (tile_end - seg_tiles) * TM
    glob_start = seg_start[None, :] + jnp.cumsum(run, axis=0) - run

    g_blk = (glob_start.T.reshape(1, -1)) // BLK
    n_blk = (run.T.reshape(1, -1)) // BLK
    l_blk = ((jnp.arange(N_TILES, dtype=i32) * LOC)[None, :] + loc_start.T).reshape(1, -1) // BLK
    gb = jnp.arange(MOE_ROWS // BLK, dtype=i32)[:, None]
    off = gb[:, 0] - _prefix_pick(g_blk, gb, g_blk)
    src_blk = jnp.where(off < _prefix_pick(g_blk, gb, n_blk), _prefix_pick(g_blk, gb, l_blk) + off,
                        ZERO_BLK).astype(i32)

    pos = (jnp.arange(LOC_BLKS, dtype=i32) * BLK)[None, :, None]
    shift = _prefix_pick(loc_start[:, None, :], pos, (glob_start - loc_start)[:, None, :])
    gsrc = ((pos[:, :, 0] + shift) // BLK).astype(i32)
    expert_tables = ((tile_end - seg_tiles).astype(i32), seg_tiles.astype(i32), src_blk)
    return expert_tables, nblk_loc.astype(i32), gsrc.reshape(-1)


def _rope_slot(pe):
    return jnp.pad(pe, [(0, 0)] * (pe.ndim - 1) + [(NOPE, SLOT - NOPE - ROPE)])


def _wuq_layout(w):
    w = w.reshape(DEPTH, Q_LORA, HEADS, NOPE + ROPE)
    w = jnp.pad(w, ((0, 0), (0, 0), (0, 0), (0, SLOT - NOPE - ROPE)))
    return w.reshape(DEPTH, Q_LORA, HEADS * SLOT).astype(BF16)


def _wukv_layout(w):
    w = w.reshape(DEPTH, KV_LORA, HEADS, NOPE + VD)
    wk = jnp.pad(w[..., :NOPE], ((0, 0), (0, 0), (0, 0), (0, SLOT - NOPE)))
    wv = w[..., NOPE:].reshape(DEPTH, KV_LORA, HEADS * VD).astype(BF16)
    return wk.reshape(DEPTH, KV_LORA, HEADS * SLOT).astype(BF16), wv, jnp.swapaxes(wv, 1, 2)


def kernel(x_prompt, x_sample, c, cache_ckv, cache_kpe, c_ctx, ln_in_g, ln_in_b, w_mod, b_mod, w_in,
           gm_ln_g, gm_ln_b, gm_ws, gm_bs, hy_conv_w, hy_conv_b, hy_f_w1, hy_f_b1, hy_f_w2, hy_f_b2,
           hy_f_w3, hy_f_freq, hy_bias, mla_gq, mla_gkv, mla_wuq, mla_wukv, w_out, ln1_g, ln1_b,
           ln2_g, ln2_b, moe_w_gr, moe_b_gr, moe_w_er, moe_b_er, moe_w_gate, moe_w_up, moe_w_down):
    rope = _rope_tables()
    dft = {L: _dft_tables(L) for L in (CTX_L, LAT_L)}
    pos_tab = {L: _hyena_positions(L) for L in (CTX_L, LAT_L)}
    hd = np.arange(GM_W) // GM_HD
    avg = jnp.asarray((hd[:, None] == hd[None, :]) / GM_HD, BF16)
    tri = jnp.asarray(np.tril(np.ones((TM, TM)), -1), BF16)
    upper = jnp.asarray(np.triu(np.ones((128, 128)), 1), BF16)

    cond8 = jnp.concatenate([c_ctx[None], c, jnp.zeros((8 - 1 - N_LAT_B, D), F32)], axis=0)
    mod = _modulation(cond8, w_mod, b_mod).reshape(DEPTH, 8, 6, D)

    wuk_all, wuv_all, wuvt_all = _wukv_layout(mla_wukv)
    w_in_t = jnp.swapaxes(w_in, 1, 2)
    wuq_arr = _wuq_layout(mla_wuq)
    ws_b = gm_ws.astype(BF16)
    kc_all, vc_all = _cache_kv(cache_ckv, _rope_slot(cache_kpe), wuk_all, wuvt_all)

    x_pair = (x_prompt.reshape(T_CTX, D), x_sample.reshape(T_LAT, D))
    norm = (ln_in_g.reshape(1, D), ln_in_b.reshape(1, D))
    bs_full = jnp.repeat(jnp.swapaxes(gm_bs, 1, 2), GM_HD, axis=2)
    w_route = jnp.pad(jnp.concatenate([moe_w_gr, moe_w_er.reshape(DEPTH, D, N_EXP)], axis=2),
                      ((0, 0), (0, 0), (0, 128 - N_GROUPS - N_EXP)))
    b_route = jnp.pad(jnp.concatenate([moe_b_gr, moe_b_er.reshape(DEPTH, N_EXP)], axis=1),
                      ((0, 0), (0, 128 - N_GROUPS - N_EXP)))
    hyena_groups = ((CTX_L, N_CTX_B, 4, 0), (LAT_L, N_LAT_B, 2, T_CTX // LAT_L))
    spectra = {L: _filter_spectra(L, *pos_tab[L], dft[L], hy_f_w1, hy_f_b1, hy_f_w2, hy_f_b2, hy_f_w3,
                                  hy_f_freq) for L, _, _, _ in hyena_groups}
    ckv_states, kpe_states = [], []
    for l in range(DEPTH):
        a, hy, q, k, v, vt, ckv, kpe = _in_proj(
            l, x_pair, norm, mod, w_in_t, gm_ln_g, gm_ln_b, avg,
            ws_b, bs_full, mla_gq, mla_gkv, wuq_arr, wuk_all, wuv_all, wuvt_all, rope)
        ckv_states.append(ckv[:T_CTX].reshape(N_CTX_B, CTX_L, KV_LORA))
        kpe_states.append(kpe[:T_CTX, NOPE:NOPE + ROPE].reshape(N_CTX_B, CTX_L, ROPE))

        hyb = [_hyena_group(l, hy, L, nb, group, blk0, dft[L], spectra[L], hy_conv_w, hy_conv_b, hy_bias)
               for L, nb, group, blk0 in hyena_groups]

        att = _attention(l, q, k, v, vt, kc_all, vc_all)

        x1, xloc, route, counts = _out_proj(l, a, hyb, att, w_out, x_pair, norm, mod,
                                            ln1_g, ln1_b, w_route, b_route, tri, upper)
        cnt = counts.reshape(N_TILES, 8, 128)[:, 0, :N_EXP].astype(jnp.int32)
        expert_tables, nblk_loc, gsrc = _dispatch_tables(cnt)
        y = _experts(l, expert_tables, xloc, moe_w_gate, moe_w_up, moe_w_down)
        x_pair = _combine(l, nblk_loc, gsrc, y, route, x1, mod, ln2_g, ln2_b)

    y_prompt = x_pair[0].reshape(N_CTX_B, CTX_L, D)
    y_sample = x_pair[1].reshape(N_LAT_B, LAT_L, D)
    return (y_prompt, y_sample, jnp.stack(ckv_states, axis=1), jnp.stack(kpe_states, axis=1))
```

```python
import functools
import math

import numpy as np
import jax
import jax.numpy as jnp
from jax import lax
from jax.experimental import pallas as pl
from jax.experimental.pallas import tpu as pltpu

D = 1024
N_CTX_B, CTX_L = 16, 256
N_LAT_B, LAT_L = 4, 1024
DEPTH = 2
T_CTX = N_CTX_B * CTX_L
T_LAT = N_LAT_B * LAT_L
T = T_CTX + T_LAT
PAST = 256
GRID_W = 64

GM_HEADS, GM_HD, GM_W, CHUNK = 4, 64, 256, 128
HY_W, HY_EMB, HY_FH = 256, 33, 64
NOPE, ROPE, VD, HEADS = 64, 32, 64, 8
Q_LORA, KV_LORA = 256, 128
SLOT = 128
N_GROUPS, EPG, N_EXP, D_EXP = 4, 8, 32, 256
ALPHA = (2.0 * DEPTH) ** 0.25
LN_EPS, RMS_EPS = 1e-5, 1e-6
ROPE_THETA = 10000.0

TM = 256
N_TILES = T // TM
CTX_TILES = T_CTX // TM
LAT_TILES_PER_B = LAT_L // TM
N_ASSIGN = 2 * T
BLK = 16
BLK_PER_TILE = TM // BLK
LOC = 2 * TM + N_EXP * BLK
LOC_BLKS = LOC // BLK
ZERO_BLK = LOC_BLKS - 1
MOE_TILES = (N_ASSIGN + N_TILES * N_EXP * (BLK - 1)) // TM + N_EXP
MOE_ROWS = MOE_TILES * TM
IN_COLS = 1696
IN_EXT = 1792

F32, BF16 = jnp.float32, jnp.bfloat16
_NT = (((1,), (1,)), ((), ()))
VMEM_LIMIT = 52 * 1024 * 1024


def _cparams(*sem):
    return pltpu.CompilerParams(dimension_semantics=sem, vmem_limit_bytes=VMEM_LIMIT)


def _cond_of_tile(i):
    return jnp.where(i < CTX_TILES, 0, 1 + (i - CTX_TILES) // LAT_TILES_PER_B)


def _rope_block_of_tile(i):
    return jnp.where(i < CTX_TILES, 0, 1 + (i - CTX_TILES) % LAT_TILES_PER_B)


def _full(shape):
    n = len(shape)
    return pl.BlockSpec(shape, lambda *_: (0,) * n)


def _layer_spec(l, *shape):
    return pl.BlockSpec((None,) + shape, lambda *_: (l,) + (0,) * len(shape))


def _mod_spec(l):
    return pl.BlockSpec((None, 1, 6, D), lambda i, *_: (l, _cond_of_tile(i), 0, 0))


def _pair_specs(n):
    return [pl.BlockSpec((TM, n), lambda i, *_: (jnp.minimum(i, CTX_TILES - 1), 0)),
            pl.BlockSpec((TM, n), lambda i, *_: (jnp.maximum(i - CTX_TILES, 0), 0))]


def _pair_read(c_ref, l_ref):
    return jnp.where(pl.program_id(0) < CTX_TILES, c_ref[...], l_ref[...])


def _layer_norm(x, g, b):
    mu = jnp.mean(x, -1, keepdims=True)
    xc = x - mu
    var = jnp.mean(xc * xc, -1, keepdims=True)
    return xc * lax.rsqrt(var + LN_EPS) * g + b


def _rms_norm(x, g):
    return x * lax.rsqrt(jnp.mean(x * x, -1, keepdims=True) + RMS_EPS) * g


def _bdot(a, b):
    return jnp.dot(a.astype(BF16), b.astype(BF16), preferred_element_type=F32)


def _dot3(x, w):
    x_hi = x.astype(BF16)
    x_lo = (x - x_hi.astype(F32)).astype(BF16)
    w_hi = w.astype(BF16)
    w_lo = (w - w_hi.astype(F32)).astype(BF16)
    dot = functools.partial(jnp.dot, preferred_element_type=F32)
    return dot(x_hi, w_hi) + dot(x_hi, w_lo) + dot(x_lo, w_hi)


def _swap_pair_lanes(x):
    n = x.shape[1]
    lane = lax.broadcasted_iota(jnp.int32, x.shape, 1)
    return jnp.where(lane % 2 == 0, pltpu.roll(x, n - 1, 1), pltpu.roll(x, 1, 1))


def _split2_dot(x, w_bf16):
    hi = x.astype(BF16)
    lo = (x - hi.astype(F32)).astype(BF16)
    dot = functools.partial(jnp.dot, preferred_element_type=F32)
    return dot(hi, w_bf16) + dot(lo, w_bf16)


def _dft_tables(L):
    f = np.arange(L, dtype=np.int64)
    ft = np.outer(f, f) % (2 * L)
    ang = np.pi * ft / L
    c = np.cos(ang)
    s = np.sin(ang)
    alt = np.where(f % 2 == 0, 1.0, -1.0)
    sf = s.copy()
    sf[0, :] = alt
    return (jnp.asarray(c, F32).astype(BF16), jnp.asarray(sf, F32).astype(BF16),
            jnp.asarray(sf.T.copy(), F32).astype(BF16), jnp.asarray(alt[:, None], F32))


def _rope_tables():
    rows = LAT_L // GRID_W
    row = np.repeat(np.arange(rows, dtype=np.float64), GRID_W)
    col = np.tile(np.arange(GRID_W, dtype=np.float64), rows)
    n = ROPE // 4
    inv = ROPE_THETA ** (-np.arange(n, dtype=np.float64) / n)
    ang = np.concatenate([row[:, None] * inv, col[:, None] * inv], -1)
    cos = np.zeros((CTX_L + LAT_L, SLOT))
    sin = np.zeros((CTX_L + LAT_L, SLOT))
    cos[:, :NOPE + ROPE] = 1.0
    cos[CTX_L:, NOPE:NOPE + ROPE:2] = np.cos(ang)
    cos[CTX_L:, NOPE + 1:NOPE + ROPE:2] = np.cos(ang)
    sin[CTX_L:, NOPE:NOPE + ROPE:2] = -np.sin(ang)
    sin[CTX_L:, NOPE + 1:NOPE + ROPE:2] = np.sin(ang)
    scale = math.log2(math.e) / math.sqrt(NOPE + ROPE)
    return (jnp.asarray(cos * scale, F32), jnp.asarray(sin * scale, F32),
            jnp.asarray(cos, F32), jnp.asarray(sin, F32))


def _hyena_positions(L):
    t = jnp.linspace(0.0, 1.0, L, dtype=F32)[:, None]
    bands = (HY_EMB - 1) // 2
    w = 2.0 * math.pi * jnp.arange(L, dtype=F32)[:, None] / L
    f = jnp.linspace(1e-4, bands - 1, bands, dtype=F32)[None]
    z = jnp.concatenate([t, jnp.cos(f * w), -jnp.sin(f * w)], -1)
    z = jnp.pad(z, ((0, 0), (0, 128 - HY_EMB)))
    min_decay = math.log(1e-2) / 1.5
    max_decay = math.log(1e-2) / 0.3
    deltas = jnp.abs(jnp.linspace(min_decay, max_decay, HY_W, dtype=F32))
    return z, jnp.exp(-t * deltas)


def _mod_kernel(c_ref, w_ref, b_ref, o_ref):
    c = c_ref[...]
    s = c / (1.0 + jnp.exp(-c))
    o_ref[0] = _bdot(s, w_ref[0]) + b_ref[0]


def _modulation(cond8, w_mod, b_mod):
    tn = 1536
    return pl.pallas_call(
        _mod_kernel,
        out_shape=jax.ShapeDtypeStruct((DEPTH, 8, 6 * D), F32),
        grid=(DEPTH, 6 * D // tn),
        in_specs=[pl.BlockSpec((8, D), lambda l, j: (0, 0)),
                  pl.BlockSpec((1, D, tn), lambda l, j: (l, 0, j)),
                  pl.BlockSpec((1, 1, tn), lambda l, j: (l, 0, j))],
        out_specs=pl.BlockSpec((1, 8, tn), lambda l, j: (l, 0, j)),
        compiler_params=_cparams("parallel", "parallel"),
        name="modulation",
    )(cond8, w_mod, b_mod.reshape(DEPTH, 1, 6 * D))


def _filter_kernel(z_ref, w1_ref, b1_ref, w2_ref, b2_ref, w3_ref, fr_ref, dec_ref, c_ref, sf_ref,
                   alt_ref, kr_ref, ki_ref, krn_ref, *, L):
    h1 = jnp.sin(fr_ref[0:1] * (_dot3(z_ref[...], w1_ref[...]) + b1_ref[...]))
    h2 = jnp.sin(fr_ref[1:2] * (_dot3(h1, w2_ref[...]) + b2_ref[...]))
    filt = _dot3(h2, w3_ref[...])
    row = lax.broadcasted_iota(jnp.int32, (L, HY_W), 0)
    dec = dec_ref[...]
    n = DEPTH * 2
    fwd = [filt[:, j * 2 * HY_W:j * 2 * HY_W + HY_W] * dec for j in range(n)]
    bwd = [jnp.where(row == 0, 0.0, filt[:, j * 2 * HY_W + HY_W:(j + 1) * 2 * HY_W] * dec) for j in range(n)]
    sm = [f + b for f, b in zip(fwd, bwd)]
    kr_all = _bdot(c_ref[...], jnp.concatenate(sm, axis=1))
    ki_all = -_bdot(sf_ref[...], jnp.concatenate([f - b for f, b in zip(fwd, bwd)], axis=1))
    for j in range(n):
        l, o = divmod(j, 2)
        kr = kr_all[:, j * HY_W:(j + 1) * HY_W]
        kr_nyq = jnp.sum(alt_ref[...] * sm[j], axis=0, keepdims=True)
        kr_ref[l, o] = jnp.where(row == 0, kr * (0.5 / L), kr * (1.0 / L))
        ki_ref[l, o] = jnp.where(row == 0, 0.0, ki_all[:, j * HY_W:(j + 1) * HY_W] * (1.0 / L))
        krn_ref[l, o] = jnp.where(row == 0, kr_nyq * (0.5 / L), kr * (1.0 / L))


def _block_diag(blocks):
    rows = []
    for i, blk in enumerate(blocks):
        rows.append(jnp.concatenate([blk if j == i else jnp.zeros((blk.shape[0], other.shape[1]), blk.dtype)
                                     for j, other in enumerate(blocks)], axis=1))
    return jnp.concatenate(rows, axis=0)


def _filter_spectra(L, z, dec, dft, w1, b1, w2, b2, w3, freq):
    assert DEPTH * HY_FH == 128
    c, sf, _, alt = dft
    layers = range(DEPTH)
    w1c = jnp.pad(jnp.concatenate([w1[l] for l in layers], axis=1), ((0, 128 - HY_EMB), (0, 0)))
    w2c = _block_diag([w2[l] for l in layers])
    w3c = _block_diag([w3[l] for l in layers])
    b1c, b2c = b1.reshape(1, 128), b2.reshape(1, 128)
    frc = jnp.swapaxes(freq, 0, 1).reshape(2, 128)
    shp = jax.ShapeDtypeStruct((DEPTH, 2, L, HY_W), F32)
    args = (z, w1c, b1c, w2c, b2c, w3c, frc, dec, c, sf, alt)
    return pl.pallas_call(
        functools.partial(_filter_kernel, L=L),
        out_shape=(shp, shp, shp),
        grid=(1,),
        in_specs=[_full(a.shape) for a in args],
        out_specs=(_full((DEPTH, 2, L, HY_W)),) * 3,
        compiler_params=_cparams("arbitrary"),
        name=f"hyena_filter_{L}",
    )(*args)


def _in_kernel(xc_ref, xl_ref, ng_ref, nb_ref, mod_ref, wt_ref, gg_ref, gb_ref, avg_ref, ws_ref, bs_ref,
               gq_ref, gkv_ref, wuq_ref, wuk_ref, wuv_ref, wuvt_ref, cq_ref, sq_ref, ck_ref, sk_ref,
               a_ref, hy_ref, q_ref, k_ref, v_ref, vt_ref, ckv_ref, kpe_ref, w_b, *, layer):
    l = layer

    @pl.when(pl.program_id(0) == 0)
    def _():
        for c0 in range(0, 1664, 128):
            w_b[:, c0:c0 + 128] = jnp.transpose(wt_ref[c0:c0 + 128, :]).astype(BF16)
        slot_rows = jnp.concatenate([jnp.zeros((NOPE, D), F32), wt_ref[1664:IN_COLS, :],
                                     jnp.zeros((SLOT - NOPE - ROPE, D), F32)], axis=0)
        w_b[:, 1664:IN_EXT] = jnp.transpose(slot_rows).astype(BF16)

    x = _pair_read(xc_ref, xl_ref)
    if l == 0:
        x = _layer_norm(x, ng_ref[...], nb_ref[...])
    m = mod_ref[0]
    h = x * (1.0 + m[1:2]) + m[0:1]
    proj = _bdot(h, w_b[...])

    gu = jax.nn.gelu(proj[:, 0:GM_W], approximate=True)
    gv = jax.nn.gelu(proj[:, GM_W:2 * GM_W], approximate=True)
    avg = avg_ref[...]
    mu = _split2_dot(gv, avg)
    vc = gv - mu
    var = _split2_dot(vc * vc, avg)
    vln = (vc * lax.rsqrt(var + LN_EPS) * gg_ref[l:l + 1, :] + gb_ref[l:l + 1, :]).astype(BF16)
    lane = lax.broadcasted_iota(jnp.int32, (CHUNK, GM_W), 1)
    for c in range(TM // CHUNK):
        vchunk = vln[c * CHUNK:(c + 1) * CHUNK]
        s = bs_ref[...]
        for hd in range(GM_HEADS):
            sh = jnp.dot(ws_ref[0, hd], vchunk, preferred_element_type=F32)
            s = s + jnp.where(lane // GM_HD == hd, sh, 0.0)
        a_ref[c * CHUNK:(c + 1) * CHUNK, :] = (gu[c * CHUNK:(c + 1) * CHUNK] * s).astype(BF16)

    hy_ref[...] = proj[:, 512:1280]

    cq = _rms_norm(proj[:, 1280:1536], gq_ref[l:l + 1, :])
    qq = _bdot(cq, wuq_ref[0])
    cos_q = jnp.concatenate([cq_ref[...]] * HEADS, axis=1)
    sin_q = jnp.concatenate([sq_ref[...]] * HEADS, axis=1)
    q_ref[...] = (qq * cos_q + _swap_pair_lanes(qq) * sin_q).astype(BF16)

    ckv = _rms_norm(proj[:, 1536:1664], gkv_ref[l:l + 1, :])
    ckv_ref[...] = ckv
    kpe = proj[:, 1664:IN_EXT]
    kpe_ref[...] = kpe
    krot = kpe * ck_ref[...] + _swap_pair_lanes(kpe) * sk_ref[...]
    kn = _bdot(ckv, wuk_ref[0])
    k_ref[...] = (kn + jnp.concatenate([krot] * HEADS, axis=1)).astype(BF16)
    v_ref[...] = _bdot(ckv, wuv_ref[0]).astype(BF16)
    vt_ref[...] = lax.dot_general(wuvt_ref[0], ckv.astype(BF16), _NT,
                                  preferred_element_type=F32).astype(BF16)


def _in_proj(l, x_pair, norm, mod_l, w_in_t, gm_g, gm_b, avg, ws, bs_full, gq, gkv, wuq_arr,
             wuk_arr, wuv_arr, wuvt_arr, rope):
    layer = lambda *shape: pl.BlockSpec((1,) + shape, lambda i: (l,) + (0,) * len(shape))
    cos_q, sin_q, cos_k, sin_k = rope
    tile = lambda n: pl.BlockSpec((TM, n), lambda i: (i, 0))
    rope_spec = pl.BlockSpec((TM, SLOT), lambda i: (_rope_block_of_tile(i), 0))
    out_shapes = (jax.ShapeDtypeStruct((T, GM_W), BF16),
                  jax.ShapeDtypeStruct((T, 3 * HY_W), F32),
                  jax.ShapeDtypeStruct((T, HEADS * SLOT), BF16),
                  jax.ShapeDtypeStruct((T, HEADS * SLOT), BF16),
                  jax.ShapeDtypeStruct((T, HEADS * VD), BF16),
                  jax.ShapeDtypeStruct((HEADS * VD, T), BF16),
                  jax.ShapeDtypeStruct((T, KV_LORA), F32),
                  jax.ShapeDtypeStruct((T, SLOT), F32))
    return pl.pallas_call(
        functools.partial(_in_kernel, layer=l),
        out_shape=out_shapes,
        grid=(N_TILES,),
        in_specs=_pair_specs(D) + [
                  _full((1, D)), _full((1, D)),
                  _mod_spec(l),
                  pl.BlockSpec((None, IN_COLS, D), lambda i: (l, 0, 0), pipeline_mode=pl.Buffered(1)),
                  _full((DEPTH, GM_W)), _full((DEPTH, GM_W)), _full((GM_W, GM_W)),
                  layer(GM_HEADS, CHUNK, CHUNK), _layer_spec(l, CHUNK, GM_W),
                  _full((DEPTH, Q_LORA)), _full((DEPTH, KV_LORA)),
                  layer(Q_LORA, HEADS * SLOT), layer(KV_LORA, HEADS * SLOT),
                  layer(KV_LORA, HEADS * VD), layer(HEADS * VD, KV_LORA),
                  rope_spec, rope_spec, rope_spec, rope_spec],
        out_specs=(tile(GM_W), tile(3 * HY_W), tile(HEADS * SLOT), tile(HEADS * SLOT), tile(HEADS * VD),
                   pl.BlockSpec((HEADS * VD, TM), lambda i: (0, i)), tile(KV_LORA), tile(SLOT)),
        scratch_shapes=[pltpu.VMEM((D, IN_EXT), BF16)],
        compiler_params=_cparams("arbitrary"),
        name="in_proj",
    )(*x_pair, *norm, mod_l, w_in_t, gm_g, gm_b, avg, ws, bs_full, gq, gkv, wuq_arr, wuk_arr, wuv_arr,
      wuvt_arr, cos_q, sin_q, cos_k, sin_k)


def _cache_kernel(ckv_ref, kpe_ref, wuk_ref, wuv_ref, k_ref, v_ref):
    ckv = ckv_ref[0, 0]
    kn = _bdot(ckv, wuk_ref[0])
    k_ref[0, 0] = (kn + jnp.concatenate([kpe_ref[0, 0]] * HEADS, axis=1)).astype(BF16)
    v_ref[0, 0] = lax.dot_general(wuv_ref[0], ckv.astype(BF16), _NT,
                                  preferred_element_type=F32).astype(BF16)


def _cache_kv(cache_ckv, cache_kpe_slot, wuk_arr, wuv_arr):
    return pl.pallas_call(
        _cache_kernel,
        out_shape=(jax.ShapeDtypeStruct((DEPTH, N_LAT_B, PAST, HEADS * SLOT), BF16),
                   jax.ShapeDtypeStruct((DEPTH, N_LAT_B, HEADS * VD, PAST), BF16)),
        grid=(DEPTH, N_LAT_B),
        in_specs=[pl.BlockSpec((1, 1, PAST, KV_LORA), lambda l, b: (b, l, 0, 0)),
                  pl.BlockSpec((1, 1, PAST, SLOT), lambda l, b: (b, l, 0, 0)),
                  pl.BlockSpec((1, KV_LORA, HEADS * SLOT), lambda l, b: (l, 0, 0)),
                  pl.BlockSpec((1, HEADS * VD, KV_LORA), lambda l, b: (l, 0, 0))],
        out_specs=(pl.BlockSpec((1, 1, PAST, HEADS * SLOT), lambda l, b: (l, b, 0, 0)),
                   pl.BlockSpec((1, 1, HEADS * VD, PAST), lambda l, b: (l, b, 0, 0))),
        compiler_params=_cparams("parallel", "parallel"),
        name="cache_kv",
    )(cache_ckv, cache_kpe_slot, wuk_arr, wuv_arr)


def _attend_rows(q_ref, k_ref, v_ref, o_ref):
    lq = q_ref.shape[0]
    lane = lax.broadcasted_iota(jnp.int32, (lq, 2 * VD), 1)

    def scores_of(hd):
        return lax.dot_general(q_ref[:, hd * SLOT:(hd + 1) * SLOT], k_ref[:, hd * SLOT:(hd + 1) * SLOT], _NT,
                               preferred_element_type=F32)

    outs = []
    nxt = scores_of(0)
    for hd in range(HEADS):
        s = nxt
        if hd + 1 < HEADS:
            nxt = scores_of(hd + 1)
        p = jnp.exp2(s - jnp.max(s, -1, keepdims=True))
        acc = jnp.dot(p.astype(BF16), v_ref[:, (hd // 2) * 2 * VD:(hd // 2 + 1) * 2 * VD],
                      preferred_element_type=F32)
        outs.append(acc / jnp.sum(p, -1, keepdims=True))
        if hd % 2 == 1:
            pair = hd // 2
            o_ref[:, pair * 2 * VD:(pair + 1) * 2 * VD] = jnp.where(lane < VD, outs[-2], outs[-1]).astype(BF16)


def _attend_cols(q_ref, segments, o_ref):
    lq = q_ref.shape[0]
    lane = lax.broadcasted_iota(jnp.int32, (lq, 2 * SLOT), 1)
    row = lax.broadcasted_iota(jnp.int32, (2 * VD, lq), 0)
    n_pairs = HEADS // 2

    def scores_of(pair):
        qp = q_ref[:, pair * 2 * SLOT:(pair + 1) * 2 * SLOT]
        zero = jnp.zeros_like(qp)
        q_bd = jnp.concatenate([jnp.where(lane < SLOT, qp, zero), jnp.where(lane < SLOT, zero, qp)], axis=0)
        return [lax.dot_general(k_ref[:, pair * 2 * SLOT:(pair + 1) * 2 * SLOT], q_bd, _NT,
                                preferred_element_type=F32) for k_ref, _ in segments]

    def head_pair_out(pair, scores):
        mx = functools.reduce(jnp.maximum, [jnp.max(s, 0, keepdims=True) for s in scores])
        ps = [jnp.exp2(s - mx) for s in scores]
        den = functools.reduce(jnp.add, [jnp.sum(p, 0, keepdims=True) for p in ps])
        acc = functools.reduce(jnp.add, [
            jnp.dot(vt_ref[pair * 2 * VD:(pair + 1) * 2 * VD, :], p.astype(BF16),
                    preferred_element_type=F32) for p, (_, vt_ref) in zip(ps, segments)])
        acc = acc / den
        return jnp.where(row < VD, acc[:, :lq], acc[:, lq:])

    outs = []
    nxt = scores_of(0)
    for pair in range(n_pairs):
        cur = nxt
        if pair + 1 < n_pairs:
            nxt = scores_of(pair + 1)
        outs.append(head_pair_out(pair, cur))
    o_ref[...] = jnp.transpose(jnp.concatenate(outs, axis=0)).astype(BF16)


def _attn_ctx_kernel(q_ref, k_ref, v_ref, o_ref):
    _attend_rows(q_ref, k_ref, v_ref, o_ref)


def _attn_lat_kernel(q_ref, kc_ref, vc_ref, k_ref, vt_ref, o_ref):
    _attend_cols(q_ref, [(kc_ref.at[0], vc_ref.at[0]), (k_ref, vt_ref)], o_ref)


def _attention(l, q, k, v, vt, kc, vc):
    kw, vw = HEADS * SLOT, HEADS * VD
    ctx = pl.pallas_call(
        _attn_ctx_kernel,
        out_shape=jax.ShapeDtypeStruct((T_CTX, vw), BF16),
        grid=(N_CTX_B,),
        in_specs=[pl.BlockSpec((CTX_L, kw), lambda b: (b, 0)),
                  pl.BlockSpec((CTX_L, kw), lambda b: (b, 0)),
                  pl.BlockSpec((CTX_L, vw), lambda b: (b, 0))],
        out_specs=pl.BlockSpec((CTX_L, vw), lambda b: (b, 0)),
        compiler_params=_cparams("parallel"),
        name="attn_ctx",
    )(q, k, v)
    nq = LAT_L // TM
    off = T_CTX // LAT_L
    lat = pl.pallas_call(
        _attn_lat_kernel,
        out_shape=jax.ShapeDtypeStruct((T_LAT, vw), BF16),
        grid=(N_LAT_B, nq),
        in_specs=[pl.BlockSpec((TM, kw), lambda b, j: (CTX_TILES + b * nq + j, 0)),
                  pl.BlockSpec((None, 1, PAST, kw), lambda b, j: (l, b, 0, 0)),
                  pl.BlockSpec((None, 1, vw, PAST), lambda b, j: (l, b, 0, 0)),
                  pl.BlockSpec((LAT_L, kw), lambda b, j: (off + b, 0)),
                  pl.BlockSpec((vw, LAT_L), lambda b, j: (0, off + b))],
        out_specs=pl.BlockSpec((TM, vw), lambda b, j: (b * nq + j, 0)),
        compiler_params=_cparams("parallel", "parallel"),
        name="attn_lat",
    )(q, kc, vc, k, vt)
    return ctx, lat


def _hyena_kernel(hy_ref, cw_ref, cb_ref, c_ref, sf_ref, sb_ref, kr_ref, ki_ref, krn_ref, hb_ref,
                  o_ref, *, L, group, layer):
    row = lax.broadcasted_iota(jnp.int32, (L, 3 * HY_W), 0)
    zs = []
    for j in range(group):
        x = hy_ref[j * L:(j + 1) * L, :]
        prev = jnp.where(row == 0, 0.0, pltpu.roll(x, 1, 0))
        nxt = jnp.where(row == L - 1, 0.0, pltpu.roll(x, L - 1, 0))
        zs.append(prev * cw_ref[0:1] + x * cw_ref[1:2] + nxt * cw_ref[2:3] + cb_ref[layer:layer + 1, :])
    side = lambda parts: jnp.concatenate(parts, axis=1)
    y = side([z[:, 2 * HY_W:] for z in zs])
    for o in range(2):
        gate = side([z[:, o * HY_W:(o + 1) * HY_W] for z in zs])
        kr, ki, krn = (side([r[o]] * group) for r in (kr_ref, ki_ref, krn_ref))
        yb = y.astype(BF16)
        a_re = jnp.dot(c_ref[...], yb, preferred_element_type=F32)
        a_im = jnp.dot(sf_ref[...], yb, preferred_element_type=F32)
        z_re = a_re * kr + a_im * ki
        z_im = a_im * krn - a_re * ki
        conv = (jnp.dot(c_ref[...], z_re.astype(BF16), preferred_element_type=F32)
                + jnp.dot(sb_ref[...], z_im.astype(BF16), preferred_element_type=F32))
        y = gate * (conv + y * side([hb_ref[o:o + 1]] * group))
    for j in range(group):
        o_ref[j * L:(j + 1) * L, :] = y[:, j * HY_W:(j + 1) * HY_W].astype(BF16)


def _hyena_group(l, hy, L, nb, group, blk0, dft, spectra, conv_w, conv_b, hy_bias):
    c, sf, sb, _ = dft
    kr, ki, krn = spectra
    const = lambda shape: pl.BlockSpec(shape, lambda b: (0,) * len(shape), pipeline_mode=pl.Buffered(1))
    spec = pl.BlockSpec((None, 2, L, HY_W), lambda b: (l, 0, 0, 0), pipeline_mode=pl.Buffered(1))
    return pl.pallas_call(
        functools.partial(_hyena_kernel, L=L, group=group, layer=l),
        out_shape=jax.ShapeDtypeStruct((nb * L, HY_W), BF16),
        grid=(nb // group,),
        in_specs=[pl.BlockSpec((group * L, 3 * HY_W), lambda b: (blk0 // group + b, 0)),
                  _layer_spec(l, 3, 3 * HY_W), _full((DEPTH, 3 * HY_W)),
                  const((L, L)), const((L, L)), const((L, L)), spec, spec, spec,
                  _layer_spec(l, 2, HY_W)],
        out_specs=pl.BlockSpec((group * L, HY_W), lambda b: (b, 0)),
        compiler_params=_cparams("parallel"),
        name=f"hyena_{L}",
    )(hy, conv_w, conv_b, c, sf, sb, kr, ki, krn, hy_bias)


def _out_kernel(a_ref, bc_ref, bl_ref, mc_ref, ml_ref, w_ref, xc_ref, xl_ref, ng_ref, nb_ref, mod_ref,
                g_ref, be_ref, wr_ref, br_ref, tri_ref, upper_ref, x1_ref, xloc_ref, route_ref, cnt_ref,
                w_b, *, layer):
    l = layer

    @pl.when(pl.program_id(0) == 0)
    def _():
        w_b[...] = w_ref[...].astype(BF16)

    dot = functools.partial(jnp.dot, preferred_element_type=F32)
    mixed = (dot(a_ref[...], w_b[0:GM_W]) + dot(_pair_read(bc_ref, bl_ref), w_b[GM_W:GM_W + HY_W])
             + dot(_pair_read(mc_ref, ml_ref), w_b[GM_W + HY_W:]))
    x = _pair_read(xc_ref, xl_ref)
    if l == 0:
        x = _layer_norm(x, ng_ref[...], nb_ref[...])
    m = mod_ref[0]
    x1 = _layer_norm(ALPHA * x + m[2:3] * mixed, g_ref[l:l + 1, :], be_ref[l:l + 1, :])
    x1_ref[...] = x1
    h2 = x1 * (1.0 + m[4:5]) + m[3:4]

    h_hi = h2.astype(BF16)
    h_lo = (h2 - h_hi.astype(F32)).astype(BF16)
    w_hi = wr_ref[...].astype(BF16)
    w_lo = (wr_ref[...] - w_hi.astype(F32)).astype(BF16)
    part = dot(h_hi, jnp.concatenate([w_hi, w_lo], axis=1))
    logits = part[:, :128] + part[:, 128:] + dot(h_lo, w_hi) + br_ref[l:l + 1, :]
    lane = lax.broadcasted_iota(jnp.int32, logits.shape, 1)
    lanef = lane.astype(F32)
    big = jnp.float32(1e9)
    ninf = jnp.float32(-jnp.inf)
    is_g = lane < N_GROUPS
    gl = jnp.where(is_g, logits, ninf)
    gmax = jnp.max(gl, -1, keepdims=True)
    gidx = jnp.min(jnp.where(gl == gmax, lanef, big), -1, keepdims=True)
    gw = 1.0 / jnp.sum(jnp.where(is_g, jnp.exp(logits - gmax), 0.0), -1, keepdims=True)
    ex = lane - N_GROUPS
    in_group = (ex >= 0) & (ex < N_EXP) & ((ex // EPG).astype(F32) == gidx)
    el = jnp.where(in_group, logits, ninf)
    v1 = jnp.max(el, -1, keepdims=True)
    i1 = jnp.min(jnp.where(el == v1, lanef, big), -1, keepdims=True)
    el2 = jnp.where(lanef == i1, ninf, el)
    v2 = jnp.max(el2, -1, keepdims=True)
    i2 = jnp.min(jnp.where(el2 == v2, lanef, big), -1, keepdims=True)
    e21 = jnp.exp(v2 - v1)
    w1 = gw / (1.0 + e21)
    w2 = gw * e21 / (1.0 + e21)
    e1 = i1 - N_GROUPS
    e2 = i2 - N_GROUPS

    oh1 = jnp.where(lanef == e1, 1.0, 0.0)
    oh2 = jnp.where(lanef == e2, 1.0, 0.0)
    ex1 = dot(tri_ref[...], oh1.astype(BF16))
    ex2 = dot(tri_ref[...], oh2.astype(BF16))
    col1 = jnp.sum(oh1, axis=0, keepdims=True)
    col2 = jnp.sum(oh2, axis=0, keepdims=True)
    n = col1 + col2
    run = jnp.floor((n + (BLK - 1.0)) * (1.0 / BLK)) * BLK
    start = dot(jnp.broadcast_to(run, (8, 128)).astype(BF16), upper_ref[...])[0:1]
    loc1 = jnp.sum(oh1 * (start + ex1), -1, keepdims=True)
    loc2 = jnp.sum(oh2 * (start + col1 + ex2), -1, keepdims=True)
    cnt_ref[...] = jnp.broadcast_to(n, cnt_ref.shape)

    route = jnp.zeros_like(logits)
    for j, val in enumerate((e1, e2, w1, w2, loc1, loc2)):
        route = jnp.where(lane == j, val, route)
    route_ref[...] = route

    loc1_row = jnp.transpose(jnp.broadcast_to(loc1, (TM, 128)))[0:1].astype(jnp.int32)
    loc2_row = jnp.transpose(jnp.broadcast_to(loc2, (TM, 128)))[0:1].astype(jnp.int32)
    slot = lax.broadcasted_iota(jnp.int32, (LOC, TM), 0)
    perm = jnp.where((slot == loc1_row) | (slot == loc2_row), 1.0, 0.0).astype(BF16)
    xloc_ref[...] = dot(perm, h_hi).astype(BF16)


def _out_proj(l, a, hyb_pair, att_pair, w_out, x_pair, norm, mod_l, g, b, w_route, b_route, tri, upper):
    tile = lambda n: pl.BlockSpec((TM, n), lambda i: (i, 0))
    return pl.pallas_call(
        functools.partial(_out_kernel, layer=l),
        out_shape=(jax.ShapeDtypeStruct((T, D), F32), jax.ShapeDtypeStruct((N_TILES * LOC, D), BF16),
                   jax.ShapeDtypeStruct((T, 128), F32), jax.ShapeDtypeStruct((N_TILES * 8, 128), F32)),
        grid=(N_TILES,),
        in_specs=[tile(GM_W)] + _pair_specs(HY_W) + _pair_specs(HEADS * VD)
        + [pl.BlockSpec((None, D, D), lambda i: (l, 0, 0), pipeline_mode=pl.Buffered(1))]
        + _pair_specs(D) + [
                  _full((1, D)), _full((1, D)),
                  _mod_spec(l),
                  _full((DEPTH, D)), _full((DEPTH, D)), _layer_spec(l, D, 128), _full((DEPTH, 128)),
                  _full((TM, TM)), _full((128, 128))],
        out_specs=(tile(D), pl.BlockSpec((LOC, D), lambda i: (i, 0)), tile(128),
                   pl.BlockSpec((8, 128), lambda i: (i, 0))),
        scratch_shapes=[pltpu.VMEM((D, D), BF16)],
        compiler_params=_cparams("arbitrary"),
        name="out_proj_route",
    )(a, *hyb_pair, *att_pair, w_out, *x_pair, *norm, mod_l, g, b, w_route, b_route, tri, upper)


Y_RING = 4
X_AHEAD = 6
X_RING = 8


def _expert_kernel(off_ref, cnt_ref, src_ref, x_hbm, wg_ref, wu_ref, wd_ref, y_hbm,
                   xbuf, ybuf, zbuf, wgu_b, wd_b, sem, osem, zsem):
    e = pl.program_id(0)
    first_tile = off_ref[e]
    n_tiles = cnt_ref[e]
    total = off_ref[N_EXP - 1] + cnt_ref[N_EXP - 1]

    def block_copies(t, s):
        return [pltpu.make_async_copy(
            x_hbm.at[pl.ds(pl.multiple_of(src_ref[t * BLK_PER_TILE + b] * BLK, BLK), BLK), :],
            xbuf.at[s, pl.ds(b * BLK, BLK), :], sem.at[s]) for b in range(BLK_PER_TILE)]

    def out_copy(t, s):
        return pltpu.make_async_copy(ybuf.at[s], y_hbm.at[pl.ds(pl.multiple_of(t * TM, TM), TM), :],
                                     osem.at[s])

    def zero_copy(t):
        return pltpu.make_async_copy(zbuf, y_hbm.at[pl.ds(pl.multiple_of(t * TM, TM), TM), :], zsem)

    @pl.when(e == 0)
    def _():
        zbuf[...] = jnp.zeros((TM, D), BF16)

        def zero_start(t, carry):
            zero_copy(t).start()
            return carry

        lax.fori_loop(total, MOE_TILES, zero_start, 0)
        for t0 in range(X_AHEAD):
            @pl.when(t0 < total)
            def _():
                for cp in block_copies(t0, t0):
                    cp.start()

    @pl.when(n_tiles > 0)
    def _():
        wgu_b[:, 0:D_EXP] = wg_ref[0, 0].astype(BF16)
        wgu_b[:, D_EXP:2 * D_EXP] = wu_ref[0, 0].astype(BF16)
        wd_b[...] = wd_ref[0, 0].astype(BF16)

    def process(tiles):
        for t in tiles:
            @pl.when(t + X_AHEAD < total)
            def _(t=t):
                for cp in block_copies(t + X_AHEAD, (t + X_AHEAD) % X_RING):
                    cp.start()

        for t in tiles:
            for cp in block_copies(t, t % X_RING):
                cp.wait()

            @pl.when(t >= Y_RING)
            def _(t=t):
                out_copy(t - Y_RING, t % Y_RING).wait()

        gus = [jnp.dot(xbuf[t % X_RING], wgu_b[...], preferred_element_type=F32) for t in tiles]
        for t, gu in zip(tiles, gus):
            gate, up = gu[:, :D_EXP], gu[:, D_EXP:]
            hid = gate / (1.0 + jnp.exp(-gate)) * up
            ybuf[t % Y_RING] = jnp.dot(hid.astype(BF16), wd_b[...], preferred_element_type=F32).astype(BF16)
            out_copy(t, t % Y_RING).start()

    def pair_body(p, carry):
        t = first_tile + 2 * p
        process([t, t + 1])
        return carry

    lax.fori_loop(0, n_tiles // 2, pair_body, 0)

    @pl.when(n_tiles % 2 == 1)
    def _():
        process([first_tile + n_tiles - 1])

    @pl.when(e == N_EXP - 1)
    def _():
        for back in range(1, Y_RING + 1):
            @pl.when(total >= back)
            def _(back=back):
                out_copy(total - back, (total - back) % Y_RING).wait()


        def zero_wait(t, carry):
            zero_copy(t).wait()
            return carry

        lax.fori_loop(total, MOE_TILES, zero_wait, 0)


def _experts(l, tables, xloc, w_gate, w_up, w_down):
    grid_spec = pltpu.PrefetchScalarGridSpec(
        num_scalar_prefetch=len(tables),
        grid=(N_EXP,),
        in_specs=[pl.BlockSpec(memory_space=pl.ANY),
                  pl.BlockSpec((1, 1, D, D_EXP), lambda e, *_: (l, e, 0, 0)),
                  pl.BlockSpec((1, 1, D, D_EXP), lambda e, *_: (l, e, 0, 0)),
                  pl.BlockSpec((1, 1, D_EXP, D), lambda e, *_: (l, e, 0, 0))],
        out_specs=pl.BlockSpec(memory_space=pl.ANY),
        scratch_shapes=[pltpu.VMEM((X_RING, TM, D), BF16), pltpu.VMEM((Y_RING, TM, D), BF16),
                        pltpu.VMEM((TM, D), BF16),
                        pltpu.VMEM((D, 2 * D_EXP), BF16), pltpu.VMEM((D_EXP, D), BF16),
                        pltpu.SemaphoreType.DMA((X_RING,)), pltpu.SemaphoreType.DMA((Y_RING,)),
                        pltpu.SemaphoreType.DMA(())],
    )
    return pl.pallas_call(
        _expert_kernel,
        out_shape=jax.ShapeDtypeStruct((MOE_ROWS, D), BF16),
        grid_spec=grid_spec,
        compiler_params=_cparams("arbitrary"),
        name="experts",
    )(*tables, xloc, w_gate, w_up, w_down)


def _combine_kernel(nblk_ref, gsrc_ref, y_hbm, route_ref, x1_ref, mod_ref, g_ref, b_ref, oc_ref, ol_ref,
                    ybuf, sem, *, layer):
    i = pl.program_id(0)

    slot_i = i % 2

    def block_copy(t, s, lb):
        src = pl.multiple_of(gsrc_ref[t * LOC_BLKS + lb] * BLK, BLK)
        return pltpu.make_async_copy(y_hbm.at[pl.ds(src, BLK), :],
                                     ybuf.at[s, pl.ds(pl.multiple_of(lb * BLK, BLK), BLK), :], sem.at[s])

    def fetch(t, s):
        def start(lb, c):
            block_copy(t, s, lb).start()
            return c
        lax.fori_loop(0, nblk_ref[t], start, 0)

    @pl.when(i == 0)
    def _():
        ybuf[...] = jnp.zeros_like(ybuf)
        fetch(0, 0)

    @pl.when(i + 1 < N_TILES)
    def _():
        fetch(jnp.minimum(i + 1, N_TILES - 1), 1 - slot_i)

    def wait(lb, c):
        block_copy(i, slot_i, lb).wait()
        return c

    lax.fori_loop(0, nblk_ref[i], wait, 0)
    route = route_ref[...]
    yb = ybuf[slot_i]
    slot = lax.broadcasted_iota(jnp.int32, (TM, LOC), 1)
    pick = (jnp.where(slot == route[:, 4:5].astype(jnp.int32), route[:, 2:3], 0.0)
            + jnp.where(slot == route[:, 5:6].astype(jnp.int32), route[:, 3:4], 0.0)).astype(BF16)
    moe = jnp.dot(pick, yb, preferred_element_type=F32)
    m = mod_ref[0]
    res = _layer_norm(ALPHA * x1_ref[...] + m[5:6] * moe, g_ref[layer:layer + 1, :], b_ref[layer:layer + 1, :])

    @pl.when(i < CTX_TILES)
    def _():
        oc_ref[...] = res

    @pl.when(i >= CTX_TILES)
    def _():
        ol_ref[...] = res


def _combine(l, nblk, gsrc, y, route, x1, mod_l, g, b):
    out_shape = (jax.ShapeDtypeStruct((T_CTX, D), F32), jax.ShapeDtypeStruct((T_LAT, D), F32))
    out_specs = tuple(_pair_specs(D))
    grid_spec = pltpu.PrefetchScalarGridSpec(
        num_scalar_prefetch=2,
        grid=(N_TILES,),
        in_specs=[pl.BlockSpec(memory_space=pl.ANY),
                  pl.BlockSpec((TM, 128), lambda i, nb, gs: (i, 0)),
                  pl.BlockSpec((TM, D), lambda i, nb, gs: (i, 0)),
                  _mod_spec(l), _full((DEPTH, D)), _full((DEPTH, D))],
        out_specs=out_specs,
        scratch_shapes=[pltpu.VMEM((2, LOC, D), BF16), pltpu.SemaphoreType.DMA((2,))],
    )
    return pl.pallas_call(
        functools.partial(_combine_kernel, layer=l),
        out_shape=out_shape,
        grid_spec=grid_spec,
        compiler_params=_cparams("arbitrary"),
        name="moe_combine",
    )(nblk, gsrc, y, route, x1, mod_l, g, b)


def _prefix_pick(starts, query, table):
    delta = table - jnp.concatenate([jnp.zeros_like(table[..., :1]), table[..., :-1]], axis=-1)
    return jnp.sum(jnp.where(starts <= query, delta, 0), axis=-1)


def _dispatch_tables(cnt):
    i32 = jnp.int32
    run = (cnt + BLK - 1) // BLK * BLK
    loc_start = jnp.cumsum(run, axis=1) - run
    nblk_loc = jnp.sum(run, axis=1) // BLK
    seg_rows = jnp.sum(run, axis=0)
    seg_tiles = (seg_rows + TM - 1) // TM
    tile_end = jnp.cumsum(seg_tiles)
    seg_start = (tile_end - seg_tiles) * TM
    glob_start = seg_start[None, :] + jnp.cumsum(run, axis=0) - run

    g_blk = (glob_start.T.reshape(1, -1)) // BLK
    n_blk = (run.T.reshape(1, -1)) // BLK
    l_blk = ((jnp.arange(N_TILES, dtype=i32) * LOC)[None, :] + loc_start.T).reshape(1, -1) // BLK
    gb = jnp.arange(MOE_ROWS // BLK, dtype=i32)[:, None]
    off = gb[:, 0] - _prefix_pick(g_blk, gb, g_blk)
    src_blk = jnp.where(off < _prefix_pick(g_blk, gb, n_blk), _prefix_pick(g_blk, gb, l_blk) + off,
                        ZERO_BLK).astype(i32)

    pos = (jnp.arange(LOC_BLKS, dtype=i32) * BLK)[None, :, None]
    shift = _prefix_pick(loc_start[:, None, :], pos, (glob_start - loc_start)[:, None, :])
    gsrc = ((pos[:, :, 0] + shift) // BLK).astype(i32)
    expert_tables = ((tile_end - seg_tiles).astype(i32), seg_tiles.astype(i32), src_blk)
    return expert_tables, nblk_loc.astype(i32), gsrc.reshape(-1)


def _rope_slot(pe):
    return jnp.pad(pe, [(0, 0)] * (pe.ndim - 1) + [(NOPE, SLOT - NOPE - ROPE)])


def _wuq_layout(w):
    w = w.reshape(DEPTH, Q_LORA, HEADS, NOPE + ROPE)
    w = jnp.pad(w, ((0, 0), (0, 0), (0, 0), (0, SLOT - NOPE - ROPE)))
    return w.reshape(DEPTH, Q_LORA, HEADS * SLOT).astype(BF16)


def _wukv_layout(w):
    w = w.reshape(DEPTH, KV_LORA, HEADS, NOPE + VD)
    wk = jnp.pad(w[..., :NOPE], ((0, 0), (0, 0), (0, 0), (0, SLOT - NOPE)))
    wv = w[..., NOPE:].reshape(DEPTH, KV_LORA, HEADS * VD).astype(BF16)
    return wk.reshape(DEPTH, KV_LORA, HEADS * SLOT).astype(BF16), wv, jnp.swapaxes(wv, 1, 2)


def kernel(x_prompt, x_sample, c, cache_ckv, cache_kpe, c_ctx, ln_in_g, ln_in_b, w_mod, b_mod, w_in,
           gm_ln_g, gm_ln_b, gm_ws, gm_bs, hy_conv_w, hy_conv_b, hy_f_w1, hy_f_b1, hy_f_w2, hy_f_b2,
           hy_f_w3, hy_f_freq, hy_bias, mla_gq, mla_gkv, mla_wuq, mla_wukv, w_out, ln1_g, ln1_b,
           ln2_g, ln2_b, moe_w_gr, moe_b_gr, moe_w_er, moe_b_er, moe_w_gate, moe_w_up, moe_w_down):
    rope = _rope_tables()
    dft = {L: _dft_tables(L) for L in (CTX_L, LAT_L)}
    pos_tab = {L: _hyena_positions(L) for L in (CTX_L, LAT_L)}
    hd = np.arange(GM_W) // GM_HD
    avg = jnp.asarray((hd[:, None] == hd[None, :]) / GM_HD, BF16)
    tri = jnp.asarray(np.tril(np.ones((TM, TM)), -1), BF16)
    upper = jnp.asarray(np.triu(np.ones((128, 128)), 1), BF16)

    cond8 = jnp.concatenate([c_ctx[None], c, jnp.zeros((8 - 1 - N_LAT_B, D), F32)], axis=0)
    mod = _modulation(cond8, w_mod, b_mod).reshape(DEPTH, 8, 6, D)

    wuk_all, wuv_all, wuvt_all = _wukv_layout(mla_wukv)
    w_in_t = jnp.swapaxes(w_in, 1, 2)
    wuq_arr = _wuq_layout(mla_wuq)
    ws_b = gm_ws.astype(BF16)
    kc_all, vc_all = _cache_kv(cache_ckv, _rope_slot(cache_kpe), wuk_all, wuvt_all)

    x_pair = (x_prompt.reshape(T_CTX, D), x_sample.reshape(T_LAT, D))
    norm = (ln_in_g.reshape(1, D), ln_in_b.reshape(1, D))
    bs_full = jnp.repeat(jnp.swapaxes(gm_bs, 1, 2), GM_HD, axis=2)
    w_route = jnp.pad(jnp.concatenate([moe_w_gr, moe_w_er.reshape(DEPTH, D, N_EXP)], axis=2),
                      ((0, 0), (0, 0), (0, 128 - N_GROUPS - N_EXP)))
    b_route = jnp.pad(jnp.concatenate([moe_b_gr, moe_b_er.reshape(DEPTH, N_EXP)], axis=1),
                      ((0, 0), (0, 128 - N_GROUPS - N_EXP)))
    hyena_groups = ((CTX_L, N_CTX_B, 4, 0), (LAT_L, N_LAT_B, 2, T_CTX // LAT_L))
    spectra = {L: _filter_spectra(L, *pos_tab[L], dft[L], hy_f_w1, hy_f_b1, hy_f_w2, hy_f_b2, hy_f_w3,
                                  hy_f_freq) for L, _, _, _ in hyena_groups}
    ckv_states, kpe_states = [], []
    for l in range(DEPTH):
        a, hy, q, k, v, vt, ckv, kpe = _in_proj(
            l, x_pair, norm, mod, w_in_t, gm_ln_g, gm_ln_b, avg,
            ws_b, bs_full, mla_gq, mla_gkv, wuq_arr, wuk_all, wuv_all, wuvt_all, rope)
        ckv_states.append(ckv[:T_CTX].reshape(N_CTX_B, CTX_L, KV_LORA))
        kpe_states.append(kpe[:T_CTX, NOPE:NOPE + ROPE].reshape(N_CTX_B, CTX_L, ROPE))

        hyb = [_hyena_group(l, hy, L, nb, group, blk0, dft[L], spectra[L], hy_conv_w, hy_conv_b, hy_bias)
               for L, nb, group, blk0 in hyena_groups]

        att = _attention(l, q, k, v, vt, kc_all, vc_all)

        x1, xloc, route, counts = _out_proj(l, a, hyb, att, w_out, x_pair, norm, mod,
                                            ln1_g, ln1_b, w_route, b_route, tri, upper)
        cnt = counts.reshape(N_TILES, 8, 128)[:, 0, :N_EXP].astype(jnp.int32)
        expert_tables, nblk_loc, gsrc = _dispatch_tables(cnt)
        y = _experts(l, expert_tables, xloc, moe_w_gate, moe_w_up, moe_w_down)
        x_pair = _combine(l, nblk_loc, gsrc, y, route, x1, mod, ln2_g, ln2_b)

    y_prompt = x_pair[0].reshape(N_CTX_B, CTX_L, D)
    y_sample = x_pair[1].reshape(N_LAT_B, LAT_L, D)
    return (y_prompt, y_sample, jnp.stack(ckv_states, axis=1), jnp.stack(kpe_states, axis=1))
```

```python
import functools
import math

import numpy as np
import jax
import jax.numpy as jnp
from jax import lax
from jax.experimental import pallas as pl
from jax.experimental.pallas import tpu as pltpu

D = 1024
N_CTX_B, CTX_L = 16, 256
N_LAT_B, LAT_L = 4, 1024
DEPTH = 2
T_CTX = N_CTX_B * CTX_L
T_LAT = N_LAT_B * LAT_L
T = T_CTX + T_LAT
PAST = 256
GRID_W = 64

GM_HEADS, GM_HD, GM_W, CHUNK = 4, 64, 256, 128
HY_W, HY_EMB, HY_FH = 256, 33, 64
NOPE, ROPE, VD, HEADS = 64, 32, 64, 8
Q_LORA, KV_LORA = 256, 128
SLOT = 128
N_GROUPS, EPG, N_EXP, D_EXP = 4, 8, 32, 256
ALPHA = (2.0 * DEPTH) ** 0.25
LN_EPS, RMS_EPS = 1e-5, 1e-6
ROPE_THETA = 10000.0

TM = 256
N_TILES = T // TM
CTX_TILES = T_CTX // TM
LAT_TILES_PER_B = LAT_L // TM
N_ASSIGN = 2 * T
BLK = 16
BLK_PER_TILE = TM // BLK
LOC = 2 * TM + N_EXP * BLK
LOC_BLKS = LOC // BLK
ZERO_BLK = LOC_BLKS - 1
MOE_TILES = (N_ASSIGN + N_TILES * N_EXP * (BLK - 1)) // TM + N_EXP
MOE_ROWS = MOE_TILES * TM
IN_COLS = 1696
IN_EXT = 1792

F32, BF16 = jnp.float32, jnp.bfloat16
_NT = (((1,), (1,)), ((), ()))
VMEM_LIMIT = 52 * 1024 * 1024


def _cparams(*sem):
    return pltpu.CompilerParams(dimension_semantics=sem, vmem_limit_bytes=VMEM_LIMIT)


def _cond_of_tile(i):
    return jnp.where(i < CTX_TILES, 0, 1 + (i - CTX_TILES) // LAT_TILES_PER_B)


def _rope_block_of_tile(i):
    return jnp.where(i < CTX_TILES, 0, 1 + (i - CTX_TILES) % LAT_TILES_PER_B)


def _full(shape):
    n = len(shape)
    return pl.BlockSpec(shape, lambda *_: (0,) * n)


def _layer_spec(l, *shape):
    return pl.BlockSpec((None,) + shape, lambda *_: (l,) + (0,) * len(shape))


def _mod_spec(l):
    return pl.BlockSpec((None, 1, 6, D), lambda i, *_: (l, _cond_of_tile(i), 0, 0))


def _pair_specs(n):
    return [pl.BlockSpec((TM, n), lambda i, *_: (jnp.minimum(i, CTX_TILES - 1), 0)),
            pl.BlockSpec((TM, n), lambda i, *_: (jnp.maximum(i - CTX_TILES, 0), 0))]


def _pair_read(c_ref, l_ref):
    return jnp.where(pl.program_id(0) < CTX_TILES, c_ref[...], l_ref[...])


def _layer_norm(x, g, b):
    mu = jnp.mean(x, -1, keepdims=True)
    xc = x - mu
    var = jnp.mean(xc * xc, -1, keepdims=True)
    return xc * lax.rsqrt(var + LN_EPS) * g + b


def _rms_norm(x, g):
    return x * lax.rsqrt(jnp.mean(x * x, -1, keepdims=True) + RMS_EPS) * g


def _bdot(a, b):
    return jnp.dot(a.astype(BF16), b.astype(BF16), preferred_element_type=F32)


def _dot3(x, w):
    x_hi = x.astype(BF16)
    x_lo = (x - x_hi.astype(F32)).astype(BF16)
    w_hi = w.astype(BF16)
    w_lo = (w - w_hi.astype(F32)).astype(BF16)
    dot = functools.partial(jnp.dot, preferred_element_type=F32)
    return dot(x_hi, w_hi) + dot(x_hi, w_lo) + dot(x_lo, w_hi)


def _swap_pair_lanes(x):
    n = x.shape[1]
    lane = lax.broadcasted_iota(jnp.int32, x.shape, 1)
    return jnp.where(lane % 2 == 0, pltpu.roll(x, n - 1, 1), pltpu.roll(x, 1, 1))


def _split2_dot(x, w_bf16):
    hi = x.astype(BF16)
    lo = (x - hi.astype(F32)).astype(BF16)
    dot = functools.partial(jnp.dot, preferred_element_type=F32)
    return dot(hi, w_bf16) + dot(lo, w_bf16)


def _dft_tables(L):
    f = np.arange(L, dtype=np.int64)
    ft = np.outer(f, f) % (2 * L)
    ang = np.pi * ft / L
    c = np.cos(ang)
    s = np.sin(ang)
    alt = np.where(f % 2 == 0, 1.0, -1.0)
    sf = s.copy()
    sf[0, :] = alt
    return (jnp.asarray(c, F32).astype(BF16), jnp.asarray(sf, F32).astype(BF16),
            jnp.asarray(sf.T.copy(), F32).astype(BF16), jnp.asarray(alt[:, None], F32))


def _rope_tables():
    rows = LAT_L // GRID_W
    row = np.repeat(np.arange(rows, dtype=np.float64), GRID_W)
    col = np.tile(np.arange(GRID_W, dtype=np.float64), rows)
    n = ROPE // 4
    inv = ROPE_THETA ** (-np.arange(n, dtype=np.float64) / n)
    ang = np.concatenate([row[:, None] * inv, col[:, None] * inv], -1)
    cos = np.zeros((CTX_L + LAT_L, SLOT))
    sin = np.zeros((CTX_L + LAT_L, SLOT))
    cos[:, :NOPE + ROPE] = 1.0
    cos[CTX_L:, NOPE:NOPE + ROPE:2] = np.cos(ang)
    cos[CTX_L:, NOPE + 1:NOPE + ROPE:2] = np.cos(ang)
    sin[CTX_L:, NOPE:NOPE + ROPE:2] = -np.sin(ang)
    sin[CTX_L:, NOPE + 1:NOPE + ROPE:2] = np.sin(ang)
    scale = math.log2(math.e) / math.sqrt(NOPE + ROPE)
    return (jnp.asarray(cos * scale, F32), jnp.asarray(sin * scale, F32),
            jnp.asarray(cos, F32), jnp.asarray(sin, F32))


def _hyena_positions(L):
    t = jnp.linspace(0.0, 1.0, L, dtype=F32)[:, None]
    bands = (HY_EMB - 1) // 2
    w = 2.0 * math.pi * jnp.arange(L, dtype=F32)[:, None] / L
    f = jnp.linspace(1e-4, bands - 1, bands, dtype=F32)[None]
    z = jnp.concatenate([t, jnp.cos(f * w), -jnp.sin(f * w)], -1)
    z = jnp.pad(z, ((0, 0), (0, 128 - HY_EMB)))
    min_decay = math.log(1e-2) / 1.5
    max_decay = math.log(1e-2) / 0.3
    deltas = jnp.abs(jnp.linspace(min_decay, max_decay, HY_W, dtype=F32))
    return z, jnp.exp(-t * deltas)


def _mod_kernel(c_ref, w_ref, b_ref, o_ref):
    c = c_ref[...]
    s = c / (1.0 + jnp.exp(-c))
    o_ref[0] = _bdot(s, w_ref[0]) + b_ref[0]


def _modulation(cond8, w_mod, b_mod):
    tn = 1536
    return pl.pallas_call(
        _mod_kernel,
        out_shape=jax.ShapeDtypeStruct((DEPTH, 8, 6 * D), F32),
        grid=(DEPTH, 6 * D // tn),
        in_specs=[pl.BlockSpec((8, D), lambda l, j: (0, 0)),
                  pl.BlockSpec((1, D, tn), lambda l, j: (l, 0, j)),
                  pl.BlockSpec((1, 1, tn), lambda l, j: (l, 0, j))],
        out_specs=pl.BlockSpec((1, 8, tn), lambda l, j: (l, 0, j)),
        compiler_params=_cparams("parallel", "parallel"),
        name="modulation",
    )(cond8, w_mod, b_mod.reshape(DEPTH, 1, 6 * D))


def _filter_kernel(z_ref, w1_ref, b1_ref, w2_ref, b2_ref, w3_ref, fr_ref, dec_ref, c_ref, sf_ref,
                   alt_ref, kr_ref, ki_ref, krn_ref, *, L):
    h1 = jnp.sin(fr_ref[0:1] * (_dot3(z_ref[...], w1_ref[...]) + b1_ref[...]))
    h2 = jnp.sin(fr_ref[1:2] * (_dot3(h1, w2_ref[...]) + b2_ref[...]))
    filt = _dot3(h2, w3_ref[...])
    row = lax.broadcasted_iota(jnp.int32, (L, HY_W), 0)
    dec = dec_ref[...]
    n = DEPTH * 2
    fwd = [filt[:, j * 2 * HY_W:j * 2 * HY_W + HY_W] * dec for j in range(n)]
    bwd = [jnp.where(row == 0, 0.0, filt[:, j * 2 * HY_W + HY_W:(j + 1) * 2 * HY_W] * dec) for j in range(n)]
    sm = [f + b for f, b in zip(fwd, bwd)]
    kr_all = _bdot(c_ref[...], jnp.concatenate(sm, axis=1))
    ki_all = -_bdot(sf_ref[...], jnp.concatenate([f - b for f, b in zip(fwd, bwd)], axis=1))
    for j in range(n):
        l, o = divmod(j, 2)
        kr = kr_all[:, j * HY_W:(j + 1) * HY_W]
        kr_nyq = jnp.sum(alt_ref[...] * sm[j], axis=0, keepdims=True)
        kr_ref[l, o] = jnp.where(row == 0, kr * (0.5 / L), kr * (1.0 / L))
        ki_ref[l, o] = jnp.where(row == 0, 0.0, ki_all[:, j * HY_W:(j + 1) * HY_W] * (1.0 / L))
        krn_ref[l, o] = jnp.where(row == 0, kr_nyq * (0.5 / L), kr * (1.0 / L))


def _block_diag(blocks):
    rows = []
    for i, blk in enumerate(blocks):
        rows.append(jnp.concatenate([blk if j == i else jnp.zeros((blk.shape[0], other.shape[1]), blk.dtype)
                                     for j, other in enumerate(blocks)], axis=1))
    return jnp.concatenate(rows, axis=0)


def _filter_spectra(L, z, dec, dft, w1, b1, w2, b2, w3, freq):
    assert DEPTH * HY_FH == 128
    c, sf, _, alt = dft
    layers = range(DEPTH)
    w1c = jnp.pad(jnp.concatenate([w1[l] for l in layers], axis=1), ((0, 128 - HY_EMB), (0, 0)))
    w2c = _block_diag([w2[l] for l in layers])
    w3c = _block_diag([w3[l] for l in layers])
    b1c, b2c = b1.reshape(1, 128), b2.reshape(1, 128)
    frc = jnp.swapaxes(freq, 0, 1).reshape(2, 128)
    shp = jax.ShapeDtypeStruct((DEPTH, 2, L, HY_W), F32)
    args = (z, w1c, b1c, w2c, b2c, w3c, frc, dec, c, sf, alt)
    return pl.pallas_call(
        functools.partial(_filter_kernel, L=L),
        out_shape=(shp, shp, shp),
        grid=(1,),
        in_specs=[_full(a.shape) for a in args],
        out_specs=(_full((DEPTH, 2, L, HY_W)),) * 3,
        compiler_params=_cparams("arbitrary"),
        name=f"hyena_filter_{L}",
    )(*args)


def _in_kernel(xc_ref, xl_ref, ng_ref, nb_ref, mod_ref, wt_ref, gg_ref, gb_ref, avg_ref, ws_ref, bs_ref,
               gq_ref, gkv_ref, wuq_ref, wuk_ref, wuv_ref, wuvt_ref, cq_ref, sq_ref, ck_ref, sk_ref,
               a_ref, hy_ref, q_ref, k_ref, v_ref, vt_ref, ckv_ref, kpe_ref, w_b, *, layer):
    l = layer

    @pl.when(pl.program_id(0) == 0)
    def _():
        for c0 in range(0, 1664, 128):
            w_b[:, c0:c0 + 128] = jnp.transpose(wt_ref[c0:c0 + 128, :]).astype(BF16)
        slot_rows = jnp.concatenate([jnp.zeros((NOPE, D), F32), wt_ref[1664:IN_COLS, :],
                                     jnp.zeros((SLOT - NOPE - ROPE, D), F32)], axis=0)
        w_b[:, 1664:IN_EXT] = jnp.transpose(slot_rows).astype(BF16)

    x = _pair_read(xc_ref, xl_ref)
    if l == 0:
        x = _layer_norm(x, ng_ref[...], nb_ref[...])
    m = mod_ref[0]
    h = x * (1.0 + m[1:2]) + m[0:1]
    proj = _bdot(h, w_b[...])

    gu = jax.nn.gelu(proj[:, 0:GM_W], approximate=True)
    gv = jax.nn.gelu(proj[:, GM_W:2 * GM_W], approximate=True)
    avg = avg_ref[...]
    mu = _split2_dot(gv, avg)
    vc = gv - mu
    var = _split2_dot(vc * vc, avg)
    vln = (vc * lax.rsqrt(var + LN_EPS) * gg_ref[l:l + 1, :] + gb_ref[l:l + 1, :]).astype(BF16)
    lane = lax.broadcasted_iota(jnp.int32, (CHUNK, GM_W), 1)
    for c in range(TM // CHUNK):
        vchunk = vln[c * CHUNK:(c + 1) * CHUNK]
        s = bs_ref[...]
        for hd in range(GM_HEADS):
            sh = jnp.dot(ws_ref[0, hd], vchunk, preferred_element_type=F32)
            s = s + jnp.where(lane // GM_HD == hd, sh, 0.0)
        a_ref[c * CHUNK:(c + 1) * CHUNK, :] = (gu[c * CHUNK:(c + 1) * CHUNK] * s).astype(BF16)

    hy_ref[...] = proj[:, 512:1280]

    cq = _rms_norm(proj[:, 1280:1536], gq_ref[l:l + 1, :])
    qq = _bdot(cq, wuq_ref[0])
    cos_q = jnp.concatenate([cq_ref[...]] * HEADS, axis=1)
    sin_q = jnp.concatenate([sq_ref[...]] * HEADS, axis=1)
    q_ref[...] = (qq * cos_q + _swap_pair_lanes(qq) * sin_q).astype(BF16)

    ckv = _rms_norm(proj[:, 1536:1664], gkv_ref[l:l + 1, :])
    ckv_ref[...] = ckv
    kpe = proj[:, 1664:IN_EXT]
    kpe_ref[...] = kpe
    krot = kpe * ck_ref[...] + _swap_pair_lanes(kpe) * sk_ref[...]
    kn = _bdot(ckv, wuk_ref[0])
    k_ref[...] = (kn + jnp.concatenate([krot] * HEADS, axis=1)).astype(BF16)
    v_ref[...] = _bdot(ckv, wuv_ref[0]).astype(BF16)
    vt_ref[...] = lax.dot_general(wuvt_ref[0], ckv.astype(BF16), _NT,
                                  preferred_element_type=F32).astype(BF16)


def _in_proj(l, x_pair, norm, mod_l, w_in_t, gm_g, gm_b, avg, ws, bs_full, gq, gkv, wuq_arr,
             wuk_arr, wuv_arr, wuvt_arr, rope):
    layer = lambda *shape: pl.BlockSpec((1,) + shape, lambda i: (l,) + (0,) * len(shape))
    cos_q, sin_q, cos_k, sin_k = rope
    tile = lambda n: pl.BlockSpec((TM, n), lambda i: (i, 0))
    rope_spec = pl.BlockSpec((TM, SLOT), lambda i: (_rope_block_of_tile(i), 0))
    out_shapes = (jax.ShapeDtypeStruct((T, GM_W), BF16),
                  jax.ShapeDtypeStruct((T, 3 * HY_W), F32),
                  jax.ShapeDtypeStruct((T, HEADS * SLOT), BF16),
                  jax.ShapeDtypeStruct((T, HEADS * SLOT), BF16),
                  jax.ShapeDtypeStruct((T, HEADS * VD), BF16),
                  jax.ShapeDtypeStruct((HEADS * VD, T), BF16),
                  jax.ShapeDtypeStruct((T, KV_LORA), F32),
                  jax.ShapeDtypeStruct((T, SLOT), F32))
    return pl.pallas_call(
        functools.partial(_in_kernel, layer=l),
        out_shape=out_shapes,
        grid=(N_TILES,),
        in_specs=_pair_specs(D) + [
                  _full((1, D)), _full((1, D)),
                  _mod_spec(l),
                  pl.BlockSpec((None, IN_COLS, D), lambda i: (l, 0, 0), pipeline_mode=pl.Buffered(1)),
                  _full((DEPTH, GM_W)), _full((DEPTH, GM_W)), _full((GM_W, GM_W)),
                  layer(GM_HEADS, CHUNK, CHUNK), _layer_spec(l, CHUNK, GM_W),
                  _full((DEPTH, Q_LORA)), _full((DEPTH, KV_LORA)),
                  layer(Q_LORA, HEADS * SLOT), layer(KV_LORA, HEADS * SLOT),
                  layer(KV_LORA, HEADS * VD), layer(HEADS * VD, KV_LORA),
                  rope_spec, rope_spec, rope_spec, rope_spec],
        out_specs=(tile(GM_W), tile(3 * HY_W), tile(HEADS * SLOT), tile(HEADS * SLOT), tile(HEADS * VD),
                   pl.BlockSpec((HEADS * VD, TM), lambda i: (0, i)), tile(KV_LORA), tile(SLOT)),
        scratch_shapes=[pltpu.VMEM((D, IN_EXT), BF16)],
        compiler_params=_cparams("arbitrary"),
        name="in_proj",
    )(*x_pair, *norm, mod_l, w_in_t, gm_g, gm_b, avg, ws, bs_full, gq, gkv, wuq_arr, wuk_arr, wuv_arr,
      wuvt_arr, cos_q, sin_q, cos_k, sin_k)


def _cache_kernel(ckv_ref, kpe_ref, wuk_ref, wuv_ref, k_ref, v_ref):
    ckv = ckv_ref[0, 0]
    kn = _bdot(ckv, wuk_ref[0])
    k_ref[0, 0] = (kn + jnp.concatenate([kpe_ref[0, 0]] * HEADS, axis=1)).astype(BF16)
    v_ref[0, 0] = lax.dot_general(wuv_ref[0], ckv.astype(BF16), _NT,
                                  preferred_element_type=F32).astype(BF16)


def _cache_kv(cache_ckv, cache_kpe_slot, wuk_arr, wuv_arr):
    return pl.pallas_call(
        _cache_kernel,
        out_shape=(jax.ShapeDtypeStruct((DEPTH, N_LAT_B, PAST, HEADS * SLOT), BF16),
                   jax.ShapeDtypeStruct((DEPTH, N_LAT_B, HEADS * VD, PAST), BF16)),
        grid=(DEPTH, N_LAT_B),
        in_specs=[pl.BlockSpec((1, 1, PAST, KV_LORA), lambda l, b: (b, l, 0, 0)),
                  pl.BlockSpec((1, 1, PAST, SLOT), lambda l, b: (b, l, 0, 0)),
                  pl.BlockSpec((1, KV_LORA, HEADS * SLOT), lambda l, b: (l, 0, 0)),
                  pl.BlockSpec((1, HEADS * VD, KV_LORA), lambda l, b: (l, 0, 0))],
        out_specs=(pl.BlockSpec((1, 1, PAST, HEADS * SLOT), lambda l, b: (l, b, 0, 0)),
                   pl.BlockSpec((1, 1, HEADS * VD, PAST), lambda l, b: (l, b, 0, 0))),
        compiler_params=_cparams("parallel", "parallel"),
        name="cache_kv",
    )(cache_ckv, cache_kpe_slot, wuk_arr, wuv_arr)


def _attend_rows(q_ref, k_ref, v_ref, o_ref):
    lq = q_ref.shape[0]
    lane = lax.broadcasted_iota(jnp.int32, (lq, 2 * VD), 1)

    def scores_of(hd):
        return lax.dot_general(q_ref[:, hd * SLOT:(hd + 1) * SLOT], k_ref[:, hd * SLOT:(hd + 1) * SLOT], _NT,
                               preferred_element_type=F32)

    outs = []
    nxt = scores_of(0)
    for hd in range(HEADS):
        s = nxt
        if hd + 1 < HEADS:
            nxt = scores_of(hd + 1)
        p = jnp.exp2(s - jnp.max(s, -1, keepdims=True))
        acc = jnp.dot(p.astype(BF16), v_ref[:, (hd // 2) * 2 * VD:(hd // 2 + 1) * 2 * VD],
                      preferred_element_type=F32)
        outs.append(acc / jnp.sum(p, -1, keepdims=True))
        if hd % 2 == 1:
            pair = hd // 2
            o_ref[:, pair * 2 * VD:(pair + 1) * 2 * VD] = jnp.where(lane < VD, outs[-2], outs[-1]).astype(BF16)


def _attend_cols(q_ref, segments, o_ref):
    lq = q_ref.shape[0]
    lane = lax.broadcasted_iota(jnp.int32, (lq, 2 * SLOT), 1)
    row = lax.broadcasted_iota(jnp.int32, (2 * VD, lq), 0)
    n_pairs = HEADS // 2

    def scores_of(pair):
        qp = q_ref[:, pair * 2 * SLOT:(pair + 1) * 2 * SLOT]
        zero = jnp.zeros_like(qp)
        q_bd = jnp.concatenate([jnp.where(lane < SLOT, qp, zero), jnp.where(lane < SLOT, zero, qp)], axis=0)
        return [lax.dot_general(k_ref[:, pair * 2 * SLOT:(pair + 1) * 2 * SLOT], q_bd, _NT,
                                preferred_element_type=F32) for k_ref, _ in segments]

    def head_pair_out(pair, scores):
        mx = functools.reduce(jnp.maximum, [jnp.max(s, 0, keepdims=True) for s in scores])
        ps = [jnp.exp2(s - mx) for s in scores]
        den = functools.reduce(jnp.add, [jnp.sum(p, 0, keepdims=True) for p in ps])
        acc = functools.reduce(jnp.add, [
            jnp.dot(vt_ref[pair * 2 * VD:(pair + 1) * 2 * VD, :], p.astype(BF16),
                    preferred_element_type=F32) for p, (_, vt_ref) in zip(ps, segments)])
        acc = acc / den
        return jnp.where(row < VD, acc[:, :lq], acc[:, lq:])

    outs = []
    nxt = scores_of(0)
    for pair in range(n_pairs):
        cur = nxt
        if pair + 1 < n_pairs:
            nxt = scores_of(pair + 1)
        outs.append(head_pair_out(pair, cur))
    o_ref[...] = jnp.transpose(jnp.concatenate(outs, axis=0)).astype(BF16)


def _attn_ctx_kernel(q_ref, k_ref, v_ref, o_ref):
    _attend_rows(q_ref, k_ref, v_ref, o_ref)


def _attn_lat_kernel(q_ref, kc_ref, vc_ref, k_ref, vt_ref, o_ref):
    _attend_cols(q_ref, [(kc_ref.at[0], vc_ref.at[0]), (k_ref, vt_ref)], o_ref)


def _attention(l, q, k, v, vt, kc, vc):
    kw, vw = HEADS * SLOT, HEADS * VD
    ctx = pl.pallas_call(
        _attn_ctx_kernel,
        out_shape=jax.ShapeDtypeStruct((T_CTX, vw), BF16),
        grid=(N_CTX_B,),
        in_specs=[pl.BlockSpec((CTX_L, kw), lambda b: (b, 0)),
                  pl.BlockSpec((CTX_L, kw), lambda b: (b, 0)),
                  pl.BlockSpec((CTX_L, vw), lambda b: (b, 0))],
        out_specs=pl.BlockSpec((CTX_L, vw), lambda b: (b, 0)),
        compiler_params=_cparams("parallel"),
        name="attn_ctx",
    )(q, k, v)
    nq = LAT_L // TM
    off = T_CTX // LAT_L
    lat = pl.pallas_call(
        _attn_lat_kernel,
        out_shape=jax.ShapeDtypeStruct((T_LAT, vw), BF16),
        grid=(N_LAT_B, nq),
        in_specs=[pl.BlockSpec((TM, kw), lambda b, j: (CTX_TILES + b * nq + j, 0)),
                  pl.BlockSpec((None, 1, PAST, kw), lambda b, j: (l, b, 0, 0)),
                  pl.BlockSpec((None, 1, vw, PAST), lambda b, j: (l, b, 0, 0)),
                  pl.BlockSpec((LAT_L, kw), lambda b, j: (off + b, 0)),
                  pl.BlockSpec((vw, LAT_L), lambda b, j: (0, off + b))],
        out_specs=pl.BlockSpec((TM, vw), lambda b, j: (b * nq + j, 0)),
        compiler_params=_cparams("parallel", "parallel"),
        name="attn_lat",
    )(q, kc, vc, k, vt)
    return ctx, lat


def _hyena_kernel(hy_ref, cw_ref, cb_ref, c_ref, sf_ref, sb_ref, kr_ref, ki_ref, krn_ref, hb_ref,
                  o_ref, *, L, group, layer):
    row = lax.broadcasted_iota(jnp.int32, (L, 3 * HY_W), 0)
    zs = []
    for j in range(group):
        x = hy_ref[j * L:(j + 1) * L, :]
        prev = jnp.where(row == 0, 0.0, pltpu.roll(x, 1, 0))
        nxt = jnp.where(row == L - 1, 0.0, pltpu.roll(x, L - 1, 0))
        zs.append(prev * cw_ref[0:1] + x * cw_ref[1:2] + nxt * cw_ref[2:3] + cb_ref[layer:layer + 1, :])
    side = lambda parts: jnp.concatenate(parts, axis=1)
    y = side([z[:, 2 * HY_W:] for z in zs])
    for o in range(2):
        gate = side([z[:, o * HY_W:(o + 1) * HY_W] for z in zs])
        kr, ki, krn = (side([r[o]] * group) for r in (kr_ref, ki_ref, krn_ref))
        yb = y.astype(BF16)
        a_re = jnp.dot(c_ref[...], yb, preferred_element_type=F32)
        a_im = jnp.dot(sf_ref[...], yb, preferred_element_type=F32)
        z_re = a_re * kr + a_im * ki
        z_im = a_im * krn - a_re * ki
        conv = (jnp.dot(c_ref[...], z_re.astype(BF16), preferred_element_type=F32)
                + jnp.dot(sb_ref[...], z_im.astype(BF16), preferred_element_type=F32))
        y = gate * (conv + y * side([hb_ref[o:o + 1]] * group))
    for j in range(group):
        o_ref[j * L:(j + 1) * L, :] = y[:, j * HY_W:(j + 1) * HY_W].astype(BF16)


def _hyena_group(l, hy, L, nb, group, blk0, dft, spectra, conv_w, conv_b, hy_bias):
    c, sf, sb, _ = dft
    kr, ki, krn = spectra
    const = lambda shape: pl.BlockSpec(shape, lambda b: (0,) * len(shape), pipeline_mode=pl.Buffered(1))
    spec = pl.BlockSpec((None, 2, L, HY_W), lambda b: (l, 0, 0, 0), pipeline_mode=pl.Buffered(1))
    return pl.pallas_call(
        functools.partial(_hyena_kernel, L=L, group=group, layer=l),
        out_shape=jax.ShapeDtypeStruct((nb * L, HY_W), BF16),
        grid=(nb // group,),
        in_specs=[pl.BlockSpec((group * L, 3 * HY_W), lambda b: (blk0 // group + b, 0)),
                  _layer_spec(l, 3, 3 * HY_W), _full((DEPTH, 3 * HY_W)),
                  const((L, L)), const((L, L)), const((L, L)), spec, spec, spec,
                  _layer_spec(l, 2, HY_W)],
        out_specs=pl.BlockSpec((group * L, HY_W), lambda b: (b, 0)),
        compiler_params=_cparams("parallel"),
        name=f"hyena_{L}",
    )(hy, conv_w, conv_b, c, sf, sb, kr, ki, krn, hy_bias)


OUT_G = 2


def _out_kernel(a_ref, bc_ref, bl_ref, mc_ref, ml_ref, w_ref, xc_ref, xl_ref, ng_ref, nb_ref, mod_ref,
                g_ref, be_ref, wr_ref, br_ref, tri_ref, upper_ref, x1_ref, xloc_ref, route_ref, cnt_ref,
                w_b, *, layer):
    l = layer

    @pl.when(pl.program_id(0) == 0)
    def _():
        w_b[...] = w_ref[...].astype(BF16)

    dot = functools.partial(jnp.dot, preferred_element_type=F32)
    is_ctx = pl.program_id(0) < CTX_TILES // OUT_G
    w_hi = wr_ref[...].astype(BF16)
    w_lo = (wr_ref[...] - w_hi.astype(F32)).astype(BF16)
    w_cat = jnp.concatenate([w_hi, w_lo], axis=1)
    lane = lax.broadcasted_iota(jnp.int32, (TM, 128), 1)
    lanef = lane.astype(F32)
    big = jnp.float32(1e9)
    ninf = jnp.float32(-jnp.inf)
    is_g = lane < N_GROUPS
    ex = lane - N_GROUPS
    m = mod_ref[0]

    def front(k):
        rows = pl.ds(k * TM, TM)
        pair = lambda c_ref, l_ref: jnp.where(is_ctx, c_ref[rows, :], l_ref[rows, :])
        mixed = (dot(a_ref[rows, :], w_b[0:GM_W]) + dot(pair(bc_ref, bl_ref), w_b[GM_W:GM_W + HY_W])
                 + dot(pair(mc_ref, ml_ref), w_b[GM_W + HY_W:]))
        x = pair(xc_ref, xl_ref)
        if l == 0:
            x = _layer_norm(x, ng_ref[...], nb_ref[...])
        x1 = _layer_norm(ALPHA * x + m[2:3] * mixed, g_ref[l:l + 1, :], be_ref[l:l + 1, :])
        x1_ref[rows, :] = x1
        h2 = x1 * (1.0 + m[4:5]) + m[3:4]
        h_hi = h2.astype(BF16)
        h_lo = (h2 - h_hi.astype(F32)).astype(BF16)
        part = dot(h_hi, w_cat)
        logits = part[:, :128] + part[:, 128:] + dot(h_lo, w_hi) + br_ref[l:l + 1, :]
        return h_hi, logits

    def back(k, h_hi, logits):
        gl = jnp.where(is_g, logits, ninf)
        gmax = jnp.max(gl, -1, keepdims=True)
        gidx = jnp.min(jnp.where(gl == gmax, lanef, big), -1, keepdims=True)
        gw = 1.0 / jnp.sum(jnp.where(is_g, jnp.exp(logits - gmax), 0.0), -1, keepdims=True)
        in_group = (ex >= 0) & (ex < N_EXP) & ((ex // EPG).astype(F32) == gidx)
        el = jnp.where(in_group, logits, ninf)
        v1 = jnp.max(el, -1, keepdims=True)
        i1 = jnp.min(jnp.where(el == v1, lanef, big), -1, keepdims=True)
        el2 = jnp.where(lanef == i1, ninf, el)
        v2 = jnp.max(el2, -1, keepdims=True)
        i2 = jnp.min(jnp.where(el2 == v2, lanef, big), -1, keepdims=True)
        e21 = jnp.exp(v2 - v1)
        w1 = gw / (1.0 + e21)
        w2 = gw * e21 / (1.0 + e21)
        e1 = i1 - N_GROUPS
        e2 = i2 - N_GROUPS

        oh1 = jnp.where(lanef == e1, 1.0, 0.0)
        oh2 = jnp.where(lanef == e2, 1.0, 0.0)
        ex1 = dot(tri_ref[...], oh1.astype(BF16))
        ex2 = dot(tri_ref[...], oh2.astype(BF16))
        col1 = jnp.sum(oh1, axis=0, keepdims=True)
        col2 = jnp.sum(oh2, axis=0, keepdims=True)
        n = col1 + col2
        run = jnp.floor((n + (BLK - 1.0)) * (1.0 / BLK)) * BLK
        start = dot(jnp.broadcast_to(run, (8, 128)).astype(BF16), upper_ref[...])[0:1]
        loc1 = jnp.sum(oh1 * (start + ex1), -1, keepdims=True)
        loc2 = jnp.sum(oh2 * (start + col1 + ex2), -1, keepdims=True)
        cnt_ref[k * 8:(k + 1) * 8, :] = jnp.broadcast_to(n, (8, 128))

        route = jnp.zeros((TM, 128), F32)
        for j, val in enumerate((e1, e2, w1, w2, loc1, loc2)):
            route = jnp.where(lane == j, val, route)
        route_ref[k * TM:(k + 1) * TM, :] = route

        loc1_row = jnp.transpose(jnp.broadcast_to(loc1, (TM, 128)))[0:1].astype(jnp.int32)
        loc2_row = jnp.transpose(jnp.broadcast_to(loc2, (TM, 128)))[0:1].astype(jnp.int32)
        slot = lax.broadcasted_iota(jnp.int32, (LOC, TM), 0)
        perm = jnp.where((slot == loc1_row) | (slot == loc2_row), 1.0, 0.0).astype(BF16)
        xloc_ref[k * LOC:(k + 1) * LOC, :] = dot(perm, h_hi).astype(BF16)

    fronts = [front(k) for k in range(OUT_G)]
    for k, (h_hi, logits) in enumerate(fronts):
        back(k, h_hi, logits)


def _out_proj(l, a, hyb_pair, att_pair, w_out, x_pair, norm, mod_l, g, b, w_route, b_route, tri, upper):
    rows = OUT_G * TM
    tile = lambda n: pl.BlockSpec((rows, n), lambda i: (i, 0))
    pair = lambda n: [pl.BlockSpec((rows, n), lambda i: (jnp.minimum(i, CTX_TILES // OUT_G - 1), 0)),
                      pl.BlockSpec((rows, n), lambda i: (jnp.maximum(i - CTX_TILES // OUT_G, 0), 0))]
    return pl.pallas_call(
        functools.partial(_out_kernel, layer=l),
        out_shape=(jax.ShapeDtypeStruct((T, D), F32), jax.ShapeDtypeStruct((N_TILES * LOC, D), BF16),
                   jax.ShapeDtypeStruct((T, 128), F32), jax.ShapeDtypeStruct((N_TILES * 8, 128), F32)),
        grid=(N_TILES // OUT_G,),
        in_specs=[tile(GM_W)] + pair(HY_W) + pair(HEADS * VD)
        + [pl.BlockSpec((None, D, D), lambda i: (l, 0, 0), pipeline_mode=pl.Buffered(1))]
        + pair(D) + [
                  _full((1, D)), _full((1, D)),
                  pl.BlockSpec((None, 1, 6, D), lambda i: (l, _cond_of_tile(i * OUT_G), 0, 0)),
                  _full((DEPTH, D)), _full((DEPTH, D)), _layer_spec(l, D, 128), _full((DEPTH, 128)),
                  _full((TM, TM)), _full((128, 128))],
        out_specs=(tile(D), pl.BlockSpec((OUT_G * LOC, D), lambda i: (i, 0)), tile(128),
                   pl.BlockSpec((OUT_G * 8, 128), lambda i: (i, 0))),
        scratch_shapes=[pltpu.VMEM((D, D), BF16)],
        compiler_params=_cparams("arbitrary"),
        name="out_proj_route",
    )(a, *hyb_pair, *att_pair, w_out, *x_pair, *norm, mod_l, g, b, w_route, b_route, tri, upper)


Y_RING = 4
X_AHEAD = 6
X_RING = 8


def _expert_kernel(off_ref, cnt_ref, src_ref, x_hbm, wg_ref, wu_ref, wd_ref, y_hbm,
                   xbuf, ybuf, zbuf, wgu_b, wd_b, sem, osem, zsem):
    e = pl.program_id(0)
    first_tile = off_ref[e]
    n_tiles = cnt_ref[e]
    total = off_ref[N_EXP - 1] + cnt_ref[N_EXP - 1]

    def block_copies(t, s):
        return [pltpu.make_async_copy(
            x_hbm.at[pl.ds(pl.multiple_of(src_ref[t * BLK_PER_TILE + b] * BLK, BLK), BLK), :],
            xbuf.at[s, pl.ds(b * BLK, BLK), :], sem.at[s]) for b in range(BLK_PER_TILE)]

    def out_copy(t, s):
        return pltpu.make_async_copy(ybuf.at[s], y_hbm.at[pl.ds(pl.multiple_of(t * TM, TM), TM), :],
                                     osem.at[s])

    def zero_copy(t):
        return pltpu.make_async_copy(zbuf, y_hbm.at[pl.ds(pl.multiple_of(t * TM, TM), TM), :], zsem)

    @pl.when(e == 0)
    def _():
        zbuf[...] = jnp.zeros((TM, D), BF16)

        def zero_start(t, carry):
            zero_copy(t).start()
            return carry

        lax.fori_loop(total, MOE_TILES, zero_start, 0)
        for t0 in range(X_AHEAD):
            @pl.when(t0 < total)
            def _():
                for cp in block_copies(t0, t0):
                    cp.start()

    @pl.when(n_tiles > 0)
    def _():
        wgu_b[:, 0:D_EXP] = wg_ref[0, 0].astype(BF16)
        wgu_b[:, D_EXP:2 * D_EXP] = wu_ref[0, 0].astype(BF16)
        wd_b[...] = wd_ref[0, 0].astype(BF16)

    def process(tiles):
        for t in tiles:
            @pl.when(t + X_AHEAD < total)
            def _(t=t):
                for cp in block_copies(t + X_AHEAD, (t + X_AHEAD) % X_RING):
                    cp.start()

        for t in tiles:
            for cp in block_copies(t, t % X_RING):
                cp.wait()

            @pl.when(t >= Y_RING)
            def _(t=t):
                out_copy(t - Y_RING, t % Y_RING).wait()

        gus = [jnp.dot(xbuf[t % X_RING], wgu_b[...], preferred_element_type=F32) for t in tiles]
        for t, gu in zip(tiles, gus):
            gate, up = gu[:, :D_EXP], gu[:, D_EXP:]
            hid = gate / (1.0 + jnp.exp(-gate)) * up
            ybuf[t % Y_RING] = jnp.dot(hid.astype(BF16), wd_b[...], preferred_element_type=F32).astype(BF16)
            out_copy(t, t % Y_RING).start()

    def pair_body(p, carry):
        t = first_tile + 2 * p
        process([t, t + 1])
        return carry

    lax.fori_loop(0, n_tiles // 2, pair_body, 0)

    @pl.when(n_tiles % 2 == 1)
    def _():
        process([first_tile + n_tiles - 1])

    @pl.when(e == N_EXP - 1)
    def _():
        for back in range(1, Y_RING + 1):
            @pl.when(total >= back)
            def _(back=back):
                out_copy(total - back, (total - back) % Y_RING).wait()


        def zero_wait(t, carry):
            zero_copy(t).wait()
            return carry

        lax.fori_loop(total, MOE_TILES, zero_wait, 0)


def _experts(l, tables, xloc, w_gate, w_up, w_down):
    grid_spec = pltpu.PrefetchScalarGridSpec(
        num_scalar_prefetch=len(tables),
        grid=(N_EXP,),
        in_specs=[pl.BlockSpec(memory_space=pl.ANY),
                  pl.BlockSpec((1, 1, D, D_EXP), lambda e, *_: (l, e, 0, 0)),
                  pl.BlockSpec((1, 1, D, D_EXP), lambda e, *_: (l, e, 0, 0)),
                  pl.BlockSpec((1, 1, D_EXP, D), lambda e, *_: (l, e, 0, 0))],
        out_specs=pl.BlockSpec(memory_space=pl.ANY),
        scratch_shapes=[pltpu.VMEM((X_RING, TM, D), BF16), pltpu.VMEM((Y_RING, TM, D), BF16),
                        pltpu.VMEM((TM, D), BF16),
                        pltpu.VMEM((D, 2 * D_EXP), BF16), pltpu.VMEM((D_EXP, D), BF16),
                        pltpu.SemaphoreType.DMA((X_RING,)), pltpu.SemaphoreType.DMA((Y_RING,)),
                        pltpu.SemaphoreType.DMA(())],
    )
    return pl.pallas_call(
        _expert_kernel,
        out_shape=jax.ShapeDtypeStruct((MOE_ROWS, D), BF16),
        grid_spec=grid_spec,
        compiler_params=_cparams("arbitrary"),
        name="experts",
    )(*tables, xloc, w_gate, w_up, w_down)


def _combine_kernel(nblk_ref, gsrc_ref, y_hbm, route_ref, x1_ref, mod_ref, g_ref, b_ref, oc_ref, ol_ref,
                    ybuf, sem, *, layer):
    i = pl.program_id(0)

    slot_i = i % 2

    def block_copy(t, s, lb):
        src = pl.multiple_of(gsrc_ref[t * LOC_BLKS + lb] * BLK, BLK)
        return pltpu.make_async_copy(y_hbm.at[pl.ds(src, BLK), :],
                                     ybuf.at[s, pl.ds(pl.multiple_of(lb * BLK, BLK), BLK), :], sem.at[s])

    def fetch(t, s):
        def start(lb, c):
            block_copy(t, s, lb).start()
            return c
        lax.fori_loop(0, nblk_ref[t], start, 0)

    @pl.when(i == 0)
    def _():
        ybuf[...] = jnp.zeros_like(ybuf)
        fetch(0, 0)

    @pl.when(i + 1 < N_TILES)
    def _():
        fetch(jnp.minimum(i + 1, N_TILES - 1), 1 - slot_i)

    def wait(lb, c):
        block_copy(i, slot_i, lb).wait()
        return c

    lax.fori_loop(0, nblk_ref[i], wait, 0)
    route = route_ref[...]
    yb = ybuf[slot_i]
    slot = lax.broadcasted_iota(jnp.int32, (TM, LOC), 1)
    pick = (jnp.where(slot == route[:, 4:5].astype(jnp.int32), route[:, 2:3], 0.0)
            + jnp.where(slot == route[:, 5:6].astype(jnp.int32), route[:, 3:4], 0.0)).astype(BF16)
    moe = jnp.dot(pick, yb, preferred_element_type=F32)
    m = mod_ref[0]
    res = _layer_norm(ALPHA * x1_ref[...] + m[5:6] * moe, g_ref[layer:layer + 1, :], b_ref[layer:layer + 1, :])

    @pl.when(i < CTX_TILES)
    def _():
        oc_ref[...] = res

    @pl.when(i >= CTX_TILES)
    def _():
        ol_ref[...] = res


def _combine(l, nblk, gsrc, y, route, x1, mod_l, g, b):
    out_shape = (jax.ShapeDtypeStruct((T_CTX, D), F32), jax.ShapeDtypeStruct((T_LAT, D), F32))
    out_specs = tuple(_pair_specs(D))
    grid_spec = pltpu.PrefetchScalarGridSpec(
        num_scalar_prefetch=2,
        grid=(N_TILES,),
        in_specs=[pl.BlockSpec(memory_space=pl.ANY),
                  pl.BlockSpec((TM, 128), lambda i, nb, gs: (i, 0)),
                  pl.BlockSpec((TM, D), lambda i, nb, gs: (i, 0)),
                  _mod_spec(l), _full((DEPTH, D)), _full((DEPTH, D))],
        out_specs=out_specs,
        scratch_shapes=[pltpu.VMEM((2, LOC, D), BF16), pltpu.SemaphoreType.DMA((2,))],
    )
    return pl.pallas_call(
        functools.partial(_combine_kernel, layer=l),
        out_shape=out_shape,
        grid_spec=grid_spec,
        compiler_params=_cparams("arbitrary"),
        name="moe_combine",
    )(nblk, gsrc, y, route, x1, mod_l, g, b)


def _prefix_pick(starts, query, table):
    delta = table - jnp.concatenate([jnp.zeros_like(table[..., :1]), table[..., :-1]], axis=-1)
    return jnp.sum(jnp.where(starts <= query, delta, 0), axis=-1)


def _dispatch_tables(cnt):
    i32 = jnp.int32
    run = (cnt + BLK - 1) // BLK * BLK
    loc_start = jnp.cumsum(run, axis=1) - run
    nblk_loc = jnp.sum(run, axis=1) // BLK
    seg_rows = jnp.sum(run, axis=0)
    seg_tiles = (seg_rows + TM - 1) // TM
    tile_end = jnp.cumsum(seg_tiles)
    seg_start = (tile_end - seg_tiles) * TM
    glob_start = seg_start[None, :] + jnp.cumsum(run, axis=0) - run

    g_blk = (glob_start.T.reshape(1, -1)) // BLK
    n_blk = (run.T.reshape(1, -1)) // BLK
    l_blk = ((jnp.arange(N_TILES, dtype=i32) * LOC)[None, :] + loc_start.T).reshape(1, -1) // BLK
    gb = jnp.arange(MOE_ROWS // BLK, dtype=i32)[:, None]
    off = gb[:, 0] - _prefix_pick(g_blk, gb, g_blk)
    src_blk = jnp.where(off < _prefix_pick(g_blk, gb, n_blk), _prefix_pick(g_blk, gb, l_blk) + off,
                        ZERO_BLK).astype(i32)

    pos = (jnp.arange(LOC_BLKS, dtype=i32) * BLK)[None, :, None]
    shift = _prefix_pick(loc_start[:, None, :], pos, (glob_start - loc_start)[:, None, :])
    gsrc = ((pos[:, :, 0] + shift) // BLK).astype(i32)
    expert_tables = ((tile_end - seg_tiles).astype(i32), seg_tiles.astype(i32), src_blk)
    return expert_tables, nblk_loc.astype(i32), gsrc.reshape(-1)


def _rope_slot(pe):
    return jnp.pad(pe, [(0, 0)] * (pe.ndim - 1) + [(NOPE, SLOT - NOPE - ROPE)])


def _wuq_layout(w):
    w = w.reshape(DEPTH, Q_LORA, HEADS, NOPE + ROPE)
    w = jnp.pad(w, ((0, 0), (0, 0), (0, 0), (0, SLOT - NOPE - ROPE)))
    return w.reshape(DEPTH, Q_LORA, HEADS * SLOT).astype(BF16)


def _wukv_layout(w):
    w = w.reshape(DEPTH, KV_LORA, HEADS, NOPE + VD)
    wk = jnp.pad(w[..., :NOPE], ((0, 0), (0, 0), (0, 0), (0, SLOT - NOPE)))
    wv = w[..., NOPE:].reshape(DEPTH, KV_LORA, HEADS * VD).astype(BF16)
    return wk.reshape(DEPTH, KV_LORA, HEADS * SLOT).astype(BF16), wv, jnp.swapaxes(wv, 1, 2)


def kernel(x_prompt, x_sample, c, cache_ckv, cache_kpe, c_ctx, ln_in_g, ln_in_b, w_mod, b_mod, w_in,
           gm_ln_g, gm_ln_b, gm_ws, gm_bs, hy_conv_w, hy_conv_b, hy_f_w1, hy_f_b1, hy_f_w2, hy_f_b2,
           hy_f_w3, hy_f_freq, hy_bias, mla_gq, mla_gkv, mla_wuq, mla_wukv, w_out, ln1_g, ln1_b,
           ln2_g, ln2_b, moe_w_gr, moe_b_gr, moe_w_er, moe_b_er, moe_w_gate, moe_w_up, moe_w_down):
    rope = _rope_tables()
    dft = {L: _dft_tables(L) for L in (CTX_L, LAT_L)}
    pos_tab = {L: _hyena_positions(L) for L in (CTX_L, LAT_L)}
    hd = np.arange(GM_W) // GM_HD
    avg = jnp.asarray((hd[:, None] == hd[None, :]) / GM_HD, BF16)
    tri = jnp.asarray(np.tril(np.ones((TM, TM)), -1), BF16)
    upper = jnp.asarray(np.triu(np.ones((128, 128)), 1), BF16)

    cond8 = jnp.concatenate([c_ctx[None], c, jnp.zeros((8 - 1 - N_LAT_B, D), F32)], axis=0)
    mod = _modulation(cond8, w_mod, b_mod).reshape(DEPTH, 8, 6, D)

    wuk_all, wuv_all, wuvt_all = _wukv_layout(mla_wukv)
    w_in_t = jnp.swapaxes(w_in, 1, 2)
    wuq_arr = _wuq_layout(mla_wuq)
    ws_b = gm_ws.astype(BF16)
    kc_all, vc_all = _cache_kv(cache_ckv, _rope_slot(cache_kpe), wuk_all, wuvt_all)

    x_pair = (x_prompt.reshape(T_CTX, D), x_sample.reshape(T_LAT, D))
    norm = (ln_in_g.reshape(1, D), ln_in_b.reshape(1, D))
    bs_full = jnp.repeat(jnp.swapaxes(gm_bs, 1, 2), GM_HD, axis=2)
    w_route = jnp.pad(jnp.concatenate([moe_w_gr, moe_w_er.reshape(DEPTH, D, N_EXP)], axis=2),
                      ((0, 0), (0, 0), (0, 128 - N_GROUPS - N_EXP)))
    b_route = jnp.pad(jnp.concatenate([moe_b_gr, moe_b_er.reshape(DEPTH, N_EXP)], axis=1),
                      ((0, 0), (0, 128 - N_GROUPS - N_EXP)))
    hyena_groups = ((CTX_L, N_CTX_B, 4, 0), (LAT_L, N_LAT_B, 2, T_CTX // LAT_L))
    spectra = {L: _filter_spectra(L, *pos_tab[L], dft[L], hy_f_w1, hy_f_b1, hy_f_w2, hy_f_b2, hy_f_w3,
                                  hy_f_freq) for L, _, _, _ in hyena_groups}
    ckv_states, kpe_states = [], []
    for l in range(DEPTH):
        a, hy, q, k, v, vt, ckv, kpe = _in_proj(
            l, x_pair, norm, mod, w_in_t, gm_ln_g, gm_ln_b, avg,
            ws_b, bs_full, mla_gq, mla_gkv, wuq_arr, wuk_all, wuv_all, wuvt_all, rope)
        ckv_states.append(ckv[:T_CTX].reshape(N_CTX_B, CTX_L, KV_LORA))
        kpe_states.append(kpe[:T_CTX, NOPE:NOPE + ROPE].reshape(N_CTX_B, CTX_L, ROPE))

        hyb = [_hyena_group(l, hy, L, nb, group, blk0, dft[L], spectra[L], hy_conv_w, hy_conv_b, hy_bias)
               for L, nb, group, blk0 in hyena_groups]

        att = _attention(l, q, k, v, vt, kc_all, vc_all)

        x1, xloc, route, counts = _out_proj(l, a, hyb, att, w_out, x_pair, norm, mod,
                                            ln1_g, ln1_b, w_route, b_route, tri, upper)
        cnt = counts.reshape(N_TILES, 8, 128)[:, 0, :N_EXP].astype(jnp.int32)
        expert_tables, nblk_loc, gsrc = _dispatch_tables(cnt)
        y = _experts(l, expert_tables, xloc, moe_w_gate, moe_w_up, moe_w_down)
        x_pair = _combine(l, nblk_loc, gsrc, y, route, x1, mod, ln2_g, ln2_b)

    y_prompt = x_pair[0].reshape(N_CTX_B, CTX_L, D)
    y_sample = x_pair[1].reshape(N_LAT_B, LAT_L, D)
    return (y_prompt, y_sample, jnp.stack(ckv_states, axis=1), jnp.stack(kpe_states, axis=1))
```

```python
import functools
import math

import numpy as np
import jax
import jax.numpy as jnp
from jax import lax
from jax.experimental import pallas as pl
from jax.experimental.pallas import tpu as pltpu

D = 1024
N_CTX_B, CTX_L = 16, 256
N_LAT_B, LAT_L = 4, 1024
DEPTH = 2
T_CTX = N_CTX_B * CTX_L
T_LAT = N_LAT_B * LAT_L
T = T_CTX + T_LAT
PAST = 256
GRID_W = 64

GM_HEADS, GM_HD, GM_W, CHUNK = 4, 64, 256, 128
HY_W, HY_EMB, HY_FH = 256, 33, 64
NOPE, ROPE, VD, HEADS = 64, 32, 64, 8
Q_LORA, KV_LORA = 256, 128
SLOT = 128
N_GROUPS, EPG, N_EXP, D_EXP = 4, 8, 32, 256
ALPHA = (2.0 * DEPTH) ** 0.25
LN_EPS, RMS_EPS = 1e-5, 1e-6
ROPE_THETA = 10000.0

TM = 256
N_TILES = T // TM
CTX_TILES = T_CTX // TM
LAT_TILES_PER_B = LAT_L // TM
N_ASSIGN = 2 * T
BLK = 16
BLK_PER_TILE = TM // BLK
LOC = 2 * TM + N_EXP * BLK
LOC_BLKS = LOC // BLK
ZERO_BLK = LOC_BLKS - 1
MOE_TILES = (N_ASSIGN + N_TILES * N_EXP * (BLK - 1)) // TM + N_EXP
MOE_ROWS = MOE_TILES * TM
IN_COLS = 1696
IN_EXT = 1792

F32, BF16 = jnp.float32, jnp.bfloat16
_NT = (((1,), (1,)), ((), ()))
VMEM_LIMIT = 52 * 1024 * 1024


def _cparams(*sem):
    return pltpu.CompilerParams(dimension_semantics=sem, vmem_limit_bytes=VMEM_LIMIT)


def _cond_of_tile(i):
    return jnp.where(i < CTX_TILES, 0, 1 + (i - CTX_TILES) // LAT_TILES_PER_B)


def _rope_block_of_tile(i):
    return jnp.where(i < CTX_TILES, 0, 1 + (i - CTX_TILES) % LAT_TILES_PER_B)


def _full(shape):
    n = len(shape)
    return pl.BlockSpec(shape, lambda *_: (0,) * n)


def _layer_spec(l, *shape):
    return pl.BlockSpec((None,) + shape, lambda *_: (l,) + (0,) * len(shape))


def _mod_spec(l):
    return pl.BlockSpec((None, 1, 6, D), lambda i, *_: (l, _cond_of_tile(i), 0, 0))


def _pair_specs(n):
    return [pl.BlockSpec((TM, n), lambda i, *_: (jnp.minimum(i, CTX_TILES - 1), 0)),
            pl.BlockSpec((TM, n), lambda i, *_: (jnp.maximum(i - CTX_TILES, 0), 0))]


def _pair_read(c_ref, l_ref):
    return jnp.where(pl.program_id(0) < CTX_TILES, c_ref[...], l_ref[...])


def _layer_norm(x, g, b):
    mu = jnp.mean(x, -1, keepdims=True)
    xc = x - mu
    var = jnp.mean(xc * xc, -1, keepdims=True)
    return xc * lax.rsqrt(var + LN_EPS) * g + b


def _rms_norm(x, g):
    return x * lax.rsqrt(jnp.mean(x * x, -1, keepdims=True) + RMS_EPS) * g


def _bdot(a, b):
    return jnp.dot(a.astype(BF16), b.astype(BF16), preferred_element_type=F32)


def _dot3(x, w):
    x_hi = x.astype(BF16)
    x_lo = (x - x_hi.astype(F32)).astype(BF16)
    w_hi = w.astype(BF16)
    w_lo = (w - w_hi.astype(F32)).astype(BF16)
    dot = functools.partial(jnp.dot, preferred_element_type=F32)
    return dot(x_hi, w_hi) + dot(x_hi, w_lo) + dot(x_lo, w_hi)


def _swap_pair_lanes(x):
    n = x.shape[1]
    lane = lax.broadcasted_iota(jnp.int32, x.shape, 1)
    return jnp.where(lane % 2 == 0, pltpu.roll(x, n - 1, 1), pltpu.roll(x, 1, 1))


def _split2_dot(x, w_bf16):
    hi = x.astype(BF16)
    lo = (x - hi.astype(F32)).astype(BF16)
    dot = functools.partial(jnp.dot, preferred_element_type=F32)
    return dot(hi, w_bf16) + dot(lo, w_bf16)


def _dft_tables(L):
    f = np.arange(L, dtype=np.int64)
    ft = np.outer(f, f) % (2 * L)
    ang = np.pi * ft / L
    c = np.cos(ang)
    s = np.sin(ang)
    alt = np.where(f % 2 == 0, 1.0, -1.0)
    sf = s.copy()
    sf[0, :] = alt
    return (jnp.asarray(c, F32).astype(BF16), jnp.asarray(sf, F32).astype(BF16),
            jnp.asarray(sf.T.copy(), F32).astype(BF16), jnp.asarray(alt[:, None], F32))


def _rope_tables():
    rows = LAT_L // GRID_W
    row = np.repeat(np.arange(rows, dtype=np.float64), GRID_W)
    col = np.tile(np.arange(GRID_W, dtype=np.float64), rows)
    n = ROPE // 4
    inv = ROPE_THETA ** (-np.arange(n, dtype=np.float64) / n)
    ang = np.concatenate([row[:, None] * inv, col[:, None] * inv], -1)
    cos = np.zeros((CTX_L + LAT_L, SLOT))
    sin = np.zeros((CTX_L + LAT_L, SLOT))
    cos[:, :NOPE + ROPE] = 1.0
    cos[CTX_L:, NOPE:NOPE + ROPE:2] = np.cos(ang)
    cos[CTX_L:, NOPE + 1:NOPE + ROPE:2] = np.cos(ang)
    sin[CTX_L:, NOPE:NOPE + ROPE:2] = -np.sin(ang)
    sin[CTX_L:, NOPE + 1:NOPE + ROPE:2] = np.sin(ang)
    scale = math.log2(math.e) / math.sqrt(NOPE + ROPE)
    return (jnp.asarray(cos * scale, F32), jnp.asarray(sin * scale, F32),
            jnp.asarray(cos, F32), jnp.asarray(sin, F32))


def _hyena_positions(L):
    t = jnp.linspace(0.0, 1.0, L, dtype=F32)[:, None]
    bands = (HY_EMB - 1) // 2
    w = 2.0 * math.pi * jnp.arange(L, dtype=F32)[:, None] / L
    f = jnp.linspace(1e-4, bands - 1, bands, dtype=F32)[None]
    z = jnp.concatenate([t, jnp.cos(f * w), -jnp.sin(f * w)], -1)
    z = jnp.pad(z, ((0, 0), (0, 128 - HY_EMB)))
    min_decay = math.log(1e-2) / 1.5
    max_decay = math.log(1e-2) / 0.3
    deltas = jnp.abs(jnp.linspace(min_decay, max_decay, HY_W, dtype=F32))
    return z, jnp.exp(-t * deltas)


def _mod_kernel(c_ref, w_ref, b_ref, o_ref):
    c = c_ref[...]
    s = c / (1.0 + jnp.exp(-c))
    o_ref[0] = _bdot(s, w_ref[0]) + b_ref[0]


def _modulation(cond8, w_mod, b_mod):
    tn = 1536
    return pl.pallas_call(
        _mod_kernel,
        out_shape=jax.ShapeDtypeStruct((DEPTH, 8, 6 * D), F32),
        grid=(DEPTH, 6 * D // tn),
        in_specs=[pl.BlockSpec((8, D), lambda l, j: (0, 0)),
                  pl.BlockSpec((1, D, tn), lambda l, j: (l, 0, j)),
                  pl.BlockSpec((1, 1, tn), lambda l, j: (l, 0, j))],
        out_specs=pl.BlockSpec((1, 8, tn), lambda l, j: (l, 0, j)),
        compiler_params=_cparams("parallel", "parallel"),
        name="modulation",
    )(cond8, w_mod, b_mod.reshape(DEPTH, 1, 6 * D))


def _filter_kernel(z_ref, w1_ref, b1_ref, w2_ref, b2_ref, w3_ref, fr_ref, dec_ref, c_ref, sf_ref,
                   alt_ref, kr_ref, ki_ref, krn_ref, *, L):
    h1 = jnp.sin(fr_ref[0:1] * (_dot3(z_ref[...], w1_ref[...]) + b1_ref[...]))
    h2 = jnp.sin(fr_ref[1:2] * (_dot3(h1, w2_ref[...]) + b2_ref[...]))
    filt = _dot3(h2, w3_ref[...])
    row = lax.broadcasted_iota(jnp.int32, (L, HY_W), 0)
    dec = dec_ref[...]
    n = DEPTH * 2
    fwd = [filt[:, j * 2 * HY_W:j * 2 * HY_W + HY_W] * dec for j in range(n)]
    bwd = [jnp.where(row == 0, 0.0, filt[:, j * 2 * HY_W + HY_W:(j + 1) * 2 * HY_W] * dec) for j in range(n)]
    sm = [f + b for f, b in zip(fwd, bwd)]
    kr_all = _bdot(c_ref[...], jnp.concatenate(sm, axis=1))
    ki_all = -_bdot(sf_ref[...], jnp.concatenate([f - b for f, b in zip(fwd, bwd)], axis=1))
    for j in range(n):
        l, o = divmod(j, 2)
        kr = kr_all[:, j * HY_W:(j + 1) * HY_W]
        kr_nyq = jnp.sum(alt_ref[...] * sm[j], axis=0, keepdims=True)
        kr_ref[l, o] = jnp.where(row == 0, kr * (0.5 / L), kr * (1.0 / L))
        ki_ref[l, o] = jnp.where(row == 0, 0.0, ki_all[:, j * HY_W:(j + 1) * HY_W] * (1.0 / L))
        krn_ref[l, o] = jnp.where(row == 0, kr_nyq * (0.5 / L), kr * (1.0 / L))


def _block_diag(blocks):
    rows = []
    for i, blk in enumerate(blocks):
        rows.append(jnp.concatenate([blk if j == i else jnp.zeros((blk.shape[0], other.shape[1]), blk.dtype)
                                     for j, other in enumerate(blocks)], axis=1))
    return jnp.concatenate(rows, axis=0)


def _filter_spectra(L, z, dec, dft, w1, b1, w2, b2, w3, freq):
    assert DEPTH * HY_FH == 128
    c, sf, _, alt = dft
    layers = range(DEPTH)
    w1c = jnp.pad(jnp.concatenate([w1[l] for l in layers], axis=1), ((0, 128 - HY_EMB), (0, 0)))
    w2c = _block_diag([w2[l] for l in layers])
    w3c = _block_diag([w3[l] for l in layers])
    b1c, b2c = b1.reshape(1, 128), b2.reshape(1, 128)
    frc = jnp.swapaxes(freq, 0, 1).reshape(2, 128)
    shp = jax.ShapeDtypeStruct((DEPTH, 2, L, HY_W), F32)
    args = (z, w1c, b1c, w2c, b2c, w3c, frc, dec, c, sf, alt)
    return pl.pallas_call(
        functools.partial(_filter_kernel, L=L),
        out_shape=(shp, shp, shp),
        grid=(1,),
        in_specs=[_full(a.shape) for a in args],
        out_specs=(_full((DEPTH, 2, L, HY_W)),) * 3,
        compiler_params=_cparams("arbitrary"),
        name=f"hyena_filter_{L}",
    )(*args)


IN_G = 2


def _in_kernel(xc_ref, xl_ref, ng_ref, nb_ref, mod_ref, wt_ref, gg_ref, gb_ref, avg_ref, ws_ref, bs_ref,
               gq_ref, gkv_ref, wuq_ref, wuk_ref, wuv_ref, wuvt_ref, *rest, layer):
    rope_refs = rest[:4 * IN_G]
    a_ref, hy_ref, q_ref, k_ref, v_ref, vt_ref, ckv_ref, kpe_ref, w_b = rest[4 * IN_G:]
    l = layer

    @pl.when(pl.program_id(0) == 0)
    def _():
        for c0 in range(0, 1664, 128):
            w_b[:, c0:c0 + 128] = jnp.transpose(wt_ref[c0:c0 + 128, :]).astype(BF16)
        slot_rows = jnp.concatenate([jnp.zeros((NOPE, D), F32), wt_ref[1664:IN_COLS, :],
                                     jnp.zeros((SLOT - NOPE - ROPE, D), F32)], axis=0)
        w_b[:, 1664:IN_EXT] = jnp.transpose(slot_rows).astype(BF16)

    is_ctx = pl.program_id(0) < CTX_TILES // IN_G
    m = mod_ref[0]
    avg = avg_ref[...]
    lane = lax.broadcasted_iota(jnp.int32, (CHUNK, GM_W), 1)

    def front(k):
        rows = pl.ds(k * TM, TM)
        x = jnp.where(is_ctx, xc_ref[rows, :], xl_ref[rows, :])
        if l == 0:
            x = _layer_norm(x, ng_ref[...], nb_ref[...])
        h = x * (1.0 + m[1:2]) + m[0:1]
        return _bdot(h, w_b[...])

    def back(k, proj):
        r0 = k * TM
        cq_ref, sq_ref, ck_ref, sk_ref = rope_refs[4 * k:4 * k + 4]
        gu = jax.nn.gelu(proj[:, 0:GM_W], approximate=True)
        gv = jax.nn.gelu(proj[:, GM_W:2 * GM_W], approximate=True)
        mu = _split2_dot(gv, avg)
        vc = gv - mu
        var = _split2_dot(vc * vc, avg)
        vln = (vc * lax.rsqrt(var + LN_EPS) * gg_ref[l:l + 1, :] + gb_ref[l:l + 1, :]).astype(BF16)
        for c in range(TM // CHUNK):
            vchunk = vln[c * CHUNK:(c + 1) * CHUNK]
            s = bs_ref[...]
            for hd in range(GM_HEADS):
                sh = jnp.dot(ws_ref[0, hd], vchunk, preferred_element_type=F32)
                s = s + jnp.where(lane // GM_HD == hd, sh, 0.0)
            a_ref[r0 + c * CHUNK:r0 + (c + 1) * CHUNK, :] = (gu[c * CHUNK:(c + 1) * CHUNK] * s).astype(BF16)

        hy_ref[r0:r0 + TM, :] = proj[:, 512:1280]

        cq = _rms_norm(proj[:, 1280:1536], gq_ref[l:l + 1, :])
        qq = _bdot(cq, wuq_ref[0])
        cos_q = jnp.concatenate([cq_ref[...]] * HEADS, axis=1)
        sin_q = jnp.concatenate([sq_ref[...]] * HEADS, axis=1)
        q_ref[r0:r0 + TM, :] = (qq * cos_q + _swap_pair_lanes(qq) * sin_q).astype(BF16)

        ckv = _rms_norm(proj[:, 1536:1664], gkv_ref[l:l + 1, :])
        ckv_ref[r0:r0 + TM, :] = ckv
        kpe = proj[:, 1664:IN_EXT]
        kpe_ref[r0:r0 + TM, :] = kpe
        krot = kpe * ck_ref[...] + _swap_pair_lanes(kpe) * sk_ref[...]
        kn = _bdot(ckv, wuk_ref[0])
        k_ref[r0:r0 + TM, :] = (kn + jnp.concatenate([krot] * HEADS, axis=1)).astype(BF16)
        v_ref[r0:r0 + TM, :] = _bdot(ckv, wuv_ref[0]).astype(BF16)
        vt_ref[:, r0:r0 + TM] = lax.dot_general(wuvt_ref[0], ckv.astype(BF16), _NT,
                                                preferred_element_type=F32).astype(BF16)

    projs = [front(k) for k in range(IN_G)]
    for k, proj in enumerate(projs):
        back(k, proj)


def _in_proj(l, x_pair, norm, mod_l, w_in_t, gm_g, gm_b, avg, ws, bs_full, gq, gkv, wuq_arr,
             wuk_arr, wuv_arr, wuvt_arr, rope):
    layer = lambda *shape: pl.BlockSpec((1,) + shape, lambda i: (l,) + (0,) * len(shape))
    rows = IN_G * TM
    tile = lambda n: pl.BlockSpec((rows, n), lambda i: (i, 0))
    pair = [pl.BlockSpec((rows, D), lambda i: (jnp.minimum(i, CTX_TILES // IN_G - 1), 0)),
            pl.BlockSpec((rows, D), lambda i: (jnp.maximum(i - CTX_TILES // IN_G, 0), 0))]
    rope_specs = [pl.BlockSpec((TM, SLOT), lambda i, k=k: (_rope_block_of_tile(i * IN_G + k), 0))
                  for k in range(IN_G) for _ in rope]
    out_shapes = (jax.ShapeDtypeStruct((T, GM_W), BF16),
                  jax.ShapeDtypeStruct((T, 3 * HY_W), F32),
                  jax.ShapeDtypeStruct((T, HEADS * SLOT), BF16),
                  jax.ShapeDtypeStruct((T, HEADS * SLOT), BF16),
                  jax.ShapeDtypeStruct((T, HEADS * VD), BF16),
                  jax.ShapeDtypeStruct((HEADS * VD, T), BF16),
                  jax.ShapeDtypeStruct((T, KV_LORA), F32),
                  jax.ShapeDtypeStruct((T, SLOT), F32))
    return pl.pallas_call(
        functools.partial(_in_kernel, layer=l),
        out_shape=out_shapes,
        grid=(N_TILES // IN_G,),
        in_specs=pair + [
                  _full((1, D)), _full((1, D)),
                  pl.BlockSpec((None, 1, 6, D), lambda i: (l, _cond_of_tile(i * IN_G), 0, 0)),
                  pl.BlockSpec((None, IN_COLS, D), lambda i: (l, 0, 0), pipeline_mode=pl.Buffered(1)),
                  _full((DEPTH, GM_W)), _full((DEPTH, GM_W)), _full((GM_W, GM_W)),
                  layer(GM_HEADS, CHUNK, CHUNK), _layer_spec(l, CHUNK, GM_W),
                  _full((DEPTH, Q_LORA)), _full((DEPTH, KV_LORA)),
                  layer(Q_LORA, HEADS * SLOT), layer(KV_LORA, HEADS * SLOT),
                  layer(KV_LORA, HEADS * VD), layer(HEADS * VD, KV_LORA)] + rope_specs,
        out_specs=(tile(GM_W), tile(3 * HY_W), tile(HEADS * SLOT), tile(HEADS * SLOT), tile(HEADS * VD),
                   pl.BlockSpec((HEADS * VD, rows), lambda i: (0, i)), tile(KV_LORA), tile(SLOT)),
        scratch_shapes=[pltpu.VMEM((D, IN_EXT), BF16)],
        compiler_params=_cparams("arbitrary"),
        name="in_proj",
    )(*x_pair, *norm, mod_l, w_in_t, gm_g, gm_b, avg, ws, bs_full, gq, gkv, wuq_arr, wuk_arr, wuv_arr,
      wuvt_arr, *(list(rope) * IN_G))


def _cache_kernel(ckv_ref, kpe_ref, wuk_ref, wuv_ref, k_ref, v_ref):
    ckv = ckv_ref[0, 0]
    kn = _bdot(ckv, wuk_ref[0])
    k_ref[0, 0] = (kn + jnp.concatenate([kpe_ref[0, 0]] * HEADS, axis=1)).astype(BF16)
    v_ref[0, 0] = lax.dot_general(wuv_ref[0], ckv.astype(BF16), _NT,
                                  preferred_element_type=F32).astype(BF16)


def _cache_kv(cache_ckv, cache_kpe_slot, wuk_arr, wuv_arr):
    return pl.pallas_call(
        _cache_kernel,
        out_shape=(jax.ShapeDtypeStruct((DEPTH, N_LAT_B, PAST, HEADS * SLOT), BF16),
                   jax.ShapeDtypeStruct((DEPTH, N_LAT_B, HEADS * VD, PAST), BF16)),
        grid=(DEPTH, N_LAT_B),
        in_specs=[pl.BlockSpec((1, 1, PAST, KV_LORA), lambda l, b: (b, l, 0, 0)),
                  pl.BlockSpec((1, 1, PAST, SLOT), lambda l, b: (b, l, 0, 0)),
                  pl.BlockSpec((1, KV_LORA, HEADS * SLOT), lambda l, b: (l, 0, 0)),
                  pl.BlockSpec((1, HEADS * VD, KV_LORA), lambda l, b: (l, 0, 0))],
        out_specs=(pl.BlockSpec((1, 1, PAST, HEADS * SLOT), lambda l, b: (l, b, 0, 0)),
                   pl.BlockSpec((1, 1, HEADS * VD, PAST), lambda l, b: (l, b, 0, 0))),
        compiler_params=_cparams("parallel", "parallel"),
        name="cache_kv",
    )(cache_ckv, cache_kpe_slot, wuk_arr, wuv_arr)


def _attend_rows(q_ref, k_ref, v_ref, o_ref):
    lq = q_ref.shape[0]
    lane = lax.broadcasted_iota(jnp.int32, (lq, 2 * VD), 1)

    def scores_of(hd):
        return lax.dot_general(q_ref[:, hd * SLOT:(hd + 1) * SLOT], k_ref[:, hd * SLOT:(hd + 1) * SLOT], _NT,
                               preferred_element_type=F32)

    outs = []
    nxt = scores_of(0)
    for hd in range(HEADS):
        s = nxt
        if hd + 1 < HEADS:
            nxt = scores_of(hd + 1)
        p = jnp.exp2(s - jnp.max(s, -1, keepdims=True))
        acc = jnp.dot(p.astype(BF16), v_ref[:, (hd // 2) * 2 * VD:(hd // 2 + 1) * 2 * VD],
                      preferred_element_type=F32)
        outs.append(acc / jnp.sum(p, -1, keepdims=True))
        if hd % 2 == 1:
            pair = hd // 2
            o_ref[:, pair * 2 * VD:(pair + 1) * 2 * VD] = jnp.where(lane < VD, outs[-2], outs[-1]).astype(BF16)


def _attend_cols(q_ref, segments, o_ref):
    lq = q_ref.shape[0]
    lane = lax.broadcasted_iota(jnp.int32, (lq, 2 * SLOT), 1)
    row = lax.broadcasted_iota(jnp.int32, (2 * VD, lq), 0)
    n_pairs = HEADS // 2

    def scores_of(pair):
        qp = q_ref[:, pair * 2 * SLOT:(pair + 1) * 2 * SLOT]
        zero = jnp.zeros_like(qp)
        q_bd = jnp.concatenate([jnp.where(lane < SLOT, qp, zero), jnp.where(lane < SLOT, zero, qp)], axis=0)
        return [lax.dot_general(k_ref[:, pair * 2 * SLOT:(pair + 1) * 2 * SLOT], q_bd, _NT,
                                preferred_element_type=F32) for k_ref, _ in segments]

    def head_pair_out(pair, scores):
        mx = functools.reduce(jnp.maximum, [jnp.max(s, 0, keepdims=True) for s in scores])
        ps = [jnp.exp2(s - mx) for s in scores]
        den = functools.reduce(jnp.add, [jnp.sum(p, 0, keepdims=True) for p in ps])
        acc = functools.reduce(jnp.add, [
            jnp.dot(vt_ref[pair * 2 * VD:(pair + 1) * 2 * VD, :], p.astype(BF16),
                    preferred_element_type=F32) for p, (_, vt_ref) in zip(ps, segments)])
        acc = acc / den
        return jnp.where(row < VD, acc[:, :lq], acc[:, lq:])

    outs = []
    nxt = scores_of(0)
    for pair in range(n_pairs):
        cur = nxt
        if pair + 1 < n_pairs:
            nxt = scores_of(pair + 1)
        outs.append(head_pair_out(pair, cur))
    o_ref[...] = jnp.transpose(jnp.concatenate(outs, axis=0)).astype(BF16)


def _attn_ctx_kernel(q_ref, k_ref, v_ref, o_ref):
    _attend_rows(q_ref, k_ref, v_ref, o_ref)


def _attn_lat_kernel(q_ref, kc_ref, vc_ref, k_ref, vt_ref, o_ref):
    _attend_cols(q_ref, [(kc_ref.at[0], vc_ref.at[0]), (k_ref, vt_ref)], o_ref)


def _attention(l, q, k, v, vt, kc, vc):
    kw, vw = HEADS * SLOT, HEADS * VD
    ctx = pl.pallas_call(
        _attn_ctx_kernel,
        out_shape=jax.ShapeDtypeStruct((T_CTX, vw), BF16),
        grid=(N_CTX_B,),
        in_specs=[pl.BlockSpec((CTX_L, kw), lambda b: (b, 0)),
                  pl.BlockSpec((CTX_L, kw), lambda b: (b, 0)),
                  pl.BlockSpec((CTX_L, vw), lambda b: (b, 0))],
        out_specs=pl.BlockSpec((CTX_L, vw), lambda b: (b, 0)),
        compiler_params=_cparams("parallel"),
        name="attn_ctx",
    )(q, k, v)
    nq = LAT_L // TM
    off = T_CTX // LAT_L
    lat = pl.pallas_call(
        _attn_lat_kernel,
        out_shape=jax.ShapeDtypeStruct((T_LAT, vw), BF16),
        grid=(N_LAT_B, nq),
        in_specs=[pl.BlockSpec((TM, kw), lambda b, j: (CTX_TILES + b * nq + j, 0)),
                  pl.BlockSpec((None, 1, PAST, kw), lambda b, j: (l, b, 0, 0)),
                  pl.BlockSpec((None, 1, vw, PAST), lambda b, j: (l, b, 0, 0)),
                  pl.BlockSpec((LAT_L, kw), lambda b, j: (off + b, 0)),
                  pl.BlockSpec((vw, LAT_L), lambda b, j: (0, off + b))],
        out_specs=pl.BlockSpec((TM, vw), lambda b, j: (b * nq + j, 0)),
        compiler_params=_cparams("parallel", "parallel"),
        name="attn_lat",
    )(q, kc, vc, k, vt)
    return ctx, lat


def _hyena_kernel(hy_ref, cw_ref, cb_ref, c_ref, sf_ref, sb_ref, kr_ref, ki_ref, krn_ref, hb_ref,
                  o_ref, *, L, group, layer):
    row = lax.broadcasted_iota(jnp.int32, (L, 3 * HY_W), 0)
    zs = []
    for j in range(group):
        x = hy_ref[j * L:(j + 1) * L, :]
        prev = jnp.where(row == 0, 0.0, pltpu.roll(x, 1, 0))
        nxt = jnp.where(row == L - 1, 0.0, pltpu.roll(x, L - 1, 0))
        zs.append(prev * cw_ref[0:1] + x * cw_ref[1:2] + nxt * cw_ref[2:3] + cb_ref[layer:layer + 1, :])
    side = lambda parts: jnp.concatenate(parts, axis=1)
    y = side([z[:, 2 * HY_W:] for z in zs])
    for o in range(2):
        gate = side([z[:, o * HY_W:(o + 1) * HY_W] for z in zs])
        kr, ki, krn = (side([r[o]] * group) for r in (kr_ref, ki_ref, krn_ref))
        yb = y.astype(BF16)
        a_re = jnp.dot(c_ref[...], yb, preferred_element_type=F32)
        a_im = jnp.dot(sf_ref[...], yb, preferred_element_type=F32)
        z_re = a_re * kr + a_im * ki
        z_im = a_im * krn - a_re * ki
        conv = (jnp.dot(c_ref[...], z_re.astype(BF16), preferred_element_type=F32)
                + jnp.dot(sb_ref[...], z_im.astype(BF16), preferred_element_type=F32))
        y = gate * (conv + y * side([hb_ref[o:o + 1]] * group))
    for j in range(group):
        o_ref[j * L:(j + 1) * L, :] = y[:, j * HY_W:(j + 1) * HY_W].astype(BF16)


def _hyena_group(l, hy, L, nb, group, blk0, dft, spectra, conv_w, conv_b, hy_bias):
    c, sf, sb, _ = dft
    kr, ki, krn = spectra
    const = lambda shape: pl.BlockSpec(shape, lambda b: (0,) * len(shape), pipeline_mode=pl.Buffered(1))
    spec = pl.BlockSpec((None, 2, L, HY_W), lambda b: (l, 0, 0, 0), pipeline_mode=pl.Buffered(1))
    return pl.pallas_call(
        functools.partial(_hyena_kernel, L=L, group=group, layer=l),
        out_shape=jax.ShapeDtypeStruct((nb * L, HY_W), BF16),
        grid=(nb // group,),
        in_specs=[pl.BlockSpec((group * L, 3 * HY_W), lambda b: (blk0 // group + b, 0)),
                  _layer_spec(l, 3, 3 * HY_W), _full((DEPTH, 3 * HY_W)),
                  const((L, L)), const((L, L)), const((L, L)), spec, spec, spec,
                  _layer_spec(l, 2, HY_W)],
        out_specs=pl.BlockSpec((group * L, HY_W), lambda b: (b, 0)),
        compiler_params=_cparams("parallel"),
        name=f"hyena_{L}",
    )(hy, conv_w, conv_b, c, sf, sb, kr, ki, krn, hy_bias)


OUT_G = 2


def _out_kernel(a_ref, bc_ref, bl_ref, mc_ref, ml_ref, w_ref, xc_ref, xl_ref, ng_ref, nb_ref, mod_ref,
                g_ref, be_ref, wr_ref, br_ref, tri_ref, upper_ref, x1_ref, xloc_ref, route_ref, cnt_ref,
                w_b, *, layer):
    l = layer

    @pl.when(pl.program_id(0) == 0)
    def _():
        w_b[...] = w_ref[...].astype(BF16)

    dot = functools.partial(jnp.dot, preferred_element_type=F32)
    is_ctx = pl.program_id(0) < CTX_TILES // OUT_G
    w_hi = wr_ref[...].astype(BF16)
    w_lo = (wr_ref[...] - w_hi.astype(F32)).astype(BF16)
    w_cat = jnp.concatenate([w_hi, w_lo], axis=1)
    lane = lax.broadcasted_iota(jnp.int32, (TM, 128), 1)
    lanef = lane.astype(F32)
    big = jnp.float32(1e9)
    ninf = jnp.float32(-jnp.inf)
    is_g = lane < N_GROUPS
    ex = lane - N_GROUPS
    m = mod_ref[0]

    def front(k):
        rows = pl.ds(k * TM, TM)
        pair = lambda c_ref, l_ref: jnp.where(is_ctx, c_ref[rows, :], l_ref[rows, :])
        mixed = (dot(a_ref[rows, :], w_b[0:GM_W]) + dot(pair(bc_ref, bl_ref), w_b[GM_W:GM_W + HY_W])
                 + dot(pair(mc_ref, ml_ref), w_b[GM_W + HY_W:]))
        x = pair(xc_ref, xl_ref)
        if l == 0:
            x = _layer_norm(x, ng_ref[...], nb_ref[...])
        x1 = _layer_norm(ALPHA * x + m[2:3] * mixed, g_ref[l:l + 1, :], be_ref[l:l + 1, :])
        x1_ref[rows, :] = x1
        h2 = x1 * (1.0 + m[4:5]) + m[3:4]
        h_hi = h2.astype(BF16)
        h_lo = (h2 - h_hi.astype(F32)).astype(BF16)
        part = dot(h_hi, w_cat)
        logits = part[:, :128] + part[:, 128:] + dot(h_lo, w_hi) + br_ref[l:l + 1, :]
        return h_hi, logits

    def back(k, h_hi, logits):
        gl = jnp.where(is_g, logits, ninf)
        gmax = jnp.max(gl, -1, keepdims=True)
        gidx = jnp.min(jnp.where(gl == gmax, lanef, big), -1, keepdims=True)
        gw = 1.0 / jnp.sum(jnp.where(is_g, jnp.exp(logits - gmax), 0.0), -1, keepdims=True)
        in_group = (ex >= 0) & (ex < N_EXP) & ((ex // EPG).astype(F32) == gidx)
        el = jnp.where(in_group, logits, ninf)
        v1 = jnp.max(el, -1, keepdims=True)
        i1 = jnp.min(jnp.where(el == v1, lanef, big), -1, keepdims=True)
        el2 = jnp.where(lanef == i1, ninf, el)
        v2 = jnp.max(el2, -1, keepdims=True)
        i2 = jnp.min(jnp.where(el2 == v2, lanef, big), -1, keepdims=True)
        e21 = jnp.exp(v2 - v1)
        w1 = gw / (1.0 + e21)
        w2 = gw * e21 / (1.0 + e21)
        e1 = i1 - N_GROUPS
        e2 = i2 - N_GROUPS

        oh1 = jnp.where(lanef == e1, 1.0, 0.0)
        oh2 = jnp.where(lanef == e2, 1.0, 0.0)
        ex1 = dot(tri_ref[...], oh1.astype(BF16))
        ex2 = dot(tri_ref[...], oh2.astype(BF16))
        col1 = jnp.sum(oh1, axis=0, keepdims=True)
        col2 = jnp.sum(oh2, axis=0, keepdims=True)
        n = col1 + col2
        run = jnp.floor((n + (BLK - 1.0)) * (1.0 / BLK)) * BLK
        start = dot(jnp.broadcast_to(run, (8, 128)).astype(BF16), upper_ref[...])[0:1]
        loc1 = jnp.sum(oh1 * (start + ex1), -1, keepdims=True)
        loc2 = jnp.sum(oh2 * (start + col1 + ex2), -1, keepdims=True)
        cnt_ref[k * 8:(k + 1) * 8, :] = jnp.broadcast_to(n, (8, 128))

        route = jnp.zeros((TM, 128), F32)
        for j, val in enumerate((e1, e2, w1, w2, loc1, loc2)):
            route = jnp.where(lane == j, val, route)
        route_ref[k * TM:(k + 1) * TM, :] = route

        loc1_row = jnp.transpose(jnp.broadcast_to(loc1, (TM, 128)))[0:1].astype(jnp.int32)
        loc2_row = jnp.transpose(jnp.broadcast_to(loc2, (TM, 128)))[0:1].astype(jnp.int32)
        slot = lax.broadcasted_iota(jnp.int32, (LOC, TM), 0)
        perm = jnp.where((slot == loc1_row) | (slot == loc2_row), 1.0, 0.0).astype(BF16)
        xloc_ref[k * LOC:(k + 1) * LOC, :] = dot(perm, h_hi).astype(BF16)

    fronts = [front(k) for k in range(OUT_G)]
    for k, (h_hi, logits) in enumerate(fronts):
        back(k, h_hi, logits)


def _out_proj(l, a, hyb_pair, att_pair, w_out, x_pair, norm, mod_l, g, b, w_route, b_route, tri, upper):
    rows = OUT_G * TM
    tile = lambda n: pl.BlockSpec((rows, n), lambda i: (i, 0))
    pair = lambda n: [pl.BlockSpec((rows, n), lambda i: (jnp.minimum(i, CTX_TILES // OUT_G - 1), 0)),
                      pl.BlockSpec((rows, n), lambda i: (jnp.maximum(i - CTX_TILES // OUT_G, 0), 0))]
    return pl.pallas_call(
        functools.partial(_out_kernel, layer=l),
        out_shape=(jax.ShapeDtypeStruct((T, D), F32), jax.ShapeDtypeStruct((N_TILES * LOC, D), BF16),
                   jax.ShapeDtypeStruct((T, 128), F32), jax.ShapeDtypeStruct((N_TILES * 8, 128), F32)),
        grid=(N_TILES // OUT_G,),
        in_specs=[tile(GM_W)] + pair(HY_W) + pair(HEADS * VD)
        + [pl.BlockSpec((None, D, D), lambda i: (l, 0, 0), pipeline_mode=pl.Buffered(1))]
        + pair(D) + [
                  _full((1, D)), _full((1, D)),
                  pl.BlockSpec((None, 1, 6, D), lambda i: (l, _cond_of_tile(i * OUT_G), 0, 0)),
                  _full((DEPTH, D)), _full((DEPTH, D)), _layer_spec(l, D, 128), _full((DEPTH, 128)),
                  _full((TM, TM)), _full((128, 128))],
        out_specs=(tile(D), pl.BlockSpec((OUT_G * LOC, D), lambda i: (i, 0)), tile(128),
                   pl.BlockSpec((OUT_G * 8, 128), lambda i: (i, 0))),
        scratch_shapes=[pltpu.VMEM((D, D), BF16)],
        compiler_params=_cparams("arbitrary"),
        name="out_proj_route",
    )(a, *hyb_pair, *att_pair, w_out, *x_pair, *norm, mod_l, g, b, w_route, b_route, tri, upper)


Y_RING = 4
X_AHEAD = 6
X_RING = 8


def _expert_kernel(off_ref, cnt_ref, src_ref, x_hbm, wg_ref, wu_ref, wd_ref, y_hbm,
                   xbuf, ybuf, zbuf, wgu_b, wd_b, sem, osem, zsem):
    e = pl.program_id(0)
    first_tile = off_ref[e]
    n_tiles = cnt_ref[e]
    total = off_ref[N_EXP - 1] + cnt_ref[N_EXP - 1]

    def block_copies(t, s):
        return [pltpu.make_async_copy(
            x_hbm.at[pl.ds(pl.multiple_of(src_ref[t * BLK_PER_TILE + b] * BLK, BLK), BLK), :],
            xbuf.at[s, pl.ds(b * BLK, BLK), :], sem.at[s]) for b in range(BLK_PER_TILE)]

    def out_copy(t, s):
        return pltpu.make_async_copy(ybuf.at[s], y_hbm.at[pl.ds(pl.multiple_of(t * TM, TM), TM), :],
                                     osem.at[s])

    def zero_copy(t):
        return pltpu.make_async_copy(zbuf, y_hbm.at[pl.ds(pl.multiple_of(t * TM, TM), TM), :], zsem)

    @pl.when(e == 0)
    def _():
        zbuf[...] = jnp.zeros((TM, D), BF16)

        def zero_start(t, carry):
            zero_copy(t).start()
            return carry

        lax.fori_loop(total, MOE_TILES, zero_start, 0)
        for t0 in range(X_AHEAD):
            @pl.when(t0 < total)
            def _():
                for cp in block_copies(t0, t0):
                    cp.start()

    @pl.when(n_tiles > 0)
    def _():
        wgu_b[:, 0:D_EXP] = wg_ref[0, 0].astype(BF16)
        wgu_b[:, D_EXP:2 * D_EXP] = wu_ref[0, 0].astype(BF16)
        wd_b[...] = wd_ref[0, 0].astype(BF16)

    def process(tiles):
        for t in tiles:
            @pl.when(t + X_AHEAD < total)
            def _(t=t):
                for cp in block_copies(t + X_AHEAD, (t + X_AHEAD) % X_RING):
                    cp.start()

        for t in tiles:
            for cp in block_copies(t, t % X_RING):
                cp.wait()

            @pl.when(t >= Y_RING)
            def _(t=t):
                out_copy(t - Y_RING, t % Y_RING).wait()

        gus = [jnp.dot(xbuf[t % X_RING], wgu_b[...], preferred_element_type=F32) for t in tiles]
        for t, gu in zip(tiles, gus):
            gate, up = gu[:, :D_EXP], gu[:, D_EXP:]
            hid = gate / (1.0 + jnp.exp(-gate)) * up
            ybuf[t % Y_RING] = jnp.dot(hid.astype(BF16), wd_b[...], preferred_element_type=F32).astype(BF16)
            out_copy(t, t % Y_RING).start()

    def pair_body(p, carry):
        t = first_tile + 2 * p
        process([t, t + 1])
        return carry

    lax.fori_loop(0, n_tiles // 2, pair_body, 0)

    @pl.when(n_tiles % 2 == 1)
    def _():
        process([first_tile + n_tiles - 1])

    @pl.when(e == N_EXP - 1)
    def _():
        for back in range(1, Y_RING + 1):
            @pl.when(total >= back)
            def _(back=back):
                out_copy(total - back, (total - back) % Y_RING).wait()


        def zero_wait(t, carry):
            zero_copy(t).wait()
            return carry

        lax.fori_loop(total, MOE_TILES, zero_wait, 0)


def _experts(l, tables, xloc, w_gate, w_up, w_down):
    grid_spec = pltpu.PrefetchScalarGridSpec(
        num_scalar_prefetch=len(tables),
        grid=(N_EXP,),
        in_specs=[pl.BlockSpec(memory_space=pl.ANY),
                  pl.BlockSpec((1, 1, D, D_EXP), lambda e, *_: (l, e, 0, 0)),
                  pl.BlockSpec((1, 1, D, D_EXP), lambda e, *_: (l, e, 0, 0)),
                  pl.BlockSpec((1, 1, D_EXP, D), lambda e, *_: (l, e, 0, 0))],
        out_specs=pl.BlockSpec(memory_space=pl.ANY),
        scratch_shapes=[pltpu.VMEM((X_RING, TM, D), BF16), pltpu.VMEM((Y_RING, TM, D), BF16),
                        pltpu.VMEM((TM, D), BF16),
                        pltpu.VMEM((D, 2 * D_EXP), BF16), pltpu.VMEM((D_EXP, D), BF16),
                        pltpu.SemaphoreType.DMA((X_RING,)), pltpu.SemaphoreType.DMA((Y_RING,)),
                        pltpu.SemaphoreType.DMA(())],
    )
    return pl.pallas_call(
        _expert_kernel,
        out_shape=jax.ShapeDtypeStruct((MOE_ROWS, D), BF16),
        grid_spec=grid_spec,
        compiler_params=_cparams("arbitrary"),
        name="experts",
    )(*tables, xloc, w_gate, w_up, w_down)


def _combine_kernel(nblk_ref, gsrc_ref, y_hbm, route_ref, x1_ref, mod_ref, g_ref, b_ref, oc_ref, ol_ref,
                    ybuf, sem, *, layer):
    i = pl.program_id(0)

    slot_i = i % 2

    def block_copy(t, s, lb):
        src = pl.multiple_of(gsrc_ref[t * LOC_BLKS + lb] * BLK, BLK)
        return pltpu.make_async_copy(y_hbm.at[pl.ds(src, BLK), :],
                                     ybuf.at[s, pl.ds(pl.multiple_of(lb * BLK, BLK), BLK), :], sem.at[s])

    def fetch(t, s):
        def start(lb, c):
            block_copy(t, s, lb).start()
            return c
        lax.fori_loop(0, nblk_ref[t], start, 0)

    @pl.when(i == 0)
    def _():
        ybuf[...] = jnp.zeros_like(ybuf)
        fetch(0, 0)

    @pl.when(i + 1 < N_TILES)
    def _():
        fetch(jnp.minimum(i + 1, N_TILES - 1), 1 - slot_i)

    def wait(lb, c):
        block_copy(i, slot_i, lb).wait()
        return c

    lax.fori_loop(0, nblk_ref[i], wait, 0)
    route = route_ref[...]
    yb = ybuf[slot_i]
    slot = lax.broadcasted_iota(jnp.int32, (TM, LOC), 1)
    pick = (jnp.where(slot == route[:, 4:5].astype(jnp.int32), route[:, 2:3], 0.0)
            + jnp.where(slot == route[:, 5:6].astype(jnp.int32), route[:, 3:4], 0.0)).astype(BF16)
    moe = jnp.dot(pick, yb, preferred_element_type=F32)
    m = mod_ref[0]
    res = _layer_norm(ALPHA * x1_ref[...] + m[5:6] * moe, g_ref[layer:layer + 1, :], b_ref[layer:layer + 1, :])

    @pl.when(i < CTX_TILES)
    def _():
        oc_ref[...] = res

    @pl.when(i >= CTX_TILES)
    def _():
        ol_ref[...] = res


def _combine(l, nblk, gsrc, y, route, x1, mod_l, g, b):
    out_shape = (jax.ShapeDtypeStruct((T_CTX, D), F32), jax.ShapeDtypeStruct((T_LAT, D), F32))
    out_specs = tuple(_pair_specs(D))
    grid_spec = pltpu.PrefetchScalarGridSpec(
        num_scalar_prefetch=2,
        grid=(N_TILES,),
        in_specs=[pl.BlockSpec(memory_space=pl.ANY),
                  pl.BlockSpec((TM, 128), lambda i, nb, gs: (i, 0)),
                  pl.BlockSpec((TM, D), lambda i, nb, gs: (i, 0)),
                  _mod_spec(l), _full((DEPTH, D)), _full((DEPTH, D))],
        out_specs=out_specs,
        scratch_shapes=[pltpu.VMEM((2, LOC, D), BF16), pltpu.SemaphoreType.DMA((2,))],
    )
    return pl.pallas_call(
        functools.partial(_combine_kernel, layer=l),
        out_shape=out_shape,
        grid_spec=grid_spec,
        compiler_params=_cparams("arbitrary"),
        name="moe_combine",
    )(nblk, gsrc, y, route, x1, mod_l, g, b)


def _prefix_pick(starts, query, table):
    delta = table - jnp.concatenate([jnp.zeros_like(table[..., :1]), table[..., :-1]], axis=-1)
    return jnp.sum(jnp.where(starts <= query, delta, 0), axis=-1)


def _dispatch_tables(cnt):
    i32 = jnp.int32
    run = (cnt + BLK - 1) // BLK * BLK
    loc_start = jnp.cumsum(run, axis=1) - run
    nblk_loc = jnp.sum(run, axis=1) // BLK
    seg_rows = jnp.sum(run, axis=0)
    seg_tiles = (seg_rows + TM - 1) // TM
    tile_end = jnp.cumsum(seg_tiles)
    seg_start = (tile_end - seg_tiles) * TM
    glob_start = seg_start[None, :] + jnp.cumsum(run, axis=0) - run

    g_blk = (glob_start.T.reshape(1, -1)) // BLK
    n_blk = (run.T.reshape(1, -1)) // BLK
    l_blk = ((jnp.arange(N_TILES, dtype=i32) * LOC)[None, :] + loc_start.T).reshape(1, -1) // BLK
    gb = jnp.arange(MOE_ROWS // BLK, dtype=i32)[:, None]
    off = gb[:, 0] - _prefix_pick(g_blk, gb, g_blk)
    src_blk = jnp.where(off < _prefix_pick(g_blk, gb, n_blk), _prefix_pick(g_blk, gb, l_blk) + off,
                        ZERO_BLK).astype(i32)

    pos = (jnp.arange(LOC_BLKS, dtype=i32) * BLK)[None, :, None]
    shift = _prefix_pick(loc_start[:, None, :], pos, (glob_start - loc_start)[:, None, :])
    gsrc = ((pos[:, :, 0] + shift) // BLK).astype(i32)
    expert_tables = ((tile_end - seg_tiles).astype(i32), seg_tiles.astype(i32), src_blk)
    return expert_tables, nblk_loc.astype(i32), gsrc.reshape(-1)


def _rope_slot(pe):
    return jnp.pad(pe, [(0, 0)] * (pe.ndim - 1) + [(NOPE, SLOT - NOPE - ROPE)])


def _wuq_layout(w):
    w = w.reshape(DEPTH, Q_LORA, HEADS, NOPE + ROPE)
    w = jnp.pad(w, ((0, 0), (0, 0), (0, 0), (0, SLOT - NOPE - ROPE)))
    return w.reshape(DEPTH, Q_LORA, HEADS * SLOT).astype(BF16)


def _wukv_layout(w):
    w = w.reshape(DEPTH, KV_LORA, HEADS, NOPE + VD)
    wk = jnp.pad(w[..., :NOPE], ((0, 0), (0, 0), (0, 0), (0, SLOT - NOPE)))
    wv = w[..., NOPE:].reshape(DEPTH, KV_LORA, HEADS * VD).astype(BF16)
    return wk.reshape(DEPTH, KV_LORA, HEADS * SLOT).astype(BF16), wv, jnp.swapaxes(wv, 1, 2)


def kernel(x_prompt, x_sample, c, cache_ckv, cache_kpe, c_ctx, ln_in_g, ln_in_b, w_mod, b_mod, w_in,
           gm_ln_g, gm_ln_b, gm_ws, gm_bs, hy_conv_w, hy_conv_b, hy_f_w1, hy_f_b1, hy_f_w2, hy_f_b2,
           hy_f_w3, hy_f_freq, hy_bias, mla_gq, mla_gkv, mla_wuq, mla_wukv, w_out, ln1_g, ln1_b,
           ln2_g, ln2_b, moe_w_gr, moe_b_gr, moe_w_er, moe_b_er, moe_w_gate, moe_w_up, moe_w_down):
    rope = _rope_tables()
    dft = {L: _dft_tables(L) for L in (CTX_L, LAT_L)}
    pos_tab = {L: _hyena_positions(L) for L in (CTX_L, LAT_L)}
    hd = np.arange(GM_W) // GM_HD
    avg = jnp.asarray((hd[:, None] == hd[None, :]) / GM_HD, BF16)
    tri = jnp.asarray(np.tril(np.ones((TM, TM)), -1), BF16)
    upper = jnp.asarray(np.triu(np.ones((128, 128)), 1), BF16)

    cond8 = jnp.concatenate([c_ctx[None], c, jnp.zeros((8 - 1 - N_LAT_B, D), F32)], axis=0)
    mod = _modulation(cond8, w_mod, b_mod).reshape(DEPTH, 8, 6, D)

    wuk_all, wuv_all, wuvt_all = _wukv_layout(mla_wukv)
    w_in_t = jnp.swapaxes(w_in, 1, 2)
    wuq_arr = _wuq_layout(mla_wuq)
    ws_b = gm_ws.astype(BF16)
    kc_all, vc_all = _cache_kv(cache_ckv, _rope_slot(cache_kpe), wuk_all, wuvt_all)

    x_pair = (x_prompt.reshape(T_CTX, D), x_sample.reshape(T_LAT, D))
    norm = (ln_in_g.reshape(1, D), ln_in_b.reshape(1, D))
    bs_full = jnp.repeat(jnp.swapaxes(gm_bs, 1, 2), GM_HD, axis=2)
    w_route = jnp.pad(jnp.concatenate([moe_w_gr, moe_w_er.reshape(DEPTH, D, N_EXP)], axis=2),
                      ((0, 0), (0, 0), (0, 128 - N_GROUPS - N_EXP)))
    b_route = jnp.pad(jnp.concatenate([moe_b_gr, moe_b_er.reshape(DEPTH, N_EXP)], axis=1),
                      ((0, 0), (0, 128 - N_GROUPS - N_EXP)))
    hyena_groups = ((CTX_L, N_CTX_B, 4, 0), (LAT_L, N_LAT_B, 2, T_CTX // LAT_L))
    spectra = {L: _filter_spectra(L, *pos_tab[L], dft[L], hy_f_w1, hy_f_b1, hy_f_w2, hy_f_b2, hy_f_w3,
                                  hy_f_freq) for L, _, _, _ in hyena_groups}
    ckv_states, kpe_states = [], []
    for l in range(DEPTH):
        a, hy, q, k, v, vt, ckv, kpe = _in_proj(
            l, x_pair, norm, mod, w_in_t, gm_ln_g, gm_ln_b, avg,
            ws_b, bs_full, mla_gq, mla_gkv, wuq_arr, wuk_all, wuv_all, wuvt_all, rope)
        ckv_states.append(ckv[:T_CTX].reshape(N_CTX_B, CTX_L, KV_LORA))
        kpe_states.append(kpe[:T_CTX, NOPE:NOPE + ROPE].reshape(N_CTX_B, CTX_L, ROPE))

        hyb = [_hyena_group(l, hy, L, nb, group, blk0, dft[L], spectra[L], hy_conv_w, hy_conv_b, hy_bias)
               for L, nb, group, blk0 in hyena_groups]

        att = _attention(l, q, k, v, vt, kc_all, vc_all)

        x1, xloc, route, counts = _out_proj(l, a, hyb, att, w_out, x_pair, norm, mod,
                                            ln1_g, ln1_b, w_route, b_route, tri, upper)
        cnt = counts.reshape(N_TILES, 8, 128)[:, 0, :N_EXP].astype(jnp.int32)
        expert_tables, nblk_loc, gsrc = _dispatch_tables(cnt)
        y = _experts(l, expert_tables, xloc, moe_w_gate, moe_w_up, moe_w_down)
        x_pair = _combine(l, nblk_loc, gsrc, y, route, x1, mod, ln2_g, ln2_b)

    y_prompt = x_pair[0].reshape(N_CTX_B, CTX_L, D)
    y_sample = x_pair[1].reshape(N_LAT_B, LAT_L, D)
    return (y_prompt, y_sample, jnp.stack(ckv_states, axis=1), jnp.stack(kpe_states, axis=1))
```

```python
import functools
import math

import numpy as np
import jax
import jax.numpy as jnp
from jax import lax
from jax.experimental import pallas as pl
from jax.experimental.pallas import tpu as pltpu

D = 1024
N_CTX_B, CTX_L = 16, 256
N_LAT_B, LAT_L = 4, 1024
DEPTH = 2
T_CTX = N_CTX_B * CTX_L
T_LAT = N_LAT_B * LAT_L
T = T_CTX + T_LAT
PAST = 256
GRID_W = 64

GM_HEADS, GM_HD, GM_W, CHUNK = 4, 64, 256, 128
HY_W, HY_EMB, HY_FH = 256, 33, 64
NOPE, ROPE, VD, HEADS = 64, 32, 64, 8
Q_LORA, KV_LORA = 256, 128
SLOT = 128
N_GROUPS, EPG, N_EXP, D_EXP = 4, 8, 32, 256
ALPHA = (2.0 * DEPTH) ** 0.25
LN_EPS, RMS_EPS = 1e-5, 1e-6
ROPE_THETA = 10000.0

TM = 256
N_TILES = T // TM
CTX_TILES = T_CTX // TM
LAT_TILES_PER_B = LAT_L // TM
N_ASSIGN = 2 * T
BLK = 16
BLK_PER_TILE = TM // BLK
LOC = 2 * TM + N_EXP * BLK
LOC_BLKS = LOC // BLK
ZERO_BLK = LOC_BLKS - 1
MOE_TILES = (N_ASSIGN + N_TILES * N_EXP * (BLK - 1)) // TM + N_EXP
MOE_ROWS = MOE_TILES * TM
IN_COLS = 1696
IN_EXT = 1792

F32, BF16 = jnp.float32, jnp.bfloat16
_NT = (((1,), (1,)), ((), ()))
VMEM_LIMIT = 52 * 1024 * 1024


def _cparams(*sem):
    return pltpu.CompilerParams(dimension_semantics=sem, vmem_limit_bytes=VMEM_LIMIT)


def _cond_of_tile(i):
    return jnp.where(i < CTX_TILES, 0, 1 + (i - CTX_TILES) // LAT_TILES_PER_B)


def _rope_block_of_tile(i):
    return jnp.where(i < CTX_TILES, 0, 1 + (i - CTX_TILES) % LAT_TILES_PER_B)


def _full(shape):
    n = len(shape)
    return pl.BlockSpec(shape, lambda *_: (0,) * n)


def _layer_spec(l, *shape):
    return pl.BlockSpec((None,) + shape, lambda *_: (l,) + (0,) * len(shape))


def _mod_spec(l):
    return pl.BlockSpec((None, 1, 6, D), lambda i, *_: (l, _cond_of_tile(i), 0, 0))


def _pair_specs(n):
    return [pl.BlockSpec((TM, n), lambda i, *_: (jnp.minimum(i, CTX_TILES - 1), 0)),
            pl.BlockSpec((TM, n), lambda i, *_: (jnp.maximum(i - CTX_TILES, 0), 0))]


def _pair_read(c_ref, l_ref):
    return jnp.where(pl.program_id(0) < CTX_TILES, c_ref[...], l_ref[...])


def _layer_norm(x, g, b):
    mu = jnp.mean(x, -1, keepdims=True)
    xc = x - mu
    var = jnp.mean(xc * xc, -1, keepdims=True)
    return xc * lax.rsqrt(var + LN_EPS) * g + b


def _rms_norm(x, g):
    return x * lax.rsqrt(jnp.mean(x * x, -1, keepdims=True) + RMS_EPS) * g


def _bdot(a, b):
    return jnp.dot(a.astype(BF16), b.astype(BF16), preferred_element_type=F32)


def _dot3(x, w):
    x_hi = x.astype(BF16)
    x_lo = (x - x_hi.astype(F32)).astype(BF16)
    w_hi = w.astype(BF16)
    w_lo = (w - w_hi.astype(F32)).astype(BF16)
    dot = functools.partial(jnp.dot, preferred_element_type=F32)
    return dot(x_hi, w_hi) + dot(x_hi, w_lo) + dot(x_lo, w_hi)


def _swap_pair_lanes(x):
    n = x.shape[1]
    lane = lax.broadcasted_iota(jnp.int32, x.shape, 1)
    return jnp.where(lane % 2 == 0, pltpu.roll(x, n - 1, 1), pltpu.roll(x, 1, 1))


def _split2_dot(x, w_bf16):
    hi = x.astype(BF16)
    lo = (x - hi.astype(F32)).astype(BF16)
    dot = functools.partial(jnp.dot, preferred_element_type=F32)
    return dot(hi, w_bf16) + dot(lo, w_bf16)


def _dft_tables(L):
    f = np.arange(L, dtype=np.int64)
    ft = np.outer(f, f) % (2 * L)
    ang = np.pi * ft / L
    c = np.cos(ang)
    s = np.sin(ang)
    alt = np.where(f % 2 == 0, 1.0, -1.0)
    sf = s.copy()
    sf[0, :] = alt
    return (jnp.asarray(c, F32).astype(BF16), jnp.asarray(sf, F32).astype(BF16),
            jnp.asarray(sf.T.copy(), F32).astype(BF16), jnp.asarray(alt[:, None], F32))


def _rope_tables():
    rows = LAT_L // GRID_W
    row = np.repeat(np.arange(rows, dtype=np.float64), GRID_W)
    col = np.tile(np.arange(GRID_W, dtype=np.float64), rows)
    n = ROPE // 4
    inv = ROPE_THETA ** (-np.arange(n, dtype=np.float64) / n)
    ang = np.concatenate([row[:, None] * inv, col[:, None] * inv], -1)
    cos = np.zeros((CTX_L + LAT_L, SLOT))
    sin = np.zeros((CTX_L + LAT_L, SLOT))
    cos[:, :NOPE + ROPE] = 1.0
    cos[CTX_L:, NOPE:NOPE + ROPE:2] = np.cos(ang)
    cos[CTX_L:, NOPE + 1:NOPE + ROPE:2] = np.cos(ang)
    sin[CTX_L:, NOPE:NOPE + ROPE:2] = -np.sin(ang)
    sin[CTX_L:, NOPE + 1:NOPE + ROPE:2] = np.sin(ang)
    scale = math.log2(math.e) / math.sqrt(NOPE + ROPE)
    return (jnp.asarray(cos * scale, F32), jnp.asarray(sin * scale, F32),
            jnp.asarray(cos, F32), jnp.asarray(sin, F32))


def _hyena_positions(L):
    t = jnp.linspace(0.0, 1.0, L, dtype=F32)[:, None]
    bands = (HY_EMB - 1) // 2
    w = 2.0 * math.pi * jnp.arange(L, dtype=F32)[:, None] / L
    f = jnp.linspace(1e-4, bands - 1, bands, dtype=F32)[None]
    z = jnp.concatenate([t, jnp.cos(f * w), -jnp.sin(f * w)], -1)
    z = jnp.pad(z, ((0, 0), (0, 128 - HY_EMB)))
    min_decay = math.log(1e-2) / 1.5
    max_decay = math.log(1e-2) / 0.3
    deltas = jnp.abs(jnp.linspace(min_decay, max_decay, HY_W, dtype=F32))
    return z, jnp.exp(-t * deltas)


def _mod_kernel(c_ref, w_ref, b_ref, o_ref):
    c = c_ref[...]
    s = c / (1.0 + jnp.exp(-c))
    o_ref[0] = _bdot(s, w_ref[0]) + b_ref[0]


def _modulation(cond8, w_mod, b_mod):
    tn = 1536
    return pl.pallas_call(
        _mod_kernel,
        out_shape=jax.ShapeDtypeStruct((DEPTH, 8, 6 * D), F32),
        grid=(DEPTH, 6 * D // tn),
        in_specs=[pl.BlockSpec((8, D), lambda l, j: (0, 0)),
                  pl.BlockSpec((1, D, tn), lambda l, j: (l, 0, j)),
                  pl.BlockSpec((1, 1, tn), lambda l, j: (l, 0, j))],
        out_specs=pl.BlockSpec((1, 8, tn), lambda l, j: (l, 0, j)),
        compiler_params=_cparams("parallel", "parallel"),
        name="modulation",
    )(cond8, w_mod, b_mod.reshape(DEPTH, 1, 6 * D))


def _filter_kernel(z_ref, w1_ref, b1_ref, w2_ref, b2_ref, w3_ref, fr_ref, dec_ref, c_ref, sf_ref,
                   alt_ref, kr_ref, ki_ref, krn_ref, *, L):
    h1 = jnp.sin(fr_ref[0:1] * (_dot3(z_ref[...], w1_ref[...]) + b1_ref[...]))
    h2 = jnp.sin(fr_ref[1:2] * (_dot3(h1, w2_ref[...]) + b2_ref[...]))
    filt = _dot3(h2, w3_ref[...])
    row = lax.broadcasted_iota(jnp.int32, (L, HY_W), 0)
    dec = dec_ref[...]
    n = DEPTH * 2
    fwd = [filt[:, j * 2 * HY_W:j * 2 * HY_W + HY_W] * dec for j in range(n)]
    bwd = [jnp.where(row == 0, 0.0, filt[:, j * 2 * HY_W + HY_W:(j + 1) * 2 * HY_W] * dec) for j in range(n)]
    sm = [f + b for f, b in zip(fwd, bwd)]
    kr_all = _bdot(c_ref[...], jnp.concatenate(sm, axis=1))
    ki_all = -_bdot(sf_ref[...], jnp.concatenate([f - b for f, b in zip(fwd, bwd)], axis=1))
    for j in range(n):
        l, o = divmod(j, 2)
        kr = kr_all[:, j * HY_W:(j + 1) * HY_W]
        kr_nyq = jnp.sum(alt_ref[...] * sm[j], axis=0, keepdims=True)
        kr_ref[l, o] = jnp.where(row == 0, kr * (0.5 / L), kr * (1.0 / L))
        ki_ref[l, o] = jnp.where(row == 0, 0.0, ki_all[:, j * HY_W:(j + 1) * HY_W] * (1.0 / L))
        krn_ref[l, o] = jnp.where(row == 0, kr_nyq * (0.5 / L), kr * (1.0 / L))


def _block_diag(blocks):
    rows = []
    for i, blk in enumerate(blocks):
        rows.append(jnp.concatenate([blk if j == i else jnp.zeros((blk.shape[0], other.shape[1]), blk.dtype)
                                     for j, other in enumerate(blocks)], axis=1))
    return jnp.concatenate(rows, axis=0)


def _filter_spectra(L, z, dec, dft, w1, b1, w2, b2, w3, freq):
    assert DEPTH * HY_FH == 128
    c, sf, _, alt = dft
    layers = range(DEPTH)
    w1c = jnp.pad(jnp.concatenate([w1[l] for l in layers], axis=1), ((0, 128 - HY_EMB), (0, 0)))
    w2c = _block_diag([w2[l] for l in layers])
    w3c = _block_diag([w3[l] for l in layers])
    b1c, b2c = b1.reshape(1, 128), b2.reshape(1, 128)
    frc = jnp.swapaxes(freq, 0, 1).reshape(2, 128)
    shp = jax.ShapeDtypeStruct((DEPTH, 2, L, HY_W), F32)
    args = (z, w1c, b1c, w2c, b2c, w3c, frc, dec, c, sf, alt)
    return pl.pallas_call(
        functools.partial(_filter_kernel, L=L),
        out_shape=(shp, shp, shp),
        grid=(1,),
        in_specs=[_full(a.shape) for a in args],
        out_specs=(_full((DEPTH, 2, L, HY_W)),) * 3,
        compiler_params=_cparams("arbitrary"),
        name=f"hyena_filter_{L}",
    )(*args)


IN_G = 2


def _in_kernel(xc_ref, xl_ref, ng_ref, nb_ref, mod_ref, wt_ref, gg_ref, gb_ref, avg_ref, ws_ref, bs_ref,
               gq_ref, gkv_ref, wuq_ref, wuk_ref, wuv_ref, wuvt_ref, *rest, layer):
    rope_refs = rest[:4 * IN_G]
    a_ref, hy_ref, q_ref, k_ref, v_ref, vt_ref, ckv_ref, kpe_ref, w_b = rest[4 * IN_G:]
    l = layer

    @pl.when(pl.program_id(0) == 0)
    def _():
        for c0 in range(0, 1664, 128):
            w_b[:, c0:c0 + 128] = jnp.transpose(wt_ref[c0:c0 + 128, :]).astype(BF16)
        slot_rows = jnp.concatenate([jnp.zeros((NOPE, D), F32), wt_ref[1664:IN_COLS, :],
                                     jnp.zeros((SLOT - NOPE - ROPE, D), F32)], axis=0)
        w_b[:, 1664:IN_EXT] = jnp.transpose(slot_rows).astype(BF16)

    is_ctx = pl.program_id(0) < CTX_TILES // IN_G
    m = mod_ref[0]
    avg = avg_ref[...]
    lane = lax.broadcasted_iota(jnp.int32, (CHUNK, GM_W), 1)

    def front(k):
        rows = pl.ds(k * TM, TM)
        x = jnp.where(is_ctx, xc_ref[rows, :], xl_ref[rows, :])
        if l == 0:
            x = _layer_norm(x, ng_ref[...], nb_ref[...])
        h = x * (1.0 + m[1:2]) + m[0:1]
        return _bdot(h, w_b[...])

    def back(k, proj):
        r0 = k * TM
        cq_ref, sq_ref, ck_ref, sk_ref = rope_refs[4 * k:4 * k + 4]
        gu = jax.nn.gelu(proj[:, 0:GM_W], approximate=True)
        gv = jax.nn.gelu(proj[:, GM_W:2 * GM_W], approximate=True)
        mu = _split2_dot(gv, avg)
        vc = gv - mu
        var = _split2_dot(vc * vc, avg)
        vln = (vc * lax.rsqrt(var + LN_EPS) * gg_ref[l:l + 1, :] + gb_ref[l:l + 1, :]).astype(BF16)
        for c in range(TM // CHUNK):
            vchunk = vln[c * CHUNK:(c + 1) * CHUNK]
            s = bs_ref[...]
            for hd in range(GM_HEADS):
                sh = jnp.dot(ws_ref[0, hd], vchunk, preferred_element_type=F32)
                s = s + jnp.where(lane // GM_HD == hd, sh, 0.0)
            a_ref[r0 + c * CHUNK:r0 + (c + 1) * CHUNK, :] = (gu[c * CHUNK:(c + 1) * CHUNK] * s).astype(BF16)

        hy_ref[r0:r0 + TM, :] = proj[:, 512:1280]

        cq = _rms_norm(proj[:, 1280:1536], gq_ref[l:l + 1, :])
        qq = _bdot(cq, wuq_ref[0])
        cos_q = jnp.concatenate([cq_ref[...]] * HEADS, axis=1)
        sin_q = jnp.concatenate([sq_ref[...]] * HEADS, axis=1)
        q_ref[r0:r0 + TM, :] = (qq * cos_q + _swap_pair_lanes(qq) * sin_q).astype(BF16)

        ckv = _rms_norm(proj[:, 1536:1664], gkv_ref[l:l + 1, :])
        ckv_ref[r0:r0 + TM, :] = ckv
        kpe = proj[:, 1664:IN_EXT]
        kpe_ref[r0:r0 + TM, :] = kpe
        krot = kpe * ck_ref[...] + _swap_pair_lanes(kpe) * sk_ref[...]
        kn = _bdot(ckv, wuk_ref[0])
        k_ref[r0:r0 + TM, :] = (kn + jnp.concatenate([krot] * HEADS, axis=1)).astype(BF16)
        v_ref[r0:r0 + TM, :] = _bdot(ckv, wuv_ref[0]).astype(BF16)
        vt_ref[:, r0:r0 + TM] = lax.dot_general(wuvt_ref[0], ckv.astype(BF16), _NT,
                                                preferred_element_type=F32).astype(BF16)

    projs = [front(k) for k in range(IN_G)]
    for k, proj in enumerate(projs):
        back(k, proj)


def _in_proj(l, x_pair, norm, mod_l, w_in_t, gm_g, gm_b, avg, ws, bs_full, gq, gkv, wuq_arr,
             wuk_arr, wuv_arr, wuvt_arr, rope):
    layer = lambda *shape: pl.BlockSpec((1,) + shape, lambda i: (l,) + (0,) * len(shape))
    rows = IN_G * TM
    tile = lambda n: pl.BlockSpec((rows, n), lambda i: (i, 0))
    pair = [pl.BlockSpec((rows, D), lambda i: (jnp.minimum(i, CTX_TILES // IN_G - 1), 0)),
            pl.BlockSpec((rows, D), lambda i: (jnp.maximum(i - CTX_TILES // IN_G, 0), 0))]
    rope_specs = [pl.BlockSpec((TM, SLOT), lambda i, k=k: (_rope_block_of_tile(i * IN_G + k), 0))
                  for k in range(IN_G) for _ in rope]
    out_shapes = (jax.ShapeDtypeStruct((T, GM_W), BF16),
                  jax.ShapeDtypeStruct((T, 3 * HY_W), F32),
                  jax.ShapeDtypeStruct((T, HEADS * SLOT), BF16),
                  jax.ShapeDtypeStruct((T, HEADS * SLOT), BF16),
                  jax.ShapeDtypeStruct((T, HEADS * VD), BF16),
                  jax.ShapeDtypeStruct((HEADS * VD, T), BF16),
                  jax.ShapeDtypeStruct((T, KV_LORA), F32),
                  jax.ShapeDtypeStruct((T, SLOT), F32))
    return pl.pallas_call(
        functools.partial(_in_kernel, layer=l),
        out_shape=out_shapes,
        grid=(N_TILES // IN_G,),
        in_specs=pair + [
                  _full((1, D)), _full((1, D)),
                  pl.BlockSpec((None, 1, 6, D), lambda i: (l, _cond_of_tile(i * IN_G), 0, 0)),
                  pl.BlockSpec((None, IN_COLS, D), lambda i: (l, 0, 0), pipeline_mode=pl.Buffered(1)),
                  _full((DEPTH, GM_W)), _full((DEPTH, GM_W)), _full((GM_W, GM_W)),
                  layer(GM_HEADS, CHUNK, CHUNK), _layer_spec(l, CHUNK, GM_W),
                  _full((DEPTH, Q_LORA)), _full((DEPTH, KV_LORA)),
                  layer(Q_LORA, HEADS * SLOT), layer(KV_LORA, HEADS * SLOT),
                  layer(KV_LORA, HEADS * VD), layer(HEADS * VD, KV_LORA)] + rope_specs,
        out_specs=(tile(GM_W), tile(3 * HY_W), tile(HEADS * SLOT), tile(HEADS * SLOT), tile(HEADS * VD),
                   pl.BlockSpec((HEADS * VD, rows), lambda i: (0, i)), tile(KV_LORA), tile(SLOT)),
        scratch_shapes=[pltpu.VMEM((D, IN_EXT), BF16)],
        compiler_params=_cparams("arbitrary"),
        name="in_proj",
    )(*x_pair, *norm, mod_l, w_in_t, gm_g, gm_b, avg, ws, bs_full, gq, gkv, wuq_arr, wuk_arr, wuv_arr,
      wuvt_arr, *(list(rope) * IN_G))


def _cache_kernel(ckv_ref, kpe_ref, wuk_ref, wuv_ref, k_ref, v_ref):
    ckv = ckv_ref[0, 0]
    kn = _bdot(ckv, wuk_ref[0])
    k_ref[0, 0] = (kn + jnp.concatenate([kpe_ref[0, 0]] * HEADS, axis=1)).astype(BF16)
    v_ref[0, 0] = lax.dot_general(wuv_ref[0], ckv.astype(BF16), _NT,
                                  preferred_element_type=F32).astype(BF16)


def _cache_kv(cache_ckv, cache_kpe_slot, wuk_arr, wuv_arr):
    return pl.pallas_call(
        _cache_kernel,
        out_shape=(jax.ShapeDtypeStruct((DEPTH, N_LAT_B, PAST, HEADS * SLOT), BF16),
                   jax.ShapeDtypeStruct((DEPTH, N_LAT_B, HEADS * VD, PAST), BF16)),
        grid=(DEPTH, N_LAT_B),
        in_specs=[pl.BlockSpec((1, 1, PAST, KV_LORA), lambda l, b: (b, l, 0, 0)),
                  pl.BlockSpec((1, 1, PAST, SLOT), lambda l, b: (b, l, 0, 0)),
                  pl.BlockSpec((1, KV_LORA, HEADS * SLOT), lambda l, b: (l, 0, 0)),
                  pl.BlockSpec((1, HEADS * VD, KV_LORA), lambda l, b: (l, 0, 0))],
        out_specs=(pl.BlockSpec((1, 1, PAST, HEADS * SLOT), lambda l, b: (l, b, 0, 0)),
                   pl.BlockSpec((1, 1, HEADS * VD, PAST), lambda l, b: (l, b, 0, 0))),
        compiler_params=_cparams("parallel", "parallel"),
        name="cache_kv",
    )(cache_ckv, cache_kpe_slot, wuk_arr, wuv_arr)


def _attend_rows(q_ref, k_ref, v_ref, o_ref):
    lq = q_ref.shape[0]
    lane = lax.broadcasted_iota(jnp.int32, (lq, 2 * VD), 1)

    def scores_of(hd):
        return lax.dot_general(q_ref[:, hd * SLOT:(hd + 1) * SLOT], k_ref[:, hd * SLOT:(hd + 1) * SLOT], _NT,
                               preferred_element_type=F32)

    outs = []
    nxt = scores_of(0)
    for hd in range(HEADS):
        s = nxt
        if hd + 1 < HEADS:
            nxt = scores_of(hd + 1)
        p = jnp.exp2(s - jnp.max(s, -1, keepdims=True))
        acc = jnp.dot(p.astype(BF16), v_ref[:, (hd // 2) * 2 * VD:(hd // 2 + 1) * 2 * VD],
                      preferred_element_type=F32)
        outs.append(acc / jnp.sum(p, -1, keepdims=True))
        if hd % 2 == 1:
            pair = hd // 2
            o_ref[:, pair * 2 * VD:(pair + 1) * 2 * VD] = jnp.where(lane < VD, outs[-2], outs[-1]).astype(BF16)


def _attend_cols(q_ref, segments, o_ref):
    lq = q_ref.shape[0]
    lane = lax.broadcasted_iota(jnp.int32, (lq, 2 * SLOT), 1)
    row = lax.broadcasted_iota(jnp.int32, (2 * VD, lq), 0)
    n_pairs = HEADS // 2

    def scores_of(pair):
        qp = q_ref[:, pair * 2 * SLOT:(pair + 1) * 2 * SLOT]
        zero = jnp.zeros_like(qp)
        q_bd = jnp.concatenate([jnp.where(lane < SLOT, qp, zero), jnp.where(lane < SLOT, zero, qp)], axis=0)
        return [lax.dot_general(k_ref[:, pair * 2 * SLOT:(pair + 1) * 2 * SLOT], q_bd, _NT,
                                preferred_element_type=F32) for k_ref, _ in segments]

    def head_pair_out(pair, scores):
        mx = functools.reduce(jnp.maximum, [jnp.max(s, 0, keepdims=True) for s in scores])
        ps = [jnp.exp2(s - mx) for s in scores]
        den = functools.reduce(jnp.add, [jnp.sum(p, 0, keepdims=True) for p in ps])
        acc = functools.reduce(jnp.add, [
            jnp.dot(vt_ref[pair * 2 * VD:(pair + 1) * 2 * VD, :], p.astype(BF16),
                    preferred_element_type=F32) for p, (_, vt_ref) in zip(ps, segments)])
        acc = acc / den
        return jnp.where(row < VD, acc[:, :lq], acc[:, lq:])

    outs = []
    nxt = scores_of(0)
    for pair in range(n_pairs):
        cur = nxt
        if pair + 1 < n_pairs:
            nxt = scores_of(pair + 1)
        outs.append(head_pair_out(pair, cur))
    o_ref[...] = jnp.transpose(jnp.concatenate(outs, axis=0)).astype(BF16)


def _attn_ctx_kernel(q_ref, k_ref, v_ref, o_ref):
    _attend_rows(q_ref, k_ref, v_ref, o_ref)


def _attn_lat_kernel(q_ref, kc_ref, vc_ref, k_ref, vt_ref, o_ref):
    _attend_cols(q_ref, [(kc_ref.at[0], vc_ref.at[0]), (k_ref, vt_ref)], o_ref)


def _attention(l, q, k, v, vt, kc, vc):
    kw, vw = HEADS * SLOT, HEADS * VD
    ctx = pl.pallas_call(
        _attn_ctx_kernel,
        out_shape=jax.ShapeDtypeStruct((T_CTX, vw), BF16),
        grid=(N_CTX_B,),
        in_specs=[pl.BlockSpec((CTX_L, kw), lambda b: (b, 0)),
                  pl.BlockSpec((CTX_L, kw), lambda b: (b, 0)),
                  pl.BlockSpec((CTX_L, vw), lambda b: (b, 0))],
        out_specs=pl.BlockSpec((CTX_L, vw), lambda b: (b, 0)),
        compiler_params=_cparams("parallel"),
        name="attn_ctx",
    )(q, k, v)
    tq = 2 * TM
    nq = LAT_L // tq
    off = T_CTX // LAT_L
    lat = pl.pallas_call(
        _attn_lat_kernel,
        out_shape=jax.ShapeDtypeStruct((T_LAT, vw), BF16),
        grid=(N_LAT_B, nq),
        in_specs=[pl.BlockSpec((tq, kw), lambda b, j: (T_CTX // tq + b * nq + j, 0)),
                  pl.BlockSpec((None, 1, PAST, kw), lambda b, j: (l, b, 0, 0)),
                  pl.BlockSpec((None, 1, vw, PAST), lambda b, j: (l, b, 0, 0)),
                  pl.BlockSpec((LAT_L, kw), lambda b, j: (off + b, 0)),
                  pl.BlockSpec((vw, LAT_L), lambda b, j: (0, off + b))],
        out_specs=pl.BlockSpec((tq, vw), lambda b, j: (b * nq + j, 0)),
        compiler_params=_cparams("parallel", "parallel"),
        name="attn_lat",
    )(q, kc, vc, k, vt)
    return ctx, lat


def _hyena_kernel(hy_ref, cw_ref, cb_ref, c_ref, sf_ref, sb_ref, kr_ref, ki_ref, krn_ref, hb_ref,
                  o_ref, *, L, group, layer):
    row = lax.broadcasted_iota(jnp.int32, (L, 3 * HY_W), 0)
    zs = []
    for j in range(group):
        x = hy_ref[j * L:(j + 1) * L, :]
        prev = jnp.where(row == 0, 0.0, pltpu.roll(x, 1, 0))
        nxt = jnp.where(row == L - 1, 0.0, pltpu.roll(x, L - 1, 0))
        zs.append(prev * cw_ref[0:1] + x * cw_ref[1:2] + nxt * cw_ref[2:3] + cb_ref[layer:layer + 1, :])
    side = lambda parts: jnp.concatenate(parts, axis=1)
    y = side([z[:, 2 * HY_W:] for z in zs])
    for o in range(2):
        gate = side([z[:, o * HY_W:(o + 1) * HY_W] for z in zs])
        kr, ki, krn = (side([r[o]] * group) for r in (kr_ref, ki_ref, krn_ref))
        yb = y.astype(BF16)
        a_re = jnp.dot(c_ref[...], yb, preferred_element_type=F32)
        a_im = jnp.dot(sf_ref[...], yb, preferred_element_type=F32)
        z_re = a_re * kr + a_im * ki
        z_im = a_im * krn - a_re * ki
        conv = (jnp.dot(c_ref[...], z_re.astype(BF16), preferred_element_type=F32)
                + jnp.dot(sb_ref[...], z_im.astype(BF16), preferred_element_type=F32))
        y = gate * (conv + y * side([hb_ref[o:o + 1]] * group))
    for j in range(group):
        o_ref[j * L:(j + 1) * L, :] = y[:, j * HY_W:(j + 1) * HY_W].astype(BF16)


def _hyena_group(l, hy, L, nb, group, blk0, dft, spectra, conv_w, conv_b, hy_bias):
    c, sf, sb, _ = dft
    kr, ki, krn = spectra
    const = lambda shape: pl.BlockSpec(shape, lambda b: (0,) * len(shape), pipeline_mode=pl.Buffered(1))
    spec = pl.BlockSpec((None, 2, L, HY_W), lambda b: (l, 0, 0, 0), pipeline_mode=pl.Buffered(1))
    return pl.pallas_call(
        functools.partial(_hyena_kernel, L=L, group=group, layer=l),
        out_shape=jax.ShapeDtypeStruct((nb * L, HY_W), BF16),
        grid=(nb // group,),
        in_specs=[pl.BlockSpec((group * L, 3 * HY_W), lambda b: (blk0 // group + b, 0)),
                  _layer_spec(l, 3, 3 * HY_W), _full((DEPTH, 3 * HY_W)),
                  const((L, L)), const((L, L)), const((L, L)), spec, spec, spec,
                  _layer_spec(l, 2, HY_W)],
        out_specs=pl.BlockSpec((group * L, HY_W), lambda b: (b, 0)),
        compiler_params=_cparams("parallel"),
        name=f"hyena_{L}",
    )(hy, conv_w, conv_b, c, sf, sb, kr, ki, krn, hy_bias)


OUT_G = 2


def _out_kernel(a_ref, bc_ref, bl_ref, mc_ref, ml_ref, w_ref, xc_ref, xl_ref, ng_ref, nb_ref, mod_ref,
                g_ref, be_ref, wr_ref, br_ref, tri_ref, upper_ref, x1_ref, xloc_ref, route_ref, cnt_ref,
                w_b, *, layer):
    l = layer

    @pl.when(pl.program_id(0) == 0)
    def _():
        w_b[...] = w_ref[...].astype(BF16)

    dot = functools.partial(jnp.dot, preferred_element_type=F32)
    is_ctx = pl.program_id(0) < CTX_TILES // OUT_G
    w_hi = wr_ref[...].astype(BF16)
    w_lo = (wr_ref[...] - w_hi.astype(F32)).astype(BF16)
    w_cat = jnp.concatenate([w_hi, w_lo], axis=1)
    lane = lax.broadcasted_iota(jnp.int32, (TM, 128), 1)
    lanef = lane.astype(F32)
    big = jnp.float32(1e9)
    ninf = jnp.float32(-jnp.inf)
    is_g = lane < N_GROUPS
    ex = lane - N_GROUPS
    m = mod_ref[0]

    def front(k):
        rows = pl.ds(k * TM, TM)
        pair = lambda c_ref, l_ref: jnp.where(is_ctx, c_ref[rows, :], l_ref[rows, :])
        mixed = (dot(a_ref[rows, :], w_b[0:GM_W]) + dot(pair(bc_ref, bl_ref), w_b[GM_W:GM_W + HY_W])
                 + dot(pair(mc_ref, ml_ref), w_b[GM_W + HY_W:]))
        x = pair(xc_ref, xl_ref)
        if l == 0:
            x = _layer_norm(x, ng_ref[...], nb_ref[...])
        x1 = _layer_norm(ALPHA * x + m[2:3] * mixed, g_ref[l:l + 1, :], be_ref[l:l + 1, :])
        x1_ref[rows, :] = x1
        h2 = x1 * (1.0 + m[4:5]) + m[3:4]
        h_hi = h2.astype(BF16)
        h_lo = (h2 - h_hi.astype(F32)).astype(BF16)
        part = dot(h_hi, w_cat)
        logits = part[:, :128] + part[:, 128:] + dot(h_lo, w_hi) + br_ref[l:l + 1, :]
        return h_hi, logits

    def back(k, h_hi, logits):
        gl = jnp.where(is_g, logits, ninf)
        gmax = jnp.max(gl, -1, keepdims=True)
        gidx = jnp.min(jnp.where(gl == gmax, lanef, big), -1, keepdims=True)
        gw = 1.0 / jnp.sum(jnp.where(is_g, jnp.exp(logits - gmax), 0.0), -1, keepdims=True)
        in_group = (ex >= 0) & (ex < N_EXP) & ((ex // EPG).astype(F32) == gidx)
        el = jnp.where(in_group, logits, ninf)
        v1 = jnp.max(el, -1, keepdims=True)
        i1 = jnp.min(jnp.where(el == v1, lanef, big), -1, keepdims=True)
        el2 = jnp.where(lanef == i1, ninf, el)
        v2 = jnp.max(el2, -1, keepdims=True)
        i2 = jnp.min(jnp.where(el2 == v2, lanef, big), -1, keepdims=True)
        e21 = jnp.exp(v2 - v1)
        w1 = gw / (1.0 + e21)
        w2 = gw * e21 / (1.0 + e21)
        e1 = i1 - N_GROUPS
        e2 = i2 - N_GROUPS

        oh1 = jnp.where(lanef == e1, 1.0, 0.0)
        oh2 = jnp.where(lanef == e2, 1.0, 0.0)
        ex1 = dot(tri_ref[...], oh1.astype(BF16))
        ex2 = dot(tri_ref[...], oh2.astype(BF16))
        col1 = jnp.sum(oh1, axis=0, keepdims=True)
        col2 = jnp.sum(oh2, axis=0, keepdims=True)
        n = col1 + col2
        run = jnp.floor((n + (BLK - 1.0)) * (1.0 / BLK)) * BLK
        start = dot(jnp.broadcast_to(run, (8, 128)).astype(BF16), upper_ref[...])[0:1]
        loc1 = jnp.sum(oh1 * (start + ex1), -1, keepdims=True)
        loc2 = jnp.sum(oh2 * (start + col1 + ex2), -1, keepdims=True)
        cnt_ref[k * 8:(k + 1) * 8, :] = jnp.broadcast_to(n, (8, 128))

        route = jnp.zeros((TM, 128), F32)
        for j, val in enumerate((e1, e2, w1, w2, loc1, loc2)):
            route = jnp.where(lane == j, val, route)
        route_ref[k * TM:(k + 1) * TM, :] = route

        loc1_row = jnp.transpose(jnp.broadcast_to(loc1, (TM, 128)))[0:1].astype(jnp.int32)
        loc2_row = jnp.transpose(jnp.broadcast_to(loc2, (TM, 128)))[0:1].astype(jnp.int32)
        slot = lax.broadcasted_iota(jnp.int32, (LOC, TM), 0)
        perm = jnp.where((slot == loc1_row) | (slot == loc2_row), 1.0, 0.0).astype(BF16)
        xloc_ref[k * LOC:(k + 1) * LOC, :] = dot(perm, h_hi).astype(BF16)

    fronts = [front(k) for k in range(OUT_G)]
    for k, (h_hi, logits) in enumerate(fronts):
        back(k, h_hi, logits)


def _out_proj(l, a, hyb_pair, att_pair, w_out, x_pair, norm, mod_l, g, b, w_route, b_route, tri, upper):
    rows = OUT_G * TM
    tile = lambda n: pl.BlockSpec((rows, n), lambda i: (i, 0))
    pair = lambda n: [pl.BlockSpec((rows, n), lambda i: (jnp.minimum(i, CTX_TILES // OUT_G - 1), 0)),
                      pl.BlockSpec((rows, n), lambda i: (jnp.maximum(i - CTX_TILES // OUT_G, 0), 0))]
    return pl.pallas_call(
        functools.partial(_out_kernel, layer=l),
        out_shape=(jax.ShapeDtypeStruct((T, D), F32), jax.ShapeDtypeStruct((N_TILES * LOC, D), BF16),
                   jax.ShapeDtypeStruct((T, 128), F32), jax.ShapeDtypeStruct((N_TILES * 8, 128), F32)),
        grid=(N_TILES // OUT_G,),
        in_specs=[tile(GM_W)] + pair(HY_W) + pair(HEADS * VD)
        + [pl.BlockSpec((None, D, D), lambda i: (l, 0, 0), pipeline_mode=pl.Buffered(1))]
        + pair(D) + [
                  _full((1, D)), _full((1, D)),
                  pl.BlockSpec((None, 1, 6, D), lambda i: (l, _cond_of_tile(i * OUT_G), 0, 0)),
                  _full((DEPTH, D)), _full((DEPTH, D)), _layer_spec(l, D, 128), _full((DEPTH, 128)),
                  _full((TM, TM)), _full((128, 128))],
        out_specs=(tile(D), pl.BlockSpec((OUT_G * LOC, D), lambda i: (i, 0)), tile(128),
                   pl.BlockSpec((OUT_G * 8, 128), lambda i: (i, 0))),
        scratch_shapes=[pltpu.VMEM((D, D), BF16)],
        compiler_params=_cparams("arbitrary"),
        name="out_proj_route",
    )(a, *hyb_pair, *att_pair, w_out, *x_pair, *norm, mod_l, g, b, w_route, b_route, tri, upper)


Y_RING = 4
X_AHEAD = 6
X_RING = 8


def _expert_kernel(off_ref, cnt_ref, src_ref, x_hbm, wg_ref, wu_ref, wd_ref, y_hbm,
                   xbuf, ybuf, zbuf, wgu_b, wd_b, sem, osem, zsem):
    e = pl.program_id(0)
    first_tile = off_ref[e]
    n_tiles = cnt_ref[e]
    total = off_ref[N_EXP - 1] + cnt_ref[N_EXP - 1]

    def block_copies(t, s):
        return [pltpu.make_async_copy(
            x_hbm.at[pl.ds(pl.multiple_of(src_ref[t * BLK_PER_TILE + b] * BLK, BLK), BLK), :],
            xbuf.at[s, pl.ds(b * BLK, BLK), :], sem.at[s]) for b in range(BLK_PER_TILE)]

    def out_copy(t, s):
        return pltpu.make_async_copy(ybuf.at[s], y_hbm.at[pl.ds(pl.multiple_of(t * TM, TM), TM), :],
                                     osem.at[s])

    def zero_copy(t):
        return pltpu.make_async_copy(zbuf, y_hbm.at[pl.ds(pl.multiple_of(t * TM, TM), TM), :], zsem)

    @pl.when(e == 0)
    def _():
        zbuf[...] = jnp.zeros((TM, D), BF16)

        def zero_start(t, carry):
            zero_copy(t).start()
            return carry

        lax.fori_loop(total, MOE_TILES, zero_start, 0)
        for t0 in range(X_AHEAD):
            @pl.when(t0 < total)
            def _():
                for cp in block_copies(t0, t0):
                    cp.start()

    @pl.when(n_tiles > 0)
    def _():
        wgu_b[:, 0:D_EXP] = wg_ref[0, 0].astype(BF16)
        wgu_b[:, D_EXP:2 * D_EXP] = wu_ref[0, 0].astype(BF16)
        wd_b[...] = wd_ref[0, 0].astype(BF16)

    def process(tiles):
        for t in tiles:
            @pl.when(t + X_AHEAD < total)
            def _(t=t):
                for cp in block_copies(t + X_AHEAD, (t + X_AHEAD) % X_RING):
                    cp.start()

        for t in tiles:
            for cp in block_copies(t, t % X_RING):
                cp.wait()

            @pl.when(t >= Y_RING)
            def _(t=t):
                out_copy(t - Y_RING, t % Y_RING).wait()

        gus = [jnp.dot(xbuf[t % X_RING], wgu_b[...], preferred_element_type=F32) for t in tiles]
        for t, gu in zip(tiles, gus):
            gate, up = gu[:, :D_EXP], gu[:, D_EXP:]
            hid = gate / (1.0 + jnp.exp(-gate)) * up
            ybuf[t % Y_RING] = jnp.dot(hid.astype(BF16), wd_b[...], preferred_element_type=F32).astype(BF16)
            out_copy(t, t % Y_RING).start()

    def pair_body(p, carry):
        t = first_tile + 2 * p
        process([t, t + 1])
        return carry

    lax.fori_loop(0, n_tiles // 2, pair_body, 0)

    @pl.when(n_tiles % 2 == 1)
    def _():
        process([first_tile + n_tiles - 1])

    @pl.when(e == N_EXP - 1)
    def _():
        for back in range(1, Y_RING + 1):
            @pl.when(total >= back)
            def _(back=back):
                out_copy(total - back, (total - back) % Y_RING).wait()


        def zero_wait(t, carry):
            zero_copy(t).wait()
            return carry

        lax.fori_loop(total, MOE_TILES, zero_wait, 0)


def _experts(l, tables, xloc, w_gate, w_up, w_down):
    grid_spec = pltpu.PrefetchScalarGridSpec(
        num_scalar_prefetch=len(tables),
        grid=(N_EXP,),
        in_specs=[pl.BlockSpec(memory_space=pl.ANY),
                  pl.BlockSpec((1, 1, D, D_EXP), lambda e, *_: (l, e, 0, 0)),
                  pl.BlockSpec((1, 1, D, D_EXP), lambda e, *_: (l, e, 0, 0)),
                  pl.BlockSpec((1, 1, D_EXP, D), lambda e, *_: (l, e, 0, 0))],
        out_specs=pl.BlockSpec(memory_space=pl.ANY),
        scratch_shapes=[pltpu.VMEM((X_RING, TM, D), BF16), pltpu.VMEM((Y_RING, TM, D), BF16),
                        pltpu.VMEM((TM, D), BF16),
                        pltpu.VMEM((D, 2 * D_EXP), BF16), pltpu.VMEM((D_EXP, D), BF16),
                        pltpu.SemaphoreType.DMA((X_RING,)), pltpu.SemaphoreType.DMA((Y_RING,)),
                        pltpu.SemaphoreType.DMA(())],
    )
    return pl.pallas_call(
        _expert_kernel,
        out_shape=jax.ShapeDtypeStruct((MOE_ROWS, D), BF16),
        grid_spec=grid_spec,
        compiler_params=_cparams("arbitrary"),
        name="experts",
    )(*tables, xloc, w_gate, w_up, w_down)


def _combine_kernel(nblk_ref, gsrc_ref, y_hbm, route_ref, x1_ref, mod_ref, g_ref, b_ref, oc_ref, ol_ref,
                    ybuf, sem, *, layer):
    i = pl.program_id(0)

    slot_i = i % 2

    def block_copy(t, s, lb):
        src = pl.multiple_of(gsrc_ref[t * LOC_BLKS + lb] * BLK, BLK)
        return pltpu.make_async_copy(y_hbm.at[pl.ds(src, BLK), :],
                                     ybuf.at[s, pl.ds(pl.multiple_of(lb * BLK, BLK), BLK), :], sem.at[s])

    def fetch(t, s):
        def start(lb, c):
            block_copy(t, s, lb).start()
            return c
        lax.fori_loop(0, nblk_ref[t], start, 0)

    @pl.when(i == 0)
    def _():
        ybuf[...] = jnp.zeros_like(ybuf)
        fetch(0, 0)

    @pl.when(i + 1 < N_TILES)
    def _():
        fetch(jnp.minimum(i + 1, N_TILES - 1), 1 - slot_i)

    def wait(lb, c):
        block_copy(i, slot_i, lb).wait()
        return c

    lax.fori_loop(0, nblk_ref[i], wait, 0)
    route = route_ref[...]
    yb = ybuf[slot_i]
    slot = lax.broadcasted_iota(jnp.int32, (TM, LOC), 1)
    pick = (jnp.where(slot == route[:, 4:5].astype(jnp.int32), route[:, 2:3], 0.0)
            + jnp.where(slot == route[:, 5:6].astype(jnp.int32), route[:, 3:4], 0.0)).astype(BF16)
    moe = jnp.dot(pick, yb, preferred_element_type=F32)
    m = mod_ref[0]
    res = _layer_norm(ALPHA * x1_ref[...] + m[5:6] * moe, g_ref[layer:layer + 1, :], b_ref[layer:layer + 1, :])

    @pl.when(i < CTX_TILES)
    def _():
        oc_ref[...] = res

    @pl.when(i >= CTX_TILES)
    def _():
        ol_ref[...] = res


def _combine(l, nblk, gsrc, y, route, x1, mod_l, g, b):
    out_shape = (jax.ShapeDtypeStruct((T_CTX, D), F32), jax.ShapeDtypeStruct((T_LAT, D), F32))
    out_specs = tuple(_pair_specs(D))
    grid_spec = pltpu.PrefetchScalarGridSpec(
        num_scalar_prefetch=2,
        grid=(N_TILES,),
        in_specs=[pl.BlockSpec(memory_space=pl.ANY),
                  pl.BlockSpec((TM, 128), lambda i, nb, gs: (i, 0)),
                  pl.BlockSpec((TM, D), lambda i, nb, gs: (i, 0)),
                  _mod_spec(l), _full((DEPTH, D)), _full((DEPTH, D))],
        out_specs=out_specs,
        scratch_shapes=[pltpu.VMEM((2, LOC, D), BF16), pltpu.SemaphoreType.DMA((2,))],
    )
    return pl.pallas_call(
        functools.partial(_combine_kernel, layer=l),
        out_shape=out_shape,
        grid_spec=grid_spec,
        compiler_params=_cparams("arbitrary"),
        name="moe_combine",
    )(nblk, gsrc, y, route, x1, mod_l, g, b)


def _prefix_pick(starts, query, table):
    delta = table - jnp.concatenate([jnp.zeros_like(table[..., :1]), table[..., :-1]], axis=-1)
    return jnp.sum(jnp.where(starts <= query, delta, 0), axis=-1)


def _dispatch_tables(cnt):
    i32 = jnp.int32
    run = (cnt + BLK - 1) // BLK * BLK
    loc_start = jnp.cumsum(run, axis=1) - run
    nblk_loc = jnp.sum(run, axis=1) // BLK
    seg_rows = jnp.sum(run, axis=0)
    seg_tiles = (seg_rows + TM - 1) // TM
    tile_end = jnp.cumsum(seg_tiles)
    seg_start = (tile_end - seg_tiles) * TM
    glob_start = seg_start[None, :] + jnp.cumsum(run, axis=0) - run

    g_blk = (glob_start.T.reshape(1, -1)) // BLK
    n_blk = (run.T.reshape(1, -1)) // BLK
    l_blk = ((jnp.arange(N_TILES, dtype=i32) * LOC)[None, :] + loc_start.T).reshape(1, -1) // BLK
    gb = jnp.arange(MOE_ROWS // BLK, dtype=i32)[:, None]
    off = gb[:, 0] - _prefix_pick(g_blk, gb, g_blk)
    src_blk = jnp.where(off < _prefix_pick(g_blk, gb, n_blk), _prefix_pick(g_blk, gb, l_blk) + off,
                        ZERO_BLK).astype(i32)

    pos = (jnp.arange(LOC_BLKS, dtype=i32) * BLK)[None, :, None]
    shift = _prefix_pick(loc_start[:, None, :], pos, (glob_start - loc_start)[:, None, :])
    gsrc = ((pos[:, :, 0] + shift) // BLK).astype(i32)
    expert_tables = ((tile_end - seg_tiles).astype(i32), seg_tiles.astype(i32), src_blk)
    return expert_tables, nblk_loc.astype(i32), gsrc.reshape(-1)


def _rope_slot(pe):
    return jnp.pad(pe, [(0, 0)] * (pe.ndim - 1) + [(NOPE, SLOT - NOPE - ROPE)])


def _wuq_layout(w):
    w = w.reshape(DEPTH, Q_LORA, HEADS, NOPE + ROPE)
    w = jnp.pad(w, ((0, 0), (0, 0), (0, 0), (0, SLOT - NOPE - ROPE)))
    return w.reshape(DEPTH, Q_LORA, HEADS * SLOT).astype(BF16)


def _wukv_layout(w):
    w = w.reshape(DEPTH, KV_LORA, HEADS, NOPE + VD)
    wk = jnp.pad(w[..., :NOPE], ((0, 0), (0, 0), (0, 0), (0, SLOT - NOPE)))
    wv = w[..., NOPE:].reshape(DEPTH, KV_LORA, HEADS * VD).astype(BF16)
    return wk.reshape(DEPTH, KV_LORA, HEADS * SLOT).astype(BF16), wv, jnp.swapaxes(wv, 1, 2)


def kernel(x_prompt, x_sample, c, cache_ckv, cache_kpe, c_ctx, ln_in_g, ln_in_b, w_mod, b_mod, w_in,
           gm_ln_g, gm_ln_b, gm_ws, gm_bs, hy_conv_w, hy_conv_b, hy_f_w1, hy_f_b1, hy_f_w2, hy_f_b2,
           hy_f_w3, hy_f_freq, hy_bias, mla_gq, mla_gkv, mla_wuq, mla_wukv, w_out, ln1_g, ln1_b,
           ln2_g, ln2_b, moe_w_gr, moe_b_gr, moe_w_er, moe_b_er, moe_w_gate, moe_w_up, moe_w_down):
    rope = _rope_tables()
    dft = {L: _dft_tables(L) for L in (CTX_L, LAT_L)}
    pos_tab = {L: _hyena_positions(L) for L in (CTX_L, LAT_L)}
    hd = np.arange(GM_W) // GM_HD
    avg = jnp.asarray((hd[:, None] == hd[None, :]) / GM_HD, BF16)
    tri = jnp.asarray(np.tril(np.ones((TM, TM)), -1), BF16)
    upper = jnp.asarray(np.triu(np.ones((128, 128)), 1), BF16)

    cond8 = jnp.concatenate([c_ctx[None], c, jnp.zeros((8 - 1 - N_LAT_B, D), F32)], axis=0)
    mod = _modulation(cond8, w_mod, b_mod).reshape(DEPTH, 8, 6, D)

    wuk_all, wuv_all, wuvt_all = _wukv_layout(mla_wukv)
    w_in_t = jnp.swapaxes(w_in, 1, 2)
    wuq_arr = _wuq_layout(mla_wuq)
    ws_b = gm_ws.astype(BF16)
    kc_all, vc_all = _cache_kv(cache_ckv, _rope_slot(cache_kpe), wuk_all, wuvt_all)

    x_pair = (x_prompt.reshape(T_CTX, D), x_sample.reshape(T_LAT, D))
    norm = (ln_in_g.reshape(1, D), ln_in_b.reshape(1, D))
    bs_full = jnp.repeat(jnp.swapaxes(gm_bs, 1, 2), GM_HD, axis=2)
    w_route = jnp.pad(jnp.concatenate([moe_w_gr, moe_w_er.reshape(DEPTH, D, N_EXP)], axis=2),
                      ((0, 0), (0, 0), (0, 128 - N_GROUPS - N_EXP)))
    b_route = jnp.pad(jnp.concatenate([moe_b_gr, moe_b_er.reshape(DEPTH, N_EXP)], axis=1),
                      ((0, 0), (0, 128 - N_GROUPS - N_EXP)))
    hyena_groups = ((CTX_L, N_CTX_B, 8, 0), (LAT_L, N_LAT_B, 2, T_CTX // LAT_L))
    spectra = {L: _filter_spectra(L, *pos_tab[L], dft[L], hy_f_w1, hy_f_b1, hy_f_w2, hy_f_b2, hy_f_w3,
                                  hy_f_freq) for L, _, _, _ in hyena_groups}
    ckv_states, kpe_states = [], []
    for l in range(DEPTH):
        a, hy, q, k, v, vt, ckv, kpe = _in_proj(
            l, x_pair, norm, mod, w_in_t, gm_ln_g, gm_ln_b, avg,
            ws_b, bs_full, mla_gq, mla_gkv, wuq_arr, wuk_all, wuv_all, wuvt_all, rope)
        ckv_states.append(ckv[:T_CTX].reshape(N_CTX_B, CTX_L, KV_LORA))
        kpe_states.append(kpe[:T_CTX, NOPE:NOPE + ROPE].reshape(N_CTX_B, CTX_L, ROPE))

        hyb = [_hyena_group(l, hy, L, nb, group, blk0, dft[L], spectra[L], hy_conv_w, hy_conv_b, hy_bias)
               for L, nb, group, blk0 in hyena_groups]

        att = _attention(l, q, k, v, vt, kc_all, vc_all)

        x1, xloc, route, counts = _out_proj(l, a, hyb, att, w_out, x_pair, norm, mod,
                                            ln1_g, ln1_b, w_route, b_route, tri, upper)
        cnt = counts.reshape(N_TILES, 8, 128)[:, 0, :N_EXP].astype(jnp.int32)
        expert_tables, nblk_loc, gsrc = _dispatch_tables(cnt)
        y = _experts(l, expert_tables, xloc, moe_w_gate, moe_w_up, moe_w_down)
        x_pair = _combine(l, nblk_loc, gsrc, y, route, x1, mod, ln2_g, ln2_b)

    y_prompt = x_pair[0].reshape(N_CTX_B, CTX_L, D)
    y_sample = x_pair[1].reshape(N_LAT_B, LAT_L, D)
    return (y_prompt, y_sample, jnp.stack(ckv_states, axis=1), jnp.stack(kpe_states, axis=1))
```
